```python
import jax, jax.numpy as jnp
from jax import lax
import numpy as np

D_MODEL = 1024
BATCH = 8
SEQ = 8192
DEPTH = 1

CTX_LEN = 256
GRID_W = 64
D_FF = 2816
N_MOD = 9
EPS = 1e-6
GDN_HEADS = D_MODEL // 128
GDN_HEAD_DIM = 128
GDN_WIDTH = GDN_HEADS * GDN_HEAD_DIM
CONV_K = 5
CHUNK = 64
POOL_WINDOWS = (2, 4, 8, 16)
POOL_GROUPS = len(POOL_WINDOWS)
POOL_WIDTH = D_MODEL // 2
POOL_GROUP_DIM = POOL_WIDTH // POOL_GROUPS
N_BRANCH = 2
QKV_END = 3 * GDN_WIDTH
AB_END = QKV_END + 4 * GDN_HEADS
GATE_END = AB_END + GDN_WIDTH
POOL_END = GATE_END + POOL_WIDTH
MIX_IN = POOL_END + N_BRANCH * D_MODEL

kernel_name = "hybrid_pool_gdn_macaron_dit"


def rmsnorm(x, w):
    xf = x.astype(jnp.float32)
    y = xf * lax.rsqrt(jnp.mean(xf * xf, axis=-1, keepdims=True) + EPS)
    return (y * w.astype(jnp.float32)).astype(x.dtype)


def modulate(n, shift, scale):
    return n * (1 + scale) + shift


def adaln(cvec, w, b):
    return jnp.split(jax.nn.silu(cvec) @ w + b, N_MOD, axis=-1)


def swiglu(h, w_in, w_out):
    g, u = jnp.split(h @ w_in, 2, axis=-1)
    return (jax.nn.silu(g) * u) @ w_out


def short_conv(x, w):
    pad = CONV_K // 2
    L = x.shape[1]
    xp = jnp.pad(x, ((0, 0), (pad, pad), (0, 0)))
    y = sum(xp[:, k:k + L] * w[k] for k in range(CONV_K))
    return jax.nn.silu(y)


def l2norm(x):
    return x * lax.rsqrt(jnp.sum(x * x, axis=-1, keepdims=True) + EPS)


def gdn_inputs(p, conv_w, a_log, dt_bias):
    B, L, _ = p.shape
    qkv = short_conv(p[..., :QKV_END], conv_w).astype(jnp.float32)
    q, k, v = [t.reshape(B, L, GDN_HEADS, GDN_HEAD_DIM) for t in jnp.split(qkv, 3, axis=-1)]
    q = l2norm(q) * GDN_HEAD_DIM ** -0.5
    k = l2norm(k)
    ab = p[..., QKV_END:AB_END].astype(jnp.float32).reshape(B, L, 4, GDN_HEADS)
    beta = jax.nn.sigmoid(ab[:, :, 0:2])
    g = -jnp.exp(a_log.astype(jnp.float32)) * jax.nn.softplus(ab[:, :, 2:4] + dt_bias.astype(jnp.float32))
    return q, k, v, beta, g


def _to_chunks(t, n):
    b, _, h = t.shape[:3]
    t = t.reshape((b, n, CHUNK, h) + t.shape[3:])
    return jnp.transpose(t, (1, 0, 3, 2) + tuple(range(4, t.ndim)))


def gated_delta(q, k, v, beta, g, S0):
    B, L, H, _ = q.shape
    dv = v.shape[-1]
    n = L // CHUNK
    qc, kc, vc = _to_chunks(q, n), _to_chunks(k, n), _to_chunks(v, n)
    bc, gc = _to_chunks(beta, n), _to_chunks(g, n)
    cum = jnp.cumsum(gc, axis=-1)
    idx = jnp.arange(CHUNK)
    incl = idx[:, None] >= idx[None, :]
    strict = idx[:, None] > idx[None, :]
    diff = cum[..., :, None] - cum[..., None, :]
    decay = jnp.where(incl, jnp.exp(jnp.where(incl, diff, 0.0)), 0.0)
    kb = kc * bc[..., None]
    vb = vc * bc[..., None]
    lmat = jnp.where(strict, jnp.einsum('nbhid,nbhjd->nbhij', kb, kc) * decay, 0.0)
    a_mat = lmat + jnp.eye(CHUNK, dtype=jnp.float32)
    rhs = jnp.concatenate([vb, kb * jnp.exp(cum)[..., None]], axis=-1)
    sol = lax.linalg.triangular_solve(a_mat, rhs, left_side=True, lower=True, unit_diagonal=True)
    u, w = sol[..., :dv], sol[..., dv:]
    aqk = jnp.einsum('nbhid,nbhjd->nbhij', qc, kc) * decay
    qd = qc * jnp.exp(cum)[..., None]
    kd = kc * jnp.exp(cum[..., -1:] - cum)[..., None]
    blast = jnp.exp(cum[..., -1])

    def step(S, xs):
        u_n, w_n, qd_n, kd_n, aqk_n, bl_n = xs
        v_new = u_n - jnp.einsum('bhcd,bhde->bhce', w_n, S)
        o = jnp.einsum('bhcd,bhde->bhce', qd_n, S) + jnp.einsum('bhij,bhje->bhie', aqk_n, v_new)
        S = S * bl_n[..., None, None] + jnp.einsum('bhcd,bhce->bhde', kd_n, v_new)
        return S, o

    S_fin, o = lax.scan(step, S0, (u, w, qd, kd, aqk, blast))
    o = jnp.transpose(o, (1, 0, 3, 2, 4)).reshape(B, L, H, dv)
    return o, S_fin


def bidir_gdn(lat, ctx):
    qx, kx, vx, bx, gx = lat
    qc, kc, vc, bcx, gcx = ctx
    B = qx.shape[0]
    S0 = jnp.zeros((B, GDN_HEADS, GDN_HEAD_DIM, GDN_HEAD_DIM), jnp.float32)
    fl = lambda t: jnp.flip(t, axis=1)
    oc_f, Sc_f = gated_delta(qc, kc, vc, bcx[:, :, 0], gcx[:, :, 0], S0)
    ox_f, _ = gated_delta(qx, kx, vx, bx[:, :, 0], gx[:, :, 0], Sc_f)
    oc_b, Sc_b = gated_delta(fl(qc), fl(kc), fl(vc), fl(bcx[:, :, 1]), fl(gcx[:, :, 1]), S0)
    ox_b, _ = gated_delta(fl(qx), fl(kx), fl(vx), fl(bx[:, :, 1]), fl(gx[:, :, 1]), Sc_b)
    return ox_f + fl(ox_b), oc_f + fl(oc_b)


def _bounds(n):
    t = jnp.arange(n)
    lo = jnp.stack([jnp.clip(t - w // 2, 0, n) for w in POOL_WINDOWS], axis=-1)
    hi = jnp.stack([jnp.clip(t + w - w // 2, 0, n) for w in POOL_WINDOWS], axis=-1)
    return lo, hi


def pool_grid(u):
    B, L, _ = u.shape
    R = L // GRID_W
    xg = u.astype(jnp.float32).reshape(B, R, GRID_W, POOL_GROUPS, POOL_GROUP_DIM)
    S = jnp.pad(jnp.cumsum(jnp.cumsum(xg, axis=1), axis=2), ((0, 0), (1, 0), (1, 0), (0, 0), (0, 0)))
    rlo, rhi = _bounds(R)
    clo, chi = _bounds(GRID_W)
    gi = jnp.arange(POOL_GROUPS)

    def corner(ri, ci):
        return S[:, ri[:, None, :], ci[None, :, :], gi[None, None, :], :]

    total = corner(rhi, chi) - corner(rlo, chi) - corner(rhi, clo) + corner(rlo, clo)
    area = ((rhi - rlo)[:, None, :] * (chi - clo)[None, :, :]).astype(jnp.float32)[..., None]
    return (total / area - xg).reshape(B, L, POOL_GROUPS, POOL_GROUP_DIM)


def pool_seq(u):
    B, L, _ = u.shape
    xg = u.astype(jnp.float32).reshape(B, L, POOL_GROUPS, POOL_GROUP_DIM)
    S = jnp.pad(jnp.cumsum(xg, axis=1), ((0, 0), (1, 0), (0, 0), (0, 0)))
    lo, hi = _bounds(L)
    gi = jnp.arange(POOL_GROUPS)[None, :]
    total = S[:, hi, gi, :] - S[:, lo, gi, :]
    count = (hi - lo).astype(jnp.float32)[..., None]
    return total / count - xg


def merge_branches(p, pool_diff, o, pool_w, pool_scale, gdn_norm_w, w_gdn_proj, w_pool_proj, w_mix_out):
    B, L, _ = p.shape
    dt = p.dtype
    gate = p[..., AB_END:GATE_END].reshape(B, L, GDN_HEADS, GDN_HEAD_DIM)
    o = rmsnorm(o.astype(dt), gdn_norm_w) * jax.nn.silu(gate)
    y_gdn = o.reshape(B, L, GDN_WIDTH) @ w_gdn_proj
    y_pool = jnp.einsum('blgc,gce->blge', pool_diff, pool_w.astype(jnp.float32)).reshape(B, L, POOL_WIDTH)
    y_pool = (y_pool * pool_scale.astype(jnp.float32)).astype(dt) @ w_pool_proj
    g_pool, g_gdn = jnp.split(jax.nn.sigmoid(p[..., POOL_END:]), N_BRANCH, axis=-1)
    return (g_pool * y_pool + g_gdn * y_gdn) @ w_mix_out


def _fwd_setup_inputs(seed: int = 0) -> dict:
    key = jax.random.key(seed)
    ks = jax.random.split(key, 24)
    f32 = jnp.float32

    def nrm(k, shape, fan_in):
        return jax.random.normal(k, shape, f32) * fan_in ** -0.5

    def gain(k, shape):
        return 1.0 + 0.02 * jax.random.normal(k, shape, f32)

    dt = jnp.exp(jax.random.uniform(ks[12], (DEPTH, 2, GDN_HEADS), f32, np.log(1e-3), np.log(1e-1)))
    return {
        "x": jax.random.normal(ks[0], (BATCH, SEQ, D_MODEL), f32),
        "c": jax.random.normal(ks[1], (BATCH, D_MODEL), f32),
        "ctx": jax.random.normal(ks[2], (BATCH, CTX_LEN, D_MODEL), f32),
        "c_ctx": jax.random.normal(ks[3], (D_MODEL,), f32),
        "w_ada": nrm(ks[4], (DEPTH, D_MODEL, N_MOD * D_MODEL), D_MODEL),
        "b_ada": 0.02 * jax.random.normal(ks[5], (DEPTH, N_MOD * D_MODEL), f32),
        "norm1_w": gain(ks[6], (DEPTH, D_MODEL)),
        "ffn1_w_in": nrm(ks[7], (DEPTH, D_MODEL, 2 * D_FF), D_MODEL),
        "ffn1_w_out": nrm(ks[8], (DEPTH, D_FF, D_MODEL), D_FF),
        "norm2_w": gain(ks[9], (DEPTH, D_MODEL)),
        "w_mix_in": nrm(ks[10], (DEPTH, D_MODEL, MIX_IN), D_MODEL),
        "conv_w": nrm(ks[11], (DEPTH, CONV_K, QKV_END), CONV_K),
        "a_log": jnp.log(jax.random.uniform(ks[13], (DEPTH, 2, GDN_HEADS), f32, 1.0, 16.0)),
        "dt_bias": dt + jnp.log(-jnp.expm1(-dt)),
        "gdn_norm_w": gain(ks[14], (DEPTH, GDN_HEAD_DIM)),
        "w_gdn_proj": nrm(ks[15], (DEPTH, GDN_WIDTH, D_MODEL), GDN_WIDTH),
        "pool_w": nrm(ks[16], (DEPTH, POOL_GROUPS, POOL_GROUP_DIM, POOL_GROUP_DIM), POOL_GROUP_DIM),
        "pool_scale": gain(ks[17], (DEPTH, POOL_WIDTH)),
        "w_pool_proj": nrm(ks[18], (DEPTH, POOL_WIDTH, D_MODEL), POOL_WIDTH),
        "w_mix_out": nrm(ks[19], (DEPTH, D_MODEL, D_MODEL), D_MODEL),
        "norm3_w": gain(ks[20], (DEPTH, D_MODEL)),
        "ffn2_w_in": nrm(ks[21], (DEPTH, D_MODEL, 2 * D_FF), D_MODEL),
        "ffn2_w_out": nrm(ks[22], (DEPTH, D_FF, D_MODEL), D_FF),
        "final_norm_w": gain(ks[23], (D_MODEL,)),
    }


def _fwd_reference(x, c, ctx, c_ctx, w_ada, b_ada, norm1_w, ffn1_w_in, ffn1_w_out, norm2_w, w_mix_in,
              conv_w, a_log, dt_bias, gdn_norm_w, w_gdn_proj, pool_w, pool_scale, w_pool_proj,
              w_mix_out, norm3_w, ffn2_w_in, ffn2_w_out, final_norm_w):
    for i in range(DEPTH):
        last = i == DEPTH - 1
        mx = [m[:, None, :] for m in adaln(c, w_ada[i], b_ada[i])]
        mc = adaln(c_ctx, w_ada[i], b_ada[i])

        x = x + 0.5 * mx[2] * swiglu(modulate(rmsnorm(x, norm1_w[i]), mx[0], mx[1]), ffn1_w_in[i], ffn1_w_out[i])
        ctx = ctx + 0.5 * mc[2] * swiglu(modulate(rmsnorm(ctx, norm1_w[i]), mc[0], mc[1]), ffn1_w_in[i], ffn1_w_out[i])

        ux = modulate(rmsnorm(x, norm2_w[i]), mx[3], mx[4])
        uc = modulate(rmsnorm(ctx, norm2_w[i]), mc[3], mc[4])
        px = ux @ w_mix_in[i]
        pc = uc @ (w_mix_in[i][:, :AB_END] if last else w_mix_in[i])
        lat_in = gdn_inputs(px[..., :AB_END], conv_w[i], a_log[i], dt_bias[i])
        ctx_in = gdn_inputs(pc[..., :AB_END], conv_w[i], a_log[i], dt_bias[i])
        ox, oc = bidir_gdn(lat_in, ctx_in)
        mix_x = merge_branches(px, pool_grid(px[..., GATE_END:POOL_END]), ox, pool_w[i], pool_scale[i],
                               gdn_norm_w[i], w_gdn_proj[i], w_pool_proj[i], w_mix_out[i])
        x = x + mx[5] * mix_x

        x = x + 0.5 * mx[8] * swiglu(modulate(rmsnorm(x, norm3_w[i]), mx[6], mx[7]), ffn2_w_in[i], ffn2_w_out[i])

        if not last:
            mix_c = merge_branches(pc, pool_seq(pc[..., GATE_END:POOL_END]), oc, pool_w[i], pool_scale[i],
                                   gdn_norm_w[i], w_gdn_proj[i], w_pool_proj[i], w_mix_out[i])
            ctx = ctx + mc[5] * mix_c
            ctx = ctx + 0.5 * mc[8] * swiglu(modulate(rmsnorm(ctx, norm3_w[i]), mc[6], mc[7]), ffn2_w_in[i], ffn2_w_out[i])
    return rmsnorm(x, final_norm_w)


import jax as _jax
import jax.numpy as _jnp

TWIN_FORMAT = 'train_step'
FWD_PARAMS = ['x', 'c', 'ctx', 'c_ctx', 'w_ada', 'b_ada', 'norm1_w', 'ffn1_w_in', 'ffn1_w_out', 'norm2_w', 'w_mix_in', 'conv_w', 'a_log', 'dt_bias', 'gdn_norm_w', 'w_gdn_proj', 'pool_w', 'pool_scale', 'w_pool_proj', 'w_mix_out', 'norm3_w', 'ffn2_w_in', 'ffn2_w_out', 'final_norm_w']
TWIN_WEIGHTS = ['c_ctx', 'w_ada', 'b_ada', 'norm1_w', 'ffn1_w_in', 'ffn1_w_out', 'norm2_w', 'w_mix_in', 'conv_w', 'a_log', 'dt_bias', 'gdn_norm_w', 'w_gdn_proj', 'pool_w', 'pool_scale', 'w_pool_proj', 'w_mix_out', 'norm3_w', 'ffn2_w_in', 'ffn2_w_out', 'final_norm_w']
TWIN_DIFF_INPUT = 'x'
TWIN_INPUTS = ['x', 'c', 'ctx', 'c_ctx', 'w_ada', 'b_ada', 'norm1_w', 'ffn1_w_in', 'ffn1_w_out', 'norm2_w', 'w_mix_in', 'conv_w', 'a_log', 'dt_bias', 'gdn_norm_w', 'w_gdn_proj', 'pool_w', 'pool_scale', 'w_pool_proj', 'w_mix_out', 'norm3_w', 'ffn2_w_in', 'ffn2_w_out', 'final_norm_w', 'loss_target', 'm_c_ctx', 'm_w_ada', 'm_b_ada', 'm_norm1_w', 'm_ffn1_w_in', 'm_ffn1_w_out', 'm_norm2_w', 'm_w_mix_in', 'm_conv_w', 'm_a_log', 'm_dt_bias', 'm_gdn_norm_w', 'm_w_gdn_proj', 'm_pool_w', 'm_pool_scale', 'm_w_pool_proj', 'm_w_mix_out', 'm_norm3_w', 'm_ffn2_w_in', 'm_ffn2_w_out', 'm_final_norm_w', 'v_c_ctx', 'v_w_ada', 'v_b_ada', 'v_norm1_w', 'v_ffn1_w_in', 'v_ffn1_w_out', 'v_norm2_w', 'v_w_mix_in', 'v_conv_w', 'v_a_log', 'v_dt_bias', 'v_gdn_norm_w', 'v_w_gdn_proj', 'v_pool_w', 'v_pool_scale', 'v_w_pool_proj', 'v_w_mix_out', 'v_norm3_w', 'v_ffn2_w_in', 'v_ffn2_w_out', 'v_final_norm_w']
TWIN_OUTPUTS = ['loss', 'grad_x', 'grad_c_ctx', 'grad_w_ada', 'grad_b_ada', 'grad_norm1_w', 'grad_ffn1_w_in', 'grad_ffn1_w_out', 'grad_norm2_w', 'grad_w_mix_in', 'grad_conv_w', 'grad_a_log', 'grad_dt_bias', 'grad_gdn_norm_w', 'grad_w_gdn_proj', 'grad_pool_w', 'grad_pool_scale', 'grad_w_pool_proj', 'grad_w_mix_out', 'grad_norm3_w', 'grad_ffn2_w_in', 'grad_ffn2_w_out', 'grad_final_norm_w', 'delta_c_ctx', 'delta_w_ada', 'delta_b_ada', 'delta_norm1_w', 'delta_ffn1_w_in', 'delta_ffn1_w_out', 'delta_norm2_w', 'delta_w_mix_in', 'delta_conv_w', 'delta_a_log', 'delta_dt_bias', 'delta_gdn_norm_w', 'delta_w_gdn_proj', 'delta_pool_w', 'delta_pool_scale', 'delta_w_pool_proj', 'delta_w_mix_out', 'delta_norm3_w', 'delta_ffn2_w_in', 'delta_ffn2_w_out', 'delta_final_norm_w', 'new_m_c_ctx', 'new_m_w_ada', 'new_m_b_ada', 'new_m_norm1_w', 'new_m_ffn1_w_in', 'new_m_ffn1_w_out', 'new_m_norm2_w', 'new_m_w_mix_in', 'new_m_conv_w', 'new_m_a_log', 'new_m_dt_bias', 'new_m_gdn_norm_w', 'new_m_w_gdn_proj', 'new_m_pool_w', 'new_m_pool_scale', 'new_m_w_pool_proj', 'new_m_w_mix_out', 'new_m_norm3_w', 'new_m_ffn2_w_in', 'new_m_ffn2_w_out', 'new_m_final_norm_w', 'new_v_c_ctx', 'new_v_w_ada', 'new_v_b_ada', 'new_v_norm1_w', 'new_v_ffn1_w_in', 'new_v_ffn1_w_out', 'new_v_norm2_w', 'new_v_w_mix_in', 'new_v_conv_w', 'new_v_a_log', 'new_v_dt_bias', 'new_v_gdn_norm_w', 'new_v_w_gdn_proj', 'new_v_pool_w', 'new_v_pool_scale', 'new_v_w_pool_proj', 'new_v_w_mix_out', 'new_v_norm3_w', 'new_v_ffn2_w_in', 'new_v_ffn2_w_out', 'new_v_final_norm_w']
TWIN_LEAF_KINDS = {'loss': 'loss', 'grad_x': 'grad_x', 'grad_c_ctx': 'grad_w', 'grad_w_ada': 'grad_w', 'grad_b_ada': 'grad_w', 'grad_norm1_w': 'grad_w', 'grad_ffn1_w_in': 'grad_w', 'grad_ffn1_w_out': 'grad_w', 'grad_norm2_w': 'grad_w', 'grad_w_mix_in': 'grad_w', 'grad_conv_w': 'grad_w', 'grad_a_log': 'grad_w', 'grad_dt_bias': 'grad_w', 'grad_gdn_norm_w': 'grad_w', 'grad_w_gdn_proj': 'grad_w', 'grad_pool_w': 'grad_w', 'grad_pool_scale': 'grad_w', 'grad_w_pool_proj': 'grad_w', 'grad_w_mix_out': 'grad_w', 'grad_norm3_w': 'grad_w', 'grad_ffn2_w_in': 'grad_w', 'grad_ffn2_w_out': 'grad_w', 'grad_final_norm_w': 'grad_w', 'delta_c_ctx': 'delta_w', 'delta_w_ada': 'delta_w', 'delta_b_ada': 'delta_w', 'delta_norm1_w': 'delta_w', 'delta_ffn1_w_in': 'delta_w', 'delta_ffn1_w_out': 'delta_w', 'delta_norm2_w': 'delta_w', 'delta_w_mix_in': 'delta_w', 'delta_conv_w': 'delta_w', 'delta_a_log': 'delta_w', 'delta_dt_bias': 'delta_w', 'delta_gdn_norm_w': 'delta_w', 'delta_w_gdn_proj': 'delta_w', 'delta_pool_w': 'delta_w', 'delta_pool_scale': 'delta_w', 'delta_w_pool_proj': 'delta_w', 'delta_w_mix_out': 'delta_w', 'delta_norm3_w': 'delta_w', 'delta_ffn2_w_in': 'delta_w', 'delta_ffn2_w_out': 'delta_w', 'delta_final_norm_w': 'delta_w', 'new_m_c_ctx': 'new_m', 'new_m_w_ada': 'new_m', 'new_m_b_ada': 'new_m', 'new_m_norm1_w': 'new_m', 'new_m_ffn1_w_in': 'new_m', 'new_m_ffn1_w_out': 'new_m', 'new_m_norm2_w': 'new_m', 'new_m_w_mix_in': 'new_m', 'new_m_conv_w': 'new_m', 'new_m_a_log': 'new_m', 'new_m_dt_bias': 'new_m', 'new_m_gdn_norm_w': 'new_m', 'new_m_w_gdn_proj': 'new_m', 'new_m_pool_w': 'new_m', 'new_m_pool_scale': 'new_m', 'new_m_w_pool_proj': 'new_m', 'new_m_w_mix_out': 'new_m', 'new_m_norm3_w': 'new_m', 'new_m_ffn2_w_in': 'new_m', 'new_m_ffn2_w_out': 'new_m', 'new_m_final_norm_w': 'new_m', 'new_v_c_ctx': 'new_v', 'new_v_w_ada': 'new_v', 'new_v_b_ada': 'new_v', 'new_v_norm1_w': 'new_v', 'new_v_ffn1_w_in': 'new_v', 'new_v_ffn1_w_out': 'new_v', 'new_v_norm2_w': 'new_v', 'new_v_w_mix_in': 'new_v', 'new_v_conv_w': 'new_v', 'new_v_a_log': 'new_v', 'new_v_dt_bias': 'new_v', 'new_v_gdn_norm_w': 'new_v', 'new_v_w_gdn_proj': 'new_v', 'new_v_pool_w': 'new_v', 'new_v_pool_scale': 'new_v', 'new_v_w_pool_proj': 'new_v', 'new_v_w_mix_out': 'new_v', 'new_v_norm3_w': 'new_v', 'new_v_ffn2_w_in': 'new_v', 'new_v_ffn2_w_out': 'new_v', 'new_v_final_norm_w': 'new_v'}


def _forward(args):
    return _fwd_reference(*[args[k] for k in FWD_PARAMS])


def _output_shape():
    out = _jax.eval_shape(lambda: _forward(_fwd_setup_inputs(0)))
    return out.shape, out.dtype

N_MICROBATCH = 1
ADAM_LR = 0.001
ADAM_B1 = 0.9
ADAM_B2 = 0.999
ADAM_EPS = 1e-08
ADAM_WD = 0.01
ADAM_STEP = 10
PER_EXAMPLE_BATCH_AXIS = {'x': 0, 'c': 0, 'ctx': 0, 'loss_target': 0}
SHARED_INPUTS = []
_WEIGHT_DTYPES = {'c_ctx': _jnp.float32, 'w_ada': _jnp.float32, 'b_ada': _jnp.float32, 'norm1_w': _jnp.float32, 'ffn1_w_in': _jnp.float32, 'ffn1_w_out': _jnp.float32, 'norm2_w': _jnp.float32, 'w_mix_in': _jnp.float32, 'conv_w': _jnp.float32, 'a_log': _jnp.float32, 'dt_bias': _jnp.float32, 'gdn_norm_w': _jnp.float32, 'w_gdn_proj': _jnp.float32, 'pool_w': _jnp.float32, 'pool_scale': _jnp.float32, 'w_pool_proj': _jnp.float32, 'w_mix_out': _jnp.float32, 'norm3_w': _jnp.float32, 'ffn2_w_in': _jnp.float32, 'ffn2_w_out': _jnp.float32, 'final_norm_w': _jnp.float32}
MOMENT_SCALE = {'c_ctx': 6.649125e-03, 'w_ada': 7.632755e-02, 'b_ada': 1.277326e-01, 'norm1_w': 1.079578e-01, 'ffn1_w_in': 5.247058e-02, 'ffn1_w_out': 8.531988e-02, 'norm2_w': 1.277810e-01, 'w_mix_in': 5.515046e-02, 'conv_w': 4.260145e-02, 'a_log': 1.412907e-01, 'dt_bias': 1.379246e-01, 'gdn_norm_w': 1.867579e-01, 'w_gdn_proj': 6.312657e-02, 'pool_w': 1.237981e-01, 'pool_scale': 1.217314e-01, 'w_pool_proj': 8.795823e-02, 'w_mix_out': 1.078948e-01, 'norm3_w': 9.643761e-02, 'ffn2_w_in': 4.368125e-02, 'ffn2_w_out': 7.085196e-02, 'final_norm_w': 6.433552e+01}


def _to_microbatches(a, axis):
    t = _jnp.moveaxis(a, axis, 0)
    t = t.reshape((N_MICROBATCH, t.shape[0] // N_MICROBATCH) + t.shape[1:])
    return _jnp.moveaxis(t, 1, axis + 1)


def setup_inputs(seed: int = 0) -> dict:
    inp = _fwd_setup_inputs(seed)
    key = _jax.random.fold_in(_jax.random.key(seed), 7919)
    shape, _ = _output_shape()
    out = dict(inp)
    out["loss_target"] = _jax.random.normal(_jax.random.fold_in(key, 0), shape, _jnp.float32)
    for i, name in enumerate(TWIN_WEIGHTS):
        w = inp[name].astype(_jnp.float32)
        if MOMENT_SCALE is None:
            s = _jnp.sqrt(_jnp.mean(_jnp.square(w)) + 1e-30)
        else:
            s = MOMENT_SCALE[name]
        km, kv = _jax.random.split(_jax.random.fold_in(key, i + 1))
        out[name] = w
        out["m_" + name] = s * _jax.random.normal(km, w.shape, _jnp.float32)
        out["v_" + name] = (s * s) * _jax.random.uniform(kv, w.shape, _jnp.float32, 0.5, 1.5)
    if N_MICROBATCH > 1:
        for name, axis in PER_EXAMPLE_BATCH_AXIS.items():
            out[name] = _to_microbatches(out[name], axis)
    return {'x': out['x'], 'c': out['c'], 'ctx': out['ctx'], 'c_ctx': out['c_ctx'], 'w_ada': out['w_ada'], 'b_ada': out['b_ada'], 'norm1_w': out['norm1_w'], 'ffn1_w_in': out['ffn1_w_in'], 'ffn1_w_out': out['ffn1_w_out'], 'norm2_w': out['norm2_w'], 'w_mix_in': out['w_mix_in'], 'conv_w': out['conv_w'], 'a_log': out['a_log'], 'dt_bias': out['dt_bias'], 'gdn_norm_w': out['gdn_norm_w'], 'w_gdn_proj': out['w_gdn_proj'], 'pool_w': out['pool_w'], 'pool_scale': out['pool_scale'], 'w_pool_proj': out['w_pool_proj'], 'w_mix_out': out['w_mix_out'], 'norm3_w': out['norm3_w'], 'ffn2_w_in': out['ffn2_w_in'], 'ffn2_w_out': out['ffn2_w_out'], 'final_norm_w': out['final_norm_w'], 'loss_target': out['loss_target'], 'm_c_ctx': out['m_c_ctx'], 'm_w_ada': out['m_w_ada'], 'm_b_ada': out['m_b_ada'], 'm_norm1_w': out['m_norm1_w'], 'm_ffn1_w_in': out['m_ffn1_w_in'], 'm_ffn1_w_out': out['m_ffn1_w_out'], 'm_norm2_w': out['m_norm2_w'], 'm_w_mix_in': out['m_w_mix_in'], 'm_conv_w': out['m_conv_w'], 'm_a_log': out['m_a_log'], 'm_dt_bias': out['m_dt_bias'], 'm_gdn_norm_w': out['m_gdn_norm_w'], 'm_w_gdn_proj': out['m_w_gdn_proj'], 'm_pool_w': out['m_pool_w'], 'm_pool_scale': out['m_pool_scale'], 'm_w_pool_proj': out['m_w_pool_proj'], 'm_w_mix_out': out['m_w_mix_out'], 'm_norm3_w': out['m_norm3_w'], 'm_ffn2_w_in': out['m_ffn2_w_in'], 'm_ffn2_w_out': out['m_ffn2_w_out'], 'm_final_norm_w': out['m_final_norm_w'], 'v_c_ctx': out['v_c_ctx'], 'v_w_ada': out['v_w_ada'], 'v_b_ada': out['v_b_ada'], 'v_norm1_w': out['v_norm1_w'], 'v_ffn1_w_in': out['v_ffn1_w_in'], 'v_ffn1_w_out': out['v_ffn1_w_out'], 'v_norm2_w': out['v_norm2_w'], 'v_w_mix_in': out['v_w_mix_in'], 'v_conv_w': out['v_conv_w'], 'v_a_log': out['v_a_log'], 'v_dt_bias': out['v_dt_bias'], 'v_gdn_norm_w': out['v_gdn_norm_w'], 'v_w_gdn_proj': out['v_w_gdn_proj'], 'v_pool_w': out['v_pool_w'], 'v_pool_scale': out['v_pool_scale'], 'v_w_pool_proj': out['v_w_pool_proj'], 'v_w_mix_out': out['v_w_mix_out'], 'v_norm3_w': out['v_norm3_w'], 'v_ffn2_w_in': out['v_ffn2_w_in'], 'v_ffn2_w_out': out['v_ffn2_w_out'], 'v_final_norm_w': out['v_final_norm_w']}


def _loss(weights, diff, rest, loss_target):
    with _jax.named_scope("forward"):
        args = {**rest, TWIN_DIFF_INPUT: diff, **{k: w.astype(_WEIGHT_DTYPES[k]) for k, w in weights.items()}}
        y = _forward(args)
    with _jax.named_scope("loss_head"):
        err = _jnp.square(y.astype(_jnp.float32) - loss_target)
        return 0.5 * _jnp.sum(_jnp.mean(err, axis=-1)) if err.ndim else 0.5 * err


def _adamw(w, g, m, v):
    m = ADAM_B1 * m + (1.0 - ADAM_B1) * g
    v = ADAM_B2 * v + (1.0 - ADAM_B2) * _jnp.square(g)
    m_hat = m / (1.0 - ADAM_B1 ** ADAM_STEP)
    v_hat = v / (1.0 - ADAM_B2 ** ADAM_STEP)
    delta = -ADAM_LR * (m_hat / (_jnp.sqrt(v_hat) + ADAM_EPS) + ADAM_WD * w)
    return delta, m, v


def reference(x, c, ctx, c_ctx, w_ada, b_ada, norm1_w, ffn1_w_in, ffn1_w_out, norm2_w, w_mix_in, conv_w, a_log, dt_bias, gdn_norm_w, w_gdn_proj, pool_w, pool_scale, w_pool_proj, w_mix_out, norm3_w, ffn2_w_in, ffn2_w_out, final_norm_w, loss_target, m_c_ctx, m_w_ada, m_b_ada, m_norm1_w, m_ffn1_w_in, m_ffn1_w_out, m_norm2_w, m_w_mix_in, m_conv_w, m_a_log, m_dt_bias, m_gdn_norm_w, m_w_gdn_proj, m_pool_w, m_pool_scale, m_w_pool_proj, m_w_mix_out, m_norm3_w, m_ffn2_w_in, m_ffn2_w_out, m_final_norm_w, v_c_ctx, v_w_ada, v_b_ada, v_norm1_w, v_ffn1_w_in, v_ffn1_w_out, v_norm2_w, v_w_mix_in, v_conv_w, v_a_log, v_dt_bias, v_gdn_norm_w, v_w_gdn_proj, v_pool_w, v_pool_scale, v_w_pool_proj, v_w_mix_out, v_norm3_w, v_ffn2_w_in, v_ffn2_w_out, v_final_norm_w):
    given = dict(x=x, c=c, ctx=ctx, c_ctx=c_ctx, w_ada=w_ada, b_ada=b_ada, norm1_w=norm1_w, ffn1_w_in=ffn1_w_in, ffn1_w_out=ffn1_w_out, norm2_w=norm2_w, w_mix_in=w_mix_in, conv_w=conv_w, a_log=a_log, dt_bias=dt_bias, gdn_norm_w=gdn_norm_w, w_gdn_proj=w_gdn_proj, pool_w=pool_w, pool_scale=pool_scale, w_pool_proj=w_pool_proj, w_mix_out=w_mix_out, norm3_w=norm3_w, ffn2_w_in=ffn2_w_in, ffn2_w_out=ffn2_w_out, final_norm_w=final_norm_w, loss_target=loss_target, m_c_ctx=m_c_ctx, m_w_ada=m_w_ada, m_b_ada=m_b_ada, m_norm1_w=m_norm1_w, m_ffn1_w_in=m_ffn1_w_in, m_ffn1_w_out=m_ffn1_w_out, m_norm2_w=m_norm2_w, m_w_mix_in=m_w_mix_in, m_conv_w=m_conv_w, m_a_log=m_a_log, m_dt_bias=m_dt_bias, m_gdn_norm_w=m_gdn_norm_w, m_w_gdn_proj=m_w_gdn_proj, m_pool_w=m_pool_w, m_pool_scale=m_pool_scale, m_w_pool_proj=m_w_pool_proj, m_w_mix_out=m_w_mix_out, m_norm3_w=m_norm3_w, m_ffn2_w_in=m_ffn2_w_in, m_ffn2_w_out=m_ffn2_w_out, m_final_norm_w=m_final_norm_w, v_c_ctx=v_c_ctx, v_w_ada=v_w_ada, v_b_ada=v_b_ada, v_norm1_w=v_norm1_w, v_ffn1_w_in=v_ffn1_w_in, v_ffn1_w_out=v_ffn1_w_out, v_norm2_w=v_norm2_w, v_w_mix_in=v_w_mix_in, v_conv_w=v_conv_w, v_a_log=v_a_log, v_dt_bias=v_dt_bias, v_gdn_norm_w=v_gdn_norm_w, v_w_gdn_proj=v_w_gdn_proj, v_pool_w=v_pool_w, v_pool_scale=v_pool_scale, v_w_pool_proj=v_w_pool_proj, v_w_mix_out=v_w_mix_out, v_norm3_w=v_norm3_w, v_ffn2_w_in=v_ffn2_w_in, v_ffn2_w_out=v_ffn2_w_out, v_final_norm_w=v_final_norm_w)
    weights = {n: given[n] for n in TWIN_WEIGHTS}
    shared = {n: given[n] for n in SHARED_INPUTS}
    per_example = {n: given[n] for n in ['x', 'c', 'ctx']}
    grad_fn = _jax.value_and_grad(_loss, argnums=(0, 1))

    def one_microbatch(ex, loss_target):
        ex = dict(ex)
        diff = ex.pop(TWIN_DIFF_INPUT)
        return grad_fn(weights, diff, {**shared, **ex}, loss_target)

    if N_MICROBATCH == 1:
        loss, (grad_w, grad_x) = one_microbatch(per_example, given["loss_target"])
    else:
        def body(carry, xs):
            loss_sum, grad_sum = carry
            l_k, (gw_k, gx_k) = one_microbatch(xs[0], xs[1])
            with _jax.named_scope("update"):
                return (loss_sum + l_k, _jax.tree.map(_jnp.add, grad_sum, gw_k)), gx_k

        init = (_jnp.zeros((), _jnp.float32), _jax.tree.map(_jnp.zeros_like, weights))
        (loss, grad_w), grad_x = _jax.lax.scan(body, init, (per_example, given["loss_target"]))
    with _jax.named_scope("update"):
        delta_w, new_m, new_v = {}, {}, {}
        for n in TWIN_WEIGHTS:
            delta_w[n], new_m[n], new_v[n] = _adamw(weights[n], grad_w[n], given["m_" + n], given["v_" + n])
    return (loss, grad_x, *[grad_w[n] for n in TWIN_WEIGHTS], *[delta_w[n] for n in TWIN_WEIGHTS],
            *[new_m[n] for n in TWIN_WEIGHTS], *[new_v[n] for n in TWIN_WEIGHTS])
```

```python
import functools

import jax
import jax.numpy as jnp
from jax import lax
from jax.experimental import pallas as pl
from jax.experimental.pallas import tpu as pltpu

F32 = jnp.float32
BF = jnp.bfloat16

D = 1024
FF = 2816
NH = 8
HD = 128
CH = 64
GW = 64
TM = 256
NQKV = 3 * NH * HD
NPOOL = 512
POOL_WINDOWS = (2, 4, 8, 16)
NMIX = 6688
NMIXP = 6784
EPS = 1e-6
NCHIP = 4
VMEM_LIMIT = 56 * 1024 * 1024

ADAM_LR, ADAM_B1, ADAM_B2, ADAM_EPS, ADAM_WD, ADAM_STEP = 0.001, 0.9, 0.999, 1e-08, 0.01, 10


def _cparams(*sem):
    return pltpu.CompilerParams(dimension_semantics=sem, vmem_limit_bytes=VMEM_LIMIT)


def _const_spec(shape):
    nd = len(shape)
    return pl.BlockSpec(shape, lambda *_: (0,) * nd, pipeline_mode=pl.Buffered(1))


def _dot(a, b, dims):
    return lax.dot_general(a.astype(BF), b.astype(BF), (dims, ((), ())), preferred_element_type=F32)


def _nn(a, b):
    return _dot(a, b, ((1,), (0,)))


def _nt(a, b):
    return _dot(a, b, ((1,), (1,)))


def _tn(a, b):
    return _dot(a, b, ((0,), (0,)))


def _silu(x):
    return x * jax.nn.sigmoid(x)


def _dsilu(x):
    s = jax.nn.sigmoid(x)
    return s * (1.0 + x * (1.0 - s))


def _norm_mod(x, nw, shift, scale):
    r = lax.rsqrt(jnp.mean(x * x, axis=-1, keepdims=True) + EPS)
    xh = x * r
    n = xh * nw
    return n * (1.0 + scale) + shift, n, xh, r


def _norm_mod_bwd(dh, n, xh, r, nw, scale):
    dn = dh * (1.0 + scale)
    dxh = dn * nw
    dx = r * (dxh - xh * jnp.mean(dxh * xh, axis=-1, keepdims=True))
    rs = lambda t: jnp.sum(t, axis=0, keepdims=True)
    return dx, rs(dh), rs(dh * n), rs(dn * xh)


def _ffn_fwd(x, modv, nw, w_in4, w_out, *, mrow, name):
    n_tok = x.shape[0]
    nt = n_tok // TM
    nset = modv.shape[0]
    ws = w_in4.shape[2]

    def body(x_ref, mod_ref, nw_ref, win_ref, wout_ref, x1_ref, h_ref, gu_ref, f_ref):
        xv = x_ref[...]
        shift, scale, gate = mod_ref[0, mrow:mrow + 1, :], mod_ref[0, mrow + 1:mrow + 2, :], mod_ref[0, mrow + 2:mrow + 3, :]
        h, _, _, _ = _norm_mod(xv, nw_ref[...], shift, scale)
        hb = h.astype(BF)
        h_ref[...] = hb
        gus = [_nn(hb, win_ref[s]) for s in range(NCHIP)]
        for s in range(NCHIP):
            gu_ref[:, s * ws:(s + 1) * ws] = gus[s].astype(BF)
        g = jnp.concatenate(gus[:2], axis=1)
        u = jnp.concatenate(gus[2:], axis=1)
        f = _nn(_silu(g) * u, wout_ref[...])
        f_ref[...] = f.astype(BF)
        x1_ref[...] = xv + 0.5 * gate * f

    tile = lambda w: pl.BlockSpec((TM, w), lambda i: (i, 0))
    return pl.pallas_call(
        body, name=name, grid=(nt,),
        in_specs=[tile(D), pl.BlockSpec((1, 16, D), lambda i: (jnp.minimum(i, nset - 1), 0, 0)), _const_spec((1, D)),
                  _const_spec(w_in4.shape), _const_spec(w_out.shape)],
        out_specs=[tile(D), tile(D), tile(2 * FF), tile(D)],
        out_shape=[jax.ShapeDtypeStruct((n_tok, D), F32), jax.ShapeDtypeStruct((n_tok, D), BF),
                   jax.ShapeDtypeStruct((n_tok, 2 * FF), BF), jax.ShapeDtypeStruct((n_tok, D), BF)],
        compiler_params=_cparams("parallel"),
    )(x, modv, nw, w_in4, w_out)


def _ffn_bwd(dxo, x, gu, fo, modv, nw, w_in4, w_out, *, mrow, dxo_off, name):
    n_tok = x.shape[0]
    nt = n_tok // TM
    nset = modv.shape[0]
    ws = w_in4.shape[2]

    def body(dxo_ref, x_ref, gu_ref, f_ref, mod_ref, nw_ref, win_ref, wout_ref, dx_ref, a_ref, df_ref, dgu_ref, acc_ref):
        i = pl.program_id(0)
        xv = x_ref[...]
        dxo_v = dxo_ref[...]
        if dxo_off:
            dxo_v = jnp.where(i >= dxo_off, dxo_v, 0.0)
        shift, scale, gate = mod_ref[0, mrow:mrow + 1, :], mod_ref[0, mrow + 1:mrow + 2, :], mod_ref[0, mrow + 2:mrow + 3, :]
        _, n, xh, r = _norm_mod(xv, nw_ref[...], shift, scale)
        df = 0.5 * gate * dxo_v
        dfb = df.astype(BF)
        df_ref[...] = dfb
        dgate = jnp.sum(0.5 * dxo_v * f_ref[...].astype(F32), axis=0, keepdims=True)
        da = _nt(dfb, wout_ref[...])
        g = gu_ref[:, :FF].astype(F32)
        u = gu_ref[:, FF:].astype(F32)
        sg = _silu(g)
        a_ref[...] = (sg * u).astype(BF)
        dgu_ref[:, :FF] = (da * u * _dsilu(g)).astype(BF)
        dgu_ref[:, FF:] = (da * sg).astype(BF)
        dh = _nt(dgu_ref[:, 0:ws], win_ref[0])
        for s in range(1, NCHIP):
            dh = dh + _nt(dgu_ref[:, s * ws:(s + 1) * ws], win_ref[s])
        dx, dshift, dscale, dnw = _norm_mod_bwd(dh, n, xh, r, nw_ref[...], scale)
        dx_ref[...] = dxo_v + dx

        @pl.when((i == 0) | (i == nset - 1))
        def _():
            acc_ref[...] = jnp.zeros_like(acc_ref)

        acc_ref[0, 0:1, :] += dshift
        acc_ref[0, 1:2, :] += dscale
        acc_ref[0, 2:3, :] += dgate
        acc_ref[0, 3:4, :] += dnw

    tile = lambda w: pl.BlockSpec((TM, w), lambda i: (i, 0))
    return pl.pallas_call(
        body, name=name, grid=(nt,),
        in_specs=[pl.BlockSpec((TM, D), lambda i: (jnp.maximum(i - dxo_off, 0), 0)), tile(D), tile(2 * FF), tile(D),
                  pl.BlockSpec((1, 16, D), lambda i: (jnp.minimum(i, nset - 1), 0, 0)), _const_spec((1, D)),
                  _const_spec(w_in4.shape), _const_spec(w_out.shape)],
        out_specs=[tile(D), tile(FF), tile(D), tile(2 * FF),
                   pl.BlockSpec((1, 8, D), lambda i: (jnp.minimum(i, nset - 1), 0, 0))],
        out_shape=[jax.ShapeDtypeStruct((n_tok, D), F32), jax.ShapeDtypeStruct((n_tok, FF), BF),
                   jax.ShapeDtypeStruct((n_tok, D), BF), jax.ShapeDtypeStruct((n_tok, 2 * FF), BF),
                   jax.ShapeDtypeStruct((nset, 8, D), F32)],
        compiler_params=_cparams("arbitrary"),
    )(dxo, x, gu, fo, modv, nw, w_in4, w_out)


def _matmul_tn(a, b, *, tmm, tn, tk, nsplit=1, name):
    n_tok, m = a.shape
    kk = b.shape[1]
    nk = n_tok // tk

    def body(a_ref, b_ref, o_ref, acc):
        k = pl.program_id(2)

        @pl.when(k == 0)
        def _():
            acc[...] = jnp.zeros_like(acc)

        acc[...] += _tn(a_ref[...], b_ref[...])

        @pl.when(k == nk - 1)
        def _():
            o_ref[...] = acc[...].astype(BF).reshape(o_ref.shape)

    if nsplit == 1:
        out_shape = jax.ShapeDtypeStruct((m, kk), BF)
        out_spec = pl.BlockSpec((tmm, tn), lambda i, j, k: (i, j))
    else:
        assert tn == kk // nsplit
        out_shape = jax.ShapeDtypeStruct((nsplit, m, tn), BF)
        out_spec = pl.BlockSpec((1, tmm, tn), lambda i, j, k: (j, i, 0))
    return pl.pallas_call(
        body, name=name, grid=(m // tmm, kk // tn, nk),
        in_specs=[pl.BlockSpec((tk, tmm), lambda i, j, k: (k, i)), pl.BlockSpec((tk, tn), lambda i, j, k: (k, j))],
        out_specs=out_spec, out_shape=out_shape,
        scratch_shapes=[pltpu.VMEM((tmm, tn), F32)],
        compiler_params=_cparams("parallel", "parallel", "arbitrary"),
    )(a, b)


_MIX_PARTS = (("qkv", 0, NQKV), ("gate", NQKV, 1024), ("pool", NQKV + 1024, NPOOL), ("br", NQKV + 1024 + NPOOL, 2048),
              ("ab", NMIXP - 128, 128))


def _mix_in_fwd(x1, modv, nw, w_mix, *, name):
    n_tok = x1.shape[0]

    def body(x_ref, mod_ref, nw_ref, w_ref, u_ref, *p_refs):
        u, _, _, _ = _norm_mod(x_ref[...], nw_ref[...], mod_ref[0, 3:4, :], mod_ref[0, 4:5, :])
        ub = u.astype(BF)
        u_ref[...] = ub
        for (_, c0, w), p_ref in zip(_MIX_PARTS, p_refs):
            p_ref[...] = _nn(ub, w_ref[:, c0:c0 + w])

    tile = lambda w: pl.BlockSpec((TM, w), lambda i: (i, 0))
    ctile = lambda w: pl.BlockSpec((TM, w), lambda i: (i + 1, 0))
    return pl.pallas_call(
        body, name=name, grid=(n_tok // TM,),
        in_specs=[tile(D), pl.BlockSpec((1, 16, D), lambda i: (jnp.minimum(i, 1), 0, 0)), _const_spec((1, D)),
                  _const_spec(w_mix.shape)],
        out_specs=[tile(D)] + [tile(w) for _, _, w in _MIX_PARTS],
        out_shape=[jax.ShapeDtypeStruct((n_tok, D), BF)] + [jax.ShapeDtypeStruct((n_tok, w), F32) for _, _, w in _MIX_PARTS],
        compiler_params=_cparams("parallel"),
    )(x1, modv, nw, w_mix)


def _mix_in_bwd(dxo, x1, dqkv, dgate, dpool, dbr, dab, modv, nw, w_mix, *, name):
    n_tok = x1.shape[0]

    def body(dxo_ref, x_ref, dqkv_ref, dgate_ref, dpool_ref, dbr_ref, dab_ref, mod_ref, nw_ref, w_ref,
             dx_ref, dp_ref, acc_ref):
        i = pl.program_id(0)
        lat = i >= 1
        scale = mod_ref[0, 4:5, :]
        _, n, xh, r = _norm_mod(x_ref[...], nw_ref[...], mod_ref[0, 3:4, :], scale)
        dp_ref[:, 0:NQKV] = dqkv_ref[...].astype(BF)
        dp_ref[:, NQKV:NQKV + 1024] = jnp.where(lat, dgate_ref[...], 0.0).astype(BF)
        dp_ref[:, NQKV + 1024:NQKV + 1536] = jnp.where(lat, dpool_ref[...], 0.0).astype(BF)
        dp_ref[:, NQKV + 1536:NMIXP - 128] = jnp.where(lat, dbr_ref[...], 0.0).astype(BF)
        dp_ref[:, NMIXP - 128:] = dab_ref[...].astype(BF)
        du = _nt(dp_ref[...], w_ref[...])
        dx, dshift, dscale, dnw = _norm_mod_bwd(du, n, xh, r, nw_ref[...], scale)
        dx_ref[...] = jnp.where(lat, dxo_ref[...], 0.0) + dx

        @pl.when(i <= 1)
        def _():
            acc_ref[...] = jnp.zeros_like(acc_ref)

        acc_ref[0, 0:1, :] += dshift
        acc_ref[0, 1:2, :] += dscale
        acc_ref[0, 3:4, :] += dnw

    tile = lambda w: pl.BlockSpec((TM, w), lambda i: (i, 0))
    ltile = lambda w: pl.BlockSpec((TM, w), lambda i: (jnp.maximum(i - 1, 0), 0))
    return pl.pallas_call(
        body, name=name, grid=(n_tok // TM,),
        in_specs=[ltile(D), tile(D), tile(NQKV), ltile(1024), ltile(NPOOL), ltile(2048), tile(128),
                  pl.BlockSpec((1, 16, D), lambda i: (jnp.minimum(i, 1), 0, 0)), _const_spec((1, D)), _const_spec(w_mix.shape)],
        out_specs=[tile(D), tile(NMIXP), pl.BlockSpec((1, 8, D), lambda i: (jnp.minimum(i, 1), 0, 0))],
        out_shape=[jax.ShapeDtypeStruct((n_tok, D), F32), jax.ShapeDtypeStruct((n_tok, NMIXP), BF),
                   jax.ShapeDtypeStruct((2, 8, D), F32)],
        compiler_params=_cparams("arbitrary"),
    )(dxo, x1, dqkv, dgate, dpool, dbr, dab, modv, nw, w_mix)


def _qkv_act(pre, j):
    s = _silu(pre)
    nrm = s * lax.rsqrt(jnp.sum(s * s, axis=-1, keepdims=True) + EPS)
    nrm = nrm * jnp.where(j < NH, HD ** -0.5, 1.0)
    return jnp.where(j < 2 * NH, nrm, s)


def _halo_specs(nt, width_blocks):
    r = TM // 8
    main = pl.BlockSpec((TM, 128), lambda j, i: (i, j))
    prev = pl.BlockSpec((8, 128), lambda j, i: (jnp.maximum(i * r - 1, 0), j))
    nxt = pl.BlockSpec((8, 128), lambda j, i: (jnp.minimum((i + 1) * r, nt * r - 1), j))
    return main, prev, nxt


def _prep_fwd(p_qkv, conv_w8, *, name):
    n_tok = p_qkv.shape[0]
    nt = n_tok // TM

    def body(x_ref, xp_ref, xn_ref, w_ref, o_ref, win):
        j, i = pl.program_id(0), pl.program_id(1)
        has_prev = (i != 0) & (i != 1)
        has_next = (i != 0) & (i != nt - 1)
        win[0:8, :] = jnp.where(has_prev, xp_ref[...], 0.0)
        win[8:8 + TM, :] = x_ref[...]
        win[8 + TM:, :] = jnp.where(has_next, xn_ref[...], 0.0)
        pre = win[6:6 + TM, :] * w_ref[0:1, :]
        for k in range(1, 5):
            pre = pre + win[6 + k:6 + k + TM, :] * w_ref[k:k + 1, :]
        o_ref[...] = _qkv_act(pre, j)

    main, prev, nxt = _halo_specs(nt, NQKV // 128)
    return pl.pallas_call(
        body, name=name, grid=(NQKV // 128, nt),
        in_specs=[main, prev, nxt, pl.BlockSpec((8, 128), lambda j, i: (0, j))],
        out_specs=main, out_shape=jax.ShapeDtypeStruct((n_tok, NQKV), F32),
        scratch_shapes=[pltpu.VMEM((TM + 16, 128), F32)],
        compiler_params=_cparams("parallel", "arbitrary"),
    )(p_qkv, p_qkv, p_qkv, conv_w8)


def _prep_bwd(p_qkv, dqkv, conv_w8, *, name):
    n_tok = p_qkv.shape[0]
    nt = n_tok // TM
    wr = TM + 16

    def body(x_ref, xp_ref, xn_ref, g_ref, gp_ref, gn_ref, w_ref, dx_ref, dw_ref, xwin, dwin):
        j, i = pl.program_id(0), pl.program_id(1)
        has_prev = (i != 0) & (i != 1)
        has_next = (i != 0) & (i != nt - 1)
        z8 = jnp.zeros((8, 128), F32)
        xwin[0:8, :] = z8
        xwin[8:16, :] = jnp.where(has_prev, xp_ref[...], 0.0)
        xwin[16:16 + TM, :] = x_ref[...]
        xwin[16 + TM:24 + TM, :] = jnp.where(has_next, xn_ref[...], 0.0)
        xwin[24 + TM:, :] = z8
        pre = xwin[6:6 + wr, :] * w_ref[0:1, :]
        for k in range(1, 5):
            pre = pre + xwin[6 + k:6 + k + wr, :] * w_ref[k:k + 1, :]
        gwin = jnp.concatenate([jnp.where(has_prev, gp_ref[...], 0.0), g_ref[...], jnp.where(has_next, gn_ref[...], 0.0)], axis=0)
        _, vjp = jax.vjp(lambda t: _qkv_act(t, j), pre)
        dwin[...] = vjp(gwin)[0]
        dx = dwin[10:10 + TM, :] * w_ref[0:1, :]
        for k in range(1, 5):
            dx = dx + dwin[10 - k:10 - k + TM, :] * w_ref[k:k + 1, :]
        dx_ref[...] = dx

        @pl.when(i == 0)
        def _():
            dw_ref[...] = jnp.zeros_like(dw_ref)

        dmid = dwin[8:8 + TM, :]
        for k in range(5):
            dw_ref[k:k + 1, :] += jnp.sum(dmid * xwin[14 + k:14 + k + TM, :], axis=0, keepdims=True)

    main, prev, nxt = _halo_specs(nt, NQKV // 128)
    return pl.pallas_call(
        body, name=name, grid=(NQKV // 128, nt),
        in_specs=[main, prev, nxt, main, prev, nxt, pl.BlockSpec((8, 128), lambda j, i: (0, j))],
        out_specs=[main, pl.BlockSpec((8, 128), lambda j, i: (0, j))],
        out_shape=[jax.ShapeDtypeStruct((n_tok, NQKV), F32), jax.ShapeDtypeStruct((8, NQKV), F32)],
        scratch_shapes=[pltpu.VMEM((TM + 32, 128), F32), pltpu.VMEM((TM + 16, 128), F32)],
        compiler_params=_cparams("parallel", "arbitrary"),
    )(p_qkv, p_qkv, p_qkv, dqkv, dqkv, dqkv, conv_w8)


@jax.custom_vjp
def _mm_nn(a, b):
    return _nn(a, b)


@jax.custom_vjp
def _mm_nt(a, b):
    return _nt(a, b)


@jax.custom_vjp
def _mm_tn(a, b):
    return _tn(a, b)


_mm_nn.defvjp(lambda a, b: (_nn(a, b), (a, b)), lambda r, g: (_mm_nt(g, r[1]), _mm_tn(r[0], g)))
_mm_nt.defvjp(lambda a, b: (_nt(a, b), (a, b)), lambda r, g: (_mm_nn(g, r[1]), _mm_tn(g, r[0])))
_mm_tn.defvjp(lambda a, b: (_tn(a, b), (a, b)), lambda r, g: (_mm_nt(r[1], g), _mm_nn(r[0], g)))


def _unit_tri_inv(l, rev):
    ii = lax.broadcasted_iota(jnp.int32, (CH, CH), 0)
    jj = lax.broadcasted_iota(jnp.int32, (CH, CH), 1)
    if rev:
        ii, jj = jj, ii
    x = None
    s = 1
    while s < CH:
        off = ((ii & -(2 * s)) == (jj & -(2 * s))) & ((ii & s) != 0) & ((jj & s) == 0)
        c = jnp.where(off, l, 0.0)
        x = ((ii == jj).astype(F32) - c) if x is None else (x - _nn(_nn(x, c), x))
        s *= 2
    return x


def _make_tri_solve(rev):
    @jax.custom_vjp
    def solve(l, rhs):
        return _mm_nn(_unit_tri_inv(l, rev), rhs)

    def fwd(l, rhs):
        ainv = _unit_tri_inv(l, rev)
        x = _mm_nn(ainv, rhs)
        return x, (ainv, x)

    def bwd(res, g):
        ainv, x = res
        drhs = _mm_tn(ainv, g)
        return -_mm_nt(drhs, x), drhs

    solve.defvjp(fwd, bwd)
    return solve


_TRI_SOLVE = {False: _make_tri_solve(False), True: _make_tri_solve(True)}


def _chunk_step(q, k, v, beta, g, s, *, rev):
    ii = lax.broadcasted_iota(jnp.int32, (CH, CH), 0)
    jj = lax.broadcasted_iota(jnp.int32, (CH, CH), 1)
    eye = ii == jj
    incl = (ii <= jj) if rev else (ii >= jj)
    strict = (ii < jj) if rev else (ii > jj)
    g_row = jnp.sum(jnp.where(eye, g, 0.0), axis=0, keepdims=True)
    cum = jnp.sum(jnp.where(incl, g_row, 0.0), axis=1, keepdims=True)
    cum_row = jnp.sum(jnp.where(eye, cum, 0.0), axis=0, keepdims=True)
    total = jnp.sum(g, axis=0, keepdims=True)
    decay = jnp.where(incl, jnp.exp(jnp.where(incl, cum - cum_row, 0.0)), 0.0)
    kb = k * beta
    vb = v * beta
    lmat = jnp.where(strict, _mm_nt(kb, k) * decay, 0.0)
    ecum = jnp.exp(cum)
    sol = _TRI_SOLVE[rev](lmat, jnp.concatenate([vb, kb * ecum], axis=1))
    u, w = sol[:, :HD], sol[:, HD:]
    aqk = _mm_nt(q, k) * decay
    v_new = u - _mm_nn(w, s)
    o = _mm_nn(q * ecum, s) + _mm_nn(aqk, v_new)
    s_new = s * jnp.exp(total) + _mm_tn(k * jnp.exp(total - cum), v_new)
    return o, s_new


def _lane_col(x, c):
    lane = lax.broadcasted_iota(jnp.int32, x.shape, 1)
    return jnp.sum(jnp.where(lane == c, x, 0.0), axis=1, keepdims=True)


def _beta_g(ab, cst, d, h):
    braw = _lane_col(ab, NH * d + h)
    araw = _lane_col(ab, 2 * NH + NH * d + h)
    ea = _lane_col(cst[0:1, :], 2 * NH + NH * d + h)
    dt = _lane_col(cst[1:2, :], 2 * NH + NH * d + h)
    z = araw + dt
    softplus = jnp.maximum(z, 0.0) + jnp.log(1.0 + jnp.exp(-jnp.abs(z)))
    return jax.nn.sigmoid(braw), -ea * softplus, z, ea


def _scan_fwd(qkv, ab, cst, s0, *, d, row_blk0, nb, name):
    rev = bool(d)
    cb = TM // CH
    w = NH * HD

    def body(q_ref, k_ref, v_ref, ab_ref, cst_ref, s0_ref, o_ref, sall_ref, sfin_ref, s_scr):
        i = pl.program_id(0)

        @pl.when(i == 0)
        def _():
            s_scr[...] = s0_ref[...]

        def chunk(ci, carry):
            c = (cb - 1 - ci) if rev else ci
            r0 = pl.multiple_of(c * CH, CH)
            abv = ab_ref[pl.ds(r0, CH), :]
            for h in range(NH):
                hs = slice(h * HD, (h + 1) * HD)
                beta, g, _, _ = _beta_g(abv, cst_ref[...], d, h)
                s = s_scr[h]
                sall_ref[c, h] = s
                o, s_new = _chunk_step(q_ref[pl.ds(r0, CH), hs], k_ref[pl.ds(r0, CH), hs], v_ref[pl.ds(r0, CH), hs],
                                       beta, g, s, rev=rev)
                o_ref[pl.ds(r0, CH), hs] = o
                s_scr[h] = s_new
            return carry

        lax.fori_loop(0, cb, chunk, 0)

        @pl.when(i == nb - 1)
        def _():
            sfin_ref[...] = s_scr[...]

    pos = (lambda i: nb - 1 - i) if rev else (lambda i: i)
    col = lambda c: pl.BlockSpec((TM, w), lambda i: (row_blk0 + pos(i), c))
    full3 = pl.BlockSpec((NH, HD, HD), lambda i: (0, 0, 0))
    return pl.pallas_call(
        body, name=name, grid=(nb,),
        in_specs=[col(0), col(1), col(2), pl.BlockSpec((TM, 128), lambda i: (row_blk0 + pos(i), 0)),
                  pl.BlockSpec((8, 128), lambda i: (0, 0)), full3],
        out_specs=[pl.BlockSpec((TM, w), lambda i: (pos(i), 0)), pl.BlockSpec((cb, NH, HD, HD), lambda i: (pos(i), 0, 0, 0)), full3],
        out_shape=[jax.ShapeDtypeStruct((nb * TM, w), F32), jax.ShapeDtypeStruct((nb * cb, NH, HD, HD), F32),
                   jax.ShapeDtypeStruct((NH, HD, HD), F32)],
        scratch_shapes=[pltpu.VMEM((NH, HD, HD), F32)],
        compiler_params=_cparams("arbitrary"),
    )(qkv, qkv, qkv, ab, cst, s0)


def _scan_bwd(qkv, ab, cst, sall, do, dsfin, dqkv_acc, dab_acc, dcst_acc, *, d, row_blk0, nb, has_do, add, name):
    rev = bool(d)
    cb = TM // CH
    w = NH * HD

    def body(q_ref, k_ref, v_ref, ab_ref, cst_ref, sall_ref, do_ref, dsfin_ref, dqkv_in, dab_in, dcst_in,
             dqkv_ref, dab_ref, dcst_ref, ds0_ref, ds_scr):
        i = pl.program_id(0)

        @pl.when(i == 0)
        def _():
            ds_scr[...] = dsfin_ref[...]
            dcst_ref[...] = dcst_in[...]

        lane = lax.broadcasted_iota(jnp.int32, (CH, 128), 1)
        lane1 = lax.broadcasted_iota(jnp.int32, (1, 128), 1)

        def chunk(ci, carry):
            c = ci if rev else (cb - 1 - ci)
            r0 = pl.multiple_of(c * CH, CH)
            abv = ab_ref[pl.ds(r0, CH), :]
            dab = jnp.zeros((CH, 128), F32)
            dal = jnp.zeros((1, 128), F32)
            for h in range(NH):
                hs = slice(h * HD, (h + 1) * HD)
                beta, g, z, ea = _beta_g(abv, cst_ref[...], d, h)
                _, vjp = jax.vjp(functools.partial(_chunk_step, rev=rev), q_ref[pl.ds(r0, CH), hs], k_ref[pl.ds(r0, CH), hs],
                                 v_ref[pl.ds(r0, CH), hs], beta, g, sall_ref[c, h])
                do_h = do_ref[pl.ds(r0, CH), hs] if has_do else jnp.zeros((CH, HD), F32)
                dqh, dkh, dvh, dbeta, dg, ds = vjp((do_h, ds_scr[h]))
                for part, val in enumerate((dqh, dkh, dvh)):
                    cs = slice(part * w + h * HD, part * w + (h + 1) * HD)
                    dqkv_ref[pl.ds(r0, CH), cs] = (dqkv_in[pl.ds(r0, CH), cs] + val) if add else val
                ds_scr[h] = ds
                dbraw = dbeta * beta * (1.0 - beta)
                daraw = dg * (-ea) * jax.nn.sigmoid(z)
                dab = dab + jnp.where(lane == NH * d + h, dbraw, 0.0) + jnp.where(lane == 2 * NH + NH * d + h, daraw, 0.0)
                dal = dal + jnp.where(lane1 == 2 * NH + NH * d + h, jnp.sum(dg * g, axis=0, keepdims=True), 0.0)
            dab_ref[pl.ds(r0, CH), :] = (dab_in[pl.ds(r0, CH), :] + dab) if add else dab
            dcst_ref[0:1, :] += dal
            dcst_ref[1:2, :] += jnp.sum(jnp.where(lane >= 2 * NH, dab, 0.0), axis=0, keepdims=True)
            return carry

        lax.fori_loop(0, cb, chunk, 0)

        @pl.when(i == nb - 1)
        def _():
            ds0_ref[...] = ds_scr[...]

    pos = (lambda i: i) if rev else (lambda i: nb - 1 - i)
    col = lambda c: pl.BlockSpec((TM, w), lambda i: (row_blk0 + pos(i), c))
    full3 = pl.BlockSpec((NH, HD, HD), lambda i: (0, 0, 0))
    do_spec = pl.BlockSpec((TM, w), lambda i: (pos(i), 0)) if has_do else pl.BlockSpec((8, 128), lambda i: (0, 0))
    small = pl.BlockSpec((8, 128), lambda i: (0, 0))
    acc_specs = [pl.BlockSpec((TM, 3 * w), lambda i: (row_blk0 + pos(i), 0)),
                 pl.BlockSpec((TM, 128), lambda i: (row_blk0 + pos(i), 0)), small]
    return pl.pallas_call(
        body, name=name, grid=(nb,),
        in_specs=[col(0), col(1), col(2), pl.BlockSpec((TM, 128), lambda i: (row_blk0 + pos(i), 0)),
                  small, pl.BlockSpec((cb, NH, HD, HD), lambda i: (pos(i), 0, 0, 0)), do_spec, full3] + acc_specs,
        out_specs=acc_specs + [full3],
        out_shape=[jax.ShapeDtypeStruct(dqkv_acc.shape, F32), jax.ShapeDtypeStruct(dab_acc.shape, F32),
                   jax.ShapeDtypeStruct((8, 128), F32), jax.ShapeDtypeStruct((NH, HD, HD), F32)],
        input_output_aliases={8: 0, 9: 1, 10: 2},
        scratch_shapes=[pltpu.VMEM((NH, HD, HD), F32)],
        compiler_params=_cparams("arbitrary"),
    )(qkv, qkv, qkv, ab, cst, sall, do, dsfin, dqkv_acc, dab_acc, dcst_acc)


def _pool(xin, *, row0, transpose, name):
    n_tok = xin.shape[0] - row0
    rows = n_tok // GW
    pad = 8 * GW
    tt = 512
    gsh = GW.bit_length() - 1

    def body(x_ref, o_ref, ybuf):
        ii = lax.broadcasted_iota(jnp.int32, (128, 128), 0)
        jj = lax.broadcasted_iota(jnp.int32, (128, 128), 1)
        same_row = (ii >> gsh) == (jj >> gsh)
        ci, cj = ii & (GW - 1), jj & (GW - 1)
        tok = lax.broadcasted_iota(jnp.int32, (tt, 1), 0)
        zpad = jnp.zeros((pad, 128), F32)
        for gi, wdw in enumerate(POOL_WINDOWS):
            lo, hi = wdw // 2, wdw - wdw // 2
            if transpose:
                band = same_row & (ci - cj >= -lo) & (ci - cj < hi)
                offs = range(-hi + 1, lo + 1)
            else:
                band = same_row & (cj - ci >= -lo) & (cj - ci < hi)
                offs = range(-lo, hi)
            bandm = band.astype(BF)
            cs = slice(gi * 128, (gi + 1) * 128)
            ybuf[0:pad, :] = zpad
            ybuf[pad + n_tok:, :] = zpad

            def inv_area(t0):
                t = t0 + tok
                r, c = t >> gsh, t & (GW - 1)
                nr = jnp.minimum(r + hi, rows) - jnp.maximum(r - lo, 0)
                nc = jnp.minimum(c + hi, GW) - jnp.maximum(c - lo, 0)
                return 1.0 / (nr * nc).astype(F32)

            def col_pass(b, carry):
                t0 = pl.multiple_of(b * tt, tt)
                xv = x_ref[pl.ds(row0 + t0, tt), cs]
                if transpose:
                    xv = xv * inv_area(t0)
                hi_part = xv.astype(BF)
                lo_part = (xv - hi_part.astype(F32)).astype(BF)
                for s in range(tt // 128):
                    sl = slice(s * 128, (s + 1) * 128)
                    y = (jnp.dot(bandm, hi_part[sl], preferred_element_type=F32)
                         + jnp.dot(bandm, lo_part[sl], preferred_element_type=F32))
                    ybuf[pl.ds(pad + t0 + s * 128, 128), :] = y
                return carry

            lax.fori_loop(0, n_tok // tt, col_pass, 0)

            def row_pass(b, carry):
                t0 = pl.multiple_of(b * tt, tt)
                acc = ybuf[pl.ds(pad + t0 + offs[0] * GW, tt), :]
                for dr in offs[1:]:
                    acc = acc + ybuf[pl.ds(pad + t0 + dr * GW, tt), :]
                xv = x_ref[pl.ds(row0 + t0, tt), cs]
                if not transpose:
                    acc = acc * inv_area(t0)
                o_ref[pl.ds(t0, tt), cs] = acc - xv
                return carry

            lax.fori_loop(0, n_tok // tt, row_pass, 0)

    return pl.pallas_call(
        body, name=name, out_shape=jax.ShapeDtypeStruct((n_tok, NPOOL), F32),
        in_specs=[pl.BlockSpec(memory_space=pltpu.VMEM)], out_specs=pl.BlockSpec(memory_space=pltpu.VMEM),
        scratch_shapes=[pltpu.VMEM((n_tok + 2 * pad, 128), F32)],
        compiler_params=pltpu.CompilerParams(vmem_limit_bytes=VMEM_LIMIT),
    )(xin)


def _merge_parts(of, ob, pgate, pd, br, gnw, pw_ref, pscale, wg_ref, wp_ref):
    o = of + ob
    ons, ohs, rs = [], [], []
    for h in range(NH):
        oh = o[:, h * HD:(h + 1) * HD]
        r = lax.rsqrt(jnp.mean(oh * oh, axis=-1, keepdims=True) + EPS)
        ohs.append(oh * r)
        rs.append(r)
        ons.append(oh * r * gnw)
    on = jnp.concatenate(ons, axis=1)
    og = on * _silu(pgate)
    y_gdn = _nn(og, wg_ref[...])
    ypre = jnp.concatenate([_nn(pd[:, g * 128:(g + 1) * 128], pw_ref[g]) for g in range(4)], axis=1)
    yp = ypre * pscale
    y_pool = _nn(yp, wp_ref[...])
    g_pool = jax.nn.sigmoid(br[:, :D])
    g_gdn = jax.nn.sigmoid(br[:, D:])
    return dict(on=on, ohs=ohs, rs=rs, og=og, y_gdn=y_gdn, ypre=ypre, yp=yp, y_pool=y_pool, g_pool=g_pool, g_gdn=g_gdn)


def _merge_fwd(x1, of, ob, pgate, pd, br, modv, gnw, pool_w, pscale, w_gdn, w_pool, w_mo, *, name):
    n_tok = of.shape[0]

    def body(x_ref, of_ref, ob_ref, pg_ref, pd_ref, br_ref, mod_ref, gnw_ref, pw_ref, ps_ref, wg_ref, wp_ref, wmo_ref,
             x2_ref, og_ref, yp_ref, m_ref, mix_ref):
        t = _merge_parts(of_ref[...], ob_ref[...], pg_ref[...], pd_ref[...], br_ref[...], gnw_ref[...], pw_ref, ps_ref[...],
                         wg_ref, wp_ref)
        m = t["g_pool"] * t["y_pool"] + t["g_gdn"] * t["y_gdn"]
        mix = _nn(m, wmo_ref[...])
        og_ref[...] = t["og"].astype(BF)
        yp_ref[...] = t["yp"].astype(BF)
        m_ref[...] = m.astype(BF)
        mix_ref[...] = mix.astype(BF)
        x2_ref[...] = x_ref[...] + mod_ref[0, 5:6, :] * mix

    tile = lambda w: pl.BlockSpec((TM, w), lambda i: (i, 0))
    ctile = lambda w: pl.BlockSpec((TM, w), lambda i: (i + 1, 0))
    return pl.pallas_call(
        body, name=name, grid=(n_tok // TM,),
        in_specs=[ctile(D), tile(D), tile(D), ctile(D), tile(NPOOL), ctile(2 * D),
                  pl.BlockSpec((1, 16, D), lambda i: (1, 0, 0)), _const_spec((1, HD)), _const_spec((4, 128, 128)),
                  _const_spec((1, NPOOL)), _const_spec((D, D)), _const_spec((NPOOL, D)), _const_spec((D, D))],
        out_specs=[tile(D), tile(D), tile(NPOOL), tile(D), tile(D)],
        out_shape=[jax.ShapeDtypeStruct((n_tok, D), F32), jax.ShapeDtypeStruct((n_tok, D), BF),
                   jax.ShapeDtypeStruct((n_tok, NPOOL), BF), jax.ShapeDtypeStruct((n_tok, D), BF),
                   jax.ShapeDtypeStruct((n_tok, D), BF)],
        compiler_params=_cparams("parallel"),
    )(x1, of, ob, pgate, pd, br, modv, gnw, pool_w, pscale, w_gdn, w_pool, w_mo)


def _merge_bwd(dx2, mix, of, ob, pgate, pd, br, modv, gnw, pool_w, pscale, w_gdn, w_pool, w_mo, *, name):
    n_tok = of.shape[0]

    def body(dx2_ref, mix_ref, of_ref, ob_ref, pg_ref, pd_ref, br_ref, mod_ref, gnw_ref, pw_ref, ps_ref, wg_ref, wp_ref, wmo_ref,
             do_ref, dgate_ref, dpd_ref, dbr_ref, dmix_ref, dyg_ref, dyp_ref, acc_ref, dpw_ref):
        i = pl.program_id(0)
        pgate, pdv, gnw = pg_ref[...], pd_ref[...], gnw_ref[...]
        t = _merge_parts(of_ref[...], ob_ref[...], pgate, pdv, br_ref[...], gnw, pw_ref, ps_ref[...], wg_ref, wp_ref)
        dx2v = dx2_ref[...]
        dmix = mod_ref[0, 5:6, :] * dx2v
        dmixb = dmix.astype(BF)
        dmix_ref[...] = dmixb
        dm = _nt(dmixb, wmo_ref[...])
        gp, gg = t["g_pool"], t["g_gdn"]
        dbr_ref[:, :D] = dm * t["y_pool"] * gp * (1.0 - gp)
        dbr_ref[:, D:] = dm * t["y_gdn"] * gg * (1.0 - gg)
        dyp = (dm * gp).astype(BF)
        dyg = (dm * gg).astype(BF)
        dyp_ref[...] = dyp
        dyg_ref[...] = dyg
        dyp_in = _nt(dyp, wp_ref[...])
        dypre = dyp_in * ps_ref[...]
        for g in range(4):
            gs = slice(g * 128, (g + 1) * 128)
            dpd_ref[:, gs] = _nt(dypre[:, gs], pw_ref[g])
        dog = _nt(dyg, wg_ref[...])
        dgate_ref[...] = dog * t["on"] * _dsilu(pgate)
        don = dog * _silu(pgate)
        dgnw = jnp.zeros((1, HD), F32)
        for h in range(NH):
            hs = slice(h * HD, (h + 1) * HD)
            donh, oh, r = don[:, hs], t["ohs"][h], t["rs"][h]
            dgnw = dgnw + jnp.sum(donh * oh, axis=0, keepdims=True)
            doh = donh * gnw
            do_ref[:, hs] = r * (doh - oh * jnp.mean(doh * oh, axis=-1, keepdims=True))

        @pl.when(i == 0)
        def _():
            acc_ref[...] = jnp.zeros_like(acc_ref)
            dpw_ref[...] = jnp.zeros_like(dpw_ref)

        acc_ref[0:1, :] += jnp.sum(dx2v * mix_ref[...].astype(F32), axis=0, keepdims=True)
        acc_ref[1:2, 0:HD] += dgnw
        acc_ref[2:3, 0:NPOOL] += jnp.sum(dyp_in * t["ypre"], axis=0, keepdims=True)
        for g in range(4):
            gs = slice(g * 128, (g + 1) * 128)
            dpw_ref[g] += _tn(pdv[:, gs], dypre[:, gs])

    tile = lambda w: pl.BlockSpec((TM, w), lambda i: (i, 0))
    ctile = lambda w: pl.BlockSpec((TM, w), lambda i: (i + 1, 0))
    return pl.pallas_call(
        body, name=name, grid=(n_tok // TM,),
        in_specs=[tile(D), tile(D), tile(D), tile(D), ctile(D), tile(NPOOL), ctile(2 * D),
                  pl.BlockSpec((1, 16, D), lambda i: (1, 0, 0)), _const_spec((1, HD)), _const_spec((4, 128, 128)),
                  _const_spec((1, NPOOL)), _const_spec((D, D)), _const_spec((NPOOL, D)), _const_spec((D, D))],
        out_specs=[tile(D), tile(D), tile(NPOOL), tile(2 * D), tile(D), tile(D), tile(D),
                   pl.BlockSpec((8, D), lambda i: (0, 0)), pl.BlockSpec((4, 128, 128), lambda i: (0, 0, 0))],
        out_shape=[jax.ShapeDtypeStruct((n_tok, D), F32), jax.ShapeDtypeStruct((n_tok, D), F32),
                   jax.ShapeDtypeStruct((n_tok, NPOOL), F32), jax.ShapeDtypeStruct((n_tok, 2 * D), F32),
                   jax.ShapeDtypeStruct((n_tok, D), BF), jax.ShapeDtypeStruct((n_tok, D), BF), jax.ShapeDtypeStruct((n_tok, D), BF),
                   jax.ShapeDtypeStruct((8, D), F32), jax.ShapeDtypeStruct((4, 128, 128), F32)],
        compiler_params=_cparams("arbitrary"),
    )(dx2, mix, of, ob, pgate, pd, br, modv, gnw, pool_w, pscale, w_gdn, w_pool, w_mo)


def _final(x3, target, fnw, *, name):
    n_tok = x3.shape[0]

    def body(x_ref, t_ref, w_ref, dx_ref, acc_ref):
        xv, w = x_ref[...], w_ref[...]
        r = lax.rsqrt(jnp.mean(xv * xv, axis=-1, keepdims=True) + EPS)
        xh = xv * r
        err = xh * w - t_ref[...]
        dy = err * (1.0 / D)
        dxh = dy * w
        dx_ref[...] = r * (dxh - xh * jnp.mean(dxh * xh, axis=-1, keepdims=True))

        @pl.when(pl.program_id(0) == 0)
        def _():
            acc_ref[...] = jnp.zeros_like(acc_ref)

        acc_ref[0:1, :] += jnp.sum(dy * xh, axis=0, keepdims=True)
        acc_ref[1:2, :] += jnp.sum(err * err, axis=0, keepdims=True) * (0.5 / D)

    tile = pl.BlockSpec((TM, D), lambda i: (i, 0))
    return pl.pallas_call(
        body, name=name, grid=(n_tok // TM,),
        in_specs=[tile, tile, _const_spec((1, D))],
        out_specs=[tile, pl.BlockSpec((8, D), lambda i: (0, 0))],
        out_shape=[jax.ShapeDtypeStruct((n_tok, D), F32), jax.ShapeDtypeStruct((8, D), F32)],
        compiler_params=_cparams("arbitrary"),
    )(x3, target, fnw)


def _local_step(xc, target, modv, p):
    n_all = xc.shape[0]
    t_lat = n_all - TM
    nbx = t_lat // TM
    mod_lat = modv[1:2]

    x1, h1, gu1, f1 = _ffn_fwd(xc, modv, p["norm1"], p["w1_in"], p["w1_out"], mrow=0, name="ffn1_fwd")
    u, p_qkv, p_gate, p_pool, p_br, p_ab = _mix_in_fwd(x1, modv, p["norm2"], p["w_mix"], name="mix_in_fwd")
    qkv = _prep_fwd(p_qkv, p["conv"], name="prep_fwd")
    s_zero = jnp.zeros((NH, HD, HD), F32)
    _, sall_cf, sc_f = _scan_fwd(qkv, p_ab, p["cst"], s_zero, d=0, row_blk0=0, nb=1, name="scan_ctx_f")
    _, sall_cb, sc_b = _scan_fwd(qkv, p_ab, p["cst"], s_zero, d=1, row_blk0=0, nb=1, name="scan_ctx_b")
    o_f, sall_f, _ = _scan_fwd(qkv, p_ab, p["cst"], sc_f, d=0, row_blk0=1, nb=nbx, name="scan_lat_f")
    o_b, sall_b, _ = _scan_fwd(qkv, p_ab, p["cst"], sc_b, d=1, row_blk0=1, nb=nbx, name="scan_lat_b")
    pd = _pool(p_pool, row0=TM, transpose=False, name="pool_fwd")
    merge_w = (modv, p["gnw"], p["pool_w"], p["pscale"], p["w_gdn"], p["w_pool"], p["w_mo"])
    x2, og, yp, m, mix = _merge_fwd(x1, o_f, o_b, p_gate, pd, p_br, *merge_w, name="merge_fwd")
    x3, h3, gu3, f3 = _ffn_fwd(x2, mod_lat, p["norm3"], p["w2_in"], p["w2_out"], mrow=6, name="ffn2_fwd")
    dx3, acc_fin = _final(x3, target, p["fnorm"], name="final")

    dx2, a3, df3, dgu3, acc3 = _ffn_bwd(dx3, x2, gu3, f3, mod_lat, p["norm3"], p["w2_in"], p["w2_out"], mrow=6, dxo_off=0,
                                        name="ffn2_bwd")
    g = {}
    g["w2_out"] = _matmul_tn(a3, df3, tmm=FF // 2, tn=D, tk=512, name="ffn2_wout_grad").reshape(NCHIP, FF // NCHIP, D)
    g["w2_in"] = _matmul_tn(h3, dgu3, tmm=512, tn=2 * FF // NCHIP, tk=512, nsplit=NCHIP, name="ffn2_win_grad")
    do, dgate, dpd, dbr, dmix, dyg, dyp, acc_m, dpw = _merge_bwd(dx2, mix, o_f, o_b, p_gate, pd, p_br, *merge_w, name="merge_bwd")
    g["w_mo"] = _matmul_tn(m, dmix, tmm=512, tn=D, tk=512, name="wmo_grad").reshape(NCHIP, D // NCHIP, D)
    g["w_gdn"] = _matmul_tn(og, dyg, tmm=512, tn=D, tk=512, name="wgdn_grad").reshape(NCHIP, D // NCHIP, D)
    g["w_pool"] = _matmul_tn(yp, dyp, tmm=NPOOL, tn=D // NCHIP, tk=512, nsplit=NCHIP, name="wpool_grad")
    dpool_in = _pool(dpd, row0=0, transpose=True, name="pool_bwd")
    dqkv = jnp.zeros((n_all, NQKV), F32)
    dab = jnp.zeros((n_all, 128), F32)
    dcst = jnp.zeros((8, 128), F32)
    dqkv, dab, dcst, ds0_f = _scan_bwd(qkv, p_ab, p["cst"], sall_f, do, s_zero, dqkv, dab, dcst, d=0, row_blk0=1, nb=nbx,
                                       has_do=True, add=False, name="scan_lat_f_bwd")
    dqkv, dab, dcst, ds0_b = _scan_bwd(qkv, p_ab, p["cst"], sall_b, do, s_zero, dqkv, dab, dcst, d=1, row_blk0=1, nb=nbx,
                                       has_do=True, add=True, name="scan_lat_b_bwd")
    no_do = jnp.zeros((8, 128), F32)
    dqkv, dab, dcst, _ = _scan_bwd(qkv, p_ab, p["cst"], sall_cf, no_do, ds0_f, dqkv, dab, dcst, d=0, row_blk0=0, nb=1,
                                   has_do=False, add=False, name="scan_ctx_f_bwd")
    dqkv, dab, dcst, _ = _scan_bwd(qkv, p_ab, p["cst"], sall_cb, no_do, ds0_b, dqkv, dab, dcst, d=1, row_blk0=0, nb=1,
                                   has_do=False, add=True, name="scan_ctx_b_bwd")
    dpqkv, dconv = _prep_bwd(p_qkv, dqkv, p["conv"], name="prep_bwd")
    dx1, dp, acc_mix = _mix_in_bwd(dx2, x1, dpqkv, dgate, dpool_in, dbr, dab, modv, p["norm2"], p["w_mix"], name="mix_in_bwd")
    g["w_mix"] = _matmul_tn(u, dp, tmm=256, tn=NMIXP, tk=256, name="wmix_grad")
    dxc, a1, df1, dgu1, acc1 = _ffn_bwd(dx1, xc, gu1, f1, modv, p["norm1"], p["w1_in"], p["w1_out"], mrow=0, dxo_off=0,
                                        name="ffn1_bwd")
    g["w1_out"] = _matmul_tn(a1, df1, tmm=FF // 2, tn=D, tk=256, name="ffn1_wout_grad").reshape(NCHIP, FF // NCHIP, D)
    g["w1_in"] = _matmul_tn(h1, dgu1, tmm=512, tn=2 * FF // NCHIP, tk=256, nsplit=NCHIP, name="ffn1_win_grad")

    small = dict(norm1=acc1[0, 3] + acc1[1, 3], norm2=acc_mix[0, 3] + acc_mix[1, 3], norm3=acc3[0, 3], fnorm=acc_fin[0],
                 gnw=acc_m[1, :HD], pscale=acc_m[2, :NPOOL], pool_w=dpw, conv=dconv[:5],
                 a_log=dcst[0, 2 * NH:4 * NH], dt_bias=dcst[1, 2 * NH:4 * NH])
    zero = jnp.zeros((D,), F32)
    dmod = jnp.stack([
        jnp.stack([acc1[0, 0], acc1[0, 1], acc1[0, 2], acc_mix[0, 0], acc_mix[0, 1], zero, zero, zero, zero]),
        jnp.stack([acc1[1, 0], acc1[1, 1], acc1[1, 2], acc_mix[1, 0], acc_mix[1, 1], acc_m[0], acc3[0, 0], acc3[0, 1], acc3[0, 2]]),
    ])
    return jnp.sum(acc_fin[1]), dxc, g, small, dmod


_HI = lax.Precision.HIGHEST


def _ada_fwd(c_all, w_sh, b_sh, *, name):
    def body(c_ref, w_ref, b_ref, o_ref):
        o_ref[...] = jnp.dot(_silu(c_ref[...]), w_ref[...], precision=_HI, preferred_element_type=F32) + b_ref[...]

    return pl.pallas_call(body, name=name, out_shape=jax.ShapeDtypeStruct((16, w_sh.shape[1]), F32),
                          compiler_params=pltpu.CompilerParams(vmem_limit_bytes=VMEM_LIMIT))(c_all, w_sh, b_sh)


def _ada_bwd(c_all, dm, w_sh, *, name):
    def body(c_ref, dm_ref, w_ref, dw_ref, dc_ref):
        sc = _silu(c_ref[...])
        dw_ref[...] = lax.dot_general(sc, dm_ref[...], (((0,), (0,)), ((), ())), precision=_HI, preferred_element_type=F32)
        part = lax.dot_general(dm_ref[8:9, :], w_ref[...], (((1,), (1,)), ((), ())), precision=_HI, preferred_element_type=F32)
        dc_ref[...] = jnp.broadcast_to(part, dc_ref.shape)

    return pl.pallas_call(body, name=name,
                          out_shape=[jax.ShapeDtypeStruct(w_sh.shape, F32), jax.ShapeDtypeStruct((8, D), F32)],
                          compiler_params=pltpu.CompilerParams(vmem_limit_bytes=VMEM_LIMIT))(c_all, dm, w_sh)


def _cctx_grad(parts, c_ctx, *, name):
    def body(p_ref, c_ref, o_ref):
        tot = (p_ref[0, 0:1, :] + p_ref[2, 0:1, :]) + (p_ref[4, 0:1, :] + p_ref[6, 0:1, :])
        o_ref[...] = tot * _dsilu(c_ref[...])

    return pl.pallas_call(body, name=name, out_shape=jax.ShapeDtypeStruct((1, D), F32))(parts, c_ctx)


_MESH = pl.DeviceIdType.MESH
_ANY = pl.BlockSpec(memory_space=pl.ANY)


def _flip(v, bit):
    return (1 - v) if bit else v


def _all_gather8(x, *, name):
    def body(x_ref, out_ref, send_sems, recv_sems, local_sem):
        mx, my, mc = lax.axis_index("x"), lax.axis_index("y"), lax.axis_index("c")
        me = 4 * mx + 2 * my + mc
        mine = pltpu.make_async_copy(x_ref, out_ref.at[me], local_sem)
        mine.start()
        sends, recvs = [], []
        for k in range(1, 8):
            px, py, pc = _flip(mx, k & 4), _flip(my, k & 2), _flip(mc, k & 1)
            sends.append(pltpu.make_async_remote_copy(src_ref=x_ref, dst_ref=out_ref.at[me], send_sem=send_sems.at[k - 1],
                                                      recv_sem=recv_sems.at[k - 1], device_id=(px, py, pc), device_id_type=_MESH))
            recvs.append(pltpu.make_async_remote_copy(src_ref=x_ref, dst_ref=out_ref.at[4 * px + 2 * py + pc],
                                                      send_sem=send_sems.at[k - 1], recv_sem=recv_sems.at[k - 1],
                                                      device_id=(px, py, pc), device_id_type=_MESH))
        for cp in sends:
            cp.start()
        for cp in recvs:
            cp.wait_recv()
        for cp in sends:
            cp.wait_send()
        mine.wait()

    vm = pl.BlockSpec(memory_space=pltpu.VMEM)
    return pl.pallas_call(
        body, name=name, out_shape=jax.ShapeDtypeStruct((8,) + x.shape, x.dtype), in_specs=[vm], out_specs=vm,
        scratch_shapes=[pltpu.SemaphoreType.DMA((7,)), pltpu.SemaphoreType.DMA((7,)), pltpu.SemaphoreType.DMA],
        compiler_params=pltpu.CompilerParams(vmem_limit_bytes=VMEM_LIMIT),
    )(x)


def _chip_exchange(arrs, *, scatter, name):
    n = len(arrs)

    def body(*refs):
        ins, outs = refs[:n], refs[n:2 * n]
        send_sems, recv_sems, local_sems = refs[2 * n:]
        mx, my, mc = lax.axis_index("x"), lax.axis_index("y"), lax.axis_index("c")
        me = 2 * mx + my
        local, sends, recvs = [], [], []
        for j in range(n):
            src_own = ins[j].at[me] if scatter else ins[j]
            local.append(pltpu.make_async_copy(src_own, outs[j].at[me], local_sems.at[j]))
            for k in range(1, NCHIP):
                px, py = _flip(mx, k & 2), _flip(my, k & 1)
                peer = 2 * px + py
                sem = j * (NCHIP - 1) + k - 1
                src = ins[j].at[peer] if scatter else ins[j]
                sends.append(pltpu.make_async_remote_copy(src_ref=src, dst_ref=outs[j].at[me], send_sem=send_sems.at[sem],
                                                          recv_sem=recv_sems.at[sem], device_id=(px, py, mc), device_id_type=_MESH))
                recvs.append(pltpu.make_async_remote_copy(src_ref=src, dst_ref=outs[j].at[peer], send_sem=send_sems.at[sem],
                                                          recv_sem=recv_sems.at[sem], device_id=(px, py, mc), device_id_type=_MESH))
        for cp in local + sends:
            cp.start()
        for cp in recvs:
            cp.wait_recv()
        for cp in sends:
            cp.wait_send()
        for cp in local:
            cp.wait()

    out_shape = [jax.ShapeDtypeStruct(a.shape if scatter else (NCHIP,) + a.shape, a.dtype) for a in arrs]
    return pl.pallas_call(
        body, name=name, out_shape=out_shape, in_specs=[_ANY] * n, out_specs=[_ANY] * n,
        scratch_shapes=[pltpu.SemaphoreType.DMA((n * (NCHIP - 1),)), pltpu.SemaphoreType.DMA((n * (NCHIP - 1),)),
                        pltpu.SemaphoreType.DMA((n,))],
    )(*arrs)


def _core_swap(arrs, *, name):
    n = len(arrs)

    def body(*refs):
        ins, outs = refs[:n], refs[n:2 * n]
        send_sems, recv_sems = refs[2 * n:]
        sib = (lax.axis_index("x"), lax.axis_index("y"), 1 - lax.axis_index("c"))
        cps = [pltpu.make_async_remote_copy(src_ref=ins[j], dst_ref=outs[j], send_sem=send_sems.at[j], recv_sem=recv_sems.at[j],
                                            device_id=sib, device_id_type=_MESH) for j in range(n)]
        for cp in cps:
            cp.start()
        for cp in cps:
            cp.wait_recv()
        for cp in cps:
            cp.wait_send()

    return pl.pallas_call(
        body, name=name, out_shape=[jax.ShapeDtypeStruct(a.shape, a.dtype) for a in arrs],
        in_specs=[_ANY] * n, out_specs=[_ANY] * n,
        scratch_shapes=[pltpu.SemaphoreType.DMA((n,)), pltpu.SemaphoreType.DMA((n,))],
    )(*arrs)


def _row_tile(rows, cols, budget=1 << 18):
    best = None
    for t in range(8, rows + 1, 8):
        if rows % t == 0 and t * cols <= budget:
            best = t
    return best or rows


def _sum_slots(x, *, name):
    ns, r, c = x.shape
    tr = _row_tile(r, c * ns)

    def body(x_ref, o_ref):
        acc = x_ref[0].astype(F32)
        for s in range(1, ns):
            acc = acc + x_ref[s].astype(F32)
        o_ref[...] = acc

    return pl.pallas_call(
        body, name=name, grid=(r // tr,), out_shape=jax.ShapeDtypeStruct((r, c), F32),
        in_specs=[pl.BlockSpec((ns, tr, c), lambda i: (0, i, 0))], out_specs=pl.BlockSpec((tr, c), lambda i: (i, 0)),
        compiler_params=_cparams("parallel"),
    )(x)


def _adamw(w, ga, gb, m, v, *, name):
    r, c = w.shape
    tr = _row_tile(r, c, budget=1 << 17)
    two = gb is not None

    def body(*refs):
        w_ref, ga_ref = refs[0], refs[1]
        m_ref, v_ref = refs[2 + two], refs[3 + two]
        g_ref, d_ref, mo_ref, vo_ref = refs[4 + two:]
        g = ga_ref[...] + refs[2][...] if two else ga_ref[...]
        mn = ADAM_B1 * m_ref[...] + (1.0 - ADAM_B1) * g
        vn = ADAM_B2 * v_ref[...] + (1.0 - ADAM_B2) * (g * g)
        m_hat = mn / (1.0 - ADAM_B1 ** ADAM_STEP)
        v_hat = vn / (1.0 - ADAM_B2 ** ADAM_STEP)
        g_ref[...] = g
        d_ref[...] = -ADAM_LR * (m_hat / (jnp.sqrt(v_hat) + ADAM_EPS) + ADAM_WD * w_ref[...])
        mo_ref[...] = mn
        vo_ref[...] = vn

    spec = pl.BlockSpec((tr, c), lambda i: (i, 0))
    ins = [w, ga] + ([gb] if two else []) + [m, v]
    return pl.pallas_call(
        body, name=name, grid=(r // tr,), out_shape=[jax.ShapeDtypeStruct((r, c), F32)] * 4,
        in_specs=[spec] * len(ins), out_specs=[spec] * 4, compiler_params=_cparams("parallel"),
    )(*ins)


_MIX_AB0, _MIX_AB1 = NQKV, NQKV + 4 * NH


def _regroup_mix(w):
    pad = jnp.zeros((w.shape[0], NMIXP - NMIX), w.dtype)
    return jnp.concatenate([w[:, :_MIX_AB0], w[:, _MIX_AB1:], w[:, _MIX_AB0:_MIX_AB1], pad], axis=1)


def _ungroup_mix(w):
    n_ab = _MIX_AB1 - _MIX_AB0
    return jnp.concatenate([w[:, :_MIX_AB0], w[:, NMIX - n_ab:NMIX], w[:, _MIX_AB0:NMIX - n_ab]], axis=1)


def _chip_major_cols(w):
    r, c = w.shape
    return w.reshape(r, NCHIP, c // NCHIP).transpose(1, 0, 2)


def _from_chip_major_cols(w):
    return w.transpose(1, 0, 2).reshape(w.shape[1], -1)


_SMALL = (("c_ctx", D), ("b_ada", 9 * D), ("norm1_w", D), ("norm2_w", D), ("norm3_w", D), ("final_norm_w", D),
          ("a_log", 2 * NH), ("dt_bias", 2 * NH), ("gdn_norm_w", HD), ("pool_w", 4 * 128 * 128), ("pool_scale", NPOOL),
          ("conv_w", 5 * NQKV // NCHIP))


def _pack(vals, lanes=128, row_mult=8):
    flat = jnp.concatenate([jnp.ravel(v) for v in vals])
    n = flat.shape[0]
    rows = -(-n // (lanes * row_mult)) * row_mult
    return jnp.pad(flat, (0, rows * lanes - n)).reshape(rows, lanes)


def _unpack(packed, sizes):
    flat = packed.reshape(-1)
    out, o = [], 0
    for n in sizes:
        out.append(flat[o:o + n])
        o += n
    return out


def kernel(x, c, ctx, c_ctx, w_ada, b_ada, norm1_w, ffn1_w_in, ffn1_w_out, norm2_w, w_mix_in, conv_w, a_log, dt_bias, gdn_norm_w, w_gdn_proj, pool_w, pool_scale, w_pool_proj, w_mix_out, norm3_w, ffn2_w_in, ffn2_w_out, final_norm_w, loss_target, m_c_ctx, m_w_ada, m_b_ada, m_norm1_w, m_ffn1_w_in, m_ffn1_w_out, m_norm2_w, m_w_mix_in, m_conv_w, m_a_log, m_dt_bias, m_gdn_norm_w, m_w_gdn_proj, m_pool_w, m_pool_scale, m_w_pool_proj, m_w_mix_out, m_norm3_w, m_ffn2_w_in, m_ffn2_w_out, m_final_norm_w, v_c_ctx, v_w_ada, v_b_ada, v_norm1_w, v_ffn1_w_in, v_ffn1_w_out, v_norm2_w, v_w_mix_in, v_conv_w, v_a_log, v_dt_bias, v_gdn_norm_w, v_w_gdn_proj, v_pool_w, v_pool_scale, v_w_pool_proj, v_w_mix_out, v_norm3_w, v_ffn2_w_in, v_ffn2_w_out, v_final_norm_w):
    names = ("c_ctx", "w_ada", "b_ada", "norm1_w", "ffn1_w_in", "ffn1_w_out", "norm2_w", "w_mix_in", "conv_w", "a_log", "dt_bias",
             "gdn_norm_w", "w_gdn_proj", "pool_w", "pool_scale", "w_pool_proj", "w_mix_out", "norm3_w", "ffn2_w_in", "ffn2_w_out",
             "final_norm_w")
    w = dict(zip(names, (c_ctx, w_ada, b_ada, norm1_w, ffn1_w_in, ffn1_w_out, norm2_w, w_mix_in, conv_w, a_log, dt_bias, gdn_norm_w,
                         w_gdn_proj, pool_w, pool_scale, w_pool_proj, w_mix_out, norm3_w, ffn2_w_in, ffn2_w_out, final_norm_w)))
    mom = dict(zip(names, (m_c_ctx, m_w_ada, m_b_ada, m_norm1_w, m_ffn1_w_in, m_ffn1_w_out, m_norm2_w, m_w_mix_in, m_conv_w, m_a_log,
                           m_dt_bias, m_gdn_norm_w, m_w_gdn_proj, m_pool_w, m_pool_scale, m_w_pool_proj, m_w_mix_out, m_norm3_w,
                           m_ffn2_w_in, m_ffn2_w_out, m_final_norm_w)))
    var = dict(zip(names, (v_c_ctx, v_w_ada, v_b_ada, v_norm1_w, v_ffn1_w_in, v_ffn1_w_out, v_norm2_w, v_w_mix_in, v_conv_w, v_a_log,
                           v_dt_bias, v_gdn_norm_w, v_w_gdn_proj, v_pool_w, v_pool_scale, v_w_pool_proj, v_w_mix_out, v_norm3_w,
                           v_ffn2_w_in, v_ffn2_w_out, v_final_norm_w)))
    mx, my, mc = lax.axis_index("x"), lax.axis_index("y"), lax.axis_index("c")
    chip = 2 * mx + my
    dev = 2 * chip + mc
    ada_cols = w_ada.shape[2]

    c_rows = _all_gather8(jnp.pad(c, ((0, 7), (0, 0))), name="gather_c")[:, 0, :]
    c_all = jnp.concatenate([c_rows, c_ctx[None], jnp.zeros((7, D), F32)], axis=0)
    b_sh = lax.dynamic_slice(b_ada, (0, chip * ada_cols), (1, ada_cols))
    mod_sh = _ada_fwd(c_all, w_ada[0], b_sh, name="ada_fwd")
    mod_parts = _all_gather8(mod_sh, name="gather_mod")
    mod_all = jnp.concatenate([mod_parts[2 * s] for s in range(NCHIP)], axis=1)
    mod_lat = lax.dynamic_index_in_dim(mod_all, dev, axis=0, keepdims=False).reshape(9, D)
    modv = jnp.zeros((2, 16, D), F32).at[0, :9].set(mod_all[8].reshape(9, D)).at[1, :9].set(mod_lat)

    big = ("ffn1_w_in", "ffn1_w_out", "w_mix_in", "w_gdn_proj", "w_pool_proj", "w_mix_out", "ffn2_w_in", "ffn2_w_out")
    gathered = _chip_exchange([w[k][0].astype(BF) for k in big] + [conv_w[0]], scatter=False, name="gather_weights")
    gw = dict(zip(big + ("conv_w",), gathered))
    conv_full = _from_chip_major_cols(gw["conv_w"])
    p = dict(
        norm1=norm1_w, norm2=norm2_w, norm3=norm3_w, fnorm=final_norm_w[None],
        w1_in=gw["ffn1_w_in"], w1_out=gw["ffn1_w_out"].reshape(FF, D), w2_in=gw["ffn2_w_in"], w2_out=gw["ffn2_w_out"].reshape(FF, D),
        w_mix=_regroup_mix(_from_chip_major_cols(gw["w_mix_in"])), conv=jnp.pad(conv_full, ((0, 3), (0, 0))),
        cst=jnp.zeros((8, 128), F32).at[0, 2 * NH:4 * NH].set(jnp.exp(a_log).reshape(-1)).at[1, 2 * NH:4 * NH].set(dt_bias.reshape(-1)),
        gnw=gdn_norm_w, pool_w=pool_w[0], pscale=pool_scale,
        w_gdn=gw["w_gdn_proj"].reshape(D, D), w_pool=_from_chip_major_cols(gw["w_pool_proj"]), w_mo=gw["w_mix_out"].reshape(D, D))

    xc = jnp.concatenate([ctx[0], x[0]], axis=0)
    loss_dev, dxc, g, small, dmod = _local_step(xc, loss_target[0], modv, p)
    loss = lax.psum(loss_dev, ("x", "y", "c"))
    grad_x = dxc[TM:][None]

    g_mix = _chip_major_cols(_ungroup_mix(g["w_mix"]))
    parts = [g["w1_in"], g["w1_out"], g_mix, g["w_gdn"], g["w_pool"], g["w_mo"], g["w2_in"], g["w2_out"]]
    landed = _chip_exchange(parts, scatter=True, name="scatter_grads")
    mine = [_sum_slots(a, name=f"sum_{k}") for k, a in zip(big, landed)]
    theirs = _core_swap(mine, name="swap_grad_sums")

    small_vals = [dmod[1], dmod[0], small["norm1"], small["norm2"], small["norm3"], small["fnorm"], small["a_log"], small["dt_bias"],
                  small["gnw"], small["pool_w"], small["pscale"], small["conv"]]
    small_sizes = [v.size for v in small_vals]
    packed = _all_gather8(_pack(small_vals), name="gather_small")
    tot = _unpack(_sum_slots(packed, name="sum_small"), small_sizes)
    dmod_lat_all = packed[:, :9 * D // 128, :].reshape(8, 9 * D)
    dm = jnp.concatenate([dmod_lat_all, tot[1][None], jnp.zeros((7, 9 * D), F32)], axis=0)
    dm_sh = lax.dynamic_slice(dm, (0, chip * ada_cols), (16, ada_cols))
    g_w_ada, cctx_part = _ada_bwd(c_all, dm_sh, w_ada[0], name="ada_bwd")
    g_c_ctx = _cctx_grad(_all_gather8(cctx_part, name="gather_cctx"), c_ctx[None], name="cctx_grad")[0]
    conv_tot = tot[11].reshape(5, NQKV)
    g_small = dict(c_ctx=g_c_ctx, b_ada=tot[0] + tot[1], norm1_w=tot[2], norm2_w=tot[3], norm3_w=tot[4], final_norm_w=tot[5],
                   a_log=tot[6], dt_bias=tot[7], gdn_norm_w=tot[8], pool_w=tot[9], pool_scale=tot[10],
                   conv_w=lax.dynamic_slice(conv_tot, (0, chip * (NQKV // NCHIP)), (5, NQKV // NCHIP)))

    out = {}
    as2d = lambda a: a.reshape(-1, a.shape[-1])
    for k, ga, gb in zip(big, mine, theirs):
        shp = w[k].shape
        res = _adamw(as2d(w[k]), as2d(ga), as2d(gb), as2d(mom[k]), as2d(var[k]), name=f"adamw_{k}")
        out[k] = [r.reshape(shp) for r in res]
    out["w_ada"] = [r.reshape(w_ada.shape) for r in _adamw(w_ada[0], g_w_ada, None, m_w_ada[0], v_w_ada[0], name="adamw_w_ada")]
    sm_names = [n for n, _ in _SMALL]
    sm_sizes = [n for _, n in _SMALL]
    res = _adamw(_pack([w[k] for k in sm_names]), _pack([g_small[k] for k in sm_names]), None,
                 _pack([mom[k] for k in sm_names]), _pack([var[k] for k in sm_names]), name="adamw_small")
    res = [_unpack(r, sm_sizes) for r in res]
    for i, k in enumerate(sm_names):
        out[k] = [r[i].reshape(w[k].shape) for r in res]
    return (loss, grad_x, *[out[k][0] for k in names], *[out[k][1] for k in names], *[out[k][2] for k in names],
            *[out[k][3] for k in names])
```

```python
import functools

import jax
import jax.numpy as jnp
from jax import lax
from jax.experimental import pallas as pl
from jax.experimental.pallas import tpu as pltpu

F32 = jnp.float32
BF = jnp.bfloat16

D = 1024
FF = 2816
NH = 8
HD = 128
CH = 64
GW = 64
TM = 256
NQKV = 3 * NH * HD
NPOOL = 512
POOL_WINDOWS = (2, 4, 8, 16)
NMIX = 6688
NMIXP = 6784
EPS = 1e-6
NCHIP = 4
VMEM_LIMIT = 56 * 1024 * 1024

ADAM_LR, ADAM_B1, ADAM_B2, ADAM_EPS, ADAM_WD, ADAM_STEP = 0.001, 0.9, 0.999, 1e-08, 0.01, 10


def _cparams(*sem):
    return pltpu.CompilerParams(dimension_semantics=sem, vmem_limit_bytes=VMEM_LIMIT)


def _const_spec(shape):
    nd = len(shape)
    return pl.BlockSpec(shape, lambda *_: (0,) * nd, pipeline_mode=pl.Buffered(1))


def _dot(a, b, dims):
    return lax.dot_general(a.astype(BF), b.astype(BF), (dims, ((), ())), preferred_element_type=F32)


def _nn(a, b):
    return _dot(a, b, ((1,), (0,)))


def _nt(a, b):
    return _dot(a, b, ((1,), (1,)))


def _tn(a, b):
    return _dot(a, b, ((0,), (0,)))


def _silu(x):
    return x * jax.nn.sigmoid(x)


def _dsilu(x):
    s = jax.nn.sigmoid(x)
    return s * (1.0 + x * (1.0 - s))


def _norm_mod(x, nw, shift, scale):
    r = lax.rsqrt(jnp.mean(x * x, axis=-1, keepdims=True) + EPS)
    xh = x * r
    n = xh * nw
    return n * (1.0 + scale) + shift, n, xh, r


def _norm_mod_bwd(dh, n, xh, r, nw, scale):
    dn = dh * (1.0 + scale)
    dxh = dn * nw
    dx = r * (dxh - xh * jnp.mean(dxh * xh, axis=-1, keepdims=True))
    rs = lambda t: jnp.sum(t, axis=0, keepdims=True)
    return dx, rs(dh), rs(dh * n), rs(dn * xh)


def _ffn_fwd(x, modv, nw, w_in4, w_out, *, mrow, name):
    n_tok = x.shape[0]
    nt = n_tok // TM
    nset = modv.shape[0]
    ws = w_in4.shape[2]

    def body(x_ref, mod_ref, nw_ref, win_ref, wout_ref, x1_ref, h_ref, gu_ref, f_ref):
        xv = x_ref[...]
        shift, scale, gate = mod_ref[0, mrow:mrow + 1, :], mod_ref[0, mrow + 1:mrow + 2, :], mod_ref[0, mrow + 2:mrow + 3, :]
        h, _, _, _ = _norm_mod(xv, nw_ref[...], shift, scale)
        hb = h.astype(BF)
        h_ref[...] = hb
        gus = [_nn(hb, win_ref[s]) for s in range(NCHIP)]
        for s in range(NCHIP):
            gu_ref[:, s * ws:(s + 1) * ws] = gus[s].astype(BF)
        g = jnp.concatenate(gus[:2], axis=1)
        u = jnp.concatenate(gus[2:], axis=1)
        f = _nn(_silu(g) * u, wout_ref[...])
        f_ref[...] = f.astype(BF)
        x1_ref[...] = xv + 0.5 * gate * f

    tile = lambda w: pl.BlockSpec((TM, w), lambda i: (i, 0))
    return pl.pallas_call(
        body, name=name, grid=(nt,),
        in_specs=[tile(D), pl.BlockSpec((1, 16, D), lambda i: (jnp.minimum(i, nset - 1), 0, 0)), _const_spec((1, D)),
                  _const_spec(w_in4.shape), _const_spec(w_out.shape)],
        out_specs=[tile(D), tile(D), tile(2 * FF), tile(D)],
        out_shape=[jax.ShapeDtypeStruct((n_tok, D), F32), jax.ShapeDtypeStruct((n_tok, D), BF),
                   jax.ShapeDtypeStruct((n_tok, 2 * FF), BF), jax.ShapeDtypeStruct((n_tok, D), BF)],
        compiler_params=_cparams("parallel"),
    )(x, modv, nw, w_in4, w_out)


def _ffn_bwd(dxo, x, gu, fo, modv, nw, w_in4, w_out, *, mrow, dxo_off, name):
    n_tok = x.shape[0]
    nt = n_tok // TM
    nset = modv.shape[0]
    ws = w_in4.shape[2]

    def body(dxo_ref, x_ref, gu_ref, f_ref, mod_ref, nw_ref, win_ref, wout_ref, dx_ref, a_ref, df_ref, dgu_ref, acc_ref):
        i = pl.program_id(0)
        xv = x_ref[...]
        dxo_v = dxo_ref[...]
        if dxo_off:
            dxo_v = jnp.where(i >= dxo_off, dxo_v, 0.0)
        shift, scale, gate = mod_ref[0, mrow:mrow + 1, :], mod_ref[0, mrow + 1:mrow + 2, :], mod_ref[0, mrow + 2:mrow + 3, :]
        _, n, xh, r = _norm_mod(xv, nw_ref[...], shift, scale)
        df = 0.5 * gate * dxo_v
        dfb = df.astype(BF)
        df_ref[...] = dfb
        dgate = jnp.sum(0.5 * dxo_v * f_ref[...].astype(F32), axis=0, keepdims=True)
        da = _nt(dfb, wout_ref[...])
        g = gu_ref[:, :FF].astype(F32)
        u = gu_ref[:, FF:].astype(F32)
        sg = _silu(g)
        a_ref[...] = (sg * u).astype(BF)
        dgu_ref[:, :FF] = (da * u * _dsilu(g)).astype(BF)
        dgu_ref[:, FF:] = (da * sg).astype(BF)
        dh = _nt(dgu_ref[:, 0:ws], win_ref[0])
        for s in range(1, NCHIP):
            dh = dh + _nt(dgu_ref[:, s * ws:(s + 1) * ws], win_ref[s])
        dx, dshift, dscale, dnw = _norm_mod_bwd(dh, n, xh, r, nw_ref[...], scale)
        dx_ref[...] = dxo_v + dx

        @pl.when((i == 0) | (i == nset - 1))
        def _():
            acc_ref[...] = jnp.zeros_like(acc_ref)

        acc_ref[0, 0:1, :] += dshift
        acc_ref[0, 1:2, :] += dscale
        acc_ref[0, 2:3, :] += dgate
        acc_ref[0, 3:4, :] += dnw

    tile = lambda w: pl.BlockSpec((TM, w), lambda i: (i, 0))
    return pl.pallas_call(
        body, name=name, grid=(nt,),
        in_specs=[pl.BlockSpec((TM, D), lambda i: (jnp.maximum(i - dxo_off, 0), 0)), tile(D), tile(2 * FF), tile(D),
                  pl.BlockSpec((1, 16, D), lambda i: (jnp.minimum(i, nset - 1), 0, 0)), _const_spec((1, D)),
                  _const_spec(w_in4.shape), _const_spec(w_out.shape)],
        out_specs=[tile(D), tile(FF), tile(D), tile(2 * FF),
                   pl.BlockSpec((1, 8, D), lambda i: (jnp.minimum(i, nset - 1), 0, 0))],
        out_shape=[jax.ShapeDtypeStruct((n_tok, D), F32), jax.ShapeDtypeStruct((n_tok, FF), BF),
                   jax.ShapeDtypeStruct((n_tok, D), BF), jax.ShapeDtypeStruct((n_tok, 2 * FF), BF),
                   jax.ShapeDtypeStruct((nset, 8, D), F32)],
        compiler_params=_cparams("arbitrary"),
    )(dxo, x, gu, fo, modv, nw, w_in4, w_out)


def _matmul_tn(a, b, *, tmm, tn, tk, nsplit=1, name):
    n_tok, m = a.shape
    kk = b.shape[1]
    nk = n_tok // tk

    def body(a_ref, b_ref, o_ref, acc):
        k = pl.program_id(2)

        @pl.when(k == 0)
        def _():
            acc[...] = jnp.zeros_like(acc)

        acc[...] += _tn(a_ref[...], b_ref[...])

        @pl.when(k == nk - 1)
        def _():
            o_ref[...] = acc[...].astype(BF).reshape(o_ref.shape)

    if nsplit == 1:
        out_shape = jax.ShapeDtypeStruct((m, kk), BF)
        out_spec = pl.BlockSpec((tmm, tn), lambda i, j, k: (i, j))
    else:
        assert tn == kk // nsplit
        out_shape = jax.ShapeDtypeStruct((nsplit, m, tn), BF)
        out_spec = pl.BlockSpec((1, tmm, tn), lambda i, j, k: (j, i, 0))
    return pl.pallas_call(
        body, name=name, grid=(m // tmm, kk // tn, nk),
        in_specs=[pl.BlockSpec((tk, tmm), lambda i, j, k: (k, i)), pl.BlockSpec((tk, tn), lambda i, j, k: (k, j))],
        out_specs=out_spec, out_shape=out_shape,
        scratch_shapes=[pltpu.VMEM((tmm, tn), F32)],
        compiler_params=_cparams("parallel", "parallel", "arbitrary"),
    )(a, b)


_MIX_PARTS = (("qkv", 0, NQKV), ("gate", NQKV, 1024), ("pool", NQKV + 1024, NPOOL), ("br", NQKV + 1024 + NPOOL, 2048),
              ("ab", NMIXP - 128, 128))


def _mix_in_fwd(x1, modv, nw, w_mix, *, name):
    n_tok = x1.shape[0]

    def body(x_ref, mod_ref, nw_ref, w_ref, u_ref, *p_refs):
        u, _, _, _ = _norm_mod(x_ref[...], nw_ref[...], mod_ref[0, 3:4, :], mod_ref[0, 4:5, :])
        ub = u.astype(BF)
        u_ref[...] = ub
        for (_, c0, w), p_ref in zip(_MIX_PARTS, p_refs):
            p_ref[...] = _nn(ub, w_ref[:, c0:c0 + w])

    tile = lambda w: pl.BlockSpec((TM, w), lambda i: (i, 0))
    ctile = lambda w: pl.BlockSpec((TM, w), lambda i: (i + 1, 0))
    return pl.pallas_call(
        body, name=name, grid=(n_tok // TM,),
        in_specs=[tile(D), pl.BlockSpec((1, 16, D), lambda i: (jnp.minimum(i, 1), 0, 0)), _const_spec((1, D)),
                  _const_spec(w_mix.shape)],
        out_specs=[tile(D)] + [tile(w) for _, _, w in _MIX_PARTS],
        out_shape=[jax.ShapeDtypeStruct((n_tok, D), BF)] + [jax.ShapeDtypeStruct((n_tok, w), F32) for _, _, w in _MIX_PARTS],
        compiler_params=_cparams("parallel"),
    )(x1, modv, nw, w_mix)


def _mix_in_bwd(dxo, x1, dqkv, dgate, dpool, dbr, dab, modv, nw, w_mix, *, name):
    n_tok = x1.shape[0]

    def body(dxo_ref, x_ref, dqkv_ref, dgate_ref, dpool_ref, dbr_ref, dab_ref, mod_ref, nw_ref, w_ref,
             dx_ref, dp_ref, acc_ref):
        i = pl.program_id(0)
        lat = i >= 1
        scale = mod_ref[0, 4:5, :]
        _, n, xh, r = _norm_mod(x_ref[...], nw_ref[...], mod_ref[0, 3:4, :], scale)
        dp_ref[:, 0:NQKV] = dqkv_ref[...].astype(BF)
        dp_ref[:, NQKV:NQKV + 1024] = jnp.where(lat, dgate_ref[...], 0.0).astype(BF)
        dp_ref[:, NQKV + 1024:NQKV + 1536] = jnp.where(lat, dpool_ref[...], 0.0).astype(BF)
        dp_ref[:, NQKV + 1536:NMIXP - 128] = jnp.where(lat, dbr_ref[...], 0.0).astype(BF)
        dp_ref[:, NMIXP - 128:] = dab_ref[...].astype(BF)
        du = _nt(dp_ref[...], w_ref[...])
        dx, dshift, dscale, dnw = _norm_mod_bwd(du, n, xh, r, nw_ref[...], scale)
        dx_ref[...] = jnp.where(lat, dxo_ref[...], 0.0) + dx

        @pl.when(i <= 1)
        def _():
            acc_ref[...] = jnp.zeros_like(acc_ref)

        acc_ref[0, 0:1, :] += dshift
        acc_ref[0, 1:2, :] += dscale
        acc_ref[0, 3:4, :] += dnw

    tile = lambda w: pl.BlockSpec((TM, w), lambda i: (i, 0))
    ltile = lambda w: pl.BlockSpec((TM, w), lambda i: (jnp.maximum(i - 1, 0), 0))
    return pl.pallas_call(
        body, name=name, grid=(n_tok // TM,),
        in_specs=[ltile(D), tile(D), tile(NQKV), ltile(1024), ltile(NPOOL), ltile(2048), tile(128),
                  pl.BlockSpec((1, 16, D), lambda i: (jnp.minimum(i, 1), 0, 0)), _const_spec((1, D)), _const_spec(w_mix.shape)],
        out_specs=[tile(D), tile(NMIXP), pl.BlockSpec((1, 8, D), lambda i: (jnp.minimum(i, 1), 0, 0))],
        out_shape=[jax.ShapeDtypeStruct((n_tok, D), F32), jax.ShapeDtypeStruct((n_tok, NMIXP), BF),
                   jax.ShapeDtypeStruct((2, 8, D), F32)],
        compiler_params=_cparams("arbitrary"),
    )(dxo, x1, dqkv, dgate, dpool, dbr, dab, modv, nw, w_mix)


def _qkv_act(pre, j):
    s = _silu(pre)
    nrm = s * lax.rsqrt(jnp.sum(s * s, axis=-1, keepdims=True) + EPS)
    nrm = nrm * jnp.where(j == 0, HD ** -0.5, 1.0)
    return jnp.where(j < 2, nrm, s)


def _halo_specs(nt):
    r = TM // 8
    w = NH * HD
    main = pl.BlockSpec((TM, w), lambda j, i: (i, j))
    prev = pl.BlockSpec((8, w), lambda j, i: (jnp.maximum(i * r - 1, 0), j))
    nxt = pl.BlockSpec((8, w), lambda j, i: (jnp.minimum((i + 1) * r, nt * r - 1), j))
    return main, prev, nxt


def _prep_fwd(p_qkv, conv_w8, *, name):
    n_tok = p_qkv.shape[0]
    nt = n_tok // TM

    def body(x_ref, xp_ref, xn_ref, w_ref, o_ref, win):
        j, i = pl.program_id(0), pl.program_id(1)
        has_prev = (i != 0) & (i != 1)
        has_next = (i != 0) & (i != nt - 1)
        win[0:8, :] = jnp.where(has_prev, xp_ref[...], 0.0)
        win[8:8 + TM, :] = x_ref[...]
        win[8 + TM:, :] = jnp.where(has_next, xn_ref[...], 0.0)
        for h in range(NH):
            hs = slice(h * HD, (h + 1) * HD)
            pre = win[6:6 + TM, hs] * w_ref[0:1, hs]
            for k in range(1, 5):
                pre = pre + win[6 + k:6 + k + TM, hs] * w_ref[k:k + 1, hs]
            o_ref[:, hs] = _qkv_act(pre, j)

    main, prev, nxt = _halo_specs(nt)
    wq = NH * HD
    return pl.pallas_call(
        body, name=name, grid=(3, nt),
        in_specs=[main, prev, nxt, pl.BlockSpec((8, wq), lambda j, i: (0, j))],
        out_specs=main, out_shape=jax.ShapeDtypeStruct((n_tok, NQKV), F32),
        scratch_shapes=[pltpu.VMEM((TM + 16, wq), F32)],
        compiler_params=_cparams("parallel", "arbitrary"),
    )(p_qkv, p_qkv, p_qkv, conv_w8)


def _prep_bwd(p_qkv, dqkv, conv_w8, *, name):
    n_tok = p_qkv.shape[0]
    nt = n_tok // TM
    wr = TM + 16

    wq = NH * HD

    def body(x_ref, xp_ref, xn_ref, g_ref, gp_ref, gn_ref, w_ref, dx_ref, dw_ref, xwin, gwin, dwin):
        j, i = pl.program_id(0), pl.program_id(1)
        has_prev = (i != 0) & (i != 1)
        has_next = (i != 0) & (i != nt - 1)
        z8 = jnp.zeros((8, wq), F32)
        xwin[0:8, :] = z8
        xwin[8:16, :] = jnp.where(has_prev, xp_ref[...], 0.0)
        xwin[16:16 + TM, :] = x_ref[...]
        xwin[16 + TM:24 + TM, :] = jnp.where(has_next, xn_ref[...], 0.0)
        xwin[24 + TM:, :] = z8
        gwin[0:8, :] = jnp.where(has_prev, gp_ref[...], 0.0)
        gwin[8:8 + TM, :] = g_ref[...]
        gwin[8 + TM:, :] = jnp.where(has_next, gn_ref[...], 0.0)

        @pl.when(i == 0)
        def _():
            dw_ref[...] = jnp.zeros_like(dw_ref)

        for h in range(NH):
            hs = slice(h * HD, (h + 1) * HD)
            pre = xwin[6:6 + wr, hs] * w_ref[0:1, hs]
            for k in range(1, 5):
                pre = pre + xwin[6 + k:6 + k + wr, hs] * w_ref[k:k + 1, hs]
            _, vjp = jax.vjp(lambda t: _qkv_act(t, j), pre)
            dwin[:, hs] = vjp(gwin[:, hs])[0]
            dx = dwin[10:10 + TM, hs] * w_ref[0:1, hs]
            for k in range(1, 5):
                dx = dx + dwin[10 - k:10 - k + TM, hs] * w_ref[k:k + 1, hs]
            dx_ref[:, hs] = dx
            dmid = dwin[8:8 + TM, hs]
            for k in range(5):
                dw_ref[k:k + 1, hs] += jnp.sum(dmid * xwin[14 + k:14 + k + TM, hs], axis=0, keepdims=True)

    main, prev, nxt = _halo_specs(nt)
    wq = NH * HD
    wspec = pl.BlockSpec((8, wq), lambda j, i: (0, j))
    return pl.pallas_call(
        body, name=name, grid=(3, nt),
        in_specs=[main, prev, nxt, main, prev, nxt, wspec],
        out_specs=[main, wspec],
        out_shape=[jax.ShapeDtypeStruct((n_tok, NQKV), F32), jax.ShapeDtypeStruct((8, NQKV), F32)],
        scratch_shapes=[pltpu.VMEM((TM + 32, wq), F32), pltpu.VMEM((TM + 16, wq), F32), pltpu.VMEM((TM + 16, wq), F32)],
        compiler_params=_cparams("parallel", "arbitrary"),
    )(p_qkv, p_qkv, p_qkv, dqkv, dqkv, dqkv, conv_w8)


@jax.custom_vjp
def _mm_nn(a, b):
    return _nn(a, b)


@jax.custom_vjp
def _mm_nt(a, b):
    return _nt(a, b)


@jax.custom_vjp
def _mm_tn(a, b):
    return _tn(a, b)


_mm_nn.defvjp(lambda a, b: (_nn(a, b), (a, b)), lambda r, g: (_mm_nt(g, r[1]), _mm_tn(r[0], g)))
_mm_nt.defvjp(lambda a, b: (_nt(a, b), (a, b)), lambda r, g: (_mm_nn(g, r[1]), _mm_tn(g, r[0])))
_mm_tn.defvjp(lambda a, b: (_tn(a, b), (a, b)), lambda r, g: (_mm_nt(r[1], g), _mm_nn(r[0], g)))


def _each(f, *lists):
    return tuple(f(*a) for a in zip(*lists))


def _unit_tri_inv(ls, rev):
    ii = lax.broadcasted_iota(jnp.int32, (CH, CH), 0)
    jj = lax.broadcasted_iota(jnp.int32, (CH, CH), 1)
    if rev:
        ii, jj = jj, ii
    eye = (ii == jj).astype(F32)
    xs = None
    s = 1
    while s < CH:
        off = ((ii & -(2 * s)) == (jj & -(2 * s))) & ((ii & s) != 0) & ((jj & s) == 0)
        cs = _each(lambda l: jnp.where(off, l, 0.0), ls)
        if xs is None:
            xs = _each(lambda c: eye - c, cs)
        else:
            xc = _each(_nn, xs, cs)
            xcx = _each(_nn, xc, xs)
            xs = _each(lambda x, t: x - t, xs, xcx)
        s *= 2
    return xs


def _make_tri_solve(rev):
    @jax.custom_vjp
    def solve(ls, rhss):
        return _each(_mm_nn, _unit_tri_inv(ls, rev), rhss)

    def fwd(ls, rhss):
        ainv = _unit_tri_inv(ls, rev)
        xs = _each(_mm_nn, ainv, rhss)
        return xs, (ainv, xs)

    def bwd(res, gs):
        ainv, xs = res
        drhs = _each(_mm_tn, ainv, gs)
        return _each(lambda d, x: -_mm_nt(d, x), drhs, xs), drhs

    solve.defvjp(fwd, bwd)
    return solve


_TRI_SOLVE = {False: _make_tri_solve(False), True: _make_tri_solve(True)}


def _chunk_step(q, k, v, beta, g, s, *, rev):
    ii = lax.broadcasted_iota(jnp.int32, (CH, CH), 0)
    jj = lax.broadcasted_iota(jnp.int32, (CH, CH), 1)
    eye = ii == jj
    incl = (ii <= jj) if rev else (ii >= jj)
    strict = (ii < jj) if rev else (ii > jj)
    g_row = _each(lambda t: jnp.sum(jnp.where(eye, t, 0.0), axis=0, keepdims=True), g)
    cum = _each(lambda t: jnp.sum(jnp.where(incl, t, 0.0), axis=1, keepdims=True), g_row)
    cum_row = _each(lambda t: jnp.sum(jnp.where(eye, t, 0.0), axis=0, keepdims=True), cum)
    total = _each(lambda t: jnp.sum(t, axis=0, keepdims=True), g)
    decay = _each(lambda c, cr: jnp.where(incl, jnp.exp(jnp.where(incl, c - cr, 0.0)), 0.0), cum, cum_row)
    kb = _each(jnp.multiply, k, beta)
    vb = _each(jnp.multiply, v, beta)
    kk = _each(_mm_nt, kb, k)
    lmat = _each(lambda t, dc: jnp.where(strict, t * dc, 0.0), kk, decay)
    ecum = _each(jnp.exp, cum)
    rhs = _each(lambda a, b, e: jnp.concatenate([a, b * e], axis=1), vb, kb, ecum)
    sol = _TRI_SOLVE[rev](lmat, rhs)
    qk = _each(_mm_nt, q, k)
    aqk = _each(jnp.multiply, qk, decay)
    ws = _each(lambda so, st: _mm_nn(so[:, HD:], st), sol, s)
    v_new = _each(lambda so, t: so[:, :HD] - t, sol, ws)
    qs = _each(lambda a, e, st: _mm_nn(a * e, st), q, ecum, s)
    av = _each(_mm_nn, aqk, v_new)
    o = _each(jnp.add, qs, av)
    kv = _each(lambda a, t, c, vn: _mm_tn(a * jnp.exp(t - c), vn), k, total, cum, v_new)
    s_new = _each(lambda st, t, u: st * jnp.exp(t) + u, s, total, kv)
    return o, s_new


def _lane_col(x, c):
    lane = lax.broadcasted_iota(jnp.int32, x.shape, 1)
    return jnp.sum(jnp.where(lane == c, x, 0.0), axis=1, keepdims=True)


def _beta_g(ab, cst, d, h):
    braw = _lane_col(ab, NH * d + h)
    araw = _lane_col(ab, 2 * NH + NH * d + h)
    ea = _lane_col(cst[0:1, :], 2 * NH + NH * d + h)
    dt = _lane_col(cst[1:2, :], 2 * NH + NH * d + h)
    z = araw + dt
    softplus = jnp.maximum(z, 0.0) + jnp.log(1.0 + jnp.exp(-jnp.abs(z)))
    return jax.nn.sigmoid(braw), -ea * softplus, z, ea


def _scan_fwd(qkv, ab, cst, s0, *, d, row_blk0, nb, name):
    rev = bool(d)
    cb = TM // CH
    w = NH * HD

    def body(q_ref, k_ref, v_ref, ab_ref, cst_ref, s0_ref, o_ref, sall_ref, sfin_ref, s_scr):
        i = pl.program_id(0)

        @pl.when(i == 0)
        def _():
            s_scr[...] = s0_ref[...]

        def chunk(ci, carry):
            c = (cb - 1 - ci) if rev else ci
            r0 = pl.multiple_of(c * CH, CH)
            abv = ab_ref[pl.ds(r0, CH), :]
            heads = tuple(range(NH))
            hs = lambda h: slice(h * HD, (h + 1) * HD)
            bg = _each(lambda h: _beta_g(abv, cst_ref[...], d, h), heads)
            s = _each(lambda h: s_scr[h], heads)
            for h in heads:
                sall_ref[c, h] = s[h]
            o, s_new = _chunk_step(_each(lambda h: q_ref[pl.ds(r0, CH), hs(h)], heads),
                                   _each(lambda h: k_ref[pl.ds(r0, CH), hs(h)], heads),
                                   _each(lambda h: v_ref[pl.ds(r0, CH), hs(h)], heads),
                                   _each(lambda t: t[0], bg), _each(lambda t: t[1], bg), s, rev=rev)
            for h in heads:
                o_ref[pl.ds(r0, CH), hs(h)] = o[h]
                s_scr[h] = s_new[h]
            return carry

        lax.fori_loop(0, cb, chunk, 0)

        @pl.when(i == nb - 1)
        def _():
            sfin_ref[...] = s_scr[...]

    pos = (lambda i: nb - 1 - i) if rev else (lambda i: i)
    col = lambda c: pl.BlockSpec((TM, w), lambda i: (row_blk0 + pos(i), c))
    full3 = pl.BlockSpec((NH, HD, HD), lambda i: (0, 0, 0))
    return pl.pallas_call(
        body, name=name, grid=(nb,),
        in_specs=[col(0), col(1), col(2), pl.BlockSpec((TM, 128), lambda i: (row_blk0 + pos(i), 0)),
                  pl.BlockSpec((8, 128), lambda i: (0, 0)), full3],
        out_specs=[pl.BlockSpec((TM, w), lambda i: (pos(i), 0)), pl.BlockSpec((cb, NH, HD, HD), lambda i: (pos(i), 0, 0, 0)), full3],
        out_shape=[jax.ShapeDtypeStruct((nb * TM, w), F32), jax.ShapeDtypeStruct((nb * cb, NH, HD, HD), F32),
                   jax.ShapeDtypeStruct((NH, HD, HD), F32)],
        scratch_shapes=[pltpu.VMEM((NH, HD, HD), F32)],
        compiler_params=_cparams("arbitrary"),
    )(qkv, qkv, qkv, ab, cst, s0)


def _scan_bwd(qkv, ab, cst, sall, do, dsfin, dqkv_acc, dab_acc, dcst_acc, *, d, row_blk0, nb, has_do, add, name):
    rev = bool(d)
    cb = TM // CH
    w = NH * HD

    def body(q_ref, k_ref, v_ref, ab_ref, cst_ref, sall_ref, do_ref, dsfin_ref, dqkv_in, dab_in, dcst_in,
             dqkv_ref, dab_ref, dcst_ref, ds0_ref, ds_scr):
        i = pl.program_id(0)

        @pl.when(i == 0)
        def _():
            ds_scr[...] = dsfin_ref[...]
            dcst_ref[...] = dcst_in[...]

        lane = lax.broadcasted_iota(jnp.int32, (CH, 128), 1)
        lane1 = lax.broadcasted_iota(jnp.int32, (1, 128), 1)

        def chunk(ci, carry):
            c = ci if rev else (cb - 1 - ci)
            r0 = pl.multiple_of(c * CH, CH)
            abv = ab_ref[pl.ds(r0, CH), :]
            dab = jnp.zeros((CH, 128), F32)
            dal = jnp.zeros((1, 128), F32)
            heads = tuple(range(NH))
            hs = lambda h: slice(h * HD, (h + 1) * HD)
            bg = _each(lambda h: _beta_g(abv, cst_ref[...], d, h), heads)
            beta, g = _each(lambda t: t[0], bg), _each(lambda t: t[1], bg)
            _, vjp = jax.vjp(functools.partial(_chunk_step, rev=rev),
                             _each(lambda h: q_ref[pl.ds(r0, CH), hs(h)], heads),
                             _each(lambda h: k_ref[pl.ds(r0, CH), hs(h)], heads),
                             _each(lambda h: v_ref[pl.ds(r0, CH), hs(h)], heads),
                             beta, g, _each(lambda h: sall_ref[c, h], heads))
            do_all = _each(lambda h: do_ref[pl.ds(r0, CH), hs(h)] if has_do else jnp.zeros((CH, HD), F32), heads)
            dq, dk, dv, dbeta, dg, ds = vjp((do_all, _each(lambda h: ds_scr[h], heads)))
            for h in heads:
                for part, val in enumerate((dq[h], dk[h], dv[h])):
                    cs = slice(part * w + h * HD, part * w + (h + 1) * HD)
                    dqkv_ref[pl.ds(r0, CH), cs] = (dqkv_in[pl.ds(r0, CH), cs] + val) if add else val
                ds_scr[h] = ds[h]
                z, ea = bg[h][2], bg[h][3]
                dbraw = dbeta[h] * beta[h] * (1.0 - beta[h])
                daraw = dg[h] * (-ea) * jax.nn.sigmoid(z)
                dab = dab + jnp.where(lane == NH * d + h, dbraw, 0.0) + jnp.where(lane == 2 * NH + NH * d + h, daraw, 0.0)
                dal = dal + jnp.where(lane1 == 2 * NH + NH * d + h, jnp.sum(dg[h] * g[h], axis=0, keepdims=True), 0.0)
            dab_ref[pl.ds(r0, CH), :] = (dab_in[pl.ds(r0, CH), :] + dab) if add else dab
            dcst_ref[0:1, :] += dal
            dcst_ref[1:2, :] += jnp.sum(jnp.where(lane >= 2 * NH, dab, 0.0), axis=0, keepdims=True)
            return carry

        lax.fori_loop(0, cb, chunk, 0)

        @pl.when(i == nb - 1)
        def _():
            ds0_ref[...] = ds_scr[...]

    pos = (lambda i: i) if rev else (lambda i: nb - 1 - i)
    col = lambda c: pl.BlockSpec((TM, w), lambda i: (row_blk0 + pos(i), c))
    full3 = pl.BlockSpec((NH, HD, HD), lambda i: (0, 0, 0))
    do_spec = pl.BlockSpec((TM, w), lambda i: (pos(i), 0)) if has_do else pl.BlockSpec((8, 128), lambda i: (0, 0))
    small = pl.BlockSpec((8, 128), lambda i: (0, 0))
    acc_specs = [pl.BlockSpec((TM, 3 * w), lambda i: (row_blk0 + pos(i), 0)),
                 pl.BlockSpec((TM, 128), lambda i: (row_blk0 + pos(i), 0)), small]
    return pl.pallas_call(
        body, name=name, grid=(nb,),
        in_specs=[col(0), col(1), col(2), pl.BlockSpec((TM, 128), lambda i: (row_blk0 + pos(i), 0)),
                  small, pl.BlockSpec((cb, NH, HD, HD), lambda i: (pos(i), 0, 0, 0)), do_spec, full3] + acc_specs,
        out_specs=acc_specs + [full3],
        out_shape=[jax.ShapeDtypeStruct(dqkv_acc.shape, F32), jax.ShapeDtypeStruct(dab_acc.shape, F32),
                   jax.ShapeDtypeStruct((8, 128), F32), jax.ShapeDtypeStruct((NH, HD, HD), F32)],
        input_output_aliases={8: 0, 9: 1, 10: 2},
        scratch_shapes=[pltpu.VMEM((NH, HD, HD), F32)],
        compiler_params=_cparams("arbitrary"),
    )(qkv, qkv, qkv, ab, cst, sall, do, dsfin, dqkv_acc, dab_acc, dcst_acc)


def _pool(xin, *, row0, transpose, name):
    n_tok = xin.shape[0] - row0
    rows = n_tok // GW
    pad = 8 * GW
    tt = 512
    gsh = GW.bit_length() - 1

    def body(x_ref, o_ref, ybuf):
        ii = lax.broadcasted_iota(jnp.int32, (128, 128), 0)
        jj = lax.broadcasted_iota(jnp.int32, (128, 128), 1)
        same_row = (ii >> gsh) == (jj >> gsh)
        ci, cj = ii & (GW - 1), jj & (GW - 1)
        tok = lax.broadcasted_iota(jnp.int32, (tt, 1), 0)
        zpad = jnp.zeros((pad, 128), F32)
        for gi, wdw in enumerate(POOL_WINDOWS):
            lo, hi = wdw // 2, wdw - wdw // 2
            if transpose:
                band = same_row & (ci - cj >= -lo) & (ci - cj < hi)
                offs = range(-hi + 1, lo + 1)
            else:
                band = same_row & (cj - ci >= -lo) & (cj - ci < hi)
                offs = range(-lo, hi)
            bandm = band.astype(BF)
            cs = slice(gi * 128, (gi + 1) * 128)
            ybuf[0:pad, :] = zpad
            ybuf[pad + n_tok:, :] = zpad

            def inv_area(t0):
                t = t0 + tok
                r, c = t >> gsh, t & (GW - 1)
                nr = jnp.minimum(r + hi, rows) - jnp.maximum(r - lo, 0)
                nc = jnp.minimum(c + hi, GW) - jnp.maximum(c - lo, 0)
                return 1.0 / (nr * nc).astype(F32)

            def col_pass(b, carry):
                t0 = pl.multiple_of(b * tt, tt)
                xv = x_ref[pl.ds(row0 + t0, tt), cs]
                if transpose:
                    xv = xv * inv_area(t0)
                hi_part = xv.astype(BF)
                lo_part = (xv - hi_part.astype(F32)).astype(BF)
                for s in range(tt // 128):
                    sl = slice(s * 128, (s + 1) * 128)
                    y = (jnp.dot(bandm, hi_part[sl], preferred_element_type=F32)
                         + jnp.dot(bandm, lo_part[sl], preferred_element_type=F32))
                    ybuf[pl.ds(pad + t0 + s * 128, 128), :] = y
                return carry

            lax.fori_loop(0, n_tok // tt, col_pass, 0)

            def row_pass(b, carry):
                t0 = pl.multiple_of(b * tt, tt)
                acc = ybuf[pl.ds(pad + t0 + offs[0] * GW, tt), :]
                for dr in offs[1:]:
                    acc = acc + ybuf[pl.ds(pad + t0 + dr * GW, tt), :]
                xv = x_ref[pl.ds(row0 + t0, tt), cs]
                if not transpose:
                    acc = acc * inv_area(t0)
                o_ref[pl.ds(t0, tt), cs] = acc - xv
                return carry

            lax.fori_loop(0, n_tok // tt, row_pass, 0)

    return pl.pallas_call(
        body, name=name, out_shape=jax.ShapeDtypeStruct((n_tok, NPOOL), F32),
        in_specs=[pl.BlockSpec(memory_space=pltpu.VMEM)], out_specs=pl.BlockSpec(memory_space=pltpu.VMEM),
        scratch_shapes=[pltpu.VMEM((n_tok + 2 * pad, 128), F32)],
        compiler_params=pltpu.CompilerParams(vmem_limit_bytes=VMEM_LIMIT),
    )(xin)


def _merge_parts(of, ob, pgate, pd, br, gnw, pw_ref, pscale, wg_ref, wp_ref):
    o = of + ob
    ons, ohs, rs = [], [], []
    for h in range(NH):
        oh = o[:, h * HD:(h + 1) * HD]
        r = lax.rsqrt(jnp.mean(oh * oh, axis=-1, keepdims=True) + EPS)
        ohs.append(oh * r)
        rs.append(r)
        ons.append(oh * r * gnw)
    on = jnp.concatenate(ons, axis=1)
    og = on * _silu(pgate)
    y_gdn = _nn(og, wg_ref[...])
    ypre = jnp.concatenate([_nn(pd[:, g * 128:(g + 1) * 128], pw_ref[g]) for g in range(4)], axis=1)
    yp = ypre * pscale
    y_pool = _nn(yp, wp_ref[...])
    g_pool = jax.nn.sigmoid(br[:, :D])
    g_gdn = jax.nn.sigmoid(br[:, D:])
    return dict(on=on, ohs=ohs, rs=rs, og=og, y_gdn=y_gdn, ypre=ypre, yp=yp, y_pool=y_pool, g_pool=g_pool, g_gdn=g_gdn)


def _merge_fwd(x1, of, ob, pgate, pd, br, modv, gnw, pool_w, pscale, w_gdn, w_pool, w_mo, *, name):
    n_tok = of.shape[0]

    def body(x_ref, of_ref, ob_ref, pg_ref, pd_ref, br_ref, mod_ref, gnw_ref, pw_ref, ps_ref, wg_ref, wp_ref, wmo_ref,
             x2_ref, og_ref, yp_ref, m_ref, mix_ref):
        t = _merge_parts(of_ref[...], ob_ref[...], pg_ref[...], pd_ref[...], br_ref[...], gnw_ref[...], pw_ref, ps_ref[...],
                         wg_ref, wp_ref)
        m = t["g_pool"] * t["y_pool"] + t["g_gdn"] * t["y_gdn"]
        mix = _nn(m, wmo_ref[...])
        og_ref[...] = t["og"].astype(BF)
        yp_ref[...] = t["yp"].astype(BF)
        m_ref[...] = m.astype(BF)
        mix_ref[...] = mix.astype(BF)
        x2_ref[...] = x_ref[...] + mod_ref[0, 5:6, :] * mix

    tile = lambda w: pl.BlockSpec((TM, w), lambda i: (i, 0))
    ctile = lambda w: pl.BlockSpec((TM, w), lambda i: (i + 1, 0))
    return pl.pallas_call(
        body, name=name, grid=(n_tok // TM,),
        in_specs=[ctile(D), tile(D), tile(D), ctile(D), tile(NPOOL), ctile(2 * D),
                  pl.BlockSpec((1, 16, D), lambda i: (1, 0, 0)), _const_spec((1, HD)), _const_spec((4, 128, 128)),
                  _const_spec((1, NPOOL)), _const_spec((D, D)), _const_spec((NPOOL, D)), _const_spec((D, D))],
        out_specs=[tile(D), tile(D), tile(NPOOL), tile(D), tile(D)],
        out_shape=[jax.ShapeDtypeStruct((n_tok, D), F32), jax.ShapeDtypeStruct((n_tok, D), BF),
                   jax.ShapeDtypeStruct((n_tok, NPOOL), BF), jax.ShapeDtypeStruct((n_tok, D), BF),
                   jax.ShapeDtypeStruct((n_tok, D), BF)],
        compiler_params=_cparams("parallel"),
    )(x1, of, ob, pgate, pd, br, modv, gnw, pool_w, pscale, w_gdn, w_pool, w_mo)


def _merge_bwd(dx2, mix, of, ob, pgate, pd, br, modv, gnw, pool_w, pscale, w_gdn, w_pool, w_mo, *, name):
    n_tok = of.shape[0]

    def body(dx2_ref, mix_ref, of_ref, ob_ref, pg_ref, pd_ref, br_ref, mod_ref, gnw_ref, pw_ref, ps_ref, wg_ref, wp_ref, wmo_ref,
             do_ref, dgate_ref, dpd_ref, dbr_ref, dmix_ref, dyg_ref, dyp_ref, acc_ref, dpw_ref):
        i = pl.program_id(0)
        pgate, pdv, gnw = pg_ref[...], pd_ref[...], gnw_ref[...]
        t = _merge_parts(of_ref[...], ob_ref[...], pgate, pdv, br_ref[...], gnw, pw_ref, ps_ref[...], wg_ref, wp_ref)
        dx2v = dx2_ref[...]
        dmix = mod_ref[0, 5:6, :] * dx2v
        dmixb = dmix.astype(BF)
        dmix_ref[...] = dmixb
        dm = _nt(dmixb, wmo_ref[...])
        gp, gg = t["g_pool"], t["g_gdn"]
        dbr_ref[:, :D] = dm * t["y_pool"] * gp * (1.0 - gp)
        dbr_ref[:, D:] = dm * t["y_gdn"] * gg * (1.0 - gg)
        dyp = (dm * gp).astype(BF)
        dyg = (dm * gg).astype(BF)
        dyp_ref[...] = dyp
        dyg_ref[...] = dyg
        dyp_in = _nt(dyp, wp_ref[...])
        dypre = dyp_in * ps_ref[...]
        for g in range(4):
            gs = slice(g * 128, (g + 1) * 128)
            dpd_ref[:, gs] = _nt(dypre[:, gs], pw_ref[g])
        dog = _nt(dyg, wg_ref[...])
        dgate_ref[...] = dog * t["on"] * _dsilu(pgate)
        don = dog * _silu(pgate)
        dgnw = jnp.zeros((1, HD), F32)
        for h in range(NH):
            hs = slice(h * HD, (h + 1) * HD)
            donh, oh, r = don[:, hs], t["ohs"][h], t["rs"][h]
            dgnw = dgnw + jnp.sum(donh * oh, axis=0, keepdims=True)
            doh = donh * gnw
            do_ref[:, hs] = r * (doh - oh * jnp.mean(doh * oh, axis=-1, keepdims=True))

        @pl.when(i == 0)
        def _():
            acc_ref[...] = jnp.zeros_like(acc_ref)
            dpw_ref[...] = jnp.zeros_like(dpw_ref)

        acc_ref[0:1, :] += jnp.sum(dx2v * mix_ref[...].astype(F32), axis=0, keepdims=True)
        acc_ref[1:2, 0:HD] += dgnw
        acc_ref[2:3, 0:NPOOL] += jnp.sum(dyp_in * t["ypre"], axis=0, keepdims=True)
        for g in range(4):
            gs = slice(g * 128, (g + 1) * 128)
            dpw_ref[g] += _tn(pdv[:, gs], dypre[:, gs])

    tile = lambda w: pl.BlockSpec((TM, w), lambda i: (i, 0))
    ctile = lambda w: pl.BlockSpec((TM, w), lambda i: (i + 1, 0))
    return pl.pallas_call(
        body, name=name, grid=(n_tok // TM,),
        in_specs=[tile(D), tile(D), tile(D), tile(D), ctile(D), tile(NPOOL), ctile(2 * D),
                  pl.BlockSpec((1, 16, D), lambda i: (1, 0, 0)), _const_spec((1, HD)), _const_spec((4, 128, 128)),
                  _const_spec((1, NPOOL)), _const_spec((D, D)), _const_spec((NPOOL, D)), _const_spec((D, D))],
        out_specs=[tile(D), tile(D), tile(NPOOL), tile(2 * D), tile(D), tile(D), tile(D),
                   pl.BlockSpec((8, D), lambda i: (0, 0)), pl.BlockSpec((4, 128, 128), lambda i: (0, 0, 0))],
        out_shape=[jax.ShapeDtypeStruct((n_tok, D), F32), jax.ShapeDtypeStruct((n_tok, D), F32),
                   jax.ShapeDtypeStruct((n_tok, NPOOL), F32), jax.ShapeDtypeStruct((n_tok, 2 * D), F32),
                   jax.ShapeDtypeStruct((n_tok, D), BF), jax.ShapeDtypeStruct((n_tok, D), BF), jax.ShapeDtypeStruct((n_tok, D), BF),
                   jax.ShapeDtypeStruct((8, D), F32), jax.ShapeDtypeStruct((4, 128, 128), F32)],
        compiler_params=_cparams("arbitrary"),
    )(dx2, mix, of, ob, pgate, pd, br, modv, gnw, pool_w, pscale, w_gdn, w_pool, w_mo)


def _final(x3, target, fnw, *, name):
    n_tok = x3.shape[0]

    def body(x_ref, t_ref, w_ref, dx_ref, acc_ref):
        xv, w = x_ref[...], w_ref[...]
        r = lax.rsqrt(jnp.mean(xv * xv, axis=-1, keepdims=True) + EPS)
        xh = xv * r
        err = xh * w - t_ref[...]
        dy = err * (1.0 / D)
        dxh = dy * w
        dx_ref[...] = r * (dxh - xh * jnp.mean(dxh * xh, axis=-1, keepdims=True))

        @pl.when(pl.program_id(0) == 0)
        def _():
            acc_ref[...] = jnp.zeros_like(acc_ref)

        acc_ref[0:1, :] += jnp.sum(dy * xh, axis=0, keepdims=True)
        acc_ref[1:2, :] += jnp.sum(err * err, axis=0, keepdims=True) * (0.5 / D)

    tile = pl.BlockSpec((TM, D), lambda i: (i, 0))
    return pl.pallas_call(
        body, name=name, grid=(n_tok // TM,),
        in_specs=[tile, tile, _const_spec((1, D))],
        out_specs=[tile, pl.BlockSpec((8, D), lambda i: (0, 0))],
        out_shape=[jax.ShapeDtypeStruct((n_tok, D), F32), jax.ShapeDtypeStruct((8, D), F32)],
        compiler_params=_cparams("arbitrary"),
    )(x3, target, fnw)


def _local_step(xc, target, modv, p):
    n_all = xc.shape[0]
    t_lat = n_all - TM
    nbx = t_lat // TM
    mod_lat = modv[1:2]

    x1, h1, gu1, f1 = _ffn_fwd(xc, modv, p["norm1"], p["w1_in"], p["w1_out"], mrow=0, name="ffn1_fwd")
    u, p_qkv, p_gate, p_pool, p_br, p_ab = _mix_in_fwd(x1, modv, p["norm2"], p["w_mix"], name="mix_in_fwd")
    qkv = _prep_fwd(p_qkv, p["conv"], name="prep_fwd")
    s_zero = jnp.zeros((NH, HD, HD), F32)
    _, sall_cf, sc_f = _scan_fwd(qkv, p_ab, p["cst"], s_zero, d=0, row_blk0=0, nb=1, name="scan_ctx_f")
    _, sall_cb, sc_b = _scan_fwd(qkv, p_ab, p["cst"], s_zero, d=1, row_blk0=0, nb=1, name="scan_ctx_b")
    o_f, sall_f, _ = _scan_fwd(qkv, p_ab, p["cst"], sc_f, d=0, row_blk0=1, nb=nbx, name="scan_lat_f")
    o_b, sall_b, _ = _scan_fwd(qkv, p_ab, p["cst"], sc_b, d=1, row_blk0=1, nb=nbx, name="scan_lat_b")
    pd = _pool(p_pool, row0=TM, transpose=False, name="pool_fwd")
    merge_w = (modv, p["gnw"], p["pool_w"], p["pscale"], p["w_gdn"], p["w_pool"], p["w_mo"])
    x2, og, yp, m, mix = _merge_fwd(x1, o_f, o_b, p_gate, pd, p_br, *merge_w, name="merge_fwd")
    x3, h3, gu3, f3 = _ffn_fwd(x2, mod_lat, p["norm3"], p["w2_in"], p["w2_out"], mrow=6, name="ffn2_fwd")
    dx3, acc_fin = _final(x3, target, p["fnorm"], name="final")

    dx2, a3, df3, dgu3, acc3 = _ffn_bwd(dx3, x2, gu3, f3, mod_lat, p["norm3"], p["w2_in"], p["w2_out"], mrow=6, dxo_off=0,
                                        name="ffn2_bwd")
    g = {}
    g["w2_out"] = _matmul_tn(a3, df3, tmm=FF // 2, tn=D, tk=512, name="ffn2_wout_grad").reshape(NCHIP, FF // NCHIP, D)
    g["w2_in"] = _matmul_tn(h3, dgu3, tmm=512, tn=2 * FF // NCHIP, tk=512, nsplit=NCHIP, name="ffn2_win_grad")
    do, dgate, dpd, dbr, dmix, dyg, dyp, acc_m, dpw = _merge_bwd(dx2, mix, o_f, o_b, p_gate, pd, p_br, *merge_w, name="merge_bwd")
    g["w_mo"] = _matmul_tn(m, dmix, tmm=512, tn=D, tk=512, name="wmo_grad").reshape(NCHIP, D // NCHIP, D)
    g["w_gdn"] = _matmul_tn(og, dyg, tmm=512, tn=D, tk=512, name="wgdn_grad").reshape(NCHIP, D // NCHIP, D)
    g["w_pool"] = _matmul_tn(yp, dyp, tmm=NPOOL, tn=D // NCHIP, tk=512, nsplit=NCHIP, name="wpool_grad")
    dpool_in = _pool(dpd, row0=0, transpose=True, name="pool_bwd")
    dqkv = jnp.zeros((n_all, NQKV), F32)
    dab = jnp.zeros((n_all, 128), F32)
    dcst = jnp.zeros((8, 128), F32)
    dqkv, dab, dcst, ds0_f = _scan_bwd(qkv, p_ab, p["cst"], sall_f, do, s_zero, dqkv, dab, dcst, d=0, row_blk0=1, nb=nbx,
                                       has_do=True, add=False, name="scan_lat_f_bwd")
    dqkv, dab, dcst, ds0_b = _scan_bwd(qkv, p_ab, p["cst"], sall_b, do, s_zero, dqkv, dab, dcst, d=1, row_blk0=1, nb=nbx,
                                       has_do=True, add=True, name="scan_lat_b_bwd")
    no_do = jnp.zeros((8, 128), F32)
    dqkv, dab, dcst, _ = _scan_bwd(qkv, p_ab, p["cst"], sall_cf, no_do, ds0_f, dqkv, dab, dcst, d=0, row_blk0=0, nb=1,
                                   has_do=False, add=False, name="scan_ctx_f_bwd")
    dqkv, dab, dcst, _ = _scan_bwd(qkv, p_ab, p["cst"], sall_cb, no_do, ds0_b, dqkv, dab, dcst, d=1, row_blk0=0, nb=1,
                                   has_do=False, add=True, name="scan_ctx_b_bwd")
    dpqkv, dconv = _prep_bwd(p_qkv, dqkv, p["conv"], name="prep_bwd")
    dx1, dp, acc_mix = _mix_in_bwd(dx2, x1, dpqkv, dgate, dpool_in, dbr, dab, modv, p["norm2"], p["w_mix"], name="mix_in_bwd")
    g["w_mix"] = _matmul_tn(u, dp, tmm=256, tn=NMIXP, tk=256, name="wmix_grad")
    dxc, a1, df1, dgu1, acc1 = _ffn_bwd(dx1, xc, gu1, f1, modv, p["norm1"], p["w1_in"], p["w1_out"], mrow=0, dxo_off=0,
                                        name="ffn1_bwd")
    g["w1_out"] = _matmul_tn(a1, df1, tmm=FF // 2, tn=D, tk=256, name="ffn1_wout_grad").reshape(NCHIP, FF // NCHIP, D)
    g["w1_in"] = _matmul_tn(h1, dgu1, tmm=512, tn=2 * FF // NCHIP, tk=256, nsplit=NCHIP, name="ffn1_win_grad")

    small = dict(norm1=acc1[0, 3] + acc1[1, 3], norm2=acc_mix[0, 3] + acc_mix[1, 3], norm3=acc3[0, 3], fnorm=acc_fin[0],
                 gnw=acc_m[1, :HD], pscale=acc_m[2, :NPOOL], pool_w=dpw, conv=dconv[:5],
                 a_log=dcst[0, 2 * NH:4 * NH], dt_bias=dcst[1, 2 * NH:4 * NH])
    zero = jnp.zeros((D,), F32)
    dmod = jnp.stack([
        jnp.stack([acc1[0, 0], acc1[0, 1], acc1[0, 2], acc_mix[0, 0], acc_mix[0, 1], zero, zero, zero, zero]),
        jnp.stack([acc1[1, 0], acc1[1, 1], acc1[1, 2], acc_mix[1, 0], acc_mix[1, 1], acc_m[0], acc3[0, 0], acc3[0, 1], acc3[0, 2]]),
    ])
    return jnp.sum(acc_fin[1]), dxc, g, small, dmod


_HI = lax.Precision.HIGHEST


def _ada_fwd(c_all, w_sh, b_sh, *, name):
    def body(c_ref, w_ref, b_ref, o_ref):
        o_ref[...] = jnp.dot(_silu(c_ref[...]), w_ref[...], precision=_HI, preferred_element_type=F32) + b_ref[...]

    return pl.pallas_call(body, name=name, out_shape=jax.ShapeDtypeStruct((16, w_sh.shape[1]), F32),
                          compiler_params=pltpu.CompilerParams(vmem_limit_bytes=VMEM_LIMIT))(c_all, w_sh, b_sh)


def _ada_bwd(c_all, dm, w_sh, *, name):
    def body(c_ref, dm_ref, w_ref, dw_ref, dc_ref):
        sc = _silu(c_ref[...])
        dw_ref[...] = lax.dot_general(sc, dm_ref[...], (((0,), (0,)), ((), ())), precision=_HI, preferred_element_type=F32)
        part = lax.dot_general(dm_ref[8:9, :], w_ref[...], (((1,), (1,)), ((), ())), precision=_HI, preferred_element_type=F32)
        dc_ref[...] = jnp.broadcast_to(part, dc_ref.shape)

    return pl.pallas_call(body, name=name,
                          out_shape=[jax.ShapeDtypeStruct(w_sh.shape, F32), jax.ShapeDtypeStruct((8, D), F32)],
                          compiler_params=pltpu.CompilerParams(vmem_limit_bytes=VMEM_LIMIT))(c_all, dm, w_sh)


def _cctx_grad(parts, c_ctx, *, name):
    def body(p_ref, c_ref, o_ref):
        tot = (p_ref[0, 0:1, :] + p_ref[2, 0:1, :]) + (p_ref[4, 0:1, :] + p_ref[6, 0:1, :])
        o_ref[...] = tot * _dsilu(c_ref[...])

    return pl.pallas_call(body, name=name, out_shape=jax.ShapeDtypeStruct((1, D), F32))(parts, c_ctx)


_MESH = pl.DeviceIdType.MESH
_ANY = pl.BlockSpec(memory_space=pl.ANY)


def _flip(v, bit):
    return (1 - v) if bit else v


def _all_gather8(x, *, name):
    def body(x_ref, out_ref, send_sems, recv_sems, local_sem):
        mx, my, mc = lax.axis_index("x"), lax.axis_index("y"), lax.axis_index("c")
        me = 4 * mx + 2 * my + mc
        mine = pltpu.make_async_copy(x_ref, out_ref.at[me], local_sem)
        mine.start()
        sends, recvs = [], []
        for k in range(1, 8):
            px, py, pc = _flip(mx, k & 4), _flip(my, k & 2), _flip(mc, k & 1)
            sends.append(pltpu.make_async_remote_copy(src_ref=x_ref, dst_ref=out_ref.at[me], send_sem=send_sems.at[k - 1],
                                                      recv_sem=recv_sems.at[k - 1], device_id=(px, py, pc), device_id_type=_MESH))
            recvs.append(pltpu.make_async_remote_copy(src_ref=x_ref, dst_ref=out_ref.at[4 * px + 2 * py + pc],
                                                      send_sem=send_sems.at[k - 1], recv_sem=recv_sems.at[k - 1],
                                                      device_id=(px, py, pc), device_id_type=_MESH))
        for cp in sends:
            cp.start()
        for cp in recvs:
            cp.wait_recv()
        for cp in sends:
            cp.wait_send()
        mine.wait()

    vm = pl.BlockSpec(memory_space=pltpu.VMEM)
    return pl.pallas_call(
        body, name=name, out_shape=jax.ShapeDtypeStruct((8,) + x.shape, x.dtype), in_specs=[vm], out_specs=vm,
        scratch_shapes=[pltpu.SemaphoreType.DMA((7,)), pltpu.SemaphoreType.DMA((7,)), pltpu.SemaphoreType.DMA],
        compiler_params=pltpu.CompilerParams(vmem_limit_bytes=VMEM_LIMIT),
    )(x)


def _chip_exchange(arrs, *, scatter, name):
    n = len(arrs)

    def body(*refs):
        ins, outs = refs[:n], refs[n:2 * n]
        send_sems, recv_sems, local_sems = refs[2 * n:]
        mx, my, mc = lax.axis_index("x"), lax.axis_index("y"), lax.axis_index("c")
        me = 2 * mx + my
        local, sends, recvs = [], [], []
        for j in range(n):
            src_own = ins[j].at[me] if scatter else ins[j]
            local.append(pltpu.make_async_copy(src_own, outs[j].at[me], local_sems.at[j]))
            for k in range(1, NCHIP):
                px, py = _flip(mx, k & 2), _flip(my, k & 1)
                peer = 2 * px + py
                sem = j * (NCHIP - 1) + k - 1
                src = ins[j].at[peer] if scatter else ins[j]
                sends.append(pltpu.make_async_remote_copy(src_ref=src, dst_ref=outs[j].at[me], send_sem=send_sems.at[sem],
                                                          recv_sem=recv_sems.at[sem], device_id=(px, py, mc), device_id_type=_MESH))
                recvs.append(pltpu.make_async_remote_copy(src_ref=src, dst_ref=outs[j].at[peer], send_sem=send_sems.at[sem],
                                                          recv_sem=recv_sems.at[sem], device_id=(px, py, mc), device_id_type=_MESH))
        for cp in local + sends:
            cp.start()
        for cp in recvs:
            cp.wait_recv()
        for cp in sends:
            cp.wait_send()
        for cp in local:
            cp.wait()

    out_shape = [jax.ShapeDtypeStruct(a.shape if scatter else (NCHIP,) + a.shape, a.dtype) for a in arrs]
    return pl.pallas_call(
        body, name=name, out_shape=out_shape, in_specs=[_ANY] * n, out_specs=[_ANY] * n,
        scratch_shapes=[pltpu.SemaphoreType.DMA((n * (NCHIP - 1),)), pltpu.SemaphoreType.DMA((n * (NCHIP - 1),)),
                        pltpu.SemaphoreType.DMA((n,))],
    )(*arrs)


def _core_swap(arrs, *, name):
    n = len(arrs)

    def body(*refs):
        ins, outs = refs[:n], refs[n:2 * n]
        send_sems, recv_sems = refs[2 * n:]
        sib = (lax.axis_index("x"), lax.axis_index("y"), 1 - lax.axis_index("c"))
        cps = [pltpu.make_async_remote_copy(src_ref=ins[j], dst_ref=outs[j], send_sem=send_sems.at[j], recv_sem=recv_sems.at[j],
                                            device_id=sib, device_id_type=_MESH) for j in range(n)]
        for cp in cps:
            cp.start()
        for cp in cps:
            cp.wait_recv()
        for cp in cps:
            cp.wait_send()

    return pl.pallas_call(
        body, name=name, out_shape=[jax.ShapeDtypeStruct(a.shape, a.dtype) for a in arrs],
        in_specs=[_ANY] * n, out_specs=[_ANY] * n,
        scratch_shapes=[pltpu.SemaphoreType.DMA((n,)), pltpu.SemaphoreType.DMA((n,))],
    )(*arrs)


def _row_tile(rows, cols, budget=1 << 18):
    best = None
    for t in range(8, rows + 1, 8):
        if rows % t == 0 and t * cols <= budget:
            best = t
    return best or rows


def _sum_slots(x, *, name):
    ns, r, c = x.shape
    tr = _row_tile(r, c * ns)

    def body(x_ref, o_ref):
        acc = x_ref[0].astype(F32)
        for s in range(1, ns):
            acc = acc + x_ref[s].astype(F32)
        o_ref[...] = acc

    return pl.pallas_call(
        body, name=name, grid=(r // tr,), out_shape=jax.ShapeDtypeStruct((r, c), F32),
        in_specs=[pl.BlockSpec((ns, tr, c), lambda i: (0, i, 0))], out_specs=pl.BlockSpec((tr, c), lambda i: (i, 0)),
        compiler_params=_cparams("parallel"),
    )(x)


def _adamw(w, ga, gb, m, v, *, name):
    r, c = w.shape
    tr = _row_tile(r, c, budget=1 << 17)
    two = gb is not None

    def body(*refs):
        w_ref, ga_ref = refs[0], refs[1]
        m_ref, v_ref = refs[2 + two], refs[3 + two]
        g_ref, d_ref, mo_ref, vo_ref = refs[4 + two:]
        g = ga_ref[...] + refs[2][...] if two else ga_ref[...]
        mn = ADAM_B1 * m_ref[...] + (1.0 - ADAM_B1) * g
        vn = ADAM_B2 * v_ref[...] + (1.0 - ADAM_B2) * (g * g)
        m_hat = mn / (1.0 - ADAM_B1 ** ADAM_STEP)
        v_hat = vn / (1.0 - ADAM_B2 ** ADAM_STEP)
        g_ref[...] = g
        d_ref[...] = -ADAM_LR * (m_hat / (jnp.sqrt(v_hat) + ADAM_EPS) + ADAM_WD * w_ref[...])
        mo_ref[...] = mn
        vo_ref[...] = vn

    spec = pl.BlockSpec((tr, c), lambda i: (i, 0))
    ins = [w, ga] + ([gb] if two else []) + [m, v]
    return pl.pallas_call(
        body, name=name, grid=(r // tr,), out_shape=[jax.ShapeDtypeStruct((r, c), F32)] * 4,
        in_specs=[spec] * len(ins), out_specs=[spec] * 4, compiler_params=_cparams("parallel"),
    )(*ins)


_MIX_AB0, _MIX_AB1 = NQKV, NQKV + 4 * NH


def _regroup_mix(w):
    pad = jnp.zeros((w.shape[0], NMIXP - NMIX), w.dtype)
    return jnp.concatenate([w[:, :_MIX_AB0], w[:, _MIX_AB1:], w[:, _MIX_AB0:_MIX_AB1], pad], axis=1)


def _ungroup_mix(w):
    n_ab = _MIX_AB1 - _MIX_AB0
    return jnp.concatenate([w[:, :_MIX_AB0], w[:, NMIX - n_ab:NMIX], w[:, _MIX_AB0:NMIX - n_ab]], axis=1)


def _chip_major_cols(w):
    r, c = w.shape
    return w.reshape(r, NCHIP, c // NCHIP).transpose(1, 0, 2)


def _from_chip_major_cols(w):
    return w.transpose(1, 0, 2).reshape(w.shape[1], -1)


_SMALL = (("c_ctx", D), ("b_ada", 9 * D), ("norm1_w", D), ("norm2_w", D), ("norm3_w", D), ("final_norm_w", D),
          ("a_log", 2 * NH), ("dt_bias", 2 * NH), ("gdn_norm_w", HD), ("pool_w", 4 * 128 * 128), ("pool_scale", NPOOL),
          ("conv_w", 5 * NQKV // NCHIP))


def _pack(vals, lanes=128, row_mult=8):
    flat = jnp.concatenate([jnp.ravel(v) for v in vals])
    n = flat.shape[0]
    rows = -(-n // (lanes * row_mult)) * row_mult
    return jnp.pad(flat, (0, rows * lanes - n)).reshape(rows, lanes)


def _unpack(packed, sizes):
    flat = packed.reshape(-1)
    out, o = [], 0
    for n in sizes:
        out.append(flat[o:o + n])
        o += n
    return out


def kernel(x, c, ctx, c_ctx, w_ada, b_ada, norm1_w, ffn1_w_in, ffn1_w_out, norm2_w, w_mix_in, conv_w, a_log, dt_bias, gdn_norm_w, w_gdn_proj, pool_w, pool_scale, w_pool_proj, w_mix_out, norm3_w, ffn2_w_in, ffn2_w_out, final_norm_w, loss_target, m_c_ctx, m_w_ada, m_b_ada, m_norm1_w, m_ffn1_w_in, m_ffn1_w_out, m_norm2_w, m_w_mix_in, m_conv_w, m_a_log, m_dt_bias, m_gdn_norm_w, m_w_gdn_proj, m_pool_w, m_pool_scale, m_w_pool_proj, m_w_mix_out, m_norm3_w, m_ffn2_w_in, m_ffn2_w_out, m_final_norm_w, v_c_ctx, v_w_ada, v_b_ada, v_norm1_w, v_ffn1_w_in, v_ffn1_w_out, v_norm2_w, v_w_mix_in, v_conv_w, v_a_log, v_dt_bias, v_gdn_norm_w, v_w_gdn_proj, v_pool_w, v_pool_scale, v_w_pool_proj, v_w_mix_out, v_norm3_w, v_ffn2_w_in, v_ffn2_w_out, v_final_norm_w):
    names = ("c_ctx", "w_ada", "b_ada", "norm1_w", "ffn1_w_in", "ffn1_w_out", "norm2_w", "w_mix_in", "conv_w", "a_log", "dt_bias",
             "gdn_norm_w", "w_gdn_proj", "pool_w", "pool_scale", "w_pool_proj", "w_mix_out", "norm3_w", "ffn2_w_in", "ffn2_w_out",
             "final_norm_w")
    w = dict(zip(names, (c_ctx, w_ada, b_ada, norm1_w, ffn1_w_in, ffn1_w_out, norm2_w, w_mix_in, conv_w, a_log, dt_bias, gdn_norm_w,
                         w_gdn_proj, pool_w, pool_scale, w_pool_proj, w_mix_out, norm3_w, ffn2_w_in, ffn2_w_out, final_norm_w)))
    mom = dict(zip(names, (m_c_ctx, m_w_ada, m_b_ada, m_norm1_w, m_ffn1_w_in, m_ffn1_w_out, m_norm2_w, m_w_mix_in, m_conv_w, m_a_log,
                           m_dt_bias, m_gdn_norm_w, m_w_gdn_proj, m_pool_w, m_pool_scale, m_w_pool_proj, m_w_mix_out, m_norm3_w,
                           m_ffn2_w_in, m_ffn2_w_out, m_final_norm_w)))
    var = dict(zip(names, (v_c_ctx, v_w_ada, v_b_ada, v_norm1_w, v_ffn1_w_in, v_ffn1_w_out, v_norm2_w, v_w_mix_in, v_conv_w, v_a_log,
                           v_dt_bias, v_gdn_norm_w, v_w_gdn_proj, v_pool_w, v_pool_scale, v_w_pool_proj, v_w_mix_out, v_norm3_w,
                           v_ffn2_w_in, v_ffn2_w_out, v_final_norm_w)))
    mx, my, mc = lax.axis_index("x"), lax.axis_index("y"), lax.axis_index("c")
    chip = 2 * mx + my
    dev = 2 * chip + mc
    ada_cols = w_ada.shape[2]

    c_rows = _all_gather8(jnp.pad(c, ((0, 7), (0, 0))), name="gather_c")[:, 0, :]
    c_all = jnp.concatenate([c_rows, c_ctx[None], jnp.zeros((7, D), F32)], axis=0)
    b_sh = lax.dynamic_slice(b_ada, (0, chip * ada_cols), (1, ada_cols))
    mod_sh = _ada_fwd(c_all, w_ada[0], b_sh, name="ada_fwd")
    mod_parts = _all_gather8(mod_sh, name="gather_mod")
    mod_all = jnp.concatenate([mod_parts[2 * s] for s in range(NCHIP)], axis=1)
    mod_lat = lax.dynamic_index_in_dim(mod_all, dev, axis=0, keepdims=False).reshape(9, D)
    modv = jnp.zeros((2, 16, D), F32).at[0, :9].set(mod_all[8].reshape(9, D)).at[1, :9].set(mod_lat)

    big = ("ffn1_w_in", "ffn1_w_out", "w_mix_in", "w_gdn_proj", "w_pool_proj", "w_mix_out", "ffn2_w_in", "ffn2_w_out")
    gathered = _chip_exchange([w[k][0].astype(BF) for k in big] + [conv_w[0]], scatter=False, name="gather_weights")
    gw = dict(zip(big + ("conv_w",), gathered))
    conv_full = _from_chip_major_cols(gw["conv_w"])
    p = dict(
        norm1=norm1_w, norm2=norm2_w, norm3=norm3_w, fnorm=final_norm_w[None],
        w1_in=gw["ffn1_w_in"], w1_out=gw["ffn1_w_out"].reshape(FF, D), w2_in=gw["ffn2_w_in"], w2_out=gw["ffn2_w_out"].reshape(FF, D),
        w_mix=_regroup_mix(_from_chip_major_cols(gw["w_mix_in"])), conv=jnp.pad(conv_full, ((0, 3), (0, 0))),
        cst=jnp.zeros((8, 128), F32).at[0, 2 * NH:4 * NH].set(jnp.exp(a_log).reshape(-1)).at[1, 2 * NH:4 * NH].set(dt_bias.reshape(-1)),
        gnw=gdn_norm_w, pool_w=pool_w[0], pscale=pool_scale,
        w_gdn=gw["w_gdn_proj"].reshape(D, D), w_pool=_from_chip_major_cols(gw["w_pool_proj"]), w_mo=gw["w_mix_out"].reshape(D, D))

    xc = jnp.concatenate([ctx[0], x[0]], axis=0)
    loss_dev, dxc, g, small, dmod = _local_step(xc, loss_target[0], modv, p)
    loss = lax.psum(loss_dev, ("x", "y", "c"))
    grad_x = dxc[TM:][None]

    g_mix = _chip_major_cols(_ungroup_mix(g["w_mix"]))
    parts = [g["w1_in"], g["w1_out"], g_mix, g["w_gdn"], g["w_pool"], g["w_mo"], g["w2_in"], g["w2_out"]]
    landed = _chip_exchange(parts, scatter=True, name="scatter_grads")
    mine = [_sum_slots(a, name=f"sum_{k}") for k, a in zip(big, landed)]
    theirs = _core_swap(mine, name="swap_grad_sums")

    small_vals = [dmod[1], dmod[0], small["norm1"], small["norm2"], small["norm3"], small["fnorm"], small["a_log"], small["dt_bias"],
                  small["gnw"], small["pool_w"], small["pscale"], small["conv"]]
    small_sizes = [v.size for v in small_vals]
    packed = _all_gather8(_pack(small_vals), name="gather_small")
    tot = _unpack(_sum_slots(packed, name="sum_small"), small_sizes)
    dmod_lat_all = packed[:, :9 * D // 128, :].reshape(8, 9 * D)
    dm = jnp.concatenate([dmod_lat_all, tot[1][None], jnp.zeros((7, 9 * D), F32)], axis=0)
    dm_sh = lax.dynamic_slice(dm, (0, chip * ada_cols), (16, ada_cols))
    g_w_ada, cctx_part = _ada_bwd(c_all, dm_sh, w_ada[0], name="ada_bwd")
    g_c_ctx = _cctx_grad(_all_gather8(cctx_part, name="gather_cctx"), c_ctx[None], name="cctx_grad")[0]
    conv_tot = tot[11].reshape(5, NQKV)
    g_small = dict(c_ctx=g_c_ctx, b_ada=tot[0] + tot[1], norm1_w=tot[2], norm2_w=tot[3], norm3_w=tot[4], final_norm_w=tot[5],
                   a_log=tot[6], dt_bias=tot[7], gdn_norm_w=tot[8], pool_w=tot[9], pool_scale=tot[10],
                   conv_w=lax.dynamic_slice(conv_tot, (0, chip * (NQKV // NCHIP)), (5, NQKV // NCHIP)))

    out = {}
    as2d = lambda a: a.reshape(-1, a.shape[-1])
    for k, ga, gb in zip(big, mine, theirs):
        shp = w[k].shape
        res = _adamw(as2d(w[k]), as2d(ga), as2d(gb), as2d(mom[k]), as2d(var[k]), name=f"adamw_{k}")
        out[k] = [r.reshape(shp) for r in res]
    out["w_ada"] = [r.reshape(w_ada.shape) for r in _adamw(w_ada[0], g_w_ada, None, m_w_ada[0], v_w_ada[0], name="adamw_w_ada")]
    sm_names = [n for n, _ in _SMALL]
    sm_sizes = [n for _, n in _SMALL]
    res = _adamw(_pack([w[k] for k in sm_names]), _pack([g_small[k] for k in sm_names]), None,
                 _pack([mom[k] for k in sm_names]), _pack([var[k] for k in sm_names]), name="adamw_small")
    res = [_unpack(r, sm_sizes) for r in res]
    for i, k in enumerate(sm_names):
        out[k] = [r[i].reshape(w[k].shape) for r in res]
    return (loss, grad_x, *[out[k][0] for k in names], *[out[k][1] for k in names], *[out[k][2] for k in names],
            *[out[k][3] for k in names])
```

```python
import functools

import jax
import jax.numpy as jnp
from jax import lax
from jax.experimental import pallas as pl
from jax.experimental.pallas import tpu as pltpu

F32 = jnp.float32
BF = jnp.bfloat16

D = 1024
FF = 2816
NH = 8
HD = 128
CH = 64
GW = 64
TM = 256
NQKV = 3 * NH * HD
NPOOL = 512
POOL_WINDOWS = (2, 4, 8, 16)
NMIX = 6688
NMIXP = 6784
EPS = 1e-6
NCHIP = 4
VMEM_LIMIT = 56 * 1024 * 1024

ADAM_LR, ADAM_B1, ADAM_B2, ADAM_EPS, ADAM_WD, ADAM_STEP = 0.001, 0.9, 0.999, 1e-08, 0.01, 10


def _cparams(*sem):
    return pltpu.CompilerParams(dimension_semantics=sem, vmem_limit_bytes=VMEM_LIMIT)


def _const_spec(shape):
    nd = len(shape)
    return pl.BlockSpec(shape, lambda *_: (0,) * nd, pipeline_mode=pl.Buffered(1))


def _dot(a, b, dims):
    return lax.dot_general(a.astype(BF), b.astype(BF), (dims, ((), ())), preferred_element_type=F32)


def _nn(a, b):
    return _dot(a, b, ((1,), (0,)))


def _nt(a, b):
    return _dot(a, b, ((1,), (1,)))


def _tn(a, b):
    return _dot(a, b, ((0,), (0,)))


def _silu(x):
    return x * jax.nn.sigmoid(x)


def _dsilu(x):
    s = jax.nn.sigmoid(x)
    return s * (1.0 + x * (1.0 - s))


def _norm_mod(x, nw, shift, scale):
    r = lax.rsqrt(jnp.mean(x * x, axis=-1, keepdims=True) + EPS)
    xh = x * r
    n = xh * nw
    return n * (1.0 + scale) + shift, n, xh, r


def _norm_mod_bwd(dh, n, xh, r, nw, scale):
    dn = dh * (1.0 + scale)
    dxh = dn * nw
    dx = r * (dxh - xh * jnp.mean(dxh * xh, axis=-1, keepdims=True))
    rs = lambda t: jnp.sum(t, axis=0, keepdims=True)
    return dx, rs(dh), rs(dh * n), rs(dn * xh)


def _ffn_fwd(x, modv, nw, w_in4, w_out, *, mrow, name):
    n_tok = x.shape[0]
    nt = n_tok // TM
    nset = modv.shape[0]
    ws = w_in4.shape[2]

    def body(x_ref, mod_ref, nw_ref, win_ref, wout_ref, x1_ref, h_ref, gu_ref, f_ref):
        xv = x_ref[...]
        shift, scale, gate = mod_ref[0, mrow:mrow + 1, :], mod_ref[0, mrow + 1:mrow + 2, :], mod_ref[0, mrow + 2:mrow + 3, :]
        h, _, _, _ = _norm_mod(xv, nw_ref[...], shift, scale)
        hb = h.astype(BF)
        h_ref[...] = hb
        gus = [_nn(hb, win_ref[s]) for s in range(NCHIP)]
        for s in range(NCHIP):
            gu_ref[:, s * ws:(s + 1) * ws] = gus[s].astype(BF)
        g = jnp.concatenate(gus[:2], axis=1)
        u = jnp.concatenate(gus[2:], axis=1)
        f = _nn(_silu(g) * u, wout_ref[...])
        f_ref[...] = f.astype(BF)
        x1_ref[...] = xv + 0.5 * gate * f

    tile = lambda w: pl.BlockSpec((TM, w), lambda i: (i, 0))
    return pl.pallas_call(
        body, name=name, grid=(nt,),
        in_specs=[tile(D), pl.BlockSpec((1, 16, D), lambda i: (jnp.minimum(i, nset - 1), 0, 0)), _const_spec((1, D)),
                  _const_spec(w_in4.shape), _const_spec(w_out.shape)],
        out_specs=[tile(D), tile(D), tile(2 * FF), tile(D)],
        out_shape=[jax.ShapeDtypeStruct((n_tok, D), F32), jax.ShapeDtypeStruct((n_tok, D), BF),
                   jax.ShapeDtypeStruct((n_tok, 2 * FF), BF), jax.ShapeDtypeStruct((n_tok, D), BF)],
        compiler_params=_cparams("parallel"),
    )(x, modv, nw, w_in4, w_out)


def _ffn_bwd(dxo, x, gu, fo, modv, nw, w_in4, w_out, *, mrow, dxo_off, name):
    n_tok = x.shape[0]
    nt = n_tok // TM
    nset = modv.shape[0]
    ws = w_in4.shape[2]

    def body(dxo_ref, x_ref, gu_ref, f_ref, mod_ref, nw_ref, win_ref, wout_ref, dx_ref, a_ref, df_ref, dgu_ref, acc_ref):
        i = pl.program_id(0)
        xv = x_ref[...]
        dxo_v = dxo_ref[...]
        if dxo_off:
            dxo_v = jnp.where(i >= dxo_off, dxo_v, 0.0)
        shift, scale, gate = mod_ref[0, mrow:mrow + 1, :], mod_ref[0, mrow + 1:mrow + 2, :], mod_ref[0, mrow + 2:mrow + 3, :]
        _, n, xh, r = _norm_mod(xv, nw_ref[...], shift, scale)
        df = 0.5 * gate * dxo_v
        dfb = df.astype(BF)
        df_ref[...] = dfb
        dgate = jnp.sum(0.5 * dxo_v * f_ref[...].astype(F32), axis=0, keepdims=True)
        da = _nt(dfb, wout_ref[...])
        g = gu_ref[:, :FF].astype(F32)
        u = gu_ref[:, FF:].astype(F32)
        sg = _silu(g)
        a_ref[...] = (sg * u).astype(BF)
        dgu_ref[:, :FF] = (da * u * _dsilu(g)).astype(BF)
        dgu_ref[:, FF:] = (da * sg).astype(BF)
        dh = _nt(dgu_ref[:, 0:ws], win_ref[0])
        for s in range(1, NCHIP):
            dh = dh + _nt(dgu_ref[:, s * ws:(s + 1) * ws], win_ref[s])
        dx, dshift, dscale, dnw = _norm_mod_bwd(dh, n, xh, r, nw_ref[...], scale)
        dx_ref[...] = dxo_v + dx

        @pl.when((i == 0) | (i == nset - 1))
        def _():
            acc_ref[...] = jnp.zeros_like(acc_ref)

        acc_ref[0, 0:1, :] += dshift
        acc_ref[0, 1:2, :] += dscale
        acc_ref[0, 2:3, :] += dgate
        acc_ref[0, 3:4, :] += dnw

    tile = lambda w: pl.BlockSpec((TM, w), lambda i: (i, 0))
    return pl.pallas_call(
        body, name=name, grid=(nt,),
        in_specs=[pl.BlockSpec((TM, D), lambda i: (jnp.maximum(i - dxo_off, 0), 0)), tile(D), tile(2 * FF), tile(D),
                  pl.BlockSpec((1, 16, D), lambda i: (jnp.minimum(i, nset - 1), 0, 0)), _const_spec((1, D)),
                  _const_spec(w_in4.shape), _const_spec(w_out.shape)],
        out_specs=[tile(D), tile(FF), tile(D), tile(2 * FF),
                   pl.BlockSpec((1, 8, D), lambda i: (jnp.minimum(i, nset - 1), 0, 0))],
        out_shape=[jax.ShapeDtypeStruct((n_tok, D), F32), jax.ShapeDtypeStruct((n_tok, FF), BF),
                   jax.ShapeDtypeStruct((n_tok, D), BF), jax.ShapeDtypeStruct((n_tok, 2 * FF), BF),
                   jax.ShapeDtypeStruct((nset, 8, D), F32)],
        compiler_params=_cparams("arbitrary"),
    )(dxo, x, gu, fo, modv, nw, w_in4, w_out)


def _k_tile(n, target=3072):
    return max(t for t in range(TM, min(n, target) + 1, TM) if n % t == 0)


def _matmul_tn(a, b, *, tmm, tn, tk, nsplit=1, name):
    n_tok, m = a.shape
    kk = b.shape[1]
    nk = n_tok // tk

    def body(a_ref, b_ref, o_ref, acc):
        k = pl.program_id(2)

        @pl.when(k == 0)
        def _():
            acc[...] = jnp.zeros_like(acc)

        acc[...] += _tn(a_ref[...], b_ref[...])

        @pl.when(k == nk - 1)
        def _():
            o_ref[...] = acc[...].astype(BF).reshape(o_ref.shape)

    if nsplit == 1:
        out_shape = jax.ShapeDtypeStruct((m, kk), BF)
        out_spec = pl.BlockSpec((tmm, tn), lambda i, j, k: (i, j))
    else:
        assert tn == kk // nsplit
        out_shape = jax.ShapeDtypeStruct((nsplit, m, tn), BF)
        out_spec = pl.BlockSpec((1, tmm, tn), lambda i, j, k: (j, i, 0))
    return pl.pallas_call(
        body, name=name, grid=(m // tmm, kk // tn, nk),
        in_specs=[pl.BlockSpec((tk, tmm), lambda i, j, k: (k, i)), pl.BlockSpec((tk, tn), lambda i, j, k: (k, j))],
        out_specs=out_spec, out_shape=out_shape,
        scratch_shapes=[pltpu.VMEM((tmm, tn), F32)],
        compiler_params=_cparams("parallel", "parallel", "arbitrary"),
    )(a, b)


_MIX_PARTS = (("qkv", 0, NQKV), ("gate", NQKV, 1024), ("pool", NQKV + 1024, NPOOL), ("br", NQKV + 1024 + NPOOL, 2048),
              ("ab", NMIXP - 128, 128))


def _mix_in_fwd(x1, modv, nw, w_mix, *, name):
    n_tok = x1.shape[0]

    def body(x_ref, mod_ref, nw_ref, w_ref, u_ref, *p_refs):
        u, _, _, _ = _norm_mod(x_ref[...], nw_ref[...], mod_ref[0, 3:4, :], mod_ref[0, 4:5, :])
        ub = u.astype(BF)
        u_ref[...] = ub
        for (_, c0, w), p_ref in zip(_MIX_PARTS, p_refs):
            p_ref[...] = _nn(ub, w_ref[:, c0:c0 + w])

    tile = lambda w: pl.BlockSpec((TM, w), lambda i: (i, 0))
    ctile = lambda w: pl.BlockSpec((TM, w), lambda i: (i + 1, 0))
    return pl.pallas_call(
        body, name=name, grid=(n_tok // TM,),
        in_specs=[tile(D), pl.BlockSpec((1, 16, D), lambda i: (jnp.minimum(i, 1), 0, 0)), _const_spec((1, D)),
                  _const_spec(w_mix.shape)],
        out_specs=[tile(D)] + [tile(w) for _, _, w in _MIX_PARTS],
        out_shape=[jax.ShapeDtypeStruct((n_tok, D), BF)] + [jax.ShapeDtypeStruct((n_tok, w), F32) for _, _, w in _MIX_PARTS],
        compiler_params=_cparams("parallel"),
    )(x1, modv, nw, w_mix)


def _mix_in_bwd(dxo, x1, dqkv, dgate, dpool, dbr, dab_f, dab_b, modv, nw, w_mix, *, name):
    n_tok = x1.shape[0]

    def body(dxo_ref, x_ref, dqkv_ref, dgate_ref, dpool_ref, dbr_ref, dabf_ref, dabb_ref, mod_ref, nw_ref, w_ref,
             dx_ref, dp_ref, acc_ref):
        i = pl.program_id(0)
        lat = i >= 1
        scale = mod_ref[0, 4:5, :]
        _, n, xh, r = _norm_mod(x_ref[...], nw_ref[...], mod_ref[0, 3:4, :], scale)
        dp_ref[:, 0:NQKV] = dqkv_ref[...].astype(BF)
        dp_ref[:, NQKV:NQKV + 1024] = jnp.where(lat, dgate_ref[...], 0.0).astype(BF)
        dp_ref[:, NQKV + 1024:NQKV + 1536] = jnp.where(lat, dpool_ref[...], 0.0).astype(BF)
        dp_ref[:, NQKV + 1536:NMIXP - 128] = jnp.where(lat, dbr_ref[...], 0.0).astype(BF)
        dp_ref[:, NMIXP - 128:] = (dabf_ref[...] + dabb_ref[...]).astype(BF)
        du = _nt(dp_ref[...], w_ref[...])
        dx, dshift, dscale, dnw = _norm_mod_bwd(du, n, xh, r, nw_ref[...], scale)
        dx_ref[...] = jnp.where(lat, dxo_ref[...], 0.0) + dx

        @pl.when(i <= 1)
        def _():
            acc_ref[...] = jnp.zeros_like(acc_ref)

        acc_ref[0, 0:1, :] += dshift
        acc_ref[0, 1:2, :] += dscale
        acc_ref[0, 3:4, :] += dnw

    tile = lambda w: pl.BlockSpec((TM, w), lambda i: (i, 0))
    ltile = lambda w: pl.BlockSpec((TM, w), lambda i: (jnp.maximum(i - 1, 0), 0))
    return pl.pallas_call(
        body, name=name, grid=(n_tok // TM,),
        in_specs=[ltile(D), tile(D), tile(NQKV), ltile(1024), ltile(NPOOL), ltile(2048), tile(128), tile(128),
                  pl.BlockSpec((1, 16, D), lambda i: (jnp.minimum(i, 1), 0, 0)), _const_spec((1, D)), _const_spec(w_mix.shape)],
        out_specs=[tile(D), tile(NMIXP), pl.BlockSpec((1, 8, D), lambda i: (jnp.minimum(i, 1), 0, 0))],
        out_shape=[jax.ShapeDtypeStruct((n_tok, D), F32), jax.ShapeDtypeStruct((n_tok, NMIXP), BF),
                   jax.ShapeDtypeStruct((2, 8, D), F32)],
        compiler_params=_cparams("arbitrary"),
    )(dxo, x1, dqkv, dgate, dpool, dbr, dab_f, dab_b, modv, nw, w_mix)


def _qkv_act(pre, j):
    s = _silu(pre)
    nrm = s * lax.rsqrt(jnp.sum(s * s, axis=-1, keepdims=True) + EPS)
    nrm = nrm * jnp.where(j == 0, HD ** -0.5, 1.0)
    return jnp.where(j < 2, nrm, s)


def _halo_specs(nt):
    r = TM // 8
    w = NH * HD
    main = pl.BlockSpec((TM, w), lambda j, i: (i, j))
    prev = pl.BlockSpec((8, w), lambda j, i: (jnp.maximum(i * r - 1, 0), j))
    nxt = pl.BlockSpec((8, w), lambda j, i: (jnp.minimum((i + 1) * r, nt * r - 1), j))
    return main, prev, nxt


def _prep_fwd(p_qkv, conv_w8, *, name):
    n_tok = p_qkv.shape[0]
    nt = n_tok // TM

    def body(x_ref, xp_ref, xn_ref, w_ref, o_ref, win):
        j, i = pl.program_id(0), pl.program_id(1)
        has_prev = (i != 0) & (i != 1)
        has_next = (i != 0) & (i != nt - 1)
        win[0:8, :] = jnp.where(has_prev, xp_ref[...], 0.0)
        win[8:8 + TM, :] = x_ref[...]
        win[8 + TM:, :] = jnp.where(has_next, xn_ref[...], 0.0)
        for h in range(NH):
            hs = slice(h * HD, (h + 1) * HD)
            pre = win[6:6 + TM, hs] * w_ref[0:1, hs]
            for k in range(1, 5):
                pre = pre + win[6 + k:6 + k + TM, hs] * w_ref[k:k + 1, hs]
            o_ref[:, hs] = _qkv_act(pre, j)

    main, prev, nxt = _halo_specs(nt)
    wq = NH * HD
    return pl.pallas_call(
        body, name=name, grid=(3, nt),
        in_specs=[main, prev, nxt, pl.BlockSpec((8, wq), lambda j, i: (0, j))],
        out_specs=main, out_shape=jax.ShapeDtypeStruct((n_tok, NQKV), F32),
        scratch_shapes=[pltpu.VMEM((TM + 16, wq), F32)],
        compiler_params=_cparams("parallel", "arbitrary"),
    )(p_qkv, p_qkv, p_qkv, conv_w8)


def _prep_bwd(p_qkv, dqkv_f, dqkv_b, conv_w8, *, name):
    n_tok = p_qkv.shape[0]
    nt = n_tok // TM
    wr = TM + 16
    wq = NH * HD

    def body(x_ref, xp_ref, xn_ref, g_ref, gp_ref, gn_ref, g2_ref, g2p_ref, g2n_ref, w_ref, dx_ref, dw_ref, xwin, gwin, dwin):
        j, i = pl.program_id(0), pl.program_id(1)
        has_prev = (i != 0) & (i != 1)
        has_next = (i != 0) & (i != nt - 1)
        z8 = jnp.zeros((8, wq), F32)
        xwin[0:8, :] = z8
        xwin[8:16, :] = jnp.where(has_prev, xp_ref[...], 0.0)
        xwin[16:16 + TM, :] = x_ref[...]
        xwin[16 + TM:24 + TM, :] = jnp.where(has_next, xn_ref[...], 0.0)
        xwin[24 + TM:, :] = z8
        gwin[0:8, :] = jnp.where(has_prev, gp_ref[...] + g2p_ref[...], 0.0)
        gwin[8:8 + TM, :] = g_ref[...] + g2_ref[...]
        gwin[8 + TM:, :] = jnp.where(has_next, gn_ref[...] + g2n_ref[...], 0.0)

        @pl.when(i == 0)
        def _():
            dw_ref[...] = jnp.zeros_like(dw_ref)

        for h in range(NH):
            hs = slice(h * HD, (h + 1) * HD)
            pre = xwin[6:6 + wr, hs] * w_ref[0:1, hs]
            for k in range(1, 5):
                pre = pre + xwin[6 + k:6 + k + wr, hs] * w_ref[k:k + 1, hs]
            _, vjp = jax.vjp(lambda t: _qkv_act(t, j), pre)
            dwin[:, hs] = vjp(gwin[:, hs])[0]
            dx = dwin[10:10 + TM, hs] * w_ref[0:1, hs]
            for k in range(1, 5):
                dx = dx + dwin[10 - k:10 - k + TM, hs] * w_ref[k:k + 1, hs]
            dx_ref[:, hs] = dx
            dmid = dwin[8:8 + TM, hs]
            for k in range(5):
                dw_ref[k:k + 1, hs] += jnp.sum(dmid * xwin[14 + k:14 + k + TM, hs], axis=0, keepdims=True)

    main, prev, nxt = _halo_specs(nt)
    wq = NH * HD
    wspec = pl.BlockSpec((8, wq), lambda j, i: (0, j))
    return pl.pallas_call(
        body, name=name, grid=(3, nt),
        in_specs=[main, prev, nxt, main, prev, nxt, main, prev, nxt, wspec],
        out_specs=[main, wspec],
        out_shape=[jax.ShapeDtypeStruct((n_tok, NQKV), F32), jax.ShapeDtypeStruct((8, NQKV), F32)],
        scratch_shapes=[pltpu.VMEM((TM + 32, wq), F32), pltpu.VMEM((TM + 16, wq), F32), pltpu.VMEM((TM + 16, wq), F32)],
        compiler_params=_cparams("parallel", "arbitrary"),
    )(p_qkv, p_qkv, p_qkv, dqkv_f, dqkv_f, dqkv_f, dqkv_b, dqkv_b, dqkv_b, conv_w8)


@jax.custom_vjp
def _mm_nn(a, b):
    return _nn(a, b)


@jax.custom_vjp
def _mm_nt(a, b):
    return _nt(a, b)


@jax.custom_vjp
def _mm_tn(a, b):
    return _tn(a, b)


_mm_nn.defvjp(lambda a, b: (_nn(a, b), (a, b)), lambda r, g: (_mm_nt(g, r[1]), _mm_tn(r[0], g)))
_mm_nt.defvjp(lambda a, b: (_nt(a, b), (a, b)), lambda r, g: (_mm_nn(g, r[1]), _mm_tn(g, r[0])))
_mm_tn.defvjp(lambda a, b: (_tn(a, b), (a, b)), lambda r, g: (_mm_nt(r[1], g), _mm_nn(r[0], g)))


def _each(f, *lists):
    return tuple(f(*a) for a in zip(*lists))


def _unit_tri_inv(ls, revs):
    ii = lax.broadcasted_iota(jnp.int32, (CH, CH), 0)
    jj = lax.broadcasted_iota(jnp.int32, (CH, CH), 1)
    eye = (ii == jj).astype(F32)
    xs = None
    s = 1
    while s < CH:
        same = (ii & -(2 * s)) == (jj & -(2 * s))
        off = {False: same & ((ii & s) != 0) & ((jj & s) == 0), True: same & ((jj & s) != 0) & ((ii & s) == 0)}
        cs = _each(lambda l, r: jnp.where(off[r], l, 0.0), ls, revs)
        if xs is None:
            xs = _each(lambda c: eye - c, cs)
        else:
            xc = _each(_nn, xs, cs)
            xcx = _each(_nn, xc, xs)
            xs = _each(lambda x, t: x - t, xs, xcx)
        s *= 2
    return xs


@functools.lru_cache(maxsize=None)
def _tri_solve(revs):
    @jax.custom_vjp
    def solve(ls, rhss):
        return _each(_mm_nn, _unit_tri_inv(ls, revs), rhss)

    def fwd(ls, rhss):
        ainv = _unit_tri_inv(ls, revs)
        xs = _each(_mm_nn, ainv, rhss)
        return xs, (ainv, xs)

    def bwd(res, gs):
        ainv, xs = res
        drhs = _each(_mm_tn, ainv, gs)
        return _each(lambda d, x: -_mm_nt(d, x), drhs, xs), drhs

    solve.defvjp(fwd, bwd)
    return solve


def _chunk_step(q, k, v, beta, g, s, *, revs):
    ii = lax.broadcasted_iota(jnp.int32, (CH, CH), 0)
    jj = lax.broadcasted_iota(jnp.int32, (CH, CH), 1)
    eye = ii == jj
    incl_of = {False: ii >= jj, True: ii <= jj}
    strict_of = {False: ii > jj, True: ii < jj}
    g_row = _each(lambda t: jnp.sum(jnp.where(eye, t, 0.0), axis=0, keepdims=True), g)
    cum = _each(lambda t, r: jnp.sum(jnp.where(incl_of[r], t, 0.0), axis=1, keepdims=True), g_row, revs)
    cum_row = _each(lambda t: jnp.sum(jnp.where(eye, t, 0.0), axis=0, keepdims=True), cum)
    total = _each(lambda t: jnp.sum(t, axis=0, keepdims=True), g)
    decay = _each(lambda c, cr, r: jnp.where(incl_of[r], jnp.exp(jnp.where(incl_of[r], c - cr, 0.0)), 0.0), cum, cum_row, revs)
    kb = _each(jnp.multiply, k, beta)
    vb = _each(jnp.multiply, v, beta)
    kk = _each(_mm_nt, kb, k)
    lmat = _each(lambda t, dc, r: jnp.where(strict_of[r], t * dc, 0.0), kk, decay, revs)
    ecum = _each(jnp.exp, cum)
    rhs = _each(lambda a, b, e: jnp.concatenate([a, b * e], axis=1), vb, kb, ecum)
    sol = _tri_solve(revs)(lmat, rhs)
    qk = _each(_mm_nt, q, k)
    aqk = _each(jnp.multiply, qk, decay)
    ws = _each(lambda so, st: _mm_nn(so[:, HD:], st), sol, s)
    v_new = _each(lambda so, t: so[:, :HD] - t, sol, ws)
    qs = _each(lambda a, e, st: _mm_nn(a * e, st), q, ecum, s)
    av = _each(_mm_nn, aqk, v_new)
    o = _each(jnp.add, qs, av)
    kv = _each(lambda a, t, c, vn: _mm_tn(a * jnp.exp(t - c), vn), k, total, cum, v_new)
    s_new = _each(lambda st, t, u: st * jnp.exp(t) + u, s, total, kv)
    return o, s_new


def _lane_col(x, c):
    lane = lax.broadcasted_iota(jnp.int32, x.shape, 1)
    return jnp.sum(jnp.where(lane == c, x, 0.0), axis=1, keepdims=True)


def _beta_g(ab, cst, d, h):
    braw = _lane_col(ab, NH * d + h)
    araw = _lane_col(ab, 2 * NH + NH * d + h)
    ea = _lane_col(cst[0:1, :], 2 * NH + NH * d + h)
    dt = _lane_col(cst[1:2, :], 2 * NH + NH * d + h)
    z = araw + dt
    softplus = jnp.maximum(z, 0.0) + jnp.log(1.0 + jnp.exp(-jnp.abs(z)))
    return jax.nn.sigmoid(braw), -ea * softplus, z, ea


_REVS = (False,) * NH + (True,) * NH


def _chain_inputs(refs, r0s, abvs, cst):
    hs = lambda h: slice(h * HD, (h + 1) * HD)
    chains = tuple((d, h) for d in (0, 1) for h in range(NH))
    q = _each(lambda dh: refs[dh[0]][0][pl.ds(r0s[dh[0]], CH), hs(dh[1])], chains)
    k = _each(lambda dh: refs[dh[0]][1][pl.ds(r0s[dh[0]], CH), hs(dh[1])], chains)
    v = _each(lambda dh: refs[dh[0]][2][pl.ds(r0s[dh[0]], CH), hs(dh[1])], chains)
    bg = _each(lambda dh: _beta_g(abvs[dh[0]], cst, dh[0], dh[1]), chains)
    return chains, q, k, v, bg


def _scan_fwd(qkv, ab, cst, s0, *, row_blk0, nb, name):
    cb = TM // CH
    w = NH * HD

    def body(qf, kf, vf, abf, qb, kb, vb, abb, cst_ref, s0_ref, of_ref, ob_ref, sallf_ref, sallb_ref, sfin_ref, s_scr):
        i = pl.program_id(0)

        @pl.when(i == 0)
        def _():
            s_scr[...] = s0_ref[...]

        o_refs, sall_refs = (of_ref, ob_ref), (sallf_ref, sallb_ref)

        def chunk(ci, carry):
            cs = (ci, cb - 1 - ci)
            r0s = tuple(pl.multiple_of(c * CH, CH) for c in cs)
            abvs = (abf[pl.ds(r0s[0], CH), :], abb[pl.ds(r0s[1], CH), :])
            chains, q, k, v, bg = _chain_inputs(((qf, kf, vf), (qb, kb, vb)), r0s, abvs, cst_ref[...])
            s = _each(lambda dh: s_scr[dh[0], dh[1]], chains)
            for (d, h), sv in zip(chains, s):
                sall_refs[d][cs[d], h] = sv
            o, s_new = _chunk_step(q, k, v, _each(lambda t: t[0], bg), _each(lambda t: t[1], bg), s, revs=_REVS)
            for (d, h), ov, sv in zip(chains, o, s_new):
                o_refs[d][pl.ds(r0s[d], CH), h * HD:(h + 1) * HD] = ov
                s_scr[d, h] = sv
            return carry

        lax.fori_loop(0, cb, chunk, 0)

        @pl.when(i == nb - 1)
        def _():
            sfin_ref[...] = s_scr[...]

    pos = (lambda i: i, lambda i: nb - 1 - i)
    col = lambda d, c: pl.BlockSpec((TM, w), lambda i: (row_blk0 + pos[d](i), c))
    abs_ = lambda d: pl.BlockSpec((TM, 128), lambda i: (row_blk0 + pos[d](i), 0))
    full4 = pl.BlockSpec((2, NH, HD, HD), lambda i: (0, 0, 0, 0))
    o_spec = lambda d: pl.BlockSpec((TM, w), lambda i: (pos[d](i), 0))
    sall_spec = lambda d: pl.BlockSpec((cb, NH, HD, HD), lambda i: (pos[d](i), 0, 0, 0))
    return pl.pallas_call(
        body, name=name, grid=(nb,),
        in_specs=[col(0, 0), col(0, 1), col(0, 2), abs_(0), col(1, 0), col(1, 1), col(1, 2), abs_(1),
                  pl.BlockSpec((8, 128), lambda i: (0, 0)), full4],
        out_specs=[o_spec(0), o_spec(1), sall_spec(0), sall_spec(1), full4],
        out_shape=[jax.ShapeDtypeStruct((nb * TM, w), F32)] * 2 + [jax.ShapeDtypeStruct((nb * cb, NH, HD, HD), F32)] * 2
        + [jax.ShapeDtypeStruct((2, NH, HD, HD), F32)],
        scratch_shapes=[pltpu.VMEM((2, NH, HD, HD), F32)],
        compiler_params=_cparams("arbitrary"),
    )(qkv, qkv, qkv, ab, qkv, qkv, qkv, ab, cst, s0)


def _scan_bwd(qkv, ab, cst, sall_f, sall_b, do, dsfin, dqkv_f, dqkv_b, dab_f, dab_b, dcst, *, row_blk0, nb, has_do, name):
    cb = TM // CH
    w = NH * HD

    def body(qf, kf, vf, abf, qb, kb, vb, abb, cst_ref, sallf_ref, sallb_ref, dof_ref, dob_ref, dsfin_ref, _f, _b, _af, _ab, dcst_in,
             dqkvf_ref, dqkvb_ref, dabf_ref, dabb_ref, dcst_ref, ds0_ref, ds_scr):
        i = pl.program_id(0)

        @pl.when(i == 0)
        def _():
            ds_scr[...] = dsfin_ref[...]
            dcst_ref[...] = dcst_in[...]

        lane = lax.broadcasted_iota(jnp.int32, (CH, 128), 1)
        lane1 = lax.broadcasted_iota(jnp.int32, (1, 128), 1)
        sall_refs, do_refs = (sallf_ref, sallb_ref), (dof_ref, dob_ref)
        dqkv_refs, dab_refs = (dqkvf_ref, dqkvb_ref), (dabf_ref, dabb_ref)

        def chunk(ci, carry):
            cs = (cb - 1 - ci, ci)
            r0s = tuple(pl.multiple_of(c * CH, CH) for c in cs)
            abvs = (abf[pl.ds(r0s[0], CH), :], abb[pl.ds(r0s[1], CH), :])
            chains, q, k, v, bg = _chain_inputs(((qf, kf, vf), (qb, kb, vb)), r0s, abvs, cst_ref[...])
            beta, g = _each(lambda t: t[0], bg), _each(lambda t: t[1], bg)
            s = _each(lambda dh: sall_refs[dh[0]][cs[dh[0]], dh[1]], chains)
            _, vjp = jax.vjp(functools.partial(_chunk_step, revs=_REVS), q, k, v, beta, g, s)
            do_all = _each(lambda dh: do_refs[dh[0]][pl.ds(r0s[dh[0]], CH), dh[1] * HD:(dh[1] + 1) * HD] if has_do
                           else jnp.zeros((CH, HD), F32), chains)
            dq, dk, dv, dbeta, dg, ds = vjp((do_all, _each(lambda dh: ds_scr[dh[0], dh[1]], chains)))
            dab = [jnp.zeros((CH, 128), F32), jnp.zeros((CH, 128), F32)]
            dal = jnp.zeros((1, 128), F32)
            for n, (d, h) in enumerate(chains):
                for part, val in enumerate((dq[n], dk[n], dv[n])):
                    dqkv_refs[d][pl.ds(r0s[d], CH), part * w + h * HD:part * w + (h + 1) * HD] = val
                ds_scr[d, h] = ds[n]
                z, ea = bg[n][2], bg[n][3]
                dbraw = dbeta[n] * beta[n] * (1.0 - beta[n])
                daraw = dg[n] * (-ea) * jax.nn.sigmoid(z)
                dab[d] = dab[d] + jnp.where(lane == NH * d + h, dbraw, 0.0) + jnp.where(lane == 2 * NH + NH * d + h, daraw, 0.0)
                dal = dal + jnp.where(lane1 == 2 * NH + NH * d + h, jnp.sum(dg[n] * g[n], axis=0, keepdims=True), 0.0)
            for d in (0, 1):
                dab_refs[d][pl.ds(r0s[d], CH), :] = dab[d]
            dcst_ref[0:1, :] += dal
            dcst_ref[1:2, :] += jnp.sum(jnp.where(lane >= 2 * NH, dab[0] + dab[1], 0.0), axis=0, keepdims=True)
            return carry

        lax.fori_loop(0, cb, chunk, 0)

        @pl.when(i == nb - 1)
        def _():
            ds0_ref[...] = ds_scr[...]

    pos = (lambda i: nb - 1 - i, lambda i: i)
    col = lambda d, c: pl.BlockSpec((TM, w), lambda i: (row_blk0 + pos[d](i), c))
    abs_ = lambda d: pl.BlockSpec((TM, 128), lambda i: (row_blk0 + pos[d](i), 0))
    full4 = pl.BlockSpec((2, NH, HD, HD), lambda i: (0, 0, 0, 0))
    small = pl.BlockSpec((8, 128), lambda i: (0, 0))
    hbm = pl.BlockSpec(memory_space=pl.ANY)
    sall_spec = lambda d: pl.BlockSpec((cb, NH, HD, HD), lambda i: (pos[d](i), 0, 0, 0))
    do_spec = (lambda d: pl.BlockSpec((TM, w), lambda i: (pos[d](i), 0))) if has_do else (lambda d: small)
    acc_specs = [pl.BlockSpec((TM, 3 * w), lambda i: (row_blk0 + pos[0](i), 0)),
                 pl.BlockSpec((TM, 3 * w), lambda i: (row_blk0 + pos[1](i), 0)), abs_(0), abs_(1), small]
    return pl.pallas_call(
        body, name=name, grid=(nb,),
        in_specs=[col(0, 0), col(0, 1), col(0, 2), abs_(0), col(1, 0), col(1, 1), col(1, 2), abs_(1), small,
                  sall_spec(0), sall_spec(1), do_spec(0), do_spec(1), full4, hbm, hbm, hbm, hbm, small],
        out_specs=acc_specs + [full4],
        out_shape=[jax.ShapeDtypeStruct(dqkv_f.shape, F32), jax.ShapeDtypeStruct(dqkv_b.shape, F32),
                   jax.ShapeDtypeStruct(dab_f.shape, F32), jax.ShapeDtypeStruct(dab_b.shape, F32),
                   jax.ShapeDtypeStruct((8, 128), F32), jax.ShapeDtypeStruct((2, NH, HD, HD), F32)],
        input_output_aliases={14: 0, 15: 1, 16: 2, 17: 3, 18: 4},
        scratch_shapes=[pltpu.VMEM((2, NH, HD, HD), F32)],
        compiler_params=_cparams("arbitrary"),
    )(qkv, qkv, qkv, ab, qkv, qkv, qkv, ab, cst, sall_f, sall_b, do, do, dsfin, dqkv_f, dqkv_b, dab_f, dab_b, dcst)


def _pool(xin, *, row0, transpose, name):
    n_tok = xin.shape[0] - row0
    rows = n_tok // GW
    pad = 8 * GW
    tt = 512
    gsh = GW.bit_length() - 1

    def body(x_ref, o_ref, ybuf):
        ii = lax.broadcasted_iota(jnp.int32, (128, 128), 0)
        jj = lax.broadcasted_iota(jnp.int32, (128, 128), 1)
        same_row = (ii >> gsh) == (jj >> gsh)
        ci, cj = ii & (GW - 1), jj & (GW - 1)
        tok = lax.broadcasted_iota(jnp.int32, (tt, 1), 0)
        zpad = jnp.zeros((pad, 128), F32)
        for gi, wdw in enumerate(POOL_WINDOWS):
            lo, hi = wdw // 2, wdw - wdw // 2
            if transpose:
                band = same_row & (ci - cj >= -lo) & (ci - cj < hi)
                offs = range(-hi + 1, lo + 1)
            else:
                band = same_row & (cj - ci >= -lo) & (cj - ci < hi)
                offs = range(-lo, hi)
            bandm = band.astype(BF)
            cs = slice(gi * 128, (gi + 1) * 128)
            ybuf[0:pad, :] = zpad
            ybuf[pad + n_tok:, :] = zpad

            def inv_area(t0):
                t = t0 + tok
                r, c = t >> gsh, t & (GW - 1)
                nr = jnp.minimum(r + hi, rows) - jnp.maximum(r - lo, 0)
                nc = jnp.minimum(c + hi, GW) - jnp.maximum(c - lo, 0)
                return 1.0 / (nr * nc).astype(F32)

            def col_pass(b, carry):
                t0 = pl.multiple_of(b * tt, tt)
                xv = x_ref[pl.ds(row0 + t0, tt), cs]
                if transpose:
                    xv = xv * inv_area(t0)
                hi_part = xv.astype(BF)
                lo_part = (xv - hi_part.astype(F32)).astype(BF)
                for s in range(tt // 128):
                    sl = slice(s * 128, (s + 1) * 128)
                    y = (jnp.dot(bandm, hi_part[sl], preferred_element_type=F32)
                         + jnp.dot(bandm, lo_part[sl], preferred_element_type=F32))
                    ybuf[pl.ds(pad + t0 + s * 128, 128), :] = y
                return carry

            lax.fori_loop(0, n_tok // tt, col_pass, 0)

            def row_pass(b, carry):
                t0 = pl.multiple_of(b * tt, tt)
                acc = ybuf[pl.ds(pad + t0 + offs[0] * GW, tt), :]
                for dr in offs[1:]:
                    acc = acc + ybuf[pl.ds(pad + t0 + dr * GW, tt), :]
                xv = x_ref[pl.ds(row0 + t0, tt), cs]
                if not transpose:
                    acc = acc * inv_area(t0)
                o_ref[pl.ds(t0, tt), cs] = acc - xv
                return carry

            lax.fori_loop(0, n_tok // tt, row_pass, 0)

    return pl.pallas_call(
        body, name=name, out_shape=jax.ShapeDtypeStruct((n_tok, NPOOL), F32),
        in_specs=[pl.BlockSpec(memory_space=pltpu.VMEM)], out_specs=pl.BlockSpec(memory_space=pltpu.VMEM),
        scratch_shapes=[pltpu.VMEM((n_tok + 2 * pad, 128), F32)],
        compiler_params=pltpu.CompilerParams(vmem_limit_bytes=VMEM_LIMIT),
    )(xin)


def _merge_parts(of, ob, pgate, pd, br, gnw, pw_ref, pscale, wg_ref, wp_ref):
    o = of + ob
    ons, ohs, rs = [], [], []
    for h in range(NH):
        oh = o[:, h * HD:(h + 1) * HD]
        r = lax.rsqrt(jnp.mean(oh * oh, axis=-1, keepdims=True) + EPS)
        ohs.append(oh * r)
        rs.append(r)
        ons.append(oh * r * gnw)
    on = jnp.concatenate(ons, axis=1)
    og = on * _silu(pgate)
    y_gdn = _nn(og, wg_ref[...])
    ypre = jnp.concatenate([_nn(pd[:, g * 128:(g + 1) * 128], pw_ref[g]) for g in range(4)], axis=1)
    yp = ypre * pscale
    y_pool = _nn(yp, wp_ref[...])
    g_pool = jax.nn.sigmoid(br[:, :D])
    g_gdn = jax.nn.sigmoid(br[:, D:])
    return dict(on=on, ohs=ohs, rs=rs, og=og, y_gdn=y_gdn, ypre=ypre, yp=yp, y_pool=y_pool, g_pool=g_pool, g_gdn=g_gdn)


def _merge_fwd(x1, of, ob, pgate, pd, br, modv, gnw, pool_w, pscale, w_gdn, w_pool, w_mo, *, name):
    n_tok = of.shape[0]

    def body(x_ref, of_ref, ob_ref, pg_ref, pd_ref, br_ref, mod_ref, gnw_ref, pw_ref, ps_ref, wg_ref, wp_ref, wmo_ref,
             x2_ref, og_ref, yp_ref, m_ref, mix_ref):
        t = _merge_parts(of_ref[...], ob_ref[...], pg_ref[...], pd_ref[...], br_ref[...], gnw_ref[...], pw_ref, ps_ref[...],
                         wg_ref, wp_ref)
        m = t["g_pool"] * t["y_pool"] + t["g_gdn"] * t["y_gdn"]
        mix = _nn(m, wmo_ref[...])
        og_ref[...] = t["og"].astype(BF)
        yp_ref[...] = t["yp"].astype(BF)
        m_ref[...] = m.astype(BF)
        mix_ref[...] = mix.astype(BF)
        x2_ref[...] = x_ref[...] + mod_ref[0, 5:6, :] * mix

    tile = lambda w: pl.BlockSpec((TM, w), lambda i: (i, 0))
    ctile = lambda w: pl.BlockSpec((TM, w), lambda i: (i + 1, 0))
    return pl.pallas_call(
        body, name=name, grid=(n_tok // TM,),
        in_specs=[ctile(D), tile(D), tile(D), ctile(D), tile(NPOOL), ctile(2 * D),
                  pl.BlockSpec((1, 16, D), lambda i: (1, 0, 0)), _const_spec((1, HD)), _const_spec((4, 128, 128)),
                  _const_spec((1, NPOOL)), _const_spec((D, D)), _const_spec((NPOOL, D)), _const_spec((D, D))],
        out_specs=[tile(D), tile(D), tile(NPOOL), tile(D), tile(D)],
        out_shape=[jax.ShapeDtypeStruct((n_tok, D), F32), jax.ShapeDtypeStruct((n_tok, D), BF),
                   jax.ShapeDtypeStruct((n_tok, NPOOL), BF), jax.ShapeDtypeStruct((n_tok, D), BF),
                   jax.ShapeDtypeStruct((n_tok, D), BF)],
        compiler_params=_cparams("parallel"),
    )(x1, of, ob, pgate, pd, br, modv, gnw, pool_w, pscale, w_gdn, w_pool, w_mo)


def _merge_bwd(dx2, mix, of, ob, pgate, pd, br, modv, gnw, pool_w, pscale, w_gdn, w_pool, w_mo, *, name):
    n_tok = of.shape[0]

    def body(dx2_ref, mix_ref, of_ref, ob_ref, pg_ref, pd_ref, br_ref, mod_ref, gnw_ref, pw_ref, ps_ref, wg_ref, wp_ref, wmo_ref,
             do_ref, dgate_ref, dpd_ref, dbr_ref, dmix_ref, dyg_ref, dyp_ref, acc_ref, dpw_ref):
        i = pl.program_id(0)
        pgate, pdv, gnw = pg_ref[...], pd_ref[...], gnw_ref[...]
        t = _merge_parts(of_ref[...], ob_ref[...], pgate, pdv, br_ref[...], gnw, pw_ref, ps_ref[...], wg_ref, wp_ref)
        dx2v = dx2_ref[...]
        dmix = mod_ref[0, 5:6, :] * dx2v
        dmixb = dmix.astype(BF)
        dmix_ref[...] = dmixb
        dm = _nt(dmixb, wmo_ref[...])
        gp, gg = t["g_pool"], t["g_gdn"]
        dbr_ref[:, :D] = dm * t["y_pool"] * gp * (1.0 - gp)
        dbr_ref[:, D:] = dm * t["y_gdn"] * gg * (1.0 - gg)
        dyp = (dm * gp).astype(BF)
        dyg = (dm * gg).astype(BF)
        dyp_ref[...] = dyp
        dyg_ref[...] = dyg
        dyp_in = _nt(dyp, wp_ref[...])
        dypre = dyp_in * ps_ref[...]
        for g in range(4):
            gs = slice(g * 128, (g + 1) * 128)
            dpd_ref[:, gs] = _nt(dypre[:, gs], pw_ref[g])
        dog = _nt(dyg, wg_ref[...])
        dgate_ref[...] = dog * t["on"] * _dsilu(pgate)
        don = dog * _silu(pgate)
        dgnw = jnp.zeros((1, HD), F32)
        for h in range(NH):
            hs = slice(h * HD, (h + 1) * HD)
            donh, oh, r = don[:, hs], t["ohs"][h], t["rs"][h]
            dgnw = dgnw + jnp.sum(donh * oh, axis=0, keepdims=True)
            doh = donh * gnw
            do_ref[:, hs] = r * (doh - oh * jnp.mean(doh * oh, axis=-1, keepdims=True))

        @pl.when(i == 0)
        def _():
            acc_ref[...] = jnp.zeros_like(acc_ref)
            dpw_ref[...] = jnp.zeros_like(dpw_ref)

        acc_ref[0:1, :] += jnp.sum(dx2v * mix_ref[...].astype(F32), axis=0, keepdims=True)
        acc_ref[1:2, 0:HD] += dgnw
        acc_ref[2:3, 0:NPOOL] += jnp.sum(dyp_in * t["ypre"], axis=0, keepdims=True)
        for g in range(4):
            gs = slice(g * 128, (g + 1) * 128)
            dpw_ref[g] += _tn(pdv[:, gs], dypre[:, gs])

    tile = lambda w: pl.BlockSpec((TM, w), lambda i: (i, 0))
    ctile = lambda w: pl.BlockSpec((TM, w), lambda i: (i + 1, 0))
    return pl.pallas_call(
        body, name=name, grid=(n_tok // TM,),
        in_specs=[tile(D), tile(D), tile(D), tile(D), ctile(D), tile(NPOOL), ctile(2 * D),
                  pl.BlockSpec((1, 16, D), lambda i: (1, 0, 0)), _const_spec((1, HD)), _const_spec((4, 128, 128)),
                  _const_spec((1, NPOOL)), _const_spec((D, D)), _const_spec((NPOOL, D)), _const_spec((D, D))],
        out_specs=[tile(D), tile(D), tile(NPOOL), tile(2 * D), tile(D), tile(D), tile(D),
                   pl.BlockSpec((8, D), lambda i: (0, 0)), pl.BlockSpec((4, 128, 128), lambda i: (0, 0, 0))],
        out_shape=[jax.ShapeDtypeStruct((n_tok, D), F32), jax.ShapeDtypeStruct((n_tok, D), F32),
                   jax.ShapeDtypeStruct((n_tok, NPOOL), F32), jax.ShapeDtypeStruct((n_tok, 2 * D), F32),
                   jax.ShapeDtypeStruct((n_tok, D), BF), jax.ShapeDtypeStruct((n_tok, D), BF), jax.ShapeDtypeStruct((n_tok, D), BF),
                   jax.ShapeDtypeStruct((8, D), F32), jax.ShapeDtypeStruct((4, 128, 128), F32)],
        compiler_params=_cparams("arbitrary"),
    )(dx2, mix, of, ob, pgate, pd, br, modv, gnw, pool_w, pscale, w_gdn, w_pool, w_mo)


def _final(x3, target, fnw, *, name):
    n_tok = x3.shape[0]

    def body(x_ref, t_ref, w_ref, dx_ref, acc_ref):
        xv, w = x_ref[...], w_ref[...]
        r = lax.rsqrt(jnp.mean(xv * xv, axis=-1, keepdims=True) + EPS)
        xh = xv * r
        err = xh * w - t_ref[...]
        dy = err * (1.0 / D)
        dxh = dy * w
        dx_ref[...] = r * (dxh - xh * jnp.mean(dxh * xh, axis=-1, keepdims=True))

        @pl.when(pl.program_id(0) == 0)
        def _():
            acc_ref[...] = jnp.zeros_like(acc_ref)

        acc_ref[0:1, :] += jnp.sum(dy * xh, axis=0, keepdims=True)
        acc_ref[1:2, :] += jnp.sum(err * err, axis=0, keepdims=True) * (0.5 / D)

    tile = pl.BlockSpec((TM, D), lambda i: (i, 0))
    return pl.pallas_call(
        body, name=name, grid=(n_tok // TM,),
        in_specs=[tile, tile, _const_spec((1, D))],
        out_specs=[tile, pl.BlockSpec((8, D), lambda i: (0, 0))],
        out_shape=[jax.ShapeDtypeStruct((n_tok, D), F32), jax.ShapeDtypeStruct((8, D), F32)],
        compiler_params=_cparams("arbitrary"),
    )(x3, target, fnw)


def _local_step(xc, target, modv, p):
    n_all = xc.shape[0]
    t_lat = n_all - TM
    nbx = t_lat // TM
    mod_lat = modv[1:2]

    x1, h1, gu1, f1 = _ffn_fwd(xc, modv, p["norm1"], p["w1_in"], p["w1_out"], mrow=0, name="ffn1_fwd")
    u, p_qkv, p_gate, p_pool, p_br, p_ab = _mix_in_fwd(x1, modv, p["norm2"], p["w_mix"], name="mix_in_fwd")
    qkv = _prep_fwd(p_qkv, p["conv"], name="prep_fwd")
    s_zero = jnp.zeros((2, NH, HD, HD), F32)
    _, _, sall_cf, sall_cb, s_ctx = _scan_fwd(qkv, p_ab, p["cst"], s_zero, row_blk0=0, nb=1, name="scan_ctx")
    o_f, o_b, sall_f, sall_b, _ = _scan_fwd(qkv, p_ab, p["cst"], s_ctx, row_blk0=1, nb=nbx, name="scan_lat")
    pd = _pool(p_pool, row0=TM, transpose=False, name="pool_fwd")
    merge_w = (modv, p["gnw"], p["pool_w"], p["pscale"], p["w_gdn"], p["w_pool"], p["w_mo"])
    x2, og, yp, m, mix = _merge_fwd(x1, o_f, o_b, p_gate, pd, p_br, *merge_w, name="merge_fwd")
    x3, h3, gu3, f3 = _ffn_fwd(x2, mod_lat, p["norm3"], p["w2_in"], p["w2_out"], mrow=6, name="ffn2_fwd")
    dx3, acc_fin = _final(x3, target, p["fnorm"], name="final")

    dx2, a3, df3, dgu3, acc3 = _ffn_bwd(dx3, x2, gu3, f3, mod_lat, p["norm3"], p["w2_in"], p["w2_out"], mrow=6, dxo_off=0,
                                        name="ffn2_bwd")
    g = {}
    tkl = _k_tile(t_lat)
    g["w2_out"] = _matmul_tn(a3, df3, tmm=FF // 2, tn=D, tk=tkl, name="ffn2_wout_grad").reshape(NCHIP, FF // NCHIP, D)
    g["w2_in"] = _matmul_tn(h3, dgu3, tmm=D, tn=2 * FF // NCHIP, tk=tkl, nsplit=NCHIP, name="ffn2_win_grad")
    do, dgate, dpd, dbr, dmix, dyg, dyp, acc_m, dpw = _merge_bwd(dx2, mix, o_f, o_b, p_gate, pd, p_br, *merge_w, name="merge_bwd")
    g["w_mo"] = _matmul_tn(m, dmix, tmm=D, tn=D, tk=tkl, name="wmo_grad").reshape(NCHIP, D // NCHIP, D)
    g["w_gdn"] = _matmul_tn(og, dyg, tmm=D, tn=D, tk=tkl, name="wgdn_grad").reshape(NCHIP, D // NCHIP, D)
    g["w_pool"] = _matmul_tn(yp, dyp, tmm=NPOOL, tn=D // NCHIP, tk=tkl, nsplit=NCHIP, name="wpool_grad")
    dpool_in = _pool(dpd, row0=0, transpose=True, name="pool_bwd")
    acc = (lax.empty((n_all, NQKV), F32), lax.empty((n_all, NQKV), F32), lax.empty((n_all, 128), F32),
           lax.empty((n_all, 128), F32), jnp.zeros((8, 128), F32))
    *acc, ds_ctx = _scan_bwd(qkv, p_ab, p["cst"], sall_f, sall_b, do, s_zero, *acc, row_blk0=1, nb=nbx, has_do=True,
                             name="scan_lat_bwd")
    dqkv_f, dqkv_b, dab_f, dab_b, dcst, _ = _scan_bwd(qkv, p_ab, p["cst"], sall_cf, sall_cb, jnp.zeros((8, 128), F32), ds_ctx, *acc,
                                                      row_blk0=0, nb=1, has_do=False, name="scan_ctx_bwd")
    dpqkv, dconv = _prep_bwd(p_qkv, dqkv_f, dqkv_b, p["conv"], name="prep_bwd")
    dx1, dp, acc_mix = _mix_in_bwd(dx2, x1, dpqkv, dgate, dpool_in, dbr, dab_f, dab_b, modv, p["norm2"], p["w_mix"],
                                   name="mix_in_bwd")
    tka = _k_tile(n_all)
    g["w_mix"] = _matmul_tn(u, dp, tmm=256, tn=NMIXP, tk=_k_tile(n_all, 1024), name="wmix_grad")
    dxc, a1, df1, dgu1, acc1 = _ffn_bwd(dx1, xc, gu1, f1, modv, p["norm1"], p["w1_in"], p["w1_out"], mrow=0, dxo_off=0,
                                        name="ffn1_bwd")
    g["w1_out"] = _matmul_tn(a1, df1, tmm=FF // 2, tn=D, tk=tka, name="ffn1_wout_grad").reshape(NCHIP, FF // NCHIP, D)
    g["w1_in"] = _matmul_tn(h1, dgu1, tmm=D, tn=2 * FF // NCHIP, tk=tka, nsplit=NCHIP, name="ffn1_win_grad")

    small = dict(norm1=acc1[0, 3] + acc1[1, 3], norm2=acc_mix[0, 3] + acc_mix[1, 3], norm3=acc3[0, 3], fnorm=acc_fin[0],
                 gnw=acc_m[1, :HD], pscale=acc_m[2, :NPOOL], pool_w=dpw, conv=dconv[:5],
                 a_log=dcst[0, 2 * NH:4 * NH], dt_bias=dcst[1, 2 * NH:4 * NH])
    zero = jnp.zeros((D,), F32)
    dmod = jnp.stack([
        jnp.stack([acc1[0, 0], acc1[0, 1], acc1[0, 2], acc_mix[0, 0], acc_mix[0, 1], zero, zero, zero, zero]),
        jnp.stack([acc1[1, 0], acc1[1, 1], acc1[1, 2], acc_mix[1, 0], acc_mix[1, 1], acc_m[0], acc3[0, 0], acc3[0, 1], acc3[0, 2]]),
    ])
    return jnp.sum(acc_fin[1]), dxc, g, small, dmod


_HI = lax.Precision.HIGHEST


def _ada_fwd(c_all, w_sh, b_sh, *, name):
    def body(c_ref, w_ref, b_ref, o_ref):
        o_ref[...] = jnp.dot(_silu(c_ref[...]), w_ref[...], precision=_HI, preferred_element_type=F32) + b_ref[...]

    return pl.pallas_call(body, name=name, out_shape=jax.ShapeDtypeStruct((16, w_sh.shape[1]), F32),
                          compiler_params=pltpu.CompilerParams(vmem_limit_bytes=VMEM_LIMIT))(c_all, w_sh, b_sh)


def _ada_bwd(c_all, dm, w_sh, *, name):
    def body(c_ref, dm_ref, w_ref, dw_ref, dc_ref):
        sc = _silu(c_ref[...])
        dw_ref[...] = lax.dot_general(sc, dm_ref[...], (((0,), (0,)), ((), ())), precision=_HI, preferred_element_type=F32)
        part = lax.dot_general(dm_ref[8:9, :], w_ref[...], (((1,), (1,)), ((), ())), precision=_HI, preferred_element_type=F32)
        dc_ref[...] = jnp.broadcast_to(part, dc_ref.shape)

    return pl.pallas_call(body, name=name,
                          out_shape=[jax.ShapeDtypeStruct(w_sh.shape, F32), jax.ShapeDtypeStruct((8, D), F32)],
                          compiler_params=pltpu.CompilerParams(vmem_limit_bytes=VMEM_LIMIT))(c_all, dm, w_sh)


def _cctx_grad(parts, c_ctx, *, name):
    def body(p_ref, c_ref, o_ref):
        tot = (p_ref[0, 0:1, :] + p_ref[2, 0:1, :]) + (p_ref[4, 0:1, :] + p_ref[6, 0:1, :])
        o_ref[...] = tot * _dsilu(c_ref[...])

    return pl.pallas_call(body, name=name, out_shape=jax.ShapeDtypeStruct((1, D), F32))(parts, c_ctx)


_MESH = pl.DeviceIdType.MESH
_ANY = pl.BlockSpec(memory_space=pl.ANY)


def _flip(v, bit):
    return (1 - v) if bit else v


def _all_gather8(x, *, name):
    def body(x_ref, out_ref, send_sems, recv_sems, local_sem):
        mx, my, mc = lax.axis_index("x"), lax.axis_index("y"), lax.axis_index("c")
        me = 4 * mx + 2 * my + mc
        mine = pltpu.make_async_copy(x_ref, out_ref.at[me], local_sem)
        mine.start()
        sends, recvs = [], []
        for k in range(1, 8):
            px, py, pc = _flip(mx, k & 4), _flip(my, k & 2), _flip(mc, k & 1)
            sends.append(pltpu.make_async_remote_copy(src_ref=x_ref, dst_ref=out_ref.at[me], send_sem=send_sems.at[k - 1],
                                                      recv_sem=recv_sems.at[k - 1], device_id=(px, py, pc), device_id_type=_MESH))
            recvs.append(pltpu.make_async_remote_copy(src_ref=x_ref, dst_ref=out_ref.at[4 * px + 2 * py + pc],
                                                      send_sem=send_sems.at[k - 1], recv_sem=recv_sems.at[k - 1],
                                                      device_id=(px, py, pc), device_id_type=_MESH))
        for cp in sends:
            cp.start()
        for cp in recvs:
            cp.wait_recv()
        for cp in sends:
            cp.wait_send()
        mine.wait()

    vm = pl.BlockSpec(memory_space=pltpu.VMEM)
    return pl.pallas_call(
        body, name=name, out_shape=jax.ShapeDtypeStruct((8,) + x.shape, x.dtype), in_specs=[vm], out_specs=vm,
        scratch_shapes=[pltpu.SemaphoreType.DMA((7,)), pltpu.SemaphoreType.DMA((7,)), pltpu.SemaphoreType.DMA],
        compiler_params=pltpu.CompilerParams(vmem_limit_bytes=VMEM_LIMIT),
    )(x)


def _chip_exchange(arrs, *, scatter, name):
    n = len(arrs)

    def body(*refs):
        ins, outs = refs[:n], refs[n:2 * n]
        send_sems, recv_sems, local_sems = refs[2 * n:]
        mx, my, mc = lax.axis_index("x"), lax.axis_index("y"), lax.axis_index("c")
        me = 2 * mx + my
        local, sends, recvs = [], [], []
        for j in range(n):
            src_own = ins[j].at[me] if scatter else ins[j]
            local.append(pltpu.make_async_copy(src_own, outs[j].at[me], local_sems.at[j]))
            for k in range(1, NCHIP):
                px, py = _flip(mx, k & 2), _flip(my, k & 1)
                peer = 2 * px + py
                sem = j * (NCHIP - 1) + k - 1
                src = ins[j].at[peer] if scatter else ins[j]
                sends.append(pltpu.make_async_remote_copy(src_ref=src, dst_ref=outs[j].at[me], send_sem=send_sems.at[sem],
                                                          recv_sem=recv_sems.at[sem], device_id=(px, py, mc), device_id_type=_MESH))
                recvs.append(pltpu.make_async_remote_copy(src_ref=src, dst_ref=outs[j].at[peer], send_sem=send_sems.at[sem],
                                                          recv_sem=recv_sems.at[sem], device_id=(px, py, mc), device_id_type=_MESH))
        for cp in local + sends:
            cp.start()
        for cp in recvs:
            cp.wait_recv()
        for cp in sends:
            cp.wait_send()
        for cp in local:
            cp.wait()

    out_shape = [jax.ShapeDtypeStruct(a.shape if scatter else (NCHIP,) + a.shape, a.dtype) for a in arrs]
    return pl.pallas_call(
        body, name=name, out_shape=out_shape, in_specs=[_ANY] * n, out_specs=[_ANY] * n,
        scratch_shapes=[pltpu.SemaphoreType.DMA((n * (NCHIP - 1),)), pltpu.SemaphoreType.DMA((n * (NCHIP - 1),)),
                        pltpu.SemaphoreType.DMA((n,))],
    )(*arrs)


def _core_swap(arrs, *, name):
    n = len(arrs)

    def body(*refs):
        ins, outs = refs[:n], refs[n:2 * n]
        send_sems, recv_sems = refs[2 * n:]
        sib = (lax.axis_index("x"), lax.axis_index("y"), 1 - lax.axis_index("c"))
        cps = [pltpu.make_async_remote_copy(src_ref=ins[j], dst_ref=outs[j], send_sem=send_sems.at[j], recv_sem=recv_sems.at[j],
                                            device_id=sib, device_id_type=_MESH) for j in range(n)]
        for cp in cps:
            cp.start()
        for cp in cps:
            cp.wait_recv()
        for cp in cps:
            cp.wait_send()

    return pl.pallas_call(
        body, name=name, out_shape=[jax.ShapeDtypeStruct(a.shape, a.dtype) for a in arrs],
        in_specs=[_ANY] * n, out_specs=[_ANY] * n,
        scratch_shapes=[pltpu.SemaphoreType.DMA((n,)), pltpu.SemaphoreType.DMA((n,))],
    )(*arrs)


def _row_tile(rows, cols, budget=1 << 18):
    best = None
    for t in range(8, rows + 1, 8):
        if rows % t == 0 and t * cols <= budget:
            best = t
    return best or rows


def _sum_slots(x, *, name):
    ns, r, c = x.shape
    tr = _row_tile(r, c * ns)

    def body(x_ref, o_ref):
        acc = x_ref[0].astype(F32)
        for s in range(1, ns):
            acc = acc + x_ref[s].astype(F32)
        o_ref[...] = acc

    return pl.pallas_call(
        body, name=name, grid=(r // tr,), out_shape=jax.ShapeDtypeStruct((r, c), F32),
        in_specs=[pl.BlockSpec((ns, tr, c), lambda i: (0, i, 0))], out_specs=pl.BlockSpec((tr, c), lambda i: (i, 0)),
        compiler_params=_cparams("parallel"),
    )(x)


def _adamw(w, ga, gb, m, v, *, name):
    r, c = w.shape
    tr = _row_tile(r, c, budget=1 << 17)
    two = gb is not None

    def body(*refs):
        w_ref, ga_ref = refs[0], refs[1]
        m_ref, v_ref = refs[2 + two], refs[3 + two]
        g_ref, d_ref, mo_ref, vo_ref = refs[4 + two:]
        g = ga_ref[...] + refs[2][...] if two else ga_ref[...]
        mn = ADAM_B1 * m_ref[...] + (1.0 - ADAM_B1) * g
        vn = ADAM_B2 * v_ref[...] + (1.0 - ADAM_B2) * (g * g)
        m_hat = mn / (1.0 - ADAM_B1 ** ADAM_STEP)
        v_hat = vn / (1.0 - ADAM_B2 ** ADAM_STEP)
        g_ref[...] = g
        d_ref[...] = -ADAM_LR * (m_hat / (jnp.sqrt(v_hat) + ADAM_EPS) + ADAM_WD * w_ref[...])
        mo_ref[...] = mn
        vo_ref[...] = vn

    spec = pl.BlockSpec((tr, c), lambda i: (i, 0))
    ins = [w, ga] + ([gb] if two else []) + [m, v]
    return pl.pallas_call(
        body, name=name, grid=(r // tr,), out_shape=[jax.ShapeDtypeStruct((r, c), F32)] * 4,
        in_specs=[spec] * len(ins), out_specs=[spec] * 4, compiler_params=_cparams("parallel"),
    )(*ins)


_MIX_AB0, _MIX_AB1 = NQKV, NQKV + 4 * NH


def _regroup_mix(w):
    pad = jnp.zeros((w.shape[0], NMIXP - NMIX), w.dtype)
    return jnp.concatenate([w[:, :_MIX_AB0], w[:, _MIX_AB1:], w[:, _MIX_AB0:_MIX_AB1], pad], axis=1)


def _ungroup_mix(w):
    n_ab = _MIX_AB1 - _MIX_AB0
    return jnp.concatenate([w[:, :_MIX_AB0], w[:, NMIX - n_ab:NMIX], w[:, _MIX_AB0:NMIX - n_ab]], axis=1)


def _chip_major_cols(w):
    r, c = w.shape
    return w.reshape(r, NCHIP, c // NCHIP).transpose(1, 0, 2)


def _from_chip_major_cols(w):
    return w.transpose(1, 0, 2).reshape(w.shape[1], -1)


_SMALL = (("c_ctx", D), ("b_ada", 9 * D), ("norm1_w", D), ("norm2_w", D), ("norm3_w", D), ("final_norm_w", D),
          ("a_log", 2 * NH), ("dt_bias", 2 * NH), ("gdn_norm_w", HD), ("pool_w", 4 * 128 * 128), ("pool_scale", NPOOL),
          ("conv_w", 5 * NQKV // NCHIP))


def _pack(vals, lanes=128, row_mult=8):
    flat = jnp.concatenate([jnp.ravel(v) for v in vals])
    n = flat.shape[0]
    rows = -(-n // (lanes * row_mult)) * row_mult
    return jnp.pad(flat, (0, rows * lanes - n)).reshape(rows, lanes)


def _unpack(packed, sizes):
    flat = packed.reshape(-1)
    out, o = [], 0
    for n in sizes:
        out.append(flat[o:o + n])
        o += n
    return out


def kernel(x, c, ctx, c_ctx, w_ada, b_ada, norm1_w, ffn1_w_in, ffn1_w_out, norm2_w, w_mix_in, conv_w, a_log, dt_bias, gdn_norm_w, w_gdn_proj, pool_w, pool_scale, w_pool_proj, w_mix_out, norm3_w, ffn2_w_in, ffn2_w_out, final_norm_w, loss_target, m_c_ctx, m_w_ada, m_b_ada, m_norm1_w, m_ffn1_w_in, m_ffn1_w_out, m_norm2_w, m_w_mix_in, m_conv_w, m_a_log, m_dt_bias, m_gdn_norm_w, m_w_gdn_proj, m_pool_w, m_pool_scale, m_w_pool_proj, m_w_mix_out, m_norm3_w, m_ffn2_w_in, m_ffn2_w_out, m_final_norm_w, v_c_ctx, v_w_ada, v_b_ada, v_norm1_w, v_ffn1_w_in, v_ffn1_w_out, v_norm2_w, v_w_mix_in, v_conv_w, v_a_log, v_dt_bias, v_gdn_norm_w, v_w_gdn_proj, v_pool_w, v_pool_scale, v_w_pool_proj, v_w_mix_out, v_norm3_w, v_ffn2_w_in, v_ffn2_w_out, v_final_norm_w):
    names = ("c_ctx", "w_ada", "b_ada", "norm1_w", "ffn1_w_in", "ffn1_w_out", "norm2_w", "w_mix_in", "conv_w", "a_log", "dt_bias",
             "gdn_norm_w", "w_gdn_proj", "pool_w", "pool_scale", "w_pool_proj", "w_mix_out", "norm3_w", "ffn2_w_in", "ffn2_w_out",
             "final_norm_w")
    w = dict(zip(names, (c_ctx, w_ada, b_ada, norm1_w, ffn1_w_in, ffn1_w_out, norm2_w, w_mix_in, conv_w, a_log, dt_bias, gdn_norm_w,
                         w_gdn_proj, pool_w, pool_scale, w_pool_proj, w_mix_out, norm3_w, ffn2_w_in, ffn2_w_out, final_norm_w)))
    mom = dict(zip(names, (m_c_ctx, m_w_ada, m_b_ada, m_norm1_w, m_ffn1_w_in, m_ffn1_w_out, m_norm2_w, m_w_mix_in, m_conv_w, m_a_log,
                           m_dt_bias, m_gdn_norm_w, m_w_gdn_proj, m_pool_w, m_pool_scale, m_w_pool_proj, m_w_mix_out, m_norm3_w,
                           m_ffn2_w_in, m_ffn2_w_out, m_final_norm_w)))
    var = dict(zip(names, (v_c_ctx, v_w_ada, v_b_ada, v_norm1_w, v_ffn1_w_in, v_ffn1_w_out, v_norm2_w, v_w_mix_in, v_conv_w, v_a_log,
                           v_dt_bias, v_gdn_norm_w, v_w_gdn_proj, v_pool_w, v_pool_scale, v_w_pool_proj, v_w_mix_out, v_norm3_w,
                           v_ffn2_w_in, v_ffn2_w_out, v_final_norm_w)))
    mx, my, mc = lax.axis_index("x"), lax.axis_index("y"), lax.axis_index("c")
    chip = 2 * mx + my
    dev = 2 * chip + mc
    ada_cols = w_ada.shape[2]

    c_rows = _all_gather8(jnp.pad(c, ((0, 7), (0, 0))), name="gather_c")[:, 0, :]
    c_all = jnp.concatenate([c_rows, c_ctx[None], jnp.zeros((7, D), F32)], axis=0)
    b_sh = lax.dynamic_slice(b_ada, (0, chip * ada_cols), (1, ada_cols))
    mod_sh = _ada_fwd(c_all, w_ada[0], b_sh, name="ada_fwd")
    mod_parts = _all_gather8(mod_sh, name="gather_mod")
    mod_all = jnp.concatenate([mod_parts[2 * s] for s in range(NCHIP)], axis=1)
    mod_lat = lax.dynamic_index_in_dim(mod_all, dev, axis=0, keepdims=False).reshape(9, D)
    modv = jnp.zeros((2, 16, D), F32).at[0, :9].set(mod_all[8].reshape(9, D)).at[1, :9].set(mod_lat)

    big = ("ffn1_w_in", "ffn1_w_out", "w_mix_in", "w_gdn_proj", "w_pool_proj", "w_mix_out", "ffn2_w_in", "ffn2_w_out")
    gathered = _chip_exchange([w[k][0].astype(BF) for k in big] + [conv_w[0]], scatter=False, name="gather_weights")
    gw = dict(zip(big + ("conv_w",), gathered))
    conv_full = _from_chip_major_cols(gw["conv_w"])
    p = dict(
        norm1=norm1_w, norm2=norm2_w, norm3=norm3_w, fnorm=final_norm_w[None],
        w1_in=gw["ffn1_w_in"], w1_out=gw["ffn1_w_out"].reshape(FF, D), w2_in=gw["ffn2_w_in"], w2_out=gw["ffn2_w_out"].reshape(FF, D),
        w_mix=_regroup_mix(_from_chip_major_cols(gw["w_mix_in"])), conv=jnp.pad(conv_full, ((0, 3), (0, 0))),
        cst=jnp.zeros((8, 128), F32).at[0, 2 * NH:4 * NH].set(jnp.exp(a_log).reshape(-1)).at[1, 2 * NH:4 * NH].set(dt_bias.reshape(-1)),
        gnw=gdn_norm_w, pool_w=pool_w[0], pscale=pool_scale,
        w_gdn=gw["w_gdn_proj"].reshape(D, D), w_pool=_from_chip_major_cols(gw["w_pool_proj"]), w_mo=gw["w_mix_out"].reshape(D, D))

    xc = jnp.concatenate([ctx[0], x[0]], axis=0)
    loss_dev, dxc, g, small, dmod = _local_step(xc, loss_target[0], modv, p)
    loss = lax.psum(loss_dev, ("x", "y", "c"))
    grad_x = dxc[TM:][None]

    g_mix = _chip_major_cols(_ungroup_mix(g["w_mix"]))
    parts = [g["w1_in"], g["w1_out"], g_mix, g["w_gdn"], g["w_pool"], g["w_mo"], g["w2_in"], g["w2_out"]]
    landed = _chip_exchange(parts, scatter=True, name="scatter_grads")
    mine = [_sum_slots(a, name=f"sum_{k}") for k, a in zip(big, landed)]
    theirs = _core_swap(mine, name="swap_grad_sums")

    small_vals = [dmod[1], dmod[0], small["norm1"], small["norm2"], small["norm3"], small["fnorm"], small["a_log"], small["dt_bias"],
                  small["gnw"], small["pool_w"], small["pscale"], small["conv"]]
    small_sizes = [v.size for v in small_vals]
    packed = _all_gather8(_pack(small_vals), name="gather_small")
    tot = _unpack(_sum_slots(packed, name="sum_small"), small_sizes)
    dmod_lat_all = packed[:, :9 * D // 128, :].reshape(8, 9 * D)
    dm = jnp.concatenate([dmod_lat_all, tot[1][None], jnp.zeros((7, 9 * D), F32)], axis=0)
    dm_sh = lax.dynamic_slice(dm, (0, chip * ada_cols), (16, ada_cols))
    g_w_ada, cctx_part = _ada_bwd(c_all, dm_sh, w_ada[0], name="ada_bwd")
    g_c_ctx = _cctx_grad(_all_gather8(cctx_part, name="gather_cctx"), c_ctx[None], name="cctx_grad")[0]
    conv_tot = tot[11].reshape(5, NQKV)
    g_small = dict(c_ctx=g_c_ctx, b_ada=tot[0] + tot[1], norm1_w=tot[2], norm2_w=tot[3], norm3_w=tot[4], final_norm_w=tot[5],
                   a_log=tot[6], dt_bias=tot[7], gdn_norm_w=tot[8], pool_w=tot[9], pool_scale=tot[10],
                   conv_w=lax.dynamic_slice(conv_tot, (0, chip * (NQKV // NCHIP)), (5, NQKV // NCHIP)))

    out = {}
    as2d = lambda a: a.reshape(-1, a.shape[-1])
    for k, ga, gb in zip(big, mine, theirs):
        shp = w[k].shape
        res = _adamw(as2d(w[k]), as2d(ga), as2d(gb), as2d(mom[k]), as2d(var[k]), name=f"adamw_{k}")
        out[k] = [r.reshape(shp) for r in res]
    out["w_ada"] = [r.reshape(w_ada.shape) for r in _adamw(w_ada[0], g_w_ada, None, m_w_ada[0], v_w_ada[0], name="adamw_w_ada")]
    sm_names = [n for n, _ in _SMALL]
    sm_sizes = [n for _, n in _SMALL]
    res = _adamw(_pack([w[k] for k in sm_names]), _pack([g_small[k] for k in sm_names]), None,
                 _pack([mom[k] for k in sm_names]), _pack([var[k] for k in sm_names]), name="adamw_small")
    res = [_unpack(r, sm_sizes) for r in res]
    for i, k in enumerate(sm_names):
        out[k] = [r[i].reshape(w[k].shape) for r in res]
    return (loss, grad_x, *[out[k][0] for k in names], *[out[k][1] for k in names], *[out[k][2] for k in names],
            *[out[k][3] for k in names])
```

```python
import functools

import jax
import jax.numpy as jnp
from jax import lax
from jax.experimental import pallas as pl
from jax.experimental.pallas import tpu as pltpu

F32 = jnp.float32
BF = jnp.bfloat16

D = 1024
FF = 2816
NH = 8
HD = 128
CH = 64
GW = 64
TM = 256
NQKV = 3 * NH * HD
NPOOL = 512
POOL_WINDOWS = (2, 4, 8, 16)
NMIX = 6688
NMIXP = 6784
EPS = 1e-6
NCHIP = 4
VMEM_LIMIT = 56 * 1024 * 1024

ADAM_LR, ADAM_B1, ADAM_B2, ADAM_EPS, ADAM_WD, ADAM_STEP = 0.001, 0.9, 0.999, 1e-08, 0.01, 10


def _cparams(*sem):
    return pltpu.CompilerParams(dimension_semantics=sem, vmem_limit_bytes=VMEM_LIMIT)


def _const_spec(shape):
    nd = len(shape)
    return pl.BlockSpec(shape, lambda *_: (0,) * nd, pipeline_mode=pl.Buffered(1))


def _dot(a, b, dims):
    return lax.dot_general(a.astype(BF), b.astype(BF), (dims, ((), ())), preferred_element_type=F32)


def _nn(a, b):
    return _dot(a, b, ((1,), (0,)))


def _nt(a, b):
    return _dot(a, b, ((1,), (1,)))


def _tn(a, b):
    return _dot(a, b, ((0,), (0,)))


def _silu(x):
    return x * jax.nn.sigmoid(x)


def _dsilu(x):
    s = jax.nn.sigmoid(x)
    return s * (1.0 + x * (1.0 - s))


def _norm_mod(x, nw, shift, scale):
    r = lax.rsqrt(jnp.mean(x * x, axis=-1, keepdims=True) + EPS)
    xh = x * r
    n = xh * nw
    return n * (1.0 + scale) + shift, n, xh, r


def _norm_mod_bwd(dh, n, xh, r, nw, scale):
    dn = dh * (1.0 + scale)
    dxh = dn * nw
    dx = r * (dxh - xh * jnp.mean(dxh * xh, axis=-1, keepdims=True))
    rs = lambda t: jnp.sum(t, axis=0, keepdims=True)
    return dx, rs(dh), rs(dh * n), rs(dn * xh)


def _ffn_fwd(x, modv, nw, w_in4, w_out, *, mrow, name, exchange=None):
    n_tok = x.shape[0]
    nt = n_tok // TM
    nset = modv.shape[0]
    ws = w_in4.shape[2]

    def body(x_ref, mod_ref, nw_ref, win_ref, wout_ref, x1_ref, h_ref, gu_ref, f_ref):
        xv = x_ref[...]
        shift, scale, gate = mod_ref[0, mrow:mrow + 1, :], mod_ref[0, mrow + 1:mrow + 2, :], mod_ref[0, mrow + 2:mrow + 3, :]
        h, _, _, _ = _norm_mod(xv, nw_ref[...], shift, scale)
        hb = h.astype(BF)
        h_ref[...] = hb
        gus = [_nn(hb, win_ref[s]) for s in range(NCHIP)]
        for s in range(NCHIP):
            gu_ref[:, s * ws:(s + 1) * ws] = gus[s].astype(BF)
        g = jnp.concatenate(gus[:2], axis=1)
        u = jnp.concatenate(gus[2:], axis=1)
        f = _nn(_silu(g) * u, wout_ref[...])
        f_ref[...] = f.astype(BF)
        x1_ref[...] = xv + 0.5 * gate * f

    tile = lambda w: pl.BlockSpec((TM, w), lambda i: (i, 0))
    return _pallas(
        body, (x, modv, nw, w_in4, w_out), name=name, grid=(nt,), sem=("parallel",), exchange=exchange,
        in_specs=[tile(D), pl.BlockSpec((1, 16, D), lambda i: (jnp.minimum(i, nset - 1), 0, 0)), _const_spec((1, D)),
                  _const_spec(w_in4.shape), _const_spec(w_out.shape)],
        out_specs=[tile(D), tile(D), tile(2 * FF), tile(D)],
        out_shape=[jax.ShapeDtypeStruct((n_tok, D), F32), jax.ShapeDtypeStruct((n_tok, D), BF),
                   jax.ShapeDtypeStruct((n_tok, 2 * FF), BF), jax.ShapeDtypeStruct((n_tok, D), BF)])


def _ffn_bwd(dxo, x, gu, fo, modv, nw, w_in4, w_out, *, mrow, dxo_off, name, exchange=None):
    n_tok = x.shape[0]
    nt = n_tok // TM
    nset = modv.shape[0]
    ws = w_in4.shape[2]

    def body(dxo_ref, x_ref, gu_ref, f_ref, mod_ref, nw_ref, win_ref, wout_ref, dx_ref, a_ref, df_ref, dgu_ref, acc_ref):
        i = pl.program_id(0)
        xv = x_ref[...]
        dxo_v = dxo_ref[...]
        if dxo_off:
            dxo_v = jnp.where(i >= dxo_off, dxo_v, 0.0)
        shift, scale, gate = mod_ref[0, mrow:mrow + 1, :], mod_ref[0, mrow + 1:mrow + 2, :], mod_ref[0, mrow + 2:mrow + 3, :]
        _, n, xh, r = _norm_mod(xv, nw_ref[...], shift, scale)
        df = 0.5 * gate * dxo_v
        dfb = df.astype(BF)
        df_ref[...] = dfb
        dgate = jnp.sum(0.5 * dxo_v * f_ref[...].astype(F32), axis=0, keepdims=True)
        da = _nt(dfb, wout_ref[...])
        g = gu_ref[:, :FF].astype(F32)
        u = gu_ref[:, FF:].astype(F32)
        sg = _silu(g)
        a_ref[...] = (sg * u).astype(BF)
        dgu_ref[:, :FF] = (da * u * _dsilu(g)).astype(BF)
        dgu_ref[:, FF:] = (da * sg).astype(BF)
        dh = _nt(dgu_ref[:, 0:ws], win_ref[0])
        for s in range(1, NCHIP):
            dh = dh + _nt(dgu_ref[:, s * ws:(s + 1) * ws], win_ref[s])
        dx, dshift, dscale, dnw = _norm_mod_bwd(dh, n, xh, r, nw_ref[...], scale)
        dx_ref[...] = dxo_v + dx

        @pl.when((i == 0) | (i == nset - 1))
        def _():
            acc_ref[...] = jnp.zeros_like(acc_ref)

        acc_ref[0, 0:1, :] += dshift
        acc_ref[0, 1:2, :] += dscale
        acc_ref[0, 2:3, :] += dgate
        acc_ref[0, 3:4, :] += dnw

    tile = lambda w: pl.BlockSpec((TM, w), lambda i: (i, 0))
    return _pallas(
        body, (dxo, x, gu, fo, modv, nw, w_in4, w_out), name=name, grid=(nt,), sem=("arbitrary",), exchange=exchange,
        in_specs=[pl.BlockSpec((TM, D), lambda i: (jnp.maximum(i - dxo_off, 0), 0)), tile(D), tile(2 * FF), tile(D),
                  pl.BlockSpec((1, 16, D), lambda i: (jnp.minimum(i, nset - 1), 0, 0)), _const_spec((1, D)),
                  _const_spec(w_in4.shape), _const_spec(w_out.shape)],
        out_specs=[tile(D), tile(FF), tile(D), tile(2 * FF),
                   pl.BlockSpec((1, 8, D), lambda i: (jnp.minimum(i, nset - 1), 0, 0))],
        out_shape=[jax.ShapeDtypeStruct((n_tok, D), F32), jax.ShapeDtypeStruct((n_tok, FF), BF),
                   jax.ShapeDtypeStruct((n_tok, D), BF), jax.ShapeDtypeStruct((n_tok, 2 * FF), BF),
                   jax.ShapeDtypeStruct((nset, 8, D), F32)])


def _k_tile(n, target=3072):
    return max(t for t in range(TM, min(n, target) + 1, TM) if n % t == 0)


def _matmul_tn(a, b, *, tmm, tn, tk, nsplit=1, name):
    n_tok, m = a.shape
    kk = b.shape[1]
    nk = n_tok // tk

    def body(a_ref, b_ref, o_ref, acc):
        k = pl.program_id(2)

        @pl.when(k == 0)
        def _():
            acc[...] = jnp.zeros_like(acc)

        acc[...] += _tn(a_ref[...], b_ref[...])

        @pl.when(k == nk - 1)
        def _():
            o_ref[...] = acc[...].astype(BF).reshape(o_ref.shape)

    if nsplit == 1:
        out_shape = jax.ShapeDtypeStruct((m, kk), BF)
        out_spec = pl.BlockSpec((tmm, tn), lambda i, j, k: (i, j))
    else:
        assert tn == kk // nsplit
        out_shape = jax.ShapeDtypeStruct((nsplit, m, tn), BF)
        out_spec = pl.BlockSpec((1, tmm, tn), lambda i, j, k: (j, i, 0))
    return pl.pallas_call(
        body, name=name, grid=(m // tmm, kk // tn, nk),
        in_specs=[pl.BlockSpec((tk, tmm), lambda i, j, k: (k, i)), pl.BlockSpec((tk, tn), lambda i, j, k: (k, j))],
        out_specs=out_spec, out_shape=out_shape,
        scratch_shapes=[pltpu.VMEM((tmm, tn), F32)],
        compiler_params=_cparams("parallel", "parallel", "arbitrary"),
    )(a, b)


_MIX_PARTS = (("qkv", 0, NQKV), ("gate", NQKV, 1024), ("pool", NQKV + 1024, NPOOL), ("br", NQKV + 1024 + NPOOL, 2048),
              ("ab", NMIXP - 128, 128))


def _mix_in_fwd(x1, modv, nw, w_mix, *, name):
    n_tok = x1.shape[0]

    def body(x_ref, mod_ref, nw_ref, w_ref, u_ref, *p_refs):
        u, _, _, _ = _norm_mod(x_ref[...], nw_ref[...], mod_ref[0, 3:4, :], mod_ref[0, 4:5, :])
        ub = u.astype(BF)
        u_ref[...] = ub
        for (_, c0, w), p_ref in zip(_MIX_PARTS, p_refs):
            p_ref[...] = _nn(ub, w_ref[:, c0:c0 + w])

    tile = lambda w: pl.BlockSpec((TM, w), lambda i: (i, 0))
    ctile = lambda w: pl.BlockSpec((TM, w), lambda i: (i + 1, 0))
    return pl.pallas_call(
        body, name=name, grid=(n_tok // TM,),
        in_specs=[tile(D), pl.BlockSpec((1, 16, D), lambda i: (jnp.minimum(i, 1), 0, 0)), _const_spec((1, D)),
                  _const_spec(w_mix.shape)],
        out_specs=[tile(D)] + [tile(w) for _, _, w in _MIX_PARTS],
        out_shape=[jax.ShapeDtypeStruct((n_tok, D), BF)] + [jax.ShapeDtypeStruct((n_tok, w), F32) for _, _, w in _MIX_PARTS],
        compiler_params=_cparams("parallel"),
    )(x1, modv, nw, w_mix)


def _mix_in_bwd(dxo, x1, dqkv, dgate, dpool, dbr, dab_f, dab_b, modv, nw, w_mix, *, name):
    n_tok = x1.shape[0]

    def body(dxo_ref, x_ref, dqkv_ref, dgate_ref, dpool_ref, dbr_ref, dabf_ref, dabb_ref, mod_ref, nw_ref, w_ref,
             dx_ref, dp_ref, acc_ref):
        i = pl.program_id(0)
        lat = i >= 1
        scale = mod_ref[0, 4:5, :]
        _, n, xh, r = _norm_mod(x_ref[...], nw_ref[...], mod_ref[0, 3:4, :], scale)
        dp_ref[:, 0:NQKV] = dqkv_ref[...].astype(BF)
        dp_ref[:, NQKV:NQKV + 1024] = jnp.where(lat, dgate_ref[...], 0.0).astype(BF)
        dp_ref[:, NQKV + 1024:NQKV + 1536] = jnp.where(lat, dpool_ref[...], 0.0).astype(BF)
        dp_ref[:, NQKV + 1536:NMIXP - 128] = jnp.where(lat, dbr_ref[...], 0.0).astype(BF)
        dp_ref[:, NMIXP - 128:] = (dabf_ref[...] + dabb_ref[...]).astype(BF)
        du = _nt(dp_ref[...], w_ref[...])
        dx, dshift, dscale, dnw = _norm_mod_bwd(du, n, xh, r, nw_ref[...], scale)
        dx_ref[...] = jnp.where(lat, dxo_ref[...], 0.0) + dx

        @pl.when(i <= 1)
        def _():
            acc_ref[...] = jnp.zeros_like(acc_ref)

        acc_ref[0, 0:1, :] += dshift
        acc_ref[0, 1:2, :] += dscale
        acc_ref[0, 3:4, :] += dnw

    tile = lambda w: pl.BlockSpec((TM, w), lambda i: (i, 0))
    ltile = lambda w: pl.BlockSpec((TM, w), lambda i: (jnp.maximum(i - 1, 0), 0))
    return pl.pallas_call(
        body, name=name, grid=(n_tok // TM,),
        in_specs=[ltile(D), tile(D), tile(NQKV), ltile(1024), ltile(NPOOL), ltile(2048), tile(128), tile(128),
                  pl.BlockSpec((1, 16, D), lambda i: (jnp.minimum(i, 1), 0, 0)), _const_spec((1, D)), _const_spec(w_mix.shape)],
        out_specs=[tile(D), tile(NMIXP), pl.BlockSpec((1, 8, D), lambda i: (jnp.minimum(i, 1), 0, 0))],
        out_shape=[jax.ShapeDtypeStruct((n_tok, D), F32), jax.ShapeDtypeStruct((n_tok, NMIXP), BF),
                   jax.ShapeDtypeStruct((2, 8, D), F32)],
        compiler_params=_cparams("arbitrary"),
    )(dxo, x1, dqkv, dgate, dpool, dbr, dab_f, dab_b, modv, nw, w_mix)


def _qkv_act(pre, j):
    s = _silu(pre)
    nrm = s * lax.rsqrt(jnp.sum(s * s, axis=-1, keepdims=True) + EPS)
    nrm = nrm * jnp.where(j == 0, HD ** -0.5, 1.0)
    return jnp.where(j < 2, nrm, s)


def _halo_specs(nt):
    r = TM // 8
    w = NH * HD
    main = pl.BlockSpec((TM, w), lambda j, i: (i, j))
    prev = pl.BlockSpec((8, w), lambda j, i: (jnp.maximum(i * r - 1, 0), j))
    nxt = pl.BlockSpec((8, w), lambda j, i: (jnp.minimum((i + 1) * r, nt * r - 1), j))
    return main, prev, nxt


def _prep_fwd(p_qkv, conv_w8, *, name):
    n_tok = p_qkv.shape[0]
    nt = n_tok // TM

    def body(x_ref, xp_ref, xn_ref, w_ref, o_ref, win):
        j, i = pl.program_id(0), pl.program_id(1)
        has_prev = (i != 0) & (i != 1)
        has_next = (i != 0) & (i != nt - 1)
        win[0:8, :] = jnp.where(has_prev, xp_ref[...], 0.0)
        win[8:8 + TM, :] = x_ref[...]
        win[8 + TM:, :] = jnp.where(has_next, xn_ref[...], 0.0)
        for h in range(NH):
            hs = slice(h * HD, (h + 1) * HD)
            pre = win[6:6 + TM, hs] * w_ref[0:1, hs]
            for k in range(1, 5):
                pre = pre + win[6 + k:6 + k + TM, hs] * w_ref[k:k + 1, hs]
            o_ref[:, hs] = _qkv_act(pre, j)

    main, prev, nxt = _halo_specs(nt)
    wq = NH * HD
    return pl.pallas_call(
        body, name=name, grid=(3, nt),
        in_specs=[main, prev, nxt, pl.BlockSpec((8, wq), lambda j, i: (0, j))],
        out_specs=main, out_shape=jax.ShapeDtypeStruct((n_tok, NQKV), F32),
        scratch_shapes=[pltpu.VMEM((TM + 16, wq), F32)],
        compiler_params=_cparams("parallel", "arbitrary"),
    )(p_qkv, p_qkv, p_qkv, conv_w8)


def _prep_bwd(p_qkv, dqkv_f, dqkv_b, conv_w8, *, name):
    n_tok = p_qkv.shape[0]
    nt = n_tok // TM
    wr = TM + 16
    wq = NH * HD

    def body(x_ref, xp_ref, xn_ref, g_ref, gp_ref, gn_ref, g2_ref, g2p_ref, g2n_ref, w_ref, dx_ref, dw_ref, xwin, gwin, dwin):
        j, i = pl.program_id(0), pl.program_id(1)
        has_prev = (i != 0) & (i != 1)
        has_next = (i != 0) & (i != nt - 1)
        z8 = jnp.zeros((8, wq), F32)
        xwin[0:8, :] = z8
        xwin[8:16, :] = jnp.where(has_prev, xp_ref[...], 0.0)
        xwin[16:16 + TM, :] = x_ref[...]
        xwin[16 + TM:24 + TM, :] = jnp.where(has_next, xn_ref[...], 0.0)
        xwin[24 + TM:, :] = z8
        gwin[0:8, :] = jnp.where(has_prev, gp_ref[...] + g2p_ref[...], 0.0)
        gwin[8:8 + TM, :] = g_ref[...] + g2_ref[...]
        gwin[8 + TM:, :] = jnp.where(has_next, gn_ref[...] + g2n_ref[...], 0.0)

        @pl.when(i == 0)
        def _():
            dw_ref[...] = jnp.zeros_like(dw_ref)

        for h in range(NH):
            hs = slice(h * HD, (h + 1) * HD)
            pre = xwin[6:6 + wr, hs] * w_ref[0:1, hs]
            for k in range(1, 5):
                pre = pre + xwin[6 + k:6 + k + wr, hs] * w_ref[k:k + 1, hs]
            _, vjp = jax.vjp(lambda t: _qkv_act(t, j), pre)
            dwin[:, hs] = vjp(gwin[:, hs])[0]
            dx = dwin[10:10 + TM, hs] * w_ref[0:1, hs]
            for k in range(1, 5):
                dx = dx + dwin[10 - k:10 - k + TM, hs] * w_ref[k:k + 1, hs]
            dx_ref[:, hs] = dx
            dmid = dwin[8:8 + TM, hs]
            for k in range(5):
                dw_ref[k:k + 1, hs] += jnp.sum(dmid * xwin[14 + k:14 + k + TM, hs], axis=0, keepdims=True)

    main, prev, nxt = _halo_specs(nt)
    wq = NH * HD
    wspec = pl.BlockSpec((8, wq), lambda j, i: (0, j))
    return pl.pallas_call(
        body, name=name, grid=(3, nt),
        in_specs=[main, prev, nxt, main, prev, nxt, main, prev, nxt, wspec],
        out_specs=[main, wspec],
        out_shape=[jax.ShapeDtypeStruct((n_tok, NQKV), F32), jax.ShapeDtypeStruct((8, NQKV), F32)],
        scratch_shapes=[pltpu.VMEM((TM + 32, wq), F32), pltpu.VMEM((TM + 16, wq), F32), pltpu.VMEM((TM + 16, wq), F32)],
        compiler_params=_cparams("parallel", "arbitrary"),
    )(p_qkv, p_qkv, p_qkv, dqkv_f, dqkv_f, dqkv_f, dqkv_b, dqkv_b, dqkv_b, conv_w8)


@jax.custom_vjp
def _mm_nn(a, b):
    return _nn(a, b)


@jax.custom_vjp
def _mm_nt(a, b):
    return _nt(a, b)


@jax.custom_vjp
def _mm_tn(a, b):
    return _tn(a, b)


_mm_nn.defvjp(lambda a, b: (_nn(a, b), (a, b)), lambda r, g: (_mm_nt(g, r[1]), _mm_tn(r[0], g)))
_mm_nt.defvjp(lambda a, b: (_nt(a, b), (a, b)), lambda r, g: (_mm_nn(g, r[1]), _mm_tn(g, r[0])))
_mm_tn.defvjp(lambda a, b: (_tn(a, b), (a, b)), lambda r, g: (_mm_nt(r[1], g), _mm_nn(r[0], g)))


def _each(f, *lists):
    return tuple(f(*a) for a in zip(*lists))


def _unit_tri_inv(ls, revs):
    ii = lax.broadcasted_iota(jnp.int32, (CH, CH), 0)
    jj = lax.broadcasted_iota(jnp.int32, (CH, CH), 1)
    eye = (ii == jj).astype(F32)
    xs = None
    s = 1
    while s < CH:
        same = (ii & -(2 * s)) == (jj & -(2 * s))
        off = {False: same & ((ii & s) != 0) & ((jj & s) == 0), True: same & ((jj & s) != 0) & ((ii & s) == 0)}
        cs = _each(lambda l, r: jnp.where(off[r], l, 0.0), ls, revs)
        if xs is None:
            xs = _each(lambda c: eye - c, cs)
        else:
            xc = _each(_nn, xs, cs)
            xcx = _each(_nn, xc, xs)
            xs = _each(lambda x, t: x - t, xs, xcx)
        s *= 2
    return xs


@functools.lru_cache(maxsize=None)
def _tri_solve(revs):
    @jax.custom_vjp
    def solve(ls, rhss):
        return _each(_mm_nn, _unit_tri_inv(ls, revs), rhss)

    def fwd(ls, rhss):
        ainv = _unit_tri_inv(ls, revs)
        xs = _each(_mm_nn, ainv, rhss)
        return xs, (ainv, xs)

    def bwd(res, gs):
        ainv, xs = res
        drhs = _each(_mm_tn, ainv, gs)
        return _each(lambda d, x: -_mm_nt(d, x), drhs, xs), drhs

    solve.defvjp(fwd, bwd)
    return solve


def _chunk_step(q, k, v, beta, g, s, *, revs):
    ii = lax.broadcasted_iota(jnp.int32, (CH, CH), 0)
    jj = lax.broadcasted_iota(jnp.int32, (CH, CH), 1)
    eye = ii == jj
    incl_of = {False: ii >= jj, True: ii <= jj}
    strict_of = {False: ii > jj, True: ii < jj}
    g_row = _each(lambda t: jnp.sum(jnp.where(eye, t, 0.0), axis=0, keepdims=True), g)
    cum = _each(lambda t, r: jnp.sum(jnp.where(incl_of[r], t, 0.0), axis=1, keepdims=True), g_row, revs)
    cum_row = _each(lambda t: jnp.sum(jnp.where(eye, t, 0.0), axis=0, keepdims=True), cum)
    total = _each(lambda t: jnp.sum(t, axis=0, keepdims=True), g)
    decay = _each(lambda c, cr, r: jnp.where(incl_of[r], jnp.exp(jnp.where(incl_of[r], c - cr, 0.0)), 0.0), cum, cum_row, revs)
    kb = _each(jnp.multiply, k, beta)
    vb = _each(jnp.multiply, v, beta)
    kk = _each(_mm_nt, kb, k)
    lmat = _each(lambda t, dc, r: jnp.where(strict_of[r], t * dc, 0.0), kk, decay, revs)
    ecum = _each(jnp.exp, cum)
    rhs = _each(lambda a, b, e: jnp.concatenate([a, b * e], axis=1), vb, kb, ecum)
    sol = _tri_solve(revs)(lmat, rhs)
    qk = _each(_mm_nt, q, k)
    aqk = _each(jnp.multiply, qk, decay)
    ws = _each(lambda so, st: _mm_nn(so[:, HD:], st), sol, s)
    v_new = _each(lambda so, t: so[:, :HD] - t, sol, ws)
    qs = _each(lambda a, e, st: _mm_nn(a * e, st), q, ecum, s)
    av = _each(_mm_nn, aqk, v_new)
    o = _each(jnp.add, qs, av)
    kv = _each(lambda a, t, c, vn: _mm_tn(a * jnp.exp(t - c), vn), k, total, cum, v_new)
    s_new = _each(lambda st, t, u: st * jnp.exp(t) + u, s, total, kv)
    return o, s_new


def _lane_col(x, c):
    lane = lax.broadcasted_iota(jnp.int32, x.shape, 1)
    return jnp.sum(jnp.where(lane == c, x, 0.0), axis=1, keepdims=True)


def _beta_g(ab, cst, d, h):
    braw = _lane_col(ab, NH * d + h)
    araw = _lane_col(ab, 2 * NH + NH * d + h)
    ea = _lane_col(cst[0:1, :], 2 * NH + NH * d + h)
    dt = _lane_col(cst[1:2, :], 2 * NH + NH * d + h)
    z = araw + dt
    softplus = jnp.maximum(z, 0.0) + jnp.log(1.0 + jnp.exp(-jnp.abs(z)))
    return jax.nn.sigmoid(braw), -ea * softplus, z, ea


_REVS = (False,) * NH + (True,) * NH


def _chain_inputs(refs, r0s, abvs, cst):
    hs = lambda h: slice(h * HD, (h + 1) * HD)
    chains = tuple((d, h) for d in (0, 1) for h in range(NH))
    q = _each(lambda dh: refs[dh[0]][0][pl.ds(r0s[dh[0]], CH), hs(dh[1])], chains)
    k = _each(lambda dh: refs[dh[0]][1][pl.ds(r0s[dh[0]], CH), hs(dh[1])], chains)
    v = _each(lambda dh: refs[dh[0]][2][pl.ds(r0s[dh[0]], CH), hs(dh[1])], chains)
    bg = _each(lambda dh: _beta_g(abvs[dh[0]], cst, dh[0], dh[1]), chains)
    return chains, q, k, v, bg


def _scan_fwd(qkv, ab, cst, s0, *, row_blk0, nb, name, exchange=None):
    cb = TM // CH
    w = NH * HD

    def body(qf, kf, vf, abf, qb, kb, vb, abb, cst_ref, s0_ref, of_ref, ob_ref, sallf_ref, sallb_ref, sfin_ref, s_scr):
        i = pl.program_id(0)

        @pl.when(i == 0)
        def _():
            s_scr[...] = s0_ref[...]

        o_refs, sall_refs = (of_ref, ob_ref), (sallf_ref, sallb_ref)

        def chunk(ci, carry):
            cs = (ci, cb - 1 - ci)
            r0s = tuple(pl.multiple_of(c * CH, CH) for c in cs)
            abvs = (abf[pl.ds(r0s[0], CH), :], abb[pl.ds(r0s[1], CH), :])
            chains, q, k, v, bg = _chain_inputs(((qf, kf, vf), (qb, kb, vb)), r0s, abvs, cst_ref[...])
            s = _each(lambda dh: s_scr[dh[0], dh[1]], chains)
            for (d, h), sv in zip(chains, s):
                sall_refs[d][cs[d], h] = sv
            o, s_new = _chunk_step(q, k, v, _each(lambda t: t[0], bg), _each(lambda t: t[1], bg), s, revs=_REVS)
            for (d, h), ov, sv in zip(chains, o, s_new):
                o_refs[d][pl.ds(r0s[d], CH), h * HD:(h + 1) * HD] = ov
                s_scr[d, h] = sv
            return carry

        lax.fori_loop(0, cb, chunk, 0)

        @pl.when(i == nb - 1)
        def _():
            sfin_ref[...] = s_scr[...]

    pos = (lambda i: i, lambda i: nb - 1 - i)
    col = lambda d, c: pl.BlockSpec((TM, w), lambda i: (row_blk0 + pos[d](i), c))
    abs_ = lambda d: pl.BlockSpec((TM, 128), lambda i: (row_blk0 + pos[d](i), 0))
    full4 = pl.BlockSpec((2, NH, HD, HD), lambda i: (0, 0, 0, 0))
    o_spec = lambda d: pl.BlockSpec((TM, w), lambda i: (pos[d](i), 0))
    sall_spec = lambda d: pl.BlockSpec((cb, NH, HD, HD), lambda i: (pos[d](i), 0, 0, 0))
    return _pallas(
        body, (qkv, qkv, qkv, ab, qkv, qkv, qkv, ab, cst, s0), name=name, grid=(nb,), sem=("arbitrary",), exchange=exchange,
        in_specs=[col(0, 0), col(0, 1), col(0, 2), abs_(0), col(1, 0), col(1, 1), col(1, 2), abs_(1),
                  pl.BlockSpec((8, 128), lambda i: (0, 0)), full4],
        out_specs=[o_spec(0), o_spec(1), sall_spec(0), sall_spec(1), full4],
        out_shape=[jax.ShapeDtypeStruct((nb * TM, w), F32)] * 2 + [jax.ShapeDtypeStruct((nb * cb, NH, HD, HD), F32)] * 2
        + [jax.ShapeDtypeStruct((2, NH, HD, HD), F32)],
        scratch_shapes=[pltpu.VMEM((2, NH, HD, HD), F32)])


def _scan_bwd(qkv, ab, cst, sall_f, sall_b, do, dsfin, dqkv_f, dqkv_b, dab_f, dab_b, dcst, *, row_blk0, nb, has_do, name,
              exchange=None):
    cb = TM // CH
    w = NH * HD

    def body(qf, kf, vf, abf, qb, kb, vb, abb, cst_ref, sallf_ref, sallb_ref, dof_ref, dob_ref, dsfin_ref, _f, _b, _af, _ab, dcst_in,
             dqkvf_ref, dqkvb_ref, dabf_ref, dabb_ref, dcst_ref, ds0_ref, ds_scr):
        i = pl.program_id(0)

        @pl.when(i == 0)
        def _():
            ds_scr[...] = dsfin_ref[...]
            dcst_ref[...] = dcst_in[...]

        lane = lax.broadcasted_iota(jnp.int32, (CH, 128), 1)
        lane1 = lax.broadcasted_iota(jnp.int32, (1, 128), 1)
        sall_refs, do_refs = (sallf_ref, sallb_ref), (dof_ref, dob_ref)
        dqkv_refs, dab_refs = (dqkvf_ref, dqkvb_ref), (dabf_ref, dabb_ref)

        def chunk(ci, carry):
            cs = (cb - 1 - ci, ci)
            r0s = tuple(pl.multiple_of(c * CH, CH) for c in cs)
            abvs = (abf[pl.ds(r0s[0], CH), :], abb[pl.ds(r0s[1], CH), :])
            chains, q, k, v, bg = _chain_inputs(((qf, kf, vf), (qb, kb, vb)), r0s, abvs, cst_ref[...])
            beta, g = _each(lambda t: t[0], bg), _each(lambda t: t[1], bg)
            s = _each(lambda dh: sall_refs[dh[0]][cs[dh[0]], dh[1]], chains)
            _, vjp = jax.vjp(functools.partial(_chunk_step, revs=_REVS), q, k, v, beta, g, s)
            do_all = _each(lambda dh: do_refs[dh[0]][pl.ds(r0s[dh[0]], CH), dh[1] * HD:(dh[1] + 1) * HD] if has_do
                           else jnp.zeros((CH, HD), F32), chains)
            dq, dk, dv, dbeta, dg, ds = vjp((do_all, _each(lambda dh: ds_scr[dh[0], dh[1]], chains)))
            dab = [jnp.zeros((CH, 128), F32), jnp.zeros((CH, 128), F32)]
            dal = jnp.zeros((1, 128), F32)
            for n, (d, h) in enumerate(chains):
                for part, val in enumerate((dq[n], dk[n], dv[n])):
                    dqkv_refs[d][pl.ds(r0s[d], CH), part * w + h * HD:part * w + (h + 1) * HD] = val
                ds_scr[d, h] = ds[n]
                z, ea = bg[n][2], bg[n][3]
                dbraw = dbeta[n] * beta[n] * (1.0 - beta[n])
                daraw = dg[n] * (-ea) * jax.nn.sigmoid(z)
                dab[d] = dab[d] + jnp.where(lane == NH * d + h, dbraw, 0.0) + jnp.where(lane == 2 * NH + NH * d + h, daraw, 0.0)
                dal = dal + jnp.where(lane1 == 2 * NH + NH * d + h, jnp.sum(dg[n] * g[n], axis=0, keepdims=True), 0.0)
            for d in (0, 1):
                dab_refs[d][pl.ds(r0s[d], CH), :] = dab[d]
            dcst_ref[0:1, :] += dal
            dcst_ref[1:2, :] += jnp.sum(jnp.where(lane >= 2 * NH, dab[0] + dab[1], 0.0), axis=0, keepdims=True)
            return carry

        lax.fori_loop(0, cb, chunk, 0)

        @pl.when(i == nb - 1)
        def _():
            ds0_ref[...] = ds_scr[...]

    pos = (lambda i: nb - 1 - i, lambda i: i)
    col = lambda d, c: pl.BlockSpec((TM, w), lambda i: (row_blk0 + pos[d](i), c))
    abs_ = lambda d: pl.BlockSpec((TM, 128), lambda i: (row_blk0 + pos[d](i), 0))
    full4 = pl.BlockSpec((2, NH, HD, HD), lambda i: (0, 0, 0, 0))
    small = pl.BlockSpec((8, 128), lambda i: (0, 0))
    hbm = pl.BlockSpec(memory_space=pl.ANY)
    sall_spec = lambda d: pl.BlockSpec((cb, NH, HD, HD), lambda i: (pos[d](i), 0, 0, 0))
    do_spec = (lambda d: pl.BlockSpec((TM, w), lambda i: (pos[d](i), 0))) if has_do else (lambda d: small)
    acc_specs = [pl.BlockSpec((TM, 3 * w), lambda i: (row_blk0 + pos[0](i), 0)),
                 pl.BlockSpec((TM, 3 * w), lambda i: (row_blk0 + pos[1](i), 0)), abs_(0), abs_(1), small]
    return _pallas(
        body, (qkv, qkv, qkv, ab, qkv, qkv, qkv, ab, cst, sall_f, sall_b, do, do, dsfin, dqkv_f, dqkv_b, dab_f, dab_b, dcst),
        name=name, grid=(nb,), sem=("arbitrary",), exchange=exchange,
        in_specs=[col(0, 0), col(0, 1), col(0, 2), abs_(0), col(1, 0), col(1, 1), col(1, 2), abs_(1), small,
                  sall_spec(0), sall_spec(1), do_spec(0), do_spec(1), full4, hbm, hbm, hbm, hbm, small],
        out_specs=acc_specs + [full4],
        out_shape=[jax.ShapeDtypeStruct(dqkv_f.shape, F32), jax.ShapeDtypeStruct(dqkv_b.shape, F32),
                   jax.ShapeDtypeStruct(dab_f.shape, F32), jax.ShapeDtypeStruct(dab_b.shape, F32),
                   jax.ShapeDtypeStruct((8, 128), F32), jax.ShapeDtypeStruct((2, NH, HD, HD), F32)],
        aliases={14: 0, 15: 1, 16: 2, 17: 3, 18: 4},
        scratch_shapes=[pltpu.VMEM((2, NH, HD, HD), F32)])


def _pool(xin, *, row0, transpose, name):
    n_tok = xin.shape[0] - row0
    rows = n_tok // GW
    pad = 8 * GW
    tt = 512
    gsh = GW.bit_length() - 1

    def body(x_ref, o_ref, ybuf):
        ii = lax.broadcasted_iota(jnp.int32, (128, 128), 0)
        jj = lax.broadcasted_iota(jnp.int32, (128, 128), 1)
        same_row = (ii >> gsh) == (jj >> gsh)
        ci, cj = ii & (GW - 1), jj & (GW - 1)
        tok = lax.broadcasted_iota(jnp.int32, (tt, 1), 0)
        zpad = jnp.zeros((pad, 128), F32)
        for gi, wdw in enumerate(POOL_WINDOWS):
            lo, hi = wdw // 2, wdw - wdw // 2
            if transpose:
                band = same_row & (ci - cj >= -lo) & (ci - cj < hi)
                offs = range(-hi + 1, lo + 1)
            else:
                band = same_row & (cj - ci >= -lo) & (cj - ci < hi)
                offs = range(-lo, hi)
            bandm = band.astype(BF)
            cs = slice(gi * 128, (gi + 1) * 128)
            ybuf[0:pad, :] = zpad
            ybuf[pad + n_tok:, :] = zpad

            def inv_area(t0):
                t = t0 + tok
                r, c = t >> gsh, t & (GW - 1)
                nr = jnp.minimum(r + hi, rows) - jnp.maximum(r - lo, 0)
                nc = jnp.minimum(c + hi, GW) - jnp.maximum(c - lo, 0)
                return 1.0 / (nr * nc).astype(F32)

            def col_pass(b, carry):
                t0 = pl.multiple_of(b * tt, tt)
                xv = x_ref[pl.ds(row0 + t0, tt), cs]
                if transpose:
                    xv = xv * inv_area(t0)
                hi_part = xv.astype(BF)
                lo_part = (xv - hi_part.astype(F32)).astype(BF)
                for s in range(tt // 128):
                    sl = slice(s * 128, (s + 1) * 128)
                    y = (jnp.dot(bandm, hi_part[sl], preferred_element_type=F32)
                         + jnp.dot(bandm, lo_part[sl], preferred_element_type=F32))
                    ybuf[pl.ds(pad + t0 + s * 128, 128), :] = y
                return carry

            lax.fori_loop(0, n_tok // tt, col_pass, 0)

            def row_pass(b, carry):
                t0 = pl.multiple_of(b * tt, tt)
                acc = ybuf[pl.ds(pad + t0 + offs[0] * GW, tt), :]
                for dr in offs[1:]:
                    acc = acc + ybuf[pl.ds(pad + t0 + dr * GW, tt), :]
                xv = x_ref[pl.ds(row0 + t0, tt), cs]
                if not transpose:
                    acc = acc * inv_area(t0)
                o_ref[pl.ds(t0, tt), cs] = acc - xv
                return carry

            lax.fori_loop(0, n_tok // tt, row_pass, 0)

    return pl.pallas_call(
        body, name=name, out_shape=jax.ShapeDtypeStruct((n_tok, NPOOL), F32),
        in_specs=[pl.BlockSpec(memory_space=pltpu.VMEM)], out_specs=pl.BlockSpec(memory_space=pltpu.VMEM),
        scratch_shapes=[pltpu.VMEM((n_tok + 2 * pad, 128), F32)],
        compiler_params=pltpu.CompilerParams(vmem_limit_bytes=VMEM_LIMIT),
    )(xin)


def _merge_parts(of, ob, pgate, pd, br, gnw, pw_ref, pscale, wg_ref, wp_ref):
    o = of + ob
    ons, ohs, rs = [], [], []
    for h in range(NH):
        oh = o[:, h * HD:(h + 1) * HD]
        r = lax.rsqrt(jnp.mean(oh * oh, axis=-1, keepdims=True) + EPS)
        ohs.append(oh * r)
        rs.append(r)
        ons.append(oh * r * gnw)
    on = jnp.concatenate(ons, axis=1)
    og = on * _silu(pgate)
    y_gdn = _nn(og, wg_ref[...])
    ypre = jnp.concatenate([_nn(pd[:, g * 128:(g + 1) * 128], pw_ref[g]) for g in range(4)], axis=1)
    yp = ypre * pscale
    y_pool = _nn(yp, wp_ref[...])
    g_pool = jax.nn.sigmoid(br[:, :D])
    g_gdn = jax.nn.sigmoid(br[:, D:])
    return dict(on=on, ohs=ohs, rs=rs, og=og, y_gdn=y_gdn, ypre=ypre, yp=yp, y_pool=y_pool, g_pool=g_pool, g_gdn=g_gdn)


def _merge_fwd(x1, of, ob, pgate, pd, br, modv, gnw, pool_w, pscale, w_gdn, w_pool, w_mo, *, name):
    n_tok = of.shape[0]

    def body(x_ref, of_ref, ob_ref, pg_ref, pd_ref, br_ref, mod_ref, gnw_ref, pw_ref, ps_ref, wg_ref, wp_ref, wmo_ref,
             x2_ref, og_ref, yp_ref, m_ref, mix_ref):
        t = _merge_parts(of_ref[...], ob_ref[...], pg_ref[...], pd_ref[...], br_ref[...], gnw_ref[...], pw_ref, ps_ref[...],
                         wg_ref, wp_ref)
        m = t["g_pool"] * t["y_pool"] + t["g_gdn"] * t["y_gdn"]
        mix = _nn(m, wmo_ref[...])
        og_ref[...] = t["og"].astype(BF)
        yp_ref[...] = t["yp"].astype(BF)
        m_ref[...] = m.astype(BF)
        mix_ref[...] = mix.astype(BF)
        x2_ref[...] = x_ref[...] + mod_ref[0, 5:6, :] * mix

    tile = lambda w: pl.BlockSpec((TM, w), lambda i: (i, 0))
    ctile = lambda w: pl.BlockSpec((TM, w), lambda i: (i + 1, 0))
    return pl.pallas_call(
        body, name=name, grid=(n_tok // TM,),
        in_specs=[ctile(D), tile(D), tile(D), ctile(D), tile(NPOOL), ctile(2 * D),
                  pl.BlockSpec((1, 16, D), lambda i: (1, 0, 0)), _const_spec((1, HD)), _const_spec((4, 128, 128)),
                  _const_spec((1, NPOOL)), _const_spec((D, D)), _const_spec((NPOOL, D)), _const_spec((D, D))],
        out_specs=[tile(D), tile(D), tile(NPOOL), tile(D), tile(D)],
        out_shape=[jax.ShapeDtypeStruct((n_tok, D), F32), jax.ShapeDtypeStruct((n_tok, D), BF),
                   jax.ShapeDtypeStruct((n_tok, NPOOL), BF), jax.ShapeDtypeStruct((n_tok, D), BF),
                   jax.ShapeDtypeStruct((n_tok, D), BF)],
        compiler_params=_cparams("parallel"),
    )(x1, of, ob, pgate, pd, br, modv, gnw, pool_w, pscale, w_gdn, w_pool, w_mo)


def _merge_bwd(dx2, mix, of, ob, pgate, pd, br, modv, gnw, pool_w, pscale, w_gdn, w_pool, w_mo, *, name):
    n_tok = of.shape[0]

    def body(dx2_ref, mix_ref, of_ref, ob_ref, pg_ref, pd_ref, br_ref, mod_ref, gnw_ref, pw_ref, ps_ref, wg_ref, wp_ref, wmo_ref,
             do_ref, dgate_ref, dpd_ref, dbr_ref, dmix_ref, dyg_ref, dyp_ref, acc_ref, dpw_ref):
        i = pl.program_id(0)
        pgate, pdv, gnw = pg_ref[...], pd_ref[...], gnw_ref[...]
        t = _merge_parts(of_ref[...], ob_ref[...], pgate, pdv, br_ref[...], gnw, pw_ref, ps_ref[...], wg_ref, wp_ref)
        dx2v = dx2_ref[...]
        dmix = mod_ref[0, 5:6, :] * dx2v
        dmixb = dmix.astype(BF)
        dmix_ref[...] = dmixb
        dm = _nt(dmixb, wmo_ref[...])
        gp, gg = t["g_pool"], t["g_gdn"]
        dbr_ref[:, :D] = dm * t["y_pool"] * gp * (1.0 - gp)
        dbr_ref[:, D:] = dm * t["y_gdn"] * gg * (1.0 - gg)
        dyp = (dm * gp).astype(BF)
        dyg = (dm * gg).astype(BF)
        dyp_ref[...] = dyp
        dyg_ref[...] = dyg
        dyp_in = _nt(dyp, wp_ref[...])
        dypre = dyp_in * ps_ref[...]
        for g in range(4):
            gs = slice(g * 128, (g + 1) * 128)
            dpd_ref[:, gs] = _nt(dypre[:, gs], pw_ref[g])
        dog = _nt(dyg, wg_ref[...])
        dgate_ref[...] = dog * t["on"] * _dsilu(pgate)
        don = dog * _silu(pgate)
        dgnw = jnp.zeros((1, HD), F32)
        for h in range(NH):
            hs = slice(h * HD, (h + 1) * HD)
            donh, oh, r = don[:, hs], t["ohs"][h], t["rs"][h]
            dgnw = dgnw + jnp.sum(donh * oh, axis=0, keepdims=True)
            doh = donh * gnw
            do_ref[:, hs] = r * (doh - oh * jnp.mean(doh * oh, axis=-1, keepdims=True))

        @pl.when(i == 0)
        def _():
            acc_ref[...] = jnp.zeros_like(acc_ref)
            dpw_ref[...] = jnp.zeros_like(dpw_ref)

        acc_ref[0:1, :] += jnp.sum(dx2v * mix_ref[...].astype(F32), axis=0, keepdims=True)
        acc_ref[1:2, 0:HD] += dgnw
        acc_ref[2:3, 0:NPOOL] += jnp.sum(dyp_in * t["ypre"], axis=0, keepdims=True)
        for g in range(4):
            gs = slice(g * 128, (g + 1) * 128)
            dpw_ref[g] += _tn(pdv[:, gs], dypre[:, gs])

    tile = lambda w: pl.BlockSpec((TM, w), lambda i: (i, 0))
    ctile = lambda w: pl.BlockSpec((TM, w), lambda i: (i + 1, 0))
    return pl.pallas_call(
        body, name=name, grid=(n_tok // TM,),
        in_specs=[tile(D), tile(D), tile(D), tile(D), ctile(D), tile(NPOOL), ctile(2 * D),
                  pl.BlockSpec((1, 16, D), lambda i: (1, 0, 0)), _const_spec((1, HD)), _const_spec((4, 128, 128)),
                  _const_spec((1, NPOOL)), _const_spec((D, D)), _const_spec((NPOOL, D)), _const_spec((D, D))],
        out_specs=[tile(D), tile(D), tile(NPOOL), tile(2 * D), tile(D), tile(D), tile(D),
                   pl.BlockSpec((8, D), lambda i: (0, 0)), pl.BlockSpec((4, 128, 128), lambda i: (0, 0, 0))],
        out_shape=[jax.ShapeDtypeStruct((n_tok, D), F32), jax.ShapeDtypeStruct((n_tok, D), F32),
                   jax.ShapeDtypeStruct((n_tok, NPOOL), F32), jax.ShapeDtypeStruct((n_tok, 2 * D), F32),
                   jax.ShapeDtypeStruct((n_tok, D), BF), jax.ShapeDtypeStruct((n_tok, D), BF), jax.ShapeDtypeStruct((n_tok, D), BF),
                   jax.ShapeDtypeStruct((8, D), F32), jax.ShapeDtypeStruct((4, 128, 128), F32)],
        compiler_params=_cparams("arbitrary"),
    )(dx2, mix, of, ob, pgate, pd, br, modv, gnw, pool_w, pscale, w_gdn, w_pool, w_mo)


def _final(x3, target, fnw, *, name):
    n_tok = x3.shape[0]

    def body(x_ref, t_ref, w_ref, dx_ref, acc_ref):
        xv, w = x_ref[...], w_ref[...]
        r = lax.rsqrt(jnp.mean(xv * xv, axis=-1, keepdims=True) + EPS)
        xh = xv * r
        err = xh * w - t_ref[...]
        dy = err * (1.0 / D)
        dxh = dy * w
        dx_ref[...] = r * (dxh - xh * jnp.mean(dxh * xh, axis=-1, keepdims=True))

        @pl.when(pl.program_id(0) == 0)
        def _():
            acc_ref[...] = jnp.zeros_like(acc_ref)

        acc_ref[0:1, :] += jnp.sum(dy * xh, axis=0, keepdims=True)
        acc_ref[1:2, :] += jnp.sum(err * err, axis=0, keepdims=True) * (0.5 / D)

    tile = pl.BlockSpec((TM, D), lambda i: (i, 0))
    return pl.pallas_call(
        body, name=name, grid=(n_tok // TM,),
        in_specs=[tile, tile, _const_spec((1, D))],
        out_specs=[tile, pl.BlockSpec((8, D), lambda i: (0, 0))],
        out_shape=[jax.ShapeDtypeStruct((n_tok, D), F32), jax.ShapeDtypeStruct((8, D), F32)],
        compiler_params=_cparams("arbitrary"),
    )(x3, target, fnw)


def _split(results, n):
    return (*results[:n], list(results[n:]))


def _mixer_weights(w_mix_in, w_gdn, w_pool, w_mo, conv):
    return dict(w_mix=_regroup_mix(_from_chip_major_cols(w_mix_in)), w_gdn=w_gdn.reshape(D, D), w_pool=_from_chip_major_cols(w_pool),
                w_mo=w_mo.reshape(D, D), conv=jnp.pad(_from_chip_major_cols(conv), ((0, 3), (0, 0))))


def _local_step(xc, target, modv, p, late=None):
    n_all = xc.shape[0]
    t_lat = n_all - TM
    nbx = t_lat // TM
    mod_lat = modv[1:2]
    gather = (lambda arrs: _ChipExchange(arrs, False)) if late else (lambda arrs: None)
    scatter = (lambda arrs: _ChipExchange(arrs, True)) if late else (lambda arrs: None)

    x1, h1, gu1, f1, *got = _ffn_fwd(xc, modv, p["norm1"], p["w1_in"], p["w1_out"], mrow=0, name="ffn1_fwd",
                                     exchange=gather(late and late[0]))
    if late:
        p = {**p, **_mixer_weights(*got)}
    u, p_qkv, p_gate, p_pool, p_br, p_ab = _mix_in_fwd(x1, modv, p["norm2"], p["w_mix"], name="mix_in_fwd")
    qkv = _prep_fwd(p_qkv, p["conv"], name="prep_fwd")
    s_zero = jnp.zeros((2, NH, HD, HD), F32)
    _, _, sall_cf, sall_cb, s_ctx = _scan_fwd(qkv, p_ab, p["cst"], s_zero, row_blk0=0, nb=1, name="scan_ctx")
    o_f, o_b, sall_f, sall_b, _, *got = _scan_fwd(qkv, p_ab, p["cst"], s_ctx, row_blk0=1, nb=nbx, name="scan_lat",
                                                  exchange=gather(late and late[1]))
    if late:
        p = {**p, "w2_in": got[0], "w2_out": got[1].reshape(FF, D)}
    pd = _pool(p_pool, row0=TM, transpose=False, name="pool_fwd")
    merge_w = (modv, p["gnw"], p["pool_w"], p["pscale"], p["w_gdn"], p["w_pool"], p["w_mo"])
    x2, og, yp, m, mix = _merge_fwd(x1, o_f, o_b, p_gate, pd, p_br, *merge_w, name="merge_fwd")
    x3, h3, gu3, f3 = _ffn_fwd(x2, mod_lat, p["norm3"], p["w2_in"], p["w2_out"], mrow=6, name="ffn2_fwd")
    dx3, acc_fin = _final(x3, target, p["fnorm"], name="final")

    dx2, a3, df3, dgu3, acc3 = _ffn_bwd(dx3, x2, gu3, f3, mod_lat, p["norm3"], p["w2_in"], p["w2_out"], mrow=6, dxo_off=0,
                                        name="ffn2_bwd")
    g = {}
    tkl = _k_tile(t_lat)
    g["w2_out"] = _matmul_tn(a3, df3, tmm=FF // 2, tn=D, tk=tkl, name="ffn2_wout_grad").reshape(NCHIP, FF // NCHIP, D)
    g["w2_in"] = _matmul_tn(h3, dgu3, tmm=D, tn=2 * FF // NCHIP, tk=tkl, nsplit=NCHIP, name="ffn2_win_grad")
    do, dgate, dpd, dbr, dmix, dyg, dyp, acc_m, dpw = _merge_bwd(dx2, mix, o_f, o_b, p_gate, pd, p_br, *merge_w, name="merge_bwd")
    g["w_mo"] = _matmul_tn(m, dmix, tmm=D, tn=D, tk=tkl, name="wmo_grad").reshape(NCHIP, D // NCHIP, D)
    g["w_gdn"] = _matmul_tn(og, dyg, tmm=D, tn=D, tk=tkl, name="wgdn_grad").reshape(NCHIP, D // NCHIP, D)
    g["w_pool"] = _matmul_tn(yp, dyp, tmm=NPOOL, tn=D // NCHIP, tk=tkl, nsplit=NCHIP, name="wpool_grad")
    dpool_in = _pool(dpd, row0=0, transpose=True, name="pool_bwd")
    acc = (lax.empty((n_all, NQKV), F32), lax.empty((n_all, NQKV), F32), lax.empty((n_all, 128), F32),
           lax.empty((n_all, 128), F32), jnp.zeros((8, 128), F32))
    behind_scan = ("w2_in", "w2_out", "w_gdn", "w_pool", "w_mo")
    *acc, ds_ctx, landed = _split(_scan_bwd(qkv, p_ab, p["cst"], sall_f, sall_b, do, s_zero, *acc, row_blk0=1, nb=nbx, has_do=True,
                                            name="scan_lat_bwd", exchange=scatter([g[k] for k in behind_scan])), 6)
    landed = dict(zip(behind_scan, landed))
    dqkv_f, dqkv_b, dab_f, dab_b, dcst, _ = _scan_bwd(qkv, p_ab, p["cst"], sall_cf, sall_cb, jnp.zeros((8, 128), F32), ds_ctx, *acc,
                                                      row_blk0=0, nb=1, has_do=False, name="scan_ctx_bwd")
    dpqkv, dconv = _prep_bwd(p_qkv, dqkv_f, dqkv_b, p["conv"], name="prep_bwd")
    dx1, dp, acc_mix = _mix_in_bwd(dx2, x1, dpqkv, dgate, dpool_in, dbr, dab_f, dab_b, modv, p["norm2"], p["w_mix"],
                                   name="mix_in_bwd")
    tka = _k_tile(n_all)
    g["w_mix"] = _chip_major_cols(_ungroup_mix(_matmul_tn(u, dp, tmm=256, tn=NMIXP, tk=_k_tile(n_all, 1024), name="wmix_grad")))
    dxc, a1, df1, dgu1, acc1, got = _split(_ffn_bwd(dx1, xc, gu1, f1, modv, p["norm1"], p["w1_in"], p["w1_out"], mrow=0, dxo_off=0,
                                                   name="ffn1_bwd", exchange=scatter([g["w_mix"]])), 5)
    landed.update(zip(("w_mix",), got))
    g["w1_out"] = _matmul_tn(a1, df1, tmm=FF // 2, tn=D, tk=tka, name="ffn1_wout_grad").reshape(NCHIP, FF // NCHIP, D)
    g["w1_in"] = _matmul_tn(h1, dgu1, tmm=D, tn=2 * FF // NCHIP, tk=tka, nsplit=NCHIP, name="ffn1_win_grad")

    small = dict(norm1=acc1[0, 3] + acc1[1, 3], norm2=acc_mix[0, 3] + acc_mix[1, 3], norm3=acc3[0, 3], fnorm=acc_fin[0],
                 gnw=acc_m[1, :HD], pscale=acc_m[2, :NPOOL], pool_w=dpw, conv=dconv[:5],
                 a_log=dcst[0, 2 * NH:4 * NH], dt_bias=dcst[1, 2 * NH:4 * NH])
    zero = jnp.zeros((D,), F32)
    dmod = jnp.stack([
        jnp.stack([acc1[0, 0], acc1[0, 1], acc1[0, 2], acc_mix[0, 0], acc_mix[0, 1], zero, zero, zero, zero]),
        jnp.stack([acc1[1, 0], acc1[1, 1], acc1[1, 2], acc_mix[1, 0], acc_mix[1, 1], acc_m[0], acc3[0, 0], acc3[0, 1], acc3[0, 2]]),
    ])
    return jnp.sum(acc_fin[1]), dxc, g, landed, small, dmod


_HI = lax.Precision.HIGHEST


def _ada_fwd(c_all, w_sh, b_sh, *, name):
    def body(c_ref, w_ref, b_ref, o_ref):
        o_ref[...] = jnp.dot(_silu(c_ref[...]), w_ref[...], precision=_HI, preferred_element_type=F32) + b_ref[...]

    return pl.pallas_call(body, name=name, out_shape=jax.ShapeDtypeStruct((16, w_sh.shape[1]), F32),
                          compiler_params=pltpu.CompilerParams(vmem_limit_bytes=VMEM_LIMIT))(c_all, w_sh, b_sh)


def _ada_bwd(c_all, dm, w_sh, *, name):
    def body(c_ref, dm_ref, w_ref, dw_ref, dc_ref):
        sc = _silu(c_ref[...])
        dw_ref[...] = lax.dot_general(sc, dm_ref[...], (((0,), (0,)), ((), ())), precision=_HI, preferred_element_type=F32)
        part = lax.dot_general(dm_ref[8:9, :], w_ref[...], (((1,), (1,)), ((), ())), precision=_HI, preferred_element_type=F32)
        dc_ref[...] = jnp.broadcast_to(part, dc_ref.shape)

    return pl.pallas_call(body, name=name,
                          out_shape=[jax.ShapeDtypeStruct(w_sh.shape, F32), jax.ShapeDtypeStruct((8, D), F32)],
                          compiler_params=pltpu.CompilerParams(vmem_limit_bytes=VMEM_LIMIT))(c_all, dm, w_sh)


def _cctx_grad(parts, c_ctx, *, name):
    def body(p_ref, c_ref, o_ref):
        tot = (p_ref[0, 0:1, :] + p_ref[2, 0:1, :]) + (p_ref[4, 0:1, :] + p_ref[6, 0:1, :])
        o_ref[...] = tot * _dsilu(c_ref[...])

    return pl.pallas_call(body, name=name, out_shape=jax.ShapeDtypeStruct((1, D), F32))(parts, c_ctx)


_MESH = pl.DeviceIdType.MESH
_ANY = pl.BlockSpec(memory_space=pl.ANY)


def _flip(v, bit):
    return (1 - v) if bit else v


def _all_gather8(x, *, name):
    def body(x_ref, out_ref, send_sems, recv_sems, local_sem):
        mx, my, mc = lax.axis_index("x"), lax.axis_index("y"), lax.axis_index("c")
        me = 4 * mx + 2 * my + mc
        mine = pltpu.make_async_copy(x_ref, out_ref.at[me], local_sem)
        mine.start()
        sends, recvs = [], []
        for k in range(1, 8):
            px, py, pc = _flip(mx, k & 4), _flip(my, k & 2), _flip(mc, k & 1)
            sends.append(pltpu.make_async_remote_copy(src_ref=x_ref, dst_ref=out_ref.at[me], send_sem=send_sems.at[k - 1],
                                                      recv_sem=recv_sems.at[k - 1], device_id=(px, py, pc), device_id_type=_MESH))
            recvs.append(pltpu.make_async_remote_copy(src_ref=x_ref, dst_ref=out_ref.at[4 * px + 2 * py + pc],
                                                      send_sem=send_sems.at[k - 1], recv_sem=recv_sems.at[k - 1],
                                                      device_id=(px, py, pc), device_id_type=_MESH))
        for cp in sends:
            cp.start()
        for cp in recvs:
            cp.wait_recv()
        for cp in sends:
            cp.wait_send()
        mine.wait()

    vm = pl.BlockSpec(memory_space=pltpu.VMEM)
    return pl.pallas_call(
        body, name=name, out_shape=jax.ShapeDtypeStruct((8,) + x.shape, x.dtype), in_specs=[vm], out_specs=vm,
        scratch_shapes=[pltpu.SemaphoreType.DMA((7,)), pltpu.SemaphoreType.DMA((7,)), pltpu.SemaphoreType.DMA],
        compiler_params=pltpu.CompilerParams(vmem_limit_bytes=VMEM_LIMIT),
    )(x)


class _ChipExchange:
    def __init__(self, arrs, scatter):
        self.arrs, self.scatter, self.n = list(arrs), scatter, len(arrs)
        self.out_shape = [jax.ShapeDtypeStruct(a.shape if scatter else (NCHIP,) + a.shape, a.dtype) for a in self.arrs]
        links = self.n * (NCHIP - 1)
        self.scratch = [pltpu.SemaphoreType.DMA((links,)), pltpu.SemaphoreType.DMA((links,)), pltpu.SemaphoreType.DMA((self.n,))]

    def copies(self, ins, outs, send_sems, recv_sems, local_sems):
        mx, my, mc = lax.axis_index("x"), lax.axis_index("y"), lax.axis_index("c")
        me = 2 * mx + my
        local, sends, recvs = [], [], []
        for j in range(self.n):
            src_own = ins[j].at[me] if self.scatter else ins[j]
            local.append(pltpu.make_async_copy(src_own, outs[j].at[me], local_sems.at[j]))
            for k in range(1, NCHIP):
                px, py = _flip(mx, k & 2), _flip(my, k & 1)
                peer = 2 * px + py
                sem = j * (NCHIP - 1) + k - 1
                src = ins[j].at[peer] if self.scatter else ins[j]
                sends.append(pltpu.make_async_remote_copy(src_ref=src, dst_ref=outs[j].at[me], send_sem=send_sems.at[sem],
                                                          recv_sem=recv_sems.at[sem], device_id=(px, py, mc), device_id_type=_MESH))
                recvs.append(pltpu.make_async_remote_copy(src_ref=src, dst_ref=outs[j].at[peer], send_sem=send_sems.at[sem],
                                                          recv_sem=recv_sems.at[sem], device_id=(px, py, mc), device_id_type=_MESH))
        return local, sends, recvs

    @staticmethod
    def start(local, sends, recvs):
        for cp in local + sends:
            cp.start()

    @staticmethod
    def finish(local, sends, recvs):
        for cp in recvs:
            cp.wait_recv()
        for cp in sends:
            cp.wait_send()
        for cp in local:
            cp.wait()


def _chip_exchange(arrs, *, scatter, name):
    ex = _ChipExchange(arrs, scatter)

    def body(*refs):
        cps = ex.copies(refs[:ex.n], refs[ex.n:2 * ex.n], *refs[2 * ex.n:])
        ex.start(*cps)
        ex.finish(*cps)

    return pl.pallas_call(body, name=name, out_shape=ex.out_shape, in_specs=[_ANY] * ex.n, out_specs=[_ANY] * ex.n,
                          scratch_shapes=ex.scratch)(*arrs)


def _pallas(body, operands, *, name, grid, in_specs, out_specs, out_shape, sem, scratch_shapes=(), aliases=None, exchange=None):
    if exchange is None:
        return pl.pallas_call(body, name=name, grid=grid, in_specs=in_specs, out_specs=out_specs, out_shape=out_shape,
                              scratch_shapes=list(scratch_shapes), input_output_aliases=aliases or {},
                              compiler_params=_cparams(*sem))(*operands)
    ex = exchange
    (steps,) = grid
    n_in, n_out, n_scr, k = len(in_specs), len(out_specs), len(scratch_shapes), ex.n

    def hosted(*refs):
        ins, refs = refs[:n_in], refs[n_in:]
        ex_in, refs = refs[:k], refs[k:]
        outs, refs = refs[:n_out], refs[n_out:]
        ex_out, refs = refs[:k], refs[k:]
        scr, ex_sems = refs[:n_scr], refs[n_scr:]
        cps = ex.copies(ex_in, ex_out, *ex_sems)
        pl.when(pl.program_id(0) == 0)(lambda: ex.start(*cps))
        body(*ins, *outs, *scr)
        pl.when(pl.program_id(0) == steps - 1)(lambda: ex.finish(*cps))

    return pl.pallas_call(
        hosted, name=name, grid=grid, in_specs=list(in_specs) + [_ANY] * k, out_specs=list(out_specs) + [_ANY] * k,
        out_shape=list(out_shape) + ex.out_shape, scratch_shapes=list(scratch_shapes) + ex.scratch,
        input_output_aliases=aliases or {}, compiler_params=_cparams("arbitrary"),
    )(*operands, *ex.arrs)


def _core_swap(arrs, *, name):
    n = len(arrs)

    def body(*refs):
        ins, outs = refs[:n], refs[n:2 * n]
        send_sems, recv_sems = refs[2 * n:]
        sib = (lax.axis_index("x"), lax.axis_index("y"), 1 - lax.axis_index("c"))
        cps = [pltpu.make_async_remote_copy(src_ref=ins[j], dst_ref=outs[j], send_sem=send_sems.at[j], recv_sem=recv_sems.at[j],
                                            device_id=sib, device_id_type=_MESH) for j in range(n)]
        for cp in cps:
            cp.start()
        for cp in cps:
            cp.wait_recv()
        for cp in cps:
            cp.wait_send()

    return pl.pallas_call(
        body, name=name, out_shape=[jax.ShapeDtypeStruct(a.shape, a.dtype) for a in arrs],
        in_specs=[_ANY] * n, out_specs=[_ANY] * n,
        scratch_shapes=[pltpu.SemaphoreType.DMA((n,)), pltpu.SemaphoreType.DMA((n,))],
    )(*arrs)


def _row_tile(rows, cols, budget=1 << 18):
    best = None
    for t in range(8, rows + 1, 8):
        if rows % t == 0 and t * cols <= budget:
            best = t
    return best or rows


def _sum_slots(x, *, name):
    ns, r, c = x.shape
    tr = _row_tile(r, c * ns)

    def body(x_ref, o_ref):
        acc = x_ref[0].astype(F32)
        for s in range(1, ns):
            acc = acc + x_ref[s].astype(F32)
        o_ref[...] = acc

    return pl.pallas_call(
        body, name=name, grid=(r // tr,), out_shape=jax.ShapeDtypeStruct((r, c), F32),
        in_specs=[pl.BlockSpec((ns, tr, c), lambda i: (0, i, 0))], out_specs=pl.BlockSpec((tr, c), lambda i: (i, 0)),
        compiler_params=_cparams("parallel"),
    )(x)


def _adamw(w, ga, gb, m, v, *, name):
    r, c = w.shape
    tr = _row_tile(r, c, budget=1 << 17)
    two = gb is not None

    def body(*refs):
        w_ref, ga_ref = refs[0], refs[1]
        m_ref, v_ref = refs[2 + two], refs[3 + two]
        g_ref, d_ref, mo_ref, vo_ref = refs[4 + two:]
        g = ga_ref[...] + refs[2][...] if two else ga_ref[...]
        mn = ADAM_B1 * m_ref[...] + (1.0 - ADAM_B1) * g
        vn = ADAM_B2 * v_ref[...] + (1.0 - ADAM_B2) * (g * g)
        m_hat = mn / (1.0 - ADAM_B1 ** ADAM_STEP)
        v_hat = vn / (1.0 - ADAM_B2 ** ADAM_STEP)
        g_ref[...] = g
        d_ref[...] = -ADAM_LR * (m_hat / (jnp.sqrt(v_hat) + ADAM_EPS) + ADAM_WD * w_ref[...])
        mo_ref[...] = mn
        vo_ref[...] = vn

    spec = pl.BlockSpec((tr, c), lambda i: (i, 0))
    ins = [w, ga] + ([gb] if two else []) + [m, v]
    return pl.pallas_call(
        body, name=name, grid=(r // tr,), out_shape=[jax.ShapeDtypeStruct((r, c), F32)] * 4,
        in_specs=[spec] * len(ins), out_specs=[spec] * 4, compiler_params=_cparams("parallel"),
    )(*ins)


_MIX_AB0, _MIX_AB1 = NQKV, NQKV + 4 * NH


def _regroup_mix(w):
    pad = jnp.zeros((w.shape[0], NMIXP - NMIX), w.dtype)
    return jnp.concatenate([w[:, :_MIX_AB0], w[:, _MIX_AB1:], w[:, _MIX_AB0:_MIX_AB1], pad], axis=1)


def _ungroup_mix(w):
    n_ab = _MIX_AB1 - _MIX_AB0
    return jnp.concatenate([w[:, :_MIX_AB0], w[:, NMIX - n_ab:NMIX], w[:, _MIX_AB0:NMIX - n_ab]], axis=1)


def _chip_major_cols(w):
    r, c = w.shape
    return w.reshape(r, NCHIP, c // NCHIP).transpose(1, 0, 2)


def _from_chip_major_cols(w):
    return w.transpose(1, 0, 2).reshape(w.shape[1], -1)


_SMALL = (("c_ctx", D), ("b_ada", 9 * D), ("norm1_w", D), ("norm2_w", D), ("norm3_w", D), ("final_norm_w", D),
          ("a_log", 2 * NH), ("dt_bias", 2 * NH), ("gdn_norm_w", HD), ("pool_w", 4 * 128 * 128), ("pool_scale", NPOOL),
          ("conv_w", 5 * NQKV // NCHIP))


def _pack(vals, lanes=128, row_mult=8):
    flat = jnp.concatenate([jnp.ravel(v) for v in vals])
    n = flat.shape[0]
    rows = -(-n // (lanes * row_mult)) * row_mult
    return jnp.pad(flat, (0, rows * lanes - n)).reshape(rows, lanes)


def _unpack(packed, sizes):
    flat = packed.reshape(-1)
    out, o = [], 0
    for n in sizes:
        out.append(flat[o:o + n])
        o += n
    return out


def kernel(x, c, ctx, c_ctx, w_ada, b_ada, norm1_w, ffn1_w_in, ffn1_w_out, norm2_w, w_mix_in, conv_w, a_log, dt_bias, gdn_norm_w, w_gdn_proj, pool_w, pool_scale, w_pool_proj, w_mix_out, norm3_w, ffn2_w_in, ffn2_w_out, final_norm_w, loss_target, m_c_ctx, m_w_ada, m_b_ada, m_norm1_w, m_ffn1_w_in, m_ffn1_w_out, m_norm2_w, m_w_mix_in, m_conv_w, m_a_log, m_dt_bias, m_gdn_norm_w, m_w_gdn_proj, m_pool_w, m_pool_scale, m_w_pool_proj, m_w_mix_out, m_norm3_w, m_ffn2_w_in, m_ffn2_w_out, m_final_norm_w, v_c_ctx, v_w_ada, v_b_ada, v_norm1_w, v_ffn1_w_in, v_ffn1_w_out, v_norm2_w, v_w_mix_in, v_conv_w, v_a_log, v_dt_bias, v_gdn_norm_w, v_w_gdn_proj, v_pool_w, v_pool_scale, v_w_pool_proj, v_w_mix_out, v_norm3_w, v_ffn2_w_in, v_ffn2_w_out, v_final_norm_w):
    names = ("c_ctx", "w_ada", "b_ada", "norm1_w", "ffn1_w_in", "ffn1_w_out", "norm2_w", "w_mix_in", "conv_w", "a_log", "dt_bias",
             "gdn_norm_w", "w_gdn_proj", "pool_w", "pool_scale", "w_pool_proj", "w_mix_out", "norm3_w", "ffn2_w_in", "ffn2_w_out",
             "final_norm_w")
    w = dict(zip(names, (c_ctx, w_ada, b_ada, norm1_w, ffn1_w_in, ffn1_w_out, norm2_w, w_mix_in, conv_w, a_log, dt_bias, gdn_norm_w,
                         w_gdn_proj, pool_w, pool_scale, w_pool_proj, w_mix_out, norm3_w, ffn2_w_in, ffn2_w_out, final_norm_w)))
    mom = dict(zip(names, (m_c_ctx, m_w_ada, m_b_ada, m_norm1_w, m_ffn1_w_in, m_ffn1_w_out, m_norm2_w, m_w_mix_in, m_conv_w, m_a_log,
                           m_dt_bias, m_gdn_norm_w, m_w_gdn_proj, m_pool_w, m_pool_scale, m_w_pool_proj, m_w_mix_out, m_norm3_w,
                           m_ffn2_w_in, m_ffn2_w_out, m_final_norm_w)))
    var = dict(zip(names, (v_c_ctx, v_w_ada, v_b_ada, v_norm1_w, v_ffn1_w_in, v_ffn1_w_out, v_norm2_w, v_w_mix_in, v_conv_w, v_a_log,
                           v_dt_bias, v_gdn_norm_w, v_w_gdn_proj, v_pool_w, v_pool_scale, v_w_pool_proj, v_w_mix_out, v_norm3_w,
                           v_ffn2_w_in, v_ffn2_w_out, v_final_norm_w)))
    mx, my, mc = lax.axis_index("x"), lax.axis_index("y"), lax.axis_index("c")
    chip = 2 * mx + my
    dev = 2 * chip + mc
    ada_cols = w_ada.shape[2]

    c_rows = _all_gather8(jnp.pad(c, ((0, 7), (0, 0))), name="gather_c")[:, 0, :]
    c_all = jnp.concatenate([c_rows, c_ctx[None], jnp.zeros((7, D), F32)], axis=0)
    b_sh = lax.dynamic_slice(b_ada, (0, chip * ada_cols), (1, ada_cols))
    mod_sh = _ada_fwd(c_all, w_ada[0], b_sh, name="ada_fwd")
    mod_parts = _all_gather8(mod_sh, name="gather_mod")
    mod_all = jnp.concatenate([mod_parts[2 * s] for s in range(NCHIP)], axis=1)
    mod_lat = lax.dynamic_index_in_dim(mod_all, dev, axis=0, keepdims=False).reshape(9, D)
    modv = jnp.zeros((2, 16, D), F32).at[0, :9].set(mod_all[8].reshape(9, D)).at[1, :9].set(mod_lat)

    big = ("ffn1_w_in", "ffn1_w_out", "w_mix_in", "w_gdn_proj", "w_pool_proj", "w_mix_out", "ffn2_w_in", "ffn2_w_out")
    shard = {k: w[k][0].astype(BF) for k in big}
    w1_in, w1_out = _chip_exchange([shard["ffn1_w_in"], shard["ffn1_w_out"]], scatter=False, name="gather_ffn1")
    p = dict(
        norm1=norm1_w, norm2=norm2_w, norm3=norm3_w, fnorm=final_norm_w[None], w1_in=w1_in, w1_out=w1_out.reshape(FF, D),
        cst=jnp.zeros((8, 128), F32).at[0, 2 * NH:4 * NH].set(jnp.exp(a_log).reshape(-1)).at[1, 2 * NH:4 * NH].set(dt_bias.reshape(-1)),
        gnw=gdn_norm_w, pool_w=pool_w[0], pscale=pool_scale)
    late = ([shard["w_mix_in"], shard["w_gdn_proj"], shard["w_pool_proj"], shard["w_mix_out"], conv_w[0]],
            [shard["ffn2_w_in"], shard["ffn2_w_out"]])

    xc = jnp.concatenate([ctx[0], x[0]], axis=0)
    loss_dev, dxc, g, landed, small, dmod = _local_step(xc, loss_target[0], modv, p, late)
    loss = lax.psum(loss_dev, ("x", "y", "c"))
    grad_x = dxc[TM:][None]

    landed["w1_in"], landed["w1_out"] = _chip_exchange([g["w1_in"], g["w1_out"]], scatter=True, name="scatter_ffn1")
    order = ("w1_in", "w1_out", "w_mix", "w_gdn", "w_pool", "w_mo", "w2_in", "w2_out")
    mine = [_sum_slots(landed[s], name=f"sum_{k}") for k, s in zip(big, order)]
    theirs = _core_swap(mine, name="swap_grad_sums")

    small_vals = [dmod[1], dmod[0], small["norm1"], small["norm2"], small["norm3"], small["fnorm"], small["a_log"], small["dt_bias"],
                  small["gnw"], small["pool_w"], small["pscale"], small["conv"]]
    small_sizes = [v.size for v in small_vals]
    packed = _all_gather8(_pack(small_vals), name="gather_small")
    tot = _unpack(_sum_slots(packed, name="sum_small"), small_sizes)
    dmod_lat_all = packed[:, :9 * D // 128, :].reshape(8, 9 * D)
    dm = jnp.concatenate([dmod_lat_all, tot[1][None], jnp.zeros((7, 9 * D), F32)], axis=0)
    dm_sh = lax.dynamic_slice(dm, (0, chip * ada_cols), (16, ada_cols))
    g_w_ada, cctx_part = _ada_bwd(c_all, dm_sh, w_ada[0], name="ada_bwd")
    g_c_ctx = _cctx_grad(_all_gather8(cctx_part, name="gather_cctx"), c_ctx[None], name="cctx_grad")[0]
    conv_tot = tot[11].reshape(5, NQKV)
    g_small = dict(c_ctx=g_c_ctx, b_ada=tot[0] + tot[1], norm1_w=tot[2], norm2_w=tot[3], norm3_w=tot[4], final_norm_w=tot[5],
                   a_log=tot[6], dt_bias=tot[7], gdn_norm_w=tot[8], pool_w=tot[9], pool_scale=tot[10],
                   conv_w=lax.dynamic_slice(conv_tot, (0, chip * (NQKV // NCHIP)), (5, NQKV // NCHIP)))

    out = {}
    as2d = lambda a: a.reshape(-1, a.shape[-1])
    for k, ga, gb in zip(big, mine, theirs):
        shp = w[k].shape
        res = _adamw(as2d(w[k]), as2d(ga), as2d(gb), as2d(mom[k]), as2d(var[k]), name=f"adamw_{k}")
        out[k] = [r.reshape(shp) for r in res]
    out["w_ada"] = [r.reshape(w_ada.shape) for r in _adamw(w_ada[0], g_w_ada, None, m_w_ada[0], v_w_ada[0], name="adamw_w_ada")]
    sm_names = [n for n, _ in _SMALL]
    sm_sizes = [n for _, n in _SMALL]
    res = _adamw(_pack([w[k] for k in sm_names]), _pack([g_small[k] for k in sm_names]), None,
                 _pack([mom[k] for k in sm_names]), _pack([var[k] for k in sm_names]), name="adamw_small")
    res = [_unpack(r, sm_sizes) for r in res]
    for i, k in enumerate(sm_names):
        out[k] = [r[i].reshape(w[k].shape) for r in res]
    return (loss, grad_x, *[out[k][0] for k in names], *[out[k][1] for k in names], *[out[k][2] for k in names],
            *[out[k][3] for k in names])
```

```python
import functools

import jax
import jax.numpy as jnp
from jax import lax
from jax.experimental import pallas as pl
from jax.experimental.pallas import tpu as pltpu

F32 = jnp.float32
BF = jnp.bfloat16

D = 1024
FF = 2816
NH = 8
HD = 128
CH = 64
GW = 64
TM = 256
NQKV = 3 * NH * HD
NPOOL = 512
POOL_WINDOWS = (2, 4, 8, 16)
NMIX = 6688
NMIXP = 6784
EPS = 1e-6
NCHIP = 4
VMEM_LIMIT = 56 * 1024 * 1024

ADAM_LR, ADAM_B1, ADAM_B2, ADAM_EPS, ADAM_WD, ADAM_STEP = 0.001, 0.9, 0.999, 1e-08, 0.01, 10


def _cparams(*sem):
    return pltpu.CompilerParams(dimension_semantics=sem, vmem_limit_bytes=VMEM_LIMIT)


def _const_spec(shape):
    nd = len(shape)
    return pl.BlockSpec(shape, lambda *_: (0,) * nd, pipeline_mode=pl.Buffered(1))


def _dot(a, b, dims):
    return lax.dot_general(a.astype(BF), b.astype(BF), (dims, ((), ())), preferred_element_type=F32)


def _nn(a, b):
    return _dot(a, b, ((1,), (0,)))


def _nt(a, b):
    return _dot(a, b, ((1,), (1,)))


def _tn(a, b):
    return _dot(a, b, ((0,), (0,)))


def _silu(x):
    return x * jax.nn.sigmoid(x)


def _dsilu(x):
    s = jax.nn.sigmoid(x)
    return s * (1.0 + x * (1.0 - s))


def _norm_mod(x, nw, shift, scale):
    r = lax.rsqrt(jnp.mean(x * x, axis=-1, keepdims=True) + EPS)
    xh = x * r
    n = xh * nw
    return n * (1.0 + scale) + shift, n, xh, r


def _norm_mod_bwd(dh, n, xh, r, nw, scale):
    dn = dh * (1.0 + scale)
    dxh = dn * nw
    dx = r * (dxh - xh * jnp.mean(dxh * xh, axis=-1, keepdims=True))
    rs = lambda t: jnp.sum(t, axis=0, keepdims=True)
    return dx, rs(dh), rs(dh * n), rs(dn * xh)


def _ffn_fwd(x, modv, nw, w_in4, w_out, *, mrow, name, exchange=None):
    n_tok = x.shape[0]
    nt = n_tok // TM
    nset = modv.shape[0]
    ws = w_in4.shape[2]

    def body(x_ref, mod_ref, nw_ref, win_ref, wout_ref, x1_ref, h_ref, gu_ref, f_ref):
        xv = x_ref[...]
        shift, scale, gate = mod_ref[0, mrow:mrow + 1, :], mod_ref[0, mrow + 1:mrow + 2, :], mod_ref[0, mrow + 2:mrow + 3, :]
        h, _, _, _ = _norm_mod(xv, nw_ref[...], shift, scale)
        hb = h.astype(BF)
        h_ref[...] = hb
        gus = [_nn(hb, win_ref[s]) for s in range(NCHIP)]
        for s in range(NCHIP):
            gu_ref[:, s * ws:(s + 1) * ws] = gus[s].astype(BF)
        g = jnp.concatenate(gus[:2], axis=1)
        u = jnp.concatenate(gus[2:], axis=1)
        f = _nn(_silu(g) * u, wout_ref[...])
        f_ref[...] = f.astype(BF)
        x1_ref[...] = xv + 0.5 * gate * f

    tile = lambda w: pl.BlockSpec((TM, w), lambda i: (i, 0))
    return _pallas(
        body, (x, modv, nw, w_in4, w_out), name=name, grid=(nt,), sem=("parallel",), exchange=exchange,
        in_specs=[tile(D), pl.BlockSpec((1, 16, D), lambda i: (jnp.minimum(i, nset - 1), 0, 0)), _const_spec((1, D)),
                  _const_spec(w_in4.shape), _const_spec(w_out.shape)],
        out_specs=[tile(D), tile(D), tile(2 * FF), tile(D)],
        out_shape=[jax.ShapeDtypeStruct((n_tok, D), F32), jax.ShapeDtypeStruct((n_tok, D), BF),
                   jax.ShapeDtypeStruct((n_tok, 2 * FF), BF), jax.ShapeDtypeStruct((n_tok, D), BF)])


def _ffn_bwd(dxo, x, gu, fo, modv, nw, w_in4, w_out, *, mrow, dx_skip, name, exchange=None):
    n_tok = x.shape[0]
    nt = n_tok // TM
    nset = modv.shape[0]
    ws = w_in4.shape[2]

    def body(dxo_ref, x_ref, gu_ref, f_ref, mod_ref, nw_ref, win_ref, wout_ref, dx_ref, a_ref, df_ref, dgu_ref, acc_ref):
        i = pl.program_id(0)
        xv = x_ref[...]
        dxo_v = dxo_ref[...]
        shift, scale, gate = mod_ref[0, mrow:mrow + 1, :], mod_ref[0, mrow + 1:mrow + 2, :], mod_ref[0, mrow + 2:mrow + 3, :]
        _, n, xh, r = _norm_mod(xv, nw_ref[...], shift, scale)
        df = 0.5 * gate * dxo_v
        dfb = df.astype(BF)
        df_ref[...] = dfb
        dgate = jnp.sum(0.5 * dxo_v * f_ref[...].astype(F32), axis=0, keepdims=True)
        da = _nt(dfb, wout_ref[...])
        g = gu_ref[:, :FF].astype(F32)
        u = gu_ref[:, FF:].astype(F32)
        sg = _silu(g)
        a_ref[...] = (sg * u).astype(BF)
        dgu_ref[:, :FF] = (da * u * _dsilu(g)).astype(BF)
        dgu_ref[:, FF:] = (da * sg).astype(BF)
        dh = _nt(dgu_ref[:, 0:ws], win_ref[0])
        for s in range(1, NCHIP):
            dh = dh + _nt(dgu_ref[:, s * ws:(s + 1) * ws], win_ref[s])
        dx, dshift, dscale, dnw = _norm_mod_bwd(dh, n, xh, r, nw_ref[...], scale)
        dx_ref[...] = dxo_v + dx

        @pl.when((i == 0) | (i == nset - 1))
        def _():
            acc_ref[...] = jnp.zeros_like(acc_ref)

        acc_ref[0, 0:1, :] += dshift
        acc_ref[0, 1:2, :] += dscale
        acc_ref[0, 2:3, :] += dgate
        acc_ref[0, 3:4, :] += dnw

    tile = lambda w: pl.BlockSpec((TM, w), lambda i: (i, 0))
    return _pallas(
        body, (dxo, x, gu, fo, modv, nw, w_in4, w_out), name=name, grid=(nt,), sem=("arbitrary",), exchange=exchange,
        in_specs=[tile(D), tile(D), tile(2 * FF), tile(D),
                  pl.BlockSpec((1, 16, D), lambda i: (jnp.minimum(i, nset - 1), 0, 0)), _const_spec((1, D)),
                  _const_spec(w_in4.shape), _const_spec(w_out.shape)],
        out_specs=[pl.BlockSpec((TM, D), lambda i: (jnp.maximum(i - dx_skip, 0), 0)), tile(FF), tile(D), tile(2 * FF),
                   pl.BlockSpec((1, 8, D), lambda i: (jnp.minimum(i, nset - 1), 0, 0))],
        out_shape=[jax.ShapeDtypeStruct((n_tok - dx_skip * TM, D), F32), jax.ShapeDtypeStruct((n_tok, FF), BF),
                   jax.ShapeDtypeStruct((n_tok, D), BF), jax.ShapeDtypeStruct((n_tok, 2 * FF), BF),
                   jax.ShapeDtypeStruct((nset, 8, D), F32)])


def _k_tile(n, target=3072):
    return max(t for t in range(TM, min(n, target) + 1, TM) if n % t == 0)


def _matmul_tn(a, b, *, tmm, tn, tk, nsplit=1, name):
    n_tok, m = a.shape
    kk = b.shape[1]
    nk = n_tok // tk

    def body(a_ref, b_ref, o_ref, acc):
        k = pl.program_id(2)

        @pl.when(k == 0)
        def _():
            acc[...] = jnp.zeros_like(acc)

        acc[...] += _tn(a_ref[...], b_ref[...])

        @pl.when(k == nk - 1)
        def _():
            o_ref[...] = acc[...].astype(BF).reshape(o_ref.shape)

    if nsplit == 1:
        out_shape = jax.ShapeDtypeStruct((m, kk), BF)
        out_spec = pl.BlockSpec((tmm, tn), lambda i, j, k: (i, j))
    else:
        assert tn == kk // nsplit
        out_shape = jax.ShapeDtypeStruct((nsplit, m, tn), BF)
        out_spec = pl.BlockSpec((1, tmm, tn), lambda i, j, k: (j, i, 0))
    return pl.pallas_call(
        body, name=name, grid=(m // tmm, kk // tn, nk),
        in_specs=[pl.BlockSpec((tk, tmm), lambda i, j, k: (k, i)), pl.BlockSpec((tk, tn), lambda i, j, k: (k, j))],
        out_specs=out_spec, out_shape=out_shape,
        scratch_shapes=[pltpu.VMEM((tmm, tn), F32)],
        compiler_params=_cparams("parallel", "parallel", "arbitrary"),
    )(a, b)


_MIX_PARTS = (("qkv", 0, NQKV), ("gate", NQKV, 1024), ("pool", NQKV + 1024, NPOOL), ("br", NQKV + 1024 + NPOOL, 2048),
              ("ab", NMIXP - 128, 128))


def _mix_in_fwd(x1, modv, nw, w_mix, *, name):
    n_tok = x1.shape[0]

    def body(x_ref, mod_ref, nw_ref, w_ref, u_ref, *p_refs):
        u, _, _, _ = _norm_mod(x_ref[...], nw_ref[...], mod_ref[0, 3:4, :], mod_ref[0, 4:5, :])
        ub = u.astype(BF)
        u_ref[...] = ub
        for (_, c0, w), p_ref in zip(_MIX_PARTS, p_refs):
            p_ref[...] = _nn(ub, w_ref[:, c0:c0 + w])

    tile = lambda w: pl.BlockSpec((TM, w), lambda i: (i, 0))
    ctile = lambda w: pl.BlockSpec((TM, w), lambda i: (i + 1, 0))
    return pl.pallas_call(
        body, name=name, grid=(n_tok // TM,),
        in_specs=[tile(D), pl.BlockSpec((1, 16, D), lambda i: (jnp.minimum(i, 1), 0, 0)), _const_spec((1, D)),
                  _const_spec(w_mix.shape)],
        out_specs=[tile(D)] + [tile(w) for _, _, w in _MIX_PARTS],
        out_shape=[jax.ShapeDtypeStruct((n_tok, D), BF)] + [jax.ShapeDtypeStruct((n_tok, w), F32) for _, _, w in _MIX_PARTS],
        compiler_params=_cparams("parallel"),
    )(x1, modv, nw, w_mix)


def _mix_in_bwd(dxo, x1, dqkv, dgate, dpool, dbr, dab_f, dab_b, modv, nw, w_mix, *, name):
    n_tok = x1.shape[0]

    def body(dxo_ref, x_ref, dqkv_ref, dgate_ref, dpool_ref, dbr_ref, dabf_ref, dabb_ref, mod_ref, nw_ref, w_ref,
             dx_ref, dp_ref, acc_ref):
        i = pl.program_id(0)
        lat = i >= 1
        scale = mod_ref[0, 4:5, :]
        _, n, xh, r = _norm_mod(x_ref[...], nw_ref[...], mod_ref[0, 3:4, :], scale)
        dp_ref[:, 0:NQKV] = dqkv_ref[...].astype(BF)
        dp_ref[:, NQKV:NQKV + 1024] = jnp.where(lat, dgate_ref[...], 0.0).astype(BF)
        dp_ref[:, NQKV + 1024:NQKV + 1536] = jnp.where(lat, dpool_ref[...], 0.0).astype(BF)
        dp_ref[:, NQKV + 1536:NMIXP - 128] = jnp.where(lat, dbr_ref[...], 0.0).astype(BF)
        dp_ref[:, NMIXP - 128:] = (dabf_ref[...] + dabb_ref[...]).astype(BF)
        du = _nt(dp_ref[...], w_ref[...])
        dx, dshift, dscale, dnw = _norm_mod_bwd(du, n, xh, r, nw_ref[...], scale)
        dx_ref[...] = jnp.where(lat, dxo_ref[...], 0.0) + dx

        @pl.when(i <= 1)
        def _():
            acc_ref[...] = jnp.zeros_like(acc_ref)

        acc_ref[0, 0:1, :] += dshift
        acc_ref[0, 1:2, :] += dscale
        acc_ref[0, 3:4, :] += dnw

    tile = lambda w: pl.BlockSpec((TM, w), lambda i: (i, 0))
    ltile = lambda w: pl.BlockSpec((TM, w), lambda i: (jnp.maximum(i - 1, 0), 0))
    return pl.pallas_call(
        body, name=name, grid=(n_tok // TM,),
        in_specs=[ltile(D), tile(D), tile(NQKV), ltile(1024), ltile(NPOOL), ltile(2048), tile(128), tile(128),
                  pl.BlockSpec((1, 16, D), lambda i: (jnp.minimum(i, 1), 0, 0)), _const_spec((1, D)), _const_spec(w_mix.shape)],
        out_specs=[tile(D), tile(NMIXP), pl.BlockSpec((1, 8, D), lambda i: (jnp.minimum(i, 1), 0, 0))],
        out_shape=[jax.ShapeDtypeStruct((n_tok, D), F32), jax.ShapeDtypeStruct((n_tok, NMIXP), BF),
                   jax.ShapeDtypeStruct((2, 8, D), F32)],
        compiler_params=_cparams("arbitrary"),
    )(dxo, x1, dqkv, dgate, dpool, dbr, dab_f, dab_b, modv, nw, w_mix)


def _qkv_act(pre, j):
    s = _silu(pre)
    nrm = s * lax.rsqrt(jnp.sum(s * s, axis=-1, keepdims=True) + EPS)
    nrm = nrm * jnp.where(j == 0, HD ** -0.5, 1.0)
    return jnp.where(j < 2, nrm, s)


def _halo_specs(nt):
    r = TM // 8
    w = NH * HD
    main = pl.BlockSpec((TM, w), lambda j, i: (i, j))
    prev = pl.BlockSpec((8, w), lambda j, i: (jnp.maximum(i * r - 1, 0), j))
    nxt = pl.BlockSpec((8, w), lambda j, i: (jnp.minimum((i + 1) * r, nt * r - 1), j))
    return main, prev, nxt


def _prep_fwd(p_qkv, conv_w8, *, name):
    n_tok = p_qkv.shape[0]
    nt = n_tok // TM

    def body(x_ref, xp_ref, xn_ref, w_ref, o_ref, pre_ref, win):
        j, i = pl.program_id(0), pl.program_id(1)
        has_prev = (i != 0) & (i != 1)
        has_next = (i != 0) & (i != nt - 1)
        win[0:8, :] = jnp.where(has_prev, xp_ref[...], 0.0)
        win[8:8 + TM, :] = x_ref[...]
        win[8 + TM:, :] = jnp.where(has_next, xn_ref[...], 0.0)
        for h in range(NH):
            hs = slice(h * HD, (h + 1) * HD)
            pre = win[6:6 + TM, hs] * w_ref[0:1, hs]
            for k in range(1, 5):
                pre = pre + win[6 + k:6 + k + TM, hs] * w_ref[k:k + 1, hs]
            pre_ref[:, hs] = pre
            o_ref[:, hs] = _qkv_act(pre, j)

    main, prev, nxt = _halo_specs(nt)
    wq = NH * HD
    return pl.pallas_call(
        body, name=name, grid=(3, nt),
        in_specs=[main, prev, nxt, pl.BlockSpec((8, wq), lambda j, i: (0, j))],
        out_specs=[main, main], out_shape=[jax.ShapeDtypeStruct((n_tok, NQKV), F32)] * 2,
        scratch_shapes=[pltpu.VMEM((TM + 16, wq), F32)],
        compiler_params=_cparams("parallel", "arbitrary"),
    )(p_qkv, p_qkv, p_qkv, conv_w8)


def _prep_bwd(p_qkv, pre, dqkv_f, dqkv_b, conv_w8, *, name):
    n_tok = p_qkv.shape[0]
    nt = n_tok // TM
    wq = NH * HD

    def body(x_ref, p_ref, pp_ref, pn_ref, g_ref, gp_ref, gn_ref, g2_ref, g2p_ref, g2n_ref, w_ref, dx_ref, dw_ref, pwin, gwin, dwin):
        j, i = pl.program_id(0), pl.program_id(1)
        has_prev = (i != 0) & (i != 1)
        has_next = (i != 0) & (i != nt - 1)
        pwin[0:8, :] = jnp.where(has_prev, pp_ref[...], 0.0)
        pwin[8:8 + TM, :] = p_ref[...]
        pwin[8 + TM:, :] = jnp.where(has_next, pn_ref[...], 0.0)
        gwin[0:8, :] = jnp.where(has_prev, gp_ref[...] + g2p_ref[...], 0.0)
        gwin[8:8 + TM, :] = g_ref[...] + g2_ref[...]
        gwin[8 + TM:, :] = jnp.where(has_next, gn_ref[...] + g2n_ref[...], 0.0)

        @pl.when(i == 0)
        def _():
            dw_ref[...] = jnp.zeros_like(dw_ref)

        for h in range(NH):
            hs = slice(h * HD, (h + 1) * HD)
            _, vjp = jax.vjp(lambda t: _qkv_act(t, j), pwin[:, hs])
            dwin[:, hs] = vjp(gwin[:, hs])[0]
            xv = x_ref[:, hs]
            dx = None
            for k in range(5):
                sh = dwin[10 - k:10 - k + TM, hs]
                dx = sh * w_ref[k:k + 1, hs] if dx is None else dx + sh * w_ref[k:k + 1, hs]
                dw_ref[k:k + 1, hs] += jnp.sum(sh * xv, axis=0, keepdims=True)
            dx_ref[:, hs] = dx

    main, prev, nxt = _halo_specs(nt)
    wspec = pl.BlockSpec((8, wq), lambda j, i: (0, j))
    return pl.pallas_call(
        body, name=name, grid=(3, nt),
        in_specs=[main, main, prev, nxt, main, prev, nxt, main, prev, nxt, wspec],
        out_specs=[main, wspec],
        out_shape=[jax.ShapeDtypeStruct((n_tok, NQKV), F32), jax.ShapeDtypeStruct((8, NQKV), F32)],
        scratch_shapes=[pltpu.VMEM((TM + 16, wq), F32)] * 3,
        compiler_params=_cparams("parallel", "arbitrary"),
    )(p_qkv, pre, pre, pre, dqkv_f, dqkv_f, dqkv_f, dqkv_b, dqkv_b, dqkv_b, conv_w8)


@jax.custom_vjp
def _mm_nn(a, b):
    return _nn(a, b)


@jax.custom_vjp
def _mm_nt(a, b):
    return _nt(a, b)


@jax.custom_vjp
def _mm_tn(a, b):
    return _tn(a, b)


_mm_nn.defvjp(lambda a, b: (_nn(a, b), (a, b)), lambda r, g: (_mm_nt(g, r[1]), _mm_tn(r[0], g)))
_mm_nt.defvjp(lambda a, b: (_nt(a, b), (a, b)), lambda r, g: (_mm_nn(g, r[1]), _mm_tn(g, r[0])))
_mm_tn.defvjp(lambda a, b: (_tn(a, b), (a, b)), lambda r, g: (_mm_nt(r[1], g), _mm_nn(r[0], g)))


def _each(f, *lists):
    return tuple(f(*a) for a in zip(*lists))


def _unit_tri_inv(ls, revs):
    ii = lax.broadcasted_iota(jnp.int32, (CH, CH), 0)
    jj = lax.broadcasted_iota(jnp.int32, (CH, CH), 1)
    eye = (ii == jj).astype(F32)
    xs = None
    s = 1
    while s < CH:
        same = (ii & -(2 * s)) == (jj & -(2 * s))
        off = {False: same & ((ii & s) != 0) & ((jj & s) == 0), True: same & ((jj & s) != 0) & ((ii & s) == 0)}
        cs = _each(lambda l, r: jnp.where(off[r], l, 0.0), ls, revs)
        if xs is None:
            xs = _each(lambda c: eye - c, cs)
        else:
            xc = _each(_nn, xs, cs)
            xcx = _each(_nn, xc, xs)
            xs = _each(lambda x, t: x - t, xs, xcx)
        s *= 2
    return xs


@functools.lru_cache(maxsize=None)
def _tri_solve(revs):
    @jax.custom_vjp
    def solve(ls, rhss):
        return _each(_mm_nn, _unit_tri_inv(ls, revs), rhss)

    def fwd(ls, rhss):
        ainv = _unit_tri_inv(ls, revs)
        xs = _each(_mm_nn, ainv, rhss)
        return xs, (ainv, xs)

    def bwd(res, gs):
        ainv, xs = res
        drhs = _each(_mm_tn, ainv, gs)
        return _each(lambda d, x: -_mm_nt(d, x), drhs, xs), drhs

    solve.defvjp(fwd, bwd)
    return solve


def _chunk_prep(q, k, v, beta, g, *, revs):
    ii = lax.broadcasted_iota(jnp.int32, (CH, CH), 0)
    jj = lax.broadcasted_iota(jnp.int32, (CH, CH), 1)
    eye = ii == jj
    incl_of = {False: ii >= jj, True: ii <= jj}
    strict_of = {False: ii > jj, True: ii < jj}
    g_row = _each(lambda t: jnp.sum(jnp.where(eye, t, 0.0), axis=0, keepdims=True), g)
    cum = _each(lambda t, r: jnp.sum(jnp.where(incl_of[r], t, 0.0), axis=1, keepdims=True), g_row, revs)
    cum_row = _each(lambda t: jnp.sum(jnp.where(eye, t, 0.0), axis=0, keepdims=True), cum)
    total = _each(lambda t: jnp.sum(t, axis=0, keepdims=True), g)
    decay = _each(lambda c, cr, r: jnp.where(incl_of[r], jnp.exp(jnp.where(incl_of[r], c - cr, 0.0)), 0.0), cum, cum_row, revs)
    kb = _each(jnp.multiply, k, beta)
    vb = _each(jnp.multiply, v, beta)
    kk = _each(_mm_nt, kb, k)
    lmat = _each(lambda t, dc, r: jnp.where(strict_of[r], t * dc, 0.0), kk, decay, revs)
    ecum = _each(jnp.exp, cum)
    rhs = _each(lambda a, b, e: jnp.concatenate([a, b * e], axis=1), vb, kb, ecum)
    sol = _tri_solve(revs)(lmat, rhs)
    qk = _each(_mm_nt, q, k)
    aqk = _each(jnp.multiply, qk, decay)
    qd = _each(jnp.multiply, q, ecum)
    kd = _each(lambda a, t, c: a * jnp.exp(t - c), k, total, cum)
    return sol, aqk, qd, kd, _each(jnp.exp, total)


def _chunk_rec(sol, aqk, qd, kd, bl, s):
    ws = _each(lambda so, st: _mm_nn(so[:, HD:], st), sol, s)
    v_new = _each(lambda so, t: so[:, :HD] - t, sol, ws)
    qs = _each(_mm_nn, qd, s)
    av = _each(_mm_nn, aqk, v_new)
    o = _each(jnp.add, qs, av)
    kv = _each(_mm_tn, kd, v_new)
    s_new = _each(lambda st, b, u: st * b + u, s, bl, kv)
    return o, s_new


def _lane_col(x, c):
    lane = lax.broadcasted_iota(jnp.int32, x.shape, 1)
    return jnp.sum(jnp.where(lane == c, x, 0.0), axis=1, keepdims=True)


def _beta_g(ab, cst, d, h):
    braw = _lane_col(ab, NH * d + h)
    araw = _lane_col(ab, 2 * NH + NH * d + h)
    ea = _lane_col(cst[0:1, :], 2 * NH + NH * d + h)
    dt = _lane_col(cst[1:2, :], 2 * NH + NH * d + h)
    z = araw + dt
    softplus = jnp.maximum(z, 0.0) + jnp.log(1.0 + jnp.exp(-jnp.abs(z)))
    return jax.nn.sigmoid(braw), -ea * softplus, z, ea


STEPS = 2
TS = STEPS * CH
_CHAINS = tuple((t, d, h) for t in range(STEPS) for d in (0, 1) for h in range(NH))
_REVS = tuple(bool(d) for _, d, _ in _CHAINS)
_PER_STEP = 2 * NH


def _chain_inputs(refs, r0s, ab_refs, cst):
    hs = lambda h: slice(h * HD, (h + 1) * HD)
    abvs = [[ab_refs[d][pl.ds(r0s[t][d], CH), :] for d in (0, 1)] for t in range(STEPS)]
    q = _each(lambda c: refs[c[1]][0][pl.ds(r0s[c[0]][c[1]], CH), hs(c[2])], _CHAINS)
    k = _each(lambda c: refs[c[1]][1][pl.ds(r0s[c[0]][c[1]], CH), hs(c[2])], _CHAINS)
    v = _each(lambda c: refs[c[1]][2][pl.ds(r0s[c[0]][c[1]], CH), hs(c[2])], _CHAINS)
    bg = _each(lambda c: _beta_g(abvs[c[0]][c[1]], cst, c[1], c[2]), _CHAINS)
    return q, k, v, bg


def _of_step(parts, t):
    return tuple(p[t * _PER_STEP:(t + 1) * _PER_STEP] for p in parts)


def _scan_fwd(qkv, ab, cst, s0, *, row_blk0, nb, name, exchange=None):
    cb = TS // CH
    w = NH * HD

    def body(qf, kf, vf, abf, qb, kb, vb, abb, cst_ref, s0_ref, of_ref, ob_ref, sallf_ref, sallb_ref, sfin_ref, s_scr):
        i = pl.program_id(0)

        @pl.when(i == 0)
        def _():
            s_scr[...] = s0_ref[...]

        o_refs, sall_refs = (of_ref, ob_ref), (sallf_ref, sallb_ref)

        def chunks(ci, carry):
            cs = [(ci * STEPS + t, cb - 1 - ci * STEPS - t) for t in range(STEPS)]
            r0s = [tuple(pl.multiple_of(c * CH, CH) for c in ct) for ct in cs]
            q, k, v, bg = _chain_inputs(((qf, kf, vf), (qb, kb, vb)), r0s, (abf, abb), cst_ref[...])
            parts = _chunk_prep(q, k, v, _each(lambda t: t[0], bg), _each(lambda t: t[1], bg), revs=_REVS)
            s = _each(lambda c: s_scr[c[1], c[2]], _CHAINS[:_PER_STEP])
            for t in range(STEPS):
                for (_, d, h), sv in zip(_CHAINS, s):
                    sall_refs[d][cs[t][d], h] = sv
                o, s = _chunk_rec(*_of_step(parts, t), s)
                for (_, d, h), ov in zip(_CHAINS, o):
                    o_refs[d][pl.ds(r0s[t][d], CH), h * HD:(h + 1) * HD] = ov
            for (_, d, h), sv in zip(_CHAINS, s):
                s_scr[d, h] = sv
            return carry

        lax.fori_loop(0, cb // STEPS, chunks, 0)

        @pl.when(i == nb - 1)
        def _():
            sfin_ref[...] = s_scr[...]

    pos = (lambda i: i, lambda i: nb - 1 - i)
    col = lambda d, c: pl.BlockSpec((TS, w), lambda i: (row_blk0 + pos[d](i), c))
    abs_ = lambda d: pl.BlockSpec((TS, 128), lambda i: (row_blk0 + pos[d](i), 0))
    full4 = pl.BlockSpec((2, NH, HD, HD), lambda i: (0, 0, 0, 0))
    o_spec = lambda d: pl.BlockSpec((TS, w), lambda i: (pos[d](i), 0))
    sall_spec = lambda d: pl.BlockSpec((cb, NH, HD, HD), lambda i: (pos[d](i), 0, 0, 0))
    return _pallas(
        body, (qkv, qkv, qkv, ab, qkv, qkv, qkv, ab, cst, s0), name=name, grid=(nb,), sem=("arbitrary",), exchange=exchange,
        in_specs=[col(0, 0), col(0, 1), col(0, 2), abs_(0), col(1, 0), col(1, 1), col(1, 2), abs_(1),
                  pl.BlockSpec((8, 128), lambda i: (0, 0)), full4],
        out_specs=[o_spec(0), o_spec(1), sall_spec(0), sall_spec(1), full4],
        out_shape=[jax.ShapeDtypeStruct((nb * TS, w), F32)] * 2 + [jax.ShapeDtypeStruct((nb * cb, NH, HD, HD), F32)] * 2
        + [jax.ShapeDtypeStruct((2, NH, HD, HD), F32)],
        scratch_shapes=[pltpu.VMEM((2, NH, HD, HD), F32)])


def _scan_bwd(qkv, ab, cst, sall_f, sall_b, do, dsfin, dqkv_f, dqkv_b, dab_f, dab_b, dcst, *, row_blk0, nb, has_do, name,
              exchange=None):
    cb = TS // CH
    w = NH * HD

    def body(qf, kf, vf, abf, qb, kb, vb, abb, cst_ref, sallf_ref, sallb_ref, dof_ref, dob_ref, dsfin_ref, _f, _b, _af, _ab, dcst_in,
             dqkvf_ref, dqkvb_ref, dabf_ref, dabb_ref, dcst_ref, ds0_ref, ds_scr):
        i = pl.program_id(0)

        @pl.when(i == 0)
        def _():
            ds_scr[...] = dsfin_ref[...]
            dcst_ref[...] = dcst_in[...]

        lane = lax.broadcasted_iota(jnp.int32, (CH, 128), 1)
        lane1 = lax.broadcasted_iota(jnp.int32, (1, 128), 1)
        sall_refs, do_refs = (sallf_ref, sallb_ref), (dof_ref, dob_ref)
        dqkv_refs, dab_refs = (dqkvf_ref, dqkvb_ref), (dabf_ref, dabb_ref)

        def chunks(ci, carry):
            cs = [(cb - 1 - ci * STEPS - t, ci * STEPS + t) for t in range(STEPS)]
            r0s = [tuple(pl.multiple_of(c * CH, CH) for c in ct) for ct in cs]
            q, k, v, bg = _chain_inputs(((qf, kf, vf), (qb, kb, vb)), r0s, (abf, abb), cst_ref[...])
            beta, g = _each(lambda t: t[0], bg), _each(lambda t: t[1], bg)
            parts, prep_vjp = jax.vjp(functools.partial(_chunk_prep, revs=_REVS), q, k, v, beta, g)
            ds = _each(lambda c: ds_scr[c[1], c[2]], _CHAINS[:_PER_STEP])
            dparts = []
            for t in range(STEPS):
                s = _each(lambda c: sall_refs[c[1]][cs[t][c[1]], c[2]], _CHAINS[:_PER_STEP])
                _, rec_vjp = jax.vjp(_chunk_rec, *_of_step(parts, t), s)
                do_t = _each(lambda c: do_refs[c[1]][pl.ds(r0s[t][c[1]], CH), c[2] * HD:(c[2] + 1) * HD] if has_do
                             else jnp.zeros((CH, HD), F32), _CHAINS[:_PER_STEP])
                *dpt, ds = rec_vjp((do_t, ds))
                dparts.append(dpt)
            for (_, d, h), dsv in zip(_CHAINS, ds):
                ds_scr[d, h] = dsv
            dq, dk, dv, dbeta, dg = prep_vjp(tuple(sum((dparts[t][j] for t in range(STEPS)), ()) for j in range(len(dparts[0]))))
            dab = [[jnp.zeros((CH, 128), F32), jnp.zeros((CH, 128), F32)] for _ in range(STEPS)]
            dal = jnp.zeros((1, 128), F32)
            for n, (t, d, h) in enumerate(_CHAINS):
                for part, val in enumerate((dq[n], dk[n], dv[n])):
                    dqkv_refs[d][pl.ds(r0s[t][d], CH), part * w + h * HD:part * w + (h + 1) * HD] = val
                z, ea = bg[n][2], bg[n][3]
                dbraw = dbeta[n] * beta[n] * (1.0 - beta[n])
                daraw = dg[n] * (-ea) * jax.nn.sigmoid(z)
                dab[t][d] = dab[t][d] + jnp.where(lane == NH * d + h, dbraw, 0.0) + jnp.where(lane == 2 * NH + NH * d + h, daraw, 0.0)
                dal = dal + jnp.where(lane1 == 2 * NH + NH * d + h, jnp.sum(dg[n] * g[n], axis=0, keepdims=True), 0.0)
            dsum = jnp.zeros((CH, 128), F32)
            for t in range(STEPS):
                for d in (0, 1):
                    dab_refs[d][pl.ds(r0s[t][d], CH), :] = dab[t][d]
                    dsum = dsum + dab[t][d]
            dcst_ref[0:1, :] += dal
            dcst_ref[1:2, :] += jnp.sum(jnp.where(lane >= 2 * NH, dsum, 0.0), axis=0, keepdims=True)
            return carry

        lax.fori_loop(0, cb // STEPS, chunks, 0)

        @pl.when(i == nb - 1)
        def _():
            ds0_ref[...] = ds_scr[...]

    pos = (lambda i: nb - 1 - i, lambda i: i)
    col = lambda d, c: pl.BlockSpec((TS, w), lambda i: (row_blk0 + pos[d](i), c))
    abs_ = lambda d: pl.BlockSpec((TS, 128), lambda i: (row_blk0 + pos[d](i), 0))
    full4 = pl.BlockSpec((2, NH, HD, HD), lambda i: (0, 0, 0, 0))
    small = pl.BlockSpec((8, 128), lambda i: (0, 0))
    hbm = pl.BlockSpec(memory_space=pl.ANY)
    sall_spec = lambda d: pl.BlockSpec((cb, NH, HD, HD), lambda i: (pos[d](i), 0, 0, 0))
    do_spec = (lambda d: pl.BlockSpec((TS, w), lambda i: (pos[d](i), 0))) if has_do else (lambda d: small)
    acc_specs = [pl.BlockSpec((TS, 3 * w), lambda i: (row_blk0 + pos[0](i), 0)),
                 pl.BlockSpec((TS, 3 * w), lambda i: (row_blk0 + pos[1](i), 0)), abs_(0), abs_(1), small]
    return _pallas(
        body, (qkv, qkv, qkv, ab, qkv, qkv, qkv, ab, cst, sall_f, sall_b, do, do, dsfin, dqkv_f, dqkv_b, dab_f, dab_b, dcst),
        name=name, grid=(nb,), sem=("arbitrary",), exchange=exchange,
        in_specs=[col(0, 0), col(0, 1), col(0, 2), abs_(0), col(1, 0), col(1, 1), col(1, 2), abs_(1), small,
                  sall_spec(0), sall_spec(1), do_spec(0), do_spec(1), full4, hbm, hbm, hbm, hbm, small],
        out_specs=acc_specs + [full4],
        out_shape=[jax.ShapeDtypeStruct(dqkv_f.shape, F32), jax.ShapeDtypeStruct(dqkv_b.shape, F32),
                   jax.ShapeDtypeStruct(dab_f.shape, F32), jax.ShapeDtypeStruct(dab_b.shape, F32),
                   jax.ShapeDtypeStruct((8, 128), F32), jax.ShapeDtypeStruct((2, NH, HD, HD), F32)],
        aliases={14: 0, 15: 1, 16: 2, 17: 3, 18: 4},
        scratch_shapes=[pltpu.VMEM((2, NH, HD, HD), F32)])


def _pool(xin, *, row0, transpose, name):
    n_tok = xin.shape[0] - row0
    rows = n_tok // GW
    pad = 8 * GW
    tt = 512
    gsh = GW.bit_length() - 1

    def body(x_ref, o_ref, ybuf):
        ii = lax.broadcasted_iota(jnp.int32, (128, 128), 0)
        jj = lax.broadcasted_iota(jnp.int32, (128, 128), 1)
        same_row = (ii >> gsh) == (jj >> gsh)
        ci, cj = ii & (GW - 1), jj & (GW - 1)
        tok = lax.broadcasted_iota(jnp.int32, (tt, 1), 0)
        zpad = jnp.zeros((pad, 128), F32)
        for gi, wdw in enumerate(POOL_WINDOWS):
            lo, hi = wdw // 2, wdw - wdw // 2
            if transpose:
                band = same_row & (ci - cj >= -lo) & (ci - cj < hi)
                offs = range(-hi + 1, lo + 1)
            else:
                band = same_row & (cj - ci >= -lo) & (cj - ci < hi)
                offs = range(-lo, hi)
            bandm = band.astype(BF)
            cs = slice(gi * 128, (gi + 1) * 128)
            ybuf[0:pad, :] = zpad
            ybuf[pad + n_tok:, :] = zpad

            def inv_area(t0):
                t = t0 + tok
                r, c = t >> gsh, t & (GW - 1)
                nr = jnp.minimum(r + hi, rows) - jnp.maximum(r - lo, 0)
                nc = jnp.minimum(c + hi, GW) - jnp.maximum(c - lo, 0)
                return 1.0 / (nr * nc).astype(F32)

            def col_pass(b, carry):
                t0 = pl.multiple_of(b * tt, tt)
                xv = x_ref[pl.ds(row0 + t0, tt), cs]
                if transpose:
                    xv = xv * inv_area(t0)
                hi_part = xv.astype(BF)
                lo_part = (xv - hi_part.astype(F32)).astype(BF)
                for s in range(tt // 128):
                    sl = slice(s * 128, (s + 1) * 128)
                    y = (jnp.dot(bandm, hi_part[sl], preferred_element_type=F32)
                         + jnp.dot(bandm, lo_part[sl], preferred_element_type=F32))
                    ybuf[pl.ds(pad + t0 + s * 128, 128), :] = y
                return carry

            lax.fori_loop(0, n_tok // tt, col_pass, 0)

            def row_pass(b, carry):
                t0 = pl.multiple_of(b * tt, tt)
                acc = ybuf[pl.ds(pad + t0 + offs[0] * GW, tt), :]
                for dr in offs[1:]:
                    acc = acc + ybuf[pl.ds(pad + t0 + dr * GW, tt), :]
                xv = x_ref[pl.ds(row0 + t0, tt), cs]
                if not transpose:
                    acc = acc * inv_area(t0)
                o_ref[pl.ds(t0, tt), cs] = acc - xv
                return carry

            lax.fori_loop(0, n_tok // tt, row_pass, 0)

    return pl.pallas_call(
        body, name=name, out_shape=jax.ShapeDtypeStruct((n_tok, NPOOL), F32),
        in_specs=[pl.BlockSpec(memory_space=pltpu.VMEM)], out_specs=pl.BlockSpec(memory_space=pltpu.VMEM),
        scratch_shapes=[pltpu.VMEM((n_tok + 2 * pad, 128), F32)],
        compiler_params=pltpu.CompilerParams(vmem_limit_bytes=VMEM_LIMIT),
    )(xin)


def _merge_parts(of, ob, pgate, pd, br, gnw, pw_ref, pscale, wg_ref, wp_ref):
    o = of + ob
    ons, ohs, rs = [], [], []
    for h in range(NH):
        oh = o[:, h * HD:(h + 1) * HD]
        r = lax.rsqrt(jnp.mean(oh * oh, axis=-1, keepdims=True) + EPS)
        ohs.append(oh * r)
        rs.append(r)
        ons.append(oh * r * gnw)
    on = jnp.concatenate(ons, axis=1)
    og = on * _silu(pgate)
    y_gdn = _nn(og, wg_ref[...])
    ypre = jnp.concatenate([_nn(pd[:, g * 128:(g + 1) * 128], pw_ref[g]) for g in range(4)], axis=1)
    yp = ypre * pscale
    y_pool = _nn(yp, wp_ref[...])
    g_pool = jax.nn.sigmoid(br[:, :D])
    g_gdn = jax.nn.sigmoid(br[:, D:])
    return dict(on=on, ohs=ohs, rs=rs, og=og, y_gdn=y_gdn, ypre=ypre, yp=yp, y_pool=y_pool, g_pool=g_pool, g_gdn=g_gdn)


def _merge_fwd(x1, of, ob, pgate, pd, br, modv, gnw, pool_w, pscale, w_gdn, w_pool, w_mo, *, name):
    n_tok = of.shape[0]

    def body(x_ref, of_ref, ob_ref, pg_ref, pd_ref, br_ref, mod_ref, gnw_ref, pw_ref, ps_ref, wg_ref, wp_ref, wmo_ref,
             x2_ref, og_ref, yp_ref, m_ref, mix_ref):
        t = _merge_parts(of_ref[...], ob_ref[...], pg_ref[...], pd_ref[...], br_ref[...], gnw_ref[...], pw_ref, ps_ref[...],
                         wg_ref, wp_ref)
        m = t["g_pool"] * t["y_pool"] + t["g_gdn"] * t["y_gdn"]
        mix = _nn(m, wmo_ref[...])
        og_ref[...] = t["og"].astype(BF)
        yp_ref[...] = t["yp"].astype(BF)
        m_ref[...] = m.astype(BF)
        mix_ref[...] = mix.astype(BF)
        x2_ref[...] = x_ref[...] + mod_ref[0, 5:6, :] * mix

    tile = lambda w: pl.BlockSpec((TM, w), lambda i: (i, 0))
    ctile = lambda w: pl.BlockSpec((TM, w), lambda i: (i + 1, 0))
    return pl.pallas_call(
        body, name=name, grid=(n_tok // TM,),
        in_specs=[ctile(D), tile(D), tile(D), ctile(D), tile(NPOOL), ctile(2 * D),
                  pl.BlockSpec((1, 16, D), lambda i: (1, 0, 0)), _const_spec((1, HD)), _const_spec((4, 128, 128)),
                  _const_spec((1, NPOOL)), _const_spec((D, D)), _const_spec((NPOOL, D)), _const_spec((D, D))],
        out_specs=[tile(D), tile(D), tile(NPOOL), tile(D), tile(D)],
        out_shape=[jax.ShapeDtypeStruct((n_tok, D), F32), jax.ShapeDtypeStruct((n_tok, D), BF),
                   jax.ShapeDtypeStruct((n_tok, NPOOL), BF), jax.ShapeDtypeStruct((n_tok, D), BF),
                   jax.ShapeDtypeStruct((n_tok, D), BF)],
        compiler_params=_cparams("parallel"),
    )(x1, of, ob, pgate, pd, br, modv, gnw, pool_w, pscale, w_gdn, w_pool, w_mo)


def _merge_bwd(dx2, mix, of, ob, pgate, pd, br, modv, gnw, pool_w, pscale, w_gdn, w_pool, w_mo, *, name):
    n_tok = of.shape[0]

    def body(dx2_ref, mix_ref, of_ref, ob_ref, pg_ref, pd_ref, br_ref, mod_ref, gnw_ref, pw_ref, ps_ref, wg_ref, wp_ref, wmo_ref,
             do_ref, dgate_ref, dpd_ref, dbr_ref, dmix_ref, dyg_ref, dyp_ref, acc_ref, dpw_ref):
        i = pl.program_id(0)
        pgate, pdv, gnw = pg_ref[...], pd_ref[...], gnw_ref[...]
        t = _merge_parts(of_ref[...], ob_ref[...], pgate, pdv, br_ref[...], gnw, pw_ref, ps_ref[...], wg_ref, wp_ref)
        dx2v = dx2_ref[...]
        dmix = mod_ref[0, 5:6, :] * dx2v
        dmixb = dmix.astype(BF)
        dmix_ref[...] = dmixb
        dm = _nt(dmixb, wmo_ref[...])
        gp, gg = t["g_pool"], t["g_gdn"]
        dbr_ref[:, :D] = dm * t["y_pool"] * gp * (1.0 - gp)
        dbr_ref[:, D:] = dm * t["y_gdn"] * gg * (1.0 - gg)
        dyp = (dm * gp).astype(BF)
        dyg = (dm * gg).astype(BF)
        dyp_ref[...] = dyp
        dyg_ref[...] = dyg
        dyp_in = _nt(dyp, wp_ref[...])
        dypre = dyp_in * ps_ref[...]
        for g in range(4):
            gs = slice(g * 128, (g + 1) * 128)
            dpd_ref[:, gs] = _nt(dypre[:, gs], pw_ref[g])
        dog = _nt(dyg, wg_ref[...])
        dgate_ref[...] = dog * t["on"] * _dsilu(pgate)
        don = dog * _silu(pgate)
        dgnw = jnp.zeros((1, HD), F32)
        for h in range(NH):
            hs = slice(h * HD, (h + 1) * HD)
            donh, oh, r = don[:, hs], t["ohs"][h], t["rs"][h]
            dgnw = dgnw + jnp.sum(donh * oh, axis=0, keepdims=True)
            doh = donh * gnw
            do_ref[:, hs] = r * (doh - oh * jnp.mean(doh * oh, axis=-1, keepdims=True))

        @pl.when(i == 0)
        def _():
            acc_ref[...] = jnp.zeros_like(acc_ref)
            dpw_ref[...] = jnp.zeros_like(dpw_ref)

        acc_ref[0:1, :] += jnp.sum(dx2v * mix_ref[...].astype(F32), axis=0, keepdims=True)
        acc_ref[1:2, 0:HD] += dgnw
        acc_ref[2:3, 0:NPOOL] += jnp.sum(dyp_in * t["ypre"], axis=0, keepdims=True)
        for g in range(4):
            gs = slice(g * 128, (g + 1) * 128)
            dpw_ref[g] += _tn(pdv[:, gs], dypre[:, gs])

    tile = lambda w: pl.BlockSpec((TM, w), lambda i: (i, 0))
    ctile = lambda w: pl.BlockSpec((TM, w), lambda i: (i + 1, 0))
    return pl.pallas_call(
        body, name=name, grid=(n_tok // TM,),
        in_specs=[tile(D), tile(D), tile(D), tile(D), ctile(D), tile(NPOOL), ctile(2 * D),
                  pl.BlockSpec((1, 16, D), lambda i: (1, 0, 0)), _const_spec((1, HD)), _const_spec((4, 128, 128)),
                  _const_spec((1, NPOOL)), _const_spec((D, D)), _const_spec((NPOOL, D)), _const_spec((D, D))],
        out_specs=[tile(D), tile(D), tile(NPOOL), tile(2 * D), tile(D), tile(D), tile(D),
                   pl.BlockSpec((8, D), lambda i: (0, 0)), pl.BlockSpec((4, 128, 128), lambda i: (0, 0, 0))],
        out_shape=[jax.ShapeDtypeStruct((n_tok, D), F32), jax.ShapeDtypeStruct((n_tok, D), F32),
                   jax.ShapeDtypeStruct((n_tok, NPOOL), F32), jax.ShapeDtypeStruct((n_tok, 2 * D), F32),
                   jax.ShapeDtypeStruct((n_tok, D), BF), jax.ShapeDtypeStruct((n_tok, D), BF), jax.ShapeDtypeStruct((n_tok, D), BF),
                   jax.ShapeDtypeStruct((8, D), F32), jax.ShapeDtypeStruct((4, 128, 128), F32)],
        compiler_params=_cparams("arbitrary"),
    )(dx2, mix, of, ob, pgate, pd, br, modv, gnw, pool_w, pscale, w_gdn, w_pool, w_mo)


def _final(x3, target, fnw, *, name):
    n_tok = x3.shape[0]

    def body(x_ref, t_ref, w_ref, dx_ref, acc_ref):
        xv, w = x_ref[...], w_ref[...]
        r = lax.rsqrt(jnp.mean(xv * xv, axis=-1, keepdims=True) + EPS)
        xh = xv * r
        err = xh * w - t_ref[...]
        dy = err * (1.0 / D)
        dxh = dy * w
        dx_ref[...] = r * (dxh - xh * jnp.mean(dxh * xh, axis=-1, keepdims=True))

        @pl.when(pl.program_id(0) == 0)
        def _():
            acc_ref[...] = jnp.zeros_like(acc_ref)

        acc_ref[0:1, :] += jnp.sum(dy * xh, axis=0, keepdims=True)
        acc_ref[1:2, :] += jnp.sum(err * err, axis=0, keepdims=True) * (0.5 / D)

    tile = pl.BlockSpec((TM, D), lambda i: (i, 0))
    return pl.pallas_call(
        body, name=name, grid=(n_tok // TM,),
        in_specs=[tile, tile, _const_spec((1, D))],
        out_specs=[tile, pl.BlockSpec((8, D), lambda i: (0, 0))],
        out_shape=[jax.ShapeDtypeStruct((n_tok, D), F32), jax.ShapeDtypeStruct((8, D), F32)],
        compiler_params=_cparams("arbitrary"),
    )(x3, target, fnw)


def _split(results, n):
    return (*results[:n], list(results[n:]))


def _mixer_weights(w_mix_in, w_gdn, w_pool, w_mo, conv):
    return dict(w_mix=_regroup_mix(_from_chip_major_cols(w_mix_in)), w_gdn=w_gdn.reshape(D, D), w_pool=_from_chip_major_cols(w_pool),
                w_mo=w_mo.reshape(D, D), conv=jnp.pad(_from_chip_major_cols(conv), ((0, 3), (0, 0))))


def _local_step(xc, target, modv, p, late=None):
    n_all = xc.shape[0]
    t_lat = n_all - TM
    nbc, nbx = TM // TS, t_lat // TS
    mod_lat = modv[1:2]
    gather = (lambda arrs: _ChipExchange(arrs, False)) if late else (lambda arrs: None)
    scatter = (lambda arrs: _ChipExchange(arrs, True)) if late else (lambda arrs: None)

    x1, h1, gu1, f1, *got = _ffn_fwd(xc, modv, p["norm1"], p["w1_in"], p["w1_out"], mrow=0, name="ffn1_fwd",
                                     exchange=gather(late and late[0]))
    if late:
        p = {**p, **_mixer_weights(*got)}
    u, p_qkv, p_gate, p_pool, p_br, p_ab = _mix_in_fwd(x1, modv, p["norm2"], p["w_mix"], name="mix_in_fwd")
    qkv, pre_qkv = _prep_fwd(p_qkv, p["conv"], name="prep_fwd")
    s_zero = jnp.zeros((2, NH, HD, HD), F32)
    _, _, sall_cf, sall_cb, s_ctx = _scan_fwd(qkv, p_ab, p["cst"], s_zero, row_blk0=0, nb=nbc, name="scan_ctx")
    o_f, o_b, sall_f, sall_b, _, *got = _scan_fwd(qkv, p_ab, p["cst"], s_ctx, row_blk0=nbc, nb=nbx, name="scan_lat",
                                                  exchange=gather(late and late[1]))
    if late:
        p = {**p, "w2_in": got[0], "w2_out": got[1].reshape(FF, D)}
    pd = _pool(p_pool, row0=TM, transpose=False, name="pool_fwd")
    merge_w = (modv, p["gnw"], p["pool_w"], p["pscale"], p["w_gdn"], p["w_pool"], p["w_mo"])
    x2, og, yp, m, mix = _merge_fwd(x1, o_f, o_b, p_gate, pd, p_br, *merge_w, name="merge_fwd")
    x3, h3, gu3, f3 = _ffn_fwd(x2, mod_lat, p["norm3"], p["w2_in"], p["w2_out"], mrow=6, name="ffn2_fwd")
    dx3, acc_fin = _final(x3, target, p["fnorm"], name="final")

    dx2, a3, df3, dgu3, acc3 = _ffn_bwd(dx3, x2, gu3, f3, mod_lat, p["norm3"], p["w2_in"], p["w2_out"], mrow=6, dx_skip=0,
                                        name="ffn2_bwd")
    g = {}
    tkl = _k_tile(t_lat)
    g["w2_out"] = _matmul_tn(a3, df3, tmm=FF // 2, tn=D, tk=tkl, name="ffn2_wout_grad").reshape(NCHIP, FF // NCHIP, D)
    g["w2_in"] = _matmul_tn(h3, dgu3, tmm=D, tn=2 * FF // NCHIP, tk=tkl, nsplit=NCHIP, name="ffn2_win_grad")
    do, dgate, dpd, dbr, dmix, dyg, dyp, acc_m, dpw = _merge_bwd(dx2, mix, o_f, o_b, p_gate, pd, p_br, *merge_w, name="merge_bwd")
    g["w_mo"] = _matmul_tn(m, dmix, tmm=D, tn=D, tk=tkl, name="wmo_grad").reshape(NCHIP, D // NCHIP, D)
    g["w_gdn"] = _matmul_tn(og, dyg, tmm=D, tn=D, tk=tkl, name="wgdn_grad").reshape(NCHIP, D // NCHIP, D)
    g["w_pool"] = _matmul_tn(yp, dyp, tmm=NPOOL, tn=D // NCHIP, tk=tkl, nsplit=NCHIP, name="wpool_grad")
    dpool_in = _pool(dpd, row0=0, transpose=True, name="pool_bwd")
    acc = (lax.empty((n_all, NQKV), F32), lax.empty((n_all, NQKV), F32), lax.empty((n_all, 128), F32),
           lax.empty((n_all, 128), F32), jnp.zeros((8, 128), F32))
    behind_scan = ("w2_in", "w2_out", "w_gdn", "w_pool", "w_mo")
    *acc, ds_ctx, landed = _split(_scan_bwd(qkv, p_ab, p["cst"], sall_f, sall_b, do, s_zero, *acc, row_blk0=nbc, nb=nbx, has_do=True,
                                            name="scan_lat_bwd", exchange=scatter([g[k] for k in behind_scan])), 6)
    landed = dict(zip(behind_scan, landed))
    dqkv_f, dqkv_b, dab_f, dab_b, dcst, _ = _scan_bwd(qkv, p_ab, p["cst"], sall_cf, sall_cb, jnp.zeros((8, 128), F32), ds_ctx, *acc,
                                                      row_blk0=0, nb=nbc, has_do=False, name="scan_ctx_bwd")
    dpqkv, dconv = _prep_bwd(p_qkv, pre_qkv, dqkv_f, dqkv_b, p["conv"], name="prep_bwd")
    dx1, dp, acc_mix = _mix_in_bwd(dx2, x1, dpqkv, dgate, dpool_in, dbr, dab_f, dab_b, modv, p["norm2"], p["w_mix"],
                                   name="mix_in_bwd")
    tka = _k_tile(n_all)
    g["w_mix"] = _chip_major_cols(_ungroup_mix(_matmul_tn(u, dp, tmm=256, tn=NMIXP, tk=_k_tile(n_all, 1024), name="wmix_grad")))
    dx_lat, a1, df1, dgu1, acc1, got = _split(_ffn_bwd(dx1, xc, gu1, f1, modv, p["norm1"], p["w1_in"], p["w1_out"], mrow=0, dx_skip=1,
                                                   name="ffn1_bwd", exchange=scatter([g["w_mix"]])), 5)
    landed.update(zip(("w_mix",), got))
    g["w1_out"] = _matmul_tn(a1, df1, tmm=FF // 2, tn=D, tk=tka, name="ffn1_wout_grad").reshape(NCHIP, FF // NCHIP, D)
    g["w1_in"] = _matmul_tn(h1, dgu1, tmm=D, tn=2 * FF // NCHIP, tk=tka, nsplit=NCHIP, name="ffn1_win_grad")

    small = dict(norm1=acc1[0, 3] + acc1[1, 3], norm2=acc_mix[0, 3] + acc_mix[1, 3], norm3=acc3[0, 3], fnorm=acc_fin[0],
                 gnw=acc_m[1, :HD], pscale=acc_m[2, :NPOOL], pool_w=dpw, conv=dconv[:5],
                 a_log=dcst[0, 2 * NH:4 * NH], dt_bias=dcst[1, 2 * NH:4 * NH])
    zero = jnp.zeros((D,), F32)
    dmod = jnp.stack([
        jnp.stack([acc1[0, 0], acc1[0, 1], acc1[0, 2], acc_mix[0, 0], acc_mix[0, 1], zero, zero, zero, zero]),
        jnp.stack([acc1[1, 0], acc1[1, 1], acc1[1, 2], acc_mix[1, 0], acc_mix[1, 1], acc_m[0], acc3[0, 0], acc3[0, 1], acc3[0, 2]]),
    ])
    return jnp.sum(acc_fin[1]), dx_lat, g, landed, small, dmod


_HI = lax.Precision.HIGHEST


def _ada_fwd(c_all, w_sh, b_sh, *, name):
    def body(c_ref, w_ref, b_ref, o_ref):
        o_ref[...] = jnp.dot(_silu(c_ref[...]), w_ref[...], precision=_HI, preferred_element_type=F32) + b_ref[...]

    return pl.pallas_call(body, name=name, out_shape=jax.ShapeDtypeStruct((16, w_sh.shape[1]), F32),
                          compiler_params=pltpu.CompilerParams(vmem_limit_bytes=VMEM_LIMIT))(c_all, w_sh, b_sh)


def _ada_bwd(c_all, dm, w_sh, *, name):
    def body(c_ref, dm_ref, w_ref, dw_ref, dc_ref):
        sc = _silu(c_ref[...])
        dw_ref[...] = lax.dot_general(sc, dm_ref[...], (((0,), (0,)), ((), ())), precision=_HI, preferred_element_type=F32)
        part = lax.dot_general(dm_ref[8:9, :], w_ref[...], (((1,), (1,)), ((), ())), precision=_HI, preferred_element_type=F32)
        dc_ref[...] = jnp.broadcast_to(part, dc_ref.shape)

    return pl.pallas_call(body, name=name,
                          out_shape=[jax.ShapeDtypeStruct(w_sh.shape, F32), jax.ShapeDtypeStruct((8, D), F32)],
                          compiler_params=pltpu.CompilerParams(vmem_limit_bytes=VMEM_LIMIT))(c_all, dm, w_sh)


def _cctx_grad(parts, c_ctx, *, name):
    def body(p_ref, c_ref, o_ref):
        tot = (p_ref[0, 0:1, :] + p_ref[2, 0:1, :]) + (p_ref[4, 0:1, :] + p_ref[6, 0:1, :])
        o_ref[...] = tot * _dsilu(c_ref[...])

    return pl.pallas_call(body, name=name, out_shape=jax.ShapeDtypeStruct((1, D), F32))(parts, c_ctx)


_MESH = pl.DeviceIdType.MESH
_ANY = pl.BlockSpec(memory_space=pl.ANY)


def _flip(v, bit):
    return (1 - v) if bit else v


def _all_gather8(x, *, name):
    def body(x_ref, out_ref, send_sems, recv_sems, local_sem):
        mx, my, mc = lax.axis_index("x"), lax.axis_index("y"), lax.axis_index("c")
        me = 4 * mx + 2 * my + mc
        mine = pltpu.make_async_copy(x_ref, out_ref.at[me], local_sem)
        mine.start()
        sends, recvs = [], []
        for k in range(1, 8):
            px, py, pc = _flip(mx, k & 4), _flip(my, k & 2), _flip(mc, k & 1)
            sends.append(pltpu.make_async_remote_copy(src_ref=x_ref, dst_ref=out_ref.at[me], send_sem=send_sems.at[k - 1],
                                                      recv_sem=recv_sems.at[k - 1], device_id=(px, py, pc), device_id_type=_MESH))
            recvs.append(pltpu.make_async_remote_copy(src_ref=x_ref, dst_ref=out_ref.at[4 * px + 2 * py + pc],
                                                      send_sem=send_sems.at[k - 1], recv_sem=recv_sems.at[k - 1],
                                                      device_id=(px, py, pc), device_id_type=_MESH))
        for cp in sends:
            cp.start()
        for cp in recvs:
            cp.wait_recv()
        for cp in sends:
            cp.wait_send()
        mine.wait()

    vm = pl.BlockSpec(memory_space=pltpu.VMEM)
    return pl.pallas_call(
        body, name=name, out_shape=jax.ShapeDtypeStruct((8,) + x.shape, x.dtype), in_specs=[vm], out_specs=vm,
        scratch_shapes=[pltpu.SemaphoreType.DMA((7,)), pltpu.SemaphoreType.DMA((7,)), pltpu.SemaphoreType.DMA],
        compiler_params=pltpu.CompilerParams(vmem_limit_bytes=VMEM_LIMIT),
    )(x)


class _ChipExchange:
    def __init__(self, arrs, scatter):
        self.arrs, self.scatter, self.n = list(arrs), scatter, len(arrs)
        self.out_shape = [jax.ShapeDtypeStruct(a.shape if scatter else (NCHIP,) + a.shape, a.dtype) for a in self.arrs]
        links = self.n * (NCHIP - 1)
        self.scratch = [pltpu.SemaphoreType.DMA((links,)), pltpu.SemaphoreType.DMA((links,)), pltpu.SemaphoreType.DMA((self.n,))]

    def copies(self, ins, outs, send_sems, recv_sems, local_sems):
        mx, my, mc = lax.axis_index("x"), lax.axis_index("y"), lax.axis_index("c")
        me = 2 * mx + my
        local, sends, recvs = [], [], []
        for j in range(self.n):
            src_own = ins[j].at[me] if self.scatter else ins[j]
            local.append(pltpu.make_async_copy(src_own, outs[j].at[me], local_sems.at[j]))
            for k in range(1, NCHIP):
                px, py = _flip(mx, k & 2), _flip(my, k & 1)
                peer = 2 * px + py
                sem = j * (NCHIP - 1) + k - 1
                src = ins[j].at[peer] if self.scatter else ins[j]
                sends.append(pltpu.make_async_remote_copy(src_ref=src, dst_ref=outs[j].at[me], send_sem=send_sems.at[sem],
                                                          recv_sem=recv_sems.at[sem], device_id=(px, py, mc), device_id_type=_MESH))
                recvs.append(pltpu.make_async_remote_copy(src_ref=src, dst_ref=outs[j].at[peer], send_sem=send_sems.at[sem],
                                                          recv_sem=recv_sems.at[sem], device_id=(px, py, mc), device_id_type=_MESH))
        return local, sends, recvs

    @staticmethod
    def start(local, sends, recvs):
        for cp in local + sends:
            cp.start()

    @staticmethod
    def finish(local, sends, recvs):
        for cp in recvs:
            cp.wait_recv()
        for cp in sends:
            cp.wait_send()
        for cp in local:
            cp.wait()


def _chip_exchange(arrs, *, scatter, name):
    ex = _ChipExchange(arrs, scatter)

    def body(*refs):
        cps = ex.copies(refs[:ex.n], refs[ex.n:2 * ex.n], *refs[2 * ex.n:])
        ex.start(*cps)
        ex.finish(*cps)

    return pl.pallas_call(body, name=name, out_shape=ex.out_shape, in_specs=[_ANY] * ex.n, out_specs=[_ANY] * ex.n,
                          scratch_shapes=ex.scratch)(*arrs)


def _pallas(body, operands, *, name, grid, in_specs, out_specs, out_shape, sem, scratch_shapes=(), aliases=None, exchange=None):
    if exchange is None:
        return pl.pallas_call(body, name=name, grid=grid, in_specs=in_specs, out_specs=out_specs, out_shape=out_shape,
                              scratch_shapes=list(scratch_shapes), input_output_aliases=aliases or {},
                              compiler_params=_cparams(*sem))(*operands)
    ex = exchange
    (steps,) = grid
    n_in, n_out, n_scr, k = len(in_specs), len(out_specs), len(scratch_shapes), ex.n

    def hosted(*refs):
        ins, refs = refs[:n_in], refs[n_in:]
        ex_in, refs = refs[:k], refs[k:]
        outs, refs = refs[:n_out], refs[n_out:]
        ex_out, refs = refs[:k], refs[k:]
        scr, ex_sems = refs[:n_scr], refs[n_scr:]
        cps = ex.copies(ex_in, ex_out, *ex_sems)
        pl.when(pl.program_id(0) == 0)(lambda: ex.start(*cps))
        body(*ins, *outs, *scr)
        pl.when(pl.program_id(0) == steps - 1)(lambda: ex.finish(*cps))

    return pl.pallas_call(
        hosted, name=name, grid=grid, in_specs=list(in_specs) + [_ANY] * k, out_specs=list(out_specs) + [_ANY] * k,
        out_shape=list(out_shape) + ex.out_shape, scratch_shapes=list(scratch_shapes) + ex.scratch,
        input_output_aliases=aliases or {}, compiler_params=_cparams("arbitrary"),
    )(*operands, *ex.arrs)


def _core_swap(arrs, *, name):
    n = len(arrs)

    def body(*refs):
        ins, outs = refs[:n], refs[n:2 * n]
        send_sems, recv_sems = refs[2 * n:]
        sib = (lax.axis_index("x"), lax.axis_index("y"), 1 - lax.axis_index("c"))
        cps = [pltpu.make_async_remote_copy(src_ref=ins[j], dst_ref=outs[j], send_sem=send_sems.at[j], recv_sem=recv_sems.at[j],
                                            device_id=sib, device_id_type=_MESH) for j in range(n)]
        for cp in cps:
            cp.start()
        for cp in cps:
            cp.wait_recv()
        for cp in cps:
            cp.wait_send()

    return pl.pallas_call(
        body, name=name, out_shape=[jax.ShapeDtypeStruct(a.shape, a.dtype) for a in arrs],
        in_specs=[_ANY] * n, out_specs=[_ANY] * n,
        scratch_shapes=[pltpu.SemaphoreType.DMA((n,)), pltpu.SemaphoreType.DMA((n,))],
    )(*arrs)


def _row_tile(rows, cols, budget=1 << 18):
    best = None
    for t in range(8, rows + 1, 8):
        if rows % t == 0 and t * cols <= budget:
            best = t
    return best or rows


def _sum_slots(x, *, name):
    ns, r, c = x.shape
    tr = _row_tile(r, c * ns)

    def body(x_ref, o_ref):
        acc = x_ref[0].astype(F32)
        for s in range(1, ns):
            acc = acc + x_ref[s].astype(F32)
        o_ref[...] = acc

    return pl.pallas_call(
        body, name=name, grid=(r // tr,), out_shape=jax.ShapeDtypeStruct((r, c), F32),
        in_specs=[pl.BlockSpec((ns, tr, c), lambda i: (0, i, 0))], out_specs=pl.BlockSpec((tr, c), lambda i: (i, 0)),
        compiler_params=_cparams("parallel"),
    )(x)


def _adamw(w, ga, gb, m, v, *, name):
    r, c = w.shape
    tr = _row_tile(r, c, budget=1 << 17)
    two = gb is not None

    def body(*refs):
        w_ref, ga_ref = refs[0], refs[1]
        m_ref, v_ref = refs[2 + two], refs[3 + two]
        g_ref, d_ref, mo_ref, vo_ref = refs[4 + two:]
        g = ga_ref[...] + refs[2][...] if two else ga_ref[...]
        mn = ADAM_B1 * m_ref[...] + (1.0 - ADAM_B1) * g
        vn = ADAM_B2 * v_ref[...] + (1.0 - ADAM_B2) * (g * g)
        m_hat = mn / (1.0 - ADAM_B1 ** ADAM_STEP)
        v_hat = vn / (1.0 - ADAM_B2 ** ADAM_STEP)
        g_ref[...] = g
        d_ref[...] = -ADAM_LR * (m_hat / (jnp.sqrt(v_hat) + ADAM_EPS) + ADAM_WD * w_ref[...])
        mo_ref[...] = mn
        vo_ref[...] = vn

    spec = pl.BlockSpec((tr, c), lambda i: (i, 0))
    ins = [w, ga] + ([gb] if two else []) + [m, v]
    return pl.pallas_call(
        body, name=name, grid=(r // tr,), out_shape=[jax.ShapeDtypeStruct((r, c), F32)] * 4,
        in_specs=[spec] * len(ins), out_specs=[spec] * 4, compiler_params=_cparams("parallel"),
    )(*ins)


_MIX_AB0, _MIX_AB1 = NQKV, NQKV + 4 * NH


def _regroup_mix(w):
    pad = jnp.zeros((w.shape[0], NMIXP - NMIX), w.dtype)
    return jnp.concatenate([w[:, :_MIX_AB0], w[:, _MIX_AB1:], w[:, _MIX_AB0:_MIX_AB1], pad], axis=1)


def _ungroup_mix(w):
    n_ab = _MIX_AB1 - _MIX_AB0
    return jnp.concatenate([w[:, :_MIX_AB0], w[:, NMIX - n_ab:NMIX], w[:, _MIX_AB0:NMIX - n_ab]], axis=1)


def _chip_major_cols(w):
    r, c = w.shape
    return w.reshape(r, NCHIP, c // NCHIP).transpose(1, 0, 2)


def _from_chip_major_cols(w):
    return w.transpose(1, 0, 2).reshape(w.shape[1], -1)


_SMALL = (("c_ctx", D), ("b_ada", 9 * D), ("norm1_w", D), ("norm2_w", D), ("norm3_w", D), ("final_norm_w", D),
          ("a_log", 2 * NH), ("dt_bias", 2 * NH), ("gdn_norm_w", HD), ("pool_w", 4 * 128 * 128), ("pool_scale", NPOOL),
          ("conv_w", 5 * NQKV // NCHIP))


def _pack(vals, lanes=128, row_mult=8):
    flat = jnp.concatenate([jnp.ravel(v) for v in vals])
    n = flat.shape[0]
    rows = -(-n // (lanes * row_mult)) * row_mult
    return jnp.pad(flat, (0, rows * lanes - n)).reshape(rows, lanes)


def _unpack(packed, sizes):
    flat = packed.reshape(-1)
    out, o = [], 0
    for n in sizes:
        out.append(flat[o:o + n])
        o += n
    return out


def kernel(x, c, ctx, c_ctx, w_ada, b_ada, norm1_w, ffn1_w_in, ffn1_w_out, norm2_w, w_mix_in, conv_w, a_log, dt_bias, gdn_norm_w, w_gdn_proj, pool_w, pool_scale, w_pool_proj, w_mix_out, norm3_w, ffn2_w_in, ffn2_w_out, final_norm_w, loss_target, m_c_ctx, m_w_ada, m_b_ada, m_norm1_w, m_ffn1_w_in, m_ffn1_w_out, m_norm2_w, m_w_mix_in, m_conv_w, m_a_log, m_dt_bias, m_gdn_norm_w, m_w_gdn_proj, m_pool_w, m_pool_scale, m_w_pool_proj, m_w_mix_out, m_norm3_w, m_ffn2_w_in, m_ffn2_w_out, m_final_norm_w, v_c_ctx, v_w_ada, v_b_ada, v_norm1_w, v_ffn1_w_in, v_ffn1_w_out, v_norm2_w, v_w_mix_in, v_conv_w, v_a_log, v_dt_bias, v_gdn_norm_w, v_w_gdn_proj, v_pool_w, v_pool_scale, v_w_pool_proj, v_w_mix_out, v_norm3_w, v_ffn2_w_in, v_ffn2_w_out, v_final_norm_w):
    names = ("c_ctx", "w_ada", "b_ada", "norm1_w", "ffn1_w_in", "ffn1_w_out", "norm2_w", "w_mix_in", "conv_w", "a_log", "dt_bias",
             "gdn_norm_w", "w_gdn_proj", "pool_w", "pool_scale", "w_pool_proj", "w_mix_out", "norm3_w", "ffn2_w_in", "ffn2_w_out",
             "final_norm_w")
    w = dict(zip(names, (c_ctx, w_ada, b_ada, norm1_w, ffn1_w_in, ffn1_w_out, norm2_w, w_mix_in, conv_w, a_log, dt_bias, gdn_norm_w,
                         w_gdn_proj, pool_w, pool_scale, w_pool_proj, w_mix_out, norm3_w, ffn2_w_in, ffn2_w_out, final_norm_w)))
    mom = dict(zip(names, (m_c_ctx, m_w_ada, m_b_ada, m_norm1_w, m_ffn1_w_in, m_ffn1_w_out, m_norm2_w, m_w_mix_in, m_conv_w, m_a_log,
                           m_dt_bias, m_gdn_norm_w, m_w_gdn_proj, m_pool_w, m_pool_scale, m_w_pool_proj, m_w_mix_out, m_norm3_w,
                           m_ffn2_w_in, m_ffn2_w_out, m_final_norm_w)))
    var = dict(zip(names, (v_c_ctx, v_w_ada, v_b_ada, v_norm1_w, v_ffn1_w_in, v_ffn1_w_out, v_norm2_w, v_w_mix_in, v_conv_w, v_a_log,
                           v_dt_bias, v_gdn_norm_w, v_w_gdn_proj, v_pool_w, v_pool_scale, v_w_pool_proj, v_w_mix_out, v_norm3_w,
                           v_ffn2_w_in, v_ffn2_w_out, v_final_norm_w)))
    mx, my, mc = lax.axis_index("x"), lax.axis_index("y"), lax.axis_index("c")
    chip = 2 * mx + my
    dev = 2 * chip + mc
    ada_cols = w_ada.shape[2]

    c_rows = _all_gather8(jnp.pad(c, ((0, 7), (0, 0))), name="gather_c")[:, 0, :]
    c_all = jnp.concatenate([c_rows, c_ctx[None], jnp.zeros((7, D), F32)], axis=0)
    b_sh = lax.dynamic_slice(b_ada, (0, chip * ada_cols), (1, ada_cols))
    mod_sh = _ada_fwd(c_all, w_ada[0], b_sh, name="ada_fwd")
    mod_parts = _all_gather8(mod_sh, name="gather_mod")
    mod_all = jnp.concatenate([mod_parts[2 * s] for s in range(NCHIP)], axis=1)
    mod_lat = lax.dynamic_index_in_dim(mod_all, dev, axis=0, keepdims=False).reshape(9, D)
    modv = jnp.zeros((2, 16, D), F32).at[0, :9].set(mod_all[8].reshape(9, D)).at[1, :9].set(mod_lat)

    big = ("ffn1_w_in", "ffn1_w_out", "w_mix_in", "w_gdn_proj", "w_pool_proj", "w_mix_out", "ffn2_w_in", "ffn2_w_out")
    shard = {k: w[k][0].astype(BF) for k in big}
    w1_in, w1_out = _chip_exchange([shard["ffn1_w_in"], shard["ffn1_w_out"]], scatter=False, name="gather_ffn1")
    p = dict(
        norm1=norm1_w, norm2=norm2_w, norm3=norm3_w, fnorm=final_norm_w[None], w1_in=w1_in, w1_out=w1_out.reshape(FF, D),
        cst=jnp.zeros((8, 128), F32).at[0, 2 * NH:4 * NH].set(jnp.exp(a_log).reshape(-1)).at[1, 2 * NH:4 * NH].set(dt_bias.reshape(-1)),
        gnw=gdn_norm_w, pool_w=pool_w[0], pscale=pool_scale)
    late = ([shard["w_mix_in"], shard["w_gdn_proj"], shard["w_pool_proj"], shard["w_mix_out"], conv_w[0]],
            [shard["ffn2_w_in"], shard["ffn2_w_out"]])

    xc = jnp.concatenate([ctx[0], x[0]], axis=0)
    loss_dev, dx_lat, g, landed, small, dmod = _local_step(xc, loss_target[0], modv, p, late)
    loss = lax.psum(loss_dev, ("x", "y", "c"))
    grad_x = dx_lat[None]

    landed["w1_in"], landed["w1_out"] = _chip_exchange([g["w1_in"], g["w1_out"]], scatter=True, name="scatter_ffn1")
    order = ("w1_in", "w1_out", "w_mix", "w_gdn", "w_pool", "w_mo", "w2_in", "w2_out")
    mine = [_sum_slots(landed[s], name=f"sum_{k}") for k, s in zip(big, order)]
    theirs = _core_swap(mine, name="swap_grad_sums")

    small_vals = [dmod[1], dmod[0], small["norm1"], small["norm2"], small["norm3"], small["fnorm"], small["a_log"], small["dt_bias"],
                  small["gnw"], small["pool_w"], small["pscale"], small["conv"]]
    small_sizes = [v.size for v in small_vals]
    packed = _all_gather8(_pack(small_vals), name="gather_small")
    tot = _unpack(_sum_slots(packed, name="sum_small"), small_sizes)
    dmod_lat_all = packed[:, :9 * D // 128, :].reshape(8, 9 * D)
    dm = jnp.concatenate([dmod_lat_all, tot[1][None], jnp.zeros((7, 9 * D), F32)], axis=0)
    dm_sh = lax.dynamic_slice(dm, (0, chip * ada_cols), (16, ada_cols))
    g_w_ada, cctx_part = _ada_bwd(c_all, dm_sh, w_ada[0], name="ada_bwd")
    g_c_ctx = _cctx_grad(_all_gather8(cctx_part, name="gather_cctx"), c_ctx[None], name="cctx_grad")[0]
    conv_tot = tot[11].reshape(5, NQKV)
    g_small = dict(c_ctx=g_c_ctx, b_ada=tot[0] + tot[1], norm1_w=tot[2], norm2_w=tot[3], norm3_w=tot[4], final_norm_w=tot[5],
                   a_log=tot[6], dt_bias=tot[7], gdn_norm_w=tot[8], pool_w=tot[9], pool_scale=tot[10],
                   conv_w=lax.dynamic_slice(conv_tot, (0, chip * (NQKV // NCHIP)), (5, NQKV // NCHIP)))

    out = {}
    as2d = lambda a: a.reshape(-1, a.shape[-1])
    for k, ga, gb in zip(big, mine, theirs):
        shp = w[k].shape
        res = _adamw(as2d(w[k]), as2d(ga), as2d(gb), as2d(mom[k]), as2d(var[k]), name=f"adamw_{k}")
        out[k] = [r.reshape(shp) for r in res]
    out["w_ada"] = [r.reshape(w_ada.shape) for r in _adamw(w_ada[0], g_w_ada, None, m_w_ada[0], v_w_ada[0], name="adamw_w_ada")]
    sm_names = [n for n, _ in _SMALL]
    sm_sizes = [n for _, n in _SMALL]
    res = _adamw(_pack([w[k] for k in sm_names]), _pack([g_small[k] for k in sm_names]), None,
                 _pack([mom[k] for k in sm_names]), _pack([var[k] for k in sm_names]), name="adamw_small")
    res = [_unpack(r, sm_sizes) for r in res]
    for i, k in enumerate(sm_names):
        out[k] = [r[i].reshape(w[k].shape) for r in res]
    return (loss, grad_x, *[out[k][0] for k in names], *[out[k][1] for k in names], *[out[k][2] for k in names],
            *[out[k][3] for k in names])
```

```python
import functools

import jax
import jax.numpy as jnp
from jax import lax
from jax.experimental import pallas as pl
from jax.experimental.pallas import tpu as pltpu

F32 = jnp.float32
BF = jnp.bfloat16

D = 1024
FF = 2816
NH = 8
HD = 128
CH = 64
GW = 64
TM = 256
NQKV = 3 * NH * HD
NPOOL = 512
POOL_WINDOWS = (2, 4, 8, 16)
NMIX = 6688
NMIXP = 6784
EPS = 1e-6
NCHIP = 4
VMEM_LIMIT = 56 * 1024 * 1024

ADAM_LR, ADAM_B1, ADAM_B2, ADAM_EPS, ADAM_WD, ADAM_STEP = 0.001, 0.9, 0.999, 1e-08, 0.01, 10


def _cparams(*sem):
    return pltpu.CompilerParams(dimension_semantics=sem, vmem_limit_bytes=VMEM_LIMIT)


def _const_spec(shape):
    nd = len(shape)
    return pl.BlockSpec(shape, lambda *_: (0,) * nd, pipeline_mode=pl.Buffered(1))


def _dot(a, b, dims):
    return lax.dot_general(a.astype(BF), b.astype(BF), (dims, ((), ())), preferred_element_type=F32)


def _nn(a, b):
    return _dot(a, b, ((1,), (0,)))


def _nt(a, b):
    return _dot(a, b, ((1,), (1,)))


def _tn(a, b):
    return _dot(a, b, ((0,), (0,)))


def _silu(x):
    return x * jax.nn.sigmoid(x)


def _dsilu(x):
    s = jax.nn.sigmoid(x)
    return s * (1.0 + x * (1.0 - s))


def _norm_mod(x, nw, shift, scale):
    r = lax.rsqrt(jnp.mean(x * x, axis=-1, keepdims=True) + EPS)
    xh = x * r
    n = xh * nw
    return n * (1.0 + scale) + shift, n, xh, r


def _norm_mod_bwd(dh, n, xh, r, nw, scale):
    dn = dh * (1.0 + scale)
    dxh = dn * nw
    dx = r * (dxh - xh * jnp.mean(dxh * xh, axis=-1, keepdims=True))
    rs = lambda t: jnp.sum(t, axis=0, keepdims=True)
    return dx, rs(dh), rs(dh * n), rs(dn * xh)


def _ffn_fwd(x, modv, nw, w_in4, w_out, *, mrow, name, exchange=None):
    n_tok = x.shape[0]
    nt = n_tok // TM
    nset = modv.shape[0]
    ws = w_in4.shape[2]

    def body(x_ref, mod_ref, nw_ref, win_ref, wout_ref, x1_ref, h_ref, gu_ref, f_ref):
        xv = x_ref[...]
        shift, scale, gate = mod_ref[0, mrow:mrow + 1, :], mod_ref[0, mrow + 1:mrow + 2, :], mod_ref[0, mrow + 2:mrow + 3, :]
        h, _, _, _ = _norm_mod(xv, nw_ref[...], shift, scale)
        hb = h.astype(BF)
        h_ref[...] = hb
        gus = [_nn(hb, win_ref[s]) for s in range(NCHIP)]
        for s in range(NCHIP):
            gu_ref[:, s * ws:(s + 1) * ws] = gus[s].astype(BF)
        g = jnp.concatenate(gus[:2], axis=1)
        u = jnp.concatenate(gus[2:], axis=1)
        f = _nn(_silu(g) * u, wout_ref[...])
        f_ref[...] = f.astype(BF)
        x1_ref[...] = xv + 0.5 * gate * f

    tile = lambda w: pl.BlockSpec((TM, w), lambda i: (i, 0))
    return _pallas(
        body, (x, modv, nw, w_in4, w_out), name=name, grid=(nt,), sem=("parallel",), exchange=exchange,
        in_specs=[tile(D), pl.BlockSpec((1, 16, D), lambda i: (jnp.minimum(i, nset - 1), 0, 0)), _const_spec((1, D)),
                  _const_spec(w_in4.shape), _const_spec(w_out.shape)],
        out_specs=[tile(D), tile(D), tile(2 * FF), tile(D)],
        out_shape=[jax.ShapeDtypeStruct((n_tok, D), F32), jax.ShapeDtypeStruct((n_tok, D), BF),
                   jax.ShapeDtypeStruct((n_tok, 2 * FF), BF), jax.ShapeDtypeStruct((n_tok, D), BF)])


def _ffn_bwd(dxo, x, gu, fo, modv, nw, w_in4, w_out, *, mrow, dx_skip, name, exchange=None):
    n_tok = x.shape[0]
    nt = n_tok // TM
    nset = modv.shape[0]
    ws = w_in4.shape[2]

    def body(dxo_ref, x_ref, gu_ref, f_ref, mod_ref, nw_ref, win_ref, wout_ref, dx_ref, a_ref, df_ref, dgu_ref, acc_ref):
        i = pl.program_id(0)
        xv = x_ref[...]
        dxo_v = dxo_ref[...]
        shift, scale, gate = mod_ref[0, mrow:mrow + 1, :], mod_ref[0, mrow + 1:mrow + 2, :], mod_ref[0, mrow + 2:mrow + 3, :]
        _, n, xh, r = _norm_mod(xv, nw_ref[...], shift, scale)
        df = 0.5 * gate * dxo_v
        dfb = df.astype(BF)
        df_ref[...] = dfb
        dgate = jnp.sum(0.5 * dxo_v * f_ref[...].astype(F32), axis=0, keepdims=True)
        da = _nt(dfb, wout_ref[...])
        g = gu_ref[:, :FF].astype(F32)
        u = gu_ref[:, FF:].astype(F32)
        sg = _silu(g)
        a_ref[...] = (sg * u).astype(BF)
        dgu_ref[:, :FF] = (da * u * _dsilu(g)).astype(BF)
        dgu_ref[:, FF:] = (da * sg).astype(BF)
        dh = _nt(dgu_ref[:, 0:ws], win_ref[0])
        for s in range(1, NCHIP):
            dh = dh + _nt(dgu_ref[:, s * ws:(s + 1) * ws], win_ref[s])
        dx, dshift, dscale, dnw = _norm_mod_bwd(dh, n, xh, r, nw_ref[...], scale)
        dx_ref[...] = dxo_v + dx

        @pl.when((i == 0) | (i == nset - 1))
        def _():
            acc_ref[...] = jnp.zeros_like(acc_ref)

        acc_ref[0, 0:1, :] += dshift
        acc_ref[0, 1:2, :] += dscale
        acc_ref[0, 2:3, :] += dgate
        acc_ref[0, 3:4, :] += dnw

    tile = lambda w: pl.BlockSpec((TM, w), lambda i: (i, 0))
    return _pallas(
        body, (dxo, x, gu, fo, modv, nw, w_in4, w_out), name=name, grid=(nt,), sem=("arbitrary",), exchange=exchange,
        in_specs=[tile(D), tile(D), tile(2 * FF), tile(D),
                  pl.BlockSpec((1, 16, D), lambda i: (jnp.minimum(i, nset - 1), 0, 0)), _const_spec((1, D)),
                  _const_spec(w_in4.shape), _const_spec(w_out.shape)],
        out_specs=[pl.BlockSpec((TM, D), lambda i: (jnp.maximum(i - dx_skip, 0), 0)), tile(FF), tile(D), tile(2 * FF),
                   pl.BlockSpec((1, 8, D), lambda i: (jnp.minimum(i, nset - 1), 0, 0))],
        out_shape=[jax.ShapeDtypeStruct((n_tok - dx_skip * TM, D), F32), jax.ShapeDtypeStruct((n_tok, FF), BF),
                   jax.ShapeDtypeStruct((n_tok, D), BF), jax.ShapeDtypeStruct((n_tok, 2 * FF), BF),
                   jax.ShapeDtypeStruct((nset, 8, D), F32)])


def _k_tile(n, target=3072):
    return max(t for t in range(TM, min(n, target) + 1, TM) if n % t == 0)


def _matmul_tn(a, b, *, tmm, tn, tk, nsplit=1, name):
    n_tok, m = a.shape
    kk = b.shape[1]
    nk = n_tok // tk

    def body(a_ref, b_ref, o_ref, acc):
        k = pl.program_id(2)

        @pl.when(k == 0)
        def _():
            acc[...] = jnp.zeros_like(acc)

        acc[...] += _tn(a_ref[...], b_ref[...])

        @pl.when(k == nk - 1)
        def _():
            o_ref[...] = acc[...].astype(BF).reshape(o_ref.shape)

    if nsplit == 1:
        out_shape = jax.ShapeDtypeStruct((m, kk), BF)
        out_spec = pl.BlockSpec((tmm, tn), lambda i, j, k: (i, j))
    else:
        assert tn == kk // nsplit
        out_shape = jax.ShapeDtypeStruct((nsplit, m, tn), BF)
        out_spec = pl.BlockSpec((1, tmm, tn), lambda i, j, k: (j, i, 0))
    return pl.pallas_call(
        body, name=name, grid=(m // tmm, kk // tn, nk),
        in_specs=[pl.BlockSpec((tk, tmm), lambda i, j, k: (k, i)), pl.BlockSpec((tk, tn), lambda i, j, k: (k, j))],
        out_specs=out_spec, out_shape=out_shape,
        scratch_shapes=[pltpu.VMEM((tmm, tn), F32)],
        compiler_params=_cparams("parallel", "parallel", "arbitrary"),
    )(a, b)


_MIX_PARTS = (("qkv", 0, NQKV), ("gate", NQKV, 1024), ("pool", NQKV + 1024, NPOOL), ("br", NQKV + 1024 + NPOOL, 2048),
              ("ab", NMIXP - 128, 128))


def _mix_in_fwd(x1, modv, nw, w_mix, *, name):
    n_tok = x1.shape[0]

    def body(x_ref, mod_ref, nw_ref, w_ref, u_ref, *p_refs):
        u, _, _, _ = _norm_mod(x_ref[...], nw_ref[...], mod_ref[0, 3:4, :], mod_ref[0, 4:5, :])
        ub = u.astype(BF)
        u_ref[...] = ub
        for (_, c0, w), p_ref in zip(_MIX_PARTS, p_refs):
            p_ref[...] = _nn(ub, w_ref[:, c0:c0 + w])

    tile = lambda w: pl.BlockSpec((TM, w), lambda i: (i, 0))
    ctile = lambda w: pl.BlockSpec((TM, w), lambda i: (i + 1, 0))
    return pl.pallas_call(
        body, name=name, grid=(n_tok // TM,),
        in_specs=[tile(D), pl.BlockSpec((1, 16, D), lambda i: (jnp.minimum(i, 1), 0, 0)), _const_spec((1, D)),
                  _const_spec(w_mix.shape)],
        out_specs=[tile(D)] + [tile(w) for _, _, w in _MIX_PARTS],
        out_shape=[jax.ShapeDtypeStruct((n_tok, D), BF)] + [jax.ShapeDtypeStruct((n_tok, w), F32) for _, _, w in _MIX_PARTS],
        compiler_params=_cparams("parallel"),
    )(x1, modv, nw, w_mix)


def _mix_in_bwd(dxo, x1, dqkv, dgate, dpool, dbr, dab_f, dab_b, modv, nw, w_mix, *, name):
    n_tok = x1.shape[0]

    def body(dxo_ref, x_ref, dqkv_ref, dgate_ref, dpool_ref, dbr_ref, dabf_ref, dabb_ref, mod_ref, nw_ref, w_ref,
             dx_ref, dp_ref, acc_ref):
        i = pl.program_id(0)
        lat = i >= 1
        scale = mod_ref[0, 4:5, :]
        _, n, xh, r = _norm_mod(x_ref[...], nw_ref[...], mod_ref[0, 3:4, :], scale)
        dp_ref[:, 0:NQKV] = dqkv_ref[...].astype(BF)
        dp_ref[:, NQKV:NQKV + 1024] = jnp.where(lat, dgate_ref[...], 0.0).astype(BF)
        dp_ref[:, NQKV + 1024:NQKV + 1536] = jnp.where(lat, dpool_ref[...], 0.0).astype(BF)
        dp_ref[:, NQKV + 1536:NMIXP - 128] = jnp.where(lat, dbr_ref[...], 0.0).astype(BF)
        dp_ref[:, NMIXP - 128:] = (dabf_ref[...] + dabb_ref[...]).astype(BF)
        du = _nt(dp_ref[...], w_ref[...])
        dx, dshift, dscale, dnw = _norm_mod_bwd(du, n, xh, r, nw_ref[...], scale)
        dx_ref[...] = jnp.where(lat, dxo_ref[...], 0.0) + dx

        @pl.when(i <= 1)
        def _():
            acc_ref[...] = jnp.zeros_like(acc_ref)

        acc_ref[0, 0:1, :] += dshift
        acc_ref[0, 1:2, :] += dscale
        acc_ref[0, 3:4, :] += dnw

    tile = lambda w: pl.BlockSpec((TM, w), lambda i: (i, 0))
    ltile = lambda w: pl.BlockSpec((TM, w), lambda i: (jnp.maximum(i - 1, 0), 0))
    return pl.pallas_call(
        body, name=name, grid=(n_tok // TM,),
        in_specs=[ltile(D), tile(D), tile(NQKV), ltile(1024), ltile(NPOOL), ltile(2048), tile(128), tile(128),
                  pl.BlockSpec((1, 16, D), lambda i: (jnp.minimum(i, 1), 0, 0)), _const_spec((1, D)), _const_spec(w_mix.shape)],
        out_specs=[tile(D), tile(NMIXP), pl.BlockSpec((1, 8, D), lambda i: (jnp.minimum(i, 1), 0, 0))],
        out_shape=[jax.ShapeDtypeStruct((n_tok, D), F32), jax.ShapeDtypeStruct((n_tok, NMIXP), BF),
                   jax.ShapeDtypeStruct((2, 8, D), F32)],
        compiler_params=_cparams("arbitrary"),
    )(dxo, x1, dqkv, dgate, dpool, dbr, dab_f, dab_b, modv, nw, w_mix)


def _qkv_act(pre, j):
    s = _silu(pre)
    nrm = s * lax.rsqrt(jnp.sum(s * s, axis=-1, keepdims=True) + EPS)
    nrm = nrm * jnp.where(j == 0, HD ** -0.5, 1.0)
    return jnp.where(j < 2, nrm, s)


def _halo_specs(nt):
    r = TM // 8
    w = NH * HD
    main = pl.BlockSpec((TM, w), lambda j, i: (i, j))
    prev = pl.BlockSpec((8, w), lambda j, i: (jnp.maximum(i * r - 1, 0), j))
    nxt = pl.BlockSpec((8, w), lambda j, i: (jnp.minimum((i + 1) * r, nt * r - 1), j))
    return main, prev, nxt


def _prep_fwd(p_qkv, conv_w8, *, name):
    n_tok = p_qkv.shape[0]
    nt = n_tok // TM

    def body(x_ref, xp_ref, xn_ref, w_ref, o_ref, pre_ref, win):
        j, i = pl.program_id(0), pl.program_id(1)
        has_prev = (i != 0) & (i != 1)
        has_next = (i != 0) & (i != nt - 1)
        win[0:8, :] = jnp.where(has_prev, xp_ref[...], 0.0)
        win[8:8 + TM, :] = x_ref[...]
        win[8 + TM:, :] = jnp.where(has_next, xn_ref[...], 0.0)
        for h in range(NH):
            hs = slice(h * HD, (h + 1) * HD)
            pre = win[6:6 + TM, hs] * w_ref[0:1, hs]
            for k in range(1, 5):
                pre = pre + win[6 + k:6 + k + TM, hs] * w_ref[k:k + 1, hs]
            pre_ref[:, hs] = pre
            o_ref[:, hs] = _qkv_act(pre, j)

    main, prev, nxt = _halo_specs(nt)
    wq = NH * HD
    return pl.pallas_call(
        body, name=name, grid=(3, nt),
        in_specs=[main, prev, nxt, pl.BlockSpec((8, wq), lambda j, i: (0, j))],
        out_specs=[main, main], out_shape=[jax.ShapeDtypeStruct((n_tok, NQKV), F32)] * 2,
        scratch_shapes=[pltpu.VMEM((TM + 16, wq), F32)],
        compiler_params=_cparams("parallel", "arbitrary"),
    )(p_qkv, p_qkv, p_qkv, conv_w8)


def _prep_bwd(p_qkv, pre, dqkv_f, dqkv_b, conv_w8, *, name):
    n_tok = p_qkv.shape[0]
    nt = n_tok // TM
    wq = NH * HD

    def body(x_ref, p_ref, pp_ref, pn_ref, g_ref, gp_ref, gn_ref, g2_ref, g2p_ref, g2n_ref, w_ref, dx_ref, dw_ref, pwin, gwin, dwin):
        j, i = pl.program_id(0), pl.program_id(1)
        has_prev = (i != 0) & (i != 1)
        has_next = (i != 0) & (i != nt - 1)
        pwin[0:8, :] = jnp.where(has_prev, pp_ref[...], 0.0)
        pwin[8:8 + TM, :] = p_ref[...]
        pwin[8 + TM:, :] = jnp.where(has_next, pn_ref[...], 0.0)
        gwin[0:8, :] = jnp.where(has_prev, gp_ref[...] + g2p_ref[...], 0.0)
        gwin[8:8 + TM, :] = g_ref[...] + g2_ref[...]
        gwin[8 + TM:, :] = jnp.where(has_next, gn_ref[...] + g2n_ref[...], 0.0)

        @pl.when(i == 0)
        def _():
            dw_ref[...] = jnp.zeros_like(dw_ref)

        for h in range(NH):
            hs = slice(h * HD, (h + 1) * HD)
            _, vjp = jax.vjp(lambda t: _qkv_act(t, j), pwin[:, hs])
            dwin[:, hs] = vjp(gwin[:, hs])[0]
            xv = x_ref[:, hs]
            dx = None
            for k in range(5):
                sh = dwin[10 - k:10 - k + TM, hs]
                dx = sh * w_ref[k:k + 1, hs] if dx is None else dx + sh * w_ref[k:k + 1, hs]
                dw_ref[k:k + 1, hs] += jnp.sum(sh * xv, axis=0, keepdims=True)
            dx_ref[:, hs] = dx

    main, prev, nxt = _halo_specs(nt)
    wspec = pl.BlockSpec((8, wq), lambda j, i: (0, j))
    return pl.pallas_call(
        body, name=name, grid=(3, nt),
        in_specs=[main, main, prev, nxt, main, prev, nxt, main, prev, nxt, wspec],
        out_specs=[main, wspec],
        out_shape=[jax.ShapeDtypeStruct((n_tok, NQKV), F32), jax.ShapeDtypeStruct((8, NQKV), F32)],
        scratch_shapes=[pltpu.VMEM((TM + 16, wq), F32)] * 3,
        compiler_params=_cparams("parallel", "arbitrary"),
    )(p_qkv, pre, pre, pre, dqkv_f, dqkv_f, dqkv_f, dqkv_b, dqkv_b, dqkv_b, conv_w8)


@jax.custom_vjp
def _mm_nn(a, b):
    return _nn(a, b)


@jax.custom_vjp
def _mm_nt(a, b):
    return _nt(a, b)


@jax.custom_vjp
def _mm_tn(a, b):
    return _tn(a, b)


_mm_nn.defvjp(lambda a, b: (_nn(a, b), (a, b)), lambda r, g: (_mm_nt(g, r[1]), _mm_tn(r[0], g)))
_mm_nt.defvjp(lambda a, b: (_nt(a, b), (a, b)), lambda r, g: (_mm_nn(g, r[1]), _mm_tn(g, r[0])))
_mm_tn.defvjp(lambda a, b: (_tn(a, b), (a, b)), lambda r, g: (_mm_nt(r[1], g), _mm_nn(r[0], g)))


def _each(f, *lists):
    return tuple(f(*a) for a in zip(*lists))


def _unit_tri_inv(ls, revs):
    ii = lax.broadcasted_iota(jnp.int32, (CH, CH), 0)
    jj = lax.broadcasted_iota(jnp.int32, (CH, CH), 1)
    eye = (ii == jj).astype(F32)
    xs = None
    s = 1
    while s < CH:
        same = (ii & -(2 * s)) == (jj & -(2 * s))
        off = {False: same & ((ii & s) != 0) & ((jj & s) == 0), True: same & ((jj & s) != 0) & ((ii & s) == 0)}
        cs = _each(lambda l, r: jnp.where(off[r], l, 0.0), ls, revs)
        if xs is None:
            xs = _each(lambda c: eye - c, cs)
        else:
            xc = _each(_nn, xs, cs)
            xcx = _each(_nn, xc, xs)
            xs = _each(lambda x, t: x - t, xs, xcx)
        s *= 2
    return xs


@functools.lru_cache(maxsize=None)
def _tri_solve(revs):
    @jax.custom_vjp
    def solve(ls, rhss):
        return _each(_mm_nn, _unit_tri_inv(ls, revs), rhss)

    def fwd(ls, rhss):
        ainv = _unit_tri_inv(ls, revs)
        xs = _each(_mm_nn, ainv, rhss)
        return xs, (ainv, xs)

    def bwd(res, gs):
        ainv, xs = res
        drhs = _each(_mm_tn, ainv, gs)
        return _each(lambda d, x: -_mm_nt(d, x), drhs, xs), drhs

    solve.defvjp(fwd, bwd)
    return solve


def _chunk_prep(q, k, v, beta, g, *, revs):
    ii = lax.broadcasted_iota(jnp.int32, (CH, CH), 0)
    jj = lax.broadcasted_iota(jnp.int32, (CH, CH), 1)
    eye = ii == jj
    incl_of = {False: ii >= jj, True: ii <= jj}
    strict_of = {False: ii > jj, True: ii < jj}
    g_row = _each(lambda t: jnp.sum(jnp.where(eye, t, 0.0), axis=0, keepdims=True), g)
    cum = _each(lambda t, r: jnp.sum(jnp.where(incl_of[r], t, 0.0), axis=1, keepdims=True), g_row, revs)
    cum_row = _each(lambda t: jnp.sum(jnp.where(eye, t, 0.0), axis=0, keepdims=True), cum)
    total = _each(lambda t: jnp.sum(t, axis=0, keepdims=True), g)
    decay = _each(lambda c, cr, r: jnp.where(incl_of[r], jnp.exp(jnp.where(incl_of[r], c - cr, 0.0)), 0.0), cum, cum_row, revs)
    kb = _each(jnp.multiply, k, beta)
    vb = _each(jnp.multiply, v, beta)
    kk = _each(_mm_nt, kb, k)
    lmat = _each(lambda t, dc, r: jnp.where(strict_of[r], t * dc, 0.0), kk, decay, revs)
    ecum = _each(jnp.exp, cum)
    rhs = _each(lambda a, b, e: jnp.concatenate([a, b * e], axis=1), vb, kb, ecum)
    sol = _tri_solve(revs)(lmat, rhs)
    qk = _each(_mm_nt, q, k)
    aqk = _each(jnp.multiply, qk, decay)
    qd = _each(jnp.multiply, q, ecum)
    kd = _each(lambda a, t, c: a * jnp.exp(t - c), k, total, cum)
    return sol, aqk, qd, kd, _each(jnp.exp, total)


def _chunk_rec(sol, aqk, qd, kd, bl, s):
    ws = _each(lambda so, st: _mm_nn(so[:, HD:], st), sol, s)
    v_new = _each(lambda so, t: so[:, :HD] - t, sol, ws)
    qs = _each(_mm_nn, qd, s)
    av = _each(_mm_nn, aqk, v_new)
    o = _each(jnp.add, qs, av)
    kv = _each(_mm_tn, kd, v_new)
    s_new = _each(lambda st, b, u: st * b + u, s, bl, kv)
    return o, s_new


def _lane_col(x, c):
    lane = lax.broadcasted_iota(jnp.int32, x.shape, 1)
    return jnp.sum(jnp.where(lane == c, x, 0.0), axis=1, keepdims=True)


def _beta_g(ab, cst, d, h):
    braw = _lane_col(ab, NH * d + h)
    araw = _lane_col(ab, 2 * NH + NH * d + h)
    ea = _lane_col(cst[0:1, :], 2 * NH + NH * d + h)
    dt = _lane_col(cst[1:2, :], 2 * NH + NH * d + h)
    z = araw + dt
    softplus = jnp.maximum(z, 0.0) + jnp.log(1.0 + jnp.exp(-jnp.abs(z)))
    return jax.nn.sigmoid(braw), -ea * softplus, z, ea


STEPS = 2
TS = STEPS * CH
_CHAINS = tuple((t, d, h) for t in range(STEPS) for d in (0, 1) for h in range(NH))
_REVS = tuple(bool(d) for _, d, _ in _CHAINS)
_PER_STEP = 2 * NH


def _chain_inputs(refs, r0s, ab_refs, cst):
    hs = lambda h: slice(h * HD, (h + 1) * HD)
    abvs = [[ab_refs[d][pl.ds(r0s[t][d], CH), :] for d in (0, 1)] for t in range(STEPS)]
    q = _each(lambda c: refs[c[1]][0][pl.ds(r0s[c[0]][c[1]], CH), hs(c[2])], _CHAINS)
    k = _each(lambda c: refs[c[1]][1][pl.ds(r0s[c[0]][c[1]], CH), hs(c[2])], _CHAINS)
    v = _each(lambda c: refs[c[1]][2][pl.ds(r0s[c[0]][c[1]], CH), hs(c[2])], _CHAINS)
    bg = _each(lambda c: _beta_g(abvs[c[0]][c[1]], cst, c[1], c[2]), _CHAINS)
    return q, k, v, bg


def _of_step(parts, t):
    return tuple(p[t * _PER_STEP:(t + 1) * _PER_STEP] for p in parts)


def _scan_fwd(qkv, ab, cst, s0, *, row_blk0, nb, name, exchange=None):
    cb = TS // CH
    w = NH * HD

    def body(qf, kf, vf, abf, qb, kb, vb, abb, cst_ref, s0_ref, of_ref, ob_ref, sallf_ref, sallb_ref, sfin_ref, s_scr):
        i = pl.program_id(0)

        @pl.when(i == 0)
        def _():
            s_scr[...] = s0_ref[...]

        o_refs, sall_refs = (of_ref, ob_ref), (sallf_ref, sallb_ref)

        def chunks(ci, carry):
            cs = [(ci * STEPS + t, cb - 1 - ci * STEPS - t) for t in range(STEPS)]
            r0s = [tuple(pl.multiple_of(c * CH, CH) for c in ct) for ct in cs]
            q, k, v, bg = _chain_inputs(((qf, kf, vf), (qb, kb, vb)), r0s, (abf, abb), cst_ref[...])
            parts = _chunk_prep(q, k, v, _each(lambda t: t[0], bg), _each(lambda t: t[1], bg), revs=_REVS)
            s = _each(lambda c: s_scr[c[1], c[2]], _CHAINS[:_PER_STEP])
            for t in range(STEPS):
                for (_, d, h), sv in zip(_CHAINS, s):
                    sall_refs[d][cs[t][d], h] = sv
                o, s = _chunk_rec(*_of_step(parts, t), s)
                for (_, d, h), ov in zip(_CHAINS, o):
                    o_refs[d][pl.ds(r0s[t][d], CH), h * HD:(h + 1) * HD] = ov
            for (_, d, h), sv in zip(_CHAINS, s):
                s_scr[d, h] = sv
            return carry

        lax.fori_loop(0, cb // STEPS, chunks, 0)

        @pl.when(i == nb - 1)
        def _():
            sfin_ref[...] = s_scr[...]

    pos = (lambda i: i, lambda i: nb - 1 - i)
    col = lambda d, c: pl.BlockSpec((TS, w), lambda i: (row_blk0 + pos[d](i), c))
    abs_ = lambda d: pl.BlockSpec((TS, 128), lambda i: (row_blk0 + pos[d](i), 0))
    full4 = pl.BlockSpec((2, NH, HD, HD), lambda i: (0, 0, 0, 0))
    o_spec = lambda d: pl.BlockSpec((TS, w), lambda i: (pos[d](i), 0))
    sall_spec = lambda d: pl.BlockSpec((cb, NH, HD, HD), lambda i: (pos[d](i), 0, 0, 0))
    return _pallas(
        body, (qkv, qkv, qkv, ab, qkv, qkv, qkv, ab, cst, s0), name=name, grid=(nb,), sem=("arbitrary",), exchange=exchange,
        in_specs=[col(0, 0), col(0, 1), col(0, 2), abs_(0), col(1, 0), col(1, 1), col(1, 2), abs_(1),
                  pl.BlockSpec((8, 128), lambda i: (0, 0)), full4],
        out_specs=[o_spec(0), o_spec(1), sall_spec(0), sall_spec(1), full4],
        out_shape=[jax.ShapeDtypeStruct((nb * TS, w), F32)] * 2 + [jax.ShapeDtypeStruct((nb * cb, NH, HD, HD), F32)] * 2
        + [jax.ShapeDtypeStruct((2, NH, HD, HD), F32)],
        scratch_shapes=[pltpu.VMEM((2, NH, HD, HD), F32)])


def _scan_bwd(qkv, ab, cst, sall_f, sall_b, do, dsfin, dqkv_f, dqkv_b, dab_f, dab_b, dcst, *, row_blk0, nb, has_do, name,
              exchange=None):
    cb = TS // CH
    w = NH * HD

    def body(qf, kf, vf, abf, qb, kb, vb, abb, cst_ref, sallf_ref, sallb_ref, dof_ref, dob_ref, dsfin_ref, _f, _b, _af, _ab, dcst_in,
             dqkvf_ref, dqkvb_ref, dabf_ref, dabb_ref, dcst_ref, ds0_ref, ds_scr):
        i = pl.program_id(0)

        @pl.when(i == 0)
        def _():
            ds_scr[...] = dsfin_ref[...]
            dcst_ref[...] = dcst_in[...]

        lane = lax.broadcasted_iota(jnp.int32, (CH, 128), 1)
        lane1 = lax.broadcasted_iota(jnp.int32, (1, 128), 1)
        sall_refs, do_refs = (sallf_ref, sallb_ref), (dof_ref, dob_ref)
        dqkv_refs, dab_refs = (dqkvf_ref, dqkvb_ref), (dabf_ref, dabb_ref)

        def chunks(ci, carry):
            cs = [(cb - 1 - ci * STEPS - t, ci * STEPS + t) for t in range(STEPS)]
            r0s = [tuple(pl.multiple_of(c * CH, CH) for c in ct) for ct in cs]
            q, k, v, bg = _chain_inputs(((qf, kf, vf), (qb, kb, vb)), r0s, (abf, abb), cst_ref[...])
            beta, g = _each(lambda t: t[0], bg), _each(lambda t: t[1], bg)
            parts, prep_vjp = jax.vjp(functools.partial(_chunk_prep, revs=_REVS), q, k, v, beta, g)
            ds = _each(lambda c: ds_scr[c[1], c[2]], _CHAINS[:_PER_STEP])
            dparts = []
            for t in range(STEPS):
                s = _each(lambda c: sall_refs[c[1]][cs[t][c[1]], c[2]], _CHAINS[:_PER_STEP])
                _, rec_vjp = jax.vjp(_chunk_rec, *_of_step(parts, t), s)
                do_t = _each(lambda c: do_refs[c[1]][pl.ds(r0s[t][c[1]], CH), c[2] * HD:(c[2] + 1) * HD] if has_do
                             else jnp.zeros((CH, HD), F32), _CHAINS[:_PER_STEP])
                *dpt, ds = rec_vjp((do_t, ds))
                dparts.append(dpt)
            for (_, d, h), dsv in zip(_CHAINS, ds):
                ds_scr[d, h] = dsv
            dq, dk, dv, dbeta, dg = prep_vjp(tuple(sum((dparts[t][j] for t in range(STEPS)), ()) for j in range(len(dparts[0]))))
            dab = [[jnp.zeros((CH, 128), F32), jnp.zeros((CH, 128), F32)] for _ in range(STEPS)]
            dal = jnp.zeros((1, 128), F32)
            for n, (t, d, h) in enumerate(_CHAINS):
                for part, val in enumerate((dq[n], dk[n], dv[n])):
                    dqkv_refs[d][pl.ds(r0s[t][d], CH), part * w + h * HD:part * w + (h + 1) * HD] = val
                z, ea = bg[n][2], bg[n][3]
                dbraw = dbeta[n] * beta[n] * (1.0 - beta[n])
                daraw = dg[n] * (-ea) * jax.nn.sigmoid(z)
                dab[t][d] = dab[t][d] + jnp.where(lane == NH * d + h, dbraw, 0.0) + jnp.where(lane == 2 * NH + NH * d + h, daraw, 0.0)
                dal = dal + jnp.where(lane1 == 2 * NH + NH * d + h, jnp.sum(dg[n] * g[n], axis=0, keepdims=True), 0.0)
            dsum = jnp.zeros((CH, 128), F32)
            for t in range(STEPS):
                for d in (0, 1):
                    dab_refs[d][pl.ds(r0s[t][d], CH), :] = dab[t][d]
                    dsum = dsum + dab[t][d]
            dcst_ref[0:1, :] += dal
            dcst_ref[1:2, :] += jnp.sum(jnp.where(lane >= 2 * NH, dsum, 0.0), axis=0, keepdims=True)
            return carry

        lax.fori_loop(0, cb // STEPS, chunks, 0)

        @pl.when(i == nb - 1)
        def _():
            ds0_ref[...] = ds_scr[...]

    pos = (lambda i: nb - 1 - i, lambda i: i)
    col = lambda d, c: pl.BlockSpec((TS, w), lambda i: (row_blk0 + pos[d](i), c))
    abs_ = lambda d: pl.BlockSpec((TS, 128), lambda i: (row_blk0 + pos[d](i), 0))
    full4 = pl.BlockSpec((2, NH, HD, HD), lambda i: (0, 0, 0, 0))
    small = pl.BlockSpec((8, 128), lambda i: (0, 0))
    hbm = pl.BlockSpec(memory_space=pl.ANY)
    sall_spec = lambda d: pl.BlockSpec((cb, NH, HD, HD), lambda i: (pos[d](i), 0, 0, 0))
    do_spec = (lambda d: pl.BlockSpec((TS, w), lambda i: (pos[d](i), 0))) if has_do else (lambda d: small)
    acc_specs = [pl.BlockSpec((TS, 3 * w), lambda i: (row_blk0 + pos[0](i), 0)),
                 pl.BlockSpec((TS, 3 * w), lambda i: (row_blk0 + pos[1](i), 0)), abs_(0), abs_(1), small]
    return _pallas(
        body, (qkv, qkv, qkv, ab, qkv, qkv, qkv, ab, cst, sall_f, sall_b, do, do, dsfin, dqkv_f, dqkv_b, dab_f, dab_b, dcst),
        name=name, grid=(nb,), sem=("arbitrary",), exchange=exchange,
        in_specs=[col(0, 0), col(0, 1), col(0, 2), abs_(0), col(1, 0), col(1, 1), col(1, 2), abs_(1), small,
                  sall_spec(0), sall_spec(1), do_spec(0), do_spec(1), full4, hbm, hbm, hbm, hbm, small],
        out_specs=acc_specs + [full4],
        out_shape=[jax.ShapeDtypeStruct(dqkv_f.shape, F32), jax.ShapeDtypeStruct(dqkv_b.shape, F32),
                   jax.ShapeDtypeStruct(dab_f.shape, F32), jax.ShapeDtypeStruct(dab_b.shape, F32),
                   jax.ShapeDtypeStruct((8, 128), F32), jax.ShapeDtypeStruct((2, NH, HD, HD), F32)],
        aliases={14: 0, 15: 1, 16: 2, 17: 3, 18: 4},
        scratch_shapes=[pltpu.VMEM((2, NH, HD, HD), F32)])


def _pool(xin, *, row0, transpose, name):
    n_tok = xin.shape[0] - row0
    rows = n_tok // GW
    pad = 8 * GW
    tt = 512
    gsh = GW.bit_length() - 1

    def body(x_ref, o_ref, ybuf):
        ii = lax.broadcasted_iota(jnp.int32, (128, 128), 0)
        jj = lax.broadcasted_iota(jnp.int32, (128, 128), 1)
        same_row = (ii >> gsh) == (jj >> gsh)
        ci, cj = ii & (GW - 1), jj & (GW - 1)
        tok = lax.broadcasted_iota(jnp.int32, (tt, 1), 0)
        zpad = jnp.zeros((pad, 128), F32)
        for gi, wdw in enumerate(POOL_WINDOWS):
            lo, hi = wdw // 2, wdw - wdw // 2
            if transpose:
                band = same_row & (ci - cj >= -lo) & (ci - cj < hi)
                offs = range(-hi + 1, lo + 1)
            else:
                band = same_row & (cj - ci >= -lo) & (cj - ci < hi)
                offs = range(-lo, hi)
            bandm = band.astype(BF)
            cs = slice(gi * 128, (gi + 1) * 128)
            ybuf[0:pad, :] = zpad
            ybuf[pad + n_tok:, :] = zpad

            def inv_area(t0):
                t = t0 + tok
                r, c = t >> gsh, t & (GW - 1)
                nr = jnp.minimum(r + hi, rows) - jnp.maximum(r - lo, 0)
                nc = jnp.minimum(c + hi, GW) - jnp.maximum(c - lo, 0)
                return 1.0 / (nr * nc).astype(F32)

            def col_pass(b, carry):
                t0 = pl.multiple_of(b * tt, tt)
                xv = x_ref[pl.ds(row0 + t0, tt), cs]
                if transpose:
                    xv = xv * inv_area(t0)
                hi_part = xv.astype(BF)
                lo_part = (xv - hi_part.astype(F32)).astype(BF)
                for s in range(tt // 128):
                    sl = slice(s * 128, (s + 1) * 128)
                    y = (jnp.dot(bandm, hi_part[sl], preferred_element_type=F32)
                         + jnp.dot(bandm, lo_part[sl], preferred_element_type=F32))
                    ybuf[pl.ds(pad + t0 + s * 128, 128), :] = y
                return carry

            lax.fori_loop(0, n_tok // tt, col_pass, 0)

            def row_pass(b, carry):
                t0 = pl.multiple_of(b * tt, tt)
                acc = ybuf[pl.ds(pad + t0 + offs[0] * GW, tt), :]
                for dr in offs[1:]:
                    acc = acc + ybuf[pl.ds(pad + t0 + dr * GW, tt), :]
                xv = x_ref[pl.ds(row0 + t0, tt), cs]
                if not transpose:
                    acc = acc * inv_area(t0)
                o_ref[pl.ds(t0, tt), cs] = acc - xv
                return carry

            lax.fori_loop(0, n_tok // tt, row_pass, 0)

    return pl.pallas_call(
        body, name=name, out_shape=jax.ShapeDtypeStruct((n_tok, NPOOL), F32),
        in_specs=[pl.BlockSpec(memory_space=pltpu.VMEM)], out_specs=pl.BlockSpec(memory_space=pltpu.VMEM),
        scratch_shapes=[pltpu.VMEM((n_tok + 2 * pad, 128), F32)],
        compiler_params=pltpu.CompilerParams(vmem_limit_bytes=VMEM_LIMIT),
    )(xin)


def _merge_parts(of, ob, pgate, pd, br, gnw, pw_ref, pscale, wg_ref, wp_ref):
    o = of + ob
    ons, ohs, rs = [], [], []
    for h in range(NH):
        oh = o[:, h * HD:(h + 1) * HD]
        r = lax.rsqrt(jnp.mean(oh * oh, axis=-1, keepdims=True) + EPS)
        ohs.append(oh * r)
        rs.append(r)
        ons.append(oh * r * gnw)
    on = jnp.concatenate(ons, axis=1)
    og = on * _silu(pgate)
    y_gdn = _nn(og, wg_ref[...])
    ypre = jnp.concatenate([_nn(pd[:, g * 128:(g + 1) * 128], pw_ref[g]) for g in range(4)], axis=1)
    yp = ypre * pscale
    y_pool = _nn(yp, wp_ref[...])
    g_pool = jax.nn.sigmoid(br[:, :D])
    g_gdn = jax.nn.sigmoid(br[:, D:])
    return dict(on=on, ohs=ohs, rs=rs, og=og, y_gdn=y_gdn, ypre=ypre, yp=yp, y_pool=y_pool, g_pool=g_pool, g_gdn=g_gdn)


def _merge_fwd(x1, of, ob, pgate, pd, br, modv, gnw, pool_w, pscale, w_gdn, w_pool, w_mo, *, name):
    n_tok = of.shape[0]

    def body(x_ref, of_ref, ob_ref, pg_ref, pd_ref, br_ref, mod_ref, gnw_ref, pw_ref, ps_ref, wg_ref, wp_ref, wmo_ref,
             x2_ref, og_ref, yp_ref, m_ref, mix_ref):
        t = _merge_parts(of_ref[...], ob_ref[...], pg_ref[...], pd_ref[...], br_ref[...], gnw_ref[...], pw_ref, ps_ref[...],
                         wg_ref, wp_ref)
        m = t["g_pool"] * t["y_pool"] + t["g_gdn"] * t["y_gdn"]
        mix = _nn(m, wmo_ref[...])
        og_ref[...] = t["og"].astype(BF)
        yp_ref[...] = t["yp"].astype(BF)
        m_ref[...] = m.astype(BF)
        mix_ref[...] = mix.astype(BF)
        x2_ref[...] = x_ref[...] + mod_ref[0, 5:6, :] * mix

    tile = lambda w: pl.BlockSpec((TM, w), lambda i: (i, 0))
    ctile = lambda w: pl.BlockSpec((TM, w), lambda i: (i + 1, 0))
    return pl.pallas_call(
        body, name=name, grid=(n_tok // TM,),
        in_specs=[ctile(D), tile(D), tile(D), ctile(D), tile(NPOOL), ctile(2 * D),
                  pl.BlockSpec((1, 16, D), lambda i: (1, 0, 0)), _const_spec((1, HD)), _const_spec((4, 128, 128)),
                  _const_spec((1, NPOOL)), _const_spec((D, D)), _const_spec((NPOOL, D)), _const_spec((D, D))],
        out_specs=[tile(D), tile(D), tile(NPOOL), tile(D), tile(D)],
        out_shape=[jax.ShapeDtypeStruct((n_tok, D), F32), jax.ShapeDtypeStruct((n_tok, D), BF),
                   jax.ShapeDtypeStruct((n_tok, NPOOL), BF), jax.ShapeDtypeStruct((n_tok, D), BF),
                   jax.ShapeDtypeStruct((n_tok, D), BF)],
        compiler_params=_cparams("parallel"),
    )(x1, of, ob, pgate, pd, br, modv, gnw, pool_w, pscale, w_gdn, w_pool, w_mo)


def _merge_bwd(dx2, mix, of, ob, pgate, pd, br, modv, gnw, pool_w, pscale, w_gdn, w_pool, w_mo, *, name):
    n_tok = of.shape[0]

    def body(dx2_ref, mix_ref, of_ref, ob_ref, pg_ref, pd_ref, br_ref, mod_ref, gnw_ref, pw_ref, ps_ref, wg_ref, wp_ref, wmo_ref,
             do_ref, dgate_ref, dpd_ref, dbr_ref, dmix_ref, dyg_ref, dyp_ref, acc_ref, dpw_ref):
        i = pl.program_id(0)
        pgate, pdv, gnw = pg_ref[...], pd_ref[...], gnw_ref[...]
        t = _merge_parts(of_ref[...], ob_ref[...], pgate, pdv, br_ref[...], gnw, pw_ref, ps_ref[...], wg_ref, wp_ref)
        dx2v = dx2_ref[...]
        dmix = mod_ref[0, 5:6, :] * dx2v
        dmixb = dmix.astype(BF)
        dmix_ref[...] = dmixb
        dm = _nt(dmixb, wmo_ref[...])
        gp, gg = t["g_pool"], t["g_gdn"]
        dbr_ref[:, :D] = dm * t["y_pool"] * gp * (1.0 - gp)
        dbr_ref[:, D:] = dm * t["y_gdn"] * gg * (1.0 - gg)
        dyp = (dm * gp).astype(BF)
        dyg = (dm * gg).astype(BF)
        dyp_ref[...] = dyp
        dyg_ref[...] = dyg
        dyp_in = _nt(dyp, wp_ref[...])
        dypre = dyp_in * ps_ref[...]
        for g in range(4):
            gs = slice(g * 128, (g + 1) * 128)
            dpd_ref[:, gs] = _nt(dypre[:, gs], pw_ref[g])
        dog = _nt(dyg, wg_ref[...])
        dgate_ref[...] = dog * t["on"] * _dsilu(pgate)
        don = dog * _silu(pgate)
        dgnw = jnp.zeros((1, HD), F32)
        for h in range(NH):
            hs = slice(h * HD, (h + 1) * HD)
            donh, oh, r = don[:, hs], t["ohs"][h], t["rs"][h]
            dgnw = dgnw + jnp.sum(donh * oh, axis=0, keepdims=True)
            doh = donh * gnw
            do_ref[:, hs] = r * (doh - oh * jnp.mean(doh * oh, axis=-1, keepdims=True))

        @pl.when(i == 0)
        def _():
            acc_ref[...] = jnp.zeros_like(acc_ref)
            dpw_ref[...] = jnp.zeros_like(dpw_ref)

        acc_ref[0:1, :] += jnp.sum(dx2v * mix_ref[...].astype(F32), axis=0, keepdims=True)
        acc_ref[1:2, 0:HD] += dgnw
        acc_ref[2:3, 0:NPOOL] += jnp.sum(dyp_in * t["ypre"], axis=0, keepdims=True)
        for g in range(4):
            gs = slice(g * 128, (g + 1) * 128)
            dpw_ref[g] += _tn(pdv[:, gs], dypre[:, gs])

    tile = lambda w: pl.BlockSpec((TM, w), lambda i: (i, 0))
    ctile = lambda w: pl.BlockSpec((TM, w), lambda i: (i + 1, 0))
    return pl.pallas_call(
        body, name=name, grid=(n_tok // TM,),
        in_specs=[tile(D), tile(D), tile(D), tile(D), ctile(D), tile(NPOOL), ctile(2 * D),
                  pl.BlockSpec((1, 16, D), lambda i: (1, 0, 0)), _const_spec((1, HD)), _const_spec((4, 128, 128)),
                  _const_spec((1, NPOOL)), _const_spec((D, D)), _const_spec((NPOOL, D)), _const_spec((D, D))],
        out_specs=[tile(D), tile(D), tile(NPOOL), tile(2 * D), tile(D), tile(D), tile(D),
                   pl.BlockSpec((8, D), lambda i: (0, 0)), pl.BlockSpec((4, 128, 128), lambda i: (0, 0, 0))],
        out_shape=[jax.ShapeDtypeStruct((n_tok, D), F32), jax.ShapeDtypeStruct((n_tok, D), F32),
                   jax.ShapeDtypeStruct((n_tok, NPOOL), F32), jax.ShapeDtypeStruct((n_tok, 2 * D), F32),
                   jax.ShapeDtypeStruct((n_tok, D), BF), jax.ShapeDtypeStruct((n_tok, D), BF), jax.ShapeDtypeStruct((n_tok, D), BF),
                   jax.ShapeDtypeStruct((8, D), F32), jax.ShapeDtypeStruct((4, 128, 128), F32)],
        compiler_params=_cparams("arbitrary"),
    )(dx2, mix, of, ob, pgate, pd, br, modv, gnw, pool_w, pscale, w_gdn, w_pool, w_mo)


def _final(x3, target, fnw, *, name):
    n_tok = x3.shape[0]

    def body(x_ref, t_ref, w_ref, dx_ref, acc_ref):
        xv, w = x_ref[...], w_ref[...]
        r = lax.rsqrt(jnp.mean(xv * xv, axis=-1, keepdims=True) + EPS)
        xh = xv * r
        err = xh * w - t_ref[...]
        dy = err * (1.0 / D)
        dxh = dy * w
        dx_ref[...] = r * (dxh - xh * jnp.mean(dxh * xh, axis=-1, keepdims=True))

        @pl.when(pl.program_id(0) == 0)
        def _():
            acc_ref[...] = jnp.zeros_like(acc_ref)

        acc_ref[0:1, :] += jnp.sum(dy * xh, axis=0, keepdims=True)
        acc_ref[1:2, :] += jnp.sum(err * err, axis=0, keepdims=True) * (0.5 / D)

    tile = pl.BlockSpec((TM, D), lambda i: (i, 0))
    return pl.pallas_call(
        body, name=name, grid=(n_tok // TM,),
        in_specs=[tile, tile, _const_spec((1, D))],
        out_specs=[tile, pl.BlockSpec((8, D), lambda i: (0, 0))],
        out_shape=[jax.ShapeDtypeStruct((n_tok, D), F32), jax.ShapeDtypeStruct((8, D), F32)],
        compiler_params=_cparams("arbitrary"),
    )(x3, target, fnw)


def _split(results, n):
    return (*results[:n], list(results[n:]))


def _mixer_weights(w_mix_in, w_gdn, w_pool, w_mo, conv):
    return dict(w_mix=_regroup_mix(_from_chip_major_cols(w_mix_in)), w_gdn=w_gdn.reshape(D, D), w_pool=_from_chip_major_cols(w_pool),
                w_mo=w_mo.reshape(D, D), conv=jnp.pad(_from_chip_major_cols(conv), ((0, 3), (0, 0))))


def _local_step(xc, target, modv, p, late=None):
    n_all = xc.shape[0]
    t_lat = n_all - TM
    nbc, nbx = TM // TS, t_lat // TS
    mod_lat = modv[1:2]
    gather = (lambda arrs: _ChipExchange(arrs, False)) if late else (lambda arrs: None)
    scatter = (lambda arrs: _ChipExchange(arrs, True)) if late else (lambda arrs: None)

    x1, h1, gu1, f1, *got = _ffn_fwd(xc, modv, p["norm1"], p["w1_in"], p["w1_out"], mrow=0, name="ffn1_fwd",
                                     exchange=gather(late and late[0]))
    if late:
        p = {**p, **_mixer_weights(*got)}
    u, p_qkv, p_gate, p_pool, p_br, p_ab = _mix_in_fwd(x1, modv, p["norm2"], p["w_mix"], name="mix_in_fwd")
    qkv, pre_qkv = _prep_fwd(p_qkv, p["conv"], name="prep_fwd")
    s_zero = jnp.zeros((2, NH, HD, HD), F32)
    _, _, sall_cf, sall_cb, s_ctx = _scan_fwd(qkv, p_ab, p["cst"], s_zero, row_blk0=0, nb=nbc, name="scan_ctx")
    o_f, o_b, sall_f, sall_b, _, *got = _scan_fwd(qkv, p_ab, p["cst"], s_ctx, row_blk0=nbc, nb=nbx, name="scan_lat",
                                                  exchange=gather(late and late[1]))
    if late:
        p = {**p, "w2_in": got[0], "w2_out": got[1].reshape(FF, D)}
    pd = _pool(p_pool, row0=TM, transpose=False, name="pool_fwd")
    merge_w = (modv, p["gnw"], p["pool_w"], p["pscale"], p["w_gdn"], p["w_pool"], p["w_mo"])
    x2, og, yp, m, mix = _merge_fwd(x1, o_f, o_b, p_gate, pd, p_br, *merge_w, name="merge_fwd")
    x3, h3, gu3, f3 = _ffn_fwd(x2, mod_lat, p["norm3"], p["w2_in"], p["w2_out"], mrow=6, name="ffn2_fwd")
    dx3, acc_fin = _final(x3, target, p["fnorm"], name="final")

    dx2, a3, df3, dgu3, acc3 = _ffn_bwd(dx3, x2, gu3, f3, mod_lat, p["norm3"], p["w2_in"], p["w2_out"], mrow=6, dx_skip=0,
                                        name="ffn2_bwd")
    g = {}
    tkl = _k_tile(t_lat)
    g["w2_out"] = _matmul_tn(a3, df3, tmm=FF // 2, tn=D, tk=tkl, name="ffn2_wout_grad").reshape(NCHIP, FF // NCHIP, D)
    g["w2_in"] = _matmul_tn(h3, dgu3, tmm=D, tn=2 * FF // NCHIP, tk=tkl, nsplit=NCHIP, name="ffn2_win_grad")
    do, dgate, dpd, dbr, dmix, dyg, dyp, acc_m, dpw = _merge_bwd(dx2, mix, o_f, o_b, p_gate, pd, p_br, *merge_w, name="merge_bwd")
    g["w_mo"] = _matmul_tn(m, dmix, tmm=D, tn=D, tk=tkl, name="wmo_grad").reshape(NCHIP, D // NCHIP, D)
    g["w_gdn"] = _matmul_tn(og, dyg, tmm=D, tn=D, tk=tkl, name="wgdn_grad").reshape(NCHIP, D // NCHIP, D)
    g["w_pool"] = _matmul_tn(yp, dyp, tmm=NPOOL, tn=D // NCHIP, tk=tkl, nsplit=NCHIP, name="wpool_grad")
    dpool_in = _pool(dpd, row0=0, transpose=True, name="pool_bwd")
    acc = (lax.empty((n_all, NQKV), F32), lax.empty((n_all, NQKV), F32), lax.empty((n_all, 128), F32),
           lax.empty((n_all, 128), F32), jnp.zeros((8, 128), F32))
    behind_scan = ("w2_in", "w2_out", "w_gdn", "w_pool", "w_mo")
    *acc, ds_ctx, landed = _split(_scan_bwd(qkv, p_ab, p["cst"], sall_f, sall_b, do, s_zero, *acc, row_blk0=nbc, nb=nbx, has_do=True,
                                            name="scan_lat_bwd", exchange=scatter([g[k] for k in behind_scan])), 6)
    landed = dict(zip(behind_scan, landed))
    dqkv_f, dqkv_b, dab_f, dab_b, dcst, _ = _scan_bwd(qkv, p_ab, p["cst"], sall_cf, sall_cb, jnp.zeros((8, 128), F32), ds_ctx, *acc,
                                                      row_blk0=0, nb=nbc, has_do=False, name="scan_ctx_bwd")
    dpqkv, dconv = _prep_bwd(p_qkv, pre_qkv, dqkv_f, dqkv_b, p["conv"], name="prep_bwd")
    dx1, dp, acc_mix = _mix_in_bwd(dx2, x1, dpqkv, dgate, dpool_in, dbr, dab_f, dab_b, modv, p["norm2"], p["w_mix"],
                                   name="mix_in_bwd")
    tka = _k_tile(n_all)
    g["w_mix"] = _chip_major_cols(_ungroup_mix(_matmul_tn(u, dp, tmm=256, tn=NMIXP, tk=_k_tile(n_all, 1024), name="wmix_grad")))
    dx_lat, a1, df1, dgu1, acc1, got = _split(_ffn_bwd(dx1, xc, gu1, f1, modv, p["norm1"], p["w1_in"], p["w1_out"], mrow=0, dx_skip=1,
                                                   name="ffn1_bwd", exchange=scatter([g["w_mix"]])), 5)
    landed.update(zip(("w_mix",), got))
    g["w1_out"] = _matmul_tn(a1, df1, tmm=FF // 2, tn=D, tk=tka, name="ffn1_wout_grad").reshape(NCHIP, FF // NCHIP, D)
    g["w1_in"] = _matmul_tn(h1, dgu1, tmm=D, tn=2 * FF // NCHIP, tk=tka, nsplit=NCHIP, name="ffn1_win_grad")

    small = dict(norm1=acc1[0, 3] + acc1[1, 3], norm2=acc_mix[0, 3] + acc_mix[1, 3], norm3=acc3[0, 3], fnorm=acc_fin[0],
                 gnw=acc_m[1, :HD], pscale=acc_m[2, :NPOOL], pool_w=dpw, conv=dconv[:5],
                 a_log=dcst[0, 2 * NH:4 * NH], dt_bias=dcst[1, 2 * NH:4 * NH])
    zero = jnp.zeros((D,), F32)
    dmod = jnp.stack([
        jnp.stack([acc1[0, 0], acc1[0, 1], acc1[0, 2], acc_mix[0, 0], acc_mix[0, 1], zero, zero, zero, zero]),
        jnp.stack([acc1[1, 0], acc1[1, 1], acc1[1, 2], acc_mix[1, 0], acc_mix[1, 1], acc_m[0], acc3[0, 0], acc3[0, 1], acc3[0, 2]]),
    ])
    return jnp.sum(acc_fin[1]), dx_lat, g, landed, small, dmod


_HI = lax.Precision.HIGHEST


def _ada_fwd(c_all, w_sh, b_sh, *, name):
    def body(c_ref, w_ref, b_ref, o_ref):
        o_ref[...] = jnp.dot(_silu(c_ref[...]), w_ref[...], precision=_HI, preferred_element_type=F32) + b_ref[...]

    return pl.pallas_call(body, name=name, out_shape=jax.ShapeDtypeStruct((16, w_sh.shape[1]), F32),
                          compiler_params=pltpu.CompilerParams(vmem_limit_bytes=VMEM_LIMIT))(c_all, w_sh, b_sh)


def _ada_bwd(c_all, dm, w_sh, *, name):
    def body(c_ref, dm_ref, w_ref, dw_ref, dc_ref):
        sc = _silu(c_ref[...])
        dw_ref[...] = lax.dot_general(sc, dm_ref[...], (((0,), (0,)), ((), ())), precision=_HI, preferred_element_type=F32)
        part = lax.dot_general(dm_ref[8:9, :], w_ref[...], (((1,), (1,)), ((), ())), precision=_HI, preferred_element_type=F32)
        dc_ref[...] = jnp.broadcast_to(part, dc_ref.shape)

    return pl.pallas_call(body, name=name,
                          out_shape=[jax.ShapeDtypeStruct(w_sh.shape, F32), jax.ShapeDtypeStruct((8, D), F32)],
                          compiler_params=pltpu.CompilerParams(vmem_limit_bytes=VMEM_LIMIT))(c_all, dm, w_sh)


def _cctx_grad(parts, c_ctx, *, name):
    def body(p_ref, c_ref, o_ref):
        tot = (p_ref[0, 0:1, :] + p_ref[2, 0:1, :]) + (p_ref[4, 0:1, :] + p_ref[6, 0:1, :])
        o_ref[...] = tot * _dsilu(c_ref[...])

    return pl.pallas_call(body, name=name, out_shape=jax.ShapeDtypeStruct((1, D), F32))(parts, c_ctx)


_MESH = pl.DeviceIdType.MESH
_ANY = pl.BlockSpec(memory_space=pl.ANY)


def _flip(v, bit):
    return (1 - v) if bit else v


def _all_gather8(x, *, name):
    def body(x_ref, out_ref, send_sems, recv_sems, local_sem):
        mx, my, mc = lax.axis_index("x"), lax.axis_index("y"), lax.axis_index("c")
        me = 4 * mx + 2 * my + mc
        mine = pltpu.make_async_copy(x_ref, out_ref.at[me], local_sem)
        mine.start()
        sends, recvs = [], []
        for k in range(1, 8):
            px, py, pc = _flip(mx, k & 4), _flip(my, k & 2), _flip(mc, k & 1)
            sends.append(pltpu.make_async_remote_copy(src_ref=x_ref, dst_ref=out_ref.at[me], send_sem=send_sems.at[k - 1],
                                                      recv_sem=recv_sems.at[k - 1], device_id=(px, py, pc), device_id_type=_MESH))
            recvs.append(pltpu.make_async_remote_copy(src_ref=x_ref, dst_ref=out_ref.at[4 * px + 2 * py + pc],
                                                      send_sem=send_sems.at[k - 1], recv_sem=recv_sems.at[k - 1],
                                                      device_id=(px, py, pc), device_id_type=_MESH))
        for cp in sends:
            cp.start()
        for cp in recvs:
            cp.wait_recv()
        for cp in sends:
            cp.wait_send()
        mine.wait()

    vm = pl.BlockSpec(memory_space=pltpu.VMEM)
    return pl.pallas_call(
        body, name=name, out_shape=jax.ShapeDtypeStruct((8,) + x.shape, x.dtype), in_specs=[vm], out_specs=vm,
        scratch_shapes=[pltpu.SemaphoreType.DMA((7,)), pltpu.SemaphoreType.DMA((7,)), pltpu.SemaphoreType.DMA],
        compiler_params=pltpu.CompilerParams(vmem_limit_bytes=VMEM_LIMIT),
    )(x)


class _ChipExchange:
    def __init__(self, arrs, scatter):
        self.arrs, self.scatter, self.n = list(arrs), scatter, len(arrs)
        self.out_shape = [jax.ShapeDtypeStruct(a.shape if scatter else (NCHIP,) + a.shape, a.dtype) for a in self.arrs]
        links = self.n * (NCHIP - 1)
        self.scratch = [pltpu.SemaphoreType.DMA((links,)), pltpu.SemaphoreType.DMA((links,)), pltpu.SemaphoreType.DMA((self.n,))]

    def copies(self, ins, outs, send_sems, recv_sems, local_sems):
        mx, my, mc = lax.axis_index("x"), lax.axis_index("y"), lax.axis_index("c")
        me = 2 * mx + my
        local, sends, recvs = [], [], []
        for j in range(self.n):
            src_own = ins[j].at[me] if self.scatter else ins[j]
            local.append(pltpu.make_async_copy(src_own, outs[j].at[me], local_sems.at[j]))
            for k in range(1, NCHIP):
                px, py = _flip(mx, k & 2), _flip(my, k & 1)
                peer = 2 * px + py
                sem = j * (NCHIP - 1) + k - 1
                src = ins[j].at[peer] if self.scatter else ins[j]
                sends.append(pltpu.make_async_remote_copy(src_ref=src, dst_ref=outs[j].at[me], send_sem=send_sems.at[sem],
                                                          recv_sem=recv_sems.at[sem], device_id=(px, py, mc), device_id_type=_MESH))
                recvs.append(pltpu.make_async_remote_copy(src_ref=src, dst_ref=outs[j].at[peer], send_sem=send_sems.at[sem],
                                                          recv_sem=recv_sems.at[sem], device_id=(px, py, mc), device_id_type=_MESH))
        return local, sends, recvs

    @staticmethod
    def start(local, sends, recvs):
        for cp in local + sends:
            cp.start()

    @staticmethod
    def finish(local, sends, recvs):
        for cp in recvs:
            cp.wait_recv()
        for cp in sends:
            cp.wait_send()
        for cp in local:
            cp.wait()


def _chip_exchange(arrs, *, scatter, name):
    ex = _ChipExchange(arrs, scatter)

    def body(*refs):
        cps = ex.copies(refs[:ex.n], refs[ex.n:2 * ex.n], *refs[2 * ex.n:])
        ex.start(*cps)
        ex.finish(*cps)

    return pl.pallas_call(body, name=name, out_shape=ex.out_shape, in_specs=[_ANY] * ex.n, out_specs=[_ANY] * ex.n,
                          scratch_shapes=ex.scratch)(*arrs)


_HBM = pl.BlockSpec(memory_space=pltpu.HBM)
_SEM = pl.BlockSpec(memory_space=pltpu.SEMAPHORE)
_DATAFLOW = pltpu.SideEffectType.DATAFLOW_SIDE_EFFECTING


def _scatter_copies(ins, lands, send_sems, recv_sems):
    mx, my, mc = lax.axis_index("x"), lax.axis_index("y"), lax.axis_index("c")
    me = 2 * mx + my
    sends, recvs = [], []
    for j in range(len(ins)):
        for k in range(1, NCHIP):
            px, py = _flip(mx, k & 2), _flip(my, k & 1)
            peer = 2 * px + py
            sem = j * (NCHIP - 1) + k - 1
            mk = lambda slot: pltpu.make_async_remote_copy(src_ref=ins[j].at[peer], dst_ref=lands[j].at[slot], send_sem=send_sems.at[sem],
                                                           recv_sem=recv_sems.at[sem], device_id=(px, py, mc), device_id_type=_MESH)
            sends.append(mk(me))
            recvs.append(mk(peer))
    return sends, recvs


def _scatter_start(arrs, *, name):
    n = len(arrs)
    links = n * (NCHIP - 1)

    def body(*refs):
        ins, lands = refs[:n], refs[n:2 * n]
        send_sems, recv_sems = refs[2 * n], refs[2 * n + 1]
        token = refs[-1]
        for cp in _scatter_copies(ins, lands, send_sems, recv_sems)[0]:
            cp.start()
        token[...] = jnp.zeros_like(token)

    hbm = lambda a: pltpu.HBM(a.shape, a.dtype)
    res = pl.pallas_call(
        body, name=name,
        out_shape=(pltpu.SemaphoreType.DMA((links,)), pltpu.SemaphoreType.DMA((links,)), *[hbm(a) for a in arrs], *[hbm(a) for a in arrs],
                   jax.ShapeDtypeStruct((8, 128), F32)),
        in_specs=[_HBM] * (2 * n), out_specs=(_SEM, _SEM, *[_HBM] * (2 * n), pl.BlockSpec(memory_space=pltpu.VMEM)),
        input_output_aliases={j: 2 + j for j in range(2 * n)},
        compiler_params=pltpu.CompilerParams(has_side_effects=_DATAFLOW),
    )(*[pltpu.with_memory_space_constraint(a, pltpu.HBM) for a in arrs],
      *[pltpu.with_memory_space_constraint(lax.empty(a.shape, a.dtype), pltpu.HBM) for a in arrs])
    return res[0], res[1], list(res[2:2 + n]), list(res[2 + n:2 + 2 * n]), res[-1]


def _scatter_wait(send_sems, recv_sems, arrs, lands, after, *, name):
    n = len(arrs)

    def body(*refs):
        ins, lands_in = refs[:n], refs[n:2 * n]
        sends, recvs = _scatter_copies(ins, lands_in, refs[2 * n], refs[2 * n + 1])
        for cp in sends:
            cp.wait_send()
        for cp in recvs:
            cp.wait_recv()

    hbm = lambda a: pltpu.HBM(a.shape, a.dtype)
    res = pl.pallas_call(
        body, name=name, out_shape=(*[hbm(a) for a in arrs], *[hbm(a) for a in lands]),
        in_specs=[_HBM] * (2 * n) + [_SEM, _SEM] + [_ANY] * len(after), out_specs=[_HBM] * (2 * n),
        input_output_aliases={j: j for j in range(2 * n)},
        compiler_params=pltpu.CompilerParams(has_side_effects=_DATAFLOW),
    )(*arrs, *lands, send_sems, recv_sems, *after)
    return list(res[:n]), list(res[n:])


def _pallas(body, operands, *, name, grid, in_specs, out_specs, out_shape, sem, scratch_shapes=(), aliases=None, exchange=None):
    if exchange is None:
        return pl.pallas_call(body, name=name, grid=grid, in_specs=in_specs, out_specs=out_specs, out_shape=out_shape,
                              scratch_shapes=list(scratch_shapes), input_output_aliases=aliases or {},
                              compiler_params=_cparams(*sem))(*operands)
    ex = exchange
    (steps,) = grid
    n_in, n_out, n_scr, k = len(in_specs), len(out_specs), len(scratch_shapes), ex.n

    def hosted(*refs):
        ins, refs = refs[:n_in], refs[n_in:]
        ex_in, refs = refs[:k], refs[k:]
        outs, refs = refs[:n_out], refs[n_out:]
        ex_out, refs = refs[:k], refs[k:]
        scr, ex_sems = refs[:n_scr], refs[n_scr:]
        cps = ex.copies(ex_in, ex_out, *ex_sems)
        pl.when(pl.program_id(0) == 0)(lambda: ex.start(*cps))
        body(*ins, *outs, *scr)
        pl.when(pl.program_id(0) == steps - 1)(lambda: ex.finish(*cps))

    return pl.pallas_call(
        hosted, name=name, grid=grid, in_specs=list(in_specs) + [_ANY] * k, out_specs=list(out_specs) + [_ANY] * k,
        out_shape=list(out_shape) + ex.out_shape, scratch_shapes=list(scratch_shapes) + ex.scratch,
        input_output_aliases=aliases or {}, compiler_params=_cparams("arbitrary"),
    )(*operands, *ex.arrs)


def _core_swap(arrs, *, name):
    n = len(arrs)

    def body(*refs):
        ins, outs = refs[:n], refs[n:2 * n]
        send_sems, recv_sems = refs[2 * n:]
        sib = (lax.axis_index("x"), lax.axis_index("y"), 1 - lax.axis_index("c"))
        cps = [pltpu.make_async_remote_copy(src_ref=ins[j], dst_ref=outs[j], send_sem=send_sems.at[j], recv_sem=recv_sems.at[j],
                                            device_id=sib, device_id_type=_MESH) for j in range(n)]
        for cp in cps:
            cp.start()
        for cp in cps:
            cp.wait_recv()
        for cp in cps:
            cp.wait_send()

    return pl.pallas_call(
        body, name=name, out_shape=[jax.ShapeDtypeStruct(a.shape, a.dtype) for a in arrs],
        in_specs=[_ANY] * n, out_specs=[_ANY] * n,
        scratch_shapes=[pltpu.SemaphoreType.DMA((n,)), pltpu.SemaphoreType.DMA((n,))],
    )(*arrs)


def _row_tile(rows, cols, budget=1 << 18):
    best = None
    for t in range(8, rows + 1, 8):
        if rows % t == 0 and t * cols <= budget:
            best = t
    return best or rows


def _sum_slots(x, *, name):
    ns, r, c = x.shape
    tr = _row_tile(r, c * ns)

    def body(x_ref, o_ref):
        acc = x_ref[0].astype(F32)
        for s in range(1, ns):
            acc = acc + x_ref[s].astype(F32)
        o_ref[...] = acc

    return pl.pallas_call(
        body, name=name, grid=(r // tr,), out_shape=jax.ShapeDtypeStruct((r, c), F32),
        in_specs=[pl.BlockSpec((ns, tr, c), lambda i: (0, i, 0))], out_specs=pl.BlockSpec((tr, c), lambda i: (i, 0)),
        compiler_params=_cparams("parallel"),
    )(x)


def _adamw(w, ga, gb, m, v, *, name):
    r, c = w.shape
    tr = _row_tile(r, c, budget=1 << 17)
    two = gb is not None

    def body(*refs):
        w_ref, ga_ref = refs[0], refs[1]
        m_ref, v_ref = refs[2 + two], refs[3 + two]
        g_ref, d_ref, mo_ref, vo_ref = refs[4 + two:]
        g = ga_ref[...] + refs[2][...] if two else ga_ref[...]
        mn = ADAM_B1 * m_ref[...] + (1.0 - ADAM_B1) * g
        vn = ADAM_B2 * v_ref[...] + (1.0 - ADAM_B2) * (g * g)
        m_hat = mn / (1.0 - ADAM_B1 ** ADAM_STEP)
        v_hat = vn / (1.0 - ADAM_B2 ** ADAM_STEP)
        g_ref[...] = g
        d_ref[...] = -ADAM_LR * (m_hat / (jnp.sqrt(v_hat) + ADAM_EPS) + ADAM_WD * w_ref[...])
        mo_ref[...] = mn
        vo_ref[...] = vn

    spec = pl.BlockSpec((tr, c), lambda i: (i, 0))
    ins = [w, ga] + ([gb] if two else []) + [m, v]
    return pl.pallas_call(
        body, name=name, grid=(r // tr,), out_shape=[jax.ShapeDtypeStruct((r, c), F32)] * 4,
        in_specs=[spec] * len(ins), out_specs=[spec] * 4, compiler_params=_cparams("parallel"),
    )(*ins)


_MIX_AB0, _MIX_AB1 = NQKV, NQKV + 4 * NH


def _regroup_mix(w):
    pad = jnp.zeros((w.shape[0], NMIXP - NMIX), w.dtype)
    return jnp.concatenate([w[:, :_MIX_AB0], w[:, _MIX_AB1:], w[:, _MIX_AB0:_MIX_AB1], pad], axis=1)


def _ungroup_mix(w):
    n_ab = _MIX_AB1 - _MIX_AB0
    return jnp.concatenate([w[:, :_MIX_AB0], w[:, NMIX - n_ab:NMIX], w[:, _MIX_AB0:NMIX - n_ab]], axis=1)


def _chip_major_cols(w):
    r, c = w.shape
    return w.reshape(r, NCHIP, c // NCHIP).transpose(1, 0, 2)


def _from_chip_major_cols(w):
    return w.transpose(1, 0, 2).reshape(w.shape[1], -1)


_SMALL = (("c_ctx", D), ("b_ada", 9 * D), ("norm1_w", D), ("norm2_w", D), ("norm3_w", D), ("final_norm_w", D),
          ("a_log", 2 * NH), ("dt_bias", 2 * NH), ("gdn_norm_w", HD), ("pool_w", 4 * 128 * 128), ("pool_scale", NPOOL),
          ("conv_w", 5 * NQKV // NCHIP))


def _pack(vals, lanes=128, row_mult=8):
    flat = jnp.concatenate([jnp.ravel(v) for v in vals])
    n = flat.shape[0]
    rows = -(-n // (lanes * row_mult)) * row_mult
    return jnp.pad(flat, (0, rows * lanes - n)).reshape(rows, lanes)


def _unpack(packed, sizes):
    flat = packed.reshape(-1)
    out, o = [], 0
    for n in sizes:
        out.append(flat[o:o + n])
        o += n
    return out


def kernel(x, c, ctx, c_ctx, w_ada, b_ada, norm1_w, ffn1_w_in, ffn1_w_out, norm2_w, w_mix_in, conv_w, a_log, dt_bias, gdn_norm_w, w_gdn_proj, pool_w, pool_scale, w_pool_proj, w_mix_out, norm3_w, ffn2_w_in, ffn2_w_out, final_norm_w, loss_target, m_c_ctx, m_w_ada, m_b_ada, m_norm1_w, m_ffn1_w_in, m_ffn1_w_out, m_norm2_w, m_w_mix_in, m_conv_w, m_a_log, m_dt_bias, m_gdn_norm_w, m_w_gdn_proj, m_pool_w, m_pool_scale, m_w_pool_proj, m_w_mix_out, m_norm3_w, m_ffn2_w_in, m_ffn2_w_out, m_final_norm_w, v_c_ctx, v_w_ada, v_b_ada, v_norm1_w, v_ffn1_w_in, v_ffn1_w_out, v_norm2_w, v_w_mix_in, v_conv_w, v_a_log, v_dt_bias, v_gdn_norm_w, v_w_gdn_proj, v_pool_w, v_pool_scale, v_w_pool_proj, v_w_mix_out, v_norm3_w, v_ffn2_w_in, v_ffn2_w_out, v_final_norm_w):
    names = ("c_ctx", "w_ada", "b_ada", "norm1_w", "ffn1_w_in", "ffn1_w_out", "norm2_w", "w_mix_in", "conv_w", "a_log", "dt_bias",
             "gdn_norm_w", "w_gdn_proj", "pool_w", "pool_scale", "w_pool_proj", "w_mix_out", "norm3_w", "ffn2_w_in", "ffn2_w_out",
             "final_norm_w")
    w = dict(zip(names, (c_ctx, w_ada, b_ada, norm1_w, ffn1_w_in, ffn1_w_out, norm2_w, w_mix_in, conv_w, a_log, dt_bias, gdn_norm_w,
                         w_gdn_proj, pool_w, pool_scale, w_pool_proj, w_mix_out, norm3_w, ffn2_w_in, ffn2_w_out, final_norm_w)))
    mom = dict(zip(names, (m_c_ctx, m_w_ada, m_b_ada, m_norm1_w, m_ffn1_w_in, m_ffn1_w_out, m_norm2_w, m_w_mix_in, m_conv_w, m_a_log,
                           m_dt_bias, m_gdn_norm_w, m_w_gdn_proj, m_pool_w, m_pool_scale, m_w_pool_proj, m_w_mix_out, m_norm3_w,
                           m_ffn2_w_in, m_ffn2_w_out, m_final_norm_w)))
    var = dict(zip(names, (v_c_ctx, v_w_ada, v_b_ada, v_norm1_w, v_ffn1_w_in, v_ffn1_w_out, v_norm2_w, v_w_mix_in, v_conv_w, v_a_log,
                           v_dt_bias, v_gdn_norm_w, v_w_gdn_proj, v_pool_w, v_pool_scale, v_w_pool_proj, v_w_mix_out, v_norm3_w,
                           v_ffn2_w_in, v_ffn2_w_out, v_final_norm_w)))
    mx, my, mc = lax.axis_index("x"), lax.axis_index("y"), lax.axis_index("c")
    chip = 2 * mx + my
    dev = 2 * chip + mc
    ada_cols = w_ada.shape[2]

    c_rows = _all_gather8(jnp.pad(c, ((0, 7), (0, 0))), name="gather_c")[:, 0, :]
    c_all = jnp.concatenate([c_rows, c_ctx[None], jnp.zeros((7, D), F32)], axis=0)
    b_sh = lax.dynamic_slice(b_ada, (0, chip * ada_cols), (1, ada_cols))
    mod_sh = _ada_fwd(c_all, w_ada[0], b_sh, name="ada_fwd")
    mod_parts = _all_gather8(mod_sh, name="gather_mod")
    mod_all = jnp.concatenate([mod_parts[2 * s] for s in range(NCHIP)], axis=1)
    mod_lat = lax.dynamic_index_in_dim(mod_all, dev, axis=0, keepdims=False).reshape(9, D)
    modv = jnp.zeros((2, 16, D), F32).at[0, :9].set(mod_all[8].reshape(9, D)).at[1, :9].set(mod_lat)

    big = ("ffn1_w_in", "ffn1_w_out", "w_mix_in", "w_gdn_proj", "w_pool_proj", "w_mix_out", "ffn2_w_in", "ffn2_w_out")
    shard = {k: w[k][0].astype(BF) for k in big}
    w1_in, w1_out = _chip_exchange([shard["ffn1_w_in"], shard["ffn1_w_out"]], scatter=False, name="gather_ffn1")
    p = dict(
        norm1=norm1_w, norm2=norm2_w, norm3=norm3_w, fnorm=final_norm_w[None], w1_in=w1_in, w1_out=w1_out.reshape(FF, D),
        cst=jnp.zeros((8, 128), F32).at[0, 2 * NH:4 * NH].set(jnp.exp(a_log).reshape(-1)).at[1, 2 * NH:4 * NH].set(dt_bias.reshape(-1)),
        gnw=gdn_norm_w, pool_w=pool_w[0], pscale=pool_scale)
    late = ([shard["w_mix_in"], shard["w_gdn_proj"], shard["w_pool_proj"], shard["w_mix_out"], conv_w[0]],
            [shard["ffn2_w_in"], shard["ffn2_w_out"]])

    xc = jnp.concatenate([ctx[0], x[0]], axis=0)
    loss_dev, dx_lat, g, landed, small, dmod = _local_step(xc, loss_target[0], modv, p, late)
    loss = lax.psum(loss_dev, ("x", "y", "c"))
    grad_x = dx_lat[None]

    first = ("ffn1_w_in", "ffn1_w_out")
    send_sems, recv_sems, sent, lands, token = _scatter_start([g["w1_in"], g["w1_out"]], name="scatter_ffn1_start")
    order = dict(zip(big, ("w1_in", "w1_out", "w_mix", "w_gdn", "w_pool", "w_mo", "w2_in", "w2_out")))
    rest = [k for k in big if k not in first]
    mine = {k: _sum_slots(landed[order[k]], name=f"sum_{k}") for k in rest}
    theirs = dict(zip(rest, _core_swap([mine[k] for k in rest], name="swap_grad_sums")))

    small_vals = [dmod[1], dmod[0], small["norm1"], small["norm2"], small["norm3"], small["fnorm"], small["a_log"], small["dt_bias"],
                  small["gnw"], small["pool_w"], small["pscale"], small["conv"]]
    small_sizes = [v.size for v in small_vals]
    packed = _all_gather8(_pack(small_vals) + token[0, 0], name="gather_small")
    tot = _unpack(_sum_slots(packed, name="sum_small"), small_sizes)
    dmod_lat_all = packed[:, :9 * D // 128, :].reshape(8, 9 * D)
    dm = jnp.concatenate([dmod_lat_all, tot[1][None], jnp.zeros((7, 9 * D), F32)], axis=0)
    dm_sh = lax.dynamic_slice(dm, (0, chip * ada_cols), (16, ada_cols))
    g_w_ada, cctx_part = _ada_bwd(c_all, dm_sh, w_ada[0], name="ada_bwd")
    g_c_ctx = _cctx_grad(_all_gather8(cctx_part, name="gather_cctx"), c_ctx[None], name="cctx_grad")[0]
    conv_tot = tot[11].reshape(5, NQKV)
    g_small = dict(c_ctx=g_c_ctx, b_ada=tot[0] + tot[1], norm1_w=tot[2], norm2_w=tot[3], norm3_w=tot[4], final_norm_w=tot[5],
                   a_log=tot[6], dt_bias=tot[7], gdn_norm_w=tot[8], pool_w=tot[9], pool_scale=tot[10],
                   conv_w=lax.dynamic_slice(conv_tot, (0, chip * (NQKV // NCHIP)), (5, NQKV // NCHIP)))

    out = {}
    as2d = lambda a: a.reshape(-1, a.shape[-1])

    def update(k):
        res = _adamw(as2d(w[k]), as2d(mine[k]), as2d(theirs[k]), as2d(mom[k]), as2d(var[k]), name=f"adamw_{k}")
        out[k] = [r.reshape(w[k].shape) for r in res]

    for k in rest:
        update(k)
    out["w_ada"] = [r.reshape(w_ada.shape) for r in _adamw(w_ada[0], g_w_ada, None, m_w_ada[0], v_w_ada[0], name="adamw_w_ada")]
    sent, lands = _scatter_wait(send_sems, recv_sems, sent, lands, [out[k][1] for k in rest] + [out["w_ada"][1]],
                                name="scatter_ffn1_wait")
    for k, part, land in zip(first, sent, lands):
        own = lax.dynamic_slice_in_dim(part, chip, 1, axis=0)
        mine[k] = _sum_slots(lax.dynamic_update_slice_in_dim(land, own, chip, axis=0), name=f"sum_{k}")
    theirs.update(zip(first, _core_swap([mine[k] for k in first], name="swap_ffn1_sums")))
    for k in first:
        update(k)
    sm_names = [n for n, _ in _SMALL]
    sm_sizes = [n for _, n in _SMALL]
    res = _adamw(_pack([w[k] for k in sm_names]), _pack([g_small[k] for k in sm_names]), None,
                 _pack([mom[k] for k in sm_names]), _pack([var[k] for k in sm_names]), name="adamw_small")
    res = [_unpack(r, sm_sizes) for r in res]
    for i, k in enumerate(sm_names):
        out[k] = [r[i].reshape(w[k].shape) for r in res]
    return (loss, grad_x, *[out[k][0] for k in names], *[out[k][1] for k in names], *[out[k][2] for k in names],
            *[out[k][3] for k in names])
```

```python
import functools

import jax
import jax.numpy as jnp
from jax import lax
from jax.experimental import pallas as pl
from jax.experimental.pallas import tpu as pltpu

F32 = jnp.float32
BF = jnp.bfloat16

D = 1024
FF = 2816
NH = 8
HD = 128
CH = 64
GW = 64
TM = 256
NQKV = 3 * NH * HD
NPOOL = 512
POOL_WINDOWS = (2, 4, 8, 16)
NMIX = 6688
NMIXP = 6784
EPS = 1e-6
NCHIP = 4
VMEM_LIMIT = 56 * 1024 * 1024

ADAM_LR, ADAM_B1, ADAM_B2, ADAM_EPS, ADAM_WD, ADAM_STEP = 0.001, 0.9, 0.999, 1e-08, 0.01, 10


def _cparams(*sem):
    return pltpu.CompilerParams(dimension_semantics=sem, vmem_limit_bytes=VMEM_LIMIT)


def _const_spec(shape):
    nd = len(shape)
    return pl.BlockSpec(shape, lambda *_: (0,) * nd, pipeline_mode=pl.Buffered(1))


def _dot(a, b, dims):
    return lax.dot_general(a.astype(BF), b.astype(BF), (dims, ((), ())), preferred_element_type=F32)


def _nn(a, b):
    return _dot(a, b, ((1,), (0,)))


def _nt(a, b):
    return _dot(a, b, ((1,), (1,)))


def _tn(a, b):
    return _dot(a, b, ((0,), (0,)))


def _silu(x):
    return x * jax.nn.sigmoid(x)


def _dsilu(x):
    s = jax.nn.sigmoid(x)
    return s * (1.0 + x * (1.0 - s))


def _norm_mod(x, nw, shift, scale):
    r = lax.rsqrt(jnp.mean(x * x, axis=-1, keepdims=True) + EPS)
    xh = x * r
    n = xh * nw
    return n * (1.0 + scale) + shift, n, xh, r


def _norm_mod_bwd(dh, n, xh, r, nw, scale):
    dn = dh * (1.0 + scale)
    dxh = dn * nw
    dx = r * (dxh - xh * jnp.mean(dxh * xh, axis=-1, keepdims=True))
    rs = lambda t: jnp.sum(t, axis=0, keepdims=True)
    return dx, rs(dh), rs(dh * n), rs(dn * xh)


def _ffn_fwd(x, modv, nw, w_in4, w_out, *, mrow, name, exchange=None):
    n_tok = x.shape[0]
    nt = n_tok // TM
    nset = modv.shape[0]
    ws = w_in4.shape[2]

    def body(x_ref, mod_ref, nw_ref, win_ref, wout_ref, x1_ref, h_ref, gu_ref, f_ref):
        xv = x_ref[...]
        shift, scale, gate = mod_ref[0, mrow:mrow + 1, :], mod_ref[0, mrow + 1:mrow + 2, :], mod_ref[0, mrow + 2:mrow + 3, :]
        h, _, _, _ = _norm_mod(xv, nw_ref[...], shift, scale)
        hb = h.astype(BF)
        h_ref[...] = hb
        gus = [_nn(hb, win_ref[s]) for s in range(NCHIP)]
        for s in range(NCHIP):
            gu_ref[:, s * ws:(s + 1) * ws] = gus[s].astype(BF)
        g = jnp.concatenate(gus[:2], axis=1)
        u = jnp.concatenate(gus[2:], axis=1)
        f = _nn(_silu(g) * u, wout_ref[...])
        f_ref[...] = f.astype(BF)
        x1_ref[...] = xv + 0.5 * gate * f

    tile = lambda w: pl.BlockSpec((TM, w), lambda i: (i, 0))
    return _pallas(
        body, (x, modv, nw, w_in4, w_out), name=name, grid=(nt,), sem=("parallel",), exchange=exchange,
        in_specs=[tile(D), pl.BlockSpec((1, 16, D), lambda i: (jnp.minimum(i, nset - 1), 0, 0)), _const_spec((1, D)),
                  _const_spec(w_in4.shape), _const_spec(w_out.shape)],
        out_specs=[tile(D), tile(D), tile(2 * FF), tile(D)],
        out_shape=[jax.ShapeDtypeStruct((n_tok, D), F32), jax.ShapeDtypeStruct((n_tok, D), BF),
                   jax.ShapeDtypeStruct((n_tok, 2 * FF), BF), jax.ShapeDtypeStruct((n_tok, D), BF)])


def _ffn_bwd(dxo, x, gu, fo, modv, nw, w_in4, w_out, *, mrow, dx_skip, name, exchange=None):
    n_tok = x.shape[0]
    nt = n_tok // TM
    nset = modv.shape[0]
    ws = w_in4.shape[2]

    def body(dxo_ref, x_ref, gu_ref, f_ref, mod_ref, nw_ref, win_ref, wout_ref, dx_ref, a_ref, df_ref, dgu_ref, acc_ref):
        i = pl.program_id(0)
        xv = x_ref[...]
        dxo_v = dxo_ref[...]
        shift, scale, gate = mod_ref[0, mrow:mrow + 1, :], mod_ref[0, mrow + 1:mrow + 2, :], mod_ref[0, mrow + 2:mrow + 3, :]
        _, n, xh, r = _norm_mod(xv, nw_ref[...], shift, scale)
        df = 0.5 * gate * dxo_v
        dfb = df.astype(BF)
        df_ref[...] = dfb
        dgate = jnp.sum(0.5 * dxo_v * f_ref[...].astype(F32), axis=0, keepdims=True)
        da = _nt(dfb, wout_ref[...])
        g = gu_ref[:, :FF].astype(F32)
        u = gu_ref[:, FF:].astype(F32)
        sg = _silu(g)
        a_ref[...] = (sg * u).astype(BF)
        dgu_ref[:, :FF] = (da * u * _dsilu(g)).astype(BF)
        dgu_ref[:, FF:] = (da * sg).astype(BF)
        dh = _nt(dgu_ref[:, 0:ws], win_ref[0])
        for s in range(1, NCHIP):
            dh = dh + _nt(dgu_ref[:, s * ws:(s + 1) * ws], win_ref[s])
        dx, dshift, dscale, dnw = _norm_mod_bwd(dh, n, xh, r, nw_ref[...], scale)
        dx_ref[...] = dxo_v + dx

        @pl.when((i == 0) | (i == nset - 1))
        def _():
            acc_ref[...] = jnp.zeros_like(acc_ref)

        acc_ref[0, 0:1, :] += dshift
        acc_ref[0, 1:2, :] += dscale
        acc_ref[0, 2:3, :] += dgate
        acc_ref[0, 3:4, :] += dnw

    tile = lambda w: pl.BlockSpec((TM, w), lambda i: (i, 0))
    return _pallas(
        body, (dxo, x, gu, fo, modv, nw, w_in4, w_out), name=name, grid=(nt,), sem=("arbitrary",), exchange=exchange,
        in_specs=[tile(D), tile(D), tile(2 * FF), tile(D),
                  pl.BlockSpec((1, 16, D), lambda i: (jnp.minimum(i, nset - 1), 0, 0)), _const_spec((1, D)),
                  _const_spec(w_in4.shape), _const_spec(w_out.shape)],
        out_specs=[pl.BlockSpec((TM, D), lambda i: (jnp.maximum(i - dx_skip, 0), 0)), tile(FF), tile(D), tile(2 * FF),
                   pl.BlockSpec((1, 8, D), lambda i: (jnp.minimum(i, nset - 1), 0, 0))],
        out_shape=[jax.ShapeDtypeStruct((n_tok - dx_skip * TM, D), F32), jax.ShapeDtypeStruct((n_tok, FF), BF),
                   jax.ShapeDtypeStruct((n_tok, D), BF), jax.ShapeDtypeStruct((n_tok, 2 * FF), BF),
                   jax.ShapeDtypeStruct((nset, 8, D), F32)])


def _k_tile(n, target=3072):
    return max(t for t in range(TM, min(n, target) + 1, TM) if n % t == 0)


def _matmul_tn(a, b, *, tmm, tn, tk, nsplit=1, name):
    n_tok, m = a.shape
    kk = b.shape[1]
    nk = n_tok // tk

    def body(a_ref, b_ref, o_ref, acc):
        k = pl.program_id(2)

        @pl.when(k == 0)
        def _():
            acc[...] = jnp.zeros_like(acc)

        acc[...] += _tn(a_ref[...], b_ref[...])

        @pl.when(k == nk - 1)
        def _():
            o_ref[...] = acc[...].astype(BF).reshape(o_ref.shape)

    if nsplit == 1:
        out_shape = jax.ShapeDtypeStruct((m, kk), BF)
        out_spec = pl.BlockSpec((tmm, tn), lambda i, j, k: (i, j))
    else:
        assert tn == kk // nsplit
        out_shape = jax.ShapeDtypeStruct((nsplit, m, tn), BF)
        out_spec = pl.BlockSpec((1, tmm, tn), lambda i, j, k: (j, i, 0))
    return pl.pallas_call(
        body, name=name, grid=(m // tmm, kk // tn, nk),
        in_specs=[pl.BlockSpec((tk, tmm), lambda i, j, k: (k, i)), pl.BlockSpec((tk, tn), lambda i, j, k: (k, j))],
        out_specs=out_spec, out_shape=out_shape,
        scratch_shapes=[pltpu.VMEM((tmm, tn), F32)],
        compiler_params=_cparams("parallel", "parallel", "arbitrary"),
    )(a, b)


_MIX_PARTS = (("qkv", 0, NQKV), ("gate", NQKV, 1024), ("pool", NQKV + 1024, NPOOL), ("br", NQKV + 1024 + NPOOL, 2048),
              ("ab", NMIXP - 128, 128))


def _mix_in_fwd(x1, modv, nw, w_mix, *, name):
    n_tok = x1.shape[0]

    def body(x_ref, mod_ref, nw_ref, w_ref, u_ref, *p_refs):
        u, _, _, _ = _norm_mod(x_ref[...], nw_ref[...], mod_ref[0, 3:4, :], mod_ref[0, 4:5, :])
        ub = u.astype(BF)
        u_ref[...] = ub
        for (_, c0, w), p_ref in zip(_MIX_PARTS, p_refs):
            p_ref[...] = _nn(ub, w_ref[:, c0:c0 + w])

    tile = lambda w: pl.BlockSpec((TM, w), lambda i: (i, 0))
    ctile = lambda w: pl.BlockSpec((TM, w), lambda i: (i + 1, 0))
    return pl.pallas_call(
        body, name=name, grid=(n_tok // TM,),
        in_specs=[tile(D), pl.BlockSpec((1, 16, D), lambda i: (jnp.minimum(i, 1), 0, 0)), _const_spec((1, D)),
                  _const_spec(w_mix.shape)],
        out_specs=[tile(D)] + [tile(w) for _, _, w in _MIX_PARTS],
        out_shape=[jax.ShapeDtypeStruct((n_tok, D), BF)] + [jax.ShapeDtypeStruct((n_tok, w), F32) for _, _, w in _MIX_PARTS],
        compiler_params=_cparams("parallel"),
    )(x1, modv, nw, w_mix)


def _mix_in_bwd(dxo, x1, dqkv, dgate, dpool, dbr, dab_f, dab_b, modv, nw, w_mix, *, name):
    n_tok = x1.shape[0]

    def body(dxo_ref, x_ref, dqkv_ref, dgate_ref, dpool_ref, dbr_ref, dabf_ref, dabb_ref, mod_ref, nw_ref, w_ref,
             dx_ref, dp_ref, acc_ref):
        i = pl.program_id(0)
        lat = i >= 1
        scale = mod_ref[0, 4:5, :]
        _, n, xh, r = _norm_mod(x_ref[...], nw_ref[...], mod_ref[0, 3:4, :], scale)
        dp_ref[:, 0:NQKV] = dqkv_ref[...].astype(BF)
        dp_ref[:, NQKV:NQKV + 1024] = jnp.where(lat, dgate_ref[...], 0.0).astype(BF)
        dp_ref[:, NQKV + 1024:NQKV + 1536] = jnp.where(lat, dpool_ref[...], 0.0).astype(BF)
        dp_ref[:, NQKV + 1536:NMIXP - 128] = jnp.where(lat, dbr_ref[...], 0.0).astype(BF)
        dp_ref[:, NMIXP - 128:] = (dabf_ref[...] + dabb_ref[...]).astype(BF)
        du = _nt(dp_ref[...], w_ref[...])
        dx, dshift, dscale, dnw = _norm_mod_bwd(du, n, xh, r, nw_ref[...], scale)
        dx_ref[...] = jnp.where(lat, dxo_ref[...], 0.0) + dx

        @pl.when(i <= 1)
        def _():
            acc_ref[...] = jnp.zeros_like(acc_ref)

        acc_ref[0, 0:1, :] += dshift
        acc_ref[0, 1:2, :] += dscale
        acc_ref[0, 3:4, :] += dnw

    tile = lambda w: pl.BlockSpec((TM, w), lambda i: (i, 0))
    ltile = lambda w: pl.BlockSpec((TM, w), lambda i: (jnp.maximum(i - 1, 0), 0))
    return pl.pallas_call(
        body, name=name, grid=(n_tok // TM,),
        in_specs=[ltile(D), tile(D), tile(NQKV), ltile(1024), ltile(NPOOL), ltile(2048), tile(128), tile(128),
                  pl.BlockSpec((1, 16, D), lambda i: (jnp.minimum(i, 1), 0, 0)), _const_spec((1, D)), _const_spec(w_mix.shape)],
        out_specs=[tile(D), tile(NMIXP), pl.BlockSpec((1, 8, D), lambda i: (jnp.minimum(i, 1), 0, 0))],
        out_shape=[jax.ShapeDtypeStruct((n_tok, D), F32), jax.ShapeDtypeStruct((n_tok, NMIXP), BF),
                   jax.ShapeDtypeStruct((2, 8, D), F32)],
        compiler_params=_cparams("arbitrary"),
    )(dxo, x1, dqkv, dgate, dpool, dbr, dab_f, dab_b, modv, nw, w_mix)


def _qkv_act(pre, j):
    s = _silu(pre)
    nrm = s * lax.rsqrt(jnp.sum(s * s, axis=-1, keepdims=True) + EPS)
    nrm = nrm * jnp.where(j == 0, HD ** -0.5, 1.0)
    return jnp.where(j < 2, nrm, s)


def _halo_specs(nt):
    r = TM // 8
    w = NH * HD
    main = pl.BlockSpec((TM, w), lambda j, i: (i, j))
    prev = pl.BlockSpec((8, w), lambda j, i: (jnp.maximum(i * r - 1, 0), j))
    nxt = pl.BlockSpec((8, w), lambda j, i: (jnp.minimum((i + 1) * r, nt * r - 1), j))
    return main, prev, nxt


def _prep_fwd(p_qkv, conv_w8, *, name):
    n_tok = p_qkv.shape[0]
    nt = n_tok // TM

    def body(x_ref, xp_ref, xn_ref, w_ref, o_ref, pre_ref, win):
        j, i = pl.program_id(0), pl.program_id(1)
        has_prev = (i != 0) & (i != 1)
        has_next = (i != 0) & (i != nt - 1)
        win[0:8, :] = jnp.where(has_prev, xp_ref[...], 0.0)
        win[8:8 + TM, :] = x_ref[...]
        win[8 + TM:, :] = jnp.where(has_next, xn_ref[...], 0.0)
        for h in range(NH):
            hs = slice(h * HD, (h + 1) * HD)
            pre = win[6:6 + TM, hs] * w_ref[0:1, hs]
            for k in range(1, 5):
                pre = pre + win[6 + k:6 + k + TM, hs] * w_ref[k:k + 1, hs]
            pre_ref[:, hs] = pre
            o_ref[:, hs] = _qkv_act(pre, j)

    main, prev, nxt = _halo_specs(nt)
    wq = NH * HD
    return pl.pallas_call(
        body, name=name, grid=(3, nt),
        in_specs=[main, prev, nxt, pl.BlockSpec((8, wq), lambda j, i: (0, j))],
        out_specs=[main, main], out_shape=[jax.ShapeDtypeStruct((n_tok, NQKV), F32)] * 2,
        scratch_shapes=[pltpu.VMEM((TM + 16, wq), F32)],
        compiler_params=_cparams("parallel", "arbitrary"),
    )(p_qkv, p_qkv, p_qkv, conv_w8)


def _prep_bwd(p_qkv, pre, dqkv_f, dqkv_b, conv_w8, *, name):
    n_tok = p_qkv.shape[0]
    nt = n_tok // TM
    wq = NH * HD

    def body(x_ref, p_ref, pp_ref, pn_ref, g_ref, gp_ref, gn_ref, g2_ref, g2p_ref, g2n_ref, w_ref, dx_ref, dw_ref, pwin, gwin, dwin):
        j, i = pl.program_id(0), pl.program_id(1)
        has_prev = (i != 0) & (i != 1)
        has_next = (i != 0) & (i != nt - 1)
        pwin[0:8, :] = jnp.where(has_prev, pp_ref[...], 0.0)
        pwin[8:8 + TM, :] = p_ref[...]
        pwin[8 + TM:, :] = jnp.where(has_next, pn_ref[...], 0.0)
        gwin[0:8, :] = jnp.where(has_prev, gp_ref[...] + g2p_ref[...], 0.0)
        gwin[8:8 + TM, :] = g_ref[...] + g2_ref[...]
        gwin[8 + TM:, :] = jnp.where(has_next, gn_ref[...] + g2n_ref[...], 0.0)

        @pl.when(i == 0)
        def _():
            dw_ref[...] = jnp.zeros_like(dw_ref)

        for h in range(NH):
            hs = slice(h * HD, (h + 1) * HD)
            _, vjp = jax.vjp(lambda t: _qkv_act(t, j), pwin[:, hs])
            dwin[:, hs] = vjp(gwin[:, hs])[0]
            xv = x_ref[:, hs]
            dx = None
            for k in range(5):
                sh = dwin[10 - k:10 - k + TM, hs]
                dx = sh * w_ref[k:k + 1, hs] if dx is None else dx + sh * w_ref[k:k + 1, hs]
                dw_ref[k:k + 1, hs] += jnp.sum(sh * xv, axis=0, keepdims=True)
            dx_ref[:, hs] = dx

    main, prev, nxt = _halo_specs(nt)
    wspec = pl.BlockSpec((8, wq), lambda j, i: (0, j))
    return pl.pallas_call(
        body, name=name, grid=(3, nt),
        in_specs=[main, main, prev, nxt, main, prev, nxt, main, prev, nxt, wspec],
        out_specs=[main, wspec],
        out_shape=[jax.ShapeDtypeStruct((n_tok, NQKV), F32), jax.ShapeDtypeStruct((8, NQKV), F32)],
        scratch_shapes=[pltpu.VMEM((TM + 16, wq), F32)] * 3,
        compiler_params=_cparams("parallel", "arbitrary"),
    )(p_qkv, pre, pre, pre, dqkv_f, dqkv_f, dqkv_f, dqkv_b, dqkv_b, dqkv_b, conv_w8)


@jax.custom_vjp
def _mm_nn(a, b):
    return _nn(a, b)


@jax.custom_vjp
def _mm_nt(a, b):
    return _nt(a, b)


@jax.custom_vjp
def _mm_tn(a, b):
    return _tn(a, b)


_mm_nn.defvjp(lambda a, b: (_nn(a, b), (a, b)), lambda r, g: (_mm_nt(g, r[1]), _mm_tn(r[0], g)))
_mm_nt.defvjp(lambda a, b: (_nt(a, b), (a, b)), lambda r, g: (_mm_nn(g, r[1]), _mm_tn(g, r[0])))
_mm_tn.defvjp(lambda a, b: (_tn(a, b), (a, b)), lambda r, g: (_mm_nt(r[1], g), _mm_nn(r[0], g)))


def _each(f, *lists):
    return tuple(f(*a) for a in zip(*lists))


def _unit_tri_inv(ls, revs):
    ii = lax.broadcasted_iota(jnp.int32, (CH, CH), 0)
    jj = lax.broadcasted_iota(jnp.int32, (CH, CH), 1)
    eye = (ii == jj).astype(F32)
    xs = None
    s = 1
    while s < CH:
        same = (ii & -(2 * s)) == (jj & -(2 * s))
        off = {False: same & ((ii & s) != 0) & ((jj & s) == 0), True: same & ((jj & s) != 0) & ((ii & s) == 0)}
        cs = _each(lambda l, r: jnp.where(off[r], l, 0.0), ls, revs)
        if xs is None:
            xs = _each(lambda c: eye - c, cs)
        else:
            xc = _each(_nn, xs, cs)
            xcx = _each(_nn, xc, xs)
            xs = _each(lambda x, t: x - t, xs, xcx)
        s *= 2
    return xs


@functools.lru_cache(maxsize=None)
def _tri_solve(revs):
    @jax.custom_vjp
    def solve(ls, rhss):
        return _each(_mm_nn, _unit_tri_inv(ls, revs), rhss)

    def fwd(ls, rhss):
        ainv = _unit_tri_inv(ls, revs)
        xs = _each(_mm_nn, ainv, rhss)
        return xs, (ainv, xs)

    def bwd(res, gs):
        ainv, xs = res
        drhs = _each(_mm_tn, ainv, gs)
        return _each(lambda d, x: -_mm_nt(d, x), drhs, xs), drhs

    solve.defvjp(fwd, bwd)
    return solve


def _chunk_prep(q, k, v, beta, g, *, revs):
    ii = lax.broadcasted_iota(jnp.int32, (CH, CH), 0)
    jj = lax.broadcasted_iota(jnp.int32, (CH, CH), 1)
    eye = ii == jj
    incl_of = {False: ii >= jj, True: ii <= jj}
    strict_of = {False: ii > jj, True: ii < jj}
    g_row = _each(lambda t: jnp.sum(jnp.where(eye, t, 0.0), axis=0, keepdims=True), g)
    cum = _each(lambda t, r: jnp.sum(jnp.where(incl_of[r], t, 0.0), axis=1, keepdims=True), g_row, revs)
    cum_row = _each(lambda t: jnp.sum(jnp.where(eye, t, 0.0), axis=0, keepdims=True), cum)
    total = _each(lambda t: jnp.sum(t, axis=0, keepdims=True), g)
    decay = _each(lambda c, cr, r: jnp.where(incl_of[r], jnp.exp(jnp.where(incl_of[r], c - cr, 0.0)), 0.0), cum, cum_row, revs)
    kb = _each(jnp.multiply, k, beta)
    vb = _each(jnp.multiply, v, beta)
    kk = _each(_mm_nt, kb, k)
    lmat = _each(lambda t, dc, r: jnp.where(strict_of[r], t * dc, 0.0), kk, decay, revs)
    ecum = _each(jnp.exp, cum)
    rhs = _each(lambda a, b, e: jnp.concatenate([a, b * e], axis=1), vb, kb, ecum)
    sol = _tri_solve(revs)(lmat, rhs)
    qk = _each(_mm_nt, q, k)
    aqk = _each(jnp.multiply, qk, decay)
    qd = _each(jnp.multiply, q, ecum)
    kd = _each(lambda a, t, c: a * jnp.exp(t - c), k, total, cum)
    return sol, aqk, qd, kd, _each(jnp.exp, total)


def _chunk_rec(sol, aqk, qd, kd, bl, s):
    ws = _each(lambda so, st: _mm_nn(so[:, HD:], st), sol, s)
    v_new = _each(lambda so, t: so[:, :HD] - t, sol, ws)
    qs = _each(_mm_nn, qd, s)
    av = _each(_mm_nn, aqk, v_new)
    o = _each(jnp.add, qs, av)
    kv = _each(_mm_tn, kd, v_new)
    s_new = _each(lambda st, b, u: st * b + u, s, bl, kv)
    return o, s_new


def _lane_col(x, c):
    lane = lax.broadcasted_iota(jnp.int32, x.shape, 1)
    return jnp.sum(jnp.where(lane == c, x, 0.0), axis=1, keepdims=True)


def _beta_g(ab, cst, d, h):
    braw = _lane_col(ab, NH * d + h)
    araw = _lane_col(ab, 2 * NH + NH * d + h)
    ea = _lane_col(cst[0:1, :], 2 * NH + NH * d + h)
    dt = _lane_col(cst[1:2, :], 2 * NH + NH * d + h)
    z = araw + dt
    softplus = jnp.maximum(z, 0.0) + jnp.log(1.0 + jnp.exp(-jnp.abs(z)))
    return jax.nn.sigmoid(braw), -ea * softplus, z, ea


STEPS = 2
TS = STEPS * CH
_CHAINS = tuple((t, d, h) for t in range(STEPS) for d in (0, 1) for h in range(NH))
_REVS = tuple(bool(d) for _, d, _ in _CHAINS)
_PER_STEP = 2 * NH


def _chain_inputs(refs, r0s, ab_refs, cst):
    hs = lambda h: slice(h * HD, (h + 1) * HD)
    abvs = [[ab_refs[d][pl.ds(r0s[t][d], CH), :] for d in (0, 1)] for t in range(STEPS)]
    q = _each(lambda c: refs[c[1]][0][pl.ds(r0s[c[0]][c[1]], CH), hs(c[2])], _CHAINS)
    k = _each(lambda c: refs[c[1]][1][pl.ds(r0s[c[0]][c[1]], CH), hs(c[2])], _CHAINS)
    v = _each(lambda c: refs[c[1]][2][pl.ds(r0s[c[0]][c[1]], CH), hs(c[2])], _CHAINS)
    bg = _each(lambda c: _beta_g(abvs[c[0]][c[1]], cst, c[1], c[2]), _CHAINS)
    return q, k, v, bg


def _of_step(parts, t):
    return tuple(p[t * _PER_STEP:(t + 1) * _PER_STEP] for p in parts)


def _scan_fwd(qkv, ab, cst, s0, *, row_blk0, nb, name, exchange=None):
    cb = TS // CH
    w = NH * HD

    def body(qf, kf, vf, abf, qb, kb, vb, abb, cst_ref, s0_ref, of_ref, ob_ref, sallf_ref, sallb_ref, sfin_ref, s_scr):
        i = pl.program_id(0)

        @pl.when(i == 0)
        def _():
            s_scr[...] = s0_ref[...]

        o_refs, sall_refs = (of_ref, ob_ref), (sallf_ref, sallb_ref)

        def chunks(ci, carry):
            cs = [(ci * STEPS + t, cb - 1 - ci * STEPS - t) for t in range(STEPS)]
            r0s = [tuple(pl.multiple_of(c * CH, CH) for c in ct) for ct in cs]
            q, k, v, bg = _chain_inputs(((qf, kf, vf), (qb, kb, vb)), r0s, (abf, abb), cst_ref[...])
            parts = _chunk_prep(q, k, v, _each(lambda t: t[0], bg), _each(lambda t: t[1], bg), revs=_REVS)
            s = _each(lambda c: s_scr[c[1], c[2]], _CHAINS[:_PER_STEP])
            for t in range(STEPS):
                for (_, d, h), sv in zip(_CHAINS, s):
                    sall_refs[d][cs[t][d], h] = sv
                o, s = _chunk_rec(*_of_step(parts, t), s)
                for (_, d, h), ov in zip(_CHAINS, o):
                    o_refs[d][pl.ds(r0s[t][d], CH), h * HD:(h + 1) * HD] = ov
            for (_, d, h), sv in zip(_CHAINS, s):
                s_scr[d, h] = sv
            return carry

        lax.fori_loop(0, cb // STEPS, chunks, 0)

        @pl.when(i == nb - 1)
        def _():
            sfin_ref[...] = s_scr[...]

    pos = (lambda i: i, lambda i: nb - 1 - i)
    col = lambda d, c: pl.BlockSpec((TS, w), lambda i: (row_blk0 + pos[d](i), c))
    abs_ = lambda d: pl.BlockSpec((TS, 128), lambda i: (row_blk0 + pos[d](i), 0))
    full4 = pl.BlockSpec((2, NH, HD, HD), lambda i: (0, 0, 0, 0))
    o_spec = lambda d: pl.BlockSpec((TS, w), lambda i: (pos[d](i), 0))
    sall_spec = lambda d: pl.BlockSpec((cb, NH, HD, HD), lambda i: (pos[d](i), 0, 0, 0))
    return _pallas(
        body, (qkv, qkv, qkv, ab, qkv, qkv, qkv, ab, cst, s0), name=name, grid=(nb,), sem=("arbitrary",), exchange=exchange,
        in_specs=[col(0, 0), col(0, 1), col(0, 2), abs_(0), col(1, 0), col(1, 1), col(1, 2), abs_(1),
                  pl.BlockSpec((8, 128), lambda i: (0, 0)), full4],
        out_specs=[o_spec(0), o_spec(1), sall_spec(0), sall_spec(1), full4],
        out_shape=[jax.ShapeDtypeStruct((nb * TS, w), F32)] * 2 + [jax.ShapeDtypeStruct((nb * cb, NH, HD, HD), F32)] * 2
        + [jax.ShapeDtypeStruct((2, NH, HD, HD), F32)],
        scratch_shapes=[pltpu.VMEM((2, NH, HD, HD), F32)])


def _scan_bwd(qkv, ab, cst, sall_f, sall_b, do, dsfin, dqkv_f, dqkv_b, dab_f, dab_b, dcst, *, row_blk0, nb, has_do, name,
              exchange=None):
    cb = TS // CH
    w = NH * HD

    def body(qf, kf, vf, abf, qb, kb, vb, abb, cst_ref, sallf_ref, sallb_ref, dof_ref, dob_ref, dsfin_ref, _f, _b, _af, _ab, dcst_in,
             dqkvf_ref, dqkvb_ref, dabf_ref, dabb_ref, dcst_ref, ds0_ref, ds_scr):
        i = pl.program_id(0)

        @pl.when(i == 0)
        def _():
            ds_scr[...] = dsfin_ref[...]
            dcst_ref[...] = dcst_in[...]

        lane = lax.broadcasted_iota(jnp.int32, (CH, 128), 1)
        lane1 = lax.broadcasted_iota(jnp.int32, (1, 128), 1)
        sall_refs, do_refs = (sallf_ref, sallb_ref), (dof_ref, dob_ref)
        dqkv_refs, dab_refs = (dqkvf_ref, dqkvb_ref), (dabf_ref, dabb_ref)

        def chunks(ci, carry):
            cs = [(cb - 1 - ci * STEPS - t, ci * STEPS + t) for t in range(STEPS)]
            r0s = [tuple(pl.multiple_of(c * CH, CH) for c in ct) for ct in cs]
            q, k, v, bg = _chain_inputs(((qf, kf, vf), (qb, kb, vb)), r0s, (abf, abb), cst_ref[...])
            beta, g = _each(lambda t: t[0], bg), _each(lambda t: t[1], bg)
            parts, prep_vjp = jax.vjp(functools.partial(_chunk_prep, revs=_REVS), q, k, v, beta, g)
            ds = _each(lambda c: ds_scr[c[1], c[2]], _CHAINS[:_PER_STEP])
            dparts = []
            for t in range(STEPS):
                s = _each(lambda c: sall_refs[c[1]][cs[t][c[1]], c[2]], _CHAINS[:_PER_STEP])
                _, rec_vjp = jax.vjp(_chunk_rec, *_of_step(parts, t), s)
                do_t = _each(lambda c: do_refs[c[1]][pl.ds(r0s[t][c[1]], CH), c[2] * HD:(c[2] + 1) * HD] if has_do
                             else jnp.zeros((CH, HD), F32), _CHAINS[:_PER_STEP])
                *dpt, ds = rec_vjp((do_t, ds))
                dparts.append(dpt)
            for (_, d, h), dsv in zip(_CHAINS, ds):
                ds_scr[d, h] = dsv
            dq, dk, dv, dbeta, dg = prep_vjp(tuple(sum((dparts[t][j] for t in range(STEPS)), ()) for j in range(len(dparts[0]))))
            dab = [[jnp.zeros((CH, 128), F32), jnp.zeros((CH, 128), F32)] for _ in range(STEPS)]
            dal = jnp.zeros((1, 128), F32)
            for n, (t, d, h) in enumerate(_CHAINS):
                for part, val in enumerate((dq[n], dk[n], dv[n])):
                    dqkv_refs[d][pl.ds(r0s[t][d], CH), part * w + h * HD:part * w + (h + 1) * HD] = val
                z, ea = bg[n][2], bg[n][3]
                dbraw = dbeta[n] * beta[n] * (1.0 - beta[n])
                daraw = dg[n] * (-ea) * jax.nn.sigmoid(z)
                dab[t][d] = dab[t][d] + jnp.where(lane == NH * d + h, dbraw, 0.0) + jnp.where(lane == 2 * NH + NH * d + h, daraw, 0.0)
                dal = dal + jnp.where(lane1 == 2 * NH + NH * d + h, jnp.sum(dg[n] * g[n], axis=0, keepdims=True), 0.0)
            dsum = jnp.zeros((CH, 128), F32)
            for t in range(STEPS):
                for d in (0, 1):
                    dab_refs[d][pl.ds(r0s[t][d], CH), :] = dab[t][d]
                    dsum = dsum + dab[t][d]
            dcst_ref[0:1, :] += dal
            dcst_ref[1:2, :] += jnp.sum(jnp.where(lane >= 2 * NH, dsum, 0.0), axis=0, keepdims=True)
            return carry

        lax.fori_loop(0, cb // STEPS, chunks, 0)

        @pl.when(i == nb - 1)
        def _():
            ds0_ref[...] = ds_scr[...]

    pos = (lambda i: nb - 1 - i, lambda i: i)
    col = lambda d, c: pl.BlockSpec((TS, w), lambda i: (row_blk0 + pos[d](i), c))
    abs_ = lambda d: pl.BlockSpec((TS, 128), lambda i: (row_blk0 + pos[d](i), 0))
    full4 = pl.BlockSpec((2, NH, HD, HD), lambda i: (0, 0, 0, 0))
    small = pl.BlockSpec((8, 128), lambda i: (0, 0))
    hbm = pl.BlockSpec(memory_space=pl.ANY)
    sall_spec = lambda d: pl.BlockSpec((cb, NH, HD, HD), lambda i: (pos[d](i), 0, 0, 0))
    do_spec = (lambda d: pl.BlockSpec((TS, w), lambda i: (pos[d](i), 0))) if has_do else (lambda d: small)
    acc_specs = [pl.BlockSpec((TS, 3 * w), lambda i: (row_blk0 + pos[0](i), 0)),
                 pl.BlockSpec((TS, 3 * w), lambda i: (row_blk0 + pos[1](i), 0)), abs_(0), abs_(1), small]
    return _pallas(
        body, (qkv, qkv, qkv, ab, qkv, qkv, qkv, ab, cst, sall_f, sall_b, do, do, dsfin, dqkv_f, dqkv_b, dab_f, dab_b, dcst),
        name=name, grid=(nb,), sem=("arbitrary",), exchange=exchange,
        in_specs=[col(0, 0), col(0, 1), col(0, 2), abs_(0), col(1, 0), col(1, 1), col(1, 2), abs_(1), small,
                  sall_spec(0), sall_spec(1), do_spec(0), do_spec(1), full4, hbm, hbm, hbm, hbm, small],
        out_specs=acc_specs + [full4],
        out_shape=[jax.ShapeDtypeStruct(dqkv_f.shape, F32), jax.ShapeDtypeStruct(dqkv_b.shape, F32),
                   jax.ShapeDtypeStruct(dab_f.shape, F32), jax.ShapeDtypeStruct(dab_b.shape, F32),
                   jax.ShapeDtypeStruct((8, 128), F32), jax.ShapeDtypeStruct((2, NH, HD, HD), F32)],
        aliases={14: 0, 15: 1, 16: 2, 17: 3, 18: 4},
        scratch_shapes=[pltpu.VMEM((2, NH, HD, HD), F32)])


def _pool(xin, *, row0, transpose, name):
    n_tok = xin.shape[0] - row0
    rows = n_tok // GW
    pad = 8 * GW
    tt = 512
    gsh = GW.bit_length() - 1

    def body(x_ref, o_ref, ybuf):
        ii = lax.broadcasted_iota(jnp.int32, (128, 128), 0)
        jj = lax.broadcasted_iota(jnp.int32, (128, 128), 1)
        same_row = (ii >> gsh) == (jj >> gsh)
        ci, cj = ii & (GW - 1), jj & (GW - 1)
        tok = lax.broadcasted_iota(jnp.int32, (tt, 1), 0)
        zpad = jnp.zeros((pad, 128), F32)
        for gi, wdw in enumerate(POOL_WINDOWS):
            lo, hi = wdw // 2, wdw - wdw // 2
            if transpose:
                band = same_row & (ci - cj >= -lo) & (ci - cj < hi)
                offs = range(-hi + 1, lo + 1)
            else:
                band = same_row & (cj - ci >= -lo) & (cj - ci < hi)
                offs = range(-lo, hi)
            bandm = band.astype(BF)
            cs = slice(gi * 128, (gi + 1) * 128)
            ybuf[0:pad, :] = zpad
            ybuf[pad + n_tok:, :] = zpad

            def inv_area(t0):
                t = t0 + tok
                r, c = t >> gsh, t & (GW - 1)
                nr = jnp.minimum(r + hi, rows) - jnp.maximum(r - lo, 0)
                nc = jnp.minimum(c + hi, GW) - jnp.maximum(c - lo, 0)
                return 1.0 / (nr * nc).astype(F32)

            def col_pass(b, carry):
                t0 = pl.multiple_of(b * tt, tt)
                xv = x_ref[pl.ds(row0 + t0, tt), cs]
                if transpose:
                    xv = xv * inv_area(t0)
                hi_part = xv.astype(BF)
                lo_part = (xv - hi_part.astype(F32)).astype(BF)
                for s in range(tt // 128):
                    sl = slice(s * 128, (s + 1) * 128)
                    y = (jnp.dot(bandm, hi_part[sl], preferred_element_type=F32)
                         + jnp.dot(bandm, lo_part[sl], preferred_element_type=F32))
                    ybuf[pl.ds(pad + t0 + s * 128, 128), :] = y
                return carry

            lax.fori_loop(0, n_tok // tt, col_pass, 0)

            def row_pass(b, carry):
                t0 = pl.multiple_of(b * tt, tt)
                acc = ybuf[pl.ds(pad + t0 + offs[0] * GW, tt), :]
                for dr in offs[1:]:
                    acc = acc + ybuf[pl.ds(pad + t0 + dr * GW, tt), :]
                xv = x_ref[pl.ds(row0 + t0, tt), cs]
                if not transpose:
                    acc = acc * inv_area(t0)
                o_ref[pl.ds(t0, tt), cs] = acc - xv
                return carry

            lax.fori_loop(0, n_tok // tt, row_pass, 0)

    return pl.pallas_call(
        body, name=name, out_shape=jax.ShapeDtypeStruct((n_tok, NPOOL), F32),
        in_specs=[pl.BlockSpec(memory_space=pltpu.VMEM)], out_specs=pl.BlockSpec(memory_space=pltpu.VMEM),
        scratch_shapes=[pltpu.VMEM((n_tok + 2 * pad, 128), F32)],
        compiler_params=pltpu.CompilerParams(vmem_limit_bytes=VMEM_LIMIT),
    )(xin)


def _merge_parts(of, ob, pgate, pd, br, gnw, pw_ref, pscale, wg_ref, wp_ref):
    o = of + ob
    ons, ohs, rs = [], [], []
    for h in range(NH):
        oh = o[:, h * HD:(h + 1) * HD]
        r = lax.rsqrt(jnp.mean(oh * oh, axis=-1, keepdims=True) + EPS)
        ohs.append(oh * r)
        rs.append(r)
        ons.append(oh * r * gnw)
    on = jnp.concatenate(ons, axis=1)
    og = on * _silu(pgate)
    y_gdn = _nn(og, wg_ref[...])
    ypre = jnp.concatenate([_nn(pd[:, g * 128:(g + 1) * 128], pw_ref[g]) for g in range(4)], axis=1)
    yp = ypre * pscale
    y_pool = _nn(yp, wp_ref[...])
    g_pool = jax.nn.sigmoid(br[:, :D])
    g_gdn = jax.nn.sigmoid(br[:, D:])
    return dict(on=on, ohs=ohs, rs=rs, og=og, y_gdn=y_gdn, ypre=ypre, yp=yp, y_pool=y_pool, g_pool=g_pool, g_gdn=g_gdn)


def _merge_fwd(x1, of, ob, pgate, pd, br, modv, gnw, pool_w, pscale, w_gdn, w_pool, w_mo, *, name):
    n_tok = of.shape[0]

    def body(x_ref, of_ref, ob_ref, pg_ref, pd_ref, br_ref, mod_ref, gnw_ref, pw_ref, ps_ref, wg_ref, wp_ref, wmo_ref,
             x2_ref, og_ref, yp_ref, m_ref, mix_ref):
        t = _merge_parts(of_ref[...], ob_ref[...], pg_ref[...], pd_ref[...], br_ref[...], gnw_ref[...], pw_ref, ps_ref[...],
                         wg_ref, wp_ref)
        m = t["g_pool"] * t["y_pool"] + t["g_gdn"] * t["y_gdn"]
        mix = _nn(m, wmo_ref[...])
        og_ref[...] = t["og"].astype(BF)
        yp_ref[...] = t["yp"].astype(BF)
        m_ref[...] = m.astype(BF)
        mix_ref[...] = mix.astype(BF)
        x2_ref[...] = x_ref[...] + mod_ref[0, 5:6, :] * mix

    tile = lambda w: pl.BlockSpec((TM, w), lambda i: (i, 0))
    ctile = lambda w: pl.BlockSpec((TM, w), lambda i: (i + 1, 0))
    return pl.pallas_call(
        body, name=name, grid=(n_tok // TM,),
        in_specs=[ctile(D), tile(D), tile(D), ctile(D), tile(NPOOL), ctile(2 * D),
                  pl.BlockSpec((1, 16, D), lambda i: (1, 0, 0)), _const_spec((1, HD)), _const_spec((4, 128, 128)),
                  _const_spec((1, NPOOL)), _const_spec((D, D)), _const_spec((NPOOL, D)), _const_spec((D, D))],
        out_specs=[tile(D), tile(D), tile(NPOOL), tile(D), tile(D)],
        out_shape=[jax.ShapeDtypeStruct((n_tok, D), F32), jax.ShapeDtypeStruct((n_tok, D), BF),
                   jax.ShapeDtypeStruct((n_tok, NPOOL), BF), jax.ShapeDtypeStruct((n_tok, D), BF),
                   jax.ShapeDtypeStruct((n_tok, D), BF)],
        compiler_params=_cparams("parallel"),
    )(x1, of, ob, pgate, pd, br, modv, gnw, pool_w, pscale, w_gdn, w_pool, w_mo)


def _merge_bwd(dx2, mix, of, ob, pgate, pd, br, modv, gnw, pool_w, pscale, w_gdn, w_pool, w_mo, *, name):
    n_tok = of.shape[0]

    def body(dx2_ref, mix_ref, of_ref, ob_ref, pg_ref, pd_ref, br_ref, mod_ref, gnw_ref, pw_ref, ps_ref, wg_ref, wp_ref, wmo_ref,
             do_ref, dgate_ref, dpd_ref, dbr_ref, dmix_ref, dyg_ref, dyp_ref, acc_ref, dpw_ref):
        i = pl.program_id(0)
        pgate, pdv, gnw = pg_ref[...], pd_ref[...], gnw_ref[...]
        t = _merge_parts(of_ref[...], ob_ref[...], pgate, pdv, br_ref[...], gnw, pw_ref, ps_ref[...], wg_ref, wp_ref)
        dx2v = dx2_ref[...]
        dmix = mod_ref[0, 5:6, :] * dx2v
        dmixb = dmix.astype(BF)
        dmix_ref[...] = dmixb
        dm = _nt(dmixb, wmo_ref[...])
        gp, gg = t["g_pool"], t["g_gdn"]
        dbr_ref[:, :D] = dm * t["y_pool"] * gp * (1.0 - gp)
        dbr_ref[:, D:] = dm * t["y_gdn"] * gg * (1.0 - gg)
        dyp = (dm * gp).astype(BF)
        dyg = (dm * gg).astype(BF)
        dyp_ref[...] = dyp
        dyg_ref[...] = dyg
        dyp_in = _nt(dyp, wp_ref[...])
        dypre = dyp_in * ps_ref[...]
        for g in range(4):
            gs = slice(g * 128, (g + 1) * 128)
            dpd_ref[:, gs] = _nt(dypre[:, gs], pw_ref[g])
        dog = _nt(dyg, wg_ref[...])
        dgate_ref[...] = dog * t["on"] * _dsilu(pgate)
        don = dog * _silu(pgate)
        dgnw = jnp.zeros((1, HD), F32)
        for h in range(NH):
            hs = slice(h * HD, (h + 1) * HD)
            donh, oh, r = don[:, hs], t["ohs"][h], t["rs"][h]
            dgnw = dgnw + jnp.sum(donh * oh, axis=0, keepdims=True)
            doh = donh * gnw
            do_ref[:, hs] = r * (doh - oh * jnp.mean(doh * oh, axis=-1, keepdims=True))

        @pl.when(i == 0)
        def _():
            acc_ref[...] = jnp.zeros_like(acc_ref)
            dpw_ref[...] = jnp.zeros_like(dpw_ref)

        acc_ref[0:1, :] += jnp.sum(dx2v * mix_ref[...].astype(F32), axis=0, keepdims=True)
        acc_ref[1:2, 0:HD] += dgnw
        acc_ref[2:3, 0:NPOOL] += jnp.sum(dyp_in * t["ypre"], axis=0, keepdims=True)
        for g in range(4):
            gs = slice(g * 128, (g + 1) * 128)
            dpw_ref[g] += _tn(pdv[:, gs], dypre[:, gs])

    tile = lambda w: pl.BlockSpec((TM, w), lambda i: (i, 0))
    ctile = lambda w: pl.BlockSpec((TM, w), lambda i: (i + 1, 0))
    return pl.pallas_call(
        body, name=name, grid=(n_tok // TM,),
        in_specs=[tile(D), tile(D), tile(D), tile(D), ctile(D), tile(NPOOL), ctile(2 * D),
                  pl.BlockSpec((1, 16, D), lambda i: (1, 0, 0)), _const_spec((1, HD)), _const_spec((4, 128, 128)),
                  _const_spec((1, NPOOL)), _const_spec((D, D)), _const_spec((NPOOL, D)), _const_spec((D, D))],
        out_specs=[tile(D), tile(D), tile(NPOOL), tile(2 * D), tile(D), tile(D), tile(D),
                   pl.BlockSpec((8, D), lambda i: (0, 0)), pl.BlockSpec((4, 128, 128), lambda i: (0, 0, 0))],
        out_shape=[jax.ShapeDtypeStruct((n_tok, D), F32), jax.ShapeDtypeStruct((n_tok, D), F32),
                   jax.ShapeDtypeStruct((n_tok, NPOOL), F32), jax.ShapeDtypeStruct((n_tok, 2 * D), F32),
                   jax.ShapeDtypeStruct((n_tok, D), BF), jax.ShapeDtypeStruct((n_tok, D), BF), jax.ShapeDtypeStruct((n_tok, D), BF),
                   jax.ShapeDtypeStruct((8, D), F32), jax.ShapeDtypeStruct((4, 128, 128), F32)],
        compiler_params=_cparams("arbitrary"),
    )(dx2, mix, of, ob, pgate, pd, br, modv, gnw, pool_w, pscale, w_gdn, w_pool, w_mo)


def _final(x3, target, fnw, *, name):
    n_tok = x3.shape[0]

    def body(x_ref, t_ref, w_ref, dx_ref, acc_ref):
        xv, w = x_ref[...], w_ref[...]
        r = lax.rsqrt(jnp.mean(xv * xv, axis=-1, keepdims=True) + EPS)
        xh = xv * r
        err = xh * w - t_ref[...]
        dy = err * (1.0 / D)
        dxh = dy * w
        dx_ref[...] = r * (dxh - xh * jnp.mean(dxh * xh, axis=-1, keepdims=True))

        @pl.when(pl.program_id(0) == 0)
        def _():
            acc_ref[...] = jnp.zeros_like(acc_ref)

        acc_ref[0:1, :] += jnp.sum(dy * xh, axis=0, keepdims=True)
        acc_ref[1:2, :] += jnp.sum(err * err, axis=0, keepdims=True) * (0.5 / D)

    tile = pl.BlockSpec((TM, D), lambda i: (i, 0))
    return pl.pallas_call(
        body, name=name, grid=(n_tok // TM,),
        in_specs=[tile, tile, _const_spec((1, D))],
        out_specs=[tile, pl.BlockSpec((8, D), lambda i: (0, 0))],
        out_shape=[jax.ShapeDtypeStruct((n_tok, D), F32), jax.ShapeDtypeStruct((8, D), F32)],
        compiler_params=_cparams("arbitrary"),
    )(x3, target, fnw)


def _split(results, n):
    return (*results[:n], list(results[n:]))


def _mixer_weights(w_mix_in, w_gdn, w_pool, w_mo, conv):
    return dict(w_mix=_regroup_mix(_from_chip_major_cols(w_mix_in)), w_gdn=w_gdn.reshape(D, D), w_pool=_from_chip_major_cols(w_pool),
                w_mo=w_mo.reshape(D, D), conv=jnp.pad(_from_chip_major_cols(conv), ((0, 3), (0, 0))))


def _local_step(xc, target, modv, p, late=None):
    n_all = xc.shape[0]
    t_lat = n_all - TM
    nbc, nbx = TM // TS, t_lat // TS
    mod_lat = modv[1:2]
    gather = (lambda arrs: _ChipExchange(arrs, False)) if late else (lambda arrs: None)
    scatter = (lambda arrs: _ChipExchange(arrs, True)) if late else (lambda arrs: None)

    x1, h1, gu1, f1, *got = _ffn_fwd(xc, modv, p["norm1"], p["w1_in"], p["w1_out"], mrow=0, name="ffn1_fwd",
                                     exchange=gather(late and late[0]))
    if late:
        p = {**p, **_mixer_weights(*got)}
    u, p_qkv, p_gate, p_pool, p_br, p_ab = _mix_in_fwd(x1, modv, p["norm2"], p["w_mix"], name="mix_in_fwd")
    qkv, pre_qkv = _prep_fwd(p_qkv, p["conv"], name="prep_fwd")
    s_zero = jnp.zeros((2, NH, HD, HD), F32)
    _, _, sall_cf, sall_cb, s_ctx = _scan_fwd(qkv, p_ab, p["cst"], s_zero, row_blk0=0, nb=nbc, name="scan_ctx")
    o_f, o_b, sall_f, sall_b, _, *got = _scan_fwd(qkv, p_ab, p["cst"], s_ctx, row_blk0=nbc, nb=nbx, name="scan_lat",
                                                  exchange=gather(late and late[1]))
    if late:
        p = {**p, "w2_in": got[0], "w2_out": got[1].reshape(FF, D)}
    pd = _pool(p_pool, row0=TM, transpose=False, name="pool_fwd")
    merge_w = (modv, p["gnw"], p["pool_w"], p["pscale"], p["w_gdn"], p["w_pool"], p["w_mo"])
    x2, og, yp, m, mix = _merge_fwd(x1, o_f, o_b, p_gate, pd, p_br, *merge_w, name="merge_fwd")
    x3, h3, gu3, f3 = _ffn_fwd(x2, mod_lat, p["norm3"], p["w2_in"], p["w2_out"], mrow=6, name="ffn2_fwd")
    dx3, acc_fin = _final(x3, target, p["fnorm"], name="final")

    dx2, a3, df3, dgu3, acc3 = _ffn_bwd(dx3, x2, gu3, f3, mod_lat, p["norm3"], p["w2_in"], p["w2_out"], mrow=6, dx_skip=0,
                                        name="ffn2_bwd")
    g = {}
    tkl = _k_tile(t_lat)
    g["w2_out"] = _matmul_tn(a3, df3, tmm=FF // 2, tn=D, tk=tkl, name="ffn2_wout_grad").reshape(NCHIP, FF // NCHIP, D)
    g["w2_in"] = _matmul_tn(h3, dgu3, tmm=D, tn=2 * FF // NCHIP, tk=tkl, nsplit=NCHIP, name="ffn2_win_grad")
    do, dgate, dpd, dbr, dmix, dyg, dyp, acc_m, dpw = _merge_bwd(dx2, mix, o_f, o_b, p_gate, pd, p_br, *merge_w, name="merge_bwd")
    g["w_mo"] = _matmul_tn(m, dmix, tmm=D, tn=D, tk=tkl, name="wmo_grad").reshape(NCHIP, D // NCHIP, D)
    g["w_gdn"] = _matmul_tn(og, dyg, tmm=D, tn=D, tk=tkl, name="wgdn_grad").reshape(NCHIP, D // NCHIP, D)
    g["w_pool"] = _matmul_tn(yp, dyp, tmm=NPOOL, tn=D // NCHIP, tk=tkl, nsplit=NCHIP, name="wpool_grad")
    dpool_in = _pool(dpd, row0=0, transpose=True, name="pool_bwd")
    acc = (lax.empty((n_all, NQKV), F32), lax.empty((n_all, NQKV), F32), lax.empty((n_all, 128), F32),
           lax.empty((n_all, 128), F32), jnp.zeros((8, 128), F32))
    behind_scan = ("w2_in", "w2_out", "w_gdn", "w_pool", "w_mo")
    *acc, ds_ctx, landed = _split(_scan_bwd(qkv, p_ab, p["cst"], sall_f, sall_b, do, s_zero, *acc, row_blk0=nbc, nb=nbx, has_do=True,
                                            name="scan_lat_bwd", exchange=scatter([g[k] for k in behind_scan])), 6)
    landed = dict(zip(behind_scan, landed))
    dqkv_f, dqkv_b, dab_f, dab_b, dcst, _ = _scan_bwd(qkv, p_ab, p["cst"], sall_cf, sall_cb, jnp.zeros((8, 128), F32), ds_ctx, *acc,
                                                      row_blk0=0, nb=nbc, has_do=False, name="scan_ctx_bwd")
    dpqkv, dconv = _prep_bwd(p_qkv, pre_qkv, dqkv_f, dqkv_b, p["conv"], name="prep_bwd")
    dx1, dp, acc_mix = _mix_in_bwd(dx2, x1, dpqkv, dgate, dpool_in, dbr, dab_f, dab_b, modv, p["norm2"], p["w_mix"],
                                   name="mix_in_bwd")
    tka = _k_tile(n_all)
    g["w_mix"] = _chip_major_cols(_ungroup_mix(_matmul_tn(u, dp, tmm=256, tn=NMIXP, tk=_k_tile(n_all, 1024), name="wmix_grad")))
    dx_lat, a1, df1, dgu1, acc1, got = _split(_ffn_bwd(dx1, xc, gu1, f1, modv, p["norm1"], p["w1_in"], p["w1_out"], mrow=0, dx_skip=1,
                                                   name="ffn1_bwd", exchange=scatter([g["w_mix"]])), 5)
    landed.update(zip(("w_mix",), got))
    g["w1_out"] = _matmul_tn(a1, df1, tmm=FF // 2, tn=D, tk=tka, name="ffn1_wout_grad").reshape(NCHIP, FF // NCHIP, D)
    g["w1_in"] = _matmul_tn(h1, dgu1, tmm=D, tn=2 * FF // NCHIP, tk=tka, nsplit=NCHIP, name="ffn1_win_grad")

    small = dict(norm1=acc1[0, 3] + acc1[1, 3], norm2=acc_mix[0, 3] + acc_mix[1, 3], norm3=acc3[0, 3], fnorm=acc_fin[0],
                 gnw=acc_m[1, :HD], pscale=acc_m[2, :NPOOL], pool_w=dpw, conv=dconv[:5],
                 a_log=dcst[0, 2 * NH:4 * NH], dt_bias=dcst[1, 2 * NH:4 * NH])
    zero = jnp.zeros((D,), F32)
    dmod = jnp.stack([
        jnp.stack([acc1[0, 0], acc1[0, 1], acc1[0, 2], acc_mix[0, 0], acc_mix[0, 1], zero, zero, zero, zero]),
        jnp.stack([acc1[1, 0], acc1[1, 1], acc1[1, 2], acc_mix[1, 0], acc_mix[1, 1], acc_m[0], acc3[0, 0], acc3[0, 1], acc3[0, 2]]),
    ])
    return jnp.sum(acc_fin[1]), dx_lat, g, landed, small, dmod


_HI = lax.Precision.HIGHEST


def _ada_fwd(c_all, w_sh, b_sh, *, name):
    def body(c_ref, w_ref, b_ref, o_ref):
        o_ref[...] = jnp.dot(_silu(c_ref[...]), w_ref[...], precision=_HI, preferred_element_type=F32) + b_ref[...]

    return pl.pallas_call(body, name=name, out_shape=jax.ShapeDtypeStruct((16, w_sh.shape[1]), F32),
                          compiler_params=pltpu.CompilerParams(vmem_limit_bytes=VMEM_LIMIT))(c_all, w_sh, b_sh)


def _ada_bwd(c_all, dm, w_sh, *, name):
    def body(c_ref, dm_ref, w_ref, dw_ref, dc_ref):
        sc = _silu(c_ref[...])
        dw_ref[...] = lax.dot_general(sc, dm_ref[...], (((0,), (0,)), ((), ())), precision=_HI, preferred_element_type=F32)
        part = lax.dot_general(dm_ref[8:9, :], w_ref[...], (((1,), (1,)), ((), ())), precision=_HI, preferred_element_type=F32)
        dc_ref[...] = jnp.broadcast_to(part, dc_ref.shape)

    return pl.pallas_call(body, name=name,
                          out_shape=[jax.ShapeDtypeStruct(w_sh.shape, F32), jax.ShapeDtypeStruct((8, D), F32)],
                          compiler_params=pltpu.CompilerParams(vmem_limit_bytes=VMEM_LIMIT))(c_all, dm, w_sh)


def _cctx_grad(parts, c_ctx, *, name):
    def body(p_ref, c_ref, o_ref):
        tot = (p_ref[0, 0:1, :] + p_ref[2, 0:1, :]) + (p_ref[4, 0:1, :] + p_ref[6, 0:1, :])
        o_ref[...] = tot * _dsilu(c_ref[...])

    return pl.pallas_call(body, name=name, out_shape=jax.ShapeDtypeStruct((1, D), F32))(parts, c_ctx)


_MESH = pl.DeviceIdType.MESH
_ANY = pl.BlockSpec(memory_space=pl.ANY)


def _flip(v, bit):
    return (1 - v) if bit else v


def _all_gather8(x, *, name):
    def body(x_ref, out_ref, send_sems, recv_sems, local_sem):
        mx, my, mc = lax.axis_index("x"), lax.axis_index("y"), lax.axis_index("c")
        me = 4 * mx + 2 * my + mc
        mine = pltpu.make_async_copy(x_ref, out_ref.at[me], local_sem)
        mine.start()
        sends, recvs = [], []
        for k in range(1, 8):
            px, py, pc = _flip(mx, k & 4), _flip(my, k & 2), _flip(mc, k & 1)
            sends.append(pltpu.make_async_remote_copy(src_ref=x_ref, dst_ref=out_ref.at[me], send_sem=send_sems.at[k - 1],
                                                      recv_sem=recv_sems.at[k - 1], device_id=(px, py, pc), device_id_type=_MESH))
            recvs.append(pltpu.make_async_remote_copy(src_ref=x_ref, dst_ref=out_ref.at[4 * px + 2 * py + pc],
                                                      send_sem=send_sems.at[k - 1], recv_sem=recv_sems.at[k - 1],
                                                      device_id=(px, py, pc), device_id_type=_MESH))
        for cp in sends:
            cp.start()
        for cp in recvs:
            cp.wait_recv()
        for cp in sends:
            cp.wait_send()
        mine.wait()

    vm = pl.BlockSpec(memory_space=pltpu.VMEM)
    return pl.pallas_call(
        body, name=name, out_shape=jax.ShapeDtypeStruct((8,) + x.shape, x.dtype), in_specs=[vm], out_specs=vm,
        scratch_shapes=[pltpu.SemaphoreType.DMA((7,)), pltpu.SemaphoreType.DMA((7,)), pltpu.SemaphoreType.DMA],
        compiler_params=pltpu.CompilerParams(vmem_limit_bytes=VMEM_LIMIT),
    )(x)


class _ChipExchange:
    def __init__(self, arrs, scatter):
        self.arrs, self.scatter, self.n = list(arrs), scatter, len(arrs)
        self.out_shape = [jax.ShapeDtypeStruct(a.shape if scatter else (NCHIP,) + a.shape, a.dtype) for a in self.arrs]
        links = self.n * (NCHIP - 1)
        self.scratch = [pltpu.SemaphoreType.DMA((links,)), pltpu.SemaphoreType.DMA((links,)), pltpu.SemaphoreType.DMA((self.n,))]

    def copies(self, ins, outs, send_sems, recv_sems, local_sems):
        mx, my, mc = lax.axis_index("x"), lax.axis_index("y"), lax.axis_index("c")
        me = 2 * mx + my
        local, sends, recvs = [], [], []
        for j in range(self.n):
            src_own = ins[j].at[me] if self.scatter else ins[j]
            local.append(pltpu.make_async_copy(src_own, outs[j].at[me], local_sems.at[j]))
            for k in range(1, NCHIP):
                px, py = _flip(mx, k & 2), _flip(my, k & 1)
                peer = 2 * px + py
                sem = j * (NCHIP - 1) + k - 1
                src = ins[j].at[peer] if self.scatter else ins[j]
                sends.append(pltpu.make_async_remote_copy(src_ref=src, dst_ref=outs[j].at[me], send_sem=send_sems.at[sem],
                                                          recv_sem=recv_sems.at[sem], device_id=(px, py, mc), device_id_type=_MESH))
                recvs.append(pltpu.make_async_remote_copy(src_ref=src, dst_ref=outs[j].at[peer], send_sem=send_sems.at[sem],
                                                          recv_sem=recv_sems.at[sem], device_id=(px, py, mc), device_id_type=_MESH))
        return local, sends, recvs

    @staticmethod
    def start(local, sends, recvs):
        for cp in local + sends:
            cp.start()

    @staticmethod
    def finish(local, sends, recvs):
        for cp in recvs:
            cp.wait_recv()
        for cp in sends:
            cp.wait_send()
        for cp in local:
            cp.wait()


def _chip_exchange(arrs, *, scatter, name):
    ex = _ChipExchange(arrs, scatter)

    def body(*refs):
        cps = ex.copies(refs[:ex.n], refs[ex.n:2 * ex.n], *refs[2 * ex.n:])
        ex.start(*cps)
        ex.finish(*cps)

    return pl.pallas_call(body, name=name, out_shape=ex.out_shape, in_specs=[_ANY] * ex.n, out_specs=[_ANY] * ex.n,
                          scratch_shapes=ex.scratch)(*arrs)


_HBM = pl.BlockSpec(memory_space=pltpu.HBM)
_SEM = pl.BlockSpec(memory_space=pltpu.SEMAPHORE)
_DATAFLOW = pltpu.SideEffectType.DATAFLOW_SIDE_EFFECTING


def _scatter_copies(ins, lands, send_sems, recv_sems):
    mx, my, mc = lax.axis_index("x"), lax.axis_index("y"), lax.axis_index("c")
    me = 2 * mx + my
    sends, recvs = [], []
    for j in range(len(ins)):
        for k in range(1, NCHIP):
            px, py = _flip(mx, k & 2), _flip(my, k & 1)
            peer = 2 * px + py
            sem = j * (NCHIP - 1) + k - 1
            mk = lambda slot: pltpu.make_async_remote_copy(src_ref=ins[j].at[peer], dst_ref=lands[j].at[slot], send_sem=send_sems.at[sem],
                                                           recv_sem=recv_sems.at[sem], device_id=(px, py, mc), device_id_type=_MESH)
            sends.append(mk(me))
            recvs.append(mk(peer))
    return sends, recvs


def _scatter_start(arrs, after, *, name):
    n = len(arrs)
    links = n * (NCHIP - 1)
    n_in = 2 * n + len(after)

    def body(*refs):
        ins, lands = refs[:n], refs[n:2 * n]
        send_sems, recv_sems = refs[n_in], refs[n_in + 1]
        token = refs[-1]
        for cp in _scatter_copies(ins, lands, send_sems, recv_sems)[0]:
            cp.start()
        token[...] = jnp.zeros_like(token)

    hbm = lambda a: pltpu.HBM(a.shape, a.dtype)
    res = pl.pallas_call(
        body, name=name,
        out_shape=(pltpu.SemaphoreType.DMA((links,)), pltpu.SemaphoreType.DMA((links,)), *[hbm(a) for a in arrs], *[hbm(a) for a in arrs],
                   jax.ShapeDtypeStruct((8, 128), F32)),
        in_specs=[_HBM] * (2 * n) + [_ANY] * len(after), out_specs=(_SEM, _SEM, *[_HBM] * (2 * n), pl.BlockSpec(memory_space=pltpu.VMEM)),
        input_output_aliases={j: 2 + j for j in range(2 * n)},
        compiler_params=pltpu.CompilerParams(has_side_effects=_DATAFLOW),
    )(*[pltpu.with_memory_space_constraint(a, pltpu.HBM) for a in arrs],
      *[pltpu.with_memory_space_constraint(lax.empty(a.shape, a.dtype), pltpu.HBM) for a in arrs], *after)
    return res[0], res[1], list(res[2:2 + n]), list(res[2 + n:2 + 2 * n]), res[-1]


def _scatter_wait(send_sems, recv_sems, arrs, lands, after, *, name):
    n = len(arrs)

    def body(*refs):
        ins, lands_in = refs[:n], refs[n:2 * n]
        sends, recvs = _scatter_copies(ins, lands_in, refs[2 * n], refs[2 * n + 1])
        for cp in sends:
            cp.wait_send()
        for cp in recvs:
            cp.wait_recv()

    hbm = lambda a: pltpu.HBM(a.shape, a.dtype)
    res = pl.pallas_call(
        body, name=name, out_shape=(*[hbm(a) for a in arrs], *[hbm(a) for a in lands]),
        in_specs=[_HBM] * (2 * n) + [_SEM, _SEM] + [_ANY] * len(after), out_specs=[_HBM] * (2 * n),
        input_output_aliases={j: j for j in range(2 * n)},
        compiler_params=pltpu.CompilerParams(has_side_effects=_DATAFLOW),
    )(*arrs, *lands, send_sems, recv_sems, *after)
    return list(res[:n]), list(res[n:])


def _pallas(body, operands, *, name, grid, in_specs, out_specs, out_shape, sem, scratch_shapes=(), aliases=None, exchange=None):
    if exchange is None:
        return pl.pallas_call(body, name=name, grid=grid, in_specs=in_specs, out_specs=out_specs, out_shape=out_shape,
                              scratch_shapes=list(scratch_shapes), input_output_aliases=aliases or {},
                              compiler_params=_cparams(*sem))(*operands)
    ex = exchange
    (steps,) = grid
    n_in, n_out, n_scr, k = len(in_specs), len(out_specs), len(scratch_shapes), ex.n

    def hosted(*refs):
        ins, refs = refs[:n_in], refs[n_in:]
        ex_in, refs = refs[:k], refs[k:]
        outs, refs = refs[:n_out], refs[n_out:]
        ex_out, refs = refs[:k], refs[k:]
        scr, ex_sems = refs[:n_scr], refs[n_scr:]
        cps = ex.copies(ex_in, ex_out, *ex_sems)
        pl.when(pl.program_id(0) == 0)(lambda: ex.start(*cps))
        body(*ins, *outs, *scr)
        pl.when(pl.program_id(0) == steps - 1)(lambda: ex.finish(*cps))

    return pl.pallas_call(
        hosted, name=name, grid=grid, in_specs=list(in_specs) + [_ANY] * k, out_specs=list(out_specs) + [_ANY] * k,
        out_shape=list(out_shape) + ex.out_shape, scratch_shapes=list(scratch_shapes) + ex.scratch,
        input_output_aliases=aliases or {}, compiler_params=_cparams("arbitrary"),
    )(*operands, *ex.arrs)


def _core_swap(arrs, *, name):
    n = len(arrs)

    def body(*refs):
        ins, outs = refs[:n], refs[n:2 * n]
        send_sems, recv_sems = refs[2 * n:]
        sib = (lax.axis_index("x"), lax.axis_index("y"), 1 - lax.axis_index("c"))
        cps = [pltpu.make_async_remote_copy(src_ref=ins[j], dst_ref=outs[j], send_sem=send_sems.at[j], recv_sem=recv_sems.at[j],
                                            device_id=sib, device_id_type=_MESH) for j in range(n)]
        for cp in cps:
            cp.start()
        for cp in cps:
            cp.wait_recv()
        for cp in cps:
            cp.wait_send()

    return pl.pallas_call(
        body, name=name, out_shape=[jax.ShapeDtypeStruct(a.shape, a.dtype) for a in arrs],
        in_specs=[_ANY] * n, out_specs=[_ANY] * n,
        scratch_shapes=[pltpu.SemaphoreType.DMA((n,)), pltpu.SemaphoreType.DMA((n,))],
    )(*arrs)


def _row_tile(rows, cols, budget=1 << 18):
    best = None
    for t in range(8, rows + 1, 8):
        if rows % t == 0 and t * cols <= budget:
            best = t
    return best or rows


def _sum_slots(x, *, name, after=()):
    ns, r, c = x.shape
    tr = _row_tile(r, c * ns)

    def body(x_ref, *refs):
        acc = x_ref[0].astype(F32)
        for s in range(1, ns):
            acc = acc + x_ref[s].astype(F32)
        refs[-1][...] = acc

    return pl.pallas_call(
        body, name=name, grid=(r // tr,), out_shape=jax.ShapeDtypeStruct((r, c), F32),
        in_specs=[pl.BlockSpec((ns, tr, c), lambda i: (0, i, 0))] + [_ANY] * len(after), out_specs=pl.BlockSpec((tr, c), lambda i: (i, 0)),
        compiler_params=_cparams("parallel"),
    )(x, *after)


def _adamw(w, ga, gb, m, v, *, name):
    r, c = w.shape
    tr = _row_tile(r, c, budget=1 << 17)
    two = gb is not None

    def body(*refs):
        w_ref, ga_ref = refs[0], refs[1]
        m_ref, v_ref = refs[2 + two], refs[3 + two]
        g_ref, d_ref, mo_ref, vo_ref = refs[4 + two:]
        g = ga_ref[...] + refs[2][...] if two else ga_ref[...]
        mn = ADAM_B1 * m_ref[...] + (1.0 - ADAM_B1) * g
        vn = ADAM_B2 * v_ref[...] + (1.0 - ADAM_B2) * (g * g)
        m_hat = mn / (1.0 - ADAM_B1 ** ADAM_STEP)
        v_hat = vn / (1.0 - ADAM_B2 ** ADAM_STEP)
        g_ref[...] = g
        d_ref[...] = -ADAM_LR * (m_hat / (jnp.sqrt(v_hat) + ADAM_EPS) + ADAM_WD * w_ref[...])
        mo_ref[...] = mn
        vo_ref[...] = vn

    spec = pl.BlockSpec((tr, c), lambda i: (i, 0))
    ins = [w, ga] + ([gb] if two else []) + [m, v]
    return pl.pallas_call(
        body, name=name, grid=(r // tr,), out_shape=[jax.ShapeDtypeStruct((r, c), F32)] * 4,
        in_specs=[spec] * len(ins), out_specs=[spec] * 4, compiler_params=_cparams("parallel"),
    )(*ins)


_MIX_AB0, _MIX_AB1 = NQKV, NQKV + 4 * NH


def _regroup_mix(w):
    pad = jnp.zeros((w.shape[0], NMIXP - NMIX), w.dtype)
    return jnp.concatenate([w[:, :_MIX_AB0], w[:, _MIX_AB1:], w[:, _MIX_AB0:_MIX_AB1], pad], axis=1)


def _ungroup_mix(w):
    n_ab = _MIX_AB1 - _MIX_AB0
    return jnp.concatenate([w[:, :_MIX_AB0], w[:, NMIX - n_ab:NMIX], w[:, _MIX_AB0:NMIX - n_ab]], axis=1)


def _chip_major_cols(w):
    r, c = w.shape
    return w.reshape(r, NCHIP, c // NCHIP).transpose(1, 0, 2)


def _from_chip_major_cols(w):
    return w.transpose(1, 0, 2).reshape(w.shape[1], -1)


_SMALL = (("c_ctx", D), ("b_ada", 9 * D), ("norm1_w", D), ("norm2_w", D), ("norm3_w", D), ("final_norm_w", D),
          ("a_log", 2 * NH), ("dt_bias", 2 * NH), ("gdn_norm_w", HD), ("pool_w", 4 * 128 * 128), ("pool_scale", NPOOL),
          ("conv_w", 5 * NQKV // NCHIP))


def _pack(vals, lanes=128, row_mult=8):
    flat = jnp.concatenate([jnp.ravel(v) for v in vals])
    n = flat.shape[0]
    rows = -(-n // (lanes * row_mult)) * row_mult
    return jnp.pad(flat, (0, rows * lanes - n)).reshape(rows, lanes)


def _unpack(packed, sizes):
    flat = packed.reshape(-1)
    out, o = [], 0
    for n in sizes:
        out.append(flat[o:o + n])
        o += n
    return out


def kernel(x, c, ctx, c_ctx, w_ada, b_ada, norm1_w, ffn1_w_in, ffn1_w_out, norm2_w, w_mix_in, conv_w, a_log, dt_bias, gdn_norm_w, w_gdn_proj, pool_w, pool_scale, w_pool_proj, w_mix_out, norm3_w, ffn2_w_in, ffn2_w_out, final_norm_w, loss_target, m_c_ctx, m_w_ada, m_b_ada, m_norm1_w, m_ffn1_w_in, m_ffn1_w_out, m_norm2_w, m_w_mix_in, m_conv_w, m_a_log, m_dt_bias, m_gdn_norm_w, m_w_gdn_proj, m_pool_w, m_pool_scale, m_w_pool_proj, m_w_mix_out, m_norm3_w, m_ffn2_w_in, m_ffn2_w_out, m_final_norm_w, v_c_ctx, v_w_ada, v_b_ada, v_norm1_w, v_ffn1_w_in, v_ffn1_w_out, v_norm2_w, v_w_mix_in, v_conv_w, v_a_log, v_dt_bias, v_gdn_norm_w, v_w_gdn_proj, v_pool_w, v_pool_scale, v_w_pool_proj, v_w_mix_out, v_norm3_w, v_ffn2_w_in, v_ffn2_w_out, v_final_norm_w):
    names = ("c_ctx", "w_ada", "b_ada", "norm1_w", "ffn1_w_in", "ffn1_w_out", "norm2_w", "w_mix_in", "conv_w", "a_log", "dt_bias",
             "gdn_norm_w", "w_gdn_proj", "pool_w", "pool_scale", "w_pool_proj", "w_mix_out", "norm3_w", "ffn2_w_in", "ffn2_w_out",
             "final_norm_w")
    w = dict(zip(names, (c_ctx, w_ada, b_ada, norm1_w, ffn1_w_in, ffn1_w_out, norm2_w, w_mix_in, conv_w, a_log, dt_bias, gdn_norm_w,
                         w_gdn_proj, pool_w, pool_scale, w_pool_proj, w_mix_out, norm3_w, ffn2_w_in, ffn2_w_out, final_norm_w)))
    mom = dict(zip(names, (m_c_ctx, m_w_ada, m_b_ada, m_norm1_w, m_ffn1_w_in, m_ffn1_w_out, m_norm2_w, m_w_mix_in, m_conv_w, m_a_log,
                           m_dt_bias, m_gdn_norm_w, m_w_gdn_proj, m_pool_w, m_pool_scale, m_w_pool_proj, m_w_mix_out, m_norm3_w,
                           m_ffn2_w_in, m_ffn2_w_out, m_final_norm_w)))
    var = dict(zip(names, (v_c_ctx, v_w_ada, v_b_ada, v_norm1_w, v_ffn1_w_in, v_ffn1_w_out, v_norm2_w, v_w_mix_in, v_conv_w, v_a_log,
                           v_dt_bias, v_gdn_norm_w, v_w_gdn_proj, v_pool_w, v_pool_scale, v_w_pool_proj, v_w_mix_out, v_norm3_w,
                           v_ffn2_w_in, v_ffn2_w_out, v_final_norm_w)))
    mx, my, mc = lax.axis_index("x"), lax.axis_index("y"), lax.axis_index("c")
    chip = 2 * mx + my
    dev = 2 * chip + mc
    ada_cols = w_ada.shape[2]

    c_rows = _all_gather8(jnp.pad(c, ((0, 7), (0, 0))), name="gather_c")[:, 0, :]
    c_all = jnp.concatenate([c_rows, c_ctx[None], jnp.zeros((7, D), F32)], axis=0)
    b_sh = lax.dynamic_slice(b_ada, (0, chip * ada_cols), (1, ada_cols))
    mod_sh = _ada_fwd(c_all, w_ada[0], b_sh, name="ada_fwd")
    mod_parts = _all_gather8(mod_sh, name="gather_mod")
    mod_all = jnp.concatenate([mod_parts[2 * s] for s in range(NCHIP)], axis=1)
    mod_lat = lax.dynamic_index_in_dim(mod_all, dev, axis=0, keepdims=False).reshape(9, D)
    modv = jnp.zeros((2, 16, D), F32).at[0, :9].set(mod_all[8].reshape(9, D)).at[1, :9].set(mod_lat)

    big = ("ffn1_w_in", "ffn1_w_out", "w_mix_in", "w_gdn_proj", "w_pool_proj", "w_mix_out", "ffn2_w_in", "ffn2_w_out")
    shard = {k: w[k][0].astype(BF) for k in big}
    w1_in, w1_out = _chip_exchange([shard["ffn1_w_in"], shard["ffn1_w_out"]], scatter=False, name="gather_ffn1")
    p = dict(
        norm1=norm1_w, norm2=norm2_w, norm3=norm3_w, fnorm=final_norm_w[None], w1_in=w1_in, w1_out=w1_out.reshape(FF, D),
        cst=jnp.zeros((8, 128), F32).at[0, 2 * NH:4 * NH].set(jnp.exp(a_log).reshape(-1)).at[1, 2 * NH:4 * NH].set(dt_bias.reshape(-1)),
        gnw=gdn_norm_w, pool_w=pool_w[0], pscale=pool_scale)
    late = ([shard["w_mix_in"], shard["w_gdn_proj"], shard["w_pool_proj"], shard["w_mix_out"], conv_w[0]],
            [shard["ffn2_w_in"], shard["ffn2_w_out"]])

    xc = jnp.concatenate([ctx[0], x[0]], axis=0)
    loss_dev, dx_lat, g, landed, small, dmod = _local_step(xc, loss_target[0], modv, p, late)
    loss = lax.psum(loss_dev, ("x", "y", "c"))
    grad_x = dx_lat[None]

    small_vals = [dmod[1], dmod[0], small["norm1"], small["norm2"], small["norm3"], small["fnorm"], small["a_log"], small["dt_bias"],
                  small["gnw"], small["pool_w"], small["pscale"], small["conv"]]
    small_sizes = [v.size for v in small_vals]
    packed = _all_gather8(_pack(small_vals), name="gather_small")
    tot = _unpack(_sum_slots(packed, name="sum_small"), small_sizes)
    dmod_lat_all = packed[:, :9 * D // 128, :].reshape(8, 9 * D)
    dm = jnp.concatenate([dmod_lat_all, tot[1][None], jnp.zeros((7, 9 * D), F32)], axis=0)
    dm_sh = lax.dynamic_slice(dm, (0, chip * ada_cols), (16, ada_cols))
    g_w_ada, cctx_part = _ada_bwd(c_all, dm_sh, w_ada[0], name="ada_bwd")
    g_c_ctx = _cctx_grad(_all_gather8(cctx_part, name="gather_cctx"), c_ctx[None], name="cctx_grad")[0]
    conv_tot = tot[11].reshape(5, NQKV)
    g_small = dict(c_ctx=g_c_ctx, b_ada=tot[0] + tot[1], norm1_w=tot[2], norm2_w=tot[3], norm3_w=tot[4], final_norm_w=tot[5],
                   a_log=tot[6], dt_bias=tot[7], gdn_norm_w=tot[8], pool_w=tot[9], pool_scale=tot[10],
                   conv_w=lax.dynamic_slice(conv_tot, (0, chip * (NQKV // NCHIP)), (5, NQKV // NCHIP)))

    first = ("ffn1_w_in", "ffn1_w_out")
    order = dict(zip(big, ("w1_in", "w1_out", "w_mix", "w_gdn", "w_pool", "w_mo", "w2_in", "w2_out")))
    rest = [k for k in big if k not in first]
    send_sems, recv_sems, sent, lands, token = _scatter_start([g["w1_in"], g["w1_out"]], [g_c_ctx, g_w_ada], name="scatter_ffn1_start")
    mine = {k: _sum_slots(landed[order[k]], name=f"sum_{k}", after=[token]) for k in rest}
    theirs = dict(zip(rest, _core_swap([mine[k] for k in rest], name="swap_grad_sums")))

    out = {}
    as2d = lambda a: a.reshape(-1, a.shape[-1])

    def update(k):
        res = _adamw(as2d(w[k]), as2d(mine[k]), as2d(theirs[k]), as2d(mom[k]), as2d(var[k]), name=f"adamw_{k}")
        out[k] = [r.reshape(w[k].shape) for r in res]

    for k in rest:
        update(k)
    out["w_ada"] = [r.reshape(w_ada.shape) for r in _adamw(w_ada[0], g_w_ada, None, m_w_ada[0], v_w_ada[0], name="adamw_w_ada")]
    sm_names = [n for n, _ in _SMALL]
    sm_sizes = [n for _, n in _SMALL]
    res = _adamw(_pack([w[k] for k in sm_names]), _pack([g_small[k] for k in sm_names]), None,
                 _pack([mom[k] for k in sm_names]), _pack([var[k] for k in sm_names]), name="adamw_small")
    done = [out[k][1] for k in rest] + [out["w_ada"][1], res[1]]
    res = [_unpack(r, sm_sizes) for r in res]
    for i, k in enumerate(sm_names):
        out[k] = [r[i].reshape(w[k].shape) for r in res]
    sent, lands = _scatter_wait(send_sems, recv_sems, sent, lands, done, name="scatter_ffn1_wait")
    for k, part, land in zip(first, sent, lands):
        own = lax.dynamic_slice_in_dim(part, chip, 1, axis=0)
        mine[k] = _sum_slots(lax.dynamic_update_slice_in_dim(land, own, chip, axis=0), name=f"sum_{k}")
    theirs.update(zip(first, _core_swap([mine[k] for k in first], name="swap_ffn1_sums")))
    for k in first:
        update(k)
    return (loss, grad_x, *[out[k][0] for k in names], *[out[k][1] for k in names], *[out[k][2] for k in names],
            *[out[k][3] for k in names])
```

```python
import functools

import jax
import jax.numpy as jnp
from jax import lax
from jax.experimental import pallas as pl
from jax.experimental.pallas import tpu as pltpu

F32 = jnp.float32
BF = jnp.bfloat16

D = 1024
FF = 2816
NH = 8
HD = 128
CH = 64
GW = 64
TM = 256
NQKV = 3 * NH * HD
NPOOL = 512
POOL_WINDOWS = (2, 4, 8, 16)
NMIX = 6688
NMIXP = 6784
EPS = 1e-6
NCHIP = 4
VMEM_LIMIT = 56 * 1024 * 1024

ADAM_LR, ADAM_B1, ADAM_B2, ADAM_EPS, ADAM_WD, ADAM_STEP = 0.001, 0.9, 0.999, 1e-08, 0.01, 10


def _cparams(*sem):
    return pltpu.CompilerParams(dimension_semantics=sem, vmem_limit_bytes=VMEM_LIMIT)


def _const_spec(shape):
    nd = len(shape)
    return pl.BlockSpec(shape, lambda *_: (0,) * nd, pipeline_mode=pl.Buffered(1))


def _dot(a, b, dims):
    return lax.dot_general(a.astype(BF), b.astype(BF), (dims, ((), ())), preferred_element_type=F32)


def _nn(a, b):
    return _dot(a, b, ((1,), (0,)))


def _nt(a, b):
    return _dot(a, b, ((1,), (1,)))


def _tn(a, b):
    return _dot(a, b, ((0,), (0,)))


def _silu(x):
    return x * jax.nn.sigmoid(x)


def _dsilu(x):
    s = jax.nn.sigmoid(x)
    return s * (1.0 + x * (1.0 - s))


def _norm_mod(x, nw, shift, scale):
    r = lax.rsqrt(jnp.mean(x * x, axis=-1, keepdims=True) + EPS)
    xh = x * r
    n = xh * nw
    return n * (1.0 + scale) + shift, n, xh, r


def _norm_mod_bwd(dh, n, xh, r, nw, scale):
    dn = dh * (1.0 + scale)
    dxh = dn * nw
    dx = r * (dxh - xh * jnp.mean(dxh * xh, axis=-1, keepdims=True))
    rs = lambda t: jnp.sum(t, axis=0, keepdims=True)
    return dx, rs(dh), rs(dh * n), rs(dn * xh)


def _ffn_fwd(x, modv, nw, w_in4, w_out, *, mrow, name, exchange=None):
    n_tok = x.shape[0]
    nt = n_tok // TM
    nset = modv.shape[0]
    ws = w_in4.shape[2]

    def body(x_ref, mod_ref, nw_ref, win_ref, wout_ref, x1_ref, h_ref, gu_ref, f_ref):
        xv = x_ref[...]
        shift, scale, gate = mod_ref[0, mrow:mrow + 1, :], mod_ref[0, mrow + 1:mrow + 2, :], mod_ref[0, mrow + 2:mrow + 3, :]
        h, _, _, _ = _norm_mod(xv, nw_ref[...], shift, scale)
        hb = h.astype(BF)
        h_ref[...] = hb
        gus = [_nn(hb, win_ref[s]) for s in range(NCHIP)]
        for s in range(NCHIP):
            gu_ref[:, s * ws:(s + 1) * ws] = gus[s].astype(BF)
        g = jnp.concatenate(gus[:2], axis=1)
        u = jnp.concatenate(gus[2:], axis=1)
        f = _nn(_silu(g) * u, wout_ref[...])
        f_ref[...] = f.astype(BF)
        x1_ref[...] = xv + 0.5 * gate * f

    tile = lambda w: pl.BlockSpec((TM, w), lambda i: (i, 0))
    return _pallas(
        body, (x, modv, nw, w_in4, w_out), name=name, grid=(nt,), sem=("parallel",), exchange=exchange,
        in_specs=[tile(D), pl.BlockSpec((1, 16, D), lambda i: (jnp.minimum(i, nset - 1), 0, 0)), _const_spec((1, D)),
                  _const_spec(w_in4.shape), _const_spec(w_out.shape)],
        out_specs=[tile(D), tile(D), tile(2 * FF), tile(D)],
        out_shape=[jax.ShapeDtypeStruct((n_tok, D), F32), jax.ShapeDtypeStruct((n_tok, D), BF),
                   jax.ShapeDtypeStruct((n_tok, 2 * FF), BF), jax.ShapeDtypeStruct((n_tok, D), BF)])


def _ffn_bwd(dxo, x, gu, fo, modv, nw, w_in4, w_out, *, mrow, dx_skip, name, exchange=None):
    n_tok = x.shape[0]
    nt = n_tok // TM
    nset = modv.shape[0]
    ws = w_in4.shape[2]

    def body(dxo_ref, x_ref, gu_ref, f_ref, mod_ref, nw_ref, win_ref, wout_ref, dx_ref, a_ref, df_ref, dgu_ref, acc_ref):
        i = pl.program_id(0)
        xv = x_ref[...]
        dxo_v = dxo_ref[...]
        shift, scale, gate = mod_ref[0, mrow:mrow + 1, :], mod_ref[0, mrow + 1:mrow + 2, :], mod_ref[0, mrow + 2:mrow + 3, :]
        _, n, xh, r = _norm_mod(xv, nw_ref[...], shift, scale)
        df = 0.5 * gate * dxo_v
        dfb = df.astype(BF)
        df_ref[...] = dfb
        dgate = jnp.sum(0.5 * dxo_v * f_ref[...].astype(F32), axis=0, keepdims=True)
        da = _nt(dfb, wout_ref[...])
        g = gu_ref[:, :FF].astype(F32)
        u = gu_ref[:, FF:].astype(F32)
        sg = _silu(g)
        a_ref[...] = (sg * u).astype(BF)
        dgu_ref[:, :FF] = (da * u * _dsilu(g)).astype(BF)
        dgu_ref[:, FF:] = (da * sg).astype(BF)
        dh = _nt(dgu_ref[:, 0:ws], win_ref[0])
        for s in range(1, NCHIP):
            dh = dh + _nt(dgu_ref[:, s * ws:(s + 1) * ws], win_ref[s])
        dx, dshift, dscale, dnw = _norm_mod_bwd(dh, n, xh, r, nw_ref[...], scale)
        dx_ref[...] = dxo_v + dx

        @pl.when((i == 0) | (i == nset - 1))
        def _():
            acc_ref[...] = jnp.zeros_like(acc_ref)

        acc_ref[0, 0:1, :] += dshift
        acc_ref[0, 1:2, :] += dscale
        acc_ref[0, 2:3, :] += dgate
        acc_ref[0, 3:4, :] += dnw

    tile = lambda w: pl.BlockSpec((TM, w), lambda i: (i, 0))
    return _pallas(
        body, (dxo, x, gu, fo, modv, nw, w_in4, w_out), name=name, grid=(nt,), sem=("arbitrary",), exchange=exchange,
        in_specs=[tile(D), tile(D), tile(2 * FF), tile(D),
                  pl.BlockSpec((1, 16, D), lambda i: (jnp.minimum(i, nset - 1), 0, 0)), _const_spec((1, D)),
                  _const_spec(w_in4.shape), _const_spec(w_out.shape)],
        out_specs=[pl.BlockSpec((TM, D), lambda i: (jnp.maximum(i - dx_skip, 0), 0)), tile(FF), tile(D), tile(2 * FF),
                   pl.BlockSpec((1, 8, D), lambda i: (jnp.minimum(i, nset - 1), 0, 0))],
        out_shape=[jax.ShapeDtypeStruct((n_tok - dx_skip * TM, D), F32), jax.ShapeDtypeStruct((n_tok, FF), BF),
                   jax.ShapeDtypeStruct((n_tok, D), BF), jax.ShapeDtypeStruct((n_tok, 2 * FF), BF),
                   jax.ShapeDtypeStruct((nset, 8, D), F32)])


def _k_tile(n, target=3072):
    return max(t for t in range(TM, min(n, target) + 1, TM) if n % t == 0)


def _matmul_tn(a, b, *, tmm, tn, tk, nsplit=1, name):
    n_tok, m = a.shape
    kk = b.shape[1]
    nk = n_tok // tk

    def body(a_ref, b_ref, o_ref, acc):
        k = pl.program_id(2)

        @pl.when(k == 0)
        def _():
            acc[...] = jnp.zeros_like(acc)

        acc[...] += _tn(a_ref[...], b_ref[...])

        @pl.when(k == nk - 1)
        def _():
            o_ref[...] = acc[...].astype(BF).reshape(o_ref.shape)

    if nsplit == 1:
        out_shape = jax.ShapeDtypeStruct((m, kk), BF)
        out_spec = pl.BlockSpec((tmm, tn), lambda i, j, k: (i, j))
    else:
        assert tn == kk // nsplit
        out_shape = jax.ShapeDtypeStruct((nsplit, m, tn), BF)
        out_spec = pl.BlockSpec((1, tmm, tn), lambda i, j, k: (j, i, 0))
    return pl.pallas_call(
        body, name=name, grid=(m // tmm, kk // tn, nk),
        in_specs=[pl.BlockSpec((tk, tmm), lambda i, j, k: (k, i)), pl.BlockSpec((tk, tn), lambda i, j, k: (k, j))],
        out_specs=out_spec, out_shape=out_shape,
        scratch_shapes=[pltpu.VMEM((tmm, tn), F32)],
        compiler_params=_cparams("parallel", "parallel", "arbitrary"),
    )(a, b)


_MIX_PARTS = (("qkv", 0, NQKV), ("gate", NQKV, 1024), ("pool", NQKV + 1024, NPOOL), ("br", NQKV + 1024 + NPOOL, 2048),
              ("ab", NMIXP - 128, 128))


def _mix_in_fwd(x1, modv, nw, w_mix, *, name, exchange=None):
    n_tok = x1.shape[0]

    def body(x_ref, mod_ref, nw_ref, w_ref, u_ref, *p_refs):
        u, _, _, _ = _norm_mod(x_ref[...], nw_ref[...], mod_ref[0, 3:4, :], mod_ref[0, 4:5, :])
        ub = u.astype(BF)
        u_ref[...] = ub
        for (_, c0, w), p_ref in zip(_MIX_PARTS, p_refs):
            p_ref[...] = _nn(ub, w_ref[:, c0:c0 + w])

    tile = lambda w: pl.BlockSpec((TM, w), lambda i: (i, 0))
    ctile = lambda w: pl.BlockSpec((TM, w), lambda i: (i + 1, 0))
    return _pallas(
        body, (x1, modv, nw, w_mix), name=name, grid=(n_tok // TM,), sem=("parallel",), exchange=exchange,
        in_specs=[tile(D), pl.BlockSpec((1, 16, D), lambda i: (jnp.minimum(i, 1), 0, 0)), _const_spec((1, D)),
                  _const_spec(w_mix.shape)],
        out_specs=[tile(D)] + [tile(w) for _, _, w in _MIX_PARTS],
        out_shape=[jax.ShapeDtypeStruct((n_tok, D), BF)] + [jax.ShapeDtypeStruct((n_tok, w), F32) for _, _, w in _MIX_PARTS])


def _mix_in_bwd(dxo, x1, dqkv, dgate, dpool, dbr, dab_f, dab_b, modv, nw, w_mix, *, name):
    n_tok = x1.shape[0]

    def body(dxo_ref, x_ref, dqkv_ref, dgate_ref, dpool_ref, dbr_ref, dabf_ref, dabb_ref, mod_ref, nw_ref, w_ref,
             dx_ref, dp_ref, acc_ref):
        i = pl.program_id(0)
        lat = i >= 1
        scale = mod_ref[0, 4:5, :]
        _, n, xh, r = _norm_mod(x_ref[...], nw_ref[...], mod_ref[0, 3:4, :], scale)
        dp_ref[:, 0:NQKV] = dqkv_ref[...].astype(BF)
        dp_ref[:, NQKV:NQKV + 1024] = jnp.where(lat, dgate_ref[...], 0.0).astype(BF)
        dp_ref[:, NQKV + 1024:NQKV + 1536] = jnp.where(lat, dpool_ref[...], 0.0).astype(BF)
        dp_ref[:, NQKV + 1536:NMIXP - 128] = jnp.where(lat, dbr_ref[...], 0.0).astype(BF)
        dp_ref[:, NMIXP - 128:] = (dabf_ref[...] + dabb_ref[...]).astype(BF)
        du = _nt(dp_ref[...], w_ref[...])
        dx, dshift, dscale, dnw = _norm_mod_bwd(du, n, xh, r, nw_ref[...], scale)
        dx_ref[...] = jnp.where(lat, dxo_ref[...], 0.0) + dx

        @pl.when(i <= 1)
        def _():
            acc_ref[...] = jnp.zeros_like(acc_ref)

        acc_ref[0, 0:1, :] += dshift
        acc_ref[0, 1:2, :] += dscale
        acc_ref[0, 3:4, :] += dnw

    tile = lambda w: pl.BlockSpec((TM, w), lambda i: (i, 0))
    ltile = lambda w: pl.BlockSpec((TM, w), lambda i: (jnp.maximum(i - 1, 0), 0))
    return pl.pallas_call(
        body, name=name, grid=(n_tok // TM,),
        in_specs=[ltile(D), tile(D), tile(NQKV), ltile(1024), ltile(NPOOL), ltile(2048), tile(128), tile(128),
                  pl.BlockSpec((1, 16, D), lambda i: (jnp.minimum(i, 1), 0, 0)), _const_spec((1, D)), _const_spec(w_mix.shape)],
        out_specs=[tile(D), tile(NMIXP), pl.BlockSpec((1, 8, D), lambda i: (jnp.minimum(i, 1), 0, 0))],
        out_shape=[jax.ShapeDtypeStruct((n_tok, D), F32), jax.ShapeDtypeStruct((n_tok, NMIXP), BF),
                   jax.ShapeDtypeStruct((2, 8, D), F32)],
        compiler_params=_cparams("arbitrary"),
    )(dxo, x1, dqkv, dgate, dpool, dbr, dab_f, dab_b, modv, nw, w_mix)


def _qkv_act(pre, j):
    s = _silu(pre)
    nrm = s * lax.rsqrt(jnp.sum(s * s, axis=-1, keepdims=True) + EPS)
    nrm = nrm * jnp.where(j == 0, HD ** -0.5, 1.0)
    return jnp.where(j < 2, nrm, s)


def _halo_specs(nt):
    r = TM // 8
    w = NH * HD
    main = pl.BlockSpec((TM, w), lambda j, i: (i, j))
    prev = pl.BlockSpec((8, w), lambda j, i: (jnp.maximum(i * r - 1, 0), j))
    nxt = pl.BlockSpec((8, w), lambda j, i: (jnp.minimum((i + 1) * r, nt * r - 1), j))
    return main, prev, nxt


def _prep_fwd(p_qkv, conv_w8, *, name):
    n_tok = p_qkv.shape[0]
    nt = n_tok // TM

    def body(x_ref, xp_ref, xn_ref, w_ref, o_ref, pre_ref, win):
        j, i = pl.program_id(0), pl.program_id(1)
        has_prev = (i != 0) & (i != 1)
        has_next = (i != 0) & (i != nt - 1)
        win[0:8, :] = jnp.where(has_prev, xp_ref[...], 0.0)
        win[8:8 + TM, :] = x_ref[...]
        win[8 + TM:, :] = jnp.where(has_next, xn_ref[...], 0.0)
        for h in range(NH):
            hs = slice(h * HD, (h + 1) * HD)
            pre = win[6:6 + TM, hs] * w_ref[0:1, hs]
            for k in range(1, 5):
                pre = pre + win[6 + k:6 + k + TM, hs] * w_ref[k:k + 1, hs]
            pre_ref[:, hs] = pre
            o_ref[:, hs] = _qkv_act(pre, j)

    main, prev, nxt = _halo_specs(nt)
    wq = NH * HD
    return pl.pallas_call(
        body, name=name, grid=(3, nt),
        in_specs=[main, prev, nxt, pl.BlockSpec((8, wq), lambda j, i: (0, j))],
        out_specs=[main, main], out_shape=[jax.ShapeDtypeStruct((n_tok, NQKV), F32)] * 2,
        scratch_shapes=[pltpu.VMEM((TM + 16, wq), F32)],
        compiler_params=_cparams("parallel", "arbitrary"),
    )(p_qkv, p_qkv, p_qkv, conv_w8)


def _prep_bwd(p_qkv, pre, dqkv_f, dqkv_b, conv_w8, *, name):
    n_tok = p_qkv.shape[0]
    nt = n_tok // TM
    wq = NH * HD

    def body(x_ref, p_ref, pp_ref, pn_ref, g_ref, gp_ref, gn_ref, g2_ref, g2p_ref, g2n_ref, w_ref, dx_ref, dw_ref, pwin, gwin, dwin):
        j, i = pl.program_id(0), pl.program_id(1)
        has_prev = (i != 0) & (i != 1)
        has_next = (i != 0) & (i != nt - 1)
        pwin[0:8, :] = jnp.where(has_prev, pp_ref[...], 0.0)
        pwin[8:8 + TM, :] = p_ref[...]
        pwin[8 + TM:, :] = jnp.where(has_next, pn_ref[...], 0.0)
        gwin[0:8, :] = jnp.where(has_prev, gp_ref[...] + g2p_ref[...], 0.0)
        gwin[8:8 + TM, :] = g_ref[...] + g2_ref[...]
        gwin[8 + TM:, :] = jnp.where(has_next, gn_ref[...] + g2n_ref[...], 0.0)

        @pl.when(i == 0)
        def _():
            dw_ref[...] = jnp.zeros_like(dw_ref)

        for h in range(NH):
            hs = slice(h * HD, (h + 1) * HD)
            _, vjp = jax.vjp(lambda t: _qkv_act(t, j), pwin[:, hs])
            dwin[:, hs] = vjp(gwin[:, hs])[0]
            xv = x_ref[:, hs]
            dx = None
            for k in range(5):
                sh = dwin[10 - k:10 - k + TM, hs]
                dx = sh * w_ref[k:k + 1, hs] if dx is None else dx + sh * w_ref[k:k + 1, hs]
                dw_ref[k:k + 1, hs] += jnp.sum(sh * xv, axis=0, keepdims=True)
            dx_ref[:, hs] = dx

    main, prev, nxt = _halo_specs(nt)
    wspec = pl.BlockSpec((8, wq), lambda j, i: (0, j))
    return pl.pallas_call(
        body, name=name, grid=(3, nt),
        in_specs=[main, main, prev, nxt, main, prev, nxt, main, prev, nxt, wspec],
        out_specs=[main, wspec],
        out_shape=[jax.ShapeDtypeStruct((n_tok, NQKV), F32), jax.ShapeDtypeStruct((8, NQKV), F32)],
        scratch_shapes=[pltpu.VMEM((TM + 16, wq), F32)] * 3,
        compiler_params=_cparams("parallel", "arbitrary"),
    )(p_qkv, pre, pre, pre, dqkv_f, dqkv_f, dqkv_f, dqkv_b, dqkv_b, dqkv_b, conv_w8)


@jax.custom_vjp
def _mm_nn(a, b):
    return _nn(a, b)


@jax.custom_vjp
def _mm_nt(a, b):
    return _nt(a, b)


@jax.custom_vjp
def _mm_tn(a, b):
    return _tn(a, b)


_mm_nn.defvjp(lambda a, b: (_nn(a, b), (a, b)), lambda r, g: (_mm_nt(g, r[1]), _mm_tn(r[0], g)))
_mm_nt.defvjp(lambda a, b: (_nt(a, b), (a, b)), lambda r, g: (_mm_nn(g, r[1]), _mm_tn(g, r[0])))
_mm_tn.defvjp(lambda a, b: (_tn(a, b), (a, b)), lambda r, g: (_mm_nt(r[1], g), _mm_nn(r[0], g)))


def _each(f, *lists):
    return tuple(f(*a) for a in zip(*lists))


def _unit_tri_inv(ls, revs):
    ii = lax.broadcasted_iota(jnp.int32, (CH, CH), 0)
    jj = lax.broadcasted_iota(jnp.int32, (CH, CH), 1)
    eye = (ii == jj).astype(F32)
    xs = None
    s = 1
    while s < CH:
        same = (ii & -(2 * s)) == (jj & -(2 * s))
        off = {False: same & ((ii & s) != 0) & ((jj & s) == 0), True: same & ((jj & s) != 0) & ((ii & s) == 0)}
        cs = _each(lambda l, r: jnp.where(off[r], l, 0.0), ls, revs)
        if xs is None:
            xs = _each(lambda c: eye - c, cs)
        else:
            xc = _each(_nn, xs, cs)
            xcx = _each(_nn, xc, xs)
            xs = _each(lambda x, t: x - t, xs, xcx)
        s *= 2
    return xs


@functools.lru_cache(maxsize=None)
def _tri_solve(revs):
    @jax.custom_vjp
    def solve(ls, rhss):
        return _each(_mm_nn, _unit_tri_inv(ls, revs), rhss)

    def fwd(ls, rhss):
        ainv = _unit_tri_inv(ls, revs)
        xs = _each(_mm_nn, ainv, rhss)
        return xs, (ainv, xs)

    def bwd(res, gs):
        ainv, xs = res
        drhs = _each(_mm_tn, ainv, gs)
        return _each(lambda d, x: -_mm_nt(d, x), drhs, xs), drhs

    solve.defvjp(fwd, bwd)
    return solve


def _chunk_prep(q, k, v, beta, g, *, revs):
    ii = lax.broadcasted_iota(jnp.int32, (CH, CH), 0)
    jj = lax.broadcasted_iota(jnp.int32, (CH, CH), 1)
    eye = ii == jj
    incl_of = {False: ii >= jj, True: ii <= jj}
    strict_of = {False: ii > jj, True: ii < jj}
    g_row = _each(lambda t: jnp.sum(jnp.where(eye, t, 0.0), axis=0, keepdims=True), g)
    cum = _each(lambda t, r: jnp.sum(jnp.where(incl_of[r], t, 0.0), axis=1, keepdims=True), g_row, revs)
    cum_row = _each(lambda t: jnp.sum(jnp.where(eye, t, 0.0), axis=0, keepdims=True), cum)
    total = _each(lambda t: jnp.sum(t, axis=0, keepdims=True), g)
    decay = _each(lambda c, cr, r: jnp.where(incl_of[r], jnp.exp(jnp.where(incl_of[r], c - cr, 0.0)), 0.0), cum, cum_row, revs)
    kb = _each(jnp.multiply, k, beta)
    vb = _each(jnp.multiply, v, beta)
    kk = _each(_mm_nt, kb, k)
    lmat = _each(lambda t, dc, r: jnp.where(strict_of[r], t * dc, 0.0), kk, decay, revs)
    ecum = _each(jnp.exp, cum)
    rhs = _each(lambda a, b, e: jnp.concatenate([a, b * e], axis=1), vb, kb, ecum)
    sol = _tri_solve(revs)(lmat, rhs)
    qk = _each(_mm_nt, q, k)
    aqk = _each(jnp.multiply, qk, decay)
    qd = _each(jnp.multiply, q, ecum)
    kd = _each(lambda a, t, c: a * jnp.exp(t - c), k, total, cum)
    return sol, aqk, qd, kd, _each(jnp.exp, total)


def _chunk_rec(sol, aqk, qd, kd, bl, s):
    ws = _each(lambda so, st: _mm_nn(so[:, HD:], st), sol, s)
    v_new = _each(lambda so, t: so[:, :HD] - t, sol, ws)
    qs = _each(_mm_nn, qd, s)
    av = _each(_mm_nn, aqk, v_new)
    o = _each(jnp.add, qs, av)
    kv = _each(_mm_tn, kd, v_new)
    s_new = _each(lambda st, b, u: st * b + u, s, bl, kv)
    return o, s_new


def _lane_col(x, c):
    lane = lax.broadcasted_iota(jnp.int32, x.shape, 1)
    return jnp.sum(jnp.where(lane == c, x, 0.0), axis=1, keepdims=True)


def _beta_g(ab, cst, d, h):
    braw = _lane_col(ab, NH * d + h)
    araw = _lane_col(ab, 2 * NH + NH * d + h)
    ea = _lane_col(cst[0:1, :], 2 * NH + NH * d + h)
    dt = _lane_col(cst[1:2, :], 2 * NH + NH * d + h)
    z = araw + dt
    softplus = jnp.maximum(z, 0.0) + jnp.log(1.0 + jnp.exp(-jnp.abs(z)))
    return jax.nn.sigmoid(braw), -ea * softplus, z, ea


STEPS = 2
TS = STEPS * CH
_CHAINS = tuple((t, d, h) for t in range(STEPS) for d in (0, 1) for h in range(NH))
_REVS = tuple(bool(d) for _, d, _ in _CHAINS)
_PER_STEP = 2 * NH


def _chain_inputs(refs, r0s, ab_refs, cst):
    hs = lambda h: slice(h * HD, (h + 1) * HD)
    abvs = [[ab_refs[d][pl.ds(r0s[t][d], CH), :] for d in (0, 1)] for t in range(STEPS)]
    q = _each(lambda c: refs[c[1]][0][pl.ds(r0s[c[0]][c[1]], CH), hs(c[2])], _CHAINS)
    k = _each(lambda c: refs[c[1]][1][pl.ds(r0s[c[0]][c[1]], CH), hs(c[2])], _CHAINS)
    v = _each(lambda c: refs[c[1]][2][pl.ds(r0s[c[0]][c[1]], CH), hs(c[2])], _CHAINS)
    bg = _each(lambda c: _beta_g(abvs[c[0]][c[1]], cst, c[1], c[2]), _CHAINS)
    return q, k, v, bg


def _of_step(parts, t):
    return tuple(p[t * _PER_STEP:(t + 1) * _PER_STEP] for p in parts)


def _scan_fwd(qkv, ab, cst, s0, *, row_blk0, nb, name, exchange=None):
    cb = TS // CH
    w = NH * HD

    def body(qf, kf, vf, abf, qb, kb, vb, abb, cst_ref, s0_ref, of_ref, ob_ref, sallf_ref, sallb_ref, sfin_ref, s_scr):
        i = pl.program_id(0)

        @pl.when(i == 0)
        def _():
            s_scr[...] = s0_ref[...]

        o_refs, sall_refs = (of_ref, ob_ref), (sallf_ref, sallb_ref)

        def chunks(ci, carry):
            cs = [(ci * STEPS + t, cb - 1 - ci * STEPS - t) for t in range(STEPS)]
            r0s = [tuple(pl.multiple_of(c * CH, CH) for c in ct) for ct in cs]
            q, k, v, bg = _chain_inputs(((qf, kf, vf), (qb, kb, vb)), r0s, (abf, abb), cst_ref[...])
            parts = _chunk_prep(q, k, v, _each(lambda t: t[0], bg), _each(lambda t: t[1], bg), revs=_REVS)
            s = _each(lambda c: s_scr[c[1], c[2]], _CHAINS[:_PER_STEP])
            for t in range(STEPS):
                for (_, d, h), sv in zip(_CHAINS, s):
                    sall_refs[d][cs[t][d], h] = sv
                o, s = _chunk_rec(*_of_step(parts, t), s)
                for (_, d, h), ov in zip(_CHAINS, o):
                    o_refs[d][pl.ds(r0s[t][d], CH), h * HD:(h + 1) * HD] = ov
            for (_, d, h), sv in zip(_CHAINS, s):
                s_scr[d, h] = sv
            return carry

        lax.fori_loop(0, cb // STEPS, chunks, 0)

        @pl.when(i == nb - 1)
        def _():
            sfin_ref[...] = s_scr[...]

    pos = (lambda i: i, lambda i: nb - 1 - i)
    col = lambda d, c: pl.BlockSpec((TS, w), lambda i: (row_blk0 + pos[d](i), c))
    abs_ = lambda d: pl.BlockSpec((TS, 128), lambda i: (row_blk0 + pos[d](i), 0))
    full4 = pl.BlockSpec((2, NH, HD, HD), lambda i: (0, 0, 0, 0))
    o_spec = lambda d: pl.BlockSpec((TS, w), lambda i: (pos[d](i), 0))
    sall_spec = lambda d: pl.BlockSpec((cb, NH, HD, HD), lambda i: (pos[d](i), 0, 0, 0))
    return _pallas(
        body, (qkv, qkv, qkv, ab, qkv, qkv, qkv, ab, cst, s0), name=name, grid=(nb,), sem=("arbitrary",), exchange=exchange,
        in_specs=[col(0, 0), col(0, 1), col(0, 2), abs_(0), col(1, 0), col(1, 1), col(1, 2), abs_(1),
                  pl.BlockSpec((8, 128), lambda i: (0, 0)), full4],
        out_specs=[o_spec(0), o_spec(1), sall_spec(0), sall_spec(1), full4],
        out_shape=[jax.ShapeDtypeStruct((nb * TS, w), F32)] * 2 + [jax.ShapeDtypeStruct((nb * cb, NH, HD, HD), F32)] * 2
        + [jax.ShapeDtypeStruct((2, NH, HD, HD), F32)],
        scratch_shapes=[pltpu.VMEM((2, NH, HD, HD), F32)])


def _scan_bwd(qkv, ab, cst, sall_f, sall_b, do, dsfin, dqkv_f, dqkv_b, dab_f, dab_b, dcst, *, row_blk0, nb, has_do, name,
              exchange=None):
    cb = TS // CH
    w = NH * HD

    def body(qf, kf, vf, abf, qb, kb, vb, abb, cst_ref, sallf_ref, sallb_ref, dof_ref, dob_ref, dsfin_ref, _f, _b, _af, _ab, dcst_in,
             dqkvf_ref, dqkvb_ref, dabf_ref, dabb_ref, dcst_ref, ds0_ref, ds_scr):
        i = pl.program_id(0)

        @pl.when(i == 0)
        def _():
            ds_scr[...] = dsfin_ref[...]
            dcst_ref[...] = dcst_in[...]

        lane = lax.broadcasted_iota(jnp.int32, (CH, 128), 1)
        lane1 = lax.broadcasted_iota(jnp.int32, (1, 128), 1)
        sall_refs, do_refs = (sallf_ref, sallb_ref), (dof_ref, dob_ref)
        dqkv_refs, dab_refs = (dqkvf_ref, dqkvb_ref), (dabf_ref, dabb_ref)

        def chunks(ci, carry):
            cs = [(cb - 1 - ci * STEPS - t, ci * STEPS + t) for t in range(STEPS)]
            r0s = [tuple(pl.multiple_of(c * CH, CH) for c in ct) for ct in cs]
            q, k, v, bg = _chain_inputs(((qf, kf, vf), (qb, kb, vb)), r0s, (abf, abb), cst_ref[...])
            beta, g = _each(lambda t: t[0], bg), _each(lambda t: t[1], bg)
            parts, prep_vjp = jax.vjp(functools.partial(_chunk_prep, revs=_REVS), q, k, v, beta, g)
            ds = _each(lambda c: ds_scr[c[1], c[2]], _CHAINS[:_PER_STEP])
            dparts = []
            for t in range(STEPS):
                s = _each(lambda c: sall_refs[c[1]][cs[t][c[1]], c[2]], _CHAINS[:_PER_STEP])
                _, rec_vjp = jax.vjp(_chunk_rec, *_of_step(parts, t), s)
                do_t = _each(lambda c: do_refs[c[1]][pl.ds(r0s[t][c[1]], CH), c[2] * HD:(c[2] + 1) * HD] if has_do
                             else jnp.zeros((CH, HD), F32), _CHAINS[:_PER_STEP])
                *dpt, ds = rec_vjp((do_t, ds))
                dparts.append(dpt)
            for (_, d, h), dsv in zip(_CHAINS, ds):
                ds_scr[d, h] = dsv
            dq, dk, dv, dbeta, dg = prep_vjp(tuple(sum((dparts[t][j] for t in range(STEPS)), ()) for j in range(len(dparts[0]))))
            dab = [[jnp.zeros((CH, 128), F32), jnp.zeros((CH, 128), F32)] for _ in range(STEPS)]
            dal = jnp.zeros((1, 128), F32)
            for n, (t, d, h) in enumerate(_CHAINS):
                for part, val in enumerate((dq[n], dk[n], dv[n])):
                    dqkv_refs[d][pl.ds(r0s[t][d], CH), part * w + h * HD:part * w + (h + 1) * HD] = val
                z, ea = bg[n][2], bg[n][3]
                dbraw = dbeta[n] * beta[n] * (1.0 - beta[n])
                daraw = dg[n] * (-ea) * jax.nn.sigmoid(z)
                dab[t][d] = dab[t][d] + jnp.where(lane == NH * d + h, dbraw, 0.0) + jnp.where(lane == 2 * NH + NH * d + h, daraw, 0.0)
                dal = dal + jnp.where(lane1 == 2 * NH + NH * d + h, jnp.sum(dg[n] * g[n], axis=0, keepdims=True), 0.0)
            dsum = jnp.zeros((CH, 128), F32)
            for t in range(STEPS):
                for d in (0, 1):
                    dab_refs[d][pl.ds(r0s[t][d], CH), :] = dab[t][d]
                    dsum = dsum + dab[t][d]
            dcst_ref[0:1, :] += dal
            dcst_ref[1:2, :] += jnp.sum(jnp.where(lane >= 2 * NH, dsum, 0.0), axis=0, keepdims=True)
            return carry

        lax.fori_loop(0, cb // STEPS, chunks, 0)

        @pl.when(i == nb - 1)
        def _():
            ds0_ref[...] = ds_scr[...]

    pos = (lambda i: nb - 1 - i, lambda i: i)
    col = lambda d, c: pl.BlockSpec((TS, w), lambda i: (row_blk0 + pos[d](i), c))
    abs_ = lambda d: pl.BlockSpec((TS, 128), lambda i: (row_blk0 + pos[d](i), 0))
    full4 = pl.BlockSpec((2, NH, HD, HD), lambda i: (0, 0, 0, 0))
    small = pl.BlockSpec((8, 128), lambda i: (0, 0))
    hbm = pl.BlockSpec(memory_space=pl.ANY)
    sall_spec = lambda d: pl.BlockSpec((cb, NH, HD, HD), lambda i: (pos[d](i), 0, 0, 0))
    do_spec = (lambda d: pl.BlockSpec((TS, w), lambda i: (pos[d](i), 0))) if has_do else (lambda d: small)
    acc_specs = [pl.BlockSpec((TS, 3 * w), lambda i: (row_blk0 + pos[0](i), 0)),
                 pl.BlockSpec((TS, 3 * w), lambda i: (row_blk0 + pos[1](i), 0)), abs_(0), abs_(1), small]
    return _pallas(
        body, (qkv, qkv, qkv, ab, qkv, qkv, qkv, ab, cst, sall_f, sall_b, do, do, dsfin, dqkv_f, dqkv_b, dab_f, dab_b, dcst),
        name=name, grid=(nb,), sem=("arbitrary",), exchange=exchange,
        in_specs=[col(0, 0), col(0, 1), col(0, 2), abs_(0), col(1, 0), col(1, 1), col(1, 2), abs_(1), small,
                  sall_spec(0), sall_spec(1), do_spec(0), do_spec(1), full4, hbm, hbm, hbm, hbm, small],
        out_specs=acc_specs + [full4],
        out_shape=[jax.ShapeDtypeStruct(dqkv_f.shape, F32), jax.ShapeDtypeStruct(dqkv_b.shape, F32),
                   jax.ShapeDtypeStruct(dab_f.shape, F32), jax.ShapeDtypeStruct(dab_b.shape, F32),
                   jax.ShapeDtypeStruct((8, 128), F32), jax.ShapeDtypeStruct((2, NH, HD, HD), F32)],
        aliases={14: 0, 15: 1, 16: 2, 17: 3, 18: 4},
        scratch_shapes=[pltpu.VMEM((2, NH, HD, HD), F32)])


def _pool(xin, *, row0, transpose, name):
    n_tok = xin.shape[0] - row0
    rows = n_tok // GW
    pad = 8 * GW
    tt = 512
    gsh = GW.bit_length() - 1

    def body(x_ref, o_ref, ybuf):
        ii = lax.broadcasted_iota(jnp.int32, (128, 128), 0)
        jj = lax.broadcasted_iota(jnp.int32, (128, 128), 1)
        same_row = (ii >> gsh) == (jj >> gsh)
        ci, cj = ii & (GW - 1), jj & (GW - 1)
        tok = lax.broadcasted_iota(jnp.int32, (tt, 1), 0)
        zpad = jnp.zeros((pad, 128), F32)
        for gi, wdw in enumerate(POOL_WINDOWS):
            lo, hi = wdw // 2, wdw - wdw // 2
            if transpose:
                band = same_row & (ci - cj >= -lo) & (ci - cj < hi)
                offs = range(-hi + 1, lo + 1)
            else:
                band = same_row & (cj - ci >= -lo) & (cj - ci < hi)
                offs = range(-lo, hi)
            bandm = band.astype(BF)
            cs = slice(gi * 128, (gi + 1) * 128)
            ybuf[0:pad, :] = zpad
            ybuf[pad + n_tok:, :] = zpad

            def inv_area(t0):
                t = t0 + tok
                r, c = t >> gsh, t & (GW - 1)
                nr = jnp.minimum(r + hi, rows) - jnp.maximum(r - lo, 0)
                nc = jnp.minimum(c + hi, GW) - jnp.maximum(c - lo, 0)
                return 1.0 / (nr * nc).astype(F32)

            def col_pass(b, carry):
                t0 = pl.multiple_of(b * tt, tt)
                xv = x_ref[pl.ds(row0 + t0, tt), cs]
                if transpose:
                    xv = xv * inv_area(t0)
                hi_part = xv.astype(BF)
                lo_part = (xv - hi_part.astype(F32)).astype(BF)
                for s in range(tt // 128):
                    sl = slice(s * 128, (s + 1) * 128)
                    y = (jnp.dot(bandm, hi_part[sl], preferred_element_type=F32)
                         + jnp.dot(bandm, lo_part[sl], preferred_element_type=F32))
                    ybuf[pl.ds(pad + t0 + s * 128, 128), :] = y
                return carry

            lax.fori_loop(0, n_tok // tt, col_pass, 0)

            def row_pass(b, carry):
                t0 = pl.multiple_of(b * tt, tt)
                acc = ybuf[pl.ds(pad + t0 + offs[0] * GW, tt), :]
                for dr in offs[1:]:
                    acc = acc + ybuf[pl.ds(pad + t0 + dr * GW, tt), :]
                xv = x_ref[pl.ds(row0 + t0, tt), cs]
                if not transpose:
                    acc = acc * inv_area(t0)
                o_ref[pl.ds(t0, tt), cs] = acc - xv
                return carry

            lax.fori_loop(0, n_tok // tt, row_pass, 0)

    return pl.pallas_call(
        body, name=name, out_shape=jax.ShapeDtypeStruct((n_tok, NPOOL), F32),
        in_specs=[pl.BlockSpec(memory_space=pltpu.VMEM)], out_specs=pl.BlockSpec(memory_space=pltpu.VMEM),
        scratch_shapes=[pltpu.VMEM((n_tok + 2 * pad, 128), F32)],
        compiler_params=pltpu.CompilerParams(vmem_limit_bytes=VMEM_LIMIT),
    )(xin)


def _merge_parts(of, ob, pgate, pd, br, gnw, pw_ref, pscale, wg_ref, wp_ref):
    o = of + ob
    ons, ohs, rs = [], [], []
    for h in range(NH):
        oh = o[:, h * HD:(h + 1) * HD]
        r = lax.rsqrt(jnp.mean(oh * oh, axis=-1, keepdims=True) + EPS)
        ohs.append(oh * r)
        rs.append(r)
        ons.append(oh * r * gnw)
    on = jnp.concatenate(ons, axis=1)
    og = on * _silu(pgate)
    y_gdn = _nn(og, wg_ref[...])
    ypre = jnp.concatenate([_nn(pd[:, g * 128:(g + 1) * 128], pw_ref[g]) for g in range(4)], axis=1)
    yp = ypre * pscale
    y_pool = _nn(yp, wp_ref[...])
    g_pool = jax.nn.sigmoid(br[:, :D])
    g_gdn = jax.nn.sigmoid(br[:, D:])
    return dict(on=on, ohs=ohs, rs=rs, og=og, y_gdn=y_gdn, ypre=ypre, yp=yp, y_pool=y_pool, g_pool=g_pool, g_gdn=g_gdn)


def _merge_fwd(x1, of, ob, pgate, pd, br, modv, gnw, pool_w, pscale, w_gdn, w_pool, w_mo, *, name):
    n_tok = of.shape[0]

    def body(x_ref, of_ref, ob_ref, pg_ref, pd_ref, br_ref, mod_ref, gnw_ref, pw_ref, ps_ref, wg_ref, wp_ref, wmo_ref,
             x2_ref, og_ref, yp_ref, m_ref, mix_ref):
        t = _merge_parts(of_ref[...], ob_ref[...], pg_ref[...], pd_ref[...], br_ref[...], gnw_ref[...], pw_ref, ps_ref[...],
                         wg_ref, wp_ref)
        m = t["g_pool"] * t["y_pool"] + t["g_gdn"] * t["y_gdn"]
        mix = _nn(m, wmo_ref[...])
        og_ref[...] = t["og"].astype(BF)
        yp_ref[...] = t["yp"].astype(BF)
        m_ref[...] = m.astype(BF)
        mix_ref[...] = mix.astype(BF)
        x2_ref[...] = x_ref[...] + mod_ref[0, 5:6, :] * mix

    tile = lambda w: pl.BlockSpec((TM, w), lambda i: (i, 0))
    ctile = lambda w: pl.BlockSpec((TM, w), lambda i: (i + 1, 0))
    return pl.pallas_call(
        body, name=name, grid=(n_tok // TM,),
        in_specs=[ctile(D), tile(D), tile(D), ctile(D), tile(NPOOL), ctile(2 * D),
                  pl.BlockSpec((1, 16, D), lambda i: (1, 0, 0)), _const_spec((1, HD)), _const_spec((4, 128, 128)),
                  _const_spec((1, NPOOL)), _const_spec((D, D)), _const_spec((NPOOL, D)), _const_spec((D, D))],
        out_specs=[tile(D), tile(D), tile(NPOOL), tile(D), tile(D)],
        out_shape=[jax.ShapeDtypeStruct((n_tok, D), F32), jax.ShapeDtypeStruct((n_tok, D), BF),
                   jax.ShapeDtypeStruct((n_tok, NPOOL), BF), jax.ShapeDtypeStruct((n_tok, D), BF),
                   jax.ShapeDtypeStruct((n_tok, D), BF)],
        compiler_params=_cparams("parallel"),
    )(x1, of, ob, pgate, pd, br, modv, gnw, pool_w, pscale, w_gdn, w_pool, w_mo)


def _merge_bwd(dx2, mix, of, ob, pgate, pd, br, modv, gnw, pool_w, pscale, w_gdn, w_pool, w_mo, *, name):
    n_tok = of.shape[0]

    def body(dx2_ref, mix_ref, of_ref, ob_ref, pg_ref, pd_ref, br_ref, mod_ref, gnw_ref, pw_ref, ps_ref, wg_ref, wp_ref, wmo_ref,
             do_ref, dgate_ref, dpd_ref, dbr_ref, dmix_ref, dyg_ref, dyp_ref, acc_ref, dpw_ref):
        i = pl.program_id(0)
        pgate, pdv, gnw = pg_ref[...], pd_ref[...], gnw_ref[...]
        t = _merge_parts(of_ref[...], ob_ref[...], pgate, pdv, br_ref[...], gnw, pw_ref, ps_ref[...], wg_ref, wp_ref)
        dx2v = dx2_ref[...]
        dmix = mod_ref[0, 5:6, :] * dx2v
        dmixb = dmix.astype(BF)
        dmix_ref[...] = dmixb
        dm = _nt(dmixb, wmo_ref[...])
        gp, gg = t["g_pool"], t["g_gdn"]
        dbr_ref[:, :D] = dm * t["y_pool"] * gp * (1.0 - gp)
        dbr_ref[:, D:] = dm * t["y_gdn"] * gg * (1.0 - gg)
        dyp = (dm * gp).astype(BF)
        dyg = (dm * gg).astype(BF)
        dyp_ref[...] = dyp
        dyg_ref[...] = dyg
        dyp_in = _nt(dyp, wp_ref[...])
        dypre = dyp_in * ps_ref[...]
        for g in range(4):
            gs = slice(g * 128, (g + 1) * 128)
            dpd_ref[:, gs] = _nt(dypre[:, gs], pw_ref[g])
        dog = _nt(dyg, wg_ref[...])
        dgate_ref[...] = dog * t["on"] * _dsilu(pgate)
        don = dog * _silu(pgate)
        dgnw = jnp.zeros((1, HD), F32)
        for h in range(NH):
            hs = slice(h * HD, (h + 1) * HD)
            donh, oh, r = don[:, hs], t["ohs"][h], t["rs"][h]
            dgnw = dgnw + jnp.sum(donh * oh, axis=0, keepdims=True)
            doh = donh * gnw
            do_ref[:, hs] = r * (doh - oh * jnp.mean(doh * oh, axis=-1, keepdims=True))

        @pl.when(i == 0)
        def _():
            acc_ref[...] = jnp.zeros_like(acc_ref)
            dpw_ref[...] = jnp.zeros_like(dpw_ref)

        acc_ref[0:1, :] += jnp.sum(dx2v * mix_ref[...].astype(F32), axis=0, keepdims=True)
        acc_ref[1:2, 0:HD] += dgnw
        acc_ref[2:3, 0:NPOOL] += jnp.sum(dyp_in * t["ypre"], axis=0, keepdims=True)
        for g in range(4):
            gs = slice(g * 128, (g + 1) * 128)
            dpw_ref[g] += _tn(pdv[:, gs], dypre[:, gs])

    tile = lambda w: pl.BlockSpec((TM, w), lambda i: (i, 0))
    ctile = lambda w: pl.BlockSpec((TM, w), lambda i: (i + 1, 0))
    return pl.pallas_call(
        body, name=name, grid=(n_tok // TM,),
        in_specs=[tile(D), tile(D), tile(D), tile(D), ctile(D), tile(NPOOL), ctile(2 * D),
                  pl.BlockSpec((1, 16, D), lambda i: (1, 0, 0)), _const_spec((1, HD)), _const_spec((4, 128, 128)),
                  _const_spec((1, NPOOL)), _const_spec((D, D)), _const_spec((NPOOL, D)), _const_spec((D, D))],
        out_specs=[tile(D), tile(D), tile(NPOOL), tile(2 * D), tile(D), tile(D), tile(D),
                   pl.BlockSpec((8, D), lambda i: (0, 0)), pl.BlockSpec((4, 128, 128), lambda i: (0, 0, 0))],
        out_shape=[jax.ShapeDtypeStruct((n_tok, D), F32), jax.ShapeDtypeStruct((n_tok, D), F32),
                   jax.ShapeDtypeStruct((n_tok, NPOOL), F32), jax.ShapeDtypeStruct((n_tok, 2 * D), F32),
                   jax.ShapeDtypeStruct((n_tok, D), BF), jax.ShapeDtypeStruct((n_tok, D), BF), jax.ShapeDtypeStruct((n_tok, D), BF),
                   jax.ShapeDtypeStruct((8, D), F32), jax.ShapeDtypeStruct((4, 128, 128), F32)],
        compiler_params=_cparams("arbitrary"),
    )(dx2, mix, of, ob, pgate, pd, br, modv, gnw, pool_w, pscale, w_gdn, w_pool, w_mo)


def _final(x3, target, fnw, *, name):
    n_tok = x3.shape[0]

    def body(x_ref, t_ref, w_ref, dx_ref, acc_ref):
        xv, w = x_ref[...], w_ref[...]
        r = lax.rsqrt(jnp.mean(xv * xv, axis=-1, keepdims=True) + EPS)
        xh = xv * r
        err = xh * w - t_ref[...]
        dy = err * (1.0 / D)
        dxh = dy * w
        dx_ref[...] = r * (dxh - xh * jnp.mean(dxh * xh, axis=-1, keepdims=True))

        @pl.when(pl.program_id(0) == 0)
        def _():
            acc_ref[...] = jnp.zeros_like(acc_ref)

        acc_ref[0:1, :] += jnp.sum(dy * xh, axis=0, keepdims=True)
        acc_ref[1:2, :] += jnp.sum(err * err, axis=0, keepdims=True) * (0.5 / D)

    tile = pl.BlockSpec((TM, D), lambda i: (i, 0))
    return pl.pallas_call(
        body, name=name, grid=(n_tok // TM,),
        in_specs=[tile, tile, _const_spec((1, D))],
        out_specs=[tile, pl.BlockSpec((8, D), lambda i: (0, 0))],
        out_shape=[jax.ShapeDtypeStruct((n_tok, D), F32), jax.ShapeDtypeStruct((8, D), F32)],
        compiler_params=_cparams("arbitrary"),
    )(x3, target, fnw)


def _split(results, n):
    return (*results[:n], list(results[n:]))


def _local_step(xc, target, modv, p, late=None):
    n_all = xc.shape[0]
    t_lat = n_all - TM
    nbc, nbx = TM // TS, t_lat // TS
    mod_lat = modv[1:2]
    gather = (lambda arrs: _ChipExchange(arrs, False)) if late else (lambda arrs: None)
    scatter = (lambda arrs: _ChipExchange(arrs, True)) if late else (lambda arrs: None)

    x1, h1, gu1, f1, *got = _ffn_fwd(xc, modv, p["norm1"], p["w1_in"], p["w1_out"], mrow=0, name="ffn1_fwd",
                                     exchange=gather(late and late[0]))
    if late:
        p = {**p, "w_mix": _regroup_mix(_from_chip_major_cols(got[0])), "conv": jnp.pad(_from_chip_major_cols(got[1]), ((0, 3), (0, 0)))}
    u, p_qkv, p_gate, p_pool, p_br, p_ab, *got = _mix_in_fwd(x1, modv, p["norm2"], p["w_mix"], name="mix_in_fwd",
                                                              exchange=gather(late and late[2]))
    if late:
        p = {**p, "w_gdn": got[0].reshape(D, D), "w_pool": _from_chip_major_cols(got[1]), "w_mo": got[2].reshape(D, D)}
    qkv, pre_qkv = _prep_fwd(p_qkv, p["conv"], name="prep_fwd")
    s_zero = jnp.zeros((2, NH, HD, HD), F32)
    _, _, sall_cf, sall_cb, s_ctx = _scan_fwd(qkv, p_ab, p["cst"], s_zero, row_blk0=0, nb=nbc, name="scan_ctx")
    o_f, o_b, sall_f, sall_b, _, *got = _scan_fwd(qkv, p_ab, p["cst"], s_ctx, row_blk0=nbc, nb=nbx, name="scan_lat",
                                                  exchange=gather(late and late[1]))
    if late:
        p = {**p, "w2_in": got[0], "w2_out": got[1].reshape(FF, D)}
    pd = _pool(p_pool, row0=TM, transpose=False, name="pool_fwd")
    merge_w = (modv, p["gnw"], p["pool_w"], p["pscale"], p["w_gdn"], p["w_pool"], p["w_mo"])
    x2, og, yp, m, mix = _merge_fwd(x1, o_f, o_b, p_gate, pd, p_br, *merge_w, name="merge_fwd")
    x3, h3, gu3, f3 = _ffn_fwd(x2, mod_lat, p["norm3"], p["w2_in"], p["w2_out"], mrow=6, name="ffn2_fwd")
    dx3, acc_fin = _final(x3, target, p["fnorm"], name="final")

    dx2, a3, df3, dgu3, acc3 = _ffn_bwd(dx3, x2, gu3, f3, mod_lat, p["norm3"], p["w2_in"], p["w2_out"], mrow=6, dx_skip=0,
                                        name="ffn2_bwd")
    g = {}
    tkl = _k_tile(t_lat)
    g["w2_out"] = _matmul_tn(a3, df3, tmm=FF // 2, tn=D, tk=tkl, name="ffn2_wout_grad").reshape(NCHIP, FF // NCHIP, D)
    g["w2_in"] = _matmul_tn(h3, dgu3, tmm=D, tn=2 * FF // NCHIP, tk=tkl, nsplit=NCHIP, name="ffn2_win_grad")
    do, dgate, dpd, dbr, dmix, dyg, dyp, acc_m, dpw = _merge_bwd(dx2, mix, o_f, o_b, p_gate, pd, p_br, *merge_w, name="merge_bwd")
    g["w_mo"] = _matmul_tn(m, dmix, tmm=D, tn=D, tk=tkl, name="wmo_grad").reshape(NCHIP, D // NCHIP, D)
    g["w_gdn"] = _matmul_tn(og, dyg, tmm=D, tn=D, tk=tkl, name="wgdn_grad").reshape(NCHIP, D // NCHIP, D)
    g["w_pool"] = _matmul_tn(yp, dyp, tmm=NPOOL, tn=D // NCHIP, tk=tkl, nsplit=NCHIP, name="wpool_grad")
    dpool_in = _pool(dpd, row0=0, transpose=True, name="pool_bwd")
    acc = (lax.empty((n_all, NQKV), F32), lax.empty((n_all, NQKV), F32), lax.empty((n_all, 128), F32),
           lax.empty((n_all, 128), F32), jnp.zeros((8, 128), F32))
    behind_scan = ("w2_in", "w2_out", "w_gdn", "w_pool", "w_mo")
    *acc, ds_ctx, landed = _split(_scan_bwd(qkv, p_ab, p["cst"], sall_f, sall_b, do, s_zero, *acc, row_blk0=nbc, nb=nbx, has_do=True,
                                            name="scan_lat_bwd", exchange=scatter([g[k] for k in behind_scan])), 6)
    landed = dict(zip(behind_scan, landed))
    dqkv_f, dqkv_b, dab_f, dab_b, dcst, _ = _scan_bwd(qkv, p_ab, p["cst"], sall_cf, sall_cb, jnp.zeros((8, 128), F32), ds_ctx, *acc,
                                                      row_blk0=0, nb=nbc, has_do=False, name="scan_ctx_bwd")
    dpqkv, dconv = _prep_bwd(p_qkv, pre_qkv, dqkv_f, dqkv_b, p["conv"], name="prep_bwd")
    dx1, dp, acc_mix = _mix_in_bwd(dx2, x1, dpqkv, dgate, dpool_in, dbr, dab_f, dab_b, modv, p["norm2"], p["w_mix"],
                                   name="mix_in_bwd")
    tka = _k_tile(n_all)
    g["w_mix"] = _chip_major_cols(_ungroup_mix(_matmul_tn(u, dp, tmm=256, tn=NMIXP, tk=_k_tile(n_all, 1024), name="wmix_grad")))
    dx_lat, a1, df1, dgu1, acc1, got = _split(_ffn_bwd(dx1, xc, gu1, f1, modv, p["norm1"], p["w1_in"], p["w1_out"], mrow=0, dx_skip=1,
                                                   name="ffn1_bwd", exchange=scatter([g["w_mix"]])), 5)
    landed.update(zip(("w_mix",), got))
    g["w1_out"] = _matmul_tn(a1, df1, tmm=FF // 2, tn=D, tk=tka, name="ffn1_wout_grad").reshape(NCHIP, FF // NCHIP, D)
    g["w1_in"] = _matmul_tn(h1, dgu1, tmm=D, tn=2 * FF // NCHIP, tk=tka, nsplit=NCHIP, name="ffn1_win_grad")

    small = dict(norm1=acc1[0, 3] + acc1[1, 3], norm2=acc_mix[0, 3] + acc_mix[1, 3], norm3=acc3[0, 3], fnorm=acc_fin[0],
                 gnw=acc_m[1, :HD], pscale=acc_m[2, :NPOOL], pool_w=dpw, conv=dconv[:5],
                 a_log=dcst[0, 2 * NH:4 * NH], dt_bias=dcst[1, 2 * NH:4 * NH])
    zero = jnp.zeros((D,), F32)
    dmod = jnp.stack([
        jnp.stack([acc1[0, 0], acc1[0, 1], acc1[0, 2], acc_mix[0, 0], acc_mix[0, 1], zero, zero, zero, zero]),
        jnp.stack([acc1[1, 0], acc1[1, 1], acc1[1, 2], acc_mix[1, 0], acc_mix[1, 1], acc_m[0], acc3[0, 0], acc3[0, 1], acc3[0, 2]]),
    ])
    return jnp.sum(acc_fin[1]), dx_lat, g, landed, small, dmod


_HI = lax.Precision.HIGHEST


def _ada_fwd(c_all, w_sh, b_sh, *, name):
    def body(c_ref, w_ref, b_ref, o_ref):
        o_ref[...] = jnp.dot(_silu(c_ref[...]), w_ref[...], precision=_HI, preferred_element_type=F32) + b_ref[...]

    return pl.pallas_call(body, name=name, out_shape=jax.ShapeDtypeStruct((16, w_sh.shape[1]), F32),
                          compiler_params=pltpu.CompilerParams(vmem_limit_bytes=VMEM_LIMIT))(c_all, w_sh, b_sh)


def _ada_bwd(c_all, dm, w_sh, *, name):
    def body(c_ref, dm_ref, w_ref, dw_ref, dc_ref):
        sc = _silu(c_ref[...])
        dw_ref[...] = lax.dot_general(sc, dm_ref[...], (((0,), (0,)), ((), ())), precision=_HI, preferred_element_type=F32)
        part = lax.dot_general(dm_ref[8:9, :], w_ref[...], (((1,), (1,)), ((), ())), precision=_HI, preferred_element_type=F32)
        dc_ref[...] = jnp.broadcast_to(part, dc_ref.shape)

    return pl.pallas_call(body, name=name,
                          out_shape=[jax.ShapeDtypeStruct(w_sh.shape, F32), jax.ShapeDtypeStruct((8, D), F32)],
                          compiler_params=pltpu.CompilerParams(vmem_limit_bytes=VMEM_LIMIT))(c_all, dm, w_sh)


def _cctx_grad(parts, c_ctx, *, name):
    def body(p_ref, c_ref, o_ref):
        tot = (p_ref[0, 0:1, :] + p_ref[2, 0:1, :]) + (p_ref[4, 0:1, :] + p_ref[6, 0:1, :])
        o_ref[...] = tot * _dsilu(c_ref[...])

    return pl.pallas_call(body, name=name, out_shape=jax.ShapeDtypeStruct((1, D), F32))(parts, c_ctx)


_MESH = pl.DeviceIdType.MESH
_ANY = pl.BlockSpec(memory_space=pl.ANY)


def _flip(v, bit):
    return (1 - v) if bit else v


def _all_gather8(x, *, name):
    def body(x_ref, out_ref, send_sems, recv_sems, local_sem):
        mx, my, mc = lax.axis_index("x"), lax.axis_index("y"), lax.axis_index("c")
        me = 4 * mx + 2 * my + mc
        mine = pltpu.make_async_copy(x_ref, out_ref.at[me], local_sem)
        mine.start()
        sends, recvs = [], []
        for k in range(1, 8):
            px, py, pc = _flip(mx, k & 4), _flip(my, k & 2), _flip(mc, k & 1)
            sends.append(pltpu.make_async_remote_copy(src_ref=x_ref, dst_ref=out_ref.at[me], send_sem=send_sems.at[k - 1],
                                                      recv_sem=recv_sems.at[k - 1], device_id=(px, py, pc), device_id_type=_MESH))
            recvs.append(pltpu.make_async_remote_copy(src_ref=x_ref, dst_ref=out_ref.at[4 * px + 2 * py + pc],
                                                      send_sem=send_sems.at[k - 1], recv_sem=recv_sems.at[k - 1],
                                                      device_id=(px, py, pc), device_id_type=_MESH))
        for cp in sends:
            cp.start()
        for cp in recvs:
            cp.wait_recv()
        for cp in sends:
            cp.wait_send()
        mine.wait()

    vm = pl.BlockSpec(memory_space=pltpu.VMEM)
    return pl.pallas_call(
        body, name=name, out_shape=jax.ShapeDtypeStruct((8,) + x.shape, x.dtype), in_specs=[vm], out_specs=vm,
        scratch_shapes=[pltpu.SemaphoreType.DMA((7,)), pltpu.SemaphoreType.DMA((7,)), pltpu.SemaphoreType.DMA],
        compiler_params=pltpu.CompilerParams(vmem_limit_bytes=VMEM_LIMIT),
    )(x)


class _ChipExchange:
    def __init__(self, arrs, scatter):
        self.arrs, self.scatter, self.n = list(arrs), scatter, len(arrs)
        self.out_shape = [jax.ShapeDtypeStruct(a.shape if scatter else (NCHIP,) + a.shape, a.dtype) for a in self.arrs]
        links = self.n * (NCHIP - 1)
        self.scratch = [pltpu.SemaphoreType.DMA((links,)), pltpu.SemaphoreType.DMA((links,)), pltpu.SemaphoreType.DMA((self.n,))]

    def copies(self, ins, outs, send_sems, recv_sems, local_sems):
        mx, my, mc = lax.axis_index("x"), lax.axis_index("y"), lax.axis_index("c")
        me = 2 * mx + my
        local, sends, recvs = [], [], []
        for j in range(self.n):
            src_own = ins[j].at[me] if self.scatter else ins[j]
            local.append(pltpu.make_async_copy(src_own, outs[j].at[me], local_sems.at[j]))
            for k in range(1, NCHIP):
                px, py = _flip(mx, k & 2), _flip(my, k & 1)
                peer = 2 * px + py
                sem = j * (NCHIP - 1) + k - 1
                src = ins[j].at[peer] if self.scatter else ins[j]
                sends.append(pltpu.make_async_remote_copy(src_ref=src, dst_ref=outs[j].at[me], send_sem=send_sems.at[sem],
                                                          recv_sem=recv_sems.at[sem], device_id=(px, py, mc), device_id_type=_MESH))
                recvs.append(pltpu.make_async_remote_copy(src_ref=src, dst_ref=outs[j].at[peer], send_sem=send_sems.at[sem],
                                                          recv_sem=recv_sems.at[sem], device_id=(px, py, mc), device_id_type=_MESH))
        return local, sends, recvs

    @staticmethod
    def start(local, sends, recvs):
        for cp in local + sends:
            cp.start()

    @staticmethod
    def finish(local, sends, recvs):
        for cp in recvs:
            cp.wait_recv()
        for cp in sends:
            cp.wait_send()
        for cp in local:
            cp.wait()


def _chip_exchange(arrs, *, scatter, name):
    ex = _ChipExchange(arrs, scatter)

    def body(*refs):
        cps = ex.copies(refs[:ex.n], refs[ex.n:2 * ex.n], *refs[2 * ex.n:])
        ex.start(*cps)
        ex.finish(*cps)

    return pl.pallas_call(body, name=name, out_shape=ex.out_shape, in_specs=[_ANY] * ex.n, out_specs=[_ANY] * ex.n,
                          scratch_shapes=ex.scratch)(*arrs)


def _gather_split(arrs, *, name):
    n = len(arrs)
    links = n * (NCHIP - 1)

    def body(*refs):
        ins, outs = refs[:n], refs[n:2 * n]
        ici_send, ici_recv, d2d_send, d2d_recv, local_sems = refs[2 * n:]
        mx, my, mc = lax.axis_index("x"), lax.axis_index("y"), lax.axis_index("c")
        me = 2 * mx + my
        local, first, arrive, onward, handed = [], [], [], [], []
        for j in range(n):
            hr = ins[j].shape[0] // 2
            mine, other = pl.ds(mc * hr, hr), pl.ds((1 - mc) * hr, hr)
            local.append(pltpu.make_async_copy(ins[j], outs[j].at[me], local_sems.at[j]))
            for k in range(1, NCHIP):
                px, py = _flip(mx, k & 2), _flip(my, k & 1)
                peer = 2 * px + py
                sem = j * (NCHIP - 1) + k - 1
                ici = lambda slot: pltpu.make_async_remote_copy(
                    src_ref=ins[j].at[mine], dst_ref=outs[j].at[slot, mine], send_sem=ici_send.at[sem], recv_sem=ici_recv.at[sem],
                    device_id=(px, py, mc), device_id_type=_MESH)
                d2d = lambda rows: pltpu.make_async_remote_copy(
                    src_ref=outs[j].at[peer, rows], dst_ref=outs[j].at[peer, rows], send_sem=d2d_send.at[sem], recv_sem=d2d_recv.at[sem],
                    device_id=(mx, my, 1 - mc), device_id_type=_MESH)
                first.append(ici(me))
                arrive.append(ici(peer))
                onward.append(d2d(mine))
                handed.append(d2d(other))
        for cp in local + first:
            cp.start()
        for got, fwd in zip(arrive, onward):
            got.wait_recv()
            fwd.start()
        for cp in handed:
            cp.wait_recv()
        for cp in first + onward:
            cp.wait_send()
        for cp in local:
            cp.wait()

    sems = pltpu.SemaphoreType.DMA((links,))
    return pl.pallas_call(
        body, name=name, out_shape=[jax.ShapeDtypeStruct((NCHIP,) + a.shape, a.dtype) for a in arrs],
        in_specs=[_ANY] * n, out_specs=[_ANY] * n, scratch_shapes=[sems, sems, sems, sems, pltpu.SemaphoreType.DMA((n,))],
    )(*arrs)


_HBM = pl.BlockSpec(memory_space=pltpu.HBM)
_SEM = pl.BlockSpec(memory_space=pltpu.SEMAPHORE)
_DATAFLOW = pltpu.SideEffectType.DATAFLOW_SIDE_EFFECTING


def _scatter_copies(ins, lands, send_sems, recv_sems):
    mx, my, mc = lax.axis_index("x"), lax.axis_index("y"), lax.axis_index("c")
    me = 2 * mx + my
    sends, recvs = [], []
    for j in range(len(ins)):
        for k in range(1, NCHIP):
            px, py = _flip(mx, k & 2), _flip(my, k & 1)
            peer = 2 * px + py
            sem = j * (NCHIP - 1) + k - 1
            mk = lambda slot: pltpu.make_async_remote_copy(src_ref=ins[j].at[peer], dst_ref=lands[j].at[slot], send_sem=send_sems.at[sem],
                                                           recv_sem=recv_sems.at[sem], device_id=(px, py, mc), device_id_type=_MESH)
            sends.append(mk(me))
            recvs.append(mk(peer))
    return sends, recvs


def _scatter_start(arrs, after, *, name):
    n = len(arrs)
    links = n * (NCHIP - 1)
    n_in = 2 * n + len(after)

    def body(*refs):
        ins, lands = refs[:n], refs[n:2 * n]
        send_sems, recv_sems = refs[n_in], refs[n_in + 1]
        token = refs[-1]
        for cp in _scatter_copies(ins, lands, send_sems, recv_sems)[0]:
            cp.start()
        token[...] = jnp.zeros_like(token)

    hbm = lambda a: pltpu.HBM(a.shape, a.dtype)
    res = pl.pallas_call(
        body, name=name,
        out_shape=(pltpu.SemaphoreType.DMA((links,)), pltpu.SemaphoreType.DMA((links,)), *[hbm(a) for a in arrs], *[hbm(a) for a in arrs],
                   jax.ShapeDtypeStruct((8, 128), F32)),
        in_specs=[_HBM] * (2 * n) + [_ANY] * len(after), out_specs=(_SEM, _SEM, *[_HBM] * (2 * n), pl.BlockSpec(memory_space=pltpu.VMEM)),
        input_output_aliases={j: 2 + j for j in range(2 * n)},
        compiler_params=pltpu.CompilerParams(has_side_effects=_DATAFLOW),
    )(*[pltpu.with_memory_space_constraint(a, pltpu.HBM) for a in arrs],
      *[pltpu.with_memory_space_constraint(lax.empty(a.shape, a.dtype), pltpu.HBM) for a in arrs], *after)
    return res[0], res[1], list(res[2:2 + n]), list(res[2 + n:2 + 2 * n]), res[-1]


def _scatter_wait(send_sems, recv_sems, arrs, lands, after, *, name):
    n = len(arrs)

    def body(*refs):
        ins, lands_in = refs[:n], refs[n:2 * n]
        sends, recvs = _scatter_copies(ins, lands_in, refs[2 * n], refs[2 * n + 1])
        for cp in sends:
            cp.wait_send()
        for cp in recvs:
            cp.wait_recv()

    hbm = lambda a: pltpu.HBM(a.shape, a.dtype)
    res = pl.pallas_call(
        body, name=name, out_shape=(*[hbm(a) for a in arrs], *[hbm(a) for a in lands]),
        in_specs=[_HBM] * (2 * n) + [_SEM, _SEM] + [_ANY] * len(after), out_specs=[_HBM] * (2 * n),
        input_output_aliases={j: j for j in range(2 * n)},
        compiler_params=pltpu.CompilerParams(has_side_effects=_DATAFLOW),
    )(*arrs, *lands, send_sems, recv_sems, *after)
    return list(res[:n]), list(res[n:])


def _pallas(body, operands, *, name, grid, in_specs, out_specs, out_shape, sem, scratch_shapes=(), aliases=None, exchange=None):
    if exchange is None:
        return pl.pallas_call(body, name=name, grid=grid, in_specs=in_specs, out_specs=out_specs, out_shape=out_shape,
                              scratch_shapes=list(scratch_shapes), input_output_aliases=aliases or {},
                              compiler_params=_cparams(*sem))(*operands)
    ex = exchange
    (steps,) = grid
    n_in, n_out, n_scr, k = len(in_specs), len(out_specs), len(scratch_shapes), ex.n

    def hosted(*refs):
        ins, refs = refs[:n_in], refs[n_in:]
        ex_in, refs = refs[:k], refs[k:]
        outs, refs = refs[:n_out], refs[n_out:]
        ex_out, refs = refs[:k], refs[k:]
        scr, ex_sems = refs[:n_scr], refs[n_scr:]
        cps = ex.copies(ex_in, ex_out, *ex_sems)
        pl.when(pl.program_id(0) == 0)(lambda: ex.start(*cps))
        body(*ins, *outs, *scr)
        pl.when(pl.program_id(0) == steps - 1)(lambda: ex.finish(*cps))

    return pl.pallas_call(
        hosted, name=name, grid=grid, in_specs=list(in_specs) + [_ANY] * k, out_specs=list(out_specs) + [_ANY] * k,
        out_shape=list(out_shape) + ex.out_shape, scratch_shapes=list(scratch_shapes) + ex.scratch,
        input_output_aliases=aliases or {}, compiler_params=_cparams("arbitrary"),
    )(*operands, *ex.arrs)


def _core_swap(arrs, *, name):
    n = len(arrs)

    def body(*refs):
        ins, outs = refs[:n], refs[n:2 * n]
        send_sems, recv_sems = refs[2 * n:]
        sib = (lax.axis_index("x"), lax.axis_index("y"), 1 - lax.axis_index("c"))
        cps = [pltpu.make_async_remote_copy(src_ref=ins[j], dst_ref=outs[j], send_sem=send_sems.at[j], recv_sem=recv_sems.at[j],
                                            device_id=sib, device_id_type=_MESH) for j in range(n)]
        for cp in cps:
            cp.start()
        for cp in cps:
            cp.wait_recv()
        for cp in cps:
            cp.wait_send()

    return pl.pallas_call(
        body, name=name, out_shape=[jax.ShapeDtypeStruct(a.shape, a.dtype) for a in arrs],
        in_specs=[_ANY] * n, out_specs=[_ANY] * n,
        scratch_shapes=[pltpu.SemaphoreType.DMA((n,)), pltpu.SemaphoreType.DMA((n,))],
    )(*arrs)


def _row_tile(rows, cols, budget=1 << 18):
    best = None
    for t in range(8, rows + 1, 8):
        if rows % t == 0 and t * cols <= budget:
            best = t
    return best or rows


def _sum_slots(x, *, name, after=()):
    ns, r, c = x.shape
    tr = _row_tile(r, c * ns)

    def body(x_ref, *refs):
        acc = x_ref[0].astype(F32)
        for s in range(1, ns):
            acc = acc + x_ref[s].astype(F32)
        refs[-1][...] = acc

    return pl.pallas_call(
        body, name=name, grid=(r // tr,), out_shape=jax.ShapeDtypeStruct((r, c), F32),
        in_specs=[pl.BlockSpec((ns, tr, c), lambda i: (0, i, 0))] + [_ANY] * len(after), out_specs=pl.BlockSpec((tr, c), lambda i: (i, 0)),
        compiler_params=_cparams("parallel"),
    )(x, *after)


def _adamw(w, ga, gb, m, v, *, name):
    r, c = w.shape
    tr = _row_tile(r, c, budget=1 << 17)
    two = gb is not None

    def body(*refs):
        w_ref, ga_ref = refs[0], refs[1]
        m_ref, v_ref = refs[2 + two], refs[3 + two]
        g_ref, d_ref, mo_ref, vo_ref = refs[4 + two:]
        g = ga_ref[...] + refs[2][...] if two else ga_ref[...]
        mn = ADAM_B1 * m_ref[...] + (1.0 - ADAM_B1) * g
        vn = ADAM_B2 * v_ref[...] + (1.0 - ADAM_B2) * (g * g)
        m_hat = mn / (1.0 - ADAM_B1 ** ADAM_STEP)
        v_hat = vn / (1.0 - ADAM_B2 ** ADAM_STEP)
        g_ref[...] = g
        d_ref[...] = -ADAM_LR * (m_hat / (jnp.sqrt(v_hat) + ADAM_EPS) + ADAM_WD * w_ref[...])
        mo_ref[...] = mn
        vo_ref[...] = vn

    spec = pl.BlockSpec((tr, c), lambda i: (i, 0))
    ins = [w, ga] + ([gb] if two else []) + [m, v]
    return pl.pallas_call(
        body, name=name, grid=(r // tr,), out_shape=[jax.ShapeDtypeStruct((r, c), F32)] * 4,
        in_specs=[spec] * len(ins), out_specs=[spec] * 4, compiler_params=_cparams("parallel"),
    )(*ins)


_MIX_AB0, _MIX_AB1 = NQKV, NQKV + 4 * NH


def _regroup_mix(w):
    pad = jnp.zeros((w.shape[0], NMIXP - NMIX), w.dtype)
    return jnp.concatenate([w[:, :_MIX_AB0], w[:, _MIX_AB1:], w[:, _MIX_AB0:_MIX_AB1], pad], axis=1)


def _ungroup_mix(w):
    n_ab = _MIX_AB1 - _MIX_AB0
    return jnp.concatenate([w[:, :_MIX_AB0], w[:, NMIX - n_ab:NMIX], w[:, _MIX_AB0:NMIX - n_ab]], axis=1)


def _chip_major_cols(w):
    r, c = w.shape
    return w.reshape(r, NCHIP, c // NCHIP).transpose(1, 0, 2)


def _from_chip_major_cols(w):
    return w.transpose(1, 0, 2).reshape(w.shape[1], -1)


_SMALL = (("c_ctx", D), ("b_ada", 9 * D), ("norm1_w", D), ("norm2_w", D), ("norm3_w", D), ("final_norm_w", D),
          ("a_log", 2 * NH), ("dt_bias", 2 * NH), ("gdn_norm_w", HD), ("pool_w", 4 * 128 * 128), ("pool_scale", NPOOL),
          ("conv_w", 5 * NQKV // NCHIP))


def _pack(vals, lanes=128, row_mult=8):
    flat = jnp.concatenate([jnp.ravel(v) for v in vals])
    n = flat.shape[0]
    rows = -(-n // (lanes * row_mult)) * row_mult
    return jnp.pad(flat, (0, rows * lanes - n)).reshape(rows, lanes)


def _unpack(packed, sizes):
    flat = packed.reshape(-1)
    out, o = [], 0
    for n in sizes:
        out.append(flat[o:o + n])
        o += n
    return out


def kernel(x, c, ctx, c_ctx, w_ada, b_ada, norm1_w, ffn1_w_in, ffn1_w_out, norm2_w, w_mix_in, conv_w, a_log, dt_bias, gdn_norm_w, w_gdn_proj, pool_w, pool_scale, w_pool_proj, w_mix_out, norm3_w, ffn2_w_in, ffn2_w_out, final_norm_w, loss_target, m_c_ctx, m_w_ada, m_b_ada, m_norm1_w, m_ffn1_w_in, m_ffn1_w_out, m_norm2_w, m_w_mix_in, m_conv_w, m_a_log, m_dt_bias, m_gdn_norm_w, m_w_gdn_proj, m_pool_w, m_pool_scale, m_w_pool_proj, m_w_mix_out, m_norm3_w, m_ffn2_w_in, m_ffn2_w_out, m_final_norm_w, v_c_ctx, v_w_ada, v_b_ada, v_norm1_w, v_ffn1_w_in, v_ffn1_w_out, v_norm2_w, v_w_mix_in, v_conv_w, v_a_log, v_dt_bias, v_gdn_norm_w, v_w_gdn_proj, v_pool_w, v_pool_scale, v_w_pool_proj, v_w_mix_out, v_norm3_w, v_ffn2_w_in, v_ffn2_w_out, v_final_norm_w):
    names = ("c_ctx", "w_ada", "b_ada", "norm1_w", "ffn1_w_in", "ffn1_w_out", "norm2_w", "w_mix_in", "conv_w", "a_log", "dt_bias",
             "gdn_norm_w", "w_gdn_proj", "pool_w", "pool_scale", "w_pool_proj", "w_mix_out", "norm3_w", "ffn2_w_in", "ffn2_w_out",
             "final_norm_w")
    w = dict(zip(names, (c_ctx, w_ada, b_ada, norm1_w, ffn1_w_in, ffn1_w_out, norm2_w, w_mix_in, conv_w, a_log, dt_bias, gdn_norm_w,
                         w_gdn_proj, pool_w, pool_scale, w_pool_proj, w_mix_out, norm3_w, ffn2_w_in, ffn2_w_out, final_norm_w)))
    mom = dict(zip(names, (m_c_ctx, m_w_ada, m_b_ada, m_norm1_w, m_ffn1_w_in, m_ffn1_w_out, m_norm2_w, m_w_mix_in, m_conv_w, m_a_log,
                           m_dt_bias, m_gdn_norm_w, m_w_gdn_proj, m_pool_w, m_pool_scale, m_w_pool_proj, m_w_mix_out, m_norm3_w,
                           m_ffn2_w_in, m_ffn2_w_out, m_final_norm_w)))
    var = dict(zip(names, (v_c_ctx, v_w_ada, v_b_ada, v_norm1_w, v_ffn1_w_in, v_ffn1_w_out, v_norm2_w, v_w_mix_in, v_conv_w, v_a_log,
                           v_dt_bias, v_gdn_norm_w, v_w_gdn_proj, v_pool_w, v_pool_scale, v_w_pool_proj, v_w_mix_out, v_norm3_w,
                           v_ffn2_w_in, v_ffn2_w_out, v_final_norm_w)))
    mx, my, mc = lax.axis_index("x"), lax.axis_index("y"), lax.axis_index("c")
    chip = 2 * mx + my
    dev = 2 * chip + mc
    ada_cols = w_ada.shape[2]

    c_rows = _all_gather8(jnp.pad(c, ((0, 7), (0, 0))), name="gather_c")[:, 0, :]
    c_all = jnp.concatenate([c_rows, c_ctx[None], jnp.zeros((7, D), F32)], axis=0)
    b_sh = lax.dynamic_slice(b_ada, (0, chip * ada_cols), (1, ada_cols))
    mod_sh = _ada_fwd(c_all, w_ada[0], b_sh, name="ada_fwd")
    mod_parts = _all_gather8(mod_sh, name="gather_mod")
    mod_all = jnp.concatenate([mod_parts[2 * s] for s in range(NCHIP)], axis=1)
    mod_lat = lax.dynamic_index_in_dim(mod_all, dev, axis=0, keepdims=False).reshape(9, D)
    modv = jnp.zeros((2, 16, D), F32).at[0, :9].set(mod_all[8].reshape(9, D)).at[1, :9].set(mod_lat)

    big = ("ffn1_w_in", "ffn1_w_out", "w_mix_in", "w_gdn_proj", "w_pool_proj", "w_mix_out", "ffn2_w_in", "ffn2_w_out")
    shard = {k: w[k][0].astype(BF) for k in big}
    w1_in, w1_out = _gather_split([shard["ffn1_w_in"], shard["ffn1_w_out"]], name="gather_ffn1")
    p = dict(
        norm1=norm1_w, norm2=norm2_w, norm3=norm3_w, fnorm=final_norm_w[None], w1_in=w1_in, w1_out=w1_out.reshape(FF, D),
        cst=jnp.zeros((8, 128), F32).at[0, 2 * NH:4 * NH].set(jnp.exp(a_log).reshape(-1)).at[1, 2 * NH:4 * NH].set(dt_bias.reshape(-1)),
        gnw=gdn_norm_w, pool_w=pool_w[0], pscale=pool_scale)
    late = ([shard["w_mix_in"], conv_w[0]], [shard["ffn2_w_in"], shard["ffn2_w_out"]],
            [shard["w_gdn_proj"], shard["w_pool_proj"], shard["w_mix_out"]])

    xc = jnp.concatenate([ctx[0], x[0]], axis=0)
    loss_dev, dx_lat, g, landed, small, dmod = _local_step(xc, loss_target[0], modv, p, late)
    loss = lax.psum(loss_dev, ("x", "y", "c"))
    grad_x = dx_lat[None]

    small_vals = [dmod[1], dmod[0], small["norm1"], small["norm2"], small["norm3"], small["fnorm"], small["a_log"], small["dt_bias"],
                  small["gnw"], small["pool_w"], small["pscale"], small["conv"]]
    small_sizes = [v.size for v in small_vals]
    packed = _all_gather8(_pack(small_vals), name="gather_small")
    tot = _unpack(_sum_slots(packed, name="sum_small"), small_sizes)
    dmod_lat_all = packed[:, :9 * D // 128, :].reshape(8, 9 * D)
    dm = jnp.concatenate([dmod_lat_all, tot[1][None], jnp.zeros((7, 9 * D), F32)], axis=0)
    dm_sh = lax.dynamic_slice(dm, (0, chip * ada_cols), (16, ada_cols))
    g_w_ada, cctx_part = _ada_bwd(c_all, dm_sh, w_ada[0], name="ada_bwd")
    g_c_ctx = _cctx_grad(_all_gather8(cctx_part, name="gather_cctx"), c_ctx[None], name="cctx_grad")[0]
    conv_tot = tot[11].reshape(5, NQKV)
    g_small = dict(c_ctx=g_c_ctx, b_ada=tot[0] + tot[1], norm1_w=tot[2], norm2_w=tot[3], norm3_w=tot[4], final_norm_w=tot[5],
                   a_log=tot[6], dt_bias=tot[7], gdn_norm_w=tot[8], pool_w=tot[9], pool_scale=tot[10],
                   conv_w=lax.dynamic_slice(conv_tot, (0, chip * (NQKV // NCHIP)), (5, NQKV // NCHIP)))

    first = ("ffn1_w_in", "ffn1_w_out")
    order = dict(zip(big, ("w1_in", "w1_out", "w_mix", "w_gdn", "w_pool", "w_mo", "w2_in", "w2_out")))
    rest = [k for k in big if k not in first]
    send_sems, recv_sems, sent, lands, token = _scatter_start([g["w1_in"], g["w1_out"]], [g_c_ctx, g_w_ada], name="scatter_ffn1_start")
    mine = {k: _sum_slots(landed[order[k]], name=f"sum_{k}", after=[token]) for k in rest}
    theirs = dict(zip(rest, _core_swap([mine[k] for k in rest], name="swap_grad_sums")))

    out = {}
    as2d = lambda a: a.reshape(-1, a.shape[-1])

    def update(k):
        res = _adamw(as2d(w[k]), as2d(mine[k]), as2d(theirs[k]), as2d(mom[k]), as2d(var[k]), name=f"adamw_{k}")
        out[k] = [r.reshape(w[k].shape) for r in res]

    for k in rest:
        update(k)
    out["w_ada"] = [r.reshape(w_ada.shape) for r in _adamw(w_ada[0], g_w_ada, None, m_w_ada[0], v_w_ada[0], name="adamw_w_ada")]
    sm_names = [n for n, _ in _SMALL]
    sm_sizes = [n for _, n in _SMALL]
    res = _adamw(_pack([w[k] for k in sm_names]), _pack([g_small[k] for k in sm_names]), None,
                 _pack([mom[k] for k in sm_names]), _pack([var[k] for k in sm_names]), name="adamw_small")
    done = [out[k][1] for k in rest] + [out["w_ada"][1], res[1]]
    res = [_unpack(r, sm_sizes) for r in res]
    for i, k in enumerate(sm_names):
        out[k] = [r[i].reshape(w[k].shape) for r in res]
    sent, lands = _scatter_wait(send_sems, recv_sems, sent, lands, done, name="scatter_ffn1_wait")
    for k, part, land in zip(first, sent, lands):
        own = lax.dynamic_slice_in_dim(part, chip, 1, axis=0)
        mine[k] = _sum_slots(lax.dynamic_update_slice_in_dim(land, own, chip, axis=0), name=f"sum_{k}")
    theirs.update(zip(first, _core_swap([mine[k] for k in first], name="swap_ffn1_sums")))
    for k in first:
        update(k)
    return (loss, grad_x, *[out[k][0] for k in names], *[out[k][1] for k in names], *[out[k][2] for k in names],
            *[out[k][3] for k in names])
```

```python
import functools

import jax
import jax.numpy as jnp
from jax import lax
from jax.experimental import pallas as pl
from jax.experimental.pallas import tpu as pltpu

F32 = jnp.float32
BF = jnp.bfloat16

D = 1024
FF = 2816
NH = 8
HD = 128
CH = 64
GW = 64
TM = 256
NQKV = 3 * NH * HD
NPOOL = 512
POOL_WINDOWS = (2, 4, 8, 16)
NMIX = 6688
NMIXP = 6784
EPS = 1e-6
NCHIP = 4
VMEM_LIMIT = 56 * 1024 * 1024

ADAM_LR, ADAM_B1, ADAM_B2, ADAM_EPS, ADAM_WD, ADAM_STEP = 0.001, 0.9, 0.999, 1e-08, 0.01, 10


def _cparams(*sem):
    return pltpu.CompilerParams(dimension_semantics=sem, vmem_limit_bytes=VMEM_LIMIT)


def _const_spec(shape):
    nd = len(shape)
    return pl.BlockSpec(shape, lambda *_: (0,) * nd, pipeline_mode=pl.Buffered(1))


def _dot(a, b, dims):
    return lax.dot_general(a.astype(BF), b.astype(BF), (dims, ((), ())), preferred_element_type=F32)


def _nn(a, b):
    return _dot(a, b, ((1,), (0,)))


def _nt(a, b):
    return _dot(a, b, ((1,), (1,)))


def _tn(a, b):
    return _dot(a, b, ((0,), (0,)))


def _silu(x):
    return x * jax.nn.sigmoid(x)


def _dsilu(x):
    s = jax.nn.sigmoid(x)
    return s * (1.0 + x * (1.0 - s))


def _norm_mod(x, nw, shift, scale):
    r = lax.rsqrt(jnp.mean(x * x, axis=-1, keepdims=True) + EPS)
    xh = x * r
    n = xh * nw
    return n * (1.0 + scale) + shift, n, xh, r


def _norm_mod_bwd(dh, n, xh, r, nw, scale):
    dn = dh * (1.0 + scale)
    dxh = dn * nw
    dx = r * (dxh - xh * jnp.mean(dxh * xh, axis=-1, keepdims=True))
    rs = lambda t: jnp.sum(t, axis=0, keepdims=True)
    return dx, rs(dh), rs(dh * n), rs(dn * xh)


def _ffn_fwd(x, modv, nw, w_in4, w_out, *, mrow, name, exchange=None):
    n_tok = x.shape[0]
    nt = n_tok // TM
    nset = modv.shape[0]
    ws = w_in4.shape[2]

    def body(x_ref, mod_ref, nw_ref, win_ref, wout_ref, x1_ref, h_ref, gu_ref, f_ref):
        xv = x_ref[...]
        shift, scale, gate = mod_ref[0, mrow:mrow + 1, :], mod_ref[0, mrow + 1:mrow + 2, :], mod_ref[0, mrow + 2:mrow + 3, :]
        h, _, _, _ = _norm_mod(xv, nw_ref[...], shift, scale)
        hb = h.astype(BF)
        h_ref[...] = hb
        gus = [_nn(hb, win_ref[s]) for s in range(NCHIP)]
        for s in range(NCHIP):
            gu_ref[:, s * ws:(s + 1) * ws] = gus[s].astype(BF)
        g = jnp.concatenate(gus[:2], axis=1)
        u = jnp.concatenate(gus[2:], axis=1)
        f = _nn(_silu(g) * u, wout_ref[...])
        f_ref[...] = f.astype(BF)
        x1_ref[...] = xv + 0.5 * gate * f

    tile = lambda w: pl.BlockSpec((TM, w), lambda i: (i, 0))
    return _pallas(
        body, (x, modv, nw, w_in4, w_out), name=name, grid=(nt,), sem=("parallel",), exchange=exchange,
        in_specs=[tile(D), pl.BlockSpec((1, 16, D), lambda i: (jnp.minimum(i, nset - 1), 0, 0)), _const_spec((1, D)),
                  _const_spec(w_in4.shape), _const_spec(w_out.shape)],
        out_specs=[tile(D), tile(D), tile(2 * FF), tile(D)],
        out_shape=[jax.ShapeDtypeStruct((n_tok, D), F32), jax.ShapeDtypeStruct((n_tok, D), BF),
                   jax.ShapeDtypeStruct((n_tok, 2 * FF), BF), jax.ShapeDtypeStruct((n_tok, D), BF)])


def _ffn_bwd(dxo, x, gu, fo, modv, nw, w_in4, w_out, *, mrow, dx_skip, name, exchange=None):
    n_tok = x.shape[0]
    nt = n_tok // TM
    nset = modv.shape[0]
    ws = w_in4.shape[2]

    def body(dxo_ref, x_ref, gu_ref, f_ref, mod_ref, nw_ref, win_ref, wout_ref, dx_ref, a_ref, df_ref, dgu_ref, acc_ref):
        i = pl.program_id(0)
        xv = x_ref[...]
        dxo_v = dxo_ref[...]
        shift, scale, gate = mod_ref[0, mrow:mrow + 1, :], mod_ref[0, mrow + 1:mrow + 2, :], mod_ref[0, mrow + 2:mrow + 3, :]
        _, n, xh, r = _norm_mod(xv, nw_ref[...], shift, scale)
        df = 0.5 * gate * dxo_v
        dfb = df.astype(BF)
        df_ref[...] = dfb
        dgate = jnp.sum(0.5 * dxo_v * f_ref[...].astype(F32), axis=0, keepdims=True)
        da = _nt(dfb, wout_ref[...])
        g = gu_ref[:, :FF].astype(F32)
        u = gu_ref[:, FF:].astype(F32)
        sig = jax.nn.sigmoid(g)
        sg = g * sig
        a_ref[...] = (sg * u).astype(BF)
        dgu_ref[:, :FF] = (da * u * (sig * (1.0 + g - sg))).astype(BF)
        dgu_ref[:, FF:] = (da * sg).astype(BF)
        dh = _nt(dgu_ref[:, 0:ws], win_ref[0])
        for s in range(1, NCHIP):
            dh = dh + _nt(dgu_ref[:, s * ws:(s + 1) * ws], win_ref[s])
        dx, dshift, dscale, dnw = _norm_mod_bwd(dh, n, xh, r, nw_ref[...], scale)
        dx_ref[...] = dxo_v + dx

        @pl.when((i == 0) | (i == nset - 1))
        def _():
            acc_ref[...] = jnp.zeros_like(acc_ref)

        acc_ref[0, 0:1, :] += dshift
        acc_ref[0, 1:2, :] += dscale
        acc_ref[0, 2:3, :] += dgate
        acc_ref[0, 3:4, :] += dnw

    tile = lambda w: pl.BlockSpec((TM, w), lambda i: (i, 0))
    return _pallas(
        body, (dxo, x, gu, fo, modv, nw, w_in4, w_out), name=name, grid=(nt,), sem=("arbitrary",), exchange=exchange,
        in_specs=[tile(D), tile(D), tile(2 * FF), tile(D),
                  pl.BlockSpec((1, 16, D), lambda i: (jnp.minimum(i, nset - 1), 0, 0)), _const_spec((1, D)),
                  _const_spec(w_in4.shape), _const_spec(w_out.shape)],
        out_specs=[pl.BlockSpec((TM, D), lambda i: (jnp.maximum(i - dx_skip, 0), 0)), tile(FF), tile(D), tile(2 * FF),
                   pl.BlockSpec((1, 8, D), lambda i: (jnp.minimum(i, nset - 1), 0, 0))],
        out_shape=[jax.ShapeDtypeStruct((n_tok - dx_skip * TM, D), F32), jax.ShapeDtypeStruct((n_tok, FF), BF),
                   jax.ShapeDtypeStruct((n_tok, D), BF), jax.ShapeDtypeStruct((n_tok, 2 * FF), BF),
                   jax.ShapeDtypeStruct((nset, 8, D), F32)])


def _k_tile(n, target=3072):
    return max(t for t in range(TM, min(n, target) + 1, TM) if n % t == 0)


def _matmul_tn(a, b, *, tmm, tn, tk, nsplit=1, name):
    n_tok, m = a.shape
    kk = b.shape[1]
    nk = n_tok // tk

    def body(a_ref, b_ref, o_ref, acc):
        k = pl.program_id(2)

        @pl.when(k == 0)
        def _():
            acc[...] = jnp.zeros_like(acc)

        acc[...] += _tn(a_ref[...], b_ref[...])

        @pl.when(k == nk - 1)
        def _():
            o_ref[...] = acc[...].astype(BF).reshape(o_ref.shape)

    if nsplit == 1:
        out_shape = jax.ShapeDtypeStruct((m, kk), BF)
        out_spec = pl.BlockSpec((tmm, tn), lambda i, j, k: (i, j))
    else:
        assert tn == kk // nsplit
        out_shape = jax.ShapeDtypeStruct((nsplit, m, tn), BF)
        out_spec = pl.BlockSpec((1, tmm, tn), lambda i, j, k: (j, i, 0))
    return pl.pallas_call(
        body, name=name, grid=(m // tmm, kk // tn, nk),
        in_specs=[pl.BlockSpec((tk, tmm), lambda i, j, k: (k, i)), pl.BlockSpec((tk, tn), lambda i, j, k: (k, j))],
        out_specs=out_spec, out_shape=out_shape,
        scratch_shapes=[pltpu.VMEM((tmm, tn), F32)],
        compiler_params=_cparams("parallel", "parallel", "arbitrary"),
    )(a, b)


_MIX_PARTS = (("qkv", 0, NQKV), ("gate", NQKV, 1024), ("pool", NQKV + 1024, NPOOL), ("br", NQKV + 1024 + NPOOL, 2048),
              ("ab", NMIXP - 128, 128))


def _mix_in_fwd(x1, modv, nw, w_mix, *, name, exchange=None):
    n_tok = x1.shape[0]

    def body(x_ref, mod_ref, nw_ref, w_ref, u_ref, *p_refs):
        u, _, _, _ = _norm_mod(x_ref[...], nw_ref[...], mod_ref[0, 3:4, :], mod_ref[0, 4:5, :])
        ub = u.astype(BF)
        u_ref[...] = ub
        for (_, c0, w), p_ref in zip(_MIX_PARTS, p_refs):
            p_ref[...] = _nn(ub, w_ref[:, c0:c0 + w])

    tile = lambda w: pl.BlockSpec((TM, w), lambda i: (i, 0))
    ctile = lambda w: pl.BlockSpec((TM, w), lambda i: (i + 1, 0))
    return _pallas(
        body, (x1, modv, nw, w_mix), name=name, grid=(n_tok // TM,), sem=("parallel",), exchange=exchange,
        in_specs=[tile(D), pl.BlockSpec((1, 16, D), lambda i: (jnp.minimum(i, 1), 0, 0)), _const_spec((1, D)),
                  _const_spec(w_mix.shape)],
        out_specs=[tile(D)] + [tile(w) for _, _, w in _MIX_PARTS],
        out_shape=[jax.ShapeDtypeStruct((n_tok, D), BF)] + [jax.ShapeDtypeStruct((n_tok, w), F32) for _, _, w in _MIX_PARTS])


def _mix_in_bwd(dxo, x1, dqkv, dgate, dpool, dbr, dab_f, dab_b, modv, nw, w_mix, *, name):
    n_tok = x1.shape[0]

    def body(dxo_ref, x_ref, dqkv_ref, dgate_ref, dpool_ref, dbr_ref, dabf_ref, dabb_ref, mod_ref, nw_ref, w_ref,
             dx_ref, dp_ref, acc_ref):
        i = pl.program_id(0)
        lat = i >= 1
        scale = mod_ref[0, 4:5, :]
        _, n, xh, r = _norm_mod(x_ref[...], nw_ref[...], mod_ref[0, 3:4, :], scale)
        dp_ref[:, 0:NQKV] = dqkv_ref[...].astype(BF)
        dp_ref[:, NQKV:NQKV + 1024] = jnp.where(lat, dgate_ref[...], 0.0).astype(BF)
        dp_ref[:, NQKV + 1024:NQKV + 1536] = jnp.where(lat, dpool_ref[...], 0.0).astype(BF)
        dp_ref[:, NQKV + 1536:NMIXP - 128] = jnp.where(lat, dbr_ref[...], 0.0).astype(BF)
        dp_ref[:, NMIXP - 128:] = (dabf_ref[...] + dabb_ref[...]).astype(BF)
        du = _nt(dp_ref[...], w_ref[...])
        dx, dshift, dscale, dnw = _norm_mod_bwd(du, n, xh, r, nw_ref[...], scale)
        dx_ref[...] = jnp.where(lat, dxo_ref[...], 0.0) + dx

        @pl.when(i <= 1)
        def _():
            acc_ref[...] = jnp.zeros_like(acc_ref)

        acc_ref[0, 0:1, :] += dshift
        acc_ref[0, 1:2, :] += dscale
        acc_ref[0, 3:4, :] += dnw

    tile = lambda w: pl.BlockSpec((TM, w), lambda i: (i, 0))
    ltile = lambda w: pl.BlockSpec((TM, w), lambda i: (jnp.maximum(i - 1, 0), 0))
    return pl.pallas_call(
        body, name=name, grid=(n_tok // TM,),
        in_specs=[ltile(D), tile(D), tile(NQKV), ltile(1024), ltile(NPOOL), ltile(2048), tile(128), tile(128),
                  pl.BlockSpec((1, 16, D), lambda i: (jnp.minimum(i, 1), 0, 0)), _const_spec((1, D)), _const_spec(w_mix.shape)],
        out_specs=[tile(D), tile(NMIXP), pl.BlockSpec((1, 8, D), lambda i: (jnp.minimum(i, 1), 0, 0))],
        out_shape=[jax.ShapeDtypeStruct((n_tok, D), F32), jax.ShapeDtypeStruct((n_tok, NMIXP), BF),
                   jax.ShapeDtypeStruct((2, 8, D), F32)],
        compiler_params=_cparams("arbitrary"),
    )(dxo, x1, dqkv, dgate, dpool, dbr, dab_f, dab_b, modv, nw, w_mix)


def _qkv_act(pre, part):
    s = _silu(pre)
    if part == 2:
        return s
    nrm = s * lax.rsqrt(jnp.sum(s * s, axis=-1, keepdims=True) + EPS)
    return nrm * HD ** -0.5 if part == 0 else nrm


def _halo_specs(nt):
    r = TM // 8
    main = pl.BlockSpec((TM, NQKV), lambda i: (i, 0))
    prev = pl.BlockSpec((8, NQKV), lambda i: (jnp.maximum(i * r - 1, 0), 0))
    nxt = pl.BlockSpec((8, NQKV), lambda i: (jnp.minimum((i + 1) * r, nt * r - 1), 0))
    return main, prev, nxt


def _prep_fwd(p_qkv, conv_w8, *, name):
    n_tok = p_qkv.shape[0]
    nt = n_tok // TM

    def body(x_ref, xp_ref, xn_ref, w_ref, o_ref, pre_ref, win):
        i = pl.program_id(0)
        has_prev = (i != 0) & (i != 1)
        has_next = (i != 0) & (i != nt - 1)
        win[0:8, :] = jnp.where(has_prev, xp_ref[...], 0.0)
        win[8:8 + TM, :] = x_ref[...]
        win[8 + TM:, :] = jnp.where(has_next, xn_ref[...], 0.0)
        for hb in range(3 * NH):
            hs = slice(hb * HD, (hb + 1) * HD)
            pre = win[6:6 + TM, hs] * w_ref[0:1, hs]
            for k in range(1, 5):
                pre = pre + win[6 + k:6 + k + TM, hs] * w_ref[k:k + 1, hs]
            pre_ref[:, hs] = pre
            o_ref[:, hs] = _qkv_act(pre, hb // NH)

    main, prev, nxt = _halo_specs(nt)
    return pl.pallas_call(
        body, name=name, grid=(nt,),
        in_specs=[main, prev, nxt, pl.BlockSpec((8, NQKV), lambda i: (0, 0))],
        out_specs=[main, main], out_shape=[jax.ShapeDtypeStruct((n_tok, NQKV), F32)] * 2,
        scratch_shapes=[pltpu.VMEM((TM + 16, NQKV), F32)],
        compiler_params=_cparams("parallel"),
    )(p_qkv, p_qkv, p_qkv, conv_w8)


def _prep_bwd(p_qkv, pre, dqkv_f, dqkv_b, conv_w8, *, name):
    n_tok = p_qkv.shape[0]
    nt = n_tok // TM

    def body(x_ref, p_ref, pp_ref, pn_ref, g_ref, gp_ref, gn_ref, g2_ref, g2p_ref, g2n_ref, w_ref, dx_ref, dw_ref, pwin, gwin, dwin):
        i = pl.program_id(0)
        has_prev = (i != 0) & (i != 1)
        has_next = (i != 0) & (i != nt - 1)
        pwin[0:8, :] = jnp.where(has_prev, pp_ref[...], 0.0)
        pwin[8:8 + TM, :] = p_ref[...]
        pwin[8 + TM:, :] = jnp.where(has_next, pn_ref[...], 0.0)
        gwin[0:8, :] = jnp.where(has_prev, gp_ref[...] + g2p_ref[...], 0.0)
        gwin[8:8 + TM, :] = g_ref[...] + g2_ref[...]
        gwin[8 + TM:, :] = jnp.where(has_next, gn_ref[...] + g2n_ref[...], 0.0)

        @pl.when(i == 0)
        def _():
            dw_ref[...] = jnp.zeros_like(dw_ref)

        for hb in range(3 * NH):
            hs = slice(hb * HD, (hb + 1) * HD)
            _, vjp = jax.vjp(functools.partial(_qkv_act, part=hb // NH), pwin[:, hs])
            dwin[:, hs] = vjp(gwin[:, hs])[0]
            xv = x_ref[:, hs]
            dx = None
            for k in range(5):
                sh = dwin[10 - k:10 - k + TM, hs]
                dx = sh * w_ref[k:k + 1, hs] if dx is None else dx + sh * w_ref[k:k + 1, hs]
                dw_ref[k:k + 1, hs] += jnp.sum(sh * xv, axis=0, keepdims=True)
            dx_ref[:, hs] = dx

    main, prev, nxt = _halo_specs(nt)
    wspec = pl.BlockSpec((8, NQKV), lambda i: (0, 0))
    return pl.pallas_call(
        body, name=name, grid=(nt,),
        in_specs=[main, main, prev, nxt, main, prev, nxt, main, prev, nxt, wspec],
        out_specs=[main, wspec],
        out_shape=[jax.ShapeDtypeStruct((n_tok, NQKV), F32), jax.ShapeDtypeStruct((8, NQKV), F32)],
        scratch_shapes=[pltpu.VMEM((TM + 16, NQKV), F32)] * 3,
        compiler_params=_cparams("arbitrary"),
    )(p_qkv, pre, pre, pre, dqkv_f, dqkv_f, dqkv_f, dqkv_b, dqkv_b, dqkv_b, conv_w8)


@jax.custom_vjp
def _mm_nn(a, b):
    return _nn(a, b)


@jax.custom_vjp
def _mm_nt(a, b):
    return _nt(a, b)


@jax.custom_vjp
def _mm_tn(a, b):
    return _tn(a, b)


_mm_nn.defvjp(lambda a, b: (_nn(a, b), (a, b)), lambda r, g: (_mm_nt(g, r[1]), _mm_tn(r[0], g)))
_mm_nt.defvjp(lambda a, b: (_nt(a, b), (a, b)), lambda r, g: (_mm_nn(g, r[1]), _mm_tn(g, r[0])))
_mm_tn.defvjp(lambda a, b: (_tn(a, b), (a, b)), lambda r, g: (_mm_nt(r[1], g), _mm_nn(r[0], g)))


def _each(f, *lists):
    return tuple(f(*a) for a in zip(*lists))


def _unit_tri_inv(ls, revs):
    ii = lax.broadcasted_iota(jnp.int32, (CH, CH), 0)
    jj = lax.broadcasted_iota(jnp.int32, (CH, CH), 1)
    eye = (ii == jj).astype(F32)
    xs = None
    s = 1
    while s < CH:
        same = (ii & -(2 * s)) == (jj & -(2 * s))
        off = {False: same & ((ii & s) != 0) & ((jj & s) == 0), True: same & ((jj & s) != 0) & ((ii & s) == 0)}
        cs = _each(lambda l, r: jnp.where(off[r], l, 0.0), ls, revs)
        if xs is None:
            xs = _each(lambda c: eye - c, cs)
        else:
            xc = _each(_nn, xs, cs)
            xcx = _each(_nn, xc, xs)
            xs = _each(lambda x, t: x - t, xs, xcx)
        s *= 2
    return xs


@functools.lru_cache(maxsize=None)
def _tri_solve(revs):
    @jax.custom_vjp
    def solve(ls, rhss):
        return _each(_mm_nn, _unit_tri_inv(ls, revs), rhss)

    def fwd(ls, rhss):
        ainv = _unit_tri_inv(ls, revs)
        xs = _each(_mm_nn, ainv, rhss)
        return xs, (ainv, xs)

    def bwd(res, gs):
        ainv, xs = res
        drhs = _each(_mm_tn, ainv, gs)
        return _each(lambda d, x: -_mm_nt(d, x), drhs, xs), drhs

    solve.defvjp(fwd, bwd)
    return solve


def _chunk_prep(q, k, v, beta, g, *, revs):
    ii = lax.broadcasted_iota(jnp.int32, (CH, CH), 0)
    jj = lax.broadcasted_iota(jnp.int32, (CH, CH), 1)
    eye = ii == jj
    incl_of = {False: ii >= jj, True: ii <= jj}
    strict_of = {False: ii > jj, True: ii < jj}
    g_row = _each(lambda t: jnp.sum(jnp.where(eye, t, 0.0), axis=0, keepdims=True), g)
    cum = _each(lambda t, r: jnp.sum(jnp.where(incl_of[r], t, 0.0), axis=1, keepdims=True), g_row, revs)
    cum_row = _each(lambda t: jnp.sum(jnp.where(eye, t, 0.0), axis=0, keepdims=True), cum)
    total = _each(lambda t: jnp.sum(t, axis=0, keepdims=True), g)
    decay = _each(lambda c, cr, r: jnp.where(incl_of[r], jnp.exp(jnp.where(incl_of[r], c - cr, 0.0)), 0.0), cum, cum_row, revs)
    kb = _each(jnp.multiply, k, beta)
    vb = _each(jnp.multiply, v, beta)
    kk = _each(_mm_nt, kb, k)
    lmat = _each(lambda t, dc, r: jnp.where(strict_of[r], t * dc, 0.0), kk, decay, revs)
    ecum = _each(jnp.exp, cum)
    rhs = _each(lambda a, b, e: jnp.concatenate([a, b * e], axis=1), vb, kb, ecum)
    sol = _tri_solve(revs)(lmat, rhs)
    qk = _each(_mm_nt, q, k)
    aqk = _each(jnp.multiply, qk, decay)
    qd = _each(jnp.multiply, q, ecum)
    kd = _each(lambda a, t, c: a * jnp.exp(t - c), k, total, cum)
    return sol, aqk, qd, kd, _each(jnp.exp, total)


def _chunk_rec(sol, aqk, qd, kd, bl, s):
    ws = _each(lambda so, st: _mm_nn(so[:, HD:], st), sol, s)
    v_new = _each(lambda so, t: so[:, :HD] - t, sol, ws)
    qs = _each(_mm_nn, qd, s)
    av = _each(_mm_nn, aqk, v_new)
    o = _each(jnp.add, qs, av)
    kv = _each(_mm_tn, kd, v_new)
    s_new = _each(lambda st, b, u: st * b + u, s, bl, kv)
    return o, s_new


def _lane_col(x, c):
    lane = lax.broadcasted_iota(jnp.int32, x.shape, 1)
    return jnp.sum(jnp.where(lane == c, x, 0.0), axis=1, keepdims=True)


def _beta_g(ab, cst, d, h):
    braw = _lane_col(ab, NH * d + h)
    araw = _lane_col(ab, 2 * NH + NH * d + h)
    ea = _lane_col(cst[0:1, :], 2 * NH + NH * d + h)
    dt = _lane_col(cst[1:2, :], 2 * NH + NH * d + h)
    z = araw + dt
    softplus = jnp.maximum(z, 0.0) + jnp.log(1.0 + jnp.exp(-jnp.abs(z)))
    return jax.nn.sigmoid(braw), -ea * softplus, z, ea


STEPS = 2
TS = STEPS * CH
_CHAINS = tuple((t, d, h) for t in range(STEPS) for d in (0, 1) for h in range(NH))
_REVS = tuple(bool(d) for _, d, _ in _CHAINS)
_PER_STEP = 2 * NH


def _chain_inputs(refs, r0s, ab_refs, cst):
    hs = lambda h: slice(h * HD, (h + 1) * HD)
    abvs = [[ab_refs[d][pl.ds(r0s[t][d], CH), :] for d in (0, 1)] for t in range(STEPS)]
    q = _each(lambda c: refs[c[1]][0][pl.ds(r0s[c[0]][c[1]], CH), hs(c[2])], _CHAINS)
    k = _each(lambda c: refs[c[1]][1][pl.ds(r0s[c[0]][c[1]], CH), hs(c[2])], _CHAINS)
    v = _each(lambda c: refs[c[1]][2][pl.ds(r0s[c[0]][c[1]], CH), hs(c[2])], _CHAINS)
    bg = _each(lambda c: _beta_g(abvs[c[0]][c[1]], cst, c[1], c[2]), _CHAINS)
    return q, k, v, bg


def _of_step(parts, t):
    return tuple(p[t * _PER_STEP:(t + 1) * _PER_STEP] for p in parts)


def _scan_fwd(qkv, ab, cst, s0, *, row_blk0, nb, name, exchange=None):
    cb = TS // CH
    w = NH * HD

    def body(qf, kf, vf, abf, qb, kb, vb, abb, cst_ref, s0_ref, of_ref, ob_ref, sallf_ref, sallb_ref, sfin_ref, s_scr):
        i = pl.program_id(0)

        @pl.when(i == 0)
        def _():
            s_scr[...] = s0_ref[...]

        o_refs, sall_refs = (of_ref, ob_ref), (sallf_ref, sallb_ref)

        def chunks(ci, carry):
            cs = [(ci * STEPS + t, cb - 1 - ci * STEPS - t) for t in range(STEPS)]
            r0s = [tuple(pl.multiple_of(c * CH, CH) for c in ct) for ct in cs]
            q, k, v, bg = _chain_inputs(((qf, kf, vf), (qb, kb, vb)), r0s, (abf, abb), cst_ref[...])
            parts = _chunk_prep(q, k, v, _each(lambda t: t[0], bg), _each(lambda t: t[1], bg), revs=_REVS)
            s = _each(lambda c: s_scr[c[1], c[2]], _CHAINS[:_PER_STEP])
            for t in range(STEPS):
                for (_, d, h), sv in zip(_CHAINS, s):
                    sall_refs[d][cs[t][d], h] = sv
                o, s = _chunk_rec(*_of_step(parts, t), s)
                for (_, d, h), ov in zip(_CHAINS, o):
                    o_refs[d][pl.ds(r0s[t][d], CH), h * HD:(h + 1) * HD] = ov
            for (_, d, h), sv in zip(_CHAINS, s):
                s_scr[d, h] = sv
            return carry

        lax.fori_loop(0, cb // STEPS, chunks, 0)

        @pl.when(i == nb - 1)
        def _():
            sfin_ref[...] = s_scr[...]

    pos = (lambda i: i, lambda i: nb - 1 - i)
    col = lambda d, c: pl.BlockSpec((TS, w), lambda i: (row_blk0 + pos[d](i), c))
    abs_ = lambda d: pl.BlockSpec((TS, 128), lambda i: (row_blk0 + pos[d](i), 0))
    full4 = pl.BlockSpec((2, NH, HD, HD), lambda i: (0, 0, 0, 0))
    o_spec = lambda d: pl.BlockSpec((TS, w), lambda i: (pos[d](i), 0))
    sall_spec = lambda d: pl.BlockSpec((cb, NH, HD, HD), lambda i: (pos[d](i), 0, 0, 0))
    return _pallas(
        body, (qkv, qkv, qkv, ab, qkv, qkv, qkv, ab, cst, s0), name=name, grid=(nb,), sem=("arbitrary",), exchange=exchange,
        in_specs=[col(0, 0), col(0, 1), col(0, 2), abs_(0), col(1, 0), col(1, 1), col(1, 2), abs_(1),
                  pl.BlockSpec((8, 128), lambda i: (0, 0)), full4],
        out_specs=[o_spec(0), o_spec(1), sall_spec(0), sall_spec(1), full4],
        out_shape=[jax.ShapeDtypeStruct((nb * TS, w), F32)] * 2 + [jax.ShapeDtypeStruct((nb * cb, NH, HD, HD), F32)] * 2
        + [jax.ShapeDtypeStruct((2, NH, HD, HD), F32)],
        scratch_shapes=[pltpu.VMEM((2, NH, HD, HD), F32)])


def _scan_bwd(qkv, ab, cst, sall_f, sall_b, do, dsfin, dqkv_f, dqkv_b, dab_f, dab_b, dcst, *, row_blk0, nb, has_do, name,
              exchange=None):
    cb = TS // CH
    w = NH * HD

    def body(qf, kf, vf, abf, qb, kb, vb, abb, cst_ref, sallf_ref, sallb_ref, dof_ref, dob_ref, dsfin_ref, _f, _b, _af, _ab, dcst_in,
             dqkvf_ref, dqkvb_ref, dabf_ref, dabb_ref, dcst_ref, ds0_ref, ds_scr):
        i = pl.program_id(0)

        @pl.when(i == 0)
        def _():
            ds_scr[...] = dsfin_ref[...]
            dcst_ref[...] = dcst_in[...]

        lane = lax.broadcasted_iota(jnp.int32, (CH, 128), 1)
        lane1 = lax.broadcasted_iota(jnp.int32, (1, 128), 1)
        sall_refs, do_refs = (sallf_ref, sallb_ref), (dof_ref, dob_ref)
        dqkv_refs, dab_refs = (dqkvf_ref, dqkvb_ref), (dabf_ref, dabb_ref)

        def chunks(ci, carry):
            cs = [(cb - 1 - ci * STEPS - t, ci * STEPS + t) for t in range(STEPS)]
            r0s = [tuple(pl.multiple_of(c * CH, CH) for c in ct) for ct in cs]
            q, k, v, bg = _chain_inputs(((qf, kf, vf), (qb, kb, vb)), r0s, (abf, abb), cst_ref[...])
            beta, g = _each(lambda t: t[0], bg), _each(lambda t: t[1], bg)
            parts, prep_vjp = jax.vjp(functools.partial(_chunk_prep, revs=_REVS), q, k, v, beta, g)
            ds = _each(lambda c: ds_scr[c[1], c[2]], _CHAINS[:_PER_STEP])
            dparts = []
            for t in range(STEPS):
                s = _each(lambda c: sall_refs[c[1]][cs[t][c[1]], c[2]], _CHAINS[:_PER_STEP])
                _, rec_vjp = jax.vjp(_chunk_rec, *_of_step(parts, t), s)
                do_t = _each(lambda c: do_refs[c[1]][pl.ds(r0s[t][c[1]], CH), c[2] * HD:(c[2] + 1) * HD] if has_do
                             else jnp.zeros((CH, HD), F32), _CHAINS[:_PER_STEP])
                *dpt, ds = rec_vjp((do_t, ds))
                dparts.append(dpt)
            for (_, d, h), dsv in zip(_CHAINS, ds):
                ds_scr[d, h] = dsv
            dq, dk, dv, dbeta, dg = prep_vjp(tuple(sum((dparts[t][j] for t in range(STEPS)), ()) for j in range(len(dparts[0]))))
            dab = [[jnp.zeros((CH, 128), F32), jnp.zeros((CH, 128), F32)] for _ in range(STEPS)]
            dal = jnp.zeros((1, 128), F32)
            for n, (t, d, h) in enumerate(_CHAINS):
                for part, val in enumerate((dq[n], dk[n], dv[n])):
                    dqkv_refs[d][pl.ds(r0s[t][d], CH), part * w + h * HD:part * w + (h + 1) * HD] = val
                z, ea = bg[n][2], bg[n][3]
                dbraw = dbeta[n] * beta[n] * (1.0 - beta[n])
                daraw = dg[n] * (-ea) * jax.nn.sigmoid(z)
                dab[t][d] = dab[t][d] + jnp.where(lane == NH * d + h, dbraw, 0.0) + jnp.where(lane == 2 * NH + NH * d + h, daraw, 0.0)
                dal = dal + jnp.where(lane1 == 2 * NH + NH * d + h, jnp.sum(dg[n] * g[n], axis=0, keepdims=True), 0.0)
            dsum = jnp.zeros((CH, 128), F32)
            for t in range(STEPS):
                for d in (0, 1):
                    dab_refs[d][pl.ds(r0s[t][d], CH), :] = dab[t][d]
                    dsum = dsum + dab[t][d]
            dcst_ref[0:1, :] += dal
            dcst_ref[1:2, :] += jnp.sum(jnp.where(lane >= 2 * NH, dsum, 0.0), axis=0, keepdims=True)
            return carry

        lax.fori_loop(0, cb // STEPS, chunks, 0)

        @pl.when(i == nb - 1)
        def _():
            ds0_ref[...] = ds_scr[...]

    pos = (lambda i: nb - 1 - i, lambda i: i)
    col = lambda d, c: pl.BlockSpec((TS, w), lambda i: (row_blk0 + pos[d](i), c))
    abs_ = lambda d: pl.BlockSpec((TS, 128), lambda i: (row_blk0 + pos[d](i), 0))
    full4 = pl.BlockSpec((2, NH, HD, HD), lambda i: (0, 0, 0, 0))
    small = pl.BlockSpec((8, 128), lambda i: (0, 0))
    hbm = pl.BlockSpec(memory_space=pl.ANY)
    sall_spec = lambda d: pl.BlockSpec((cb, NH, HD, HD), lambda i: (pos[d](i), 0, 0, 0))
    do_spec = (lambda d: pl.BlockSpec((TS, w), lambda i: (pos[d](i), 0))) if has_do else (lambda d: small)
    acc_specs = [pl.BlockSpec((TS, 3 * w), lambda i: (row_blk0 + pos[0](i), 0)),
                 pl.BlockSpec((TS, 3 * w), lambda i: (row_blk0 + pos[1](i), 0)), abs_(0), abs_(1), small]
    return _pallas(
        body, (qkv, qkv, qkv, ab, qkv, qkv, qkv, ab, cst, sall_f, sall_b, do, do, dsfin, dqkv_f, dqkv_b, dab_f, dab_b, dcst),
        name=name, grid=(nb,), sem=("arbitrary",), exchange=exchange,
        in_specs=[col(0, 0), col(0, 1), col(0, 2), abs_(0), col(1, 0), col(1, 1), col(1, 2), abs_(1), small,
                  sall_spec(0), sall_spec(1), do_spec(0), do_spec(1), full4, hbm, hbm, hbm, hbm, small],
        out_specs=acc_specs + [full4],
        out_shape=[jax.ShapeDtypeStruct(dqkv_f.shape, F32), jax.ShapeDtypeStruct(dqkv_b.shape, F32),
                   jax.ShapeDtypeStruct(dab_f.shape, F32), jax.ShapeDtypeStruct(dab_b.shape, F32),
                   jax.ShapeDtypeStruct((8, 128), F32), jax.ShapeDtypeStruct((2, NH, HD, HD), F32)],
        aliases={14: 0, 15: 1, 16: 2, 17: 3, 18: 4},
        scratch_shapes=[pltpu.VMEM((2, NH, HD, HD), F32)])


def _pool(xin, *, row0, transpose, name):
    n_tok = xin.shape[0] - row0
    rows = n_tok // GW
    pad = 8 * GW
    tt = 512
    gsh = GW.bit_length() - 1

    def body(x_ref, o_ref, ybuf):
        ii = lax.broadcasted_iota(jnp.int32, (128, 128), 0)
        jj = lax.broadcasted_iota(jnp.int32, (128, 128), 1)
        same_row = (ii >> gsh) == (jj >> gsh)
        ci, cj = ii & (GW - 1), jj & (GW - 1)
        tok = lax.broadcasted_iota(jnp.int32, (tt, 1), 0)
        zpad = jnp.zeros((pad, 128), F32)
        for gi, wdw in enumerate(POOL_WINDOWS):
            lo, hi = wdw // 2, wdw - wdw // 2
            if transpose:
                band = same_row & (ci - cj >= -lo) & (ci - cj < hi)
                offs = range(-hi + 1, lo + 1)
            else:
                band = same_row & (cj - ci >= -lo) & (cj - ci < hi)
                offs = range(-lo, hi)
            bandm = band.astype(BF)
            cs = slice(gi * 128, (gi + 1) * 128)
            ybuf[0:pad, :] = zpad
            ybuf[pad + n_tok:, :] = zpad

            def inv_area(t0):
                t = t0 + tok
                r, c = t >> gsh, t & (GW - 1)
                nr = jnp.minimum(r + hi, rows) - jnp.maximum(r - lo, 0)
                nc = jnp.minimum(c + hi, GW) - jnp.maximum(c - lo, 0)
                return 1.0 / (nr * nc).astype(F32)

            def col_pass(b, carry):
                t0 = pl.multiple_of(b * tt, tt)
                xv = x_ref[pl.ds(row0 + t0, tt), cs]
                if transpose:
                    xv = xv * inv_area(t0)
                hi_part = xv.astype(BF)
                lo_part = (xv - hi_part.astype(F32)).astype(BF)
                for s in range(tt // 128):
                    sl = slice(s * 128, (s + 1) * 128)
                    y = (jnp.dot(bandm, hi_part[sl], preferred_element_type=F32)
                         + jnp.dot(bandm, lo_part[sl], preferred_element_type=F32))
                    ybuf[pl.ds(pad + t0 + s * 128, 128), :] = y
                return carry

            lax.fori_loop(0, n_tok // tt, col_pass, 0)

            def row_pass(b, carry):
                t0 = pl.multiple_of(b * tt, tt)
                acc = ybuf[pl.ds(pad + t0 + offs[0] * GW, tt), :]
                for dr in offs[1:]:
                    acc = acc + ybuf[pl.ds(pad + t0 + dr * GW, tt), :]
                xv = x_ref[pl.ds(row0 + t0, tt), cs]
                if not transpose:
                    acc = acc * inv_area(t0)
                o_ref[pl.ds(t0, tt), cs] = acc - xv
                return carry

            lax.fori_loop(0, n_tok // tt, row_pass, 0)

    return pl.pallas_call(
        body, name=name, out_shape=jax.ShapeDtypeStruct((n_tok, NPOOL), F32),
        in_specs=[pl.BlockSpec(memory_space=pltpu.VMEM)], out_specs=pl.BlockSpec(memory_space=pltpu.VMEM),
        scratch_shapes=[pltpu.VMEM((n_tok + 2 * pad, 128), F32)],
        compiler_params=pltpu.CompilerParams(vmem_limit_bytes=VMEM_LIMIT),
    )(xin)


def _merge_parts(of, ob, pgate, pd, br, gnw, pw_ref, pscale, wg_ref, wp_ref):
    o = of + ob
    ons, ohs, rs = [], [], []
    for h in range(NH):
        oh = o[:, h * HD:(h + 1) * HD]
        r = lax.rsqrt(jnp.mean(oh * oh, axis=-1, keepdims=True) + EPS)
        ohs.append(oh * r)
        rs.append(r)
        ons.append(oh * r * gnw)
    on = jnp.concatenate(ons, axis=1)
    sig_gate = jax.nn.sigmoid(pgate)
    silu_gate = pgate * sig_gate
    og = on * silu_gate
    y_gdn = _nn(og, wg_ref[...])
    ypre = jnp.concatenate([_nn(pd[:, g * 128:(g + 1) * 128], pw_ref[g]) for g in range(4)], axis=1)
    yp = ypre * pscale
    y_pool = _nn(yp, wp_ref[...])
    g_pool = jax.nn.sigmoid(br[:, :D])
    g_gdn = jax.nn.sigmoid(br[:, D:])
    return dict(on=on, ohs=ohs, rs=rs, og=og, y_gdn=y_gdn, ypre=ypre, yp=yp, y_pool=y_pool, g_pool=g_pool, g_gdn=g_gdn,
                sig_gate=sig_gate, silu_gate=silu_gate)


def _merge_fwd(x1, of, ob, pgate, pd, br, modv, gnw, pool_w, pscale, w_gdn, w_pool, w_mo, *, name):
    n_tok = of.shape[0]

    def body(x_ref, of_ref, ob_ref, pg_ref, pd_ref, br_ref, mod_ref, gnw_ref, pw_ref, ps_ref, wg_ref, wp_ref, wmo_ref,
             x2_ref, og_ref, yp_ref, m_ref, mix_ref):
        t = _merge_parts(of_ref[...], ob_ref[...], pg_ref[...], pd_ref[...], br_ref[...], gnw_ref[...], pw_ref, ps_ref[...],
                         wg_ref, wp_ref)
        m = t["g_pool"] * t["y_pool"] + t["g_gdn"] * t["y_gdn"]
        mix = _nn(m, wmo_ref[...])
        og_ref[...] = t["og"].astype(BF)
        yp_ref[...] = t["yp"].astype(BF)
        m_ref[...] = m.astype(BF)
        mix_ref[...] = mix.astype(BF)
        x2_ref[...] = x_ref[...] + mod_ref[0, 5:6, :] * mix

    tile = lambda w: pl.BlockSpec((TM, w), lambda i: (i, 0))
    ctile = lambda w: pl.BlockSpec((TM, w), lambda i: (i + 1, 0))
    return pl.pallas_call(
        body, name=name, grid=(n_tok // TM,),
        in_specs=[ctile(D), tile(D), tile(D), ctile(D), tile(NPOOL), ctile(2 * D),
                  pl.BlockSpec((1, 16, D), lambda i: (1, 0, 0)), _const_spec((1, HD)), _const_spec((4, 128, 128)),
                  _const_spec((1, NPOOL)), _const_spec((D, D)), _const_spec((NPOOL, D)), _const_spec((D, D))],
        out_specs=[tile(D), tile(D), tile(NPOOL), tile(D), tile(D)],
        out_shape=[jax.ShapeDtypeStruct((n_tok, D), F32), jax.ShapeDtypeStruct((n_tok, D), BF),
                   jax.ShapeDtypeStruct((n_tok, NPOOL), BF), jax.ShapeDtypeStruct((n_tok, D), BF),
                   jax.ShapeDtypeStruct((n_tok, D), BF)],
        compiler_params=_cparams("parallel"),
    )(x1, of, ob, pgate, pd, br, modv, gnw, pool_w, pscale, w_gdn, w_pool, w_mo)


def _merge_bwd(dx2, mix, of, ob, pgate, pd, br, modv, gnw, pool_w, pscale, w_gdn, w_pool, w_mo, *, name):
    n_tok = of.shape[0]

    def body(dx2_ref, mix_ref, of_ref, ob_ref, pg_ref, pd_ref, br_ref, mod_ref, gnw_ref, pw_ref, ps_ref, wg_ref, wp_ref, wmo_ref,
             do_ref, dgate_ref, dpd_ref, dbr_ref, dmix_ref, dyg_ref, dyp_ref, acc_ref, dpw_ref):
        i = pl.program_id(0)
        pgate, pdv, gnw = pg_ref[...], pd_ref[...], gnw_ref[...]
        t = _merge_parts(of_ref[...], ob_ref[...], pgate, pdv, br_ref[...], gnw, pw_ref, ps_ref[...], wg_ref, wp_ref)
        dx2v = dx2_ref[...]
        dmix = mod_ref[0, 5:6, :] * dx2v
        dmixb = dmix.astype(BF)
        dmix_ref[...] = dmixb
        dm = _nt(dmixb, wmo_ref[...])
        gp, gg = t["g_pool"], t["g_gdn"]
        dbr_ref[:, :D] = dm * t["y_pool"] * gp * (1.0 - gp)
        dbr_ref[:, D:] = dm * t["y_gdn"] * gg * (1.0 - gg)
        dyp = (dm * gp).astype(BF)
        dyg = (dm * gg).astype(BF)
        dyp_ref[...] = dyp
        dyg_ref[...] = dyg
        dyp_in = _nt(dyp, wp_ref[...])
        dypre = dyp_in * ps_ref[...]
        for g in range(4):
            gs = slice(g * 128, (g + 1) * 128)
            dpd_ref[:, gs] = _nt(dypre[:, gs], pw_ref[g])
        dog = _nt(dyg, wg_ref[...])
        dgate_ref[...] = dog * t["on"] * (t["sig_gate"] * (1.0 + pgate - t["silu_gate"]))
        don = dog * t["silu_gate"]
        dgnw = jnp.zeros((1, HD), F32)
        for h in range(NH):
            hs = slice(h * HD, (h + 1) * HD)
            donh, oh, r = don[:, hs], t["ohs"][h], t["rs"][h]
            dgnw = dgnw + jnp.sum(donh * oh, axis=0, keepdims=True)
            doh = donh * gnw
            do_ref[:, hs] = r * (doh - oh * jnp.mean(doh * oh, axis=-1, keepdims=True))

        @pl.when(i == 0)
        def _():
            acc_ref[...] = jnp.zeros_like(acc_ref)
            dpw_ref[...] = jnp.zeros_like(dpw_ref)

        acc_ref[0:1, :] += jnp.sum(dx2v * mix_ref[...].astype(F32), axis=0, keepdims=True)
        acc_ref[1:2, 0:HD] += dgnw
        acc_ref[2:3, 0:NPOOL] += jnp.sum(dyp_in * t["ypre"], axis=0, keepdims=True)
        for g in range(4):
            gs = slice(g * 128, (g + 1) * 128)
            dpw_ref[g] += _tn(pdv[:, gs], dypre[:, gs])

    tile = lambda w: pl.BlockSpec((TM, w), lambda i: (i, 0))
    ctile = lambda w: pl.BlockSpec((TM, w), lambda i: (i + 1, 0))
    return pl.pallas_call(
        body, name=name, grid=(n_tok // TM,),
        in_specs=[tile(D), tile(D), tile(D), tile(D), ctile(D), tile(NPOOL), ctile(2 * D),
                  pl.BlockSpec((1, 16, D), lambda i: (1, 0, 0)), _const_spec((1, HD)), _const_spec((4, 128, 128)),
                  _const_spec((1, NPOOL)), _const_spec((D, D)), _const_spec((NPOOL, D)), _const_spec((D, D))],
        out_specs=[tile(D), tile(D), tile(NPOOL), tile(2 * D), tile(D), tile(D), tile(D),
                   pl.BlockSpec((8, D), lambda i: (0, 0)), pl.BlockSpec((4, 128, 128), lambda i: (0, 0, 0))],
        out_shape=[jax.ShapeDtypeStruct((n_tok, D), F32), jax.ShapeDtypeStruct((n_tok, D), F32),
                   jax.ShapeDtypeStruct((n_tok, NPOOL), F32), jax.ShapeDtypeStruct((n_tok, 2 * D), F32),
                   jax.ShapeDtypeStruct((n_tok, D), BF), jax.ShapeDtypeStruct((n_tok, D), BF), jax.ShapeDtypeStruct((n_tok, D), BF),
                   jax.ShapeDtypeStruct((8, D), F32), jax.ShapeDtypeStruct((4, 128, 128), F32)],
        compiler_params=_cparams("arbitrary"),
    )(dx2, mix, of, ob, pgate, pd, br, modv, gnw, pool_w, pscale, w_gdn, w_pool, w_mo)


def _final(x3, target, fnw, *, name):
    n_tok = x3.shape[0]

    def body(x_ref, t_ref, w_ref, dx_ref, acc_ref):
        xv, w = x_ref[...], w_ref[...]
        r = lax.rsqrt(jnp.mean(xv * xv, axis=-1, keepdims=True) + EPS)
        xh = xv * r
        err = xh * w - t_ref[...]
        dy = err * (1.0 / D)
        dxh = dy * w
        dx_ref[...] = r * (dxh - xh * jnp.mean(dxh * xh, axis=-1, keepdims=True))

        @pl.when(pl.program_id(0) == 0)
        def _():
            acc_ref[...] = jnp.zeros_like(acc_ref)

        acc_ref[0:1, :] += jnp.sum(dy * xh, axis=0, keepdims=True)
        acc_ref[1:2, :] += jnp.sum(err * err, axis=0, keepdims=True) * (0.5 / D)

    tile = pl.BlockSpec((TM, D), lambda i: (i, 0))
    return pl.pallas_call(
        body, name=name, grid=(n_tok // TM,),
        in_specs=[tile, tile, _const_spec((1, D))],
        out_specs=[tile, pl.BlockSpec((8, D), lambda i: (0, 0))],
        out_shape=[jax.ShapeDtypeStruct((n_tok, D), F32), jax.ShapeDtypeStruct((8, D), F32)],
        compiler_params=_cparams("arbitrary"),
    )(x3, target, fnw)


def _split(results, n):
    return (*results[:n], list(results[n:]))


def _local_step(xc, target, modv, p, late=None):
    n_all = xc.shape[0]
    t_lat = n_all - TM
    nbc, nbx = TM // TS, t_lat // TS
    mod_lat = modv[1:2]
    gather = (lambda arrs: _ChipExchange(arrs, False)) if late else (lambda arrs: None)
    scatter = (lambda arrs: _ChipExchange(arrs, True)) if late else (lambda arrs: None)

    x1, h1, gu1, f1, *got = _ffn_fwd(xc, modv, p["norm1"], p["w1_in"], p["w1_out"], mrow=0, name="ffn1_fwd",
                                     exchange=gather(late and late[0]))
    if late:
        p = {**p, "w_mix": _regroup_mix(_from_chip_major_cols(got[0])), "conv": jnp.pad(_from_chip_major_cols(got[1]), ((0, 3), (0, 0)))}
    u, p_qkv, p_gate, p_pool, p_br, p_ab, *got = _mix_in_fwd(x1, modv, p["norm2"], p["w_mix"], name="mix_in_fwd",
                                                              exchange=gather(late and late[2]))
    if late:
        p = {**p, "w_gdn": got[0].reshape(D, D), "w_pool": _from_chip_major_cols(got[1]), "w_mo": got[2].reshape(D, D)}
    qkv, pre_qkv = _prep_fwd(p_qkv, p["conv"], name="prep_fwd")
    s_zero = jnp.zeros((2, NH, HD, HD), F32)
    _, _, sall_cf, sall_cb, s_ctx = _scan_fwd(qkv, p_ab, p["cst"], s_zero, row_blk0=0, nb=nbc, name="scan_ctx")
    o_f, o_b, sall_f, sall_b, _, *got = _scan_fwd(qkv, p_ab, p["cst"], s_ctx, row_blk0=nbc, nb=nbx, name="scan_lat",
                                                  exchange=gather(late and late[1]))
    if late:
        p = {**p, "w2_in": got[0], "w2_out": got[1].reshape(FF, D)}
    pd = _pool(p_pool, row0=TM, transpose=False, name="pool_fwd")
    merge_w = (modv, p["gnw"], p["pool_w"], p["pscale"], p["w_gdn"], p["w_pool"], p["w_mo"])
    x2, og, yp, m, mix = _merge_fwd(x1, o_f, o_b, p_gate, pd, p_br, *merge_w, name="merge_fwd")
    x3, h3, gu3, f3 = _ffn_fwd(x2, mod_lat, p["norm3"], p["w2_in"], p["w2_out"], mrow=6, name="ffn2_fwd")
    dx3, acc_fin = _final(x3, target, p["fnorm"], name="final")

    dx2, a3, df3, dgu3, acc3 = _ffn_bwd(dx3, x2, gu3, f3, mod_lat, p["norm3"], p["w2_in"], p["w2_out"], mrow=6, dx_skip=0,
                                        name="ffn2_bwd")
    g = {}
    tkl = _k_tile(t_lat)
    g["w2_out"] = _matmul_tn(a3, df3, tmm=FF // 2, tn=D, tk=tkl, name="ffn2_wout_grad").reshape(NCHIP, FF // NCHIP, D)
    g["w2_in"] = _matmul_tn(h3, dgu3, tmm=D, tn=2 * FF // NCHIP, tk=tkl, nsplit=NCHIP, name="ffn2_win_grad")
    do, dgate, dpd, dbr, dmix, dyg, dyp, acc_m, dpw = _merge_bwd(dx2, mix, o_f, o_b, p_gate, pd, p_br, *merge_w, name="merge_bwd")
    g["w_mo"] = _matmul_tn(m, dmix, tmm=D, tn=D, tk=tkl, name="wmo_grad").reshape(NCHIP, D // NCHIP, D)
    g["w_gdn"] = _matmul_tn(og, dyg, tmm=D, tn=D, tk=tkl, name="wgdn_grad").reshape(NCHIP, D // NCHIP, D)
    g["w_pool"] = _matmul_tn(yp, dyp, tmm=NPOOL, tn=D // NCHIP, tk=tkl, nsplit=NCHIP, name="wpool_grad")
    dpool_in = _pool(dpd, row0=0, transpose=True, name="pool_bwd")
    acc = (lax.empty((n_all, NQKV), F32), lax.empty((n_all, NQKV), F32), lax.empty((n_all, 128), F32),
           lax.empty((n_all, 128), F32), jnp.zeros((8, 128), F32))
    behind_scan = ("w2_in", "w2_out", "w_gdn", "w_pool", "w_mo")
    *acc, ds_ctx, landed = _split(_scan_bwd(qkv, p_ab, p["cst"], sall_f, sall_b, do, s_zero, *acc, row_blk0=nbc, nb=nbx, has_do=True,
                                            name="scan_lat_bwd", exchange=scatter([g[k] for k in behind_scan])), 6)
    landed = dict(zip(behind_scan, landed))
    dqkv_f, dqkv_b, dab_f, dab_b, dcst, _ = _scan_bwd(qkv, p_ab, p["cst"], sall_cf, sall_cb, jnp.zeros((8, 128), F32), ds_ctx, *acc,
                                                      row_blk0=0, nb=nbc, has_do=False, name="scan_ctx_bwd")
    dpqkv, dconv = _prep_bwd(p_qkv, pre_qkv, dqkv_f, dqkv_b, p["conv"], name="prep_bwd")
    dx1, dp, acc_mix = _mix_in_bwd(dx2, x1, dpqkv, dgate, dpool_in, dbr, dab_f, dab_b, modv, p["norm2"], p["w_mix"],
                                   name="mix_in_bwd")
    tka = _k_tile(n_all)
    g["w_mix"] = _chip_major_cols(_ungroup_mix(_matmul_tn(u, dp, tmm=256, tn=NMIXP, tk=_k_tile(n_all, 1024), name="wmix_grad")))
    dx_lat, a1, df1, dgu1, acc1, got = _split(_ffn_bwd(dx1, xc, gu1, f1, modv, p["norm1"], p["w1_in"], p["w1_out"], mrow=0, dx_skip=1,
                                                   name="ffn1_bwd", exchange=scatter([g["w_mix"]])), 5)
    landed.update(zip(("w_mix",), got))
    g["w1_out"] = _matmul_tn(a1, df1, tmm=FF // 2, tn=D, tk=tka, name="ffn1_wout_grad").reshape(NCHIP, FF // NCHIP, D)
    g["w1_in"] = _matmul_tn(h1, dgu1, tmm=D, tn=2 * FF // NCHIP, tk=tka, nsplit=NCHIP, name="ffn1_win_grad")

    small = dict(norm1=acc1[0, 3] + acc1[1, 3], norm2=acc_mix[0, 3] + acc_mix[1, 3], norm3=acc3[0, 3], fnorm=acc_fin[0],
                 gnw=acc_m[1, :HD], pscale=acc_m[2, :NPOOL], pool_w=dpw, conv=dconv[:5],
                 a_log=dcst[0, 2 * NH:4 * NH], dt_bias=dcst[1, 2 * NH:4 * NH])
    zero = jnp.zeros((D,), F32)
    dmod = jnp.stack([
        jnp.stack([acc1[0, 0], acc1[0, 1], acc1[0, 2], acc_mix[0, 0], acc_mix[0, 1], zero, zero, zero, zero]),
        jnp.stack([acc1[1, 0], acc1[1, 1], acc1[1, 2], acc_mix[1, 0], acc_mix[1, 1], acc_m[0], acc3[0, 0], acc3[0, 1], acc3[0, 2]]),
    ])
    return jnp.sum(acc_fin[1]), dx_lat, g, landed, small, dmod


_HI = lax.Precision.HIGHEST


def _ada_fwd(c_all, w_sh, b_sh, *, name):
    def body(c_ref, w_ref, b_ref, o_ref):
        o_ref[...] = jnp.dot(_silu(c_ref[...]), w_ref[...], precision=_HI, preferred_element_type=F32) + b_ref[...]

    return pl.pallas_call(body, name=name, out_shape=jax.ShapeDtypeStruct((16, w_sh.shape[1]), F32),
                          compiler_params=pltpu.CompilerParams(vmem_limit_bytes=VMEM_LIMIT))(c_all, w_sh, b_sh)


def _ada_bwd(c_all, dm, w_sh, *, name):
    def body(c_ref, dm_ref, w_ref, dw_ref, dc_ref):
        sc = _silu(c_ref[...])
        dw_ref[...] = lax.dot_general(sc, dm_ref[...], (((0,), (0,)), ((), ())), precision=_HI, preferred_element_type=F32)
        part = lax.dot_general(dm_ref[8:9, :], w_ref[...], (((1,), (1,)), ((), ())), precision=_HI, preferred_element_type=F32)
        dc_ref[...] = jnp.broadcast_to(part, dc_ref.shape)

    return pl.pallas_call(body, name=name,
                          out_shape=[jax.ShapeDtypeStruct(w_sh.shape, F32), jax.ShapeDtypeStruct((8, D), F32)],
                          compiler_params=pltpu.CompilerParams(vmem_limit_bytes=VMEM_LIMIT))(c_all, dm, w_sh)


def _cctx_grad(parts, c_ctx, *, name):
    def body(p_ref, c_ref, o_ref):
        tot = (p_ref[0, 0:1, :] + p_ref[2, 0:1, :]) + (p_ref[4, 0:1, :] + p_ref[6, 0:1, :])
        o_ref[...] = tot * _dsilu(c_ref[...])

    return pl.pallas_call(body, name=name, out_shape=jax.ShapeDtypeStruct((1, D), F32))(parts, c_ctx)


_MESH = pl.DeviceIdType.MESH
_ANY = pl.BlockSpec(memory_space=pl.ANY)


def _flip(v, bit):
    return (1 - v) if bit else v


def _all_gather8(x, *, name):
    def body(x_ref, out_ref, send_sems, recv_sems, local_sem):
        mx, my, mc = lax.axis_index("x"), lax.axis_index("y"), lax.axis_index("c")
        me = 4 * mx + 2 * my + mc
        mine = pltpu.make_async_copy(x_ref, out_ref.at[me], local_sem)
        mine.start()
        sends, recvs = [], []
        for k in range(1, 8):
            px, py, pc = _flip(mx, k & 4), _flip(my, k & 2), _flip(mc, k & 1)
            sends.append(pltpu.make_async_remote_copy(src_ref=x_ref, dst_ref=out_ref.at[me], send_sem=send_sems.at[k - 1],
                                                      recv_sem=recv_sems.at[k - 1], device_id=(px, py, pc), device_id_type=_MESH))
            recvs.append(pltpu.make_async_remote_copy(src_ref=x_ref, dst_ref=out_ref.at[4 * px + 2 * py + pc],
                                                      send_sem=send_sems.at[k - 1], recv_sem=recv_sems.at[k - 1],
                                                      device_id=(px, py, pc), device_id_type=_MESH))
        for cp in sends:
            cp.start()
        for cp in recvs:
            cp.wait_recv()
        for cp in sends:
            cp.wait_send()
        mine.wait()

    vm = pl.BlockSpec(memory_space=pltpu.VMEM)
    return pl.pallas_call(
        body, name=name, out_shape=jax.ShapeDtypeStruct((8,) + x.shape, x.dtype), in_specs=[vm], out_specs=vm,
        scratch_shapes=[pltpu.SemaphoreType.DMA((7,)), pltpu.SemaphoreType.DMA((7,)), pltpu.SemaphoreType.DMA],
        compiler_params=pltpu.CompilerParams(vmem_limit_bytes=VMEM_LIMIT),
    )(x)


class _ChipExchange:
    def __init__(self, arrs, scatter):
        self.arrs, self.scatter, self.n = list(arrs), scatter, len(arrs)
        self.out_shape = [jax.ShapeDtypeStruct(a.shape if scatter else (NCHIP,) + a.shape, a.dtype) for a in self.arrs]
        links = self.n * (NCHIP - 1)
        self.scratch = [pltpu.SemaphoreType.DMA((links,)), pltpu.SemaphoreType.DMA((links,)), pltpu.SemaphoreType.DMA((self.n,))]

    def copies(self, ins, outs, send_sems, recv_sems, local_sems):
        mx, my, mc = lax.axis_index("x"), lax.axis_index("y"), lax.axis_index("c")
        me = 2 * mx + my
        local, sends, recvs = [], [], []
        for j in range(self.n):
            src_own = ins[j].at[me] if self.scatter else ins[j]
            local.append(pltpu.make_async_copy(src_own, outs[j].at[me], local_sems.at[j]))
            for k in range(1, NCHIP):
                px, py = _flip(mx, k & 2), _flip(my, k & 1)
                peer = 2 * px + py
                sem = j * (NCHIP - 1) + k - 1
                src = ins[j].at[peer] if self.scatter else ins[j]
                sends.append(pltpu.make_async_remote_copy(src_ref=src, dst_ref=outs[j].at[me], send_sem=send_sems.at[sem],
                                                          recv_sem=recv_sems.at[sem], device_id=(px, py, mc), device_id_type=_MESH))
                recvs.append(pltpu.make_async_remote_copy(src_ref=src, dst_ref=outs[j].at[peer], send_sem=send_sems.at[sem],
                                                          recv_sem=recv_sems.at[sem], device_id=(px, py, mc), device_id_type=_MESH))
        return local, sends, recvs

    @staticmethod
    def start(local, sends, recvs):
        for cp in local + sends:
            cp.start()

    @staticmethod
    def finish(local, sends, recvs):
        for cp in recvs:
            cp.wait_recv()
        for cp in sends:
            cp.wait_send()
        for cp in local:
            cp.wait()


def _chip_exchange(arrs, *, scatter, name):
    ex = _ChipExchange(arrs, scatter)

    def body(*refs):
        cps = ex.copies(refs[:ex.n], refs[ex.n:2 * ex.n], *refs[2 * ex.n:])
        ex.start(*cps)
        ex.finish(*cps)

    return pl.pallas_call(body, name=name, out_shape=ex.out_shape, in_specs=[_ANY] * ex.n, out_specs=[_ANY] * ex.n,
                          scratch_shapes=ex.scratch)(*arrs)


def _gather_split(arrs, *, name):
    n = len(arrs)
    links = n * (NCHIP - 1)

    def body(*refs):
        ins, outs = refs[:n], refs[n:2 * n]
        ici_send, ici_recv, d2d_send, d2d_recv, local_sems = refs[2 * n:]
        mx, my, mc = lax.axis_index("x"), lax.axis_index("y"), lax.axis_index("c")
        me = 2 * mx + my
        local, first, arrive, onward, handed = [], [], [], [], []
        for j in range(n):
            hr = ins[j].shape[0] // 2
            mine, other = pl.ds(mc * hr, hr), pl.ds((1 - mc) * hr, hr)
            local.append(pltpu.make_async_copy(ins[j], outs[j].at[me], local_sems.at[j]))
            for k in range(1, NCHIP):
                px, py = _flip(mx, k & 2), _flip(my, k & 1)
                peer = 2 * px + py
                sem = j * (NCHIP - 1) + k - 1
                ici = lambda slot: pltpu.make_async_remote_copy(
                    src_ref=ins[j].at[mine], dst_ref=outs[j].at[slot, mine], send_sem=ici_send.at[sem], recv_sem=ici_recv.at[sem],
                    device_id=(px, py, mc), device_id_type=_MESH)
                d2d = lambda rows: pltpu.make_async_remote_copy(
                    src_ref=outs[j].at[peer, rows], dst_ref=outs[j].at[peer, rows], send_sem=d2d_send.at[sem], recv_sem=d2d_recv.at[sem],
                    device_id=(mx, my, 1 - mc), device_id_type=_MESH)
                first.append(ici(me))
                arrive.append(ici(peer))
                onward.append(d2d(mine))
                handed.append(d2d(other))
        for cp in local + first:
            cp.start()
        for got, fwd in zip(arrive, onward):
            got.wait_recv()
            fwd.start()
        for cp in handed:
            cp.wait_recv()
        for cp in first + onward:
            cp.wait_send()
        for cp in local:
            cp.wait()

    sems = pltpu.SemaphoreType.DMA((links,))
    return pl.pallas_call(
        body, name=name, out_shape=[jax.ShapeDtypeStruct((NCHIP,) + a.shape, a.dtype) for a in arrs],
        in_specs=[_ANY] * n, out_specs=[_ANY] * n, scratch_shapes=[sems, sems, sems, sems, pltpu.SemaphoreType.DMA((n,))],
    )(*arrs)


_HBM = pl.BlockSpec(memory_space=pltpu.HBM)
_SEM = pl.BlockSpec(memory_space=pltpu.SEMAPHORE)
_DATAFLOW = pltpu.SideEffectType.DATAFLOW_SIDE_EFFECTING


def _scatter_copies(ins, lands, send_sems, recv_sems):
    mx, my, mc = lax.axis_index("x"), lax.axis_index("y"), lax.axis_index("c")
    me = 2 * mx + my
    sends, recvs = [], []
    for j in range(len(ins)):
        for k in range(1, NCHIP):
            px, py = _flip(mx, k & 2), _flip(my, k & 1)
            peer = 2 * px + py
            sem = j * (NCHIP - 1) + k - 1
            mk = lambda slot: pltpu.make_async_remote_copy(src_ref=ins[j].at[peer], dst_ref=lands[j].at[slot], send_sem=send_sems.at[sem],
                                                           recv_sem=recv_sems.at[sem], device_id=(px, py, mc), device_id_type=_MESH)
            sends.append(mk(me))
            recvs.append(mk(peer))
    return sends, recvs


def _scatter_start(arrs, after, *, name):
    n = len(arrs)
    links = n * (NCHIP - 1)
    n_in = 2 * n + len(after)

    def body(*refs):
        ins, lands = refs[:n], refs[n:2 * n]
        send_sems, recv_sems = refs[n_in], refs[n_in + 1]
        token = refs[-1]
        for cp in _scatter_copies(ins, lands, send_sems, recv_sems)[0]:
            cp.start()
        token[...] = jnp.zeros_like(token)

    hbm = lambda a: pltpu.HBM(a.shape, a.dtype)
    res = pl.pallas_call(
        body, name=name,
        out_shape=(pltpu.SemaphoreType.DMA((links,)), pltpu.SemaphoreType.DMA((links,)), *[hbm(a) for a in arrs], *[hbm(a) for a in arrs],
                   jax.ShapeDtypeStruct((8, 128), F32)),
        in_specs=[_HBM] * (2 * n) + [_ANY] * len(after), out_specs=(_SEM, _SEM, *[_HBM] * (2 * n), pl.BlockSpec(memory_space=pltpu.VMEM)),
        input_output_aliases={j: 2 + j for j in range(2 * n)},
        compiler_params=pltpu.CompilerParams(has_side_effects=_DATAFLOW),
    )(*[pltpu.with_memory_space_constraint(a, pltpu.HBM) for a in arrs],
      *[pltpu.with_memory_space_constraint(lax.empty(a.shape, a.dtype), pltpu.HBM) for a in arrs], *after)
    return res[0], res[1], list(res[2:2 + n]), list(res[2 + n:2 + 2 * n]), res[-1]


def _scatter_wait(send_sems, recv_sems, arrs, lands, after, *, name):
    n = len(arrs)

    def body(*refs):
        ins, lands_in = refs[:n], refs[n:2 * n]
        sends, recvs = _scatter_copies(ins, lands_in, refs[2 * n], refs[2 * n + 1])
        for cp in sends:
            cp.wait_send()
        for cp in recvs:
            cp.wait_recv()

    hbm = lambda a: pltpu.HBM(a.shape, a.dtype)
    res = pl.pallas_call(
        body, name=name, out_shape=(*[hbm(a) for a in arrs], *[hbm(a) for a in lands]),
        in_specs=[_HBM] * (2 * n) + [_SEM, _SEM] + [_ANY] * len(after), out_specs=[_HBM] * (2 * n),
        input_output_aliases={j: j for j in range(2 * n)},
        compiler_params=pltpu.CompilerParams(has_side_effects=_DATAFLOW),
    )(*arrs, *lands, send_sems, recv_sems, *after)
    return list(res[:n]), list(res[n:])


def _pallas(body, operands, *, name, grid, in_specs, out_specs, out_shape, sem, scratch_shapes=(), aliases=None, exchange=None):
    if exchange is None:
        return pl.pallas_call(body, name=name, grid=grid, in_specs=in_specs, out_specs=out_specs, out_shape=out_shape,
                              scratch_shapes=list(scratch_shapes), input_output_aliases=aliases or {},
                              compiler_params=_cparams(*sem))(*operands)
    ex = exchange
    (steps,) = grid
    n_in, n_out, n_scr, k = len(in_specs), len(out_specs), len(scratch_shapes), ex.n

    def hosted(*refs):
        ins, refs = refs[:n_in], refs[n_in:]
        ex_in, refs = refs[:k], refs[k:]
        outs, refs = refs[:n_out], refs[n_out:]
        ex_out, refs = refs[:k], refs[k:]
        scr, ex_sems = refs[:n_scr], refs[n_scr:]
        cps = ex.copies(ex_in, ex_out, *ex_sems)
        pl.when(pl.program_id(0) == 0)(lambda: ex.start(*cps))
        body(*ins, *outs, *scr)
        pl.when(pl.program_id(0) == steps - 1)(lambda: ex.finish(*cps))

    return pl.pallas_call(
        hosted, name=name, grid=grid, in_specs=list(in_specs) + [_ANY] * k, out_specs=list(out_specs) + [_ANY] * k,
        out_shape=list(out_shape) + ex.out_shape, scratch_shapes=list(scratch_shapes) + ex.scratch,
        input_output_aliases=aliases or {}, compiler_params=_cparams("arbitrary"),
    )(*operands, *ex.arrs)


def _core_swap(arrs, *, name):
    n = len(arrs)

    def body(*refs):
        ins, outs = refs[:n], refs[n:2 * n]
        send_sems, recv_sems = refs[2 * n:]
        sib = (lax.axis_index("x"), lax.axis_index("y"), 1 - lax.axis_index("c"))
        cps = [pltpu.make_async_remote_copy(src_ref=ins[j], dst_ref=outs[j], send_sem=send_sems.at[j], recv_sem=recv_sems.at[j],
                                            device_id=sib, device_id_type=_MESH) for j in range(n)]
        for cp in cps:
            cp.start()
        for cp in cps:
            cp.wait_recv()
        for cp in cps:
            cp.wait_send()

    return pl.pallas_call(
        body, name=name, out_shape=[jax.ShapeDtypeStruct(a.shape, a.dtype) for a in arrs],
        in_specs=[_ANY] * n, out_specs=[_ANY] * n,
        scratch_shapes=[pltpu.SemaphoreType.DMA((n,)), pltpu.SemaphoreType.DMA((n,))],
    )(*arrs)


def _row_tile(rows, cols, budget=1 << 18):
    best = None
    for t in range(8, rows + 1, 8):
        if rows % t == 0 and t * cols <= budget:
            best = t
    return best or rows


def _sum_slots(x, *, name, after=()):
    ns, r, c = x.shape
    tr = _row_tile(r, c * ns)

    def body(x_ref, *refs):
        acc = x_ref[0].astype(F32)
        for s in range(1, ns):
            acc = acc + x_ref[s].astype(F32)
        refs[-1][...] = acc

    return pl.pallas_call(
        body, name=name, grid=(r // tr,), out_shape=jax.ShapeDtypeStruct((r, c), F32),
        in_specs=[pl.BlockSpec((ns, tr, c), lambda i: (0, i, 0))] + [_ANY] * len(after), out_specs=pl.BlockSpec((tr, c), lambda i: (i, 0)),
        compiler_params=_cparams("parallel"),
    )(x, *after)


def _adamw(w, ga, gb, m, v, *, name):
    r, c = w.shape
    tr = _row_tile(r, c, budget=1 << 17)
    two = gb is not None

    def body(*refs):
        w_ref, ga_ref = refs[0], refs[1]
        m_ref, v_ref = refs[2 + two], refs[3 + two]
        g_ref, d_ref, mo_ref, vo_ref = refs[4 + two:]
        g = ga_ref[...] + refs[2][...] if two else ga_ref[...]
        mn = ADAM_B1 * m_ref[...] + (1.0 - ADAM_B1) * g
        vn = ADAM_B2 * v_ref[...] + (1.0 - ADAM_B2) * (g * g)
        m_hat = mn / (1.0 - ADAM_B1 ** ADAM_STEP)
        v_hat = vn / (1.0 - ADAM_B2 ** ADAM_STEP)
        g_ref[...] = g
        d_ref[...] = -ADAM_LR * (m_hat / (jnp.sqrt(v_hat) + ADAM_EPS) + ADAM_WD * w_ref[...])
        mo_ref[...] = mn
        vo_ref[...] = vn

    spec = pl.BlockSpec((tr, c), lambda i: (i, 0))
    ins = [w, ga] + ([gb] if two else []) + [m, v]
    return pl.pallas_call(
        body, name=name, grid=(r // tr,), out_shape=[jax.ShapeDtypeStruct((r, c), F32)] * 4,
        in_specs=[spec] * len(ins), out_specs=[spec] * 4, compiler_params=_cparams("parallel"),
    )(*ins)


_MIX_AB0, _MIX_AB1 = NQKV, NQKV + 4 * NH


def _regroup_mix(w):
    pad = jnp.zeros((w.shape[0], NMIXP - NMIX), w.dtype)
    return jnp.concatenate([w[:, :_MIX_AB0], w[:, _MIX_AB1:], w[:, _MIX_AB0:_MIX_AB1], pad], axis=1)


def _ungroup_mix(w):
    n_ab = _MIX_AB1 - _MIX_AB0
    return jnp.concatenate([w[:, :_MIX_AB0], w[:, NMIX - n_ab:NMIX], w[:, _MIX_AB0:NMIX - n_ab]], axis=1)


def _chip_major_cols(w):
    r, c = w.shape
    return w.reshape(r, NCHIP, c // NCHIP).transpose(1, 0, 2)


def _from_chip_major_cols(w):
    return w.transpose(1, 0, 2).reshape(w.shape[1], -1)


_SMALL = (("c_ctx", D), ("b_ada", 9 * D), ("norm1_w", D), ("norm2_w", D), ("norm3_w", D), ("final_norm_w", D),
          ("a_log", 2 * NH), ("dt_bias", 2 * NH), ("gdn_norm_w", HD), ("pool_w", 4 * 128 * 128), ("pool_scale", NPOOL),
          ("conv_w", 5 * NQKV // NCHIP))


def _pack(vals, lanes=128, row_mult=8):
    flat = jnp.concatenate([jnp.ravel(v) for v in vals])
    n = flat.shape[0]
    rows = -(-n // (lanes * row_mult)) * row_mult
    return jnp.pad(flat, (0, rows * lanes - n)).reshape(rows, lanes)


def _unpack(packed, sizes):
    flat = packed.reshape(-1)
    out, o = [], 0
    for n in sizes:
        out.append(flat[o:o + n])
        o += n
    return out


def kernel(x, c, ctx, c_ctx, w_ada, b_ada, norm1_w, ffn1_w_in, ffn1_w_out, norm2_w, w_mix_in, conv_w, a_log, dt_bias, gdn_norm_w, w_gdn_proj, pool_w, pool_scale, w_pool_proj, w_mix_out, norm3_w, ffn2_w_in, ffn2_w_out, final_norm_w, loss_target, m_c_ctx, m_w_ada, m_b_ada, m_norm1_w, m_ffn1_w_in, m_ffn1_w_out, m_norm2_w, m_w_mix_in, m_conv_w, m_a_log, m_dt_bias, m_gdn_norm_w, m_w_gdn_proj, m_pool_w, m_pool_scale, m_w_pool_proj, m_w_mix_out, m_norm3_w, m_ffn2_w_in, m_ffn2_w_out, m_final_norm_w, v_c_ctx, v_w_ada, v_b_ada, v_norm1_w, v_ffn1_w_in, v_ffn1_w_out, v_norm2_w, v_w_mix_in, v_conv_w, v_a_log, v_dt_bias, v_gdn_norm_w, v_w_gdn_proj, v_pool_w, v_pool_scale, v_w_pool_proj, v_w_mix_out, v_norm3_w, v_ffn2_w_in, v_ffn2_w_out, v_final_norm_w):
    names = ("c_ctx", "w_ada", "b_ada", "norm1_w", "ffn1_w_in", "ffn1_w_out", "norm2_w", "w_mix_in", "conv_w", "a_log", "dt_bias",
             "gdn_norm_w", "w_gdn_proj", "pool_w", "pool_scale", "w_pool_proj", "w_mix_out", "norm3_w", "ffn2_w_in", "ffn2_w_out",
             "final_norm_w")
    w = dict(zip(names, (c_ctx, w_ada, b_ada, norm1_w, ffn1_w_in, ffn1_w_out, norm2_w, w_mix_in, conv_w, a_log, dt_bias, gdn_norm_w,
                         w_gdn_proj, pool_w, pool_scale, w_pool_proj, w_mix_out, norm3_w, ffn2_w_in, ffn2_w_out, final_norm_w)))
    mom = dict(zip(names, (m_c_ctx, m_w_ada, m_b_ada, m_norm1_w, m_ffn1_w_in, m_ffn1_w_out, m_norm2_w, m_w_mix_in, m_conv_w, m_a_log,
                           m_dt_bias, m_gdn_norm_w, m_w_gdn_proj, m_pool_w, m_pool_scale, m_w_pool_proj, m_w_mix_out, m_norm3_w,
                           m_ffn2_w_in, m_ffn2_w_out, m_final_norm_w)))
    var = dict(zip(names, (v_c_ctx, v_w_ada, v_b_ada, v_norm1_w, v_ffn1_w_in, v_ffn1_w_out, v_norm2_w, v_w_mix_in, v_conv_w, v_a_log,
                           v_dt_bias, v_gdn_norm_w, v_w_gdn_proj, v_pool_w, v_pool_scale, v_w_pool_proj, v_w_mix_out, v_norm3_w,
                           v_ffn2_w_in, v_ffn2_w_out, v_final_norm_w)))
    mx, my, mc = lax.axis_index("x"), lax.axis_index("y"), lax.axis_index("c")
    chip = 2 * mx + my
    dev = 2 * chip + mc
    ada_cols = w_ada.shape[2]

    c_rows = _all_gather8(jnp.pad(c, ((0, 7), (0, 0))), name="gather_c")[:, 0, :]
    c_all = jnp.concatenate([c_rows, c_ctx[None], jnp.zeros((7, D), F32)], axis=0)
    b_sh = lax.dynamic_slice(b_ada, (0, chip * ada_cols), (1, ada_cols))
    mod_sh = _ada_fwd(c_all, w_ada[0], b_sh, name="ada_fwd")
    mod_parts = _all_gather8(mod_sh, name="gather_mod")
    mod_all = jnp.concatenate([mod_parts[2 * s] for s in range(NCHIP)], axis=1)
    mod_lat = lax.dynamic_index_in_dim(mod_all, dev, axis=0, keepdims=False).reshape(9, D)
    modv = jnp.zeros((2, 16, D), F32).at[0, :9].set(mod_all[8].reshape(9, D)).at[1, :9].set(mod_lat)

    big = ("ffn1_w_in", "ffn1_w_out", "w_mix_in", "w_gdn_proj", "w_pool_proj", "w_mix_out", "ffn2_w_in", "ffn2_w_out")
    shard = {k: w[k][0].astype(BF) for k in big}
    w1_in, w1_out = _gather_split([shard["ffn1_w_in"], shard["ffn1_w_out"]], name="gather_ffn1")
    p = dict(
        norm1=norm1_w, norm2=norm2_w, norm3=norm3_w, fnorm=final_norm_w[None], w1_in=w1_in, w1_out=w1_out.reshape(FF, D),
        cst=jnp.zeros((8, 128), F32).at[0, 2 * NH:4 * NH].set(jnp.exp(a_log).reshape(-1)).at[1, 2 * NH:4 * NH].set(dt_bias.reshape(-1)),
        gnw=gdn_norm_w, pool_w=pool_w[0], pscale=pool_scale)
    late = ([shard["w_mix_in"], conv_w[0]], [shard["ffn2_w_in"], shard["ffn2_w_out"]],
            [shard["w_gdn_proj"], shard["w_pool_proj"], shard["w_mix_out"]])

    xc = jnp.concatenate([ctx[0], x[0]], axis=0)
    loss_dev, dx_lat, g, landed, small, dmod = _local_step(xc, loss_target[0], modv, p, late)
    loss = lax.psum(loss_dev, ("x", "y", "c"))
    grad_x = dx_lat[None]

    small_vals = [dmod[1], dmod[0], small["norm1"], small["norm2"], small["norm3"], small["fnorm"], small["a_log"], small["dt_bias"],
                  small["gnw"], small["pool_w"], small["pscale"], small["conv"]]
    small_sizes = [v.size for v in small_vals]
    packed = _all_gather8(_pack(small_vals), name="gather_small")
    tot = _unpack(_sum_slots(packed, name="sum_small"), small_sizes)
    dmod_lat_all = packed[:, :9 * D // 128, :].reshape(8, 9 * D)
    dm = jnp.concatenate([dmod_lat_all, tot[1][None], jnp.zeros((7, 9 * D), F32)], axis=0)
    dm_sh = lax.dynamic_slice(dm, (0, chip * ada_cols), (16, ada_cols))
    g_w_ada, cctx_part = _ada_bwd(c_all, dm_sh, w_ada[0], name="ada_bwd")
    g_c_ctx = _cctx_grad(_all_gather8(cctx_part, name="gather_cctx"), c_ctx[None], name="cctx_grad")[0]
    conv_tot = tot[11].reshape(5, NQKV)
    g_small = dict(c_ctx=g_c_ctx, b_ada=tot[0] + tot[1], norm1_w=tot[2], norm2_w=tot[3], norm3_w=tot[4], final_norm_w=tot[5],
                   a_log=tot[6], dt_bias=tot[7], gdn_norm_w=tot[8], pool_w=tot[9], pool_scale=tot[10],
                   conv_w=lax.dynamic_slice(conv_tot, (0, chip * (NQKV // NCHIP)), (5, NQKV // NCHIP)))

    first = ("ffn1_w_in", "ffn1_w_out")
    order = dict(zip(big, ("w1_in", "w1_out", "w_mix", "w_gdn", "w_pool", "w_mo", "w2_in", "w2_out")))
    rest = [k for k in big if k not in first]
    send_sems, recv_sems, sent, lands, token = _scatter_start([g["w1_in"], g["w1_out"]], [g_c_ctx, g_w_ada], name="scatter_ffn1_start")
    mine = {k: _sum_slots(landed[order[k]], name=f"sum_{k}", after=[token]) for k in rest}
    theirs = dict(zip(rest, _core_swap([mine[k] for k in rest], name="swap_grad_sums")))

    out = {}
    as2d = lambda a: a.reshape(-1, a.shape[-1])

    def update(k):
        res = _adamw(as2d(w[k]), as2d(mine[k]), as2d(theirs[k]), as2d(mom[k]), as2d(var[k]), name=f"adamw_{k}")
        out[k] = [r.reshape(w[k].shape) for r in res]

    for k in rest:
        update(k)
    out["w_ada"] = [r.reshape(w_ada.shape) for r in _adamw(w_ada[0], g_w_ada, None, m_w_ada[0], v_w_ada[0], name="adamw_w_ada")]
    sm_names = [n for n, _ in _SMALL]
    sm_sizes = [n for _, n in _SMALL]
    res = _adamw(_pack([w[k] for k in sm_names]), _pack([g_small[k] for k in sm_names]), None,
                 _pack([mom[k] for k in sm_names]), _pack([var[k] for k in sm_names]), name="adamw_small")
    done = [out[k][1] for k in rest] + [out["w_ada"][1], res[1]]
    res = [_unpack(r, sm_sizes) for r in res]
    for i, k in enumerate(sm_names):
        out[k] = [r[i].reshape(w[k].shape) for r in res]
    sent, lands = _scatter_wait(send_sems, recv_sems, sent, lands, done, name="scatter_ffn1_wait")
    for k, part, land in zip(first, sent, lands):
        own = lax.dynamic_slice_in_dim(part, chip, 1, axis=0)
        mine[k] = _sum_slots(lax.dynamic_update_slice_in_dim(land, own, chip, axis=0), name=f"sum_{k}")
    theirs.update(zip(first, _core_swap([mine[k] for k in first], name="swap_ffn1_sums")))
    for k in first:
        update(k)
    return (loss, grad_x, *[out[k][0] for k in names], *[out[k][1] for k in names], *[out[k][2] for k in names],
            *[out[k][3] for k in names])
```

```python
import functools

import jax
import jax.numpy as jnp
from jax import lax
from jax.experimental import pallas as pl
from jax.experimental.pallas import tpu as pltpu

F32 = jnp.float32
BF = jnp.bfloat16

D = 1024
FF = 2816
NH = 8
HD = 128
CH = 64
GW = 64
TM = 256
NQKV = 3 * NH * HD
NPOOL = 512
POOL_WINDOWS = (2, 4, 8, 16)
NMIX = 6688
NMIXP = 6784
EPS = 1e-6
NCHIP = 4
VMEM_LIMIT = 56 * 1024 * 1024

ADAM_LR, ADAM_B1, ADAM_B2, ADAM_EPS, ADAM_WD, ADAM_STEP = 0.001, 0.9, 0.999, 1e-08, 0.01, 10


def _cparams(*sem):
    return pltpu.CompilerParams(dimension_semantics=sem, vmem_limit_bytes=VMEM_LIMIT)


def _const_spec(shape):
    nd = len(shape)
    return pl.BlockSpec(shape, lambda *_: (0,) * nd, pipeline_mode=pl.Buffered(1))


def _dot(a, b, dims):
    return lax.dot_general(a.astype(BF), b.astype(BF), (dims, ((), ())), preferred_element_type=F32)


def _nn(a, b):
    return _dot(a, b, ((1,), (0,)))


def _nt(a, b):
    return _dot(a, b, ((1,), (1,)))


def _tn(a, b):
    return _dot(a, b, ((0,), (0,)))


def _silu(x):
    return x * jax.nn.sigmoid(x)


def _dsilu(x):
    s = jax.nn.sigmoid(x)
    return s * (1.0 + x * (1.0 - s))


def _norm_mod(x, nw, shift, scale):
    r = lax.rsqrt(jnp.mean(x * x, axis=-1, keepdims=True) + EPS)
    xh = x * r
    n = xh * nw
    return n * (1.0 + scale) + shift, n, xh, r


def _norm_mod_bwd(dh, n, xh, r, nw, scale):
    dn = dh * (1.0 + scale)
    dxh = dn * nw
    dx = r * (dxh - xh * jnp.mean(dxh * xh, axis=-1, keepdims=True))
    rs = lambda t: jnp.sum(t, axis=0, keepdims=True)
    return dx, rs(dh), rs(dh * n), rs(dn * xh)


def _stream_specs(ctx):
    if ctx is None:
        return [pl.BlockSpec((TM, D), lambda i: (i, 0))]
    return [pl.BlockSpec((TM, D), lambda i: (0, 0)), pl.BlockSpec((TM, D), lambda i: (jnp.maximum(i - 1, 0), 0))]


def _stream_tile(refs, lead):
    if not lead:
        return refs[0][...], refs[1:]
    return jnp.where(pl.program_id(0) == 0, refs[0][...], refs[1][...]), refs[2:]


def _ffn_fwd(x, modv, nw, w_in4, w_out, *, mrow, name, ctx=None, exchange=None):
    lead = ctx is not None
    n_tok = x.shape[0] + (TM if lead else 0)
    nt = n_tok // TM
    nset = modv.shape[0]
    ws = w_in4.shape[2]

    def body(*refs):
        xv, (mod_ref, nw_ref, win_ref, wout_ref, x1_ref, h_ref, gu_ref, f_ref) = _stream_tile(refs, lead)
        shift, scale, gate = mod_ref[0, mrow:mrow + 1, :], mod_ref[0, mrow + 1:mrow + 2, :], mod_ref[0, mrow + 2:mrow + 3, :]
        h, _, _, _ = _norm_mod(xv, nw_ref[...], shift, scale)
        hb = h.astype(BF)
        h_ref[...] = hb
        gus = [_nn(hb, win_ref[s]) for s in range(NCHIP)]
        for s in range(NCHIP):
            gu_ref[:, s * ws:(s + 1) * ws] = gus[s].astype(BF)
        g = jnp.concatenate(gus[:2], axis=1)
        u = jnp.concatenate(gus[2:], axis=1)
        f = _nn(_silu(g) * u, wout_ref[...])
        f_ref[...] = f.astype(BF)
        x1_ref[...] = xv + 0.5 * gate * f

    tile = lambda w: pl.BlockSpec((TM, w), lambda i: (i, 0))
    return _pallas(
        body, (*([ctx] if lead else []), x, modv, nw, w_in4, w_out), name=name, grid=(nt,), sem=("parallel",), exchange=exchange,
        in_specs=_stream_specs(ctx) + [pl.BlockSpec((1, 16, D), lambda i: (jnp.minimum(i, nset - 1), 0, 0)), _const_spec((1, D)),
                                       _const_spec(w_in4.shape), _const_spec(w_out.shape)],
        out_specs=[tile(D), tile(D), tile(2 * FF), tile(D)],
        out_shape=[jax.ShapeDtypeStruct((n_tok, D), F32), jax.ShapeDtypeStruct((n_tok, D), BF),
                   jax.ShapeDtypeStruct((n_tok, 2 * FF), BF), jax.ShapeDtypeStruct((n_tok, D), BF)])


def _ffn_bwd(dxo, x, gu, fo, modv, nw, w_in4, w_out, *, mrow, name, ctx=None, exchange=None):
    lead = ctx is not None
    dx_skip = int(lead)
    n_tok = x.shape[0] + dx_skip * TM
    nt = n_tok // TM
    nset = modv.shape[0]
    ws = w_in4.shape[2]

    def body(*refs):
        xv, (dxo_ref, gu_ref, f_ref, mod_ref, nw_ref, win_ref, wout_ref, dx_ref, a_ref, df_ref, dgu_ref, acc_ref) = _stream_tile(refs, lead)
        i = pl.program_id(0)
        dxo_v = dxo_ref[...]
        shift, scale, gate = mod_ref[0, mrow:mrow + 1, :], mod_ref[0, mrow + 1:mrow + 2, :], mod_ref[0, mrow + 2:mrow + 3, :]
        _, n, xh, r = _norm_mod(xv, nw_ref[...], shift, scale)
        df = 0.5 * gate * dxo_v
        dfb = df.astype(BF)
        df_ref[...] = dfb
        dgate = jnp.sum(0.5 * dxo_v * f_ref[...].astype(F32), axis=0, keepdims=True)
        da = _nt(dfb, wout_ref[...])
        g = gu_ref[:, :FF].astype(F32)
        u = gu_ref[:, FF:].astype(F32)
        sig = jax.nn.sigmoid(g)
        sg = g * sig
        a_ref[...] = (sg * u).astype(BF)
        dgu_ref[:, :FF] = (da * u * (sig * (1.0 + g - sg))).astype(BF)
        dgu_ref[:, FF:] = (da * sg).astype(BF)
        dh = _nt(dgu_ref[:, 0:ws], win_ref[0])
        for s in range(1, NCHIP):
            dh = dh + _nt(dgu_ref[:, s * ws:(s + 1) * ws], win_ref[s])
        dx, dshift, dscale, dnw = _norm_mod_bwd(dh, n, xh, r, nw_ref[...], scale)
        dx_ref[...] = dxo_v + dx

        @pl.when((i == 0) | (i == nset - 1))
        def _():
            acc_ref[...] = jnp.zeros_like(acc_ref)

        acc_ref[0, 0:1, :] += dshift
        acc_ref[0, 1:2, :] += dscale
        acc_ref[0, 2:3, :] += dgate
        acc_ref[0, 3:4, :] += dnw

    tile = lambda w: pl.BlockSpec((TM, w), lambda i: (i, 0))
    return _pallas(
        body, (*([ctx] if lead else []), x, dxo, gu, fo, modv, nw, w_in4, w_out), name=name, grid=(nt,), sem=("arbitrary",),
        exchange=exchange,
        in_specs=_stream_specs(ctx) + [tile(D), tile(2 * FF), tile(D),
                                       pl.BlockSpec((1, 16, D), lambda i: (jnp.minimum(i, nset - 1), 0, 0)), _const_spec((1, D)),
                                       _const_spec(w_in4.shape), _const_spec(w_out.shape)],
        out_specs=[pl.BlockSpec((TM, D), lambda i: (jnp.maximum(i - dx_skip, 0), 0)), tile(FF), tile(D), tile(2 * FF),
                   pl.BlockSpec((1, 8, D), lambda i: (jnp.minimum(i, nset - 1), 0, 0))],
        out_shape=[jax.ShapeDtypeStruct((n_tok - dx_skip * TM, D), F32), jax.ShapeDtypeStruct((n_tok, FF), BF),
                   jax.ShapeDtypeStruct((n_tok, D), BF), jax.ShapeDtypeStruct((n_tok, 2 * FF), BF),
                   jax.ShapeDtypeStruct((nset, 8, D), F32)])


def _k_tile(n, target=3072):
    return max(t for t in range(TM, min(n, target) + 1, TM) if n % t == 0)


def _matmul_tn(a, b, *, tmm, tn, tk, nsplit=1, name):
    n_tok, m = a.shape
    kk = b.shape[1]
    nk = n_tok // tk

    def body(a_ref, b_ref, o_ref, acc):
        k = pl.program_id(2)

        @pl.when(k == 0)
        def _():
            acc[...] = jnp.zeros_like(acc)

        acc[...] += _tn(a_ref[...], b_ref[...])

        @pl.when(k == nk - 1)
        def _():
            o_ref[...] = acc[...].astype(BF).reshape(o_ref.shape)

    if nsplit == 1:
        out_shape = jax.ShapeDtypeStruct((m, kk), BF)
        out_spec = pl.BlockSpec((tmm, tn), lambda i, j, k: (i, j))
    else:
        assert tn == kk // nsplit
        out_shape = jax.ShapeDtypeStruct((nsplit, m, tn), BF)
        out_spec = pl.BlockSpec((1, tmm, tn), lambda i, j, k: (j, i, 0))
    return pl.pallas_call(
        body, name=name, grid=(m // tmm, kk // tn, nk),
        in_specs=[pl.BlockSpec((tk, tmm), lambda i, j, k: (k, i)), pl.BlockSpec((tk, tn), lambda i, j, k: (k, j))],
        out_specs=out_spec, out_shape=out_shape,
        scratch_shapes=[pltpu.VMEM((tmm, tn), F32)],
        compiler_params=_cparams("parallel", "parallel", "arbitrary"),
    )(a, b)


_MIX_PARTS = (("qkv", 0, NQKV), ("gate", NQKV, 1024), ("pool", NQKV + 1024, NPOOL), ("br", NQKV + 1024 + NPOOL, 2048),
              ("ab", NMIXP - 128, 128))


def _mix_in_fwd(x1, modv, nw, w_mix, *, name, exchange=None):
    n_tok = x1.shape[0]

    def body(x_ref, mod_ref, nw_ref, w_ref, u_ref, *p_refs):
        u, _, _, _ = _norm_mod(x_ref[...], nw_ref[...], mod_ref[0, 3:4, :], mod_ref[0, 4:5, :])
        ub = u.astype(BF)
        u_ref[...] = ub
        for (_, c0, w), p_ref in zip(_MIX_PARTS, p_refs):
            p_ref[...] = _nn(ub, w_ref[:, c0:c0 + w])

    tile = lambda w: pl.BlockSpec((TM, w), lambda i: (i, 0))
    return _pallas(
        body, (x1, modv, nw, w_mix), name=name, grid=(n_tok // TM,), sem=("parallel",), exchange=exchange,
        in_specs=[tile(D), pl.BlockSpec((1, 16, D), lambda i: (jnp.minimum(i, 1), 0, 0)), _const_spec((1, D)),
                  _const_spec(w_mix.shape)],
        out_specs=[tile(D)] + [tile(w) for _, _, w in _MIX_PARTS],
        out_shape=[jax.ShapeDtypeStruct((n_tok, D), BF)] + [jax.ShapeDtypeStruct((n_tok, w), F32) for _, _, w in _MIX_PARTS])


def _mix_in_bwd(dxo, x1, dqkv, dgate, dpool, dbr, dab_f, dab_b, modv, nw, w_mix, *, name):
    n_tok = x1.shape[0]

    def body(dxo_ref, x_ref, dqkv_ref, dgate_ref, dpool_ref, dbr_ref, dabf_ref, dabb_ref, mod_ref, nw_ref, w_ref,
             dx_ref, dp_ref, acc_ref):
        i = pl.program_id(0)
        lat = i >= 1
        scale = mod_ref[0, 4:5, :]
        _, n, xh, r = _norm_mod(x_ref[...], nw_ref[...], mod_ref[0, 3:4, :], scale)
        dp_ref[:, 0:NQKV] = dqkv_ref[...].astype(BF)
        dp_ref[:, NQKV:NQKV + 1024] = jnp.where(lat, dgate_ref[...], 0.0).astype(BF)
        dp_ref[:, NQKV + 1024:NQKV + 1536] = jnp.where(lat, dpool_ref[...], 0.0).astype(BF)
        dp_ref[:, NQKV + 1536:NMIXP - 128] = jnp.where(lat, dbr_ref[...], 0.0).astype(BF)
        dp_ref[:, NMIXP - 128:] = (dabf_ref[...] + dabb_ref[...]).astype(BF)
        du = _nt(dp_ref[...], w_ref[...])
        dx, dshift, dscale, dnw = _norm_mod_bwd(du, n, xh, r, nw_ref[...], scale)
        dx_ref[...] = jnp.where(lat, dxo_ref[...], 0.0) + dx

        @pl.when(i <= 1)
        def _():
            acc_ref[...] = jnp.zeros_like(acc_ref)

        acc_ref[0, 0:1, :] += dshift
        acc_ref[0, 1:2, :] += dscale
        acc_ref[0, 3:4, :] += dnw

    tile = lambda w: pl.BlockSpec((TM, w), lambda i: (i, 0))
    ltile = lambda w: pl.BlockSpec((TM, w), lambda i: (jnp.maximum(i - 1, 0), 0))
    return pl.pallas_call(
        body, name=name, grid=(n_tok // TM,),
        in_specs=[ltile(D), tile(D), tile(NQKV), ltile(1024), ltile(NPOOL), ltile(2048), tile(128), tile(128),
                  pl.BlockSpec((1, 16, D), lambda i: (jnp.minimum(i, 1), 0, 0)), _const_spec((1, D)), _const_spec(w_mix.shape)],
        out_specs=[tile(D), tile(NMIXP), pl.BlockSpec((1, 8, D), lambda i: (jnp.minimum(i, 1), 0, 0))],
        out_shape=[jax.ShapeDtypeStruct((n_tok, D), F32), jax.ShapeDtypeStruct((n_tok, NMIXP), BF),
                   jax.ShapeDtypeStruct((2, 8, D), F32)],
        compiler_params=_cparams("arbitrary"),
    )(dxo, x1, dqkv, dgate, dpool, dbr, dab_f, dab_b, modv, nw, w_mix)


def _qkv_act(pre, part):
    s = _silu(pre)
    if part == 2:
        return s
    nrm = s * lax.rsqrt(jnp.sum(s * s, axis=-1, keepdims=True) + EPS)
    return nrm * HD ** -0.5 if part == 0 else nrm


def _halo_specs(nt):
    r = TM // 8
    main = pl.BlockSpec((TM, NQKV), lambda i: (i, 0))
    prev = pl.BlockSpec((8, NQKV), lambda i: (jnp.maximum(i * r - 1, 0), 0))
    nxt = pl.BlockSpec((8, NQKV), lambda i: (jnp.minimum((i + 1) * r, nt * r - 1), 0))
    return main, prev, nxt


def _prep_fwd(p_qkv, conv_w8, *, name):
    n_tok = p_qkv.shape[0]
    nt = n_tok // TM

    def body(x_ref, xp_ref, xn_ref, w_ref, o_ref, pre_ref, win):
        i = pl.program_id(0)
        has_prev = (i != 0) & (i != 1)
        has_next = (i != 0) & (i != nt - 1)
        win[0:8, :] = jnp.where(has_prev, xp_ref[...], 0.0)
        win[8:8 + TM, :] = x_ref[...]
        win[8 + TM:, :] = jnp.where(has_next, xn_ref[...], 0.0)
        for hb in range(3 * NH):
            hs = slice(hb * HD, (hb + 1) * HD)
            pre = win[6:6 + TM, hs] * w_ref[0:1, hs]
            for k in range(1, 5):
                pre = pre + win[6 + k:6 + k + TM, hs] * w_ref[k:k + 1, hs]
            pre_ref[:, hs] = pre
            o_ref[:, hs] = _qkv_act(pre, hb // NH)

    main, prev, nxt = _halo_specs(nt)
    return pl.pallas_call(
        body, name=name, grid=(nt,),
        in_specs=[main, prev, nxt, pl.BlockSpec((8, NQKV), lambda i: (0, 0))],
        out_specs=[main, main], out_shape=[jax.ShapeDtypeStruct((n_tok, NQKV), F32)] * 2,
        scratch_shapes=[pltpu.VMEM((TM + 16, NQKV), F32)],
        compiler_params=_cparams("parallel"),
    )(p_qkv, p_qkv, p_qkv, conv_w8)


def _prep_bwd(p_qkv, pre, dqkv_f, dqkv_b, conv_w8, *, name):
    n_tok = p_qkv.shape[0]
    nt = n_tok // TM

    def body(x_ref, p_ref, pp_ref, pn_ref, g_ref, gp_ref, gn_ref, g2_ref, g2p_ref, g2n_ref, w_ref, dx_ref, dw_ref, pwin, gwin, dwin):
        i = pl.program_id(0)
        has_prev = (i != 0) & (i != 1)
        has_next = (i != 0) & (i != nt - 1)
        pwin[0:8, :] = jnp.where(has_prev, pp_ref[...], 0.0)
        pwin[8:8 + TM, :] = p_ref[...]
        pwin[8 + TM:, :] = jnp.where(has_next, pn_ref[...], 0.0)
        gwin[0:8, :] = jnp.where(has_prev, gp_ref[...] + g2p_ref[...], 0.0)
        gwin[8:8 + TM, :] = g_ref[...] + g2_ref[...]
        gwin[8 + TM:, :] = jnp.where(has_next, gn_ref[...] + g2n_ref[...], 0.0)

        @pl.when(i == 0)
        def _():
            dw_ref[...] = jnp.zeros_like(dw_ref)

        for hb in range(3 * NH):
            hs = slice(hb * HD, (hb + 1) * HD)
            _, vjp = jax.vjp(functools.partial(_qkv_act, part=hb // NH), pwin[:, hs])
            dwin[:, hs] = vjp(gwin[:, hs])[0]
            xv = x_ref[:, hs]
            dx = None
            for k in range(5):
                sh = dwin[10 - k:10 - k + TM, hs]
                dx = sh * w_ref[k:k + 1, hs] if dx is None else dx + sh * w_ref[k:k + 1, hs]
                dw_ref[k:k + 1, hs] += jnp.sum(sh * xv, axis=0, keepdims=True)
            dx_ref[:, hs] = dx

    main, prev, nxt = _halo_specs(nt)
    wspec = pl.BlockSpec((8, NQKV), lambda i: (0, 0))
    return pl.pallas_call(
        body, name=name, grid=(nt,),
        in_specs=[main, main, prev, nxt, main, prev, nxt, main, prev, nxt, wspec],
        out_specs=[main, wspec],
        out_shape=[jax.ShapeDtypeStruct((n_tok, NQKV), F32), jax.ShapeDtypeStruct((8, NQKV), F32)],
        scratch_shapes=[pltpu.VMEM((TM + 16, NQKV), F32)] * 3,
        compiler_params=_cparams("arbitrary"),
    )(p_qkv, pre, pre, pre, dqkv_f, dqkv_f, dqkv_f, dqkv_b, dqkv_b, dqkv_b, conv_w8)


@jax.custom_vjp
def _mm_nn(a, b):
    return _nn(a, b)


@jax.custom_vjp
def _mm_nt(a, b):
    return _nt(a, b)


@jax.custom_vjp
def _mm_tn(a, b):
    return _tn(a, b)


_mm_nn.defvjp(lambda a, b: (_nn(a, b), (a, b)), lambda r, g: (_mm_nt(g, r[1]), _mm_tn(r[0], g)))
_mm_nt.defvjp(lambda a, b: (_nt(a, b), (a, b)), lambda r, g: (_mm_nn(g, r[1]), _mm_tn(g, r[0])))
_mm_tn.defvjp(lambda a, b: (_tn(a, b), (a, b)), lambda r, g: (_mm_nt(r[1], g), _mm_nn(r[0], g)))


def _each(f, *lists):
    return tuple(f(*a) for a in zip(*lists))


def _unit_tri_inv(ls, revs):
    ii = lax.broadcasted_iota(jnp.int32, (CH, CH), 0)
    jj = lax.broadcasted_iota(jnp.int32, (CH, CH), 1)
    eye = (ii == jj).astype(F32)
    xs = None
    s = 1
    while s < CH:
        same = (ii & -(2 * s)) == (jj & -(2 * s))
        off = {False: same & ((ii & s) != 0) & ((jj & s) == 0), True: same & ((jj & s) != 0) & ((ii & s) == 0)}
        cs = _each(lambda l, r: jnp.where(off[r], l, 0.0), ls, revs)
        if xs is None:
            xs = _each(lambda c: eye - c, cs)
        else:
            xc = _each(_nn, xs, cs)
            xcx = _each(_nn, xc, xs)
            xs = _each(lambda x, t: x - t, xs, xcx)
        s *= 2
    return xs


@functools.lru_cache(maxsize=None)
def _tri_solve(revs):
    @jax.custom_vjp
    def solve(ls, rhss):
        return _each(_mm_nn, _unit_tri_inv(ls, revs), rhss)

    def fwd(ls, rhss):
        ainv = _unit_tri_inv(ls, revs)
        xs = _each(_mm_nn, ainv, rhss)
        return xs, (ainv, xs)

    def bwd(res, gs):
        ainv, xs = res
        drhs = _each(_mm_tn, ainv, gs)
        return _each(lambda d, x: -_mm_nt(d, x), drhs, xs), drhs

    solve.defvjp(fwd, bwd)
    return solve


def _chunk_prep(q, k, v, beta, g, *, revs):
    ii = lax.broadcasted_iota(jnp.int32, (CH, CH), 0)
    jj = lax.broadcasted_iota(jnp.int32, (CH, CH), 1)
    eye = ii == jj
    incl_of = {False: ii >= jj, True: ii <= jj}
    strict_of = {False: ii > jj, True: ii < jj}
    g_row = _each(lambda t: jnp.sum(jnp.where(eye, t, 0.0), axis=0, keepdims=True), g)
    cum = _each(lambda t, r: jnp.sum(jnp.where(incl_of[r], t, 0.0), axis=1, keepdims=True), g_row, revs)
    cum_row = _each(lambda t: jnp.sum(jnp.where(eye, t, 0.0), axis=0, keepdims=True), cum)
    total = _each(lambda t: jnp.sum(t, axis=0, keepdims=True), g)
    decay = _each(lambda c, cr, r: jnp.where(incl_of[r], jnp.exp(jnp.where(incl_of[r], c - cr, 0.0)), 0.0), cum, cum_row, revs)
    kb = _each(jnp.multiply, k, beta)
    vb = _each(jnp.multiply, v, beta)
    kk = _each(_mm_nt, kb, k)
    lmat = _each(lambda t, dc, r: jnp.where(strict_of[r], t * dc, 0.0), kk, decay, revs)
    ecum = _each(jnp.exp, cum)
    rhs = _each(lambda a, b, e: jnp.concatenate([a, b * e], axis=1), vb, kb, ecum)
    sol = _tri_solve(revs)(lmat, rhs)
    qk = _each(_mm_nt, q, k)
    aqk = _each(jnp.multiply, qk, decay)
    qd = _each(jnp.multiply, q, ecum)
    kd = _each(lambda a, t, c: a * jnp.exp(t - c), k, total, cum)
    return sol, aqk, qd, kd, _each(jnp.exp, total)


def _chunk_rec(sol, aqk, qd, kd, bl, s):
    ws = _each(lambda so, st: _mm_nn(so[:, HD:], st), sol, s)
    v_new = _each(lambda so, t: so[:, :HD] - t, sol, ws)
    qs = _each(_mm_nn, qd, s)
    av = _each(_mm_nn, aqk, v_new)
    o = _each(jnp.add, qs, av)
    kv = _each(_mm_tn, kd, v_new)
    s_new = _each(lambda st, b, u: st * b + u, s, bl, kv)
    return o, s_new


def _lane_col(x, c):
    lane = lax.broadcasted_iota(jnp.int32, x.shape, 1)
    return jnp.sum(jnp.where(lane == c, x, 0.0), axis=1, keepdims=True)


def _beta_g(ab, cst, d, h):
    braw = _lane_col(ab, NH * d + h)
    araw = _lane_col(ab, 2 * NH + NH * d + h)
    ea = _lane_col(cst[0:1, :], 2 * NH + NH * d + h)
    dt = _lane_col(cst[1:2, :], 2 * NH + NH * d + h)
    z = araw + dt
    softplus = jnp.maximum(z, 0.0) + jnp.log(1.0 + jnp.exp(-jnp.abs(z)))
    return jax.nn.sigmoid(braw), -ea * softplus, z, ea


STEPS = 2
TS = STEPS * CH
_CHAINS = tuple((t, d, h) for t in range(STEPS) for d in (0, 1) for h in range(NH))
_REVS = tuple(bool(d) for _, d, _ in _CHAINS)
_PER_STEP = 2 * NH


def _chain_inputs(refs, r0s, ab_refs, cst):
    hs = lambda h: slice(h * HD, (h + 1) * HD)
    abvs = [[ab_refs[d][pl.ds(r0s[t][d], CH), :] for d in (0, 1)] for t in range(STEPS)]
    q = _each(lambda c: refs[c[1]][0][pl.ds(r0s[c[0]][c[1]], CH), hs(c[2])], _CHAINS)
    k = _each(lambda c: refs[c[1]][1][pl.ds(r0s[c[0]][c[1]], CH), hs(c[2])], _CHAINS)
    v = _each(lambda c: refs[c[1]][2][pl.ds(r0s[c[0]][c[1]], CH), hs(c[2])], _CHAINS)
    bg = _each(lambda c: _beta_g(abvs[c[0]][c[1]], cst, c[1], c[2]), _CHAINS)
    return q, k, v, bg


def _of_step(parts, t):
    return tuple(p[t * _PER_STEP:(t + 1) * _PER_STEP] for p in parts)


def _scan_fwd(qkv, ab, cst, s0, *, row_blk0, nb, name, exchange=None):
    cb = TS // CH
    w = NH * HD

    def body(qf, kf, vf, abf, qb, kb, vb, abb, cst_ref, s0_ref, of_ref, ob_ref, sallf_ref, sallb_ref, sfin_ref, s_scr):
        i = pl.program_id(0)

        @pl.when(i == 0)
        def _():
            s_scr[...] = s0_ref[...]

        o_refs, sall_refs = (of_ref, ob_ref), (sallf_ref, sallb_ref)

        def chunks(ci, carry):
            cs = [(ci * STEPS + t, cb - 1 - ci * STEPS - t) for t in range(STEPS)]
            r0s = [tuple(pl.multiple_of(c * CH, CH) for c in ct) for ct in cs]
            q, k, v, bg = _chain_inputs(((qf, kf, vf), (qb, kb, vb)), r0s, (abf, abb), cst_ref[...])
            parts = _chunk_prep(q, k, v, _each(lambda t: t[0], bg), _each(lambda t: t[1], bg), revs=_REVS)
            s = _each(lambda c: s_scr[c[1], c[2]], _CHAINS[:_PER_STEP])
            for t in range(STEPS):
                for (_, d, h), sv in zip(_CHAINS, s):
                    sall_refs[d][cs[t][d], h] = sv
                o, s = _chunk_rec(*_of_step(parts, t), s)
                for (_, d, h), ov in zip(_CHAINS, o):
                    o_refs[d][pl.ds(r0s[t][d], CH), h * HD:(h + 1) * HD] = ov
            for (_, d, h), sv in zip(_CHAINS, s):
                s_scr[d, h] = sv
            return carry

        lax.fori_loop(0, cb // STEPS, chunks, 0)

        @pl.when(i == nb - 1)
        def _():
            sfin_ref[...] = s_scr[...]

    pos = (lambda i: i, lambda i: nb - 1 - i)
    col = lambda d, c: pl.BlockSpec((TS, w), lambda i: (row_blk0 + pos[d](i), c))
    abs_ = lambda d: pl.BlockSpec((TS, 128), lambda i: (row_blk0 + pos[d](i), 0))
    full4 = pl.BlockSpec((2, NH, HD, HD), lambda i: (0, 0, 0, 0))
    o_spec = lambda d: pl.BlockSpec((TS, w), lambda i: (pos[d](i), 0))
    sall_spec = lambda d: pl.BlockSpec((cb, NH, HD, HD), lambda i: (pos[d](i), 0, 0, 0))
    return _pallas(
        body, (qkv, qkv, qkv, ab, qkv, qkv, qkv, ab, cst, s0), name=name, grid=(nb,), sem=("arbitrary",), exchange=exchange,
        in_specs=[col(0, 0), col(0, 1), col(0, 2), abs_(0), col(1, 0), col(1, 1), col(1, 2), abs_(1),
                  pl.BlockSpec((8, 128), lambda i: (0, 0)), full4],
        out_specs=[o_spec(0), o_spec(1), sall_spec(0), sall_spec(1), full4],
        out_shape=[jax.ShapeDtypeStruct((nb * TS, w), F32)] * 2 + [jax.ShapeDtypeStruct((nb * cb, NH, HD, HD), F32)] * 2
        + [jax.ShapeDtypeStruct((2, NH, HD, HD), F32)],
        scratch_shapes=[pltpu.VMEM((2, NH, HD, HD), F32)])


def _scan_bwd(qkv, ab, cst, sall_f, sall_b, do, dsfin, dqkv_f, dqkv_b, dab_f, dab_b, dcst, *, row_blk0, nb, has_do, name,
              exchange=None):
    cb = TS // CH
    w = NH * HD

    def body(qf, kf, vf, abf, qb, kb, vb, abb, cst_ref, sallf_ref, sallb_ref, dof_ref, dob_ref, dsfin_ref, _f, _b, _af, _ab, dcst_in,
             dqkvf_ref, dqkvb_ref, dabf_ref, dabb_ref, dcst_ref, ds0_ref, ds_scr):
        i = pl.program_id(0)

        @pl.when(i == 0)
        def _():
            ds_scr[...] = dsfin_ref[...]
            dcst_ref[...] = dcst_in[...]

        lane = lax.broadcasted_iota(jnp.int32, (CH, 128), 1)
        lane1 = lax.broadcasted_iota(jnp.int32, (1, 128), 1)
        sall_refs, do_refs = (sallf_ref, sallb_ref), (dof_ref, dob_ref)
        dqkv_refs, dab_refs = (dqkvf_ref, dqkvb_ref), (dabf_ref, dabb_ref)

        def chunks(ci, carry):
            cs = [(cb - 1 - ci * STEPS - t, ci * STEPS + t) for t in range(STEPS)]
            r0s = [tuple(pl.multiple_of(c * CH, CH) for c in ct) for ct in cs]
            q, k, v, bg = _chain_inputs(((qf, kf, vf), (qb, kb, vb)), r0s, (abf, abb), cst_ref[...])
            beta, g = _each(lambda t: t[0], bg), _each(lambda t: t[1], bg)
            parts, prep_vjp = jax.vjp(functools.partial(_chunk_prep, revs=_REVS), q, k, v, beta, g)
            ds = _each(lambda c: ds_scr[c[1], c[2]], _CHAINS[:_PER_STEP])
            dparts = []
            for t in range(STEPS):
                s = _each(lambda c: sall_refs[c[1]][cs[t][c[1]], c[2]], _CHAINS[:_PER_STEP])
                _, rec_vjp = jax.vjp(_chunk_rec, *_of_step(parts, t), s)
                do_t = _each(lambda c: do_refs[c[1]][pl.ds(r0s[t][c[1]], CH), c[2] * HD:(c[2] + 1) * HD] if has_do
                             else jnp.zeros((CH, HD), F32), _CHAINS[:_PER_STEP])
                *dpt, ds = rec_vjp((do_t, ds))
                dparts.append(dpt)
            for (_, d, h), dsv in zip(_CHAINS, ds):
                ds_scr[d, h] = dsv
            dq, dk, dv, dbeta, dg = prep_vjp(tuple(sum((dparts[t][j] for t in range(STEPS)), ()) for j in range(len(dparts[0]))))
            dab = [[jnp.zeros((CH, 128), F32), jnp.zeros((CH, 128), F32)] for _ in range(STEPS)]
            dal = jnp.zeros((1, 128), F32)
            for n, (t, d, h) in enumerate(_CHAINS):
                for part, val in enumerate((dq[n], dk[n], dv[n])):
                    dqkv_refs[d][pl.ds(r0s[t][d], CH), part * w + h * HD:part * w + (h + 1) * HD] = val
                z, ea = bg[n][2], bg[n][3]
                dbraw = dbeta[n] * beta[n] * (1.0 - beta[n])
                daraw = dg[n] * (-ea) * jax.nn.sigmoid(z)
                dab[t][d] = dab[t][d] + jnp.where(lane == NH * d + h, dbraw, 0.0) + jnp.where(lane == 2 * NH + NH * d + h, daraw, 0.0)
                dal = dal + jnp.where(lane1 == 2 * NH + NH * d + h, jnp.sum(dg[n] * g[n], axis=0, keepdims=True), 0.0)
            dsum = jnp.zeros((CH, 128), F32)
            for t in range(STEPS):
                for d in (0, 1):
                    dab_refs[d][pl.ds(r0s[t][d], CH), :] = dab[t][d]
                    dsum = dsum + dab[t][d]
            dcst_ref[0:1, :] += dal
            dcst_ref[1:2, :] += jnp.sum(jnp.where(lane >= 2 * NH, dsum, 0.0), axis=0, keepdims=True)
            return carry

        lax.fori_loop(0, cb // STEPS, chunks, 0)

        @pl.when(i == nb - 1)
        def _():
            ds0_ref[...] = ds_scr[...]

    pos = (lambda i: nb - 1 - i, lambda i: i)
    col = lambda d, c: pl.BlockSpec((TS, w), lambda i: (row_blk0 + pos[d](i), c))
    abs_ = lambda d: pl.BlockSpec((TS, 128), lambda i: (row_blk0 + pos[d](i), 0))
    full4 = pl.BlockSpec((2, NH, HD, HD), lambda i: (0, 0, 0, 0))
    small = pl.BlockSpec((8, 128), lambda i: (0, 0))
    hbm = pl.BlockSpec(memory_space=pl.ANY)
    sall_spec = lambda d: pl.BlockSpec((cb, NH, HD, HD), lambda i: (pos[d](i), 0, 0, 0))
    do_spec = (lambda d: pl.BlockSpec((TS, w), lambda i: (pos[d](i), 0))) if has_do else (lambda d: small)
    acc_specs = [pl.BlockSpec((TS, 3 * w), lambda i: (row_blk0 + pos[0](i), 0)),
                 pl.BlockSpec((TS, 3 * w), lambda i: (row_blk0 + pos[1](i), 0)), abs_(0), abs_(1), small]
    return _pallas(
        body, (qkv, qkv, qkv, ab, qkv, qkv, qkv, ab, cst, sall_f, sall_b, do, do, dsfin, dqkv_f, dqkv_b, dab_f, dab_b, dcst),
        name=name, grid=(nb,), sem=("arbitrary",), exchange=exchange,
        in_specs=[col(0, 0), col(0, 1), col(0, 2), abs_(0), col(1, 0), col(1, 1), col(1, 2), abs_(1), small,
                  sall_spec(0), sall_spec(1), do_spec(0), do_spec(1), full4, hbm, hbm, hbm, hbm, small],
        out_specs=acc_specs + [full4],
        out_shape=[jax.ShapeDtypeStruct(dqkv_f.shape, F32), jax.ShapeDtypeStruct(dqkv_b.shape, F32),
                   jax.ShapeDtypeStruct(dab_f.shape, F32), jax.ShapeDtypeStruct(dab_b.shape, F32),
                   jax.ShapeDtypeStruct((8, 128), F32), jax.ShapeDtypeStruct((2, NH, HD, HD), F32)],
        aliases={14: 0, 15: 1, 16: 2, 17: 3, 18: 4},
        scratch_shapes=[pltpu.VMEM((2, NH, HD, HD), F32)])


def _pool(xin, *, row0, transpose, name):
    n_tok = xin.shape[0] - row0
    rows = n_tok // GW
    pad = 8 * GW
    tt = 512
    gsh = GW.bit_length() - 1

    def body(x_ref, o_ref, ybuf):
        ii = lax.broadcasted_iota(jnp.int32, (128, 128), 0)
        jj = lax.broadcasted_iota(jnp.int32, (128, 128), 1)
        same_row = (ii >> gsh) == (jj >> gsh)
        ci, cj = ii & (GW - 1), jj & (GW - 1)
        tok = lax.broadcasted_iota(jnp.int32, (tt, 1), 0)
        zpad = jnp.zeros((pad, 128), F32)
        for gi, wdw in enumerate(POOL_WINDOWS):
            lo, hi = wdw // 2, wdw - wdw // 2
            if transpose:
                band = same_row & (ci - cj >= -lo) & (ci - cj < hi)
                offs = range(-hi + 1, lo + 1)
            else:
                band = same_row & (cj - ci >= -lo) & (cj - ci < hi)
                offs = range(-lo, hi)
            bandm = band.astype(BF)
            cs = slice(gi * 128, (gi + 1) * 128)
            ybuf[0:pad, :] = zpad
            ybuf[pad + n_tok:, :] = zpad

            def inv_area(t0):
                t = t0 + tok
                r, c = t >> gsh, t & (GW - 1)
                nr = jnp.minimum(r + hi, rows) - jnp.maximum(r - lo, 0)
                nc = jnp.minimum(c + hi, GW) - jnp.maximum(c - lo, 0)
                return 1.0 / (nr * nc).astype(F32)

            def col_pass(b, carry):
                t0 = pl.multiple_of(b * tt, tt)
                xv = x_ref[pl.ds(row0 + t0, tt), cs]
                if transpose:
                    xv = xv * inv_area(t0)
                hi_part = xv.astype(BF)
                lo_part = (xv - hi_part.astype(F32)).astype(BF)
                for s in range(tt // 128):
                    sl = slice(s * 128, (s + 1) * 128)
                    y = (jnp.dot(bandm, hi_part[sl], preferred_element_type=F32)
                         + jnp.dot(bandm, lo_part[sl], preferred_element_type=F32))
                    ybuf[pl.ds(pad + t0 + s * 128, 128), :] = y
                return carry

            lax.fori_loop(0, n_tok // tt, col_pass, 0)

            def row_pass(b, carry):
                t0 = pl.multiple_of(b * tt, tt)
                acc = ybuf[pl.ds(pad + t0 + offs[0] * GW, tt), :]
                for dr in offs[1:]:
                    acc = acc + ybuf[pl.ds(pad + t0 + dr * GW, tt), :]
                xv = x_ref[pl.ds(row0 + t0, tt), cs]
                if not transpose:
                    acc = acc * inv_area(t0)
                o_ref[pl.ds(t0, tt), cs] = acc - xv
                return carry

            lax.fori_loop(0, n_tok // tt, row_pass, 0)

    return pl.pallas_call(
        body, name=name, out_shape=jax.ShapeDtypeStruct((n_tok, NPOOL), F32),
        in_specs=[pl.BlockSpec(memory_space=pltpu.VMEM)], out_specs=pl.BlockSpec(memory_space=pltpu.VMEM),
        scratch_shapes=[pltpu.VMEM((n_tok + 2 * pad, 128), F32)],
        compiler_params=pltpu.CompilerParams(vmem_limit_bytes=VMEM_LIMIT),
    )(xin)


def _merge_parts(of, ob, pgate, pd, br, gnw, pw_ref, pscale, wg_ref, wp_ref):
    o = of + ob
    ons, ohs, rs = [], [], []
    for h in range(NH):
        oh = o[:, h * HD:(h + 1) * HD]
        r = lax.rsqrt(jnp.mean(oh * oh, axis=-1, keepdims=True) + EPS)
        ohs.append(oh * r)
        rs.append(r)
        ons.append(oh * r * gnw)
    on = jnp.concatenate(ons, axis=1)
    sig_gate = jax.nn.sigmoid(pgate)
    silu_gate = pgate * sig_gate
    og = on * silu_gate
    y_gdn = _nn(og, wg_ref[...])
    ypre = jnp.concatenate([_nn(pd[:, g * 128:(g + 1) * 128], pw_ref[g]) for g in range(4)], axis=1)
    yp = ypre * pscale
    y_pool = _nn(yp, wp_ref[...])
    g_pool = jax.nn.sigmoid(br[:, :D])
    g_gdn = jax.nn.sigmoid(br[:, D:])
    return dict(on=on, ohs=ohs, rs=rs, og=og, y_gdn=y_gdn, ypre=ypre, yp=yp, y_pool=y_pool, g_pool=g_pool, g_gdn=g_gdn,
                sig_gate=sig_gate, silu_gate=silu_gate)


def _merge_fwd(x1, of, ob, pgate, pd, br, modv, gnw, pool_w, pscale, w_gdn, w_pool, w_mo, *, name):
    n_tok = of.shape[0]

    def body(x_ref, of_ref, ob_ref, pg_ref, pd_ref, br_ref, mod_ref, gnw_ref, pw_ref, ps_ref, wg_ref, wp_ref, wmo_ref,
             x2_ref, og_ref, yp_ref, m_ref, mix_ref):
        t = _merge_parts(of_ref[...], ob_ref[...], pg_ref[...], pd_ref[...], br_ref[...], gnw_ref[...], pw_ref, ps_ref[...],
                         wg_ref, wp_ref)
        m = t["g_pool"] * t["y_pool"] + t["g_gdn"] * t["y_gdn"]
        mix = _nn(m, wmo_ref[...])
        og_ref[...] = t["og"].astype(BF)
        yp_ref[...] = t["yp"].astype(BF)
        m_ref[...] = m.astype(BF)
        mix_ref[...] = mix.astype(BF)
        x2_ref[...] = x_ref[...] + mod_ref[0, 5:6, :] * mix

    tile = lambda w: pl.BlockSpec((TM, w), lambda i: (i, 0))
    ctile = lambda w: pl.BlockSpec((TM, w), lambda i: (i + 1, 0))
    return pl.pallas_call(
        body, name=name, grid=(n_tok // TM,),
        in_specs=[ctile(D), tile(D), tile(D), ctile(D), tile(NPOOL), ctile(2 * D),
                  pl.BlockSpec((1, 16, D), lambda i: (1, 0, 0)), _const_spec((1, HD)), _const_spec((4, 128, 128)),
                  _const_spec((1, NPOOL)), _const_spec((D, D)), _const_spec((NPOOL, D)), _const_spec((D, D))],
        out_specs=[tile(D), tile(D), tile(NPOOL), tile(D), tile(D)],
        out_shape=[jax.ShapeDtypeStruct((n_tok, D), F32), jax.ShapeDtypeStruct((n_tok, D), BF),
                   jax.ShapeDtypeStruct((n_tok, NPOOL), BF), jax.ShapeDtypeStruct((n_tok, D), BF),
                   jax.ShapeDtypeStruct((n_tok, D), BF)],
        compiler_params=_cparams("parallel"),
    )(x1, of, ob, pgate, pd, br, modv, gnw, pool_w, pscale, w_gdn, w_pool, w_mo)


def _merge_bwd(dx2, mix, of, ob, pgate, pd, br, modv, gnw, pool_w, pscale, w_gdn, w_pool, w_mo, *, name):
    n_tok = of.shape[0]

    def body(dx2_ref, mix_ref, of_ref, ob_ref, pg_ref, pd_ref, br_ref, mod_ref, gnw_ref, pw_ref, ps_ref, wg_ref, wp_ref, wmo_ref,
             do_ref, dgate_ref, dpd_ref, dbr_ref, dmix_ref, dyg_ref, dyp_ref, acc_ref, dpw_ref):
        i = pl.program_id(0)
        pgate, pdv, gnw = pg_ref[...], pd_ref[...], gnw_ref[...]
        t = _merge_parts(of_ref[...], ob_ref[...], pgate, pdv, br_ref[...], gnw, pw_ref, ps_ref[...], wg_ref, wp_ref)
        dx2v = dx2_ref[...]
        dmix = mod_ref[0, 5:6, :] * dx2v
        dmixb = dmix.astype(BF)
        dmix_ref[...] = dmixb
        dm = _nt(dmixb, wmo_ref[...])
        gp, gg = t["g_pool"], t["g_gdn"]
        dbr_ref[:, :D] = dm * t["y_pool"] * gp * (1.0 - gp)
        dbr_ref[:, D:] = dm * t["y_gdn"] * gg * (1.0 - gg)
        dyp = (dm * gp).astype(BF)
        dyg = (dm * gg).astype(BF)
        dyp_ref[...] = dyp
        dyg_ref[...] = dyg
        dyp_in = _nt(dyp, wp_ref[...])
        dypre = dyp_in * ps_ref[...]
        for g in range(4):
            gs = slice(g * 128, (g + 1) * 128)
            dpd_ref[:, gs] = _nt(dypre[:, gs], pw_ref[g])
        dog = _nt(dyg, wg_ref[...])
        dgate_ref[...] = dog * t["on"] * (t["sig_gate"] * (1.0 + pgate - t["silu_gate"]))
        don = dog * t["silu_gate"]
        dgnw = jnp.zeros((1, HD), F32)
        for h in range(NH):
            hs = slice(h * HD, (h + 1) * HD)
            donh, oh, r = don[:, hs], t["ohs"][h], t["rs"][h]
            dgnw = dgnw + jnp.sum(donh * oh, axis=0, keepdims=True)
            doh = donh * gnw
            do_ref[:, hs] = r * (doh - oh * jnp.mean(doh * oh, axis=-1, keepdims=True))

        @pl.when(i == 0)
        def _():
            acc_ref[...] = jnp.zeros_like(acc_ref)
            dpw_ref[...] = jnp.zeros_like(dpw_ref)

        acc_ref[0:1, :] += jnp.sum(dx2v * mix_ref[...].astype(F32), axis=0, keepdims=True)
        acc_ref[1:2, 0:HD] += dgnw
        acc_ref[2:3, 0:NPOOL] += jnp.sum(dyp_in * t["ypre"], axis=0, keepdims=True)
        for g in range(4):
            gs = slice(g * 128, (g + 1) * 128)
            dpw_ref[g] += _tn(pdv[:, gs], dypre[:, gs])

    tile = lambda w: pl.BlockSpec((TM, w), lambda i: (i, 0))
    ctile = lambda w: pl.BlockSpec((TM, w), lambda i: (i + 1, 0))
    return pl.pallas_call(
        body, name=name, grid=(n_tok // TM,),
        in_specs=[tile(D), tile(D), tile(D), tile(D), ctile(D), tile(NPOOL), ctile(2 * D),
                  pl.BlockSpec((1, 16, D), lambda i: (1, 0, 0)), _const_spec((1, HD)), _const_spec((4, 128, 128)),
                  _const_spec((1, NPOOL)), _const_spec((D, D)), _const_spec((NPOOL, D)), _const_spec((D, D))],
        out_specs=[tile(D), tile(D), tile(NPOOL), tile(2 * D), tile(D), tile(D), tile(D),
                   pl.BlockSpec((8, D), lambda i: (0, 0)), pl.BlockSpec((4, 128, 128), lambda i: (0, 0, 0))],
        out_shape=[jax.ShapeDtypeStruct((n_tok, D), F32), jax.ShapeDtypeStruct((n_tok, D), F32),
                   jax.ShapeDtypeStruct((n_tok, NPOOL), F32), jax.ShapeDtypeStruct((n_tok, 2 * D), F32),
                   jax.ShapeDtypeStruct((n_tok, D), BF), jax.ShapeDtypeStruct((n_tok, D), BF), jax.ShapeDtypeStruct((n_tok, D), BF),
                   jax.ShapeDtypeStruct((8, D), F32), jax.ShapeDtypeStruct((4, 128, 128), F32)],
        compiler_params=_cparams("arbitrary"),
    )(dx2, mix, of, ob, pgate, pd, br, modv, gnw, pool_w, pscale, w_gdn, w_pool, w_mo)


def _final(x3, target, fnw, *, name):
    n_tok = x3.shape[0]

    def body(x_ref, t_ref, w_ref, dx_ref, acc_ref):
        xv, w = x_ref[...], w_ref[...]
        r = lax.rsqrt(jnp.mean(xv * xv, axis=-1, keepdims=True) + EPS)
        xh = xv * r
        err = xh * w - t_ref[...]
        dy = err * (1.0 / D)
        dxh = dy * w
        dx_ref[...] = r * (dxh - xh * jnp.mean(dxh * xh, axis=-1, keepdims=True))

        @pl.when(pl.program_id(0) == 0)
        def _():
            acc_ref[...] = jnp.zeros_like(acc_ref)

        acc_ref[0:1, :] += jnp.sum(dy * xh, axis=0, keepdims=True)
        acc_ref[1:2, :] += jnp.sum(err * err, axis=0, keepdims=True) * (0.5 / D)

    tile = pl.BlockSpec((TM, D), lambda i: (i, 0))
    return pl.pallas_call(
        body, name=name, grid=(n_tok // TM,),
        in_specs=[tile, tile, _const_spec((1, D))],
        out_specs=[tile, pl.BlockSpec((8, D), lambda i: (0, 0))],
        out_shape=[jax.ShapeDtypeStruct((n_tok, D), F32), jax.ShapeDtypeStruct((8, D), F32)],
        compiler_params=_cparams("arbitrary"),
    )(x3, target, fnw)


def _split(results, n):
    return (*results[:n], list(results[n:]))


def _local_step(ctx, x, target, modv, p, late=None):
    t_lat = x.shape[0]
    n_all = t_lat + TM
    nbc, nbx = TM // TS, t_lat // TS
    mod_lat = modv[1:2]
    gather = (lambda arrs: _ChipExchange(arrs, False)) if late else (lambda arrs: None)
    scatter = (lambda arrs: _ChipExchange(arrs, True)) if late else (lambda arrs: None)

    x1, h1, gu1, f1, *got = _ffn_fwd(x, modv, p["norm1"], p["w1_in"], p["w1_out"], mrow=0, name="ffn1_fwd", ctx=ctx,
                                     exchange=gather(late and late[0]))
    if late:
        p = {**p, "w_mix": _regroup_mix(_from_chip_major_cols(got[0])), "conv": jnp.pad(_from_chip_major_cols(got[1]), ((0, 3), (0, 0)))}
    u, p_qkv, p_gate, p_pool, p_br, p_ab, *got = _mix_in_fwd(x1, modv, p["norm2"], p["w_mix"], name="mix_in_fwd",
                                                              exchange=gather(late and late[2]))
    if late:
        p = {**p, "w_gdn": got[0].reshape(D, D), "w_pool": _from_chip_major_cols(got[1]), "w_mo": got[2].reshape(D, D)}
    qkv, pre_qkv = _prep_fwd(p_qkv, p["conv"], name="prep_fwd")
    s_zero = jnp.zeros((2, NH, HD, HD), F32)
    _, _, sall_cf, sall_cb, s_ctx = _scan_fwd(qkv, p_ab, p["cst"], s_zero, row_blk0=0, nb=nbc, name="scan_ctx")
    o_f, o_b, sall_f, sall_b, _, *got = _scan_fwd(qkv, p_ab, p["cst"], s_ctx, row_blk0=nbc, nb=nbx, name="scan_lat",
                                                  exchange=gather(late and late[1]))
    if late:
        p = {**p, "w2_in": got[0], "w2_out": got[1].reshape(FF, D)}
    pd = _pool(p_pool, row0=TM, transpose=False, name="pool_fwd")
    merge_w = (modv, p["gnw"], p["pool_w"], p["pscale"], p["w_gdn"], p["w_pool"], p["w_mo"])
    x2, og, yp, m, mix = _merge_fwd(x1, o_f, o_b, p_gate, pd, p_br, *merge_w, name="merge_fwd")
    x3, h3, gu3, f3 = _ffn_fwd(x2, mod_lat, p["norm3"], p["w2_in"], p["w2_out"], mrow=6, name="ffn2_fwd")
    dx3, acc_fin = _final(x3, target, p["fnorm"], name="final")

    dx2, a3, df3, dgu3, acc3 = _ffn_bwd(dx3, x2, gu3, f3, mod_lat, p["norm3"], p["w2_in"], p["w2_out"], mrow=6,
                                        name="ffn2_bwd")
    g = {}
    tkl = _k_tile(t_lat)
    g["w2_out"] = _matmul_tn(a3, df3, tmm=FF // 2, tn=D, tk=tkl, name="ffn2_wout_grad").reshape(NCHIP, FF // NCHIP, D)
    g["w2_in"] = _matmul_tn(h3, dgu3, tmm=D, tn=2 * FF // NCHIP, tk=tkl, nsplit=NCHIP, name="ffn2_win_grad")
    do, dgate, dpd, dbr, dmix, dyg, dyp, acc_m, dpw = _merge_bwd(dx2, mix, o_f, o_b, p_gate, pd, p_br, *merge_w, name="merge_bwd")
    g["w_mo"] = _matmul_tn(m, dmix, tmm=D, tn=D, tk=tkl, name="wmo_grad").reshape(NCHIP, D // NCHIP, D)
    g["w_gdn"] = _matmul_tn(og, dyg, tmm=D, tn=D, tk=tkl, name="wgdn_grad").reshape(NCHIP, D // NCHIP, D)
    g["w_pool"] = _matmul_tn(yp, dyp, tmm=NPOOL, tn=D // NCHIP, tk=tkl, nsplit=NCHIP, name="wpool_grad")
    dpool_in = _pool(dpd, row0=0, transpose=True, name="pool_bwd")
    acc = (lax.empty((n_all, NQKV), F32), lax.empty((n_all, NQKV), F32), lax.empty((n_all, 128), F32),
           lax.empty((n_all, 128), F32), jnp.zeros((8, 128), F32))
    behind_scan = ("w2_in", "w2_out", "w_gdn", "w_pool", "w_mo")
    *acc, ds_ctx, landed = _split(_scan_bwd(qkv, p_ab, p["cst"], sall_f, sall_b, do, s_zero, *acc, row_blk0=nbc, nb=nbx, has_do=True,
                                            name="scan_lat_bwd", exchange=scatter([g[k] for k in behind_scan])), 6)
    landed = dict(zip(behind_scan, landed))
    dqkv_f, dqkv_b, dab_f, dab_b, dcst, _ = _scan_bwd(qkv, p_ab, p["cst"], sall_cf, sall_cb, jnp.zeros((8, 128), F32), ds_ctx, *acc,
                                                      row_blk0=0, nb=nbc, has_do=False, name="scan_ctx_bwd")
    dpqkv, dconv = _prep_bwd(p_qkv, pre_qkv, dqkv_f, dqkv_b, p["conv"], name="prep_bwd")
    dx1, dp, acc_mix = _mix_in_bwd(dx2, x1, dpqkv, dgate, dpool_in, dbr, dab_f, dab_b, modv, p["norm2"], p["w_mix"],
                                   name="mix_in_bwd")
    tka = _k_tile(n_all)
    g["w_mix"] = _chip_major_cols(_ungroup_mix(_matmul_tn(u, dp, tmm=256, tn=NMIXP, tk=_k_tile(n_all, 1024), name="wmix_grad")))
    dx_lat, a1, df1, dgu1, acc1, got = _split(_ffn_bwd(dx1, x, gu1, f1, modv, p["norm1"], p["w1_in"], p["w1_out"], mrow=0, ctx=ctx,
                                                   name="ffn1_bwd", exchange=scatter([g["w_mix"]])), 5)
    landed.update(zip(("w_mix",), got))
    g["w1_out"] = _matmul_tn(a1, df1, tmm=FF // 2, tn=D, tk=tka, name="ffn1_wout_grad").reshape(NCHIP, FF // NCHIP, D)
    g["w1_in"] = _matmul_tn(h1, dgu1, tmm=D, tn=2 * FF // NCHIP, tk=tka, nsplit=NCHIP, name="ffn1_win_grad")

    small = dict(norm1=acc1[0, 3] + acc1[1, 3], norm2=acc_mix[0, 3] + acc_mix[1, 3], norm3=acc3[0, 3], fnorm=acc_fin[0],
                 gnw=acc_m[1, :HD], pscale=acc_m[2, :NPOOL], pool_w=dpw, conv=dconv[:5],
                 a_log=dcst[0, 2 * NH:4 * NH], dt_bias=dcst[1, 2 * NH:4 * NH])
    zero = jnp.zeros((D,), F32)
    dmod = jnp.stack([
        jnp.stack([acc1[0, 0], acc1[0, 1], acc1[0, 2], acc_mix[0, 0], acc_mix[0, 1], zero, zero, zero, zero]),
        jnp.stack([acc1[1, 0], acc1[1, 1], acc1[1, 2], acc_mix[1, 0], acc_mix[1, 1], acc_m[0], acc3[0, 0], acc3[0, 1], acc3[0, 2]]),
    ])
    return jnp.sum(acc_fin[1]), dx_lat, g, landed, small, dmod


_HI = lax.Precision.HIGHEST


def _ada_fwd(c_all, w_sh, b_sh, *, name):
    def body(c_ref, w_ref, b_ref, o_ref):
        o_ref[...] = jnp.dot(_silu(c_ref[...]), w_ref[...], precision=_HI, preferred_element_type=F32) + b_ref[...]

    return pl.pallas_call(body, name=name, out_shape=jax.ShapeDtypeStruct((16, w_sh.shape[1]), F32),
                          compiler_params=pltpu.CompilerParams(vmem_limit_bytes=VMEM_LIMIT))(c_all, w_sh, b_sh)


def _ada_bwd(c_all, dm, w_sh, *, name):
    def body(c_ref, dm_ref, w_ref, dw_ref, dc_ref):
        sc = _silu(c_ref[...])
        dw_ref[...] = lax.dot_general(sc, dm_ref[...], (((0,), (0,)), ((), ())), precision=_HI, preferred_element_type=F32)
        part = lax.dot_general(dm_ref[8:9, :], w_ref[...], (((1,), (1,)), ((), ())), precision=_HI, preferred_element_type=F32)
        dc_ref[...] = jnp.broadcast_to(part, dc_ref.shape)

    return pl.pallas_call(body, name=name,
                          out_shape=[jax.ShapeDtypeStruct(w_sh.shape, F32), jax.ShapeDtypeStruct((8, D), F32)],
                          compiler_params=pltpu.CompilerParams(vmem_limit_bytes=VMEM_LIMIT))(c_all, dm, w_sh)


def _cctx_grad(parts, c_ctx, *, name):
    def body(p_ref, c_ref, o_ref):
        tot = (p_ref[0, 0:1, :] + p_ref[2, 0:1, :]) + (p_ref[4, 0:1, :] + p_ref[6, 0:1, :])
        o_ref[...] = tot * _dsilu(c_ref[...])

    return pl.pallas_call(body, name=name, out_shape=jax.ShapeDtypeStruct((1, D), F32))(parts, c_ctx)


_MESH = pl.DeviceIdType.MESH
_ANY = pl.BlockSpec(memory_space=pl.ANY)


def _flip(v, bit):
    return (1 - v) if bit else v


def _all_gather8(x, *, name):
    def body(x_ref, out_ref, send_sems, recv_sems, local_sem):
        mx, my, mc = lax.axis_index("x"), lax.axis_index("y"), lax.axis_index("c")
        me = 4 * mx + 2 * my + mc
        mine = pltpu.make_async_copy(x_ref, out_ref.at[me], local_sem)
        mine.start()
        sends, recvs = [], []
        for k in range(1, 8):
            px, py, pc = _flip(mx, k & 4), _flip(my, k & 2), _flip(mc, k & 1)
            sends.append(pltpu.make_async_remote_copy(src_ref=x_ref, dst_ref=out_ref.at[me], send_sem=send_sems.at[k - 1],
                                                      recv_sem=recv_sems.at[k - 1], device_id=(px, py, pc), device_id_type=_MESH))
            recvs.append(pltpu.make_async_remote_copy(src_ref=x_ref, dst_ref=out_ref.at[4 * px + 2 * py + pc],
                                                      send_sem=send_sems.at[k - 1], recv_sem=recv_sems.at[k - 1],
                                                      device_id=(px, py, pc), device_id_type=_MESH))
        for cp in sends:
            cp.start()
        for cp in recvs:
            cp.wait_recv()
        for cp in sends:
            cp.wait_send()
        mine.wait()

    vm = pl.BlockSpec(memory_space=pltpu.VMEM)
    return pl.pallas_call(
        body, name=name, out_shape=jax.ShapeDtypeStruct((8,) + x.shape, x.dtype), in_specs=[vm], out_specs=vm,
        scratch_shapes=[pltpu.SemaphoreType.DMA((7,)), pltpu.SemaphoreType.DMA((7,)), pltpu.SemaphoreType.DMA],
        compiler_params=pltpu.CompilerParams(vmem_limit_bytes=VMEM_LIMIT),
    )(x)


class _ChipExchange:
    def __init__(self, arrs, scatter):
        self.arrs, self.scatter, self.n = list(arrs), scatter, len(arrs)
        self.out_shape = [jax.ShapeDtypeStruct(a.shape if scatter else (NCHIP,) + a.shape, a.dtype) for a in self.arrs]
        links = self.n * (NCHIP - 1)
        self.scratch = [pltpu.SemaphoreType.DMA((links,)), pltpu.SemaphoreType.DMA((links,)), pltpu.SemaphoreType.DMA((self.n,))]

    def copies(self, ins, outs, send_sems, recv_sems, local_sems):
        mx, my, mc = lax.axis_index("x"), lax.axis_index("y"), lax.axis_index("c")
        me = 2 * mx + my
        local, sends, recvs = [], [], []
        for j in range(self.n):
            src_own = ins[j].at[me] if self.scatter else ins[j]
            local.append(pltpu.make_async_copy(src_own, outs[j].at[me], local_sems.at[j]))
            for k in range(1, NCHIP):
                px, py = _flip(mx, k & 2), _flip(my, k & 1)
                peer = 2 * px + py
                sem = j * (NCHIP - 1) + k - 1
                src = ins[j].at[peer] if self.scatter else ins[j]
                sends.append(pltpu.make_async_remote_copy(src_ref=src, dst_ref=outs[j].at[me], send_sem=send_sems.at[sem],
                                                          recv_sem=recv_sems.at[sem], device_id=(px, py, mc), device_id_type=_MESH))
                recvs.append(pltpu.make_async_remote_copy(src_ref=src, dst_ref=outs[j].at[peer], send_sem=send_sems.at[sem],
                                                          recv_sem=recv_sems.at[sem], device_id=(px, py, mc), device_id_type=_MESH))
        return local, sends, recvs

    @staticmethod
    def start(local, sends, recvs):
        for cp in local + sends:
            cp.start()

    @staticmethod
    def finish(local, sends, recvs):
        for cp in recvs:
            cp.wait_recv()
        for cp in sends:
            cp.wait_send()
        for cp in local:
            cp.wait()


def _gather_split(arrs, *, name):
    n = len(arrs)
    links = n * (NCHIP - 1)

    def body(*refs):
        ins, outs = refs[:n], refs[n:2 * n]
        ici_send, ici_recv, d2d_send, d2d_recv, local_sems = refs[2 * n:]
        mx, my, mc = lax.axis_index("x"), lax.axis_index("y"), lax.axis_index("c")
        me = 2 * mx + my
        local, first, arrive, onward, handed = [], [], [], [], []
        for j in range(n):
            hr = ins[j].shape[0] // 2
            mine, other = pl.ds(mc * hr, hr), pl.ds((1 - mc) * hr, hr)
            local.append(pltpu.make_async_copy(ins[j], outs[j].at[me], local_sems.at[j]))
            for k in range(1, NCHIP):
                px, py = _flip(mx, k & 2), _flip(my, k & 1)
                peer = 2 * px + py
                sem = j * (NCHIP - 1) + k - 1
                ici = lambda slot: pltpu.make_async_remote_copy(
                    src_ref=ins[j].at[mine], dst_ref=outs[j].at[slot, mine], send_sem=ici_send.at[sem], recv_sem=ici_recv.at[sem],
                    device_id=(px, py, mc), device_id_type=_MESH)
                d2d = lambda rows: pltpu.make_async_remote_copy(
                    src_ref=outs[j].at[peer, rows], dst_ref=outs[j].at[peer, rows], send_sem=d2d_send.at[sem], recv_sem=d2d_recv.at[sem],
                    device_id=(mx, my, 1 - mc), device_id_type=_MESH)
                first.append(ici(me))
                arrive.append(ici(peer))
                onward.append(d2d(mine))
                handed.append(d2d(other))
        for cp in local + first:
            cp.start()
        for got, fwd in zip(arrive, onward):
            got.wait_recv()
            fwd.start()
        for cp in handed:
            cp.wait_recv()
        for cp in first + onward:
            cp.wait_send()
        for cp in local:
            cp.wait()

    sems = pltpu.SemaphoreType.DMA((links,))
    return pl.pallas_call(
        body, name=name, out_shape=[jax.ShapeDtypeStruct((NCHIP,) + a.shape, a.dtype) for a in arrs],
        in_specs=[_ANY] * n, out_specs=[_ANY] * n, scratch_shapes=[sems, sems, sems, sems, pltpu.SemaphoreType.DMA((n,))],
    )(*arrs)


_HBM = pl.BlockSpec(memory_space=pltpu.HBM)
_SEM = pl.BlockSpec(memory_space=pltpu.SEMAPHORE)
_DATAFLOW = pltpu.SideEffectType.DATAFLOW_SIDE_EFFECTING


def _scatter_copies(ins, lands, send_sems, recv_sems):
    mx, my, mc = lax.axis_index("x"), lax.axis_index("y"), lax.axis_index("c")
    me = 2 * mx + my
    sends, recvs = [], []
    for j in range(len(ins)):
        for k in range(1, NCHIP):
            px, py = _flip(mx, k & 2), _flip(my, k & 1)
            peer = 2 * px + py
            sem = j * (NCHIP - 1) + k - 1
            mk = lambda slot: pltpu.make_async_remote_copy(src_ref=ins[j].at[peer], dst_ref=lands[j].at[slot], send_sem=send_sems.at[sem],
                                                           recv_sem=recv_sems.at[sem], device_id=(px, py, mc), device_id_type=_MESH)
            sends.append(mk(me))
            recvs.append(mk(peer))
    return sends, recvs


def _scatter_start(arrs, after, *, name):
    n = len(arrs)
    links = n * (NCHIP - 1)
    n_in = 2 * n + len(after)

    def body(*refs):
        ins, lands = refs[:n], refs[n:2 * n]
        send_sems, recv_sems = refs[n_in], refs[n_in + 1]
        token = refs[-1]
        for cp in _scatter_copies(ins, lands, send_sems, recv_sems)[0]:
            cp.start()
        token[...] = jnp.zeros_like(token)

    hbm = lambda a: pltpu.HBM(a.shape, a.dtype)
    res = pl.pallas_call(
        body, name=name,
        out_shape=(pltpu.SemaphoreType.DMA((links,)), pltpu.SemaphoreType.DMA((links,)), *[hbm(a) for a in arrs], *[hbm(a) for a in arrs],
                   jax.ShapeDtypeStruct((8, 128), F32)),
        in_specs=[_HBM] * (2 * n) + [_ANY] * len(after), out_specs=(_SEM, _SEM, *[_HBM] * (2 * n), pl.BlockSpec(memory_space=pltpu.VMEM)),
        input_output_aliases={j: 2 + j for j in range(2 * n)},
        compiler_params=pltpu.CompilerParams(has_side_effects=_DATAFLOW),
    )(*[pltpu.with_memory_space_constraint(a, pltpu.HBM) for a in arrs],
      *[pltpu.with_memory_space_constraint(lax.empty(a.shape, a.dtype), pltpu.HBM) for a in arrs], *after)
    return res[0], res[1], list(res[2:2 + n]), list(res[2 + n:2 + 2 * n]), res[-1]


def _scatter_wait(send_sems, recv_sems, arrs, lands, after, *, name):
    n = len(arrs)

    def body(*refs):
        ins, lands_in = refs[:n], refs[n:2 * n]
        sends, recvs = _scatter_copies(ins, lands_in, refs[2 * n], refs[2 * n + 1])
        for cp in sends:
            cp.wait_send()
        for cp in recvs:
            cp.wait_recv()

    hbm = lambda a: pltpu.HBM(a.shape, a.dtype)
    res = pl.pallas_call(
        body, name=name, out_shape=(*[hbm(a) for a in arrs], *[hbm(a) for a in lands]),
        in_specs=[_HBM] * (2 * n) + [_SEM, _SEM] + [_ANY] * len(after), out_specs=[_HBM] * (2 * n),
        input_output_aliases={j: j for j in range(2 * n)},
        compiler_params=pltpu.CompilerParams(has_side_effects=_DATAFLOW),
    )(*arrs, *lands, send_sems, recv_sems, *after)
    return list(res[:n]), list(res[n:])


def _pallas(body, operands, *, name, grid, in_specs, out_specs, out_shape, sem, scratch_shapes=(), aliases=None, exchange=None):
    if exchange is None:
        return pl.pallas_call(body, name=name, grid=grid, in_specs=in_specs, out_specs=out_specs, out_shape=out_shape,
                              scratch_shapes=list(scratch_shapes), input_output_aliases=aliases or {},
                              compiler_params=_cparams(*sem))(*operands)
    ex = exchange
    (steps,) = grid
    n_in, n_out, n_scr, k = len(in_specs), len(out_specs), len(scratch_shapes), ex.n

    def hosted(*refs):
        ins, refs = refs[:n_in], refs[n_in:]
        ex_in, refs = refs[:k], refs[k:]
        outs, refs = refs[:n_out], refs[n_out:]
        ex_out, refs = refs[:k], refs[k:]
        scr, ex_sems = refs[:n_scr], refs[n_scr:]
        cps = ex.copies(ex_in, ex_out, *ex_sems)
        pl.when(pl.program_id(0) == 0)(lambda: ex.start(*cps))
        body(*ins, *outs, *scr)
        pl.when(pl.program_id(0) == steps - 1)(lambda: ex.finish(*cps))

    return pl.pallas_call(
        hosted, name=name, grid=grid, in_specs=list(in_specs) + [_ANY] * k, out_specs=list(out_specs) + [_ANY] * k,
        out_shape=list(out_shape) + ex.out_shape, scratch_shapes=list(scratch_shapes) + ex.scratch,
        input_output_aliases=aliases or {}, compiler_params=_cparams("arbitrary"),
    )(*operands, *ex.arrs)


def _core_swap(arrs, *, name):
    n = len(arrs)

    def body(*refs):
        ins, outs = refs[:n], refs[n:2 * n]
        send_sems, recv_sems = refs[2 * n:]
        sib = (lax.axis_index("x"), lax.axis_index("y"), 1 - lax.axis_index("c"))
        cps = [pltpu.make_async_remote_copy(src_ref=ins[j], dst_ref=outs[j], send_sem=send_sems.at[j], recv_sem=recv_sems.at[j],
                                            device_id=sib, device_id_type=_MESH) for j in range(n)]
        for cp in cps:
            cp.start()
        for cp in cps:
            cp.wait_recv()
        for cp in cps:
            cp.wait_send()

    return pl.pallas_call(
        body, name=name, out_shape=[jax.ShapeDtypeStruct(a.shape, a.dtype) for a in arrs],
        in_specs=[_ANY] * n, out_specs=[_ANY] * n,
        scratch_shapes=[pltpu.SemaphoreType.DMA((n,)), pltpu.SemaphoreType.DMA((n,))],
    )(*arrs)


def _row_tile(rows, cols, budget=1 << 20):
    best = None
    for t in range(8, rows + 1, 8):
        if rows % t == 0 and t * cols <= budget:
            best = t
    return best or rows


def _sum_slots(x, *, name, after=()):
    ns, r, c = x.shape
    tr = _row_tile(r, c * ns)

    def body(x_ref, *refs):
        acc = x_ref[0].astype(F32)
        for s in range(1, ns):
            acc = acc + x_ref[s].astype(F32)
        refs[-1][...] = acc

    return pl.pallas_call(
        body, name=name, grid=(r // tr,), out_shape=jax.ShapeDtypeStruct((r, c), F32),
        in_specs=[pl.BlockSpec((ns, tr, c), lambda i: (0, i, 0))] + [_ANY] * len(after), out_specs=pl.BlockSpec((tr, c), lambda i: (i, 0)),
        compiler_params=_cparams("parallel"),
    )(x, *after)


def _adamw(w, ga, gb, m, v, *, name):
    r, c = w.shape
    tr = _row_tile(r, c, budget=1 << 18)
    two = gb is not None

    def body(*refs):
        w_ref, ga_ref = refs[0], refs[1]
        m_ref, v_ref = refs[2 + two], refs[3 + two]
        g_ref, d_ref, mo_ref, vo_ref = refs[4 + two:]
        g = ga_ref[...] + refs[2][...] if two else ga_ref[...]
        mn = ADAM_B1 * m_ref[...] + (1.0 - ADAM_B1) * g
        vn = ADAM_B2 * v_ref[...] + (1.0 - ADAM_B2) * (g * g)
        m_hat = mn / (1.0 - ADAM_B1 ** ADAM_STEP)
        v_hat = vn / (1.0 - ADAM_B2 ** ADAM_STEP)
        g_ref[...] = g
        d_ref[...] = -ADAM_LR * (m_hat / (jnp.sqrt(v_hat) + ADAM_EPS) + ADAM_WD * w_ref[...])
        mo_ref[...] = mn
        vo_ref[...] = vn

    spec = pl.BlockSpec((tr, c), lambda i: (i, 0))
    ins = [w, ga] + ([gb] if two else []) + [m, v]
    return pl.pallas_call(
        body, name=name, grid=(r // tr,), out_shape=[jax.ShapeDtypeStruct((r, c), F32)] * 4,
        in_specs=[spec] * len(ins), out_specs=[spec] * 4, compiler_params=_cparams("parallel"),
    )(*ins)


_MIX_AB0, _MIX_AB1 = NQKV, NQKV + 4 * NH


def _regroup_mix(w):
    pad = jnp.zeros((w.shape[0], NMIXP - NMIX), w.dtype)
    return jnp.concatenate([w[:, :_MIX_AB0], w[:, _MIX_AB1:], w[:, _MIX_AB0:_MIX_AB1], pad], axis=1)


def _ungroup_mix(w):
    n_ab = _MIX_AB1 - _MIX_AB0
    return jnp.concatenate([w[:, :_MIX_AB0], w[:, NMIX - n_ab:NMIX], w[:, _MIX_AB0:NMIX - n_ab]], axis=1)


def _chip_major_cols(w):
    r, c = w.shape
    return w.reshape(r, NCHIP, c // NCHIP).transpose(1, 0, 2)


def _from_chip_major_cols(w):
    return w.transpose(1, 0, 2).reshape(w.shape[1], -1)


_SMALL = (("c_ctx", D), ("b_ada", 9 * D), ("norm1_w", D), ("norm2_w", D), ("norm3_w", D), ("final_norm_w", D),
          ("a_log", 2 * NH), ("dt_bias", 2 * NH), ("gdn_norm_w", HD), ("pool_w", 4 * 128 * 128), ("pool_scale", NPOOL),
          ("conv_w", 5 * NQKV // NCHIP))


def _pack(vals, lanes=128, row_mult=8):
    flat = jnp.concatenate([jnp.ravel(v) for v in vals])
    n = flat.shape[0]
    rows = -(-n // (lanes * row_mult)) * row_mult
    return jnp.pad(flat, (0, rows * lanes - n)).reshape(rows, lanes)


def _unpack(packed, sizes):
    flat = packed.reshape(-1)
    out, o = [], 0
    for n in sizes:
        out.append(flat[o:o + n])
        o += n
    return out


def kernel(x, c, ctx, c_ctx, w_ada, b_ada, norm1_w, ffn1_w_in, ffn1_w_out, norm2_w, w_mix_in, conv_w, a_log, dt_bias, gdn_norm_w, w_gdn_proj, pool_w, pool_scale, w_pool_proj, w_mix_out, norm3_w, ffn2_w_in, ffn2_w_out, final_norm_w, loss_target, m_c_ctx, m_w_ada, m_b_ada, m_norm1_w, m_ffn1_w_in, m_ffn1_w_out, m_norm2_w, m_w_mix_in, m_conv_w, m_a_log, m_dt_bias, m_gdn_norm_w, m_w_gdn_proj, m_pool_w, m_pool_scale, m_w_pool_proj, m_w_mix_out, m_norm3_w, m_ffn2_w_in, m_ffn2_w_out, m_final_norm_w, v_c_ctx, v_w_ada, v_b_ada, v_norm1_w, v_ffn1_w_in, v_ffn1_w_out, v_norm2_w, v_w_mix_in, v_conv_w, v_a_log, v_dt_bias, v_gdn_norm_w, v_w_gdn_proj, v_pool_w, v_pool_scale, v_w_pool_proj, v_w_mix_out, v_norm3_w, v_ffn2_w_in, v_ffn2_w_out, v_final_norm_w):
    names = ("c_ctx", "w_ada", "b_ada", "norm1_w", "ffn1_w_in", "ffn1_w_out", "norm2_w", "w_mix_in", "conv_w", "a_log", "dt_bias",
             "gdn_norm_w", "w_gdn_proj", "pool_w", "pool_scale", "w_pool_proj", "w_mix_out", "norm3_w", "ffn2_w_in", "ffn2_w_out",
             "final_norm_w")
    w = dict(zip(names, (c_ctx, w_ada, b_ada, norm1_w, ffn1_w_in, ffn1_w_out, norm2_w, w_mix_in, conv_w, a_log, dt_bias, gdn_norm_w,
                         w_gdn_proj, pool_w, pool_scale, w_pool_proj, w_mix_out, norm3_w, ffn2_w_in, ffn2_w_out, final_norm_w)))
    mom = dict(zip(names, (m_c_ctx, m_w_ada, m_b_ada, m_norm1_w, m_ffn1_w_in, m_ffn1_w_out, m_norm2_w, m_w_mix_in, m_conv_w, m_a_log,
                           m_dt_bias, m_gdn_norm_w, m_w_gdn_proj, m_pool_w, m_pool_scale, m_w_pool_proj, m_w_mix_out, m_norm3_w,
                           m_ffn2_w_in, m_ffn2_w_out, m_final_norm_w)))
    var = dict(zip(names, (v_c_ctx, v_w_ada, v_b_ada, v_norm1_w, v_ffn1_w_in, v_ffn1_w_out, v_norm2_w, v_w_mix_in, v_conv_w, v_a_log,
                           v_dt_bias, v_gdn_norm_w, v_w_gdn_proj, v_pool_w, v_pool_scale, v_w_pool_proj, v_w_mix_out, v_norm3_w,
                           v_ffn2_w_in, v_ffn2_w_out, v_final_norm_w)))
    mx, my, mc = lax.axis_index("x"), lax.axis_index("y"), lax.axis_index("c")
    chip = 2 * mx + my
    dev = 2 * chip + mc
    ada_cols = w_ada.shape[2]

    c_rows = _all_gather8(jnp.pad(c, ((0, 7), (0, 0))), name="gather_c")[:, 0, :]
    c_all = jnp.concatenate([c_rows, c_ctx[None], jnp.zeros((7, D), F32)], axis=0)
    b_sh = lax.dynamic_slice(b_ada, (0, chip * ada_cols), (1, ada_cols))
    mod_sh = _ada_fwd(c_all, w_ada[0], b_sh, name="ada_fwd")
    mod_parts = _all_gather8(mod_sh, name="gather_mod")
    mod_all = jnp.concatenate([mod_parts[2 * s] for s in range(NCHIP)], axis=1)
    mod_lat = lax.dynamic_index_in_dim(mod_all, dev, axis=0, keepdims=False).reshape(9, D)
    modv = jnp.zeros((2, 16, D), F32).at[0, :9].set(mod_all[8].reshape(9, D)).at[1, :9].set(mod_lat)

    big = ("ffn1_w_in", "ffn1_w_out", "w_mix_in", "w_gdn_proj", "w_pool_proj", "w_mix_out", "ffn2_w_in", "ffn2_w_out")
    shard = {k: w[k][0].astype(BF) for k in big}
    w1_in, w1_out = _gather_split([shard["ffn1_w_in"], shard["ffn1_w_out"]], name="gather_ffn1")
    p = dict(
        norm1=norm1_w, norm2=norm2_w, norm3=norm3_w, fnorm=final_norm_w[None], w1_in=w1_in, w1_out=w1_out.reshape(FF, D),
        cst=jnp.zeros((8, 128), F32).at[0, 2 * NH:4 * NH].set(jnp.exp(a_log).reshape(-1)).at[1, 2 * NH:4 * NH].set(dt_bias.reshape(-1)),
        gnw=gdn_norm_w, pool_w=pool_w[0], pscale=pool_scale)
    late = ([shard["w_mix_in"], conv_w[0]], [shard["ffn2_w_in"], shard["ffn2_w_out"]],
            [shard["w_gdn_proj"], shard["w_pool_proj"], shard["w_mix_out"]])

    loss_dev, dx_lat, g, landed, small, dmod = _local_step(ctx[0], x[0], loss_target[0], modv, p, late)
    loss = lax.psum(loss_dev, ("x", "y", "c"))
    grad_x = dx_lat[None]

    small_vals = [dmod[1], dmod[0], small["norm1"], small["norm2"], small["norm3"], small["fnorm"], small["a_log"], small["dt_bias"],
                  small["gnw"], small["pool_w"], small["pscale"], small["conv"]]
    small_sizes = [v.size for v in small_vals]
    packed = _all_gather8(_pack(small_vals), name="gather_small")
    tot = _unpack(_sum_slots(packed, name="sum_small"), small_sizes)
    dmod_lat_all = packed[:, :9 * D // 128, :].reshape(8, 9 * D)
    dm = jnp.concatenate([dmod_lat_all, tot[1][None], jnp.zeros((7, 9 * D), F32)], axis=0)
    dm_sh = lax.dynamic_slice(dm, (0, chip * ada_cols), (16, ada_cols))
    g_w_ada, cctx_part = _ada_bwd(c_all, dm_sh, w_ada[0], name="ada_bwd")
    g_c_ctx = _cctx_grad(_all_gather8(cctx_part, name="gather_cctx"), c_ctx[None], name="cctx_grad")[0]
    conv_tot = tot[11].reshape(5, NQKV)
    g_small = dict(c_ctx=g_c_ctx, b_ada=tot[0] + tot[1], norm1_w=tot[2], norm2_w=tot[3], norm3_w=tot[4], final_norm_w=tot[5],
                   a_log=tot[6], dt_bias=tot[7], gdn_norm_w=tot[8], pool_w=tot[9], pool_scale=tot[10],
                   conv_w=lax.dynamic_slice(conv_tot, (0, chip * (NQKV // NCHIP)), (5, NQKV // NCHIP)))

    first = ("ffn1_w_in", "ffn1_w_out")
    order = dict(zip(big, ("w1_in", "w1_out", "w_mix", "w_gdn", "w_pool", "w_mo", "w2_in", "w2_out")))
    rest = [k for k in big if k not in first]
    send_sems, recv_sems, sent, lands, token = _scatter_start([g["w1_in"], g["w1_out"]], [g_c_ctx, g_w_ada], name="scatter_ffn1_start")
    mine = {k: _sum_slots(landed[order[k]], name=f"sum_{k}", after=[token]) for k in rest}
    theirs = dict(zip(rest, _core_swap([mine[k] for k in rest], name="swap_grad_sums")))

    out = {}
    as2d = lambda a: a.reshape(-1, a.shape[-1])

    def update(k):
        res = _adamw(as2d(w[k]), as2d(mine[k]), as2d(theirs[k]), as2d(mom[k]), as2d(var[k]), name=f"adamw_{k}")
        out[k] = [r.reshape(w[k].shape) for r in res]

    for k in rest:
        update(k)
    out["w_ada"] = [r.reshape(w_ada.shape) for r in _adamw(w_ada[0], g_w_ada, None, m_w_ada[0], v_w_ada[0], name="adamw_w_ada")]
    sm_names = [n for n, _ in _SMALL]
    sm_sizes = [n for _, n in _SMALL]
    res = _adamw(_pack([w[k] for k in sm_names]), _pack([g_small[k] for k in sm_names]), None,
                 _pack([mom[k] for k in sm_names]), _pack([var[k] for k in sm_names]), name="adamw_small")
    done = [out[k][1] for k in rest] + [out["w_ada"][1], res[1]]
    res = [_unpack(r, sm_sizes) for r in res]
    for i, k in enumerate(sm_names):
        out[k] = [r[i].reshape(w[k].shape) for r in res]
    sent, lands = _scatter_wait(send_sems, recv_sems, sent, lands, done, name="scatter_ffn1_wait")
    for k, part, land in zip(first, sent, lands):
        own = lax.dynamic_slice_in_dim(part, chip, 1, axis=0)
        mine[k] = _sum_slots(lax.dynamic_update_slice_in_dim(land, own, chip, axis=0), name=f"sum_{k}")
    theirs.update(zip(first, _core_swap([mine[k] for k in first], name="swap_ffn1_sums")))
    for k in first:
        update(k)
    return (loss, grad_x, *[out[k][0] for k in names], *[out[k][1] for k in names], *[out[k][2] for k in names],
            *[out[k][3] for k in names])
```

```python
import functools

import jax
import jax.numpy as jnp
from jax import lax
from jax.experimental import pallas as pl
from jax.experimental.pallas import tpu as pltpu

F32 = jnp.float32
BF = jnp.bfloat16

D = 1024
FF = 2816
NH = 8
HD = 128
CH = 64
GW = 64
TM = 256
NQKV = 3 * NH * HD
NPOOL = 512
POOL_WINDOWS = (2, 4, 8, 16)
NMIX = 6688
NMIXP = 6784
EPS = 1e-6
NCHIP = 4
VMEM_LIMIT = 56 * 1024 * 1024

ADAM_LR, ADAM_B1, ADAM_B2, ADAM_EPS, ADAM_WD, ADAM_STEP = 0.001, 0.9, 0.999, 1e-08, 0.01, 10


def _cparams(*sem):
    return pltpu.CompilerParams(dimension_semantics=sem, vmem_limit_bytes=VMEM_LIMIT)


def _const_spec(shape):
    nd = len(shape)
    return pl.BlockSpec(shape, lambda *_: (0,) * nd, pipeline_mode=pl.Buffered(1))


def _dot(a, b, dims):
    return lax.dot_general(a.astype(BF), b.astype(BF), (dims, ((), ())), preferred_element_type=F32)


def _nn(a, b):
    return _dot(a, b, ((1,), (0,)))


def _nt(a, b):
    return _dot(a, b, ((1,), (1,)))


def _tn(a, b):
    return _dot(a, b, ((0,), (0,)))


def _silu(x):
    return x * jax.nn.sigmoid(x)


def _dsilu(x):
    s = jax.nn.sigmoid(x)
    return s * (1.0 + x * (1.0 - s))


def _norm_mod(x, nw, shift, scale):
    r = lax.rsqrt(jnp.mean(x * x, axis=-1, keepdims=True) + EPS)
    xh = x * r
    n = xh * nw
    return n * (1.0 + scale) + shift, n, xh, r


def _norm_mod_bwd(dh, n, xh, r, nw, scale):
    dn = dh * (1.0 + scale)
    dxh = dn * nw
    dx = r * (dxh - xh * jnp.mean(dxh * xh, axis=-1, keepdims=True))
    rs = lambda t: jnp.sum(t, axis=0, keepdims=True)
    return dx, rs(dh), rs(dh * n), rs(dn * xh)


def _stream_specs(ctx):
    if ctx is None:
        return [pl.BlockSpec((TM, D), lambda i: (i, 0))]
    return [pl.BlockSpec((TM, D), lambda i: (0, 0)), pl.BlockSpec((TM, D), lambda i: (jnp.maximum(i - 1, 0), 0))]


def _stream_tile(refs, lead):
    if not lead:
        return refs[0][...], refs[1:]
    return jnp.where(pl.program_id(0) == 0, refs[0][...], refs[1][...]), refs[2:]


def _ffn_fwd(x, modv, nw, w_in4, w_out, *, mrow, name, ctx=None, exchange=None):
    lead = ctx is not None
    n_tok = x.shape[0] + (TM if lead else 0)
    nt = n_tok // TM
    nset = modv.shape[0]
    ws = w_in4.shape[2]

    def body(*refs):
        xv, (mod_ref, nw_ref, win_ref, wout_ref, x1_ref, h_ref, gu_ref, f_ref) = _stream_tile(refs, lead)
        shift, scale, gate = mod_ref[0, mrow:mrow + 1, :], mod_ref[0, mrow + 1:mrow + 2, :], mod_ref[0, mrow + 2:mrow + 3, :]
        h, _, _, _ = _norm_mod(xv, nw_ref[...], shift, scale)
        hb = h.astype(BF)
        h_ref[...] = hb
        gus = [_nn(hb, win_ref[s]) for s in range(NCHIP)]
        for s in range(NCHIP):
            gu_ref[:, s * ws:(s + 1) * ws] = gus[s].astype(BF)
        g = jnp.concatenate(gus[:2], axis=1)
        u = jnp.concatenate(gus[2:], axis=1)
        f = _nn(_silu(g) * u, wout_ref[...])
        f_ref[...] = f.astype(BF)
        x1_ref[...] = xv + 0.5 * gate * f

    tile = lambda w: pl.BlockSpec((TM, w), lambda i: (i, 0))
    return _pallas(
        body, (*([ctx] if lead else []), x, modv, nw, w_in4, w_out), name=name, grid=(nt,), sem=("parallel",), exchange=exchange,
        in_specs=_stream_specs(ctx) + [pl.BlockSpec((1, 16, D), lambda i: (jnp.minimum(i, nset - 1), 0, 0)), _const_spec((1, D)),
                                       _const_spec(w_in4.shape), _const_spec(w_out.shape)],
        out_specs=[tile(D), tile(D), tile(2 * FF), tile(D)],
        out_shape=[jax.ShapeDtypeStruct((n_tok, D), F32), jax.ShapeDtypeStruct((n_tok, D), BF),
                   jax.ShapeDtypeStruct((n_tok, 2 * FF), BF), jax.ShapeDtypeStruct((n_tok, D), BF)])


def _ffn_bwd(dxo, x, gu, fo, modv, nw, w_in4, w_out, *, mrow, name, ctx=None, exchange=None):
    lead = ctx is not None
    dx_skip = int(lead)
    n_tok = x.shape[0] + dx_skip * TM
    nt = n_tok // TM
    nset = modv.shape[0]
    ws = w_in4.shape[2]

    def body(*refs):
        xv, (dxo_ref, gu_ref, f_ref, mod_ref, nw_ref, win_ref, wout_ref, dx_ref, a_ref, df_ref, dgu_ref, acc_ref) = _stream_tile(refs, lead)
        i = pl.program_id(0)
        dxo_v = dxo_ref[...]
        shift, scale, gate = mod_ref[0, mrow:mrow + 1, :], mod_ref[0, mrow + 1:mrow + 2, :], mod_ref[0, mrow + 2:mrow + 3, :]
        _, n, xh, r = _norm_mod(xv, nw_ref[...], shift, scale)
        df = 0.5 * gate * dxo_v
        dfb = df.astype(BF)
        df_ref[...] = dfb
        dgate = jnp.sum(0.5 * dxo_v * f_ref[...].astype(F32), axis=0, keepdims=True)
        da = _nt(dfb, wout_ref[...])
        g = gu_ref[:, :FF].astype(F32)
        u = gu_ref[:, FF:].astype(F32)
        sig = jax.nn.sigmoid(g)
        sg = g * sig
        a_ref[...] = (sg * u).astype(BF)
        dgu_ref[:, :FF] = (da * u * (sig * (1.0 + g - sg))).astype(BF)
        dgu_ref[:, FF:] = (da * sg).astype(BF)
        dh = _nt(dgu_ref[:, 0:ws], win_ref[0])
        for s in range(1, NCHIP):
            dh = dh + _nt(dgu_ref[:, s * ws:(s + 1) * ws], win_ref[s])
        dx, dshift, dscale, dnw = _norm_mod_bwd(dh, n, xh, r, nw_ref[...], scale)
        dx_ref[...] = dxo_v + dx

        @pl.when((i == 0) | (i == nset - 1))
        def _():
            acc_ref[...] = jnp.zeros_like(acc_ref)

        acc_ref[0, 0:1, :] += dshift
        acc_ref[0, 1:2, :] += dscale
        acc_ref[0, 2:3, :] += dgate
        acc_ref[0, 3:4, :] += dnw

    tile = lambda w: pl.BlockSpec((TM, w), lambda i: (i, 0))
    return _pallas(
        body, (*([ctx] if lead else []), x, dxo, gu, fo, modv, nw, w_in4, w_out), name=name, grid=(nt,), sem=("arbitrary",),
        exchange=exchange,
        in_specs=_stream_specs(ctx) + [tile(D), tile(2 * FF), tile(D),
                                       pl.BlockSpec((1, 16, D), lambda i: (jnp.minimum(i, nset - 1), 0, 0)), _const_spec((1, D)),
                                       _const_spec(w_in4.shape), _const_spec(w_out.shape)],
        out_specs=[pl.BlockSpec((TM, D), lambda i: (jnp.maximum(i - dx_skip, 0), 0)), tile(FF), tile(D), tile(2 * FF),
                   pl.BlockSpec((1, 8, D), lambda i: (jnp.minimum(i, nset - 1), 0, 0))],
        out_shape=[jax.ShapeDtypeStruct((n_tok - dx_skip * TM, D), F32), jax.ShapeDtypeStruct((n_tok, FF), BF),
                   jax.ShapeDtypeStruct((n_tok, D), BF), jax.ShapeDtypeStruct((n_tok, 2 * FF), BF),
                   jax.ShapeDtypeStruct((nset, 8, D), F32)])


def _k_tile(n, target=3072):
    return max(t for t in range(TM, min(n, target) + 1, TM) if n % t == 0)


def _matmul_tn(a, b, *, tmm, tn, tk, nsplit=1, name):
    n_tok, m = a.shape
    kk = b.shape[1]
    nk = n_tok // tk

    def body(a_ref, b_ref, o_ref, acc):
        k = pl.program_id(2)

        @pl.when(k == 0)
        def _():
            acc[...] = jnp.zeros_like(acc)

        acc[...] += _tn(a_ref[...], b_ref[...])

        @pl.when(k == nk - 1)
        def _():
            o_ref[...] = acc[...].astype(BF).reshape(o_ref.shape)

    if nsplit == 1:
        out_shape = jax.ShapeDtypeStruct((m, kk), BF)
        out_spec = pl.BlockSpec((tmm, tn), lambda i, j, k: (i, j))
    else:
        assert tn == kk // nsplit
        out_shape = jax.ShapeDtypeStruct((nsplit, m, tn), BF)
        out_spec = pl.BlockSpec((1, tmm, tn), lambda i, j, k: (j, i, 0))
    return pl.pallas_call(
        body, name=name, grid=(m // tmm, kk // tn, nk),
        in_specs=[pl.BlockSpec((tk, tmm), lambda i, j, k: (k, i)), pl.BlockSpec((tk, tn), lambda i, j, k: (k, j))],
        out_specs=out_spec, out_shape=out_shape,
        scratch_shapes=[pltpu.VMEM((tmm, tn), F32)],
        compiler_params=_cparams("parallel", "parallel", "arbitrary"),
    )(a, b)


_MIX_PARTS = (("qkv", 0, NQKV), ("gate", NQKV, 1024), ("pool", NQKV + 1024, NPOOL), ("br", NQKV + 1024 + NPOOL, 2048),
              ("ab", NMIXP - 128, 128))


def _mix_in_fwd(x1, modv, nw, w_mix, *, name, exchange=None):
    n_tok = x1.shape[0]

    def body(x_ref, mod_ref, nw_ref, w_ref, u_ref, *p_refs):
        u, _, _, _ = _norm_mod(x_ref[...], nw_ref[...], mod_ref[0, 3:4, :], mod_ref[0, 4:5, :])
        ub = u.astype(BF)
        u_ref[...] = ub
        for (_, c0, w), p_ref in zip(_MIX_PARTS, p_refs):
            p_ref[...] = _nn(ub, w_ref[:, c0:c0 + w])

    tile = lambda w: pl.BlockSpec((TM, w), lambda i: (i, 0))
    return _pallas(
        body, (x1, modv, nw, w_mix), name=name, grid=(n_tok // TM,), sem=("parallel",), exchange=exchange,
        in_specs=[tile(D), pl.BlockSpec((1, 16, D), lambda i: (jnp.minimum(i, 1), 0, 0)), _const_spec((1, D)),
                  _const_spec(w_mix.shape)],
        out_specs=[tile(D)] + [tile(w) for _, _, w in _MIX_PARTS],
        out_shape=[jax.ShapeDtypeStruct((n_tok, D), BF)] + [jax.ShapeDtypeStruct((n_tok, w), F32) for _, _, w in _MIX_PARTS])


def _mix_in_bwd(dxo, x1, dqkv, dgate, dpool, dbr, dab_f, dab_b, modv, nw, w_mix, *, name):
    n_tok = x1.shape[0]

    def body(dxo_ref, x_ref, dqkv_ref, dgate_ref, dpool_ref, dbr_ref, dabf_ref, dabb_ref, mod_ref, nw_ref, w_ref,
             dx_ref, dp_ref, acc_ref):
        i = pl.program_id(0)
        lat = i >= 1
        scale = mod_ref[0, 4:5, :]
        _, n, xh, r = _norm_mod(x_ref[...], nw_ref[...], mod_ref[0, 3:4, :], scale)
        dp_ref[:, 0:NQKV] = dqkv_ref[...].astype(BF)
        dp_ref[:, NQKV:NQKV + 1024] = jnp.where(lat, dgate_ref[...], 0.0).astype(BF)
        dp_ref[:, NQKV + 1024:NQKV + 1536] = jnp.where(lat, dpool_ref[...], 0.0).astype(BF)
        dp_ref[:, NQKV + 1536:NMIXP - 128] = jnp.where(lat, dbr_ref[...], 0.0).astype(BF)
        dp_ref[:, NMIXP - 128:] = (dabf_ref[...] + dabb_ref[...]).astype(BF)
        du = _nt(dp_ref[...], w_ref[...])
        dx, dshift, dscale, dnw = _norm_mod_bwd(du, n, xh, r, nw_ref[...], scale)
        dx_ref[...] = jnp.where(lat, dxo_ref[...], 0.0) + dx

        @pl.when(i <= 1)
        def _():
            acc_ref[...] = jnp.zeros_like(acc_ref)

        acc_ref[0, 0:1, :] += dshift
        acc_ref[0, 1:2, :] += dscale
        acc_ref[0, 3:4, :] += dnw

    tile = lambda w: pl.BlockSpec((TM, w), lambda i: (i, 0))
    ltile = lambda w: pl.BlockSpec((TM, w), lambda i: (jnp.maximum(i - 1, 0), 0))
    return pl.pallas_call(
        body, name=name, grid=(n_tok // TM,),
        in_specs=[ltile(D), tile(D), tile(NQKV), ltile(1024), ltile(NPOOL), ltile(2048), tile(128), tile(128),
                  pl.BlockSpec((1, 16, D), lambda i: (jnp.minimum(i, 1), 0, 0)), _const_spec((1, D)), _const_spec(w_mix.shape)],
        out_specs=[tile(D), tile(NMIXP), pl.BlockSpec((1, 8, D), lambda i: (jnp.minimum(i, 1), 0, 0))],
        out_shape=[jax.ShapeDtypeStruct((n_tok, D), F32), jax.ShapeDtypeStruct((n_tok, NMIXP), BF),
                   jax.ShapeDtypeStruct((2, 8, D), F32)],
        compiler_params=_cparams("arbitrary"),
    )(dxo, x1, dqkv, dgate, dpool, dbr, dab_f, dab_b, modv, nw, w_mix)


def _qkv_act(pre, part):
    s = _silu(pre)
    if part == 2:
        return s
    nrm = s * lax.rsqrt(jnp.sum(s * s, axis=-1, keepdims=True) + EPS)
    return nrm * HD ** -0.5 if part == 0 else nrm


def _halo_specs(nt):
    r = TM // 8
    main = pl.BlockSpec((TM, NQKV), lambda i: (i, 0))
    prev = pl.BlockSpec((8, NQKV), lambda i: (jnp.maximum(i * r - 1, 0), 0))
    nxt = pl.BlockSpec((8, NQKV), lambda i: (jnp.minimum((i + 1) * r, nt * r - 1), 0))
    return main, prev, nxt


def _prep_fwd(p_qkv, conv_w8, *, name):
    n_tok = p_qkv.shape[0]
    nt = n_tok // TM

    def body(x_ref, xp_ref, xn_ref, w_ref, o_ref, pre_ref, win):
        i = pl.program_id(0)
        has_prev = (i != 0) & (i != 1)
        has_next = (i != 0) & (i != nt - 1)
        win[0:8, :] = jnp.where(has_prev, xp_ref[...], 0.0)
        win[8:8 + TM, :] = x_ref[...]
        win[8 + TM:, :] = jnp.where(has_next, xn_ref[...], 0.0)
        for hb in range(3 * NH):
            hs = slice(hb * HD, (hb + 1) * HD)
            pre = win[6:6 + TM, hs] * w_ref[0:1, hs]
            for k in range(1, 5):
                pre = pre + win[6 + k:6 + k + TM, hs] * w_ref[k:k + 1, hs]
            pre_ref[:, hs] = pre
            o_ref[:, hs] = _qkv_act(pre, hb // NH)

    main, prev, nxt = _halo_specs(nt)
    return pl.pallas_call(
        body, name=name, grid=(nt,),
        in_specs=[main, prev, nxt, pl.BlockSpec((8, NQKV), lambda i: (0, 0))],
        out_specs=[main, main], out_shape=[jax.ShapeDtypeStruct((n_tok, NQKV), F32)] * 2,
        scratch_shapes=[pltpu.VMEM((TM + 16, NQKV), F32)],
        compiler_params=_cparams("parallel"),
    )(p_qkv, p_qkv, p_qkv, conv_w8)


def _prep_bwd(p_qkv, pre, dqkv_f, dqkv_b, conv_w8, *, name):
    n_tok = p_qkv.shape[0]
    nt = n_tok // TM

    def body(x_ref, p_ref, pp_ref, pn_ref, g_ref, gp_ref, gn_ref, g2_ref, g2p_ref, g2n_ref, w_ref, dx_ref, dw_ref, pwin, gwin, dwin):
        i = pl.program_id(0)
        has_prev = (i != 0) & (i != 1)
        has_next = (i != 0) & (i != nt - 1)
        pwin[0:8, :] = jnp.where(has_prev, pp_ref[...], 0.0)
        pwin[8:8 + TM, :] = p_ref[...]
        pwin[8 + TM:, :] = jnp.where(has_next, pn_ref[...], 0.0)
        gwin[0:8, :] = jnp.where(has_prev, gp_ref[...] + g2p_ref[...], 0.0)
        gwin[8:8 + TM, :] = g_ref[...] + g2_ref[...]
        gwin[8 + TM:, :] = jnp.where(has_next, gn_ref[...] + g2n_ref[...], 0.0)

        @pl.when(i == 0)
        def _():
            dw_ref[...] = jnp.zeros_like(dw_ref)

        for hb in range(3 * NH):
            hs = slice(hb * HD, (hb + 1) * HD)
            _, vjp = jax.vjp(functools.partial(_qkv_act, part=hb // NH), pwin[:, hs])
            dwin[:, hs] = vjp(gwin[:, hs])[0]
            xv = x_ref[:, hs]
            dx = None
            for k in range(5):
                sh = dwin[10 - k:10 - k + TM, hs]
                dx = sh * w_ref[k:k + 1, hs] if dx is None else dx + sh * w_ref[k:k + 1, hs]
                dw_ref[k:k + 1, hs] += jnp.sum(sh * xv, axis=0, keepdims=True)
            dx_ref[:, hs] = dx

    main, prev, nxt = _halo_specs(nt)
    wspec = pl.BlockSpec((8, NQKV), lambda i: (0, 0))
    return pl.pallas_call(
        body, name=name, grid=(nt,),
        in_specs=[main, main, prev, nxt, main, prev, nxt, main, prev, nxt, wspec],
        out_specs=[main, wspec],
        out_shape=[jax.ShapeDtypeStruct((n_tok, NQKV), F32), jax.ShapeDtypeStruct((8, NQKV), F32)],
        scratch_shapes=[pltpu.VMEM((TM + 16, NQKV), F32)] * 3,
        compiler_params=_cparams("arbitrary"),
    )(p_qkv, pre, pre, pre, dqkv_f, dqkv_f, dqkv_f, dqkv_b, dqkv_b, dqkv_b, conv_w8)


@jax.custom_vjp
def _mm_nn(a, b):
    return _nn(a, b)


@jax.custom_vjp
def _mm_nt(a, b):
    return _nt(a, b)


@jax.custom_vjp
def _mm_tn(a, b):
    return _tn(a, b)


_mm_nn.defvjp(lambda a, b: (_nn(a, b), (a, b)), lambda r, g: (_mm_nt(g, r[1]), _mm_tn(r[0], g)))
_mm_nt.defvjp(lambda a, b: (_nt(a, b), (a, b)), lambda r, g: (_mm_nn(g, r[1]), _mm_tn(g, r[0])))
_mm_tn.defvjp(lambda a, b: (_tn(a, b), (a, b)), lambda r, g: (_mm_nt(r[1], g), _mm_nn(r[0], g)))


def _each(f, *lists):
    return tuple(f(*a) for a in zip(*lists))


def _unit_tri_inv(ls, revs):
    ii = lax.broadcasted_iota(jnp.int32, (CH, CH), 0)
    jj = lax.broadcasted_iota(jnp.int32, (CH, CH), 1)
    eye = (ii == jj).astype(F32)
    xs = None
    s = 1
    while s < CH:
        same = (ii & -(2 * s)) == (jj & -(2 * s))
        off = {False: same & ((ii & s) != 0) & ((jj & s) == 0), True: same & ((jj & s) != 0) & ((ii & s) == 0)}
        cs = _each(lambda l, r: jnp.where(off[r], l, 0.0), ls, revs)
        if xs is None:
            xs = _each(lambda c: eye - c, cs)
        else:
            xc = _each(_nn, xs, cs)
            xcx = _each(_nn, xc, xs)
            xs = _each(lambda x, t: x - t, xs, xcx)
        s *= 2
    return xs


@functools.lru_cache(maxsize=None)
def _tri_solve(revs):
    @jax.custom_vjp
    def solve(ls, rhss):
        return _each(_mm_nn, _unit_tri_inv(ls, revs), rhss)

    def fwd(ls, rhss):
        ainv = _unit_tri_inv(ls, revs)
        xs = _each(_mm_nn, ainv, rhss)
        return xs, (ainv, xs)

    def bwd(res, gs):
        ainv, xs = res
        drhs = _each(_mm_tn, ainv, gs)
        return _each(lambda d, x: -_mm_nt(d, x), drhs, xs), drhs

    solve.defvjp(fwd, bwd)
    return solve


def _chunk_prep(q, k, v, beta, g, *, revs):
    ii = lax.broadcasted_iota(jnp.int32, (CH, CH), 0)
    jj = lax.broadcasted_iota(jnp.int32, (CH, CH), 1)
    eye = ii == jj
    incl_of = {False: ii >= jj, True: ii <= jj}
    strict_of = {False: ii > jj, True: ii < jj}
    g_row = _each(lambda t: jnp.sum(jnp.where(eye, t, 0.0), axis=0, keepdims=True), g)
    cum = _each(lambda t, r: jnp.sum(jnp.where(incl_of[r], t, 0.0), axis=1, keepdims=True), g_row, revs)
    cum_row = _each(lambda t: jnp.sum(jnp.where(eye, t, 0.0), axis=0, keepdims=True), cum)
    total = _each(lambda t: jnp.sum(t, axis=0, keepdims=True), g)
    decay = _each(lambda c, cr, r: jnp.where(incl_of[r], jnp.exp(jnp.where(incl_of[r], c - cr, 0.0)), 0.0), cum, cum_row, revs)
    kb = _each(jnp.multiply, k, beta)
    vb = _each(jnp.multiply, v, beta)
    kk = _each(_mm_nt, kb, k)
    lmat = _each(lambda t, dc, r: jnp.where(strict_of[r], t * dc, 0.0), kk, decay, revs)
    ecum = _each(jnp.exp, cum)
    rhs = _each(lambda a, b, e: jnp.concatenate([a, b * e], axis=1), vb, kb, ecum)
    sol = _tri_solve(revs)(lmat, rhs)
    qk = _each(_mm_nt, q, k)
    aqk = _each(jnp.multiply, qk, decay)
    qd = _each(jnp.multiply, q, ecum)
    kd = _each(lambda a, t, c: a * jnp.exp(t - c), k, total, cum)
    return sol, aqk, qd, kd, _each(jnp.exp, total)


def _chunk_rec(sol, aqk, qd, kd, bl, s):
    ws = _each(lambda so, st: _mm_nn(so[:, HD:], st), sol, s)
    v_new = _each(lambda so, t: so[:, :HD] - t, sol, ws)
    qs = _each(_mm_nn, qd, s)
    av = _each(_mm_nn, aqk, v_new)
    o = _each(jnp.add, qs, av)
    kv = _each(_mm_tn, kd, v_new)
    s_new = _each(lambda st, b, u: st * b + u, s, bl, kv)
    return o, s_new


def _lane_col(x, c):
    lane = lax.broadcasted_iota(jnp.int32, x.shape, 1)
    return jnp.sum(jnp.where(lane == c, x, 0.0), axis=1, keepdims=True)


def _gates(ab, cst):
    z = ab + cst[1:2, :]
    softplus = jnp.maximum(z, 0.0) + jnp.log(1.0 + jnp.exp(-jnp.abs(z)))
    return jax.nn.sigmoid(ab), -cst[0:1, :] * softplus, -cst[0:1, :] * jax.nn.sigmoid(z)


def _beta_g(gates, d, h):
    sig, g, dg = gates
    return _lane_col(sig, NH * d + h), _lane_col(g, 2 * NH + NH * d + h), _lane_col(dg, 2 * NH + NH * d + h)


STEPS = 2
TS = STEPS * CH
_CHAINS = tuple((t, d, h) for t in range(STEPS) for d in (0, 1) for h in range(NH))
_REVS = tuple(bool(d) for _, d, _ in _CHAINS)
_PER_STEP = 2 * NH


def _chain_inputs(refs, r0s, ab_refs, cst):
    hs = lambda h: slice(h * HD, (h + 1) * HD)
    gates = [[_gates(ab_refs[d][pl.ds(r0s[t][d], CH), :], cst) for d in (0, 1)] for t in range(STEPS)]
    q = _each(lambda c: refs[c[1]][0][pl.ds(r0s[c[0]][c[1]], CH), hs(c[2])], _CHAINS)
    k = _each(lambda c: refs[c[1]][1][pl.ds(r0s[c[0]][c[1]], CH), hs(c[2])], _CHAINS)
    v = _each(lambda c: refs[c[1]][2][pl.ds(r0s[c[0]][c[1]], CH), hs(c[2])], _CHAINS)
    bg = _each(lambda c: _beta_g(gates[c[0]][c[1]], c[1], c[2]), _CHAINS)
    return q, k, v, bg


def _of_step(parts, t):
    return tuple(p[t * _PER_STEP:(t + 1) * _PER_STEP] for p in parts)


def _scan_fwd(qkv, ab, cst, s0, *, row_blk0, nb, name, exchange=None):
    cb = TS // CH
    w = NH * HD

    def body(qf, kf, vf, abf, qb, kb, vb, abb, cst_ref, s0_ref, of_ref, ob_ref, sallf_ref, sallb_ref, sfin_ref, s_scr):
        i = pl.program_id(0)

        @pl.when(i == 0)
        def _():
            s_scr[...] = s0_ref[...]

        o_refs, sall_refs = (of_ref, ob_ref), (sallf_ref, sallb_ref)

        def chunks(ci, carry):
            cs = [(ci * STEPS + t, cb - 1 - ci * STEPS - t) for t in range(STEPS)]
            r0s = [tuple(pl.multiple_of(c * CH, CH) for c in ct) for ct in cs]
            q, k, v, bg = _chain_inputs(((qf, kf, vf), (qb, kb, vb)), r0s, (abf, abb), cst_ref[...])
            parts = _chunk_prep(q, k, v, _each(lambda t: t[0], bg), _each(lambda t: t[1], bg), revs=_REVS)
            s = _each(lambda c: s_scr[c[1], c[2]], _CHAINS[:_PER_STEP])
            for t in range(STEPS):
                for (_, d, h), sv in zip(_CHAINS, s):
                    sall_refs[d][cs[t][d], h] = sv
                o, s = _chunk_rec(*_of_step(parts, t), s)
                for (_, d, h), ov in zip(_CHAINS, o):
                    o_refs[d][pl.ds(r0s[t][d], CH), h * HD:(h + 1) * HD] = ov
            for (_, d, h), sv in zip(_CHAINS, s):
                s_scr[d, h] = sv
            return carry

        lax.fori_loop(0, cb // STEPS, chunks, 0)

        @pl.when(i == nb - 1)
        def _():
            sfin_ref[...] = s_scr[...]

    pos = (lambda i: i, lambda i: nb - 1 - i)
    col = lambda d, c: pl.BlockSpec((TS, w), lambda i: (row_blk0 + pos[d](i), c))
    abs_ = lambda d: pl.BlockSpec((TS, 128), lambda i: (row_blk0 + pos[d](i), 0))
    full4 = pl.BlockSpec((2, NH, HD, HD), lambda i: (0, 0, 0, 0))
    o_spec = lambda d: pl.BlockSpec((TS, w), lambda i: (pos[d](i), 0))
    sall_spec = lambda d: pl.BlockSpec((cb, NH, HD, HD), lambda i: (pos[d](i), 0, 0, 0))
    return _pallas(
        body, (qkv, qkv, qkv, ab, qkv, qkv, qkv, ab, cst, s0), name=name, grid=(nb,), sem=("arbitrary",), exchange=exchange,
        in_specs=[col(0, 0), col(0, 1), col(0, 2), abs_(0), col(1, 0), col(1, 1), col(1, 2), abs_(1),
                  pl.BlockSpec((8, 128), lambda i: (0, 0)), full4],
        out_specs=[o_spec(0), o_spec(1), sall_spec(0), sall_spec(1), full4],
        out_shape=[jax.ShapeDtypeStruct((nb * TS, w), F32)] * 2 + [jax.ShapeDtypeStruct((nb * cb, NH, HD, HD), F32)] * 2
        + [jax.ShapeDtypeStruct((2, NH, HD, HD), F32)],
        scratch_shapes=[pltpu.VMEM((2, NH, HD, HD), F32)])


def _scan_bwd(qkv, ab, cst, sall_f, sall_b, do, dsfin, dqkv_f, dqkv_b, dab_f, dab_b, dcst, *, row_blk0, nb, has_do, name,
              exchange=None):
    cb = TS // CH
    w = NH * HD

    def body(qf, kf, vf, abf, qb, kb, vb, abb, cst_ref, sallf_ref, sallb_ref, dof_ref, dob_ref, dsfin_ref, _f, _b, _af, _ab, dcst_in,
             dqkvf_ref, dqkvb_ref, dabf_ref, dabb_ref, dcst_ref, ds0_ref, ds_scr):
        i = pl.program_id(0)

        @pl.when(i == 0)
        def _():
            ds_scr[...] = dsfin_ref[...]
            dcst_ref[...] = dcst_in[...]

        lane = lax.broadcasted_iota(jnp.int32, (CH, 128), 1)
        lane1 = lax.broadcasted_iota(jnp.int32, (1, 128), 1)
        sall_refs, do_refs = (sallf_ref, sallb_ref), (dof_ref, dob_ref)
        dqkv_refs, dab_refs = (dqkvf_ref, dqkvb_ref), (dabf_ref, dabb_ref)

        def chunks(ci, carry):
            cs = [(cb - 1 - ci * STEPS - t, ci * STEPS + t) for t in range(STEPS)]
            r0s = [tuple(pl.multiple_of(c * CH, CH) for c in ct) for ct in cs]
            q, k, v, bg = _chain_inputs(((qf, kf, vf), (qb, kb, vb)), r0s, (abf, abb), cst_ref[...])
            beta, g = _each(lambda t: t[0], bg), _each(lambda t: t[1], bg)
            parts, prep_vjp = jax.vjp(functools.partial(_chunk_prep, revs=_REVS), q, k, v, beta, g)
            ds = _each(lambda c: ds_scr[c[1], c[2]], _CHAINS[:_PER_STEP])
            dparts = []
            for t in range(STEPS):
                s = _each(lambda c: sall_refs[c[1]][cs[t][c[1]], c[2]], _CHAINS[:_PER_STEP])
                _, rec_vjp = jax.vjp(_chunk_rec, *_of_step(parts, t), s)
                do_t = _each(lambda c: do_refs[c[1]][pl.ds(r0s[t][c[1]], CH), c[2] * HD:(c[2] + 1) * HD] if has_do
                             else jnp.zeros((CH, HD), F32), _CHAINS[:_PER_STEP])
                *dpt, ds = rec_vjp((do_t, ds))
                dparts.append(dpt)
            for (_, d, h), dsv in zip(_CHAINS, ds):
                ds_scr[d, h] = dsv
            dq, dk, dv, dbeta, dg = prep_vjp(tuple(sum((dparts[t][j] for t in range(STEPS)), ()) for j in range(len(dparts[0]))))
            dab = [[jnp.zeros((CH, 128), F32), jnp.zeros((CH, 128), F32)] for _ in range(STEPS)]
            dal = jnp.zeros((1, 128), F32)
            for n, (t, d, h) in enumerate(_CHAINS):
                for part, val in enumerate((dq[n], dk[n], dv[n])):
                    dqkv_refs[d][pl.ds(r0s[t][d], CH), part * w + h * HD:part * w + (h + 1) * HD] = val
                dbraw = dbeta[n] * beta[n] * (1.0 - beta[n])
                daraw = dg[n] * bg[n][2]
                dab[t][d] = dab[t][d] + jnp.where(lane == NH * d + h, dbraw, 0.0) + jnp.where(lane == 2 * NH + NH * d + h, daraw, 0.0)
                dal = dal + jnp.where(lane1 == 2 * NH + NH * d + h, jnp.sum(dg[n] * g[n], axis=0, keepdims=True), 0.0)
            dsum = jnp.zeros((CH, 128), F32)
            for t in range(STEPS):
                for d in (0, 1):
                    dab_refs[d][pl.ds(r0s[t][d], CH), :] = dab[t][d]
                    dsum = dsum + dab[t][d]
            dcst_ref[0:1, :] += dal
            dcst_ref[1:2, :] += jnp.sum(jnp.where(lane >= 2 * NH, dsum, 0.0), axis=0, keepdims=True)
            return carry

        lax.fori_loop(0, cb // STEPS, chunks, 0)

        @pl.when(i == nb - 1)
        def _():
            ds0_ref[...] = ds_scr[...]

    pos = (lambda i: nb - 1 - i, lambda i: i)
    col = lambda d, c: pl.BlockSpec((TS, w), lambda i: (row_blk0 + pos[d](i), c))
    abs_ = lambda d: pl.BlockSpec((TS, 128), lambda i: (row_blk0 + pos[d](i), 0))
    full4 = pl.BlockSpec((2, NH, HD, HD), lambda i: (0, 0, 0, 0))
    small = pl.BlockSpec((8, 128), lambda i: (0, 0))
    hbm = pl.BlockSpec(memory_space=pl.ANY)
    sall_spec = lambda d: pl.BlockSpec((cb, NH, HD, HD), lambda i: (pos[d](i), 0, 0, 0))
    do_spec = (lambda d: pl.BlockSpec((TS, w), lambda i: (pos[d](i), 0))) if has_do else (lambda d: small)
    acc_specs = [pl.BlockSpec((TS, 3 * w), lambda i: (row_blk0 + pos[0](i), 0)),
                 pl.BlockSpec((TS, 3 * w), lambda i: (row_blk0 + pos[1](i), 0)), abs_(0), abs_(1), small]
    return _pallas(
        body, (qkv, qkv, qkv, ab, qkv, qkv, qkv, ab, cst, sall_f, sall_b, do, do, dsfin, dqkv_f, dqkv_b, dab_f, dab_b, dcst),
        name=name, grid=(nb,), sem=("arbitrary",), exchange=exchange,
        in_specs=[col(0, 0), col(0, 1), col(0, 2), abs_(0), col(1, 0), col(1, 1), col(1, 2), abs_(1), small,
                  sall_spec(0), sall_spec(1), do_spec(0), do_spec(1), full4, hbm, hbm, hbm, hbm, small],
        out_specs=acc_specs + [full4],
        out_shape=[jax.ShapeDtypeStruct(dqkv_f.shape, F32), jax.ShapeDtypeStruct(dqkv_b.shape, F32),
                   jax.ShapeDtypeStruct(dab_f.shape, F32), jax.ShapeDtypeStruct(dab_b.shape, F32),
                   jax.ShapeDtypeStruct((8, 128), F32), jax.ShapeDtypeStruct((2, NH, HD, HD), F32)],
        aliases={14: 0, 15: 1, 16: 2, 17: 3, 18: 4},
        scratch_shapes=[pltpu.VMEM((2, NH, HD, HD), F32)])


def _pool(xin, *, row0, transpose, name):
    n_tok = xin.shape[0] - row0
    rows = n_tok // GW
    pad = 8 * GW
    tt = 512
    gsh = GW.bit_length() - 1

    def body(x_ref, o_ref, ybuf):
        ii = lax.broadcasted_iota(jnp.int32, (128, 128), 0)
        jj = lax.broadcasted_iota(jnp.int32, (128, 128), 1)
        same_row = (ii >> gsh) == (jj >> gsh)
        ci, cj = ii & (GW - 1), jj & (GW - 1)
        tok = lax.broadcasted_iota(jnp.int32, (tt, 1), 0)
        zpad = jnp.zeros((pad, 128), F32)
        for gi, wdw in enumerate(POOL_WINDOWS):
            lo, hi = wdw // 2, wdw - wdw // 2
            if transpose:
                band = same_row & (ci - cj >= -lo) & (ci - cj < hi)
                offs = range(-hi + 1, lo + 1)
            else:
                band = same_row & (cj - ci >= -lo) & (cj - ci < hi)
                offs = range(-lo, hi)
            bandm = band.astype(BF)
            cs = slice(gi * 128, (gi + 1) * 128)
            ybuf[0:pad, :] = zpad
            ybuf[pad + n_tok:, :] = zpad

            def inv_area(t0):
                t = t0 + tok
                r, c = t >> gsh, t & (GW - 1)
                nr = jnp.minimum(r + hi, rows) - jnp.maximum(r - lo, 0)
                nc = jnp.minimum(c + hi, GW) - jnp.maximum(c - lo, 0)
                return 1.0 / (nr * nc).astype(F32)

            def col_pass(b, carry):
                t0 = pl.multiple_of(b * tt, tt)
                xv = x_ref[pl.ds(row0 + t0, tt), cs]
                if transpose:
                    xv = xv * inv_area(t0)
                hi_part = xv.astype(BF)
                lo_part = (xv - hi_part.astype(F32)).astype(BF)
                for s in range(tt // 128):
                    sl = slice(s * 128, (s + 1) * 128)
                    y = (jnp.dot(bandm, hi_part[sl], preferred_element_type=F32)
                         + jnp.dot(bandm, lo_part[sl], preferred_element_type=F32))
                    ybuf[pl.ds(pad + t0 + s * 128, 128), :] = y
                return carry

            lax.fori_loop(0, n_tok // tt, col_pass, 0)

            def row_pass(b, carry):
                t0 = pl.multiple_of(b * tt, tt)
                acc = ybuf[pl.ds(pad + t0 + offs[0] * GW, tt), :]
                for dr in offs[1:]:
                    acc = acc + ybuf[pl.ds(pad + t0 + dr * GW, tt), :]
                xv = x_ref[pl.ds(row0 + t0, tt), cs]
                if not transpose:
                    acc = acc * inv_area(t0)
                o_ref[pl.ds(t0, tt), cs] = acc - xv
                return carry

            lax.fori_loop(0, n_tok // tt, row_pass, 0)

    return pl.pallas_call(
        body, name=name, out_shape=jax.ShapeDtypeStruct((n_tok, NPOOL), F32),
        in_specs=[pl.BlockSpec(memory_space=pltpu.VMEM)], out_specs=pl.BlockSpec(memory_space=pltpu.VMEM),
        scratch_shapes=[pltpu.VMEM((n_tok + 2 * pad, 128), F32)],
        compiler_params=pltpu.CompilerParams(vmem_limit_bytes=VMEM_LIMIT),
    )(xin)


def _merge_parts(of, ob, pgate, pd, br, gnw, pw_ref, pscale, wg_ref, wp_ref):
    o = of + ob
    ons, ohs, rs = [], [], []
    for h in range(NH):
        oh = o[:, h * HD:(h + 1) * HD]
        r = lax.rsqrt(jnp.mean(oh * oh, axis=-1, keepdims=True) + EPS)
        ohs.append(oh * r)
        rs.append(r)
        ons.append(oh * r * gnw)
    on = jnp.concatenate(ons, axis=1)
    sig_gate = jax.nn.sigmoid(pgate)
    silu_gate = pgate * sig_gate
    og = on * silu_gate
    y_gdn = _nn(og, wg_ref[...])
    ypre = jnp.concatenate([_nn(pd[:, g * 128:(g + 1) * 128], pw_ref[g]) for g in range(4)], axis=1)
    yp = ypre * pscale
    y_pool = _nn(yp, wp_ref[...])
    g_pool = jax.nn.sigmoid(br[:, :D])
    g_gdn = jax.nn.sigmoid(br[:, D:])
    return dict(on=on, ohs=ohs, rs=rs, og=og, y_gdn=y_gdn, ypre=ypre, yp=yp, y_pool=y_pool, g_pool=g_pool, g_gdn=g_gdn,
                sig_gate=sig_gate, silu_gate=silu_gate)


def _merge_fwd(x1, of, ob, pgate, pd, br, modv, gnw, pool_w, pscale, w_gdn, w_pool, w_mo, *, name):
    n_tok = of.shape[0]

    def body(x_ref, of_ref, ob_ref, pg_ref, pd_ref, br_ref, mod_ref, gnw_ref, pw_ref, ps_ref, wg_ref, wp_ref, wmo_ref,
             x2_ref, og_ref, yp_ref, m_ref, mix_ref):
        t = _merge_parts(of_ref[...], ob_ref[...], pg_ref[...], pd_ref[...], br_ref[...], gnw_ref[...], pw_ref, ps_ref[...],
                         wg_ref, wp_ref)
        m = t["g_pool"] * t["y_pool"] + t["g_gdn"] * t["y_gdn"]
        mix = _nn(m, wmo_ref[...])
        og_ref[...] = t["og"].astype(BF)
        yp_ref[...] = t["yp"].astype(BF)
        m_ref[...] = m.astype(BF)
        mix_ref[...] = mix.astype(BF)
        x2_ref[...] = x_ref[...] + mod_ref[0, 5:6, :] * mix

    tile = lambda w: pl.BlockSpec((TM, w), lambda i: (i, 0))
    ctile = lambda w: pl.BlockSpec((TM, w), lambda i: (i + 1, 0))
    return pl.pallas_call(
        body, name=name, grid=(n_tok // TM,),
        in_specs=[ctile(D), tile(D), tile(D), ctile(D), tile(NPOOL), ctile(2 * D),
                  pl.BlockSpec((1, 16, D), lambda i: (1, 0, 0)), _const_spec((1, HD)), _const_spec((4, 128, 128)),
                  _const_spec((1, NPOOL)), _const_spec((D, D)), _const_spec((NPOOL, D)), _const_spec((D, D))],
        out_specs=[tile(D), tile(D), tile(NPOOL), tile(D), tile(D)],
        out_shape=[jax.ShapeDtypeStruct((n_tok, D), F32), jax.ShapeDtypeStruct((n_tok, D), BF),
                   jax.ShapeDtypeStruct((n_tok, NPOOL), BF), jax.ShapeDtypeStruct((n_tok, D), BF),
                   jax.ShapeDtypeStruct((n_tok, D), BF)],
        compiler_params=_cparams("parallel"),
    )(x1, of, ob, pgate, pd, br, modv, gnw, pool_w, pscale, w_gdn, w_pool, w_mo)


def _merge_bwd(dx2, mix, of, ob, pgate, pd, br, modv, gnw, pool_w, pscale, w_gdn, w_pool, w_mo, *, name):
    n_tok = of.shape[0]

    def body(dx2_ref, mix_ref, of_ref, ob_ref, pg_ref, pd_ref, br_ref, mod_ref, gnw_ref, pw_ref, ps_ref, wg_ref, wp_ref, wmo_ref,
             do_ref, dgate_ref, dpd_ref, dbr_ref, dmix_ref, dyg_ref, dyp_ref, acc_ref, dpw_ref):
        i = pl.program_id(0)
        pgate, pdv, gnw = pg_ref[...], pd_ref[...], gnw_ref[...]
        t = _merge_parts(of_ref[...], ob_ref[...], pgate, pdv, br_ref[...], gnw, pw_ref, ps_ref[...], wg_ref, wp_ref)
        dx2v = dx2_ref[...]
        dmix = mod_ref[0, 5:6, :] * dx2v
        dmixb = dmix.astype(BF)
        dmix_ref[...] = dmixb
        dm = _nt(dmixb, wmo_ref[...])
        gp, gg = t["g_pool"], t["g_gdn"]
        dbr_ref[:, :D] = dm * t["y_pool"] * gp * (1.0 - gp)
        dbr_ref[:, D:] = dm * t["y_gdn"] * gg * (1.0 - gg)
        dyp = (dm * gp).astype(BF)
        dyg = (dm * gg).astype(BF)
        dyp_ref[...] = dyp
        dyg_ref[...] = dyg
        dyp_in = _nt(dyp, wp_ref[...])
        dypre = dyp_in * ps_ref[...]
        for g in range(4):
            gs = slice(g * 128, (g + 1) * 128)
            dpd_ref[:, gs] = _nt(dypre[:, gs], pw_ref[g])
        dog = _nt(dyg, wg_ref[...])
        dgate_ref[...] = dog * t["on"] * (t["sig_gate"] * (1.0 + pgate - t["silu_gate"]))
        don = dog * t["silu_gate"]
        dgnw = jnp.zeros((1, HD), F32)
        for h in range(NH):
            hs = slice(h * HD, (h + 1) * HD)
            donh, oh, r = don[:, hs], t["ohs"][h], t["rs"][h]
            dgnw = dgnw + jnp.sum(donh * oh, axis=0, keepdims=True)
            doh = donh * gnw
            do_ref[:, hs] = r * (doh - oh * jnp.mean(doh * oh, axis=-1, keepdims=True))

        @pl.when(i == 0)
        def _():
            acc_ref[...] = jnp.zeros_like(acc_ref)
            dpw_ref[...] = jnp.zeros_like(dpw_ref)

        acc_ref[0:1, :] += jnp.sum(dx2v * mix_ref[...].astype(F32), axis=0, keepdims=True)
        acc_ref[1:2, 0:HD] += dgnw
        acc_ref[2:3, 0:NPOOL] += jnp.sum(dyp_in * t["ypre"], axis=0, keepdims=True)
        for g in range(4):
            gs = slice(g * 128, (g + 1) * 128)
            dpw_ref[g] += _tn(pdv[:, gs], dypre[:, gs])

    tile = lambda w: pl.BlockSpec((TM, w), lambda i: (i, 0))
    ctile = lambda w: pl.BlockSpec((TM, w), lambda i: (i + 1, 0))
    return pl.pallas_call(
        body, name=name, grid=(n_tok // TM,),
        in_specs=[tile(D), tile(D), tile(D), tile(D), ctile(D), tile(NPOOL), ctile(2 * D),
                  pl.BlockSpec((1, 16, D), lambda i: (1, 0, 0)), _const_spec((1, HD)), _const_spec((4, 128, 128)),
                  _const_spec((1, NPOOL)), _const_spec((D, D)), _const_spec((NPOOL, D)), _const_spec((D, D))],
        out_specs=[tile(D), tile(D), tile(NPOOL), tile(2 * D), tile(D), tile(D), tile(D),
                   pl.BlockSpec((8, D), lambda i: (0, 0)), pl.BlockSpec((4, 128, 128), lambda i: (0, 0, 0))],
        out_shape=[jax.ShapeDtypeStruct((n_tok, D), F32), jax.ShapeDtypeStruct((n_tok, D), F32),
                   jax.ShapeDtypeStruct((n_tok, NPOOL), F32), jax.ShapeDtypeStruct((n_tok, 2 * D), F32),
                   jax.ShapeDtypeStruct((n_tok, D), BF), jax.ShapeDtypeStruct((n_tok, D), BF), jax.ShapeDtypeStruct((n_tok, D), BF),
                   jax.ShapeDtypeStruct((8, D), F32), jax.ShapeDtypeStruct((4, 128, 128), F32)],
        compiler_params=_cparams("arbitrary"),
    )(dx2, mix, of, ob, pgate, pd, br, modv, gnw, pool_w, pscale, w_gdn, w_pool, w_mo)


def _final(x3, target, fnw, *, name):
    n_tok = x3.shape[0]

    def body(x_ref, t_ref, w_ref, dx_ref, acc_ref):
        xv, w = x_ref[...], w_ref[...]
        r = lax.rsqrt(jnp.mean(xv * xv, axis=-1, keepdims=True) + EPS)
        xh = xv * r
        err = xh * w - t_ref[...]
        dy = err * (1.0 / D)
        dxh = dy * w
        dx_ref[...] = r * (dxh - xh * jnp.mean(dxh * xh, axis=-1, keepdims=True))

        @pl.when(pl.program_id(0) == 0)
        def _():
            acc_ref[...] = jnp.zeros_like(acc_ref)

        acc_ref[0:1, :] += jnp.sum(dy * xh, axis=0, keepdims=True)
        acc_ref[1:2, :] += jnp.sum(err * err, axis=0, keepdims=True) * (0.5 / D)

    tile = pl.BlockSpec((TM, D), lambda i: (i, 0))
    return pl.pallas_call(
        body, name=name, grid=(n_tok // TM,),
        in_specs=[tile, tile, _const_spec((1, D))],
        out_specs=[tile, pl.BlockSpec((8, D), lambda i: (0, 0))],
        out_shape=[jax.ShapeDtypeStruct((n_tok, D), F32), jax.ShapeDtypeStruct((8, D), F32)],
        compiler_params=_cparams("arbitrary"),
    )(x3, target, fnw)


def _split(results, n):
    return (*results[:n], list(results[n:]))


def _local_step(ctx, x, target, modv, p, late=None):
    t_lat = x.shape[0]
    n_all = t_lat + TM
    nbc, nbx = TM // TS, t_lat // TS
    mod_lat = modv[1:2]
    gather = (lambda arrs: _ChipExchange(arrs, False)) if late else (lambda arrs: None)
    scatter = (lambda arrs: _ChipExchange(arrs, True)) if late else (lambda arrs: None)

    x1, h1, gu1, f1, *got = _ffn_fwd(x, modv, p["norm1"], p["w1_in"], p["w1_out"], mrow=0, name="ffn1_fwd", ctx=ctx,
                                     exchange=gather(late and late[0]))
    if late:
        p = {**p, "w_mix": _regroup_mix(_from_chip_major_cols(got[0])), "conv": jnp.pad(_from_chip_major_cols(got[1]), ((0, 3), (0, 0)))}
    u, p_qkv, p_gate, p_pool, p_br, p_ab, *got = _mix_in_fwd(x1, modv, p["norm2"], p["w_mix"], name="mix_in_fwd",
                                                              exchange=gather(late and late[2]))
    if late:
        p = {**p, "w_gdn": got[0].reshape(D, D), "w_pool": _from_chip_major_cols(got[1]), "w_mo": got[2].reshape(D, D)}
    qkv, pre_qkv = _prep_fwd(p_qkv, p["conv"], name="prep_fwd")
    s_zero = jnp.zeros((2, NH, HD, HD), F32)
    _, _, sall_cf, sall_cb, s_ctx = _scan_fwd(qkv, p_ab, p["cst"], s_zero, row_blk0=0, nb=nbc, name="scan_ctx")
    o_f, o_b, sall_f, sall_b, _, *got = _scan_fwd(qkv, p_ab, p["cst"], s_ctx, row_blk0=nbc, nb=nbx, name="scan_lat",
                                                  exchange=gather(late and late[1]))
    if late:
        p = {**p, "w2_in": got[0], "w2_out": got[1].reshape(FF, D)}
    pd = _pool(p_pool, row0=TM, transpose=False, name="pool_fwd")
    merge_w = (modv, p["gnw"], p["pool_w"], p["pscale"], p["w_gdn"], p["w_pool"], p["w_mo"])
    x2, og, yp, m, mix = _merge_fwd(x1, o_f, o_b, p_gate, pd, p_br, *merge_w, name="merge_fwd")
    x3, h3, gu3, f3 = _ffn_fwd(x2, mod_lat, p["norm3"], p["w2_in"], p["w2_out"], mrow=6, name="ffn2_fwd")
    dx3, acc_fin = _final(x3, target, p["fnorm"], name="final")

    dx2, a3, df3, dgu3, acc3 = _ffn_bwd(dx3, x2, gu3, f3, mod_lat, p["norm3"], p["w2_in"], p["w2_out"], mrow=6,
                                        name="ffn2_bwd")
    g = {}
    tkl = _k_tile(t_lat)
    g["w2_out"] = _matmul_tn(a3, df3, tmm=FF // 2, tn=D, tk=tkl, name="ffn2_wout_grad").reshape(NCHIP, FF // NCHIP, D)
    g["w2_in"] = _matmul_tn(h3, dgu3, tmm=D, tn=2 * FF // NCHIP, tk=tkl, nsplit=NCHIP, name="ffn2_win_grad")
    do, dgate, dpd, dbr, dmix, dyg, dyp, acc_m, dpw = _merge_bwd(dx2, mix, o_f, o_b, p_gate, pd, p_br, *merge_w, name="merge_bwd")
    g["w_mo"] = _matmul_tn(m, dmix, tmm=D, tn=D, tk=tkl, name="wmo_grad").reshape(NCHIP, D // NCHIP, D)
    g["w_gdn"] = _matmul_tn(og, dyg, tmm=D, tn=D, tk=tkl, name="wgdn_grad").reshape(NCHIP, D // NCHIP, D)
    g["w_pool"] = _matmul_tn(yp, dyp, tmm=NPOOL, tn=D // NCHIP, tk=tkl, nsplit=NCHIP, name="wpool_grad")
    dpool_in = _pool(dpd, row0=0, transpose=True, name="pool_bwd")
    acc = (lax.empty((n_all, NQKV), F32), lax.empty((n_all, NQKV), F32), lax.empty((n_all, 128), F32),
           lax.empty((n_all, 128), F32), jnp.zeros((8, 128), F32))
    behind_scan = ("w2_in", "w2_out", "w_gdn", "w_pool", "w_mo")
    *acc, ds_ctx, landed = _split(_scan_bwd(qkv, p_ab, p["cst"], sall_f, sall_b, do, s_zero, *acc, row_blk0=nbc, nb=nbx, has_do=True,
                                            name="scan_lat_bwd", exchange=scatter([g[k] for k in behind_scan])), 6)
    landed = dict(zip(behind_scan, landed))
    dqkv_f, dqkv_b, dab_f, dab_b, dcst, _ = _scan_bwd(qkv, p_ab, p["cst"], sall_cf, sall_cb, jnp.zeros((8, 128), F32), ds_ctx, *acc,
                                                      row_blk0=0, nb=nbc, has_do=False, name="scan_ctx_bwd")
    dpqkv, dconv = _prep_bwd(p_qkv, pre_qkv, dqkv_f, dqkv_b, p["conv"], name="prep_bwd")
    dx1, dp, acc_mix = _mix_in_bwd(dx2, x1, dpqkv, dgate, dpool_in, dbr, dab_f, dab_b, modv, p["norm2"], p["w_mix"],
                                   name="mix_in_bwd")
    tka = _k_tile(n_all)
    g["w_mix"] = _chip_major_cols(_ungroup_mix(_matmul_tn(u, dp, tmm=256, tn=NMIXP, tk=_k_tile(n_all, 1024), name="wmix_grad")))
    dx_lat, a1, df1, dgu1, acc1, got = _split(_ffn_bwd(dx1, x, gu1, f1, modv, p["norm1"], p["w1_in"], p["w1_out"], mrow=0, ctx=ctx,
                                                   name="ffn1_bwd", exchange=scatter([g["w_mix"]])), 5)
    landed.update(zip(("w_mix",), got))
    g["w1_out"] = _matmul_tn(a1, df1, tmm=FF // 2, tn=D, tk=tka, name="ffn1_wout_grad").reshape(NCHIP, FF // NCHIP, D)
    g["w1_in"] = _matmul_tn(h1, dgu1, tmm=D, tn=2 * FF // NCHIP, tk=tka, nsplit=NCHIP, name="ffn1_win_grad")

    small = dict(norm1=acc1[0, 3] + acc1[1, 3], norm2=acc_mix[0, 3] + acc_mix[1, 3], norm3=acc3[0, 3], fnorm=acc_fin[0],
                 gnw=acc_m[1, :HD], pscale=acc_m[2, :NPOOL], pool_w=dpw, conv=dconv[:5],
                 a_log=dcst[0, 2 * NH:4 * NH], dt_bias=dcst[1, 2 * NH:4 * NH])
    zero = jnp.zeros((D,), F32)
    dmod = jnp.stack([
        jnp.stack([acc1[0, 0], acc1[0, 1], acc1[0, 2], acc_mix[0, 0], acc_mix[0, 1], zero, zero, zero, zero]),
        jnp.stack([acc1[1, 0], acc1[1, 1], acc1[1, 2], acc_mix[1, 0], acc_mix[1, 1], acc_m[0], acc3[0, 0], acc3[0, 1], acc3[0, 2]]),
    ])
    return jnp.sum(acc_fin[1]), dx_lat, g, landed, small, dmod


_HI = lax.Precision.HIGHEST


def _ada_fwd(c_all, w_sh, b_sh, *, name):
    def body(c_ref, w_ref, b_ref, o_ref):
        o_ref[...] = jnp.dot(_silu(c_ref[...]), w_ref[...], precision=_HI, preferred_element_type=F32) + b_ref[...]

    return pl.pallas_call(body, name=name, out_shape=jax.ShapeDtypeStruct((16, w_sh.shape[1]), F32),
                          compiler_params=pltpu.CompilerParams(vmem_limit_bytes=VMEM_LIMIT))(c_all, w_sh, b_sh)


def _ada_bwd(c_all, dm, w_sh, *, name):
    def body(c_ref, dm_ref, w_ref, dw_ref, dc_ref):
        sc = _silu(c_ref[...])
        dw_ref[...] = lax.dot_general(sc, dm_ref[...], (((0,), (0,)), ((), ())), precision=_HI, preferred_element_type=F32)
        part = lax.dot_general(dm_ref[8:9, :], w_ref[...], (((1,), (1,)), ((), ())), precision=_HI, preferred_element_type=F32)
        dc_ref[...] = jnp.broadcast_to(part, dc_ref.shape)

    return pl.pallas_call(body, name=name,
                          out_shape=[jax.ShapeDtypeStruct(w_sh.shape, F32), jax.ShapeDtypeStruct((8, D), F32)],
                          compiler_params=pltpu.CompilerParams(vmem_limit_bytes=VMEM_LIMIT))(c_all, dm, w_sh)


def _cctx_grad(parts, c_ctx, *, name):
    def body(p_ref, c_ref, o_ref):
        tot = (p_ref[0, 0:1, :] + p_ref[2, 0:1, :]) + (p_ref[4, 0:1, :] + p_ref[6, 0:1, :])
        o_ref[...] = tot * _dsilu(c_ref[...])

    return pl.pallas_call(body, name=name, out_shape=jax.ShapeDtypeStruct((1, D), F32))(parts, c_ctx)


_MESH = pl.DeviceIdType.MESH
_ANY = pl.BlockSpec(memory_space=pl.ANY)


def _flip(v, bit):
    return (1 - v) if bit else v


def _all_gather8(x, *, name):
    def body(x_ref, out_ref, send_sems, recv_sems, local_sem):
        mx, my, mc = lax.axis_index("x"), lax.axis_index("y"), lax.axis_index("c")
        me = 4 * mx + 2 * my + mc
        mine = pltpu.make_async_copy(x_ref, out_ref.at[me], local_sem)
        mine.start()
        sends, recvs = [], []
        for k in range(1, 8):
            px, py, pc = _flip(mx, k & 4), _flip(my, k & 2), _flip(mc, k & 1)
            sends.append(pltpu.make_async_remote_copy(src_ref=x_ref, dst_ref=out_ref.at[me], send_sem=send_sems.at[k - 1],
                                                      recv_sem=recv_sems.at[k - 1], device_id=(px, py, pc), device_id_type=_MESH))
            recvs.append(pltpu.make_async_remote_copy(src_ref=x_ref, dst_ref=out_ref.at[4 * px + 2 * py + pc],
                                                      send_sem=send_sems.at[k - 1], recv_sem=recv_sems.at[k - 1],
                                                      device_id=(px, py, pc), device_id_type=_MESH))
        for cp in sends:
            cp.start()
        for cp in recvs:
            cp.wait_recv()
        for cp in sends:
            cp.wait_send()
        mine.wait()

    vm = pl.BlockSpec(memory_space=pltpu.VMEM)
    return pl.pallas_call(
        body, name=name, out_shape=jax.ShapeDtypeStruct((8,) + x.shape, x.dtype), in_specs=[vm], out_specs=vm,
        scratch_shapes=[pltpu.SemaphoreType.DMA((7,)), pltpu.SemaphoreType.DMA((7,)), pltpu.SemaphoreType.DMA],
        compiler_params=pltpu.CompilerParams(vmem_limit_bytes=VMEM_LIMIT),
    )(x)


class _ChipExchange:
    def __init__(self, arrs, scatter):
        self.arrs, self.scatter, self.n = list(arrs), scatter, len(arrs)
        self.out_shape = [jax.ShapeDtypeStruct(a.shape if scatter else (NCHIP,) + a.shape, a.dtype) for a in self.arrs]
        links = self.n * (NCHIP - 1)
        self.scratch = [pltpu.SemaphoreType.DMA((links,)), pltpu.SemaphoreType.DMA((links,)), pltpu.SemaphoreType.DMA((self.n,))]

    def copies(self, ins, outs, send_sems, recv_sems, local_sems):
        mx, my, mc = lax.axis_index("x"), lax.axis_index("y"), lax.axis_index("c")
        me = 2 * mx + my
        local, sends, recvs = [], [], []
        for j in range(self.n):
            src_own = ins[j].at[me] if self.scatter else ins[j]
            local.append(pltpu.make_async_copy(src_own, outs[j].at[me], local_sems.at[j]))
            for k in range(1, NCHIP):
                px, py = _flip(mx, k & 2), _flip(my, k & 1)
                peer = 2 * px + py
                sem = j * (NCHIP - 1) + k - 1
                src = ins[j].at[peer] if self.scatter else ins[j]
                sends.append(pltpu.make_async_remote_copy(src_ref=src, dst_ref=outs[j].at[me], send_sem=send_sems.at[sem],
                                                          recv_sem=recv_sems.at[sem], device_id=(px, py, mc), device_id_type=_MESH))
                recvs.append(pltpu.make_async_remote_copy(src_ref=src, dst_ref=outs[j].at[peer], send_sem=send_sems.at[sem],
                                                          recv_sem=recv_sems.at[sem], device_id=(px, py, mc), device_id_type=_MESH))
        return local, sends, recvs

    @staticmethod
    def start(local, sends, recvs):
        for cp in local + sends:
            cp.start()

    @staticmethod
    def finish(local, sends, recvs):
        for cp in recvs:
            cp.wait_recv()
        for cp in sends:
            cp.wait_send()
        for cp in local:
            cp.wait()


def _gather_split(arrs, *, name):
    n = len(arrs)
    links = n * (NCHIP - 1)

    def body(*refs):
        ins, outs = refs[:n], refs[n:2 * n]
        ici_send, ici_recv, d2d_send, d2d_recv, local_sems = refs[2 * n:]
        mx, my, mc = lax.axis_index("x"), lax.axis_index("y"), lax.axis_index("c")
        me = 2 * mx + my
        local, first, arrive, onward, handed = [], [], [], [], []
        for j in range(n):
            hr = ins[j].shape[0] // 2
            mine, other = pl.ds(mc * hr, hr), pl.ds((1 - mc) * hr, hr)
            local.append(pltpu.make_async_copy(ins[j], outs[j].at[me], local_sems.at[j]))
            for k in range(1, NCHIP):
                px, py = _flip(mx, k & 2), _flip(my, k & 1)
                peer = 2 * px + py
                sem = j * (NCHIP - 1) + k - 1
                ici = lambda slot: pltpu.make_async_remote_copy(
                    src_ref=ins[j].at[mine], dst_ref=outs[j].at[slot, mine], send_sem=ici_send.at[sem], recv_sem=ici_recv.at[sem],
                    device_id=(px, py, mc), device_id_type=_MESH)
                d2d = lambda rows: pltpu.make_async_remote_copy(
                    src_ref=outs[j].at[peer, rows], dst_ref=outs[j].at[peer, rows], send_sem=d2d_send.at[sem], recv_sem=d2d_recv.at[sem],
                    device_id=(mx, my, 1 - mc), device_id_type=_MESH)
                first.append(ici(me))
                arrive.append(ici(peer))
                onward.append(d2d(mine))
                handed.append(d2d(other))
        for cp in local + first:
            cp.start()
        for got, fwd in zip(arrive, onward):
            got.wait_recv()
            fwd.start()
        for cp in handed:
            cp.wait_recv()
        for cp in first + onward:
            cp.wait_send()
        for cp in local:
            cp.wait()

    sems = pltpu.SemaphoreType.DMA((links,))
    return pl.pallas_call(
        body, name=name, out_shape=[jax.ShapeDtypeStruct((NCHIP,) + a.shape, a.dtype) for a in arrs],
        in_specs=[_ANY] * n, out_specs=[_ANY] * n, scratch_shapes=[sems, sems, sems, sems, pltpu.SemaphoreType.DMA((n,))],
    )(*arrs)


_HBM = pl.BlockSpec(memory_space=pltpu.HBM)
_SEM = pl.BlockSpec(memory_space=pltpu.SEMAPHORE)
_DATAFLOW = pltpu.SideEffectType.DATAFLOW_SIDE_EFFECTING


def _scatter_copies(ins, lands, send_sems, recv_sems):
    mx, my, mc = lax.axis_index("x"), lax.axis_index("y"), lax.axis_index("c")
    me = 2 * mx + my
    sends, recvs = [], []
    for j in range(len(ins)):
        for k in range(1, NCHIP):
            px, py = _flip(mx, k & 2), _flip(my, k & 1)
            peer = 2 * px + py
            sem = j * (NCHIP - 1) + k - 1
            mk = lambda slot: pltpu.make_async_remote_copy(src_ref=ins[j].at[peer], dst_ref=lands[j].at[slot], send_sem=send_sems.at[sem],
                                                           recv_sem=recv_sems.at[sem], device_id=(px, py, mc), device_id_type=_MESH)
            sends.append(mk(me))
            recvs.append(mk(peer))
    return sends, recvs


def _scatter_start(arrs, after, *, name):
    n = len(arrs)
    links = n * (NCHIP - 1)
    n_in = 2 * n + len(after)

    def body(*refs):
        ins, lands = refs[:n], refs[n:2 * n]
        send_sems, recv_sems = refs[n_in], refs[n_in + 1]
        token = refs[-1]
        for cp in _scatter_copies(ins, lands, send_sems, recv_sems)[0]:
            cp.start()
        token[...] = jnp.zeros_like(token)

    hbm = lambda a: pltpu.HBM(a.shape, a.dtype)
    res = pl.pallas_call(
        body, name=name,
        out_shape=(pltpu.SemaphoreType.DMA((links,)), pltpu.SemaphoreType.DMA((links,)), *[hbm(a) for a in arrs], *[hbm(a) for a in arrs],
                   jax.ShapeDtypeStruct((8, 128), F32)),
        in_specs=[_HBM] * (2 * n) + [_ANY] * len(after), out_specs=(_SEM, _SEM, *[_HBM] * (2 * n), pl.BlockSpec(memory_space=pltpu.VMEM)),
        input_output_aliases={j: 2 + j for j in range(2 * n)},
        compiler_params=pltpu.CompilerParams(has_side_effects=_DATAFLOW),
    )(*[pltpu.with_memory_space_constraint(a, pltpu.HBM) for a in arrs],
      *[pltpu.with_memory_space_constraint(lax.empty(a.shape, a.dtype), pltpu.HBM) for a in arrs], *after)
    return res[0], res[1], list(res[2:2 + n]), list(res[2 + n:2 + 2 * n]), res[-1]


def _scatter_wait(send_sems, recv_sems, arrs, lands, after, *, name):
    n = len(arrs)

    def body(*refs):
        ins, lands_in = refs[:n], refs[n:2 * n]
        sends, recvs = _scatter_copies(ins, lands_in, refs[2 * n], refs[2 * n + 1])
        for cp in sends:
            cp.wait_send()
        for cp in recvs:
            cp.wait_recv()

    hbm = lambda a: pltpu.HBM(a.shape, a.dtype)
    res = pl.pallas_call(
        body, name=name, out_shape=(*[hbm(a) for a in arrs], *[hbm(a) for a in lands]),
        in_specs=[_HBM] * (2 * n) + [_SEM, _SEM] + [_ANY] * len(after), out_specs=[_HBM] * (2 * n),
        input_output_aliases={j: j for j in range(2 * n)},
        compiler_params=pltpu.CompilerParams(has_side_effects=_DATAFLOW),
    )(*arrs, *lands, send_sems, recv_sems, *after)
    return list(res[:n]), list(res[n:])


def _pallas(body, operands, *, name, grid, in_specs, out_specs, out_shape, sem, scratch_shapes=(), aliases=None, exchange=None):
    if exchange is None:
        return pl.pallas_call(body, name=name, grid=grid, in_specs=in_specs, out_specs=out_specs, out_shape=out_shape,
                              scratch_shapes=list(scratch_shapes), input_output_aliases=aliases or {},
                              compiler_params=_cparams(*sem))(*operands)
    ex = exchange
    (steps,) = grid
    n_in, n_out, n_scr, k = len(in_specs), len(out_specs), len(scratch_shapes), ex.n

    def hosted(*refs):
        ins, refs = refs[:n_in], refs[n_in:]
        ex_in, refs = refs[:k], refs[k:]
        outs, refs = refs[:n_out], refs[n_out:]
        ex_out, refs = refs[:k], refs[k:]
        scr, ex_sems = refs[:n_scr], refs[n_scr:]
        cps = ex.copies(ex_in, ex_out, *ex_sems)
        pl.when(pl.program_id(0) == 0)(lambda: ex.start(*cps))
        body(*ins, *outs, *scr)
        pl.when(pl.program_id(0) == steps - 1)(lambda: ex.finish(*cps))

    return pl.pallas_call(
        hosted, name=name, grid=grid, in_specs=list(in_specs) + [_ANY] * k, out_specs=list(out_specs) + [_ANY] * k,
        out_shape=list(out_shape) + ex.out_shape, scratch_shapes=list(scratch_shapes) + ex.scratch,
        input_output_aliases=aliases or {}, compiler_params=_cparams("arbitrary"),
    )(*operands, *ex.arrs)


def _core_swap(arrs, *, name):
    n = len(arrs)

    def body(*refs):
        ins, outs = refs[:n], refs[n:2 * n]
        send_sems, recv_sems = refs[2 * n:]
        sib = (lax.axis_index("x"), lax.axis_index("y"), 1 - lax.axis_index("c"))
        cps = [pltpu.make_async_remote_copy(src_ref=ins[j], dst_ref=outs[j], send_sem=send_sems.at[j], recv_sem=recv_sems.at[j],
                                            device_id=sib, device_id_type=_MESH) for j in range(n)]
        for cp in cps:
            cp.start()
        for cp in cps:
            cp.wait_recv()
        for cp in cps:
            cp.wait_send()

    return pl.pallas_call(
        body, name=name, out_shape=[jax.ShapeDtypeStruct(a.shape, a.dtype) for a in arrs],
        in_specs=[_ANY] * n, out_specs=[_ANY] * n,
        scratch_shapes=[pltpu.SemaphoreType.DMA((n,)), pltpu.SemaphoreType.DMA((n,))],
    )(*arrs)


def _row_tile(rows, cols, budget=1 << 20):
    best = None
    for t in range(8, rows + 1, 8):
        if rows % t == 0 and t * cols <= budget:
            best = t
    return best or rows


def _sum_slots(x, *, name, after=()):
    ns, r, c = x.shape
    tr = _row_tile(r, c * ns)

    def body(x_ref, *refs):
        acc = x_ref[0].astype(F32)
        for s in range(1, ns):
            acc = acc + x_ref[s].astype(F32)
        refs[-1][...] = acc

    return pl.pallas_call(
        body, name=name, grid=(r // tr,), out_shape=jax.ShapeDtypeStruct((r, c), F32),
        in_specs=[pl.BlockSpec((ns, tr, c), lambda i: (0, i, 0))] + [_ANY] * len(after), out_specs=pl.BlockSpec((tr, c), lambda i: (i, 0)),
        compiler_params=_cparams("parallel"),
    )(x, *after)


def _adamw(w, ga, gb, m, v, *, name):
    r, c = w.shape
    tr = _row_tile(r, c, budget=1 << 18)
    two = gb is not None

    def body(*refs):
        w_ref, ga_ref = refs[0], refs[1]
        m_ref, v_ref = refs[2 + two], refs[3 + two]
        g_ref, d_ref, mo_ref, vo_ref = refs[4 + two:]
        g = ga_ref[...] + refs[2][...] if two else ga_ref[...]
        mn = ADAM_B1 * m_ref[...] + (1.0 - ADAM_B1) * g
        vn = ADAM_B2 * v_ref[...] + (1.0 - ADAM_B2) * (g * g)
        m_hat = mn / (1.0 - ADAM_B1 ** ADAM_STEP)
        v_hat = vn / (1.0 - ADAM_B2 ** ADAM_STEP)
        g_ref[...] = g
        d_ref[...] = -ADAM_LR * (m_hat / (jnp.sqrt(v_hat) + ADAM_EPS) + ADAM_WD * w_ref[...])
        mo_ref[...] = mn
        vo_ref[...] = vn

    spec = pl.BlockSpec((tr, c), lambda i: (i, 0))
    ins = [w, ga] + ([gb] if two else []) + [m, v]
    return pl.pallas_call(
        body, name=name, grid=(r // tr,), out_shape=[jax.ShapeDtypeStruct((r, c), F32)] * 4,
        in_specs=[spec] * len(ins), out_specs=[spec] * 4, compiler_params=_cparams("parallel"),
    )(*ins)


_MIX_AB0, _MIX_AB1 = NQKV, NQKV + 4 * NH


def _regroup_mix(w):
    pad = jnp.zeros((w.shape[0], NMIXP - NMIX), w.dtype)
    return jnp.concatenate([w[:, :_MIX_AB0], w[:, _MIX_AB1:], w[:, _MIX_AB0:_MIX_AB1], pad], axis=1)


def _ungroup_mix(w):
    n_ab = _MIX_AB1 - _MIX_AB0
    return jnp.concatenate([w[:, :_MIX_AB0], w[:, NMIX - n_ab:NMIX], w[:, _MIX_AB0:NMIX - n_ab]], axis=1)


def _chip_major_cols(w):
    r, c = w.shape
    return w.reshape(r, NCHIP, c // NCHIP).transpose(1, 0, 2)


def _from_chip_major_cols(w):
    return w.transpose(1, 0, 2).reshape(w.shape[1], -1)


_SMALL = (("c_ctx", D), ("b_ada", 9 * D), ("norm1_w", D), ("norm2_w", D), ("norm3_w", D), ("final_norm_w", D),
          ("a_log", 2 * NH), ("dt_bias", 2 * NH), ("gdn_norm_w", HD), ("pool_w", 4 * 128 * 128), ("pool_scale", NPOOL),
          ("conv_w", 5 * NQKV // NCHIP))


def _pack(vals, lanes=128, row_mult=8):
    flat = jnp.concatenate([jnp.ravel(v) for v in vals])
    n = flat.shape[0]
    rows = -(-n // (lanes * row_mult)) * row_mult
    return jnp.pad(flat, (0, rows * lanes - n)).reshape(rows, lanes)


def _unpack(packed, sizes):
    flat = packed.reshape(-1)
    out, o = [], 0
    for n in sizes:
        out.append(flat[o:o + n])
        o += n
    return out


def kernel(x, c, ctx, c_ctx, w_ada, b_ada, norm1_w, ffn1_w_in, ffn1_w_out, norm2_w, w_mix_in, conv_w, a_log, dt_bias, gdn_norm_w, w_gdn_proj, pool_w, pool_scale, w_pool_proj, w_mix_out, norm3_w, ffn2_w_in, ffn2_w_out, final_norm_w, loss_target, m_c_ctx, m_w_ada, m_b_ada, m_norm1_w, m_ffn1_w_in, m_ffn1_w_out, m_norm2_w, m_w_mix_in, m_conv_w, m_a_log, m_dt_bias, m_gdn_norm_w, m_w_gdn_proj, m_pool_w, m_pool_scale, m_w_pool_proj, m_w_mix_out, m_norm3_w, m_ffn2_w_in, m_ffn2_w_out, m_final_norm_w, v_c_ctx, v_w_ada, v_b_ada, v_norm1_w, v_ffn1_w_in, v_ffn1_w_out, v_norm2_w, v_w_mix_in, v_conv_w, v_a_log, v_dt_bias, v_gdn_norm_w, v_w_gdn_proj, v_pool_w, v_pool_scale, v_w_pool_proj, v_w_mix_out, v_norm3_w, v_ffn2_w_in, v_ffn2_w_out, v_final_norm_w):
    names = ("c_ctx", "w_ada", "b_ada", "norm1_w", "ffn1_w_in", "ffn1_w_out", "norm2_w", "w_mix_in", "conv_w", "a_log", "dt_bias",
             "gdn_norm_w", "w_gdn_proj", "pool_w", "pool_scale", "w_pool_proj", "w_mix_out", "norm3_w", "ffn2_w_in", "ffn2_w_out",
             "final_norm_w")
    w = dict(zip(names, (c_ctx, w_ada, b_ada, norm1_w, ffn1_w_in, ffn1_w_out, norm2_w, w_mix_in, conv_w, a_log, dt_bias, gdn_norm_w,
                         w_gdn_proj, pool_w, pool_scale, w_pool_proj, w_mix_out, norm3_w, ffn2_w_in, ffn2_w_out, final_norm_w)))
    mom = dict(zip(names, (m_c_ctx, m_w_ada, m_b_ada, m_norm1_w, m_ffn1_w_in, m_ffn1_w_out, m_norm2_w, m_w_mix_in, m_conv_w, m_a_log,
                           m_dt_bias, m_gdn_norm_w, m_w_gdn_proj, m_pool_w, m_pool_scale, m_w_pool_proj, m_w_mix_out, m_norm3_w,
                           m_ffn2_w_in, m_ffn2_w_out, m_final_norm_w)))
    var = dict(zip(names, (v_c_ctx, v_w_ada, v_b_ada, v_norm1_w, v_ffn1_w_in, v_ffn1_w_out, v_norm2_w, v_w_mix_in, v_conv_w, v_a_log,
                           v_dt_bias, v_gdn_norm_w, v_w_gdn_proj, v_pool_w, v_pool_scale, v_w_pool_proj, v_w_mix_out, v_norm3_w,
                           v_ffn2_w_in, v_ffn2_w_out, v_final_norm_w)))
    mx, my, mc = lax.axis_index("x"), lax.axis_index("y"), lax.axis_index("c")
    chip = 2 * mx + my
    dev = 2 * chip + mc
    ada_cols = w_ada.shape[2]

    c_rows = _all_gather8(jnp.pad(c, ((0, 7), (0, 0))), name="gather_c")[:, 0, :]
    c_all = jnp.concatenate([c_rows, c_ctx[None], jnp.zeros((7, D), F32)], axis=0)
    b_sh = lax.dynamic_slice(b_ada, (0, chip * ada_cols), (1, ada_cols))
    mod_sh = _ada_fwd(c_all, w_ada[0], b_sh, name="ada_fwd")
    mod_parts = _all_gather8(mod_sh, name="gather_mod")
    mod_all = jnp.concatenate([mod_parts[2 * s] for s in range(NCHIP)], axis=1)
    mod_lat = lax.dynamic_index_in_dim(mod_all, dev, axis=0, keepdims=False).reshape(9, D)
    modv = jnp.zeros((2, 16, D), F32).at[0, :9].set(mod_all[8].reshape(9, D)).at[1, :9].set(mod_lat)

    big = ("ffn1_w_in", "ffn1_w_out", "w_mix_in", "w_gdn_proj", "w_pool_proj", "w_mix_out", "ffn2_w_in", "ffn2_w_out")
    shard = {k: w[k][0].astype(BF) for k in big}
    w1_in, w1_out = _gather_split([shard["ffn1_w_in"], shard["ffn1_w_out"]], name="gather_ffn1")
    p = dict(
        norm1=norm1_w, norm2=norm2_w, norm3=norm3_w, fnorm=final_norm_w[None], w1_in=w1_in, w1_out=w1_out.reshape(FF, D),
        cst=jnp.zeros((8, 128), F32).at[0, 2 * NH:4 * NH].set(jnp.exp(a_log).reshape(-1)).at[1, 2 * NH:4 * NH].set(dt_bias.reshape(-1)),
        gnw=gdn_norm_w, pool_w=pool_w[0], pscale=pool_scale)
    late = ([shard["w_mix_in"], conv_w[0]], [shard["ffn2_w_in"], shard["ffn2_w_out"]],
            [shard["w_gdn_proj"], shard["w_pool_proj"], shard["w_mix_out"]])

    loss_dev, dx_lat, g, landed, small, dmod = _local_step(ctx[0], x[0], loss_target[0], modv, p, late)
    loss = lax.psum(loss_dev, ("x", "y", "c"))
    grad_x = dx_lat[None]

    small_vals = [dmod[1], dmod[0], small["norm1"], small["norm2"], small["norm3"], small["fnorm"], small["a_log"], small["dt_bias"],
                  small["gnw"], small["pool_w"], small["pscale"], small["conv"]]
    small_sizes = [v.size for v in small_vals]
    packed = _all_gather8(_pack(small_vals), name="gather_small")
    tot = _unpack(_sum_slots(packed, name="sum_small"), small_sizes)
    dmod_lat_all = packed[:, :9 * D // 128, :].reshape(8, 9 * D)
    dm = jnp.concatenate([dmod_lat_all, tot[1][None], jnp.zeros((7, 9 * D), F32)], axis=0)
    dm_sh = lax.dynamic_slice(dm, (0, chip * ada_cols), (16, ada_cols))
    g_w_ada, cctx_part = _ada_bwd(c_all, dm_sh, w_ada[0], name="ada_bwd")
    g_c_ctx = _cctx_grad(_all_gather8(cctx_part, name="gather_cctx"), c_ctx[None], name="cctx_grad")[0]
    conv_tot = tot[11].reshape(5, NQKV)
    g_small = dict(c_ctx=g_c_ctx, b_ada=tot[0] + tot[1], norm1_w=tot[2], norm2_w=tot[3], norm3_w=tot[4], final_norm_w=tot[5],
                   a_log=tot[6], dt_bias=tot[7], gdn_norm_w=tot[8], pool_w=tot[9], pool_scale=tot[10],
                   conv_w=lax.dynamic_slice(conv_tot, (0, chip * (NQKV // NCHIP)), (5, NQKV // NCHIP)))

    first = ("ffn1_w_in", "ffn1_w_out")
    order = dict(zip(big, ("w1_in", "w1_out", "w_mix", "w_gdn", "w_pool", "w_mo", "w2_in", "w2_out")))
    rest = [k for k in big if k not in first]
    send_sems, recv_sems, sent, lands, token = _scatter_start([g["w1_in"], g["w1_out"]], [g_c_ctx, g_w_ada], name="scatter_ffn1_start")
    mine = {k: _sum_slots(landed[order[k]], name=f"sum_{k}", after=[token]) for k in rest}
    theirs = dict(zip(rest, _core_swap([mine[k] for k in rest], name="swap_grad_sums")))

    out = {}
    as2d = lambda a: a.reshape(-1, a.shape[-1])

    def update(k):
        res = _adamw(as2d(w[k]), as2d(mine[k]), as2d(theirs[k]), as2d(mom[k]), as2d(var[k]), name=f"adamw_{k}")
        out[k] = [r.reshape(w[k].shape) for r in res]

    for k in rest:
        update(k)
    out["w_ada"] = [r.reshape(w_ada.shape) for r in _adamw(w_ada[0], g_w_ada, None, m_w_ada[0], v_w_ada[0], name="adamw_w_ada")]
    sm_names = [n for n, _ in _SMALL]
    sm_sizes = [n for _, n in _SMALL]
    res = _adamw(_pack([w[k] for k in sm_names]), _pack([g_small[k] for k in sm_names]), None,
                 _pack([mom[k] for k in sm_names]), _pack([var[k] for k in sm_names]), name="adamw_small")
    done = [out[k][1] for k in rest] + [out["w_ada"][1], res[1]]
    res = [_unpack(r, sm_sizes) for r in res]
    for i, k in enumerate(sm_names):
        out[k] = [r[i].reshape(w[k].shape) for r in res]
    sent, lands = _scatter_wait(send_sems, recv_sems, sent, lands, done, name="scatter_ffn1_wait")
    for k, part, land in zip(first, sent, lands):
        own = lax.dynamic_slice_in_dim(part, chip, 1, axis=0)
        mine[k] = _sum_slots(lax.dynamic_update_slice_in_dim(land, own, chip, axis=0), name=f"sum_{k}")
    theirs.update(zip(first, _core_swap([mine[k] for k in first], name="swap_ffn1_sums")))
    for k in first:
        update(k)
    return (loss, grad_x, *[out[k][0] for k in names], *[out[k][1] for k in names], *[out[k][2] for k in names],
            *[out[k][3] for k in names])
```

```python
import functools

import jax
import jax.numpy as jnp
from jax import lax
from jax.experimental import pallas as pl
from jax.experimental.pallas import tpu as pltpu

F32 = jnp.float32
BF = jnp.bfloat16

D = 1024
FF = 2816
NH = 8
HD = 128
CH = 64
GW = 64
TM = 256
NQKV = 3 * NH * HD
NPOOL = 512
POOL_WINDOWS = (2, 4, 8, 16)
NMIX = 6688
NMIXP = 6784
EPS = 1e-6
NCHIP = 4
VMEM_LIMIT = 56 * 1024 * 1024

ADAM_LR, ADAM_B1, ADAM_B2, ADAM_EPS, ADAM_WD, ADAM_STEP = 0.001, 0.9, 0.999, 1e-08, 0.01, 10


def _cparams(*sem):
    return pltpu.CompilerParams(dimension_semantics=sem, vmem_limit_bytes=VMEM_LIMIT)


def _const_spec(shape):
    nd = len(shape)
    return pl.BlockSpec(shape, lambda *_: (0,) * nd, pipeline_mode=pl.Buffered(1))


def _dot(a, b, dims):
    return lax.dot_general(a.astype(BF), b.astype(BF), (dims, ((), ())), preferred_element_type=F32)


def _nn(a, b):
    return _dot(a, b, ((1,), (0,)))


def _nt(a, b):
    return _dot(a, b, ((1,), (1,)))


def _tn(a, b):
    return _dot(a, b, ((0,), (0,)))


def _silu(x):
    return x * jax.nn.sigmoid(x)


def _dsilu(x):
    s = jax.nn.sigmoid(x)
    return s * (1.0 + x * (1.0 - s))


def _norm_mod(x, nw, shift, scale):
    r = lax.rsqrt(jnp.mean(x * x, axis=-1, keepdims=True) + EPS)
    xh = x * r
    n = xh * nw
    return n * (1.0 + scale) + shift, n, xh, r


def _norm_mod_bwd(dh, n, xh, r, nw, scale):
    dn = dh * (1.0 + scale)
    dxh = dn * nw
    dx = r * (dxh - xh * jnp.mean(dxh * xh, axis=-1, keepdims=True))
    rs = lambda t: jnp.sum(t, axis=0, keepdims=True)
    return dx, rs(dh), rs(dh * n), rs(dn * xh)


def _stream_specs(ctx):
    if ctx is None:
        return [pl.BlockSpec((TM, D), lambda i: (i, 0))]
    return [pl.BlockSpec((TM, D), lambda i: (0, 0)), pl.BlockSpec((TM, D), lambda i: (jnp.maximum(i - 1, 0), 0))]


def _stream_tile(refs, lead):
    if not lead:
        return refs[0][...], refs[1:]
    return jnp.where(pl.program_id(0) == 0, refs[0][...], refs[1][...]), refs[2:]


def _ffn_fwd(x, modv, nw, w_in4, w_out, *, mrow, name, ctx=None, exchange=None):
    lead = ctx is not None
    n_tok = x.shape[0] + (TM if lead else 0)
    nt = n_tok // TM
    nset = modv.shape[0]
    ws = w_in4.shape[2]

    def body(*refs):
        xv, (mod_ref, nw_ref, win_ref, wout_ref, x1_ref, h_ref, gu_ref, f_ref) = _stream_tile(refs, lead)
        shift, scale, gate = mod_ref[0, mrow:mrow + 1, :], mod_ref[0, mrow + 1:mrow + 2, :], mod_ref[0, mrow + 2:mrow + 3, :]
        h, _, _, _ = _norm_mod(xv, nw_ref[...], shift, scale)
        hb = h.astype(BF)
        h_ref[...] = hb
        gus = [_nn(hb, win_ref[s]) for s in range(NCHIP)]
        for s in range(NCHIP):
            gu_ref[:, s * ws:(s + 1) * ws] = gus[s].astype(BF)
        g = jnp.concatenate(gus[:2], axis=1)
        u = jnp.concatenate(gus[2:], axis=1)
        f = _nn(_silu(g) * u, wout_ref[...])
        f_ref[...] = f.astype(BF)
        x1_ref[...] = xv + 0.5 * gate * f

    tile = lambda w: pl.BlockSpec((TM, w), lambda i: (i, 0))
    return _pallas(
        body, (*([ctx] if lead else []), x, modv, nw, w_in4, w_out), name=name, grid=(nt,), sem=("parallel",), exchange=exchange,
        in_specs=_stream_specs(ctx) + [pl.BlockSpec((1, 16, D), lambda i: (jnp.minimum(i, nset - 1), 0, 0)), _const_spec((1, D)),
                                       _const_spec(w_in4.shape), _const_spec(w_out.shape)],
        out_specs=[tile(D), tile(D), tile(2 * FF), tile(D)],
        out_shape=[jax.ShapeDtypeStruct((n_tok, D), F32), jax.ShapeDtypeStruct((n_tok, D), BF),
                   jax.ShapeDtypeStruct((n_tok, 2 * FF), BF), jax.ShapeDtypeStruct((n_tok, D), BF)])


def _ffn_bwd(dxo, x, gu, fo, modv, nw, w_in4, w_out, *, mrow, name, ctx=None, exchange=None):
    lead = ctx is not None
    dx_skip = int(lead)
    n_tok = x.shape[0] + dx_skip * TM
    nt = n_tok // TM
    nset = modv.shape[0]
    ws = w_in4.shape[2]

    def body(*refs):
        xv, (dxo_ref, gu_ref, f_ref, mod_ref, nw_ref, win_ref, wout_ref, dx_ref, a_ref, df_ref, dgu_ref, acc_ref) = _stream_tile(refs, lead)
        i = pl.program_id(0)
        dxo_v = dxo_ref[...]
        shift, scale, gate = mod_ref[0, mrow:mrow + 1, :], mod_ref[0, mrow + 1:mrow + 2, :], mod_ref[0, mrow + 2:mrow + 3, :]
        _, n, xh, r = _norm_mod(xv, nw_ref[...], shift, scale)
        df = 0.5 * gate * dxo_v
        dfb = df.astype(BF)
        df_ref[...] = dfb
        dgate = jnp.sum(0.5 * dxo_v * f_ref[...].astype(F32), axis=0, keepdims=True)
        da = _nt(dfb, wout_ref[...])
        g = gu_ref[:, :FF].astype(F32)
        u = gu_ref[:, FF:].astype(F32)
        sig = jax.nn.sigmoid(g)
        sg = g * sig
        a_ref[...] = (sg * u).astype(BF)
        dgu_ref[:, :FF] = (da * u * (sig * (1.0 + g - sg))).astype(BF)
        dgu_ref[:, FF:] = (da * sg).astype(BF)
        dh = _nt(dgu_ref[:, 0:ws], win_ref[0])
        for s in range(1, NCHIP):
            dh = dh + _nt(dgu_ref[:, s * ws:(s + 1) * ws], win_ref[s])
        dx, dshift, dscale, dnw = _norm_mod_bwd(dh, n, xh, r, nw_ref[...], scale)
        dx_ref[...] = dxo_v + dx

        @pl.when((i == 0) | (i == nset - 1))
        def _():
            acc_ref[...] = jnp.zeros_like(acc_ref)

        acc_ref[0, 0:1, :] += dshift
        acc_ref[0, 1:2, :] += dscale
        acc_ref[0, 2:3, :] += dgate
        acc_ref[0, 3:4, :] += dnw

    tile = lambda w: pl.BlockSpec((TM, w), lambda i: (i, 0))
    return _pallas(
        body, (*([ctx] if lead else []), x, dxo, gu, fo, modv, nw, w_in4, w_out), name=name, grid=(nt,), sem=("arbitrary",),
        exchange=exchange,
        in_specs=_stream_specs(ctx) + [tile(D), tile(2 * FF), tile(D),
                                       pl.BlockSpec((1, 16, D), lambda i: (jnp.minimum(i, nset - 1), 0, 0)), _const_spec((1, D)),
                                       _const_spec(w_in4.shape), _const_spec(w_out.shape)],
        out_specs=[pl.BlockSpec((TM, D), lambda i: (jnp.maximum(i - dx_skip, 0), 0)), tile(FF), tile(D), tile(2 * FF),
                   pl.BlockSpec((1, 8, D), lambda i: (jnp.minimum(i, nset - 1), 0, 0))],
        out_shape=[jax.ShapeDtypeStruct((n_tok - dx_skip * TM, D), F32), jax.ShapeDtypeStruct((n_tok, FF), BF),
                   jax.ShapeDtypeStruct((n_tok, D), BF), jax.ShapeDtypeStruct((n_tok, 2 * FF), BF),
                   jax.ShapeDtypeStruct((nset, 8, D), F32)])


def _k_tile(n, target=3072):
    return max(t for t in range(TM, min(n, target) + 1, TM) if n % t == 0)


def _matmul_tn(a, b, *, tmm, tn, tk, nsplit=1, name):
    n_tok, m = a.shape
    kk = b.shape[1]
    nk = n_tok // tk

    def body(a_ref, b_ref, o_ref, acc):
        k = pl.program_id(2)

        @pl.when(k == 0)
        def _():
            acc[...] = jnp.zeros_like(acc)

        acc[...] += _tn(a_ref[...], b_ref[...])

        @pl.when(k == nk - 1)
        def _():
            o_ref[...] = acc[...].astype(BF).reshape(o_ref.shape)

    if nsplit == 1:
        out_shape = jax.ShapeDtypeStruct((m, kk), BF)
        out_spec = pl.BlockSpec((tmm, tn), lambda i, j, k: (i, j))
    else:
        assert tn == kk // nsplit
        out_shape = jax.ShapeDtypeStruct((nsplit, m, tn), BF)
        out_spec = pl.BlockSpec((1, tmm, tn), lambda i, j, k: (j, i, 0))
    return pl.pallas_call(
        body, name=name, grid=(m // tmm, kk // tn, nk),
        in_specs=[pl.BlockSpec((tk, tmm), lambda i, j, k: (k, i)), pl.BlockSpec((tk, tn), lambda i, j, k: (k, j))],
        out_specs=out_spec, out_shape=out_shape,
        scratch_shapes=[pltpu.VMEM((tmm, tn), F32)],
        compiler_params=_cparams("parallel", "parallel", "arbitrary"),
    )(a, b)


_MIX_PARTS = (("qkv", 0, NQKV), ("gate", NQKV, 1024), ("pool", NQKV + 1024, NPOOL), ("br", NQKV + 1024 + NPOOL, 2048),
              ("ab", NMIXP - 128, 128))


def _mix_in_fwd(x1, modv, nw, w_mix, *, name, exchange=None):
    n_tok = x1.shape[0]

    def body(x_ref, mod_ref, nw_ref, w_ref, u_ref, *p_refs):
        u, _, _, _ = _norm_mod(x_ref[...], nw_ref[...], mod_ref[0, 3:4, :], mod_ref[0, 4:5, :])
        ub = u.astype(BF)
        u_ref[...] = ub
        for (_, c0, w), p_ref in zip(_MIX_PARTS, p_refs):
            p_ref[...] = _nn(ub, w_ref[:, c0:c0 + w])

    tile = lambda w: pl.BlockSpec((TM, w), lambda i: (i, 0))
    return _pallas(
        body, (x1, modv, nw, w_mix), name=name, grid=(n_tok // TM,), sem=("parallel",), exchange=exchange,
        in_specs=[tile(D), pl.BlockSpec((1, 16, D), lambda i: (jnp.minimum(i, 1), 0, 0)), _const_spec((1, D)),
                  _const_spec(w_mix.shape)],
        out_specs=[tile(D)] + [tile(w) for _, _, w in _MIX_PARTS],
        out_shape=[jax.ShapeDtypeStruct((n_tok, D), BF)] + [jax.ShapeDtypeStruct((n_tok, w), F32) for _, _, w in _MIX_PARTS])


def _mix_in_bwd(dxo, x1, dqkv, dgate, dpool, dbr, dab_f, dab_b, modv, nw, w_mix, *, name):
    n_tok = x1.shape[0]

    def body(dxo_ref, x_ref, dqkv_ref, dgate_ref, dpool_ref, dbr_ref, dabf_ref, dabb_ref, mod_ref, nw_ref, w_ref,
             dx_ref, dp_ref, acc_ref):
        i = pl.program_id(0)
        lat = i >= 1
        scale = mod_ref[0, 4:5, :]
        _, n, xh, r = _norm_mod(x_ref[...], nw_ref[...], mod_ref[0, 3:4, :], scale)
        dp_ref[:, 0:NQKV] = dqkv_ref[...].astype(BF)
        dp_ref[:, NQKV:NQKV + 1024] = jnp.where(lat, dgate_ref[...], 0.0).astype(BF)
        dp_ref[:, NQKV + 1024:NQKV + 1536] = jnp.where(lat, dpool_ref[...], 0.0).astype(BF)
        dp_ref[:, NQKV + 1536:NMIXP - 128] = jnp.where(lat, dbr_ref[...], 0.0).astype(BF)
        dp_ref[:, NMIXP - 128:] = (dabf_ref[...] + dabb_ref[...]).astype(BF)
        du = _nt(dp_ref[...], w_ref[...])
        dx, dshift, dscale, dnw = _norm_mod_bwd(du, n, xh, r, nw_ref[...], scale)
        dx_ref[...] = jnp.where(lat, dxo_ref[...], 0.0) + dx

        @pl.when(i <= 1)
        def _():
            acc_ref[...] = jnp.zeros_like(acc_ref)

        acc_ref[0, 0:1, :] += dshift
        acc_ref[0, 1:2, :] += dscale
        acc_ref[0, 3:4, :] += dnw

    tile = lambda w: pl.BlockSpec((TM, w), lambda i: (i, 0))
    ltile = lambda w: pl.BlockSpec((TM, w), lambda i: (jnp.maximum(i - 1, 0), 0))
    return pl.pallas_call(
        body, name=name, grid=(n_tok // TM,),
        in_specs=[ltile(D), tile(D), tile(NQKV), ltile(1024), ltile(NPOOL), ltile(2048), tile(128), tile(128),
                  pl.BlockSpec((1, 16, D), lambda i: (jnp.minimum(i, 1), 0, 0)), _const_spec((1, D)), _const_spec(w_mix.shape)],
        out_specs=[tile(D), tile(NMIXP), pl.BlockSpec((1, 8, D), lambda i: (jnp.minimum(i, 1), 0, 0))],
        out_shape=[jax.ShapeDtypeStruct((n_tok, D), F32), jax.ShapeDtypeStruct((n_tok, NMIXP), BF),
                   jax.ShapeDtypeStruct((2, 8, D), F32)],
        compiler_params=_cparams("arbitrary"),
    )(dxo, x1, dqkv, dgate, dpool, dbr, dab_f, dab_b, modv, nw, w_mix)


def _qkv_act(pre, part):
    s = _silu(pre)
    if part == 2:
        return s
    nrm = s * lax.rsqrt(jnp.sum(s * s, axis=-1, keepdims=True) + EPS)
    return nrm * HD ** -0.5 if part == 0 else nrm


def _halo_specs(nt):
    r = TM // 8
    main = pl.BlockSpec((TM, NQKV), lambda i: (i, 0))
    prev = pl.BlockSpec((8, NQKV), lambda i: (jnp.maximum(i * r - 1, 0), 0))
    nxt = pl.BlockSpec((8, NQKV), lambda i: (jnp.minimum((i + 1) * r, nt * r - 1), 0))
    return main, prev, nxt


def _prep_fwd(p_qkv, conv_w8, *, name):
    n_tok = p_qkv.shape[0]
    nt = n_tok // TM

    def body(x_ref, xp_ref, xn_ref, w_ref, o_ref, pre_ref, win):
        i = pl.program_id(0)
        has_prev = (i != 0) & (i != 1)
        has_next = (i != 0) & (i != nt - 1)
        win[0:8, :] = jnp.where(has_prev, xp_ref[...], 0.0)
        win[8:8 + TM, :] = x_ref[...]
        win[8 + TM:, :] = jnp.where(has_next, xn_ref[...], 0.0)
        for hb in range(3 * NH):
            hs = slice(hb * HD, (hb + 1) * HD)
            pre = win[6:6 + TM, hs] * w_ref[0:1, hs]
            for k in range(1, 5):
                pre = pre + win[6 + k:6 + k + TM, hs] * w_ref[k:k + 1, hs]
            pre_ref[:, hs] = pre
            o_ref[:, hs] = _qkv_act(pre, hb // NH)

    main, prev, nxt = _halo_specs(nt)
    return pl.pallas_call(
        body, name=name, grid=(nt,),
        in_specs=[main, prev, nxt, pl.BlockSpec((8, NQKV), lambda i: (0, 0))],
        out_specs=[main, main], out_shape=[jax.ShapeDtypeStruct((n_tok, NQKV), F32)] * 2,
        scratch_shapes=[pltpu.VMEM((TM + 16, NQKV), F32)],
        compiler_params=_cparams("parallel"),
    )(p_qkv, p_qkv, p_qkv, conv_w8)


def _prep_bwd(p_qkv, pre, dqkv_f, dqkv_b, conv_w8, *, name):
    n_tok = p_qkv.shape[0]
    nt = n_tok // TM

    def body(x_ref, p_ref, pp_ref, pn_ref, g_ref, gp_ref, gn_ref, g2_ref, g2p_ref, g2n_ref, w_ref, dx_ref, dw_ref, pwin, gwin, dwin):
        i = pl.program_id(0)
        has_prev = (i != 0) & (i != 1)
        has_next = (i != 0) & (i != nt - 1)
        pwin[0:8, :] = jnp.where(has_prev, pp_ref[...], 0.0)
        pwin[8:8 + TM, :] = p_ref[...]
        pwin[8 + TM:, :] = jnp.where(has_next, pn_ref[...], 0.0)
        gwin[0:8, :] = jnp.where(has_prev, gp_ref[...] + g2p_ref[...], 0.0)
        gwin[8:8 + TM, :] = g_ref[...] + g2_ref[...]
        gwin[8 + TM:, :] = jnp.where(has_next, gn_ref[...] + g2n_ref[...], 0.0)

        @pl.when(i == 0)
        def _():
            dw_ref[...] = jnp.zeros_like(dw_ref)

        for hb in range(3 * NH):
            hs = slice(hb * HD, (hb + 1) * HD)
            _, vjp = jax.vjp(functools.partial(_qkv_act, part=hb // NH), pwin[:, hs])
            dwin[:, hs] = vjp(gwin[:, hs])[0]
            xv = x_ref[:, hs]
            dx = None
            for k in range(5):
                sh = dwin[10 - k:10 - k + TM, hs]
                dx = sh * w_ref[k:k + 1, hs] if dx is None else dx + sh * w_ref[k:k + 1, hs]
                dw_ref[k:k + 1, hs] += jnp.sum(sh * xv, axis=0, keepdims=True)
            dx_ref[:, hs] = dx.astype(BF)

    main, prev, nxt = _halo_specs(nt)
    wspec = pl.BlockSpec((8, NQKV), lambda i: (0, 0))
    return pl.pallas_call(
        body, name=name, grid=(nt,),
        in_specs=[main, main, prev, nxt, main, prev, nxt, main, prev, nxt, wspec],
        out_specs=[main, wspec],
        out_shape=[jax.ShapeDtypeStruct((n_tok, NQKV), BF), jax.ShapeDtypeStruct((8, NQKV), F32)],
        scratch_shapes=[pltpu.VMEM((TM + 16, NQKV), F32)] * 3,
        compiler_params=_cparams("arbitrary"),
    )(p_qkv, pre, pre, pre, dqkv_f, dqkv_f, dqkv_f, dqkv_b, dqkv_b, dqkv_b, conv_w8)


@jax.custom_vjp
def _mm_nn(a, b):
    return _nn(a, b)


@jax.custom_vjp
def _mm_nt(a, b):
    return _nt(a, b)


@jax.custom_vjp
def _mm_tn(a, b):
    return _tn(a, b)


_mm_nn.defvjp(lambda a, b: (_nn(a, b), (a, b)), lambda r, g: (_mm_nt(g, r[1]), _mm_tn(r[0], g)))
_mm_nt.defvjp(lambda a, b: (_nt(a, b), (a, b)), lambda r, g: (_mm_nn(g, r[1]), _mm_tn(g, r[0])))
_mm_tn.defvjp(lambda a, b: (_tn(a, b), (a, b)), lambda r, g: (_mm_nt(r[1], g), _mm_nn(r[0], g)))


def _each(f, *lists):
    return tuple(f(*a) for a in zip(*lists))


def _unit_tri_inv(ls, revs):
    ii = lax.broadcasted_iota(jnp.int32, (CH, CH), 0)
    jj = lax.broadcasted_iota(jnp.int32, (CH, CH), 1)
    eye = (ii == jj).astype(F32)
    xs = None
    s = 1
    while s < CH:
        same = (ii & -(2 * s)) == (jj & -(2 * s))
        off = {False: same & ((ii & s) != 0) & ((jj & s) == 0), True: same & ((jj & s) != 0) & ((ii & s) == 0)}
        cs = _each(lambda l, r: jnp.where(off[r], l, 0.0), ls, revs)
        if xs is None:
            xs = _each(lambda c: eye - c, cs)
        else:
            xc = _each(_nn, xs, cs)
            xcx = _each(_nn, xc, xs)
            xs = _each(lambda x, t: x - t, xs, xcx)
        s *= 2
    return xs


@functools.lru_cache(maxsize=None)
def _tri_solve(revs):
    @jax.custom_vjp
    def solve(ls, rhss):
        return _each(_mm_nn, _unit_tri_inv(ls, revs), rhss)

    def fwd(ls, rhss):
        ainv = _unit_tri_inv(ls, revs)
        xs = _each(_mm_nn, ainv, rhss)
        return xs, (ainv, xs)

    def bwd(res, gs):
        ainv, xs = res
        drhs = _each(_mm_tn, ainv, gs)
        return _each(lambda d, x: -_mm_nt(d, x), drhs, xs), drhs

    solve.defvjp(fwd, bwd)
    return solve


def _chunk_prep(q, k, v, beta, g, *, revs):
    ii = lax.broadcasted_iota(jnp.int32, (CH, CH), 0)
    jj = lax.broadcasted_iota(jnp.int32, (CH, CH), 1)
    eye = ii == jj
    incl_of = {False: ii >= jj, True: ii <= jj}
    strict_of = {False: ii > jj, True: ii < jj}
    g_row = _each(lambda t: jnp.sum(jnp.where(eye, t, 0.0), axis=0, keepdims=True), g)
    cum = _each(lambda t, r: jnp.sum(jnp.where(incl_of[r], t, 0.0), axis=1, keepdims=True), g_row, revs)
    cum_row = _each(lambda t: jnp.sum(jnp.where(eye, t, 0.0), axis=0, keepdims=True), cum)
    total = _each(lambda t: jnp.sum(t, axis=0, keepdims=True), g)
    decay = _each(lambda c, cr, r: jnp.where(incl_of[r], jnp.exp(jnp.where(incl_of[r], c - cr, 0.0)), 0.0), cum, cum_row, revs)
    kb = _each(jnp.multiply, k, beta)
    vb = _each(jnp.multiply, v, beta)
    kk = _each(_mm_nt, kb, k)
    lmat = _each(lambda t, dc, r: jnp.where(strict_of[r], t * dc, 0.0), kk, decay, revs)
    ecum = _each(jnp.exp, cum)
    rhs = _each(lambda a, b, e: jnp.concatenate([a, b * e], axis=1), vb, kb, ecum)
    sol = _tri_solve(revs)(lmat, rhs)
    qk = _each(_mm_nt, q, k)
    aqk = _each(jnp.multiply, qk, decay)
    qd = _each(jnp.multiply, q, ecum)
    kd = _each(lambda a, t, c: a * jnp.exp(t - c), k, total, cum)
    return sol, aqk, qd, kd, _each(jnp.exp, total)


def _chunk_rec(sol, aqk, qd, kd, bl, s):
    ws = _each(lambda so, st: _mm_nn(so[:, HD:], st), sol, s)
    v_new = _each(lambda so, t: so[:, :HD] - t, sol, ws)
    qs = _each(_mm_nn, qd, s)
    av = _each(_mm_nn, aqk, v_new)
    o = _each(jnp.add, qs, av)
    kv = _each(_mm_tn, kd, v_new)
    s_new = _each(lambda st, b, u: st * b + u, s, bl, kv)
    return o, s_new


def _lane_col(x, c):
    lane = lax.broadcasted_iota(jnp.int32, x.shape, 1)
    return jnp.sum(jnp.where(lane == c, x, 0.0), axis=1, keepdims=True)


def _gates(ab, cst):
    z = ab + cst[1:2, :]
    softplus = jnp.maximum(z, 0.0) + jnp.log(1.0 + jnp.exp(-jnp.abs(z)))
    return jax.nn.sigmoid(ab), -cst[0:1, :] * softplus, -cst[0:1, :] * jax.nn.sigmoid(z)


def _beta_g(gates, d, h):
    sig, g, dg = gates
    return _lane_col(sig, NH * d + h), _lane_col(g, 2 * NH + NH * d + h), _lane_col(dg, 2 * NH + NH * d + h)


STEPS = 2
TS = STEPS * CH
_CHAINS = tuple((t, d, h) for t in range(STEPS) for d in (0, 1) for h in range(NH))
_REVS = tuple(bool(d) for _, d, _ in _CHAINS)
_PER_STEP = 2 * NH


def _chain_inputs(refs, r0s, ab_refs, cst):
    hs = lambda h: slice(h * HD, (h + 1) * HD)
    gates = [[_gates(ab_refs[d][pl.ds(r0s[t][d], CH), :], cst) for d in (0, 1)] for t in range(STEPS)]
    q = _each(lambda c: refs[c[1]][0][pl.ds(r0s[c[0]][c[1]], CH), hs(c[2])], _CHAINS)
    k = _each(lambda c: refs[c[1]][1][pl.ds(r0s[c[0]][c[1]], CH), hs(c[2])], _CHAINS)
    v = _each(lambda c: refs[c[1]][2][pl.ds(r0s[c[0]][c[1]], CH), hs(c[2])], _CHAINS)
    bg = _each(lambda c: _beta_g(gates[c[0]][c[1]], c[1], c[2]), _CHAINS)
    return q, k, v, bg


def _of_step(parts, t):
    return tuple(p[t * _PER_STEP:(t + 1) * _PER_STEP] for p in parts)


def _scan_fwd(qkv, ab, cst, s0, *, row_blk0, nb, name, exchange=None):
    cb = TS // CH
    w = NH * HD

    def body(qf, kf, vf, abf, qb, kb, vb, abb, cst_ref, s0_ref, of_ref, ob_ref, sallf_ref, sallb_ref, sfin_ref, s_scr):
        i = pl.program_id(0)

        @pl.when(i == 0)
        def _():
            s_scr[...] = s0_ref[...]

        o_refs, sall_refs = (of_ref, ob_ref), (sallf_ref, sallb_ref)

        def chunks(ci, carry):
            cs = [(ci * STEPS + t, cb - 1 - ci * STEPS - t) for t in range(STEPS)]
            r0s = [tuple(pl.multiple_of(c * CH, CH) for c in ct) for ct in cs]
            q, k, v, bg = _chain_inputs(((qf, kf, vf), (qb, kb, vb)), r0s, (abf, abb), cst_ref[...])
            parts = _chunk_prep(q, k, v, _each(lambda t: t[0], bg), _each(lambda t: t[1], bg), revs=_REVS)
            s = _each(lambda c: s_scr[c[1], c[2]], _CHAINS[:_PER_STEP])
            for t in range(STEPS):
                for (_, d, h), sv in zip(_CHAINS, s):
                    sall_refs[d][cs[t][d], h] = sv
                o, s = _chunk_rec(*_of_step(parts, t), s)
                for (_, d, h), ov in zip(_CHAINS, o):
                    o_refs[d][pl.ds(r0s[t][d], CH), h * HD:(h + 1) * HD] = ov
            for (_, d, h), sv in zip(_CHAINS, s):
                s_scr[d, h] = sv
            return carry

        lax.fori_loop(0, cb // STEPS, chunks, 0)

        @pl.when(i == nb - 1)
        def _():
            sfin_ref[...] = s_scr[...]

    pos = (lambda i: i, lambda i: nb - 1 - i)
    col = lambda d, c: pl.BlockSpec((TS, w), lambda i: (row_blk0 + pos[d](i), c))
    abs_ = lambda d: pl.BlockSpec((TS, 128), lambda i: (row_blk0 + pos[d](i), 0))
    full4 = pl.BlockSpec((2, NH, HD, HD), lambda i: (0, 0, 0, 0))
    o_spec = lambda d: pl.BlockSpec((TS, w), lambda i: (pos[d](i), 0))
    sall_spec = lambda d: pl.BlockSpec((cb, NH, HD, HD), lambda i: (pos[d](i), 0, 0, 0))
    return _pallas(
        body, (qkv, qkv, qkv, ab, qkv, qkv, qkv, ab, cst, s0), name=name, grid=(nb,), sem=("arbitrary",), exchange=exchange,
        in_specs=[col(0, 0), col(0, 1), col(0, 2), abs_(0), col(1, 0), col(1, 1), col(1, 2), abs_(1),
                  pl.BlockSpec((8, 128), lambda i: (0, 0)), full4],
        out_specs=[o_spec(0), o_spec(1), sall_spec(0), sall_spec(1), full4],
        out_shape=[jax.ShapeDtypeStruct((nb * TS, w), F32)] * 2 + [jax.ShapeDtypeStruct((nb * cb, NH, HD, HD), F32)] * 2
        + [jax.ShapeDtypeStruct((2, NH, HD, HD), F32)],
        scratch_shapes=[pltpu.VMEM((2, NH, HD, HD), F32)])


def _scan_bwd(qkv, ab, cst, sall_f, sall_b, do, dsfin, dqkv_f, dqkv_b, dab_f, dab_b, dcst, *, row_blk0, nb, has_do, name,
              exchange=None):
    cb = TS // CH
    w = NH * HD

    def body(qf, kf, vf, abf, qb, kb, vb, abb, cst_ref, sallf_ref, sallb_ref, dof_ref, dob_ref, dsfin_ref, _f, _b, _af, _ab, dcst_in,
             dqkvf_ref, dqkvb_ref, dabf_ref, dabb_ref, dcst_ref, ds0_ref, ds_scr):
        i = pl.program_id(0)

        @pl.when(i == 0)
        def _():
            ds_scr[...] = dsfin_ref[...]
            dcst_ref[...] = dcst_in[...]

        lane = lax.broadcasted_iota(jnp.int32, (CH, 128), 1)
        lane1 = lax.broadcasted_iota(jnp.int32, (1, 128), 1)
        sall_refs, do_refs = (sallf_ref, sallb_ref), (dof_ref, dob_ref)
        dqkv_refs, dab_refs = (dqkvf_ref, dqkvb_ref), (dabf_ref, dabb_ref)

        def chunks(ci, carry):
            cs = [(cb - 1 - ci * STEPS - t, ci * STEPS + t) for t in range(STEPS)]
            r0s = [tuple(pl.multiple_of(c * CH, CH) for c in ct) for ct in cs]
            q, k, v, bg = _chain_inputs(((qf, kf, vf), (qb, kb, vb)), r0s, (abf, abb), cst_ref[...])
            beta, g = _each(lambda t: t[0], bg), _each(lambda t: t[1], bg)
            parts, prep_vjp = jax.vjp(functools.partial(_chunk_prep, revs=_REVS), q, k, v, beta, g)
            ds = _each(lambda c: ds_scr[c[1], c[2]], _CHAINS[:_PER_STEP])
            dparts = []
            for t in range(STEPS):
                s = _each(lambda c: sall_refs[c[1]][cs[t][c[1]], c[2]], _CHAINS[:_PER_STEP])
                _, rec_vjp = jax.vjp(_chunk_rec, *_of_step(parts, t), s)
                do_t = _each(lambda c: do_refs[c[1]][pl.ds(r0s[t][c[1]], CH), c[2] * HD:(c[2] + 1) * HD] if has_do
                             else jnp.zeros((CH, HD), F32), _CHAINS[:_PER_STEP])
                *dpt, ds = rec_vjp((do_t, ds))
                dparts.append(dpt)
            for (_, d, h), dsv in zip(_CHAINS, ds):
                ds_scr[d, h] = dsv
            dq, dk, dv, dbeta, dg = prep_vjp(tuple(sum((dparts[t][j] for t in range(STEPS)), ()) for j in range(len(dparts[0]))))
            dab = [[jnp.zeros((CH, 128), F32), jnp.zeros((CH, 128), F32)] for _ in range(STEPS)]
            dal = jnp.zeros((1, 128), F32)
            for n, (t, d, h) in enumerate(_CHAINS):
                for part, val in enumerate((dq[n], dk[n], dv[n])):
                    dqkv_refs[d][pl.ds(r0s[t][d], CH), part * w + h * HD:part * w + (h + 1) * HD] = val
                dbraw = dbeta[n] * beta[n] * (1.0 - beta[n])
                daraw = dg[n] * bg[n][2]
                dab[t][d] = dab[t][d] + jnp.where(lane == NH * d + h, dbraw, 0.0) + jnp.where(lane == 2 * NH + NH * d + h, daraw, 0.0)
                dal = dal + jnp.where(lane1 == 2 * NH + NH * d + h, jnp.sum(dg[n] * g[n], axis=0, keepdims=True), 0.0)
            dsum = jnp.zeros((CH, 128), F32)
            for t in range(STEPS):
                for d in (0, 1):
                    dab_refs[d][pl.ds(r0s[t][d], CH), :] = dab[t][d]
                    dsum = dsum + dab[t][d]
            dcst_ref[0:1, :] += dal
            dcst_ref[1:2, :] += jnp.sum(jnp.where(lane >= 2 * NH, dsum, 0.0), axis=0, keepdims=True)
            return carry

        lax.fori_loop(0, cb // STEPS, chunks, 0)

        @pl.when(i == nb - 1)
        def _():
            ds0_ref[...] = ds_scr[...]

    pos = (lambda i: nb - 1 - i, lambda i: i)
    col = lambda d, c: pl.BlockSpec((TS, w), lambda i: (row_blk0 + pos[d](i), c))
    abs_ = lambda d: pl.BlockSpec((TS, 128), lambda i: (row_blk0 + pos[d](i), 0))
    full4 = pl.BlockSpec((2, NH, HD, HD), lambda i: (0, 0, 0, 0))
    small = pl.BlockSpec((8, 128), lambda i: (0, 0))
    hbm = pl.BlockSpec(memory_space=pl.ANY)
    sall_spec = lambda d: pl.BlockSpec((cb, NH, HD, HD), lambda i: (pos[d](i), 0, 0, 0))
    do_spec = (lambda d: pl.BlockSpec((TS, w), lambda i: (pos[d](i), 0))) if has_do else (lambda d: small)
    acc_specs = [pl.BlockSpec((TS, 3 * w), lambda i: (row_blk0 + pos[0](i), 0)),
                 pl.BlockSpec((TS, 3 * w), lambda i: (row_blk0 + pos[1](i), 0)), abs_(0), abs_(1), small]
    return _pallas(
        body, (qkv, qkv, qkv, ab, qkv, qkv, qkv, ab, cst, sall_f, sall_b, do, do, dsfin, dqkv_f, dqkv_b, dab_f, dab_b, dcst),
        name=name, grid=(nb,), sem=("arbitrary",), exchange=exchange,
        in_specs=[col(0, 0), col(0, 1), col(0, 2), abs_(0), col(1, 0), col(1, 1), col(1, 2), abs_(1), small,
                  sall_spec(0), sall_spec(1), do_spec(0), do_spec(1), full4, hbm, hbm, hbm, hbm, small],
        out_specs=acc_specs + [full4],
        out_shape=[jax.ShapeDtypeStruct(dqkv_f.shape, F32), jax.ShapeDtypeStruct(dqkv_b.shape, F32),
                   jax.ShapeDtypeStruct(dab_f.shape, F32), jax.ShapeDtypeStruct(dab_b.shape, F32),
                   jax.ShapeDtypeStruct((8, 128), F32), jax.ShapeDtypeStruct((2, NH, HD, HD), F32)],
        aliases={14: 0, 15: 1, 16: 2, 17: 3, 18: 4},
        scratch_shapes=[pltpu.VMEM((2, NH, HD, HD), F32)])


def _pool(xin, *, row0, transpose, name):
    n_tok = xin.shape[0] - row0
    rows = n_tok // GW
    pad = 8 * GW
    tt = 512
    gsh = GW.bit_length() - 1

    def body(x_ref, o_ref, ybuf):
        ii = lax.broadcasted_iota(jnp.int32, (128, 128), 0)
        jj = lax.broadcasted_iota(jnp.int32, (128, 128), 1)
        same_row = (ii >> gsh) == (jj >> gsh)
        ci, cj = ii & (GW - 1), jj & (GW - 1)
        tok = lax.broadcasted_iota(jnp.int32, (tt, 1), 0)
        zpad = jnp.zeros((pad, 128), F32)
        for gi, wdw in enumerate(POOL_WINDOWS):
            lo, hi = wdw // 2, wdw - wdw // 2
            if transpose:
                band = same_row & (ci - cj >= -lo) & (ci - cj < hi)
                offs = range(-hi + 1, lo + 1)
            else:
                band = same_row & (cj - ci >= -lo) & (cj - ci < hi)
                offs = range(-lo, hi)
            bandm = band.astype(BF)
            cs = slice(gi * 128, (gi + 1) * 128)
            ybuf[0:pad, :] = zpad
            ybuf[pad + n_tok:, :] = zpad

            def inv_area(t0):
                t = t0 + tok
                r, c = t >> gsh, t & (GW - 1)
                nr = jnp.minimum(r + hi, rows) - jnp.maximum(r - lo, 0)
                nc = jnp.minimum(c + hi, GW) - jnp.maximum(c - lo, 0)
                return 1.0 / (nr * nc).astype(F32)

            def col_pass(b, carry):
                t0 = pl.multiple_of(b * tt, tt)
                xv = x_ref[pl.ds(row0 + t0, tt), cs]
                if transpose:
                    xv = xv * inv_area(t0)
                hi_part = xv.astype(BF)
                lo_part = (xv - hi_part.astype(F32)).astype(BF)
                for s in range(tt // 128):
                    sl = slice(s * 128, (s + 1) * 128)
                    y = (jnp.dot(bandm, hi_part[sl], preferred_element_type=F32)
                         + jnp.dot(bandm, lo_part[sl], preferred_element_type=F32))
                    ybuf[pl.ds(pad + t0 + s * 128, 128), :] = y
                return carry

            lax.fori_loop(0, n_tok // tt, col_pass, 0)

            def row_pass(b, carry):
                t0 = pl.multiple_of(b * tt, tt)
                acc = ybuf[pl.ds(pad + t0 + offs[0] * GW, tt), :]
                for dr in offs[1:]:
                    acc = acc + ybuf[pl.ds(pad + t0 + dr * GW, tt), :]
                xv = x_ref[pl.ds(row0 + t0, tt), cs]
                if not transpose:
                    acc = acc * inv_area(t0)
                o_ref[pl.ds(t0, tt), cs] = acc - xv
                return carry

            lax.fori_loop(0, n_tok // tt, row_pass, 0)

    return pl.pallas_call(
        body, name=name, out_shape=jax.ShapeDtypeStruct((n_tok, NPOOL), F32),
        in_specs=[pl.BlockSpec(memory_space=pltpu.VMEM)], out_specs=pl.BlockSpec(memory_space=pltpu.VMEM),
        scratch_shapes=[pltpu.VMEM((n_tok + 2 * pad, 128), F32)],
        compiler_params=pltpu.CompilerParams(vmem_limit_bytes=VMEM_LIMIT),
    )(xin)


def _merge_parts(of, ob, pgate, pd, br, gnw, pw_ref, pscale, wg_ref, wp_ref):
    o = of + ob
    ons, ohs, rs = [], [], []
    for h in range(NH):
        oh = o[:, h * HD:(h + 1) * HD]
        r = lax.rsqrt(jnp.mean(oh * oh, axis=-1, keepdims=True) + EPS)
        ohs.append(oh * r)
        rs.append(r)
        ons.append(oh * r * gnw)
    on = jnp.concatenate(ons, axis=1)
    sig_gate = jax.nn.sigmoid(pgate)
    silu_gate = pgate * sig_gate
    og = on * silu_gate
    y_gdn = _nn(og, wg_ref[...])
    ypre = jnp.concatenate([_nn(pd[:, g * 128:(g + 1) * 128], pw_ref[g]) for g in range(4)], axis=1)
    yp = ypre * pscale
    y_pool = _nn(yp, wp_ref[...])
    g_pool = jax.nn.sigmoid(br[:, :D])
    g_gdn = jax.nn.sigmoid(br[:, D:])
    return dict(on=on, ohs=ohs, rs=rs, og=og, y_gdn=y_gdn, ypre=ypre, yp=yp, y_pool=y_pool, g_pool=g_pool, g_gdn=g_gdn,
                sig_gate=sig_gate, silu_gate=silu_gate)


def _merge_fwd(x1, of, ob, pgate, pd, br, modv, gnw, pool_w, pscale, w_gdn, w_pool, w_mo, *, name):
    n_tok = of.shape[0]

    def body(x_ref, of_ref, ob_ref, pg_ref, pd_ref, br_ref, mod_ref, gnw_ref, pw_ref, ps_ref, wg_ref, wp_ref, wmo_ref,
             x2_ref, og_ref, yp_ref, m_ref, mix_ref):
        t = _merge_parts(of_ref[...], ob_ref[...], pg_ref[...], pd_ref[...], br_ref[...], gnw_ref[...], pw_ref, ps_ref[...],
                         wg_ref, wp_ref)
        m = t["g_pool"] * t["y_pool"] + t["g_gdn"] * t["y_gdn"]
        mix = _nn(m, wmo_ref[...])
        og_ref[...] = t["og"].astype(BF)
        yp_ref[...] = t["yp"].astype(BF)
        m_ref[...] = m.astype(BF)
        mix_ref[...] = mix.astype(BF)
        x2_ref[...] = x_ref[...] + mod_ref[0, 5:6, :] * mix

    tile = lambda w: pl.BlockSpec((TM, w), lambda i: (i, 0))
    ctile = lambda w: pl.BlockSpec((TM, w), lambda i: (i + 1, 0))
    return pl.pallas_call(
        body, name=name, grid=(n_tok // TM,),
        in_specs=[ctile(D), tile(D), tile(D), ctile(D), tile(NPOOL), ctile(2 * D),
                  pl.BlockSpec((1, 16, D), lambda i: (1, 0, 0)), _const_spec((1, HD)), _const_spec((4, 128, 128)),
                  _const_spec((1, NPOOL)), _const_spec((D, D)), _const_spec((NPOOL, D)), _const_spec((D, D))],
        out_specs=[tile(D), tile(D), tile(NPOOL), tile(D), tile(D)],
        out_shape=[jax.ShapeDtypeStruct((n_tok, D), F32), jax.ShapeDtypeStruct((n_tok, D), BF),
                   jax.ShapeDtypeStruct((n_tok, NPOOL), BF), jax.ShapeDtypeStruct((n_tok, D), BF),
                   jax.ShapeDtypeStruct((n_tok, D), BF)],
        compiler_params=_cparams("parallel"),
    )(x1, of, ob, pgate, pd, br, modv, gnw, pool_w, pscale, w_gdn, w_pool, w_mo)


def _merge_bwd(dx2, mix, of, ob, pgate, pd, br, modv, gnw, pool_w, pscale, w_gdn, w_pool, w_mo, *, name):
    n_tok = of.shape[0]

    def body(dx2_ref, mix_ref, of_ref, ob_ref, pg_ref, pd_ref, br_ref, mod_ref, gnw_ref, pw_ref, ps_ref, wg_ref, wp_ref, wmo_ref,
             do_ref, dgate_ref, dpd_ref, dbr_ref, dmix_ref, dyg_ref, dyp_ref, acc_ref, dpw_ref):
        i = pl.program_id(0)
        pgate, pdv, gnw = pg_ref[...], pd_ref[...], gnw_ref[...]
        t = _merge_parts(of_ref[...], ob_ref[...], pgate, pdv, br_ref[...], gnw, pw_ref, ps_ref[...], wg_ref, wp_ref)
        dx2v = dx2_ref[...]
        dmix = mod_ref[0, 5:6, :] * dx2v
        dmixb = dmix.astype(BF)
        dmix_ref[...] = dmixb
        dm = _nt(dmixb, wmo_ref[...])
        gp, gg = t["g_pool"], t["g_gdn"]
        dbr_ref[:, :D] = (dm * t["y_pool"] * gp * (1.0 - gp)).astype(BF)
        dbr_ref[:, D:] = (dm * t["y_gdn"] * gg * (1.0 - gg)).astype(BF)
        dyp = (dm * gp).astype(BF)
        dyg = (dm * gg).astype(BF)
        dyp_ref[...] = dyp
        dyg_ref[...] = dyg
        dyp_in = _nt(dyp, wp_ref[...])
        dypre = dyp_in * ps_ref[...]
        for g in range(4):
            gs = slice(g * 128, (g + 1) * 128)
            dpd_ref[:, gs] = _nt(dypre[:, gs], pw_ref[g])
        dog = _nt(dyg, wg_ref[...])
        dgate_ref[...] = (dog * t["on"] * (t["sig_gate"] * (1.0 + pgate - t["silu_gate"]))).astype(BF)
        don = dog * t["silu_gate"]
        dgnw = jnp.zeros((1, HD), F32)
        for h in range(NH):
            hs = slice(h * HD, (h + 1) * HD)
            donh, oh, r = don[:, hs], t["ohs"][h], t["rs"][h]
            dgnw = dgnw + jnp.sum(donh * oh, axis=0, keepdims=True)
            doh = donh * gnw
            do_ref[:, hs] = r * (doh - oh * jnp.mean(doh * oh, axis=-1, keepdims=True))

        @pl.when(i == 0)
        def _():
            acc_ref[...] = jnp.zeros_like(acc_ref)
            dpw_ref[...] = jnp.zeros_like(dpw_ref)

        acc_ref[0:1, :] += jnp.sum(dx2v * mix_ref[...].astype(F32), axis=0, keepdims=True)
        acc_ref[1:2, 0:HD] += dgnw
        acc_ref[2:3, 0:NPOOL] += jnp.sum(dyp_in * t["ypre"], axis=0, keepdims=True)
        for g in range(4):
            gs = slice(g * 128, (g + 1) * 128)
            dpw_ref[g] += _tn(pdv[:, gs], dypre[:, gs])

    tile = lambda w: pl.BlockSpec((TM, w), lambda i: (i, 0))
    ctile = lambda w: pl.BlockSpec((TM, w), lambda i: (i + 1, 0))
    return pl.pallas_call(
        body, name=name, grid=(n_tok // TM,),
        in_specs=[tile(D), tile(D), tile(D), tile(D), ctile(D), tile(NPOOL), ctile(2 * D),
                  pl.BlockSpec((1, 16, D), lambda i: (1, 0, 0)), _const_spec((1, HD)), _const_spec((4, 128, 128)),
                  _const_spec((1, NPOOL)), _const_spec((D, D)), _const_spec((NPOOL, D)), _const_spec((D, D))],
        out_specs=[tile(D), tile(D), tile(NPOOL), tile(2 * D), tile(D), tile(D), tile(D),
                   pl.BlockSpec((8, D), lambda i: (0, 0)), pl.BlockSpec((4, 128, 128), lambda i: (0, 0, 0))],
        out_shape=[jax.ShapeDtypeStruct((n_tok, D), F32), jax.ShapeDtypeStruct((n_tok, D), BF),
                   jax.ShapeDtypeStruct((n_tok, NPOOL), F32), jax.ShapeDtypeStruct((n_tok, 2 * D), BF),
                   jax.ShapeDtypeStruct((n_tok, D), BF), jax.ShapeDtypeStruct((n_tok, D), BF), jax.ShapeDtypeStruct((n_tok, D), BF),
                   jax.ShapeDtypeStruct((8, D), F32), jax.ShapeDtypeStruct((4, 128, 128), F32)],
        compiler_params=_cparams("arbitrary"),
    )(dx2, mix, of, ob, pgate, pd, br, modv, gnw, pool_w, pscale, w_gdn, w_pool, w_mo)


def _final(x3, target, fnw, *, name):
    n_tok = x3.shape[0]

    def body(x_ref, t_ref, w_ref, dx_ref, acc_ref):
        xv, w = x_ref[...], w_ref[...]
        r = lax.rsqrt(jnp.mean(xv * xv, axis=-1, keepdims=True) + EPS)
        xh = xv * r
        err = xh * w - t_ref[...]
        dy = err * (1.0 / D)
        dxh = dy * w
        dx_ref[...] = r * (dxh - xh * jnp.mean(dxh * xh, axis=-1, keepdims=True))

        @pl.when(pl.program_id(0) == 0)
        def _():
            acc_ref[...] = jnp.zeros_like(acc_ref)

        acc_ref[0:1, :] += jnp.sum(dy * xh, axis=0, keepdims=True)
        acc_ref[1:2, :] += jnp.sum(err * err, axis=0, keepdims=True) * (0.5 / D)

    tile = pl.BlockSpec((TM, D), lambda i: (i, 0))
    return pl.pallas_call(
        body, name=name, grid=(n_tok // TM,),
        in_specs=[tile, tile, _const_spec((1, D))],
        out_specs=[tile, pl.BlockSpec((8, D), lambda i: (0, 0))],
        out_shape=[jax.ShapeDtypeStruct((n_tok, D), F32), jax.ShapeDtypeStruct((8, D), F32)],
        compiler_params=_cparams("arbitrary"),
    )(x3, target, fnw)


def _split(results, n):
    return (*results[:n], list(results[n:]))


def _local_step(ctx, x, target, modv, p, late=None):
    t_lat = x.shape[0]
    n_all = t_lat + TM
    nbc, nbx = TM // TS, t_lat // TS
    mod_lat = modv[1:2]
    gather = (lambda arrs: _ChipExchange(arrs, False)) if late else (lambda arrs: None)
    scatter = (lambda arrs: _ChipExchange(arrs, True)) if late else (lambda arrs: None)

    x1, h1, gu1, f1, *got = _ffn_fwd(x, modv, p["norm1"], p["w1_in"], p["w1_out"], mrow=0, name="ffn1_fwd", ctx=ctx,
                                     exchange=gather(late and late[0]))
    if late:
        p = {**p, "w_mix": _regroup_mix(_from_chip_major_cols(got[0])), "conv": jnp.pad(_from_chip_major_cols(got[1]), ((0, 3), (0, 0)))}
    u, p_qkv, p_gate, p_pool, p_br, p_ab, *got = _mix_in_fwd(x1, modv, p["norm2"], p["w_mix"], name="mix_in_fwd",
                                                              exchange=gather(late and late[2]))
    if late:
        p = {**p, "w_gdn": got[0].reshape(D, D), "w_pool": _from_chip_major_cols(got[1]), "w_mo": got[2].reshape(D, D)}
    qkv, pre_qkv = _prep_fwd(p_qkv, p["conv"], name="prep_fwd")
    s_zero = jnp.zeros((2, NH, HD, HD), F32)
    _, _, sall_cf, sall_cb, s_ctx = _scan_fwd(qkv, p_ab, p["cst"], s_zero, row_blk0=0, nb=nbc, name="scan_ctx")
    o_f, o_b, sall_f, sall_b, _, *got = _scan_fwd(qkv, p_ab, p["cst"], s_ctx, row_blk0=nbc, nb=nbx, name="scan_lat",
                                                  exchange=gather(late and late[1]))
    if late:
        p = {**p, "w2_in": got[0], "w2_out": got[1].reshape(FF, D)}
    pd = _pool(p_pool, row0=TM, transpose=False, name="pool_fwd")
    merge_w = (modv, p["gnw"], p["pool_w"], p["pscale"], p["w_gdn"], p["w_pool"], p["w_mo"])
    x2, og, yp, m, mix = _merge_fwd(x1, o_f, o_b, p_gate, pd, p_br, *merge_w, name="merge_fwd")
    x3, h3, gu3, f3 = _ffn_fwd(x2, mod_lat, p["norm3"], p["w2_in"], p["w2_out"], mrow=6, name="ffn2_fwd")
    dx3, acc_fin = _final(x3, target, p["fnorm"], name="final")

    dx2, a3, df3, dgu3, acc3 = _ffn_bwd(dx3, x2, gu3, f3, mod_lat, p["norm3"], p["w2_in"], p["w2_out"], mrow=6,
                                        name="ffn2_bwd")
    g = {}
    tkl = _k_tile(t_lat)
    g["w2_out"] = _matmul_tn(a3, df3, tmm=FF // 2, tn=D, tk=tkl, name="ffn2_wout_grad").reshape(NCHIP, FF // NCHIP, D)
    g["w2_in"] = _matmul_tn(h3, dgu3, tmm=D, tn=2 * FF // NCHIP, tk=tkl, nsplit=NCHIP, name="ffn2_win_grad")
    do, dgate, dpd, dbr, dmix, dyg, dyp, acc_m, dpw = _merge_bwd(dx2, mix, o_f, o_b, p_gate, pd, p_br, *merge_w, name="merge_bwd")
    g["w_mo"] = _matmul_tn(m, dmix, tmm=D, tn=D, tk=tkl, name="wmo_grad").reshape(NCHIP, D // NCHIP, D)
    g["w_gdn"] = _matmul_tn(og, dyg, tmm=D, tn=D, tk=tkl, name="wgdn_grad").reshape(NCHIP, D // NCHIP, D)
    g["w_pool"] = _matmul_tn(yp, dyp, tmm=NPOOL, tn=D // NCHIP, tk=tkl, nsplit=NCHIP, name="wpool_grad")
    dpool_in = _pool(dpd, row0=0, transpose=True, name="pool_bwd")
    acc = (lax.empty((n_all, NQKV), F32), lax.empty((n_all, NQKV), F32), lax.empty((n_all, 128), F32),
           lax.empty((n_all, 128), F32), jnp.zeros((8, 128), F32))
    behind_scan = ("w2_in", "w2_out", "w_gdn", "w_pool", "w_mo")
    *acc, ds_ctx, landed = _split(_scan_bwd(qkv, p_ab, p["cst"], sall_f, sall_b, do, s_zero, *acc, row_blk0=nbc, nb=nbx, has_do=True,
                                            name="scan_lat_bwd", exchange=scatter([g[k] for k in behind_scan])), 6)
    landed = dict(zip(behind_scan, landed))
    dqkv_f, dqkv_b, dab_f, dab_b, dcst, _ = _scan_bwd(qkv, p_ab, p["cst"], sall_cf, sall_cb, jnp.zeros((8, 128), F32), ds_ctx, *acc,
                                                      row_blk0=0, nb=nbc, has_do=False, name="scan_ctx_bwd")
    dpqkv, dconv = _prep_bwd(p_qkv, pre_qkv, dqkv_f, dqkv_b, p["conv"], name="prep_bwd")
    dx1, dp, acc_mix = _mix_in_bwd(dx2, x1, dpqkv, dgate, dpool_in, dbr, dab_f, dab_b, modv, p["norm2"], p["w_mix"],
                                   name="mix_in_bwd")
    tka = _k_tile(n_all)
    g["w_mix"] = _chip_major_cols(_ungroup_mix(_matmul_tn(u, dp, tmm=256, tn=NMIXP, tk=_k_tile(n_all, 1024), name="wmix_grad")))
    dx_lat, a1, df1, dgu1, acc1, got = _split(_ffn_bwd(dx1, x, gu1, f1, modv, p["norm1"], p["w1_in"], p["w1_out"], mrow=0, ctx=ctx,
                                                   name="ffn1_bwd", exchange=scatter([g["w_mix"]])), 5)
    landed.update(zip(("w_mix",), got))
    g["w1_out"] = _matmul_tn(a1, df1, tmm=FF // 2, tn=D, tk=tka, name="ffn1_wout_grad").reshape(NCHIP, FF // NCHIP, D)
    g["w1_in"] = _matmul_tn(h1, dgu1, tmm=D, tn=2 * FF // NCHIP, tk=tka, nsplit=NCHIP, name="ffn1_win_grad")

    small = dict(norm1=acc1[0, 3] + acc1[1, 3], norm2=acc_mix[0, 3] + acc_mix[1, 3], norm3=acc3[0, 3], fnorm=acc_fin[0],
                 gnw=acc_m[1, :HD], pscale=acc_m[2, :NPOOL], pool_w=dpw, conv=dconv[:5],
                 a_log=dcst[0, 2 * NH:4 * NH], dt_bias=dcst[1, 2 * NH:4 * NH])
    zero = jnp.zeros((D,), F32)
    dmod = jnp.stack([
        jnp.stack([acc1[0, 0], acc1[0, 1], acc1[0, 2], acc_mix[0, 0], acc_mix[0, 1], zero, zero, zero, zero]),
        jnp.stack([acc1[1, 0], acc1[1, 1], acc1[1, 2], acc_mix[1, 0], acc_mix[1, 1], acc_m[0], acc3[0, 0], acc3[0, 1], acc3[0, 2]]),
    ])
    return jnp.sum(acc_fin[1]), dx_lat, g, landed, small, dmod


_HI = lax.Precision.HIGHEST


def _ada_fwd(c_all, w_sh, b_sh, *, name):
    def body(c_ref, w_ref, b_ref, o_ref):
        o_ref[...] = jnp.dot(_silu(c_ref[...]), w_ref[...], precision=_HI, preferred_element_type=F32) + b_ref[...]

    return pl.pallas_call(body, name=name, out_shape=jax.ShapeDtypeStruct((16, w_sh.shape[1]), F32),
                          compiler_params=pltpu.CompilerParams(vmem_limit_bytes=VMEM_LIMIT))(c_all, w_sh, b_sh)


def _ada_bwd(c_all, dm, w_sh, *, name):
    def body(c_ref, dm_ref, w_ref, dw_ref, dc_ref):
        sc = _silu(c_ref[...])
        dw_ref[...] = lax.dot_general(sc, dm_ref[...], (((0,), (0,)), ((), ())), precision=_HI, preferred_element_type=F32)
        part = lax.dot_general(dm_ref[8:9, :], w_ref[...], (((1,), (1,)), ((), ())), precision=_HI, preferred_element_type=F32)
        dc_ref[...] = jnp.broadcast_to(part, dc_ref.shape)

    return pl.pallas_call(body, name=name,
                          out_shape=[jax.ShapeDtypeStruct(w_sh.shape, F32), jax.ShapeDtypeStruct((8, D), F32)],
                          compiler_params=pltpu.CompilerParams(vmem_limit_bytes=VMEM_LIMIT))(c_all, dm, w_sh)


def _cctx_grad(parts, c_ctx, *, name):
    def body(p_ref, c_ref, o_ref):
        tot = (p_ref[0, 0:1, :] + p_ref[2, 0:1, :]) + (p_ref[4, 0:1, :] + p_ref[6, 0:1, :])
        o_ref[...] = tot * _dsilu(c_ref[...])

    return pl.pallas_call(body, name=name, out_shape=jax.ShapeDtypeStruct((1, D), F32))(parts, c_ctx)


_MESH = pl.DeviceIdType.MESH
_ANY = pl.BlockSpec(memory_space=pl.ANY)


def _flip(v, bit):
    return (1 - v) if bit else v


def _all_gather8(x, *, name):
    def body(x_ref, out_ref, send_sems, recv_sems, local_sem):
        mx, my, mc = lax.axis_index("x"), lax.axis_index("y"), lax.axis_index("c")
        me = 4 * mx + 2 * my + mc
        mine = pltpu.make_async_copy(x_ref, out_ref.at[me], local_sem)
        mine.start()
        sends, recvs = [], []
        for k in range(1, 8):
            px, py, pc = _flip(mx, k & 4), _flip(my, k & 2), _flip(mc, k & 1)
            sends.append(pltpu.make_async_remote_copy(src_ref=x_ref, dst_ref=out_ref.at[me], send_sem=send_sems.at[k - 1],
                                                      recv_sem=recv_sems.at[k - 1], device_id=(px, py, pc), device_id_type=_MESH))
            recvs.append(pltpu.make_async_remote_copy(src_ref=x_ref, dst_ref=out_ref.at[4 * px + 2 * py + pc],
                                                      send_sem=send_sems.at[k - 1], recv_sem=recv_sems.at[k - 1],
                                                      device_id=(px, py, pc), device_id_type=_MESH))
        for cp in sends:
            cp.start()
        for cp in recvs:
            cp.wait_recv()
        for cp in sends:
            cp.wait_send()
        mine.wait()

    vm = pl.BlockSpec(memory_space=pltpu.VMEM)
    return pl.pallas_call(
        body, name=name, out_shape=jax.ShapeDtypeStruct((8,) + x.shape, x.dtype), in_specs=[vm], out_specs=vm,
        scratch_shapes=[pltpu.SemaphoreType.DMA((7,)), pltpu.SemaphoreType.DMA((7,)), pltpu.SemaphoreType.DMA],
        compiler_params=pltpu.CompilerParams(vmem_limit_bytes=VMEM_LIMIT),
    )(x)


class _ChipExchange:
    def __init__(self, arrs, scatter):
        self.arrs, self.scatter, self.n = list(arrs), scatter, len(arrs)
        self.out_shape = [jax.ShapeDtypeStruct(a.shape if scatter else (NCHIP,) + a.shape, a.dtype) for a in self.arrs]
        links = self.n * (NCHIP - 1)
        self.scratch = [pltpu.SemaphoreType.DMA((links,)), pltpu.SemaphoreType.DMA((links,)), pltpu.SemaphoreType.DMA((self.n,))]

    def copies(self, ins, outs, send_sems, recv_sems, local_sems):
        mx, my, mc = lax.axis_index("x"), lax.axis_index("y"), lax.axis_index("c")
        me = 2 * mx + my
        local, sends, recvs = [], [], []
        for j in range(self.n):
            src_own = ins[j].at[me] if self.scatter else ins[j]
            local.append(pltpu.make_async_copy(src_own, outs[j].at[me], local_sems.at[j]))
            for k in range(1, NCHIP):
                px, py = _flip(mx, k & 2), _flip(my, k & 1)
                peer = 2 * px + py
                sem = j * (NCHIP - 1) + k - 1
                src = ins[j].at[peer] if self.scatter else ins[j]
                sends.append(pltpu.make_async_remote_copy(src_ref=src, dst_ref=outs[j].at[me], send_sem=send_sems.at[sem],
                                                          recv_sem=recv_sems.at[sem], device_id=(px, py, mc), device_id_type=_MESH))
                recvs.append(pltpu.make_async_remote_copy(src_ref=src, dst_ref=outs[j].at[peer], send_sem=send_sems.at[sem],
                                                          recv_sem=recv_sems.at[sem], device_id=(px, py, mc), device_id_type=_MESH))
        return local, sends, recvs

    @staticmethod
    def start(local, sends, recvs):
        for cp in local + sends:
            cp.start()

    @staticmethod
    def finish(local, sends, recvs):
        for cp in recvs:
            cp.wait_recv()
        for cp in sends:
            cp.wait_send()
        for cp in local:
            cp.wait()


def _gather_split(arrs, *, name):
    n = len(arrs)
    links = n * (NCHIP - 1)

    def body(*refs):
        ins, outs = refs[:n], refs[n:2 * n]
        ici_send, ici_recv, d2d_send, d2d_recv, local_sems = refs[2 * n:]
        mx, my, mc = lax.axis_index("x"), lax.axis_index("y"), lax.axis_index("c")
        me = 2 * mx + my
        local, first, arrive, onward, handed = [], [], [], [], []
        for j in range(n):
            hr = ins[j].shape[0] // 2
            mine, other = pl.ds(mc * hr, hr), pl.ds((1 - mc) * hr, hr)
            local.append(pltpu.make_async_copy(ins[j], outs[j].at[me], local_sems.at[j]))
            for k in range(1, NCHIP):
                px, py = _flip(mx, k & 2), _flip(my, k & 1)
                peer = 2 * px + py
                sem = j * (NCHIP - 1) + k - 1
                ici = lambda slot: pltpu.make_async_remote_copy(
                    src_ref=ins[j].at[mine], dst_ref=outs[j].at[slot, mine], send_sem=ici_send.at[sem], recv_sem=ici_recv.at[sem],
                    device_id=(px, py, mc), device_id_type=_MESH)
                d2d = lambda rows: pltpu.make_async_remote_copy(
                    src_ref=outs[j].at[peer, rows], dst_ref=outs[j].at[peer, rows], send_sem=d2d_send.at[sem], recv_sem=d2d_recv.at[sem],
                    device_id=(mx, my, 1 - mc), device_id_type=_MESH)
                first.append(ici(me))
                arrive.append(ici(peer))
                onward.append(d2d(mine))
                handed.append(d2d(other))
        for cp in local + first:
            cp.start()
        for got, fwd in zip(arrive, onward):
            got.wait_recv()
            fwd.start()
        for cp in handed:
            cp.wait_recv()
        for cp in first + onward:
            cp.wait_send()
        for cp in local:
            cp.wait()

    sems = pltpu.SemaphoreType.DMA((links,))
    return pl.pallas_call(
        body, name=name, out_shape=[jax.ShapeDtypeStruct((NCHIP,) + a.shape, a.dtype) for a in arrs],
        in_specs=[_ANY] * n, out_specs=[_ANY] * n, scratch_shapes=[sems, sems, sems, sems, pltpu.SemaphoreType.DMA((n,))],
    )(*arrs)


_HBM = pl.BlockSpec(memory_space=pltpu.HBM)
_SEM = pl.BlockSpec(memory_space=pltpu.SEMAPHORE)
_DATAFLOW = pltpu.SideEffectType.DATAFLOW_SIDE_EFFECTING


def _scatter_copies(ins, lands, send_sems, recv_sems):
    mx, my, mc = lax.axis_index("x"), lax.axis_index("y"), lax.axis_index("c")
    me = 2 * mx + my
    sends, recvs = [], []
    for j in range(len(ins)):
        for k in range(1, NCHIP):
            px, py = _flip(mx, k & 2), _flip(my, k & 1)
            peer = 2 * px + py
            sem = j * (NCHIP - 1) + k - 1
            mk = lambda slot: pltpu.make_async_remote_copy(src_ref=ins[j].at[peer], dst_ref=lands[j].at[slot], send_sem=send_sems.at[sem],
                                                           recv_sem=recv_sems.at[sem], device_id=(px, py, mc), device_id_type=_MESH)
            sends.append(mk(me))
            recvs.append(mk(peer))
    return sends, recvs


def _scatter_start(arrs, after, *, name):
    n = len(arrs)
    links = n * (NCHIP - 1)
    n_in = 2 * n + len(after)

    def body(*refs):
        ins, lands = refs[:n], refs[n:2 * n]
        send_sems, recv_sems = refs[n_in], refs[n_in + 1]
        token = refs[-1]
        for cp in _scatter_copies(ins, lands, send_sems, recv_sems)[0]:
            cp.start()
        token[...] = jnp.zeros_like(token)

    hbm = lambda a: pltpu.HBM(a.shape, a.dtype)
    res = pl.pallas_call(
        body, name=name,
        out_shape=(pltpu.SemaphoreType.DMA((links,)), pltpu.SemaphoreType.DMA((links,)), *[hbm(a) for a in arrs], *[hbm(a) for a in arrs],
                   jax.ShapeDtypeStruct((8, 128), F32)),
        in_specs=[_HBM] * (2 * n) + [_ANY] * len(after), out_specs=(_SEM, _SEM, *[_HBM] * (2 * n), pl.BlockSpec(memory_space=pltpu.VMEM)),
        input_output_aliases={j: 2 + j for j in range(2 * n)},
        compiler_params=pltpu.CompilerParams(has_side_effects=_DATAFLOW),
    )(*[pltpu.with_memory_space_constraint(a, pltpu.HBM) for a in arrs],
      *[pltpu.with_memory_space_constraint(lax.empty(a.shape, a.dtype), pltpu.HBM) for a in arrs], *after)
    return res[0], res[1], list(res[2:2 + n]), list(res[2 + n:2 + 2 * n]), res[-1]


def _scatter_wait(send_sems, recv_sems, arrs, lands, after, *, name):
    n = len(arrs)

    def body(*refs):
        ins, lands_in = refs[:n], refs[n:2 * n]
        sends, recvs = _scatter_copies(ins, lands_in, refs[2 * n], refs[2 * n + 1])
        for cp in sends:
            cp.wait_send()
        for cp in recvs:
            cp.wait_recv()

    hbm = lambda a: pltpu.HBM(a.shape, a.dtype)
    res = pl.pallas_call(
        body, name=name, out_shape=(*[hbm(a) for a in arrs], *[hbm(a) for a in lands]),
        in_specs=[_HBM] * (2 * n) + [_SEM, _SEM] + [_ANY] * len(after), out_specs=[_HBM] * (2 * n),
        input_output_aliases={j: j for j in range(2 * n)},
        compiler_params=pltpu.CompilerParams(has_side_effects=_DATAFLOW),
    )(*arrs, *lands, send_sems, recv_sems, *after)
    return list(res[:n]), list(res[n:])


def _pallas(body, operands, *, name, grid, in_specs, out_specs, out_shape, sem, scratch_shapes=(), aliases=None, exchange=None):
    if exchange is None:
        return pl.pallas_call(body, name=name, grid=grid, in_specs=in_specs, out_specs=out_specs, out_shape=out_shape,
                              scratch_shapes=list(scratch_shapes), input_output_aliases=aliases or {},
                              compiler_params=_cparams(*sem))(*operands)
    ex = exchange
    (steps,) = grid
    n_in, n_out, n_scr, k = len(in_specs), len(out_specs), len(scratch_shapes), ex.n

    def hosted(*refs):
        ins, refs = refs[:n_in], refs[n_in:]
        ex_in, refs = refs[:k], refs[k:]
        outs, refs = refs[:n_out], refs[n_out:]
        ex_out, refs = refs[:k], refs[k:]
        scr, ex_sems = refs[:n_scr], refs[n_scr:]
        cps = ex.copies(ex_in, ex_out, *ex_sems)
        pl.when(pl.program_id(0) == 0)(lambda: ex.start(*cps))
        body(*ins, *outs, *scr)
        pl.when(pl.program_id(0) == steps - 1)(lambda: ex.finish(*cps))

    return pl.pallas_call(
        hosted, name=name, grid=grid, in_specs=list(in_specs) + [_ANY] * k, out_specs=list(out_specs) + [_ANY] * k,
        out_shape=list(out_shape) + ex.out_shape, scratch_shapes=list(scratch_shapes) + ex.scratch,
        input_output_aliases=aliases or {}, compiler_params=_cparams("arbitrary"),
    )(*operands, *ex.arrs)


def _core_swap(arrs, *, name):
    n = len(arrs)

    def body(*refs):
        ins, outs = refs[:n], refs[n:2 * n]
        send_sems, recv_sems = refs[2 * n:]
        sib = (lax.axis_index("x"), lax.axis_index("y"), 1 - lax.axis_index("c"))
        cps = [pltpu.make_async_remote_copy(src_ref=ins[j], dst_ref=outs[j], send_sem=send_sems.at[j], recv_sem=recv_sems.at[j],
                                            device_id=sib, device_id_type=_MESH) for j in range(n)]
        for cp in cps:
            cp.start()
        for cp in cps:
            cp.wait_recv()
        for cp in cps:
            cp.wait_send()

    return pl.pallas_call(
        body, name=name, out_shape=[jax.ShapeDtypeStruct(a.shape, a.dtype) for a in arrs],
        in_specs=[_ANY] * n, out_specs=[_ANY] * n,
        scratch_shapes=[pltpu.SemaphoreType.DMA((n,)), pltpu.SemaphoreType.DMA((n,))],
    )(*arrs)


def _row_tile(rows, cols, budget=1 << 20):
    best = None
    for t in range(8, rows + 1, 8):
        if rows % t == 0 and t * cols <= budget:
            best = t
    return best or rows


def _sum_slots(x, *, name, after=()):
    ns, r, c = x.shape
    tr = _row_tile(r, c * ns)

    def body(x_ref, *refs):
        acc = x_ref[0].astype(F32)
        for s in range(1, ns):
            acc = acc + x_ref[s].astype(F32)
        refs[-1][...] = acc

    return pl.pallas_call(
        body, name=name, grid=(r // tr,), out_shape=jax.ShapeDtypeStruct((r, c), F32),
        in_specs=[pl.BlockSpec((ns, tr, c), lambda i: (0, i, 0))] + [_ANY] * len(after), out_specs=pl.BlockSpec((tr, c), lambda i: (i, 0)),
        compiler_params=_cparams("parallel"),
    )(x, *after)


def _adamw(w, ga, gb, m, v, *, name):
    r, c = w.shape
    tr = _row_tile(r, c, budget=1 << 18)
    two = gb is not None

    def body(*refs):
        w_ref, ga_ref = refs[0], refs[1]
        m_ref, v_ref = refs[2 + two], refs[3 + two]
        g_ref, d_ref, mo_ref, vo_ref = refs[4 + two:]
        g = ga_ref[...] + refs[2][...] if two else ga_ref[...]
        mn = ADAM_B1 * m_ref[...] + (1.0 - ADAM_B1) * g
        vn = ADAM_B2 * v_ref[...] + (1.0 - ADAM_B2) * (g * g)
        m_hat = mn / (1.0 - ADAM_B1 ** ADAM_STEP)
        v_hat = vn / (1.0 - ADAM_B2 ** ADAM_STEP)
        g_ref[...] = g
        d_ref[...] = -ADAM_LR * (m_hat / (jnp.sqrt(v_hat) + ADAM_EPS) + ADAM_WD * w_ref[...])
        mo_ref[...] = mn
        vo_ref[...] = vn

    spec = pl.BlockSpec((tr, c), lambda i: (i, 0))
    ins = [w, ga] + ([gb] if two else []) + [m, v]
    return pl.pallas_call(
        body, name=name, grid=(r // tr,), out_shape=[jax.ShapeDtypeStruct((r, c), F32)] * 4,
        in_specs=[spec] * len(ins), out_specs=[spec] * 4, compiler_params=_cparams("parallel"),
    )(*ins)


_MIX_AB0, _MIX_AB1 = NQKV, NQKV + 4 * NH


def _regroup_mix(w):
    pad = jnp.zeros((w.shape[0], NMIXP - NMIX), w.dtype)
    return jnp.concatenate([w[:, :_MIX_AB0], w[:, _MIX_AB1:], w[:, _MIX_AB0:_MIX_AB1], pad], axis=1)


def _ungroup_mix(w):
    n_ab = _MIX_AB1 - _MIX_AB0
    return jnp.concatenate([w[:, :_MIX_AB0], w[:, NMIX - n_ab:NMIX], w[:, _MIX_AB0:NMIX - n_ab]], axis=1)


def _chip_major_cols(w):
    r, c = w.shape
    return w.reshape(r, NCHIP, c // NCHIP).transpose(1, 0, 2)


def _from_chip_major_cols(w):
    return w.transpose(1, 0, 2).reshape(w.shape[1], -1)


_SMALL = (("c_ctx", D), ("b_ada", 9 * D), ("norm1_w", D), ("norm2_w", D), ("norm3_w", D), ("final_norm_w", D),
          ("a_log", 2 * NH), ("dt_bias", 2 * NH), ("gdn_norm_w", HD), ("pool_w", 4 * 128 * 128), ("pool_scale", NPOOL),
          ("conv_w", 5 * NQKV // NCHIP))


def _pack(vals, lanes=128, row_mult=8):
    flat = jnp.concatenate([jnp.ravel(v) for v in vals])
    n = flat.shape[0]
    rows = -(-n // (lanes * row_mult)) * row_mult
    return jnp.pad(flat, (0, rows * lanes - n)).reshape(rows, lanes)


def _unpack(packed, sizes):
    flat = packed.reshape(-1)
    out, o = [], 0
    for n in sizes:
        out.append(flat[o:o + n])
        o += n
    return out


def kernel(x, c, ctx, c_ctx, w_ada, b_ada, norm1_w, ffn1_w_in, ffn1_w_out, norm2_w, w_mix_in, conv_w, a_log, dt_bias, gdn_norm_w, w_gdn_proj, pool_w, pool_scale, w_pool_proj, w_mix_out, norm3_w, ffn2_w_in, ffn2_w_out, final_norm_w, loss_target, m_c_ctx, m_w_ada, m_b_ada, m_norm1_w, m_ffn1_w_in, m_ffn1_w_out, m_norm2_w, m_w_mix_in, m_conv_w, m_a_log, m_dt_bias, m_gdn_norm_w, m_w_gdn_proj, m_pool_w, m_pool_scale, m_w_pool_proj, m_w_mix_out, m_norm3_w, m_ffn2_w_in, m_ffn2_w_out, m_final_norm_w, v_c_ctx, v_w_ada, v_b_ada, v_norm1_w, v_ffn1_w_in, v_ffn1_w_out, v_norm2_w, v_w_mix_in, v_conv_w, v_a_log, v_dt_bias, v_gdn_norm_w, v_w_gdn_proj, v_pool_w, v_pool_scale, v_w_pool_proj, v_w_mix_out, v_norm3_w, v_ffn2_w_in, v_ffn2_w_out, v_final_norm_w):
    names = ("c_ctx", "w_ada", "b_ada", "norm1_w", "ffn1_w_in", "ffn1_w_out", "norm2_w", "w_mix_in", "conv_w", "a_log", "dt_bias",
             "gdn_norm_w", "w_gdn_proj", "pool_w", "pool_scale", "w_pool_proj", "w_mix_out", "norm3_w", "ffn2_w_in", "ffn2_w_out",
             "final_norm_w")
    w = dict(zip(names, (c_ctx, w_ada, b_ada, norm1_w, ffn1_w_in, ffn1_w_out, norm2_w, w_mix_in, conv_w, a_log, dt_bias, gdn_norm_w,
                         w_gdn_proj, pool_w, pool_scale, w_pool_proj, w_mix_out, norm3_w, ffn2_w_in, ffn2_w_out, final_norm_w)))
    mom = dict(zip(names, (m_c_ctx, m_w_ada, m_b_ada, m_norm1_w, m_ffn1_w_in, m_ffn1_w_out, m_norm2_w, m_w_mix_in, m_conv_w, m_a_log,
                           m_dt_bias, m_gdn_norm_w, m_w_gdn_proj, m_pool_w, m_pool_scale, m_w_pool_proj, m_w_mix_out, m_norm3_w,
                           m_ffn2_w_in, m_ffn2_w_out, m_final_norm_w)))
    var = dict(zip(names, (v_c_ctx, v_w_ada, v_b_ada, v_norm1_w, v_ffn1_w_in, v_ffn1_w_out, v_norm2_w, v_w_mix_in, v_conv_w, v_a_log,
                           v_dt_bias, v_gdn_norm_w, v_w_gdn_proj, v_pool_w, v_pool_scale, v_w_pool_proj, v_w_mix_out, v_norm3_w,
                           v_ffn2_w_in, v_ffn2_w_out, v_final_norm_w)))
    mx, my, mc = lax.axis_index("x"), lax.axis_index("y"), lax.axis_index("c")
    chip = 2 * mx + my
    dev = 2 * chip + mc
    ada_cols = w_ada.shape[2]

    c_rows = _all_gather8(jnp.pad(c, ((0, 7), (0, 0))), name="gather_c")[:, 0, :]
    c_all = jnp.concatenate([c_rows, c_ctx[None], jnp.zeros((7, D), F32)], axis=0)
    b_sh = lax.dynamic_slice(b_ada, (0, chip * ada_cols), (1, ada_cols))
    mod_sh = _ada_fwd(c_all, w_ada[0], b_sh, name="ada_fwd")
    mod_parts = _all_gather8(mod_sh, name="gather_mod")
    mod_all = jnp.concatenate([mod_parts[2 * s] for s in range(NCHIP)], axis=1)
    mod_lat = lax.dynamic_index_in_dim(mod_all, dev, axis=0, keepdims=False).reshape(9, D)
    modv = jnp.zeros((2, 16, D), F32).at[0, :9].set(mod_all[8].reshape(9, D)).at[1, :9].set(mod_lat)

    big = ("ffn1_w_in", "ffn1_w_out", "w_mix_in", "w_gdn_proj", "w_pool_proj", "w_mix_out", "ffn2_w_in", "ffn2_w_out")
    shard = {k: w[k][0].astype(BF) for k in big}
    w1_in, w1_out = _gather_split([shard["ffn1_w_in"], shard["ffn1_w_out"]], name="gather_ffn1")
    p = dict(
        norm1=norm1_w, norm2=norm2_w, norm3=norm3_w, fnorm=final_norm_w[None], w1_in=w1_in, w1_out=w1_out.reshape(FF, D),
        cst=jnp.zeros((8, 128), F32).at[0, 2 * NH:4 * NH].set(jnp.exp(a_log).reshape(-1)).at[1, 2 * NH:4 * NH].set(dt_bias.reshape(-1)),
        gnw=gdn_norm_w, pool_w=pool_w[0], pscale=pool_scale)
    late = ([shard["w_mix_in"], conv_w[0]], [shard["ffn2_w_in"], shard["ffn2_w_out"]],
            [shard["w_gdn_proj"], shard["w_pool_proj"], shard["w_mix_out"]])

    loss_dev, dx_lat, g, landed, small, dmod = _local_step(ctx[0], x[0], loss_target[0], modv, p, late)
    loss = lax.psum(loss_dev, ("x", "y", "c"))
    grad_x = dx_lat[None]

    small_vals = [dmod[1], dmod[0], small["norm1"], small["norm2"], small["norm3"], small["fnorm"], small["a_log"], small["dt_bias"],
                  small["gnw"], small["pool_w"], small["pscale"], small["conv"]]
    small_sizes = [v.size for v in small_vals]
    packed = _all_gather8(_pack(small_vals), name="gather_small")
    tot = _unpack(_sum_slots(packed, name="sum_small"), small_sizes)
    dmod_lat_all = packed[:, :9 * D // 128, :].reshape(8, 9 * D)
    dm = jnp.concatenate([dmod_lat_all, tot[1][None], jnp.zeros((7, 9 * D), F32)], axis=0)
    dm_sh = lax.dynamic_slice(dm, (0, chip * ada_cols), (16, ada_cols))
    g_w_ada, cctx_part = _ada_bwd(c_all, dm_sh, w_ada[0], name="ada_bwd")
    g_c_ctx = _cctx_grad(_all_gather8(cctx_part, name="gather_cctx"), c_ctx[None], name="cctx_grad")[0]
    conv_tot = tot[11].reshape(5, NQKV)
    g_small = dict(c_ctx=g_c_ctx, b_ada=tot[0] + tot[1], norm1_w=tot[2], norm2_w=tot[3], norm3_w=tot[4], final_norm_w=tot[5],
                   a_log=tot[6], dt_bias=tot[7], gdn_norm_w=tot[8], pool_w=tot[9], pool_scale=tot[10],
                   conv_w=lax.dynamic_slice(conv_tot, (0, chip * (NQKV // NCHIP)), (5, NQKV // NCHIP)))

    first = ("ffn1_w_in", "ffn1_w_out")
    order = dict(zip(big, ("w1_in", "w1_out", "w_mix", "w_gdn", "w_pool", "w_mo", "w2_in", "w2_out")))
    rest = [k for k in big if k not in first]
    send_sems, recv_sems, sent, lands, token = _scatter_start([g["w1_in"], g["w1_out"]], [g_c_ctx, g_w_ada], name="scatter_ffn1_start")
    mine = {k: _sum_slots(landed[order[k]], name=f"sum_{k}", after=[token]) for k in rest}
    theirs = dict(zip(rest, _core_swap([mine[k] for k in rest], name="swap_grad_sums")))

    out = {}
    as2d = lambda a: a.reshape(-1, a.shape[-1])

    def update(k):
        res = _adamw(as2d(w[k]), as2d(mine[k]), as2d(theirs[k]), as2d(mom[k]), as2d(var[k]), name=f"adamw_{k}")
        out[k] = [r.reshape(w[k].shape) for r in res]

    for k in rest:
        update(k)
    out["w_ada"] = [r.reshape(w_ada.shape) for r in _adamw(w_ada[0], g_w_ada, None, m_w_ada[0], v_w_ada[0], name="adamw_w_ada")]
    sm_names = [n for n, _ in _SMALL]
    sm_sizes = [n for _, n in _SMALL]
    res = _adamw(_pack([w[k] for k in sm_names]), _pack([g_small[k] for k in sm_names]), None,
                 _pack([mom[k] for k in sm_names]), _pack([var[k] for k in sm_names]), name="adamw_small")
    done = [out[k][1] for k in rest] + [out["w_ada"][1], res[1]]
    res = [_unpack(r, sm_sizes) for r in res]
    for i, k in enumerate(sm_names):
        out[k] = [r[i].reshape(w[k].shape) for r in res]
    sent, lands = _scatter_wait(send_sems, recv_sems, sent, lands, done, name="scatter_ffn1_wait")
    for k, part, land in zip(first, sent, lands):
        own = lax.dynamic_slice_in_dim(part, chip, 1, axis=0)
        mine[k] = _sum_slots(lax.dynamic_update_slice_in_dim(land, own, chip, axis=0), name=f"sum_{k}")
    theirs.update(zip(first, _core_swap([mine[k] for k in first], name="swap_ffn1_sums")))
    for k in first:
        update(k)
    return (loss, grad_x, *[out[k][0] for k in names], *[out[k][1] for k in names], *[out[k][2] for k in names],
            *[out[k][3] for k in names])
```

```python
import functools

import jax
import jax.numpy as jnp
from jax import lax
from jax.experimental import pallas as pl
from jax.experimental.pallas import tpu as pltpu

F32 = jnp.float32
BF = jnp.bfloat16

D = 1024
FF = 2816
NH = 8
HD = 128
CH = 64
GW = 64
TM = 256
NQKV = 3 * NH * HD
NPOOL = 512
POOL_WINDOWS = (2, 4, 8, 16)
NMIX = 6688
NMIXP = 6784
EPS = 1e-6
NCHIP = 4
VMEM_LIMIT = 56 * 1024 * 1024

ADAM_LR, ADAM_B1, ADAM_B2, ADAM_EPS, ADAM_WD, ADAM_STEP = 0.001, 0.9, 0.999, 1e-08, 0.01, 10


def _cparams(*sem):
    return pltpu.CompilerParams(dimension_semantics=sem, vmem_limit_bytes=VMEM_LIMIT)


def _const_spec(shape):
    nd = len(shape)
    return pl.BlockSpec(shape, lambda *_: (0,) * nd, pipeline_mode=pl.Buffered(1))


def _dot(a, b, dims):
    return lax.dot_general(a.astype(BF), b.astype(BF), (dims, ((), ())), preferred_element_type=F32)


def _nn(a, b):
    return _dot(a, b, ((1,), (0,)))


def _nt(a, b):
    return _dot(a, b, ((1,), (1,)))


def _tn(a, b):
    return _dot(a, b, ((0,), (0,)))


def _silu(x):
    return x * jax.nn.sigmoid(x)


def _dsilu(x):
    s = jax.nn.sigmoid(x)
    return s * (1.0 + x * (1.0 - s))


def _norm_mod(x, nw, shift, scale):
    r = lax.rsqrt(jnp.mean(x * x, axis=-1, keepdims=True) + EPS)
    xh = x * r
    n = xh * nw
    return n * (1.0 + scale) + shift, n, xh, r


def _norm_mod_bwd(dh, n, xh, r, nw, scale):
    dn = dh * (1.0 + scale)
    dxh = dn * nw
    dx = r * (dxh - xh * jnp.mean(dxh * xh, axis=-1, keepdims=True))
    rs = lambda t: jnp.sum(t, axis=0, keepdims=True)
    return dx, rs(dh), rs(dh * n), rs(dn * xh)


def _stream_specs(ctx):
    if ctx is None:
        return [pl.BlockSpec((TM, D), lambda i: (i, 0))]
    return [pl.BlockSpec((TM, D), lambda i: (0, 0)), pl.BlockSpec((TM, D), lambda i: (jnp.maximum(i - 1, 0), 0))]


def _stream_tile(refs, lead):
    if not lead:
        return refs[0][...], refs[1:]
    return jnp.where(pl.program_id(0) == 0, refs[0][...], refs[1][...]), refs[2:]


def _loss_head(xv, target, w):
    r = lax.rsqrt(jnp.mean(xv * xv, axis=-1, keepdims=True) + EPS)
    xh = xv * r
    err = xh * w - target
    dy = err * (1.0 / D)
    dxh = dy * w
    dx = r * (dxh - xh * jnp.mean(dxh * xh, axis=-1, keepdims=True))
    return dx, jnp.sum(dy * xh, axis=0, keepdims=True), jnp.sum(err * err, axis=0, keepdims=True) * (0.5 / D)


def _ffn_fwd(x, modv, nw, w_in4, w_out, *, mrow, name, ctx=None, exchange=None, head=None):
    lead = ctx is not None
    n_tok = x.shape[0] + (TM if lead else 0)
    nt = n_tok // TM
    nset = modv.shape[0]
    ws = w_in4.shape[2]

    def body(*refs):
        xv, refs = _stream_tile(refs, lead)
        if head:
            (t_ref, fw_ref), refs, acc_ref = refs[:2], refs[2:-1], refs[-1]
        mod_ref, nw_ref, win_ref, wout_ref, x1_ref, h_ref, gu_ref, f_ref = refs
        shift, scale, gate = mod_ref[0, mrow:mrow + 1, :], mod_ref[0, mrow + 1:mrow + 2, :], mod_ref[0, mrow + 2:mrow + 3, :]
        h, _, _, _ = _norm_mod(xv, nw_ref[...], shift, scale)
        hb = h.astype(BF)
        h_ref[...] = hb
        gus = [_nn(hb, win_ref[s]) for s in range(NCHIP)]
        for s in range(NCHIP):
            gu_ref[:, s * ws:(s + 1) * ws] = gus[s].astype(BF)
        g = jnp.concatenate(gus[:2], axis=1)
        u = jnp.concatenate(gus[2:], axis=1)
        f = _nn(_silu(g) * u, wout_ref[...])
        f_ref[...] = f.astype(BF)
        out = xv + 0.5 * gate * f
        if not head:
            x1_ref[...] = out
            return
        x1_ref[...], dfw, loss = _loss_head(out, t_ref[...], fw_ref[...])

        @pl.when(pl.program_id(0) == 0)
        def _():
            acc_ref[...] = jnp.zeros_like(acc_ref)

        acc_ref[0:1, :] += dfw
        acc_ref[1:2, :] += loss

    tile = lambda w: pl.BlockSpec((TM, w), lambda i: (i, 0))
    return _pallas(
        body, (*([ctx] if lead else []), x, *(head or ()), modv, nw, w_in4, w_out), name=name, grid=(nt,),
        sem=("arbitrary",) if head else ("parallel",), exchange=exchange,
        in_specs=_stream_specs(ctx) + ([tile(D), _const_spec((1, D))] if head else [])
        + [pl.BlockSpec((1, 16, D), lambda i: (jnp.minimum(i, nset - 1), 0, 0)), _const_spec((1, D)),
           _const_spec(w_in4.shape), _const_spec(w_out.shape)],
        out_specs=[tile(D), tile(D), tile(2 * FF), tile(D)] + ([pl.BlockSpec((8, D), lambda i: (0, 0))] if head else []),
        out_shape=[jax.ShapeDtypeStruct((n_tok, D), F32), jax.ShapeDtypeStruct((n_tok, D), BF),
                   jax.ShapeDtypeStruct((n_tok, 2 * FF), BF), jax.ShapeDtypeStruct((n_tok, D), BF)]
        + ([jax.ShapeDtypeStruct((8, D), F32)] if head else []))


def _ffn_bwd(dxo, x, gu, fo, modv, nw, w_in4, w_out, *, mrow, name, ctx=None, exchange=None):
    lead = ctx is not None
    dx_skip = int(lead)
    n_tok = x.shape[0] + dx_skip * TM
    nt = n_tok // TM
    nset = modv.shape[0]
    ws = w_in4.shape[2]

    def body(*refs):
        xv, (dxo_ref, gu_ref, f_ref, mod_ref, nw_ref, win_ref, wout_ref, dx_ref, a_ref, df_ref, dgu_ref, acc_ref) = _stream_tile(refs, lead)
        i = pl.program_id(0)
        dxo_v = dxo_ref[...]
        shift, scale, gate = mod_ref[0, mrow:mrow + 1, :], mod_ref[0, mrow + 1:mrow + 2, :], mod_ref[0, mrow + 2:mrow + 3, :]
        _, n, xh, r = _norm_mod(xv, nw_ref[...], shift, scale)
        df = 0.5 * gate * dxo_v
        dfb = df.astype(BF)
        df_ref[...] = dfb
        dgate = jnp.sum(0.5 * dxo_v * f_ref[...].astype(F32), axis=0, keepdims=True)
        da = _nt(dfb, wout_ref[...])
        g = gu_ref[:, :FF].astype(F32)
        u = gu_ref[:, FF:].astype(F32)
        sig = jax.nn.sigmoid(g)
        sg = g * sig
        a_ref[...] = (sg * u).astype(BF)
        dgu_ref[:, :FF] = (da * u * (sig * (1.0 + g - sg))).astype(BF)
        dgu_ref[:, FF:] = (da * sg).astype(BF)
        dh = _nt(dgu_ref[:, 0:ws], win_ref[0])
        for s in range(1, NCHIP):
            dh = dh + _nt(dgu_ref[:, s * ws:(s + 1) * ws], win_ref[s])
        dx, dshift, dscale, dnw = _norm_mod_bwd(dh, n, xh, r, nw_ref[...], scale)
        dx_ref[...] = dxo_v + dx

        @pl.when((i == 0) | (i == nset - 1))
        def _():
            acc_ref[...] = jnp.zeros_like(acc_ref)

        acc_ref[0, 0:1, :] += dshift
        acc_ref[0, 1:2, :] += dscale
        acc_ref[0, 2:3, :] += dgate
        acc_ref[0, 3:4, :] += dnw

    tile = lambda w: pl.BlockSpec((TM, w), lambda i: (i, 0))
    return _pallas(
        body, (*([ctx] if lead else []), x, dxo, gu, fo, modv, nw, w_in4, w_out), name=name, grid=(nt,), sem=("arbitrary",),
        exchange=exchange,
        in_specs=_stream_specs(ctx) + [tile(D), tile(2 * FF), tile(D),
                                       pl.BlockSpec((1, 16, D), lambda i: (jnp.minimum(i, nset - 1), 0, 0)), _const_spec((1, D)),
                                       _const_spec(w_in4.shape), _const_spec(w_out.shape)],
        out_specs=[pl.BlockSpec((TM, D), lambda i: (jnp.maximum(i - dx_skip, 0), 0)), tile(FF), tile(D), tile(2 * FF),
                   pl.BlockSpec((1, 8, D), lambda i: (jnp.minimum(i, nset - 1), 0, 0))],
        out_shape=[jax.ShapeDtypeStruct((n_tok - dx_skip * TM, D), F32), jax.ShapeDtypeStruct((n_tok, FF), BF),
                   jax.ShapeDtypeStruct((n_tok, D), BF), jax.ShapeDtypeStruct((n_tok, 2 * FF), BF),
                   jax.ShapeDtypeStruct((nset, 8, D), F32)])


def _k_tile(n, target=3072):
    return max(t for t in range(TM, min(n, target) + 1, TM) if n % t == 0)


def _matmul_tn(a, b, *, tmm, tn, tk, nsplit=1, name):
    n_tok, m = a.shape
    kk = b.shape[1]
    nk = n_tok // tk

    def body(a_ref, b_ref, o_ref, acc):
        k = pl.program_id(2)

        @pl.when(k == 0)
        def _():
            acc[...] = jnp.zeros_like(acc)

        acc[...] += _tn(a_ref[...], b_ref[...])

        @pl.when(k == nk - 1)
        def _():
            o_ref[...] = acc[...].astype(BF).reshape(o_ref.shape)

    if nsplit == 1:
        out_shape = jax.ShapeDtypeStruct((m, kk), BF)
        out_spec = pl.BlockSpec((tmm, tn), lambda i, j, k: (i, j))
    else:
        assert tn == kk // nsplit
        out_shape = jax.ShapeDtypeStruct((nsplit, m, tn), BF)
        out_spec = pl.BlockSpec((1, tmm, tn), lambda i, j, k: (j, i, 0))
    return pl.pallas_call(
        body, name=name, grid=(m // tmm, kk // tn, nk),
        in_specs=[pl.BlockSpec((tk, tmm), lambda i, j, k: (k, i)), pl.BlockSpec((tk, tn), lambda i, j, k: (k, j))],
        out_specs=out_spec, out_shape=out_shape,
        scratch_shapes=[pltpu.VMEM((tmm, tn), F32)],
        compiler_params=_cparams("parallel", "parallel", "arbitrary"),
    )(a, b)


_MIX_PARTS = (("qkv", 0, NQKV), ("gate", NQKV, 1024), ("pool", NQKV + 1024, NPOOL), ("br", NQKV + 1024 + NPOOL, 2048),
              ("ab", NMIXP - 128, 128))


def _mix_in_fwd(x1, modv, nw, w_mix, *, name, exchange=None):
    n_tok = x1.shape[0]

    def body(x_ref, mod_ref, nw_ref, w_ref, u_ref, *p_refs):
        u, _, _, _ = _norm_mod(x_ref[...], nw_ref[...], mod_ref[0, 3:4, :], mod_ref[0, 4:5, :])
        ub = u.astype(BF)
        u_ref[...] = ub
        for (_, c0, w), p_ref in zip(_MIX_PARTS, p_refs):
            p_ref[...] = _nn(ub, w_ref[:, c0:c0 + w])

    tile = lambda w: pl.BlockSpec((TM, w), lambda i: (i, 0))
    return _pallas(
        body, (x1, modv, nw, w_mix), name=name, grid=(n_tok // TM,), sem=("parallel",), exchange=exchange,
        in_specs=[tile(D), pl.BlockSpec((1, 16, D), lambda i: (jnp.minimum(i, 1), 0, 0)), _const_spec((1, D)),
                  _const_spec(w_mix.shape)],
        out_specs=[tile(D)] + [tile(w) for _, _, w in _MIX_PARTS],
        out_shape=[jax.ShapeDtypeStruct((n_tok, D), BF)] + [jax.ShapeDtypeStruct((n_tok, w), F32) for _, _, w in _MIX_PARTS])


def _mix_in_bwd(dxo, x1, dqkv, dgate, dpool, dbr, dab_f, dab_b, modv, nw, w_mix, *, name):
    n_tok = x1.shape[0]

    def body(dxo_ref, x_ref, dqkv_ref, dgate_ref, dpool_ref, dbr_ref, dabf_ref, dabb_ref, mod_ref, nw_ref, w_ref,
             dx_ref, dp_ref, acc_ref):
        i = pl.program_id(0)
        lat = i >= 1
        scale = mod_ref[0, 4:5, :]
        _, n, xh, r = _norm_mod(x_ref[...], nw_ref[...], mod_ref[0, 3:4, :], scale)
        dp_ref[:, 0:NQKV] = dqkv_ref[...].astype(BF)
        dp_ref[:, NQKV:NQKV + 1024] = jnp.where(lat, dgate_ref[...], 0.0).astype(BF)
        dp_ref[:, NQKV + 1024:NQKV + 1536] = jnp.where(lat, dpool_ref[...], 0.0).astype(BF)
        dp_ref[:, NQKV + 1536:NMIXP - 128] = jnp.where(lat, dbr_ref[...], 0.0).astype(BF)
        dp_ref[:, NMIXP - 128:] = (dabf_ref[...] + dabb_ref[...]).astype(BF)
        du = _nt(dp_ref[...], w_ref[...])
        dx, dshift, dscale, dnw = _norm_mod_bwd(du, n, xh, r, nw_ref[...], scale)
        dx_ref[...] = jnp.where(lat, dxo_ref[...], 0.0) + dx

        @pl.when(i <= 1)
        def _():
            acc_ref[...] = jnp.zeros_like(acc_ref)

        acc_ref[0, 0:1, :] += dshift
        acc_ref[0, 1:2, :] += dscale
        acc_ref[0, 3:4, :] += dnw

    tile = lambda w: pl.BlockSpec((TM, w), lambda i: (i, 0))
    ltile = lambda w: pl.BlockSpec((TM, w), lambda i: (jnp.maximum(i - 1, 0), 0))
    return pl.pallas_call(
        body, name=name, grid=(n_tok // TM,),
        in_specs=[ltile(D), tile(D), tile(NQKV), ltile(1024), ltile(NPOOL), ltile(2048), tile(128), tile(128),
                  pl.BlockSpec((1, 16, D), lambda i: (jnp.minimum(i, 1), 0, 0)), _const_spec((1, D)), _const_spec(w_mix.shape)],
        out_specs=[tile(D), tile(NMIXP), pl.BlockSpec((1, 8, D), lambda i: (jnp.minimum(i, 1), 0, 0))],
        out_shape=[jax.ShapeDtypeStruct((n_tok, D), F32), jax.ShapeDtypeStruct((n_tok, NMIXP), BF),
                   jax.ShapeDtypeStruct((2, 8, D), F32)],
        compiler_params=_cparams("arbitrary"),
    )(dxo, x1, dqkv, dgate, dpool, dbr, dab_f, dab_b, modv, nw, w_mix)


def _qkv_act(pre, part):
    s = _silu(pre)
    if part == 2:
        return s
    nrm = s * lax.rsqrt(jnp.sum(s * s, axis=-1, keepdims=True) + EPS)
    return nrm * HD ** -0.5 if part == 0 else nrm


def _halo_specs(nt):
    r = TM // 8
    main = pl.BlockSpec((TM, NQKV), lambda i: (i, 0))
    prev = pl.BlockSpec((8, NQKV), lambda i: (jnp.maximum(i * r - 1, 0), 0))
    nxt = pl.BlockSpec((8, NQKV), lambda i: (jnp.minimum((i + 1) * r, nt * r - 1), 0))
    return main, prev, nxt


def _prep_fwd(p_qkv, conv_w8, *, name):
    n_tok = p_qkv.shape[0]
    nt = n_tok // TM

    def body(x_ref, xp_ref, xn_ref, w_ref, o_ref, pre_ref, win):
        i = pl.program_id(0)
        has_prev = (i != 0) & (i != 1)
        has_next = (i != 0) & (i != nt - 1)
        win[0:8, :] = jnp.where(has_prev, xp_ref[...], 0.0)
        win[8:8 + TM, :] = x_ref[...]
        win[8 + TM:, :] = jnp.where(has_next, xn_ref[...], 0.0)
        for hb in range(3 * NH):
            hs = slice(hb * HD, (hb + 1) * HD)
            pre = win[6:6 + TM, hs] * w_ref[0:1, hs]
            for k in range(1, 5):
                pre = pre + win[6 + k:6 + k + TM, hs] * w_ref[k:k + 1, hs]
            pre_ref[:, hs] = pre
            o_ref[:, hs] = _qkv_act(pre, hb // NH)

    main, prev, nxt = _halo_specs(nt)
    return pl.pallas_call(
        body, name=name, grid=(nt,),
        in_specs=[main, prev, nxt, pl.BlockSpec((8, NQKV), lambda i: (0, 0))],
        out_specs=[main, main], out_shape=[jax.ShapeDtypeStruct((n_tok, NQKV), F32)] * 2,
        scratch_shapes=[pltpu.VMEM((TM + 16, NQKV), F32)],
        compiler_params=_cparams("parallel"),
    )(p_qkv, p_qkv, p_qkv, conv_w8)


def _prep_bwd(p_qkv, pre, dqkv_f, dqkv_b, conv_w8, *, name):
    n_tok = p_qkv.shape[0]
    nt = n_tok // TM

    def body(x_ref, p_ref, pp_ref, pn_ref, g_ref, gp_ref, gn_ref, g2_ref, g2p_ref, g2n_ref, w_ref, dx_ref, dw_ref, pwin, gwin, dwin):
        i = pl.program_id(0)
        has_prev = (i != 0) & (i != 1)
        has_next = (i != 0) & (i != nt - 1)
        pwin[0:8, :] = jnp.where(has_prev, pp_ref[...], 0.0)
        pwin[8:8 + TM, :] = p_ref[...]
        pwin[8 + TM:, :] = jnp.where(has_next, pn_ref[...], 0.0)
        gwin[0:8, :] = jnp.where(has_prev, gp_ref[...] + g2p_ref[...], 0.0)
        gwin[8:8 + TM, :] = g_ref[...] + g2_ref[...]
        gwin[8 + TM:, :] = jnp.where(has_next, gn_ref[...] + g2n_ref[...], 0.0)

        @pl.when(i == 0)
        def _():
            dw_ref[...] = jnp.zeros_like(dw_ref)

        for hb in range(3 * NH):
            hs = slice(hb * HD, (hb + 1) * HD)
            _, vjp = jax.vjp(functools.partial(_qkv_act, part=hb // NH), pwin[:, hs])
            dwin[:, hs] = vjp(gwin[:, hs])[0]
            xv = x_ref[:, hs]
            dx = None
            for k in range(5):
                sh = dwin[10 - k:10 - k + TM, hs]
                dx = sh * w_ref[k:k + 1, hs] if dx is None else dx + sh * w_ref[k:k + 1, hs]
                dw_ref[k:k + 1, hs] += jnp.sum(sh * xv, axis=0, keepdims=True)
            dx_ref[:, hs] = dx.astype(BF)

    main, prev, nxt = _halo_specs(nt)
    wspec = pl.BlockSpec((8, NQKV), lambda i: (0, 0))
    return pl.pallas_call(
        body, name=name, grid=(nt,),
        in_specs=[main, main, prev, nxt, main, prev, nxt, main, prev, nxt, wspec],
        out_specs=[main, wspec],
        out_shape=[jax.ShapeDtypeStruct((n_tok, NQKV), BF), jax.ShapeDtypeStruct((8, NQKV), F32)],
        scratch_shapes=[pltpu.VMEM((TM + 16, NQKV), F32)] * 3,
        compiler_params=_cparams("arbitrary"),
    )(p_qkv, pre, pre, pre, dqkv_f, dqkv_f, dqkv_f, dqkv_b, dqkv_b, dqkv_b, conv_w8)


@jax.custom_vjp
def _mm_nn(a, b):
    return _nn(a, b)


@jax.custom_vjp
def _mm_nt(a, b):
    return _nt(a, b)


@jax.custom_vjp
def _mm_tn(a, b):
    return _tn(a, b)


_mm_nn.defvjp(lambda a, b: (_nn(a, b), (a, b)), lambda r, g: (_mm_nt(g, r[1]), _mm_tn(r[0], g)))
_mm_nt.defvjp(lambda a, b: (_nt(a, b), (a, b)), lambda r, g: (_mm_nn(g, r[1]), _mm_tn(g, r[0])))
_mm_tn.defvjp(lambda a, b: (_tn(a, b), (a, b)), lambda r, g: (_mm_nt(r[1], g), _mm_nn(r[0], g)))


def _each(f, *lists):
    return tuple(f(*a) for a in zip(*lists))


def _unit_tri_inv(ls, revs):
    ii = lax.broadcasted_iota(jnp.int32, (CH, CH), 0)
    jj = lax.broadcasted_iota(jnp.int32, (CH, CH), 1)
    eye = (ii == jj).astype(F32)
    xs = None
    s = 1
    while s < CH:
        same = (ii & -(2 * s)) == (jj & -(2 * s))
        off = {False: same & ((ii & s) != 0) & ((jj & s) == 0), True: same & ((jj & s) != 0) & ((ii & s) == 0)}
        cs = _each(lambda l, r: jnp.where(off[r], l, 0.0), ls, revs)
        if xs is None:
            xs = _each(lambda c: eye - c, cs)
        else:
            xc = _each(_nn, xs, cs)
            xcx = _each(_nn, xc, xs)
            xs = _each(lambda x, t: x - t, xs, xcx)
        s *= 2
    return xs


@functools.lru_cache(maxsize=None)
def _tri_solve(revs):
    @jax.custom_vjp
    def solve(ls, rhss):
        return _each(_mm_nn, _unit_tri_inv(ls, revs), rhss)

    def fwd(ls, rhss):
        ainv = _unit_tri_inv(ls, revs)
        xs = _each(_mm_nn, ainv, rhss)
        return xs, (ainv, xs)

    def bwd(res, gs):
        ainv, xs = res
        drhs = _each(_mm_tn, ainv, gs)
        return _each(lambda d, x: -_mm_nt(d, x), drhs, xs), drhs

    solve.defvjp(fwd, bwd)
    return solve


def _chunk_prep(q, k, v, beta, g, *, revs):
    ii = lax.broadcasted_iota(jnp.int32, (CH, CH), 0)
    jj = lax.broadcasted_iota(jnp.int32, (CH, CH), 1)
    eye = ii == jj
    incl_of = {False: ii >= jj, True: ii <= jj}
    strict_of = {False: ii > jj, True: ii < jj}
    g_row = _each(lambda t: jnp.sum(jnp.where(eye, t, 0.0), axis=0, keepdims=True), g)
    cum = _each(lambda t, r: jnp.sum(jnp.where(incl_of[r], t, 0.0), axis=1, keepdims=True), g_row, revs)
    cum_row = _each(lambda t: jnp.sum(jnp.where(eye, t, 0.0), axis=0, keepdims=True), cum)
    total = _each(lambda t: jnp.sum(t, axis=0, keepdims=True), g)
    decay = _each(lambda c, cr, r: jnp.where(incl_of[r], jnp.exp(jnp.where(incl_of[r], c - cr, 0.0)), 0.0), cum, cum_row, revs)
    kb = _each(jnp.multiply, k, beta)
    vb = _each(jnp.multiply, v, beta)
    kk = _each(_mm_nt, kb, k)
    lmat = _each(lambda t, dc, r: jnp.where(strict_of[r], t * dc, 0.0), kk, decay, revs)
    ecum = _each(jnp.exp, cum)
    rhs = _each(lambda a, b, e: jnp.concatenate([a, b * e], axis=1), vb, kb, ecum)
    sol = _tri_solve(revs)(lmat, rhs)
    qk = _each(_mm_nt, q, k)
    aqk = _each(jnp.multiply, qk, decay)
    qd = _each(jnp.multiply, q, ecum)
    kd = _each(lambda a, t, c: a * jnp.exp(t - c), k, total, cum)
    return sol, aqk, qd, kd, _each(jnp.exp, total)


def _chunk_rec(sol, aqk, qd, kd, bl, s):
    ws = _each(lambda so, st: _mm_nn(so[:, HD:], st), sol, s)
    v_new = _each(lambda so, t: so[:, :HD] - t, sol, ws)
    qs = _each(_mm_nn, qd, s)
    av = _each(_mm_nn, aqk, v_new)
    o = _each(jnp.add, qs, av)
    kv = _each(_mm_tn, kd, v_new)
    s_new = _each(lambda st, b, u: st * b + u, s, bl, kv)
    return o, s_new


def _lane_col(x, c):
    lane = lax.broadcasted_iota(jnp.int32, x.shape, 1)
    return jnp.sum(jnp.where(lane == c, x, 0.0), axis=1, keepdims=True)


def _gates(ab, cst):
    z = ab + cst[1:2, :]
    softplus = jnp.maximum(z, 0.0) + jnp.log(1.0 + jnp.exp(-jnp.abs(z)))
    return jax.nn.sigmoid(ab), -cst[0:1, :] * softplus, -cst[0:1, :] * jax.nn.sigmoid(z)


def _beta_g(gates, d, h):
    sig, g, dg = gates
    return _lane_col(sig, NH * d + h), _lane_col(g, 2 * NH + NH * d + h), _lane_col(dg, 2 * NH + NH * d + h)


STEPS = 2
TS = STEPS * CH
_CHAINS = tuple((t, d, h) for t in range(STEPS) for d in (0, 1) for h in range(NH))
_REVS = tuple(bool(d) for _, d, _ in _CHAINS)
_PER_STEP = 2 * NH


def _chain_inputs(refs, r0s, ab_refs, cst):
    hs = lambda h: slice(h * HD, (h + 1) * HD)
    gates = [[_gates(ab_refs[d][pl.ds(r0s[t][d], CH), :], cst) for d in (0, 1)] for t in range(STEPS)]
    q = _each(lambda c: refs[c[1]][0][pl.ds(r0s[c[0]][c[1]], CH), hs(c[2])], _CHAINS)
    k = _each(lambda c: refs[c[1]][1][pl.ds(r0s[c[0]][c[1]], CH), hs(c[2])], _CHAINS)
    v = _each(lambda c: refs[c[1]][2][pl.ds(r0s[c[0]][c[1]], CH), hs(c[2])], _CHAINS)
    bg = _each(lambda c: _beta_g(gates[c[0]][c[1]], c[1], c[2]), _CHAINS)
    return q, k, v, bg


def _of_step(parts, t):
    return tuple(p[t * _PER_STEP:(t + 1) * _PER_STEP] for p in parts)


def _scan_fwd(qkv, ab, cst, s0, *, row_blk0, nb, name, exchange=None):
    cb = TS // CH
    w = NH * HD

    def body(qf, kf, vf, abf, qb, kb, vb, abb, cst_ref, s0_ref, of_ref, ob_ref, sallf_ref, sallb_ref, sfin_ref, s_scr):
        i = pl.program_id(0)

        @pl.when(i == 0)
        def _():
            s_scr[...] = s0_ref[...]

        o_refs, sall_refs = (of_ref, ob_ref), (sallf_ref, sallb_ref)

        def chunks(ci, carry):
            cs = [(ci * STEPS + t, cb - 1 - ci * STEPS - t) for t in range(STEPS)]
            r0s = [tuple(pl.multiple_of(c * CH, CH) for c in ct) for ct in cs]
            q, k, v, bg = _chain_inputs(((qf, kf, vf), (qb, kb, vb)), r0s, (abf, abb), cst_ref[...])
            parts = _chunk_prep(q, k, v, _each(lambda t: t[0], bg), _each(lambda t: t[1], bg), revs=_REVS)
            s = _each(lambda c: s_scr[c[1], c[2]], _CHAINS[:_PER_STEP])
            for t in range(STEPS):
                for (_, d, h), sv in zip(_CHAINS, s):
                    sall_refs[d][cs[t][d], h] = sv
                o, s = _chunk_rec(*_of_step(parts, t), s)
                for (_, d, h), ov in zip(_CHAINS, o):
                    o_refs[d][pl.ds(r0s[t][d], CH), h * HD:(h + 1) * HD] = ov
            for (_, d, h), sv in zip(_CHAINS, s):
                s_scr[d, h] = sv
            return carry

        lax.fori_loop(0, cb // STEPS, chunks, 0)

        @pl.when(i == nb - 1)
        def _():
            sfin_ref[...] = s_scr[...]

    pos = (lambda i: i, lambda i: nb - 1 - i)
    col = lambda d, c: pl.BlockSpec((TS, w), lambda i: (row_blk0 + pos[d](i), c))
    abs_ = lambda d: pl.BlockSpec((TS, 128), lambda i: (row_blk0 + pos[d](i), 0))
    full4 = pl.BlockSpec((2, NH, HD, HD), lambda i: (0, 0, 0, 0))
    o_spec = lambda d: pl.BlockSpec((TS, w), lambda i: (pos[d](i), 0))
    sall_spec = lambda d: pl.BlockSpec((cb, NH, HD, HD), lambda i: (pos[d](i), 0, 0, 0))
    return _pallas(
        body, (qkv, qkv, qkv, ab, qkv, qkv, qkv, ab, cst, s0), name=name, grid=(nb,), sem=("arbitrary",), exchange=exchange,
        in_specs=[col(0, 0), col(0, 1), col(0, 2), abs_(0), col(1, 0), col(1, 1), col(1, 2), abs_(1),
                  pl.BlockSpec((8, 128), lambda i: (0, 0)), full4],
        out_specs=[o_spec(0), o_spec(1), sall_spec(0), sall_spec(1), full4],
        out_shape=[jax.ShapeDtypeStruct((nb * TS, w), F32)] * 2 + [jax.ShapeDtypeStruct((nb * cb, NH, HD, HD), F32)] * 2
        + [jax.ShapeDtypeStruct((2, NH, HD, HD), F32)],
        scratch_shapes=[pltpu.VMEM((2, NH, HD, HD), F32)])


def _scan_bwd(qkv, ab, cst, sall_f, sall_b, do, dsfin, dqkv_f, dqkv_b, dab_f, dab_b, dcst, *, row_blk0, nb, has_do, name,
              exchange=None):
    cb = TS // CH
    w = NH * HD

    def body(qf, kf, vf, abf, qb, kb, vb, abb, cst_ref, sallf_ref, sallb_ref, dof_ref, dob_ref, dsfin_ref, _f, _b, _af, _ab, dcst_in,
             dqkvf_ref, dqkvb_ref, dabf_ref, dabb_ref, dcst_ref, ds0_ref, ds_scr):
        i = pl.program_id(0)

        @pl.when(i == 0)
        def _():
            ds_scr[...] = dsfin_ref[...]
            dcst_ref[...] = dcst_in[...]

        lane = lax.broadcasted_iota(jnp.int32, (CH, 128), 1)
        lane1 = lax.broadcasted_iota(jnp.int32, (1, 128), 1)
        sall_refs, do_refs = (sallf_ref, sallb_ref), (dof_ref, dob_ref)
        dqkv_refs, dab_refs = (dqkvf_ref, dqkvb_ref), (dabf_ref, dabb_ref)

        def chunks(ci, carry):
            cs = [(cb - 1 - ci * STEPS - t, ci * STEPS + t) for t in range(STEPS)]
            r0s = [tuple(pl.multiple_of(c * CH, CH) for c in ct) for ct in cs]
            q, k, v, bg = _chain_inputs(((qf, kf, vf), (qb, kb, vb)), r0s, (abf, abb), cst_ref[...])
            beta, g = _each(lambda t: t[0], bg), _each(lambda t: t[1], bg)
            parts, prep_vjp = jax.vjp(functools.partial(_chunk_prep, revs=_REVS), q, k, v, beta, g)
            ds = _each(lambda c: ds_scr[c[1], c[2]], _CHAINS[:_PER_STEP])
            dparts = []
            for t in range(STEPS):
                s = _each(lambda c: sall_refs[c[1]][cs[t][c[1]], c[2]], _CHAINS[:_PER_STEP])
                _, rec_vjp = jax.vjp(_chunk_rec, *_of_step(parts, t), s)
                do_t = _each(lambda c: do_refs[c[1]][pl.ds(r0s[t][c[1]], CH), c[2] * HD:(c[2] + 1) * HD] if has_do
                             else jnp.zeros((CH, HD), F32), _CHAINS[:_PER_STEP])
                *dpt, ds = rec_vjp((do_t, ds))
                dparts.append(dpt)
            for (_, d, h), dsv in zip(_CHAINS, ds):
                ds_scr[d, h] = dsv
            dq, dk, dv, dbeta, dg = prep_vjp(tuple(sum((dparts[t][j] for t in range(STEPS)), ()) for j in range(len(dparts[0]))))
            dab = [[jnp.zeros((CH, 128), F32), jnp.zeros((CH, 128), F32)] for _ in range(STEPS)]
            dal = jnp.zeros((1, 128), F32)
            for n, (t, d, h) in enumerate(_CHAINS):
                for part, val in enumerate((dq[n], dk[n], dv[n])):
                    dqkv_refs[d][pl.ds(r0s[t][d], CH), part * w + h * HD:part * w + (h + 1) * HD] = val
                dbraw = dbeta[n] * beta[n] * (1.0 - beta[n])
                daraw = dg[n] * bg[n][2]
                dab[t][d] = dab[t][d] + jnp.where(lane == NH * d + h, dbraw, 0.0) + jnp.where(lane == 2 * NH + NH * d + h, daraw, 0.0)
                dal = dal + jnp.where(lane1 == 2 * NH + NH * d + h, jnp.sum(dg[n] * g[n], axis=0, keepdims=True), 0.0)
            dsum = jnp.zeros((CH, 128), F32)
            for t in range(STEPS):
                for d in (0, 1):
                    dab_refs[d][pl.ds(r0s[t][d], CH), :] = dab[t][d]
                    dsum = dsum + dab[t][d]
            dcst_ref[0:1, :] += dal
            dcst_ref[1:2, :] += jnp.sum(jnp.where(lane >= 2 * NH, dsum, 0.0), axis=0, keepdims=True)
            return carry

        lax.fori_loop(0, cb // STEPS, chunks, 0)

        @pl.when(i == nb - 1)
        def _():
            ds0_ref[...] = ds_scr[...]

    pos = (lambda i: nb - 1 - i, lambda i: i)
    col = lambda d, c: pl.BlockSpec((TS, w), lambda i: (row_blk0 + pos[d](i), c))
    abs_ = lambda d: pl.BlockSpec((TS, 128), lambda i: (row_blk0 + pos[d](i), 0))
    full4 = pl.BlockSpec((2, NH, HD, HD), lambda i: (0, 0, 0, 0))
    small = pl.BlockSpec((8, 128), lambda i: (0, 0))
    hbm = pl.BlockSpec(memory_space=pl.ANY)
    sall_spec = lambda d: pl.BlockSpec((cb, NH, HD, HD), lambda i: (pos[d](i), 0, 0, 0))
    do_spec = (lambda d: pl.BlockSpec((TS, w), lambda i: (pos[d](i), 0))) if has_do else (lambda d: small)
    acc_specs = [pl.BlockSpec((TS, 3 * w), lambda i: (row_blk0 + pos[0](i), 0)),
                 pl.BlockSpec((TS, 3 * w), lambda i: (row_blk0 + pos[1](i), 0)), abs_(0), abs_(1), small]
    return _pallas(
        body, (qkv, qkv, qkv, ab, qkv, qkv, qkv, ab, cst, sall_f, sall_b, do, do, dsfin, dqkv_f, dqkv_b, dab_f, dab_b, dcst),
        name=name, grid=(nb,), sem=("arbitrary",), exchange=exchange,
        in_specs=[col(0, 0), col(0, 1), col(0, 2), abs_(0), col(1, 0), col(1, 1), col(1, 2), abs_(1), small,
                  sall_spec(0), sall_spec(1), do_spec(0), do_spec(1), full4, hbm, hbm, hbm, hbm, small],
        out_specs=acc_specs + [full4],
        out_shape=[jax.ShapeDtypeStruct(dqkv_f.shape, F32), jax.ShapeDtypeStruct(dqkv_b.shape, F32),
                   jax.ShapeDtypeStruct(dab_f.shape, F32), jax.ShapeDtypeStruct(dab_b.shape, F32),
                   jax.ShapeDtypeStruct((8, 128), F32), jax.ShapeDtypeStruct((2, NH, HD, HD), F32)],
        aliases={14: 0, 15: 1, 16: 2, 17: 3, 18: 4},
        scratch_shapes=[pltpu.VMEM((2, NH, HD, HD), F32)])


def _pool(xin, *, row0, transpose, name):
    n_tok = xin.shape[0] - row0
    rows = n_tok // GW
    pad = 8 * GW
    tt = 512
    gsh = GW.bit_length() - 1

    def body(x_ref, o_ref, ybuf):
        ii = lax.broadcasted_iota(jnp.int32, (128, 128), 0)
        jj = lax.broadcasted_iota(jnp.int32, (128, 128), 1)
        same_row = (ii >> gsh) == (jj >> gsh)
        ci, cj = ii & (GW - 1), jj & (GW - 1)
        tok = lax.broadcasted_iota(jnp.int32, (tt, 1), 0)
        zpad = jnp.zeros((pad, 128), F32)
        for gi, wdw in enumerate(POOL_WINDOWS):
            lo, hi = wdw // 2, wdw - wdw // 2
            if transpose:
                band = same_row & (ci - cj >= -lo) & (ci - cj < hi)
                offs = range(-hi + 1, lo + 1)
            else:
                band = same_row & (cj - ci >= -lo) & (cj - ci < hi)
                offs = range(-lo, hi)
            bandm = band.astype(BF)
            cs = slice(gi * 128, (gi + 1) * 128)
            ybuf[0:pad, :] = zpad
            ybuf[pad + n_tok:, :] = zpad

            def inv_area(t0):
                t = t0 + tok
                r, c = t >> gsh, t & (GW - 1)
                nr = jnp.minimum(r + hi, rows) - jnp.maximum(r - lo, 0)
                nc = jnp.minimum(c + hi, GW) - jnp.maximum(c - lo, 0)
                return 1.0 / (nr * nc).astype(F32)

            def col_pass(b, carry):
                t0 = pl.multiple_of(b * tt, tt)
                xv = x_ref[pl.ds(row0 + t0, tt), cs]
                if transpose:
                    xv = xv * inv_area(t0)
                hi_part = xv.astype(BF)
                lo_part = (xv - hi_part.astype(F32)).astype(BF)
                for s in range(tt // 128):
                    sl = slice(s * 128, (s + 1) * 128)
                    y = (jnp.dot(bandm, hi_part[sl], preferred_element_type=F32)
                         + jnp.dot(bandm, lo_part[sl], preferred_element_type=F32))
                    ybuf[pl.ds(pad + t0 + s * 128, 128), :] = y
                return carry

            lax.fori_loop(0, n_tok // tt, col_pass, 0)

            def row_pass(b, carry):
                t0 = pl.multiple_of(b * tt, tt)
                acc = ybuf[pl.ds(pad + t0 + offs[0] * GW, tt), :]
                for dr in offs[1:]:
                    acc = acc + ybuf[pl.ds(pad + t0 + dr * GW, tt), :]
                xv = x_ref[pl.ds(row0 + t0, tt), cs]
                if not transpose:
                    acc = acc * inv_area(t0)
                o_ref[pl.ds(t0, tt), cs] = acc - xv
                return carry

            lax.fori_loop(0, n_tok // tt, row_pass, 0)

    return pl.pallas_call(
        body, name=name, out_shape=jax.ShapeDtypeStruct((n_tok, NPOOL), F32),
        in_specs=[pl.BlockSpec(memory_space=pltpu.VMEM)], out_specs=pl.BlockSpec(memory_space=pltpu.VMEM),
        scratch_shapes=[pltpu.VMEM((n_tok + 2 * pad, 128), F32)],
        compiler_params=pltpu.CompilerParams(vmem_limit_bytes=VMEM_LIMIT),
    )(xin)


def _merge_parts(of, ob, pgate, pd, br, gnw, pw_ref, pscale, wg_ref, wp_ref):
    o = of + ob
    ons, ohs, rs = [], [], []
    for h in range(NH):
        oh = o[:, h * HD:(h + 1) * HD]
        r = lax.rsqrt(jnp.mean(oh * oh, axis=-1, keepdims=True) + EPS)
        ohs.append(oh * r)
        rs.append(r)
        ons.append(oh * r * gnw)
    on = jnp.concatenate(ons, axis=1)
    sig_gate = jax.nn.sigmoid(pgate)
    silu_gate = pgate * sig_gate
    og = on * silu_gate
    y_gdn = _nn(og, wg_ref[...])
    ypre = jnp.concatenate([_nn(pd[:, g * 128:(g + 1) * 128], pw_ref[g]) for g in range(4)], axis=1)
    yp = ypre * pscale
    y_pool = _nn(yp, wp_ref[...])
    g_pool = jax.nn.sigmoid(br[:, :D])
    g_gdn = jax.nn.sigmoid(br[:, D:])
    return dict(on=on, ohs=ohs, rs=rs, og=og, y_gdn=y_gdn, ypre=ypre, yp=yp, y_pool=y_pool, g_pool=g_pool, g_gdn=g_gdn,
                sig_gate=sig_gate, silu_gate=silu_gate)


def _merge_fwd(x1, of, ob, pgate, pd, br, modv, gnw, pool_w, pscale, w_gdn, w_pool, w_mo, *, name):
    n_tok = of.shape[0]

    def body(x_ref, of_ref, ob_ref, pg_ref, pd_ref, br_ref, mod_ref, gnw_ref, pw_ref, ps_ref, wg_ref, wp_ref, wmo_ref,
             x2_ref, og_ref, yp_ref, m_ref, mix_ref):
        t = _merge_parts(of_ref[...], ob_ref[...], pg_ref[...], pd_ref[...], br_ref[...], gnw_ref[...], pw_ref, ps_ref[...],
                         wg_ref, wp_ref)
        m = t["g_pool"] * t["y_pool"] + t["g_gdn"] * t["y_gdn"]
        mix = _nn(m, wmo_ref[...])
        og_ref[...] = t["og"].astype(BF)
        yp_ref[...] = t["yp"].astype(BF)
        m_ref[...] = m.astype(BF)
        mix_ref[...] = mix.astype(BF)
        x2_ref[...] = x_ref[...] + mod_ref[0, 5:6, :] * mix

    tile = lambda w: pl.BlockSpec((TM, w), lambda i: (i, 0))
    ctile = lambda w: pl.BlockSpec((TM, w), lambda i: (i + 1, 0))
    return pl.pallas_call(
        body, name=name, grid=(n_tok // TM,),
        in_specs=[ctile(D), tile(D), tile(D), ctile(D), tile(NPOOL), ctile(2 * D),
                  pl.BlockSpec((1, 16, D), lambda i: (1, 0, 0)), _const_spec((1, HD)), _const_spec((4, 128, 128)),
                  _const_spec((1, NPOOL)), _const_spec((D, D)), _const_spec((NPOOL, D)), _const_spec((D, D))],
        out_specs=[tile(D), tile(D), tile(NPOOL), tile(D), tile(D)],
        out_shape=[jax.ShapeDtypeStruct((n_tok, D), F32), jax.ShapeDtypeStruct((n_tok, D), BF),
                   jax.ShapeDtypeStruct((n_tok, NPOOL), BF), jax.ShapeDtypeStruct((n_tok, D), BF),
                   jax.ShapeDtypeStruct((n_tok, D), BF)],
        compiler_params=_cparams("parallel"),
    )(x1, of, ob, pgate, pd, br, modv, gnw, pool_w, pscale, w_gdn, w_pool, w_mo)


def _merge_bwd(dx2, mix, of, ob, pgate, pd, br, modv, gnw, pool_w, pscale, w_gdn, w_pool, w_mo, *, name):
    n_tok = of.shape[0]

    def body(dx2_ref, mix_ref, of_ref, ob_ref, pg_ref, pd_ref, br_ref, mod_ref, gnw_ref, pw_ref, ps_ref, wg_ref, wp_ref, wmo_ref,
             do_ref, dgate_ref, dpd_ref, dbr_ref, dmix_ref, dyg_ref, dyp_ref, acc_ref, dpw_ref):
        i = pl.program_id(0)
        pgate, pdv, gnw = pg_ref[...], pd_ref[...], gnw_ref[...]
        t = _merge_parts(of_ref[...], ob_ref[...], pgate, pdv, br_ref[...], gnw, pw_ref, ps_ref[...], wg_ref, wp_ref)
        dx2v = dx2_ref[...]
        dmix = mod_ref[0, 5:6, :] * dx2v
        dmixb = dmix.astype(BF)
        dmix_ref[...] = dmixb
        dm = _nt(dmixb, wmo_ref[...])
        gp, gg = t["g_pool"], t["g_gdn"]
        dbr_ref[:, :D] = (dm * t["y_pool"] * gp * (1.0 - gp)).astype(BF)
        dbr_ref[:, D:] = (dm * t["y_gdn"] * gg * (1.0 - gg)).astype(BF)
        dyp = (dm * gp).astype(BF)
        dyg = (dm * gg).astype(BF)
        dyp_ref[...] = dyp
        dyg_ref[...] = dyg
        dyp_in = _nt(dyp, wp_ref[...])
        dypre = dyp_in * ps_ref[...]
        for g in range(4):
            gs = slice(g * 128, (g + 1) * 128)
            dpd_ref[:, gs] = _nt(dypre[:, gs], pw_ref[g])
        dog = _nt(dyg, wg_ref[...])
        dgate_ref[...] = (dog * t["on"] * (t["sig_gate"] * (1.0 + pgate - t["silu_gate"]))).astype(BF)
        don = dog * t["silu_gate"]
        dgnw = jnp.zeros((1, HD), F32)
        for h in range(NH):
            hs = slice(h * HD, (h + 1) * HD)
            donh, oh, r = don[:, hs], t["ohs"][h], t["rs"][h]
            dgnw = dgnw + jnp.sum(donh * oh, axis=0, keepdims=True)
            doh = donh * gnw
            do_ref[:, hs] = r * (doh - oh * jnp.mean(doh * oh, axis=-1, keepdims=True))

        @pl.when(i == 0)
        def _():
            acc_ref[...] = jnp.zeros_like(acc_ref)
            dpw_ref[...] = jnp.zeros_like(dpw_ref)

        acc_ref[0:1, :] += jnp.sum(dx2v * mix_ref[...].astype(F32), axis=0, keepdims=True)
        acc_ref[1:2, 0:HD] += dgnw
        acc_ref[2:3, 0:NPOOL] += jnp.sum(dyp_in * t["ypre"], axis=0, keepdims=True)
        for g in range(4):
            gs = slice(g * 128, (g + 1) * 128)
            dpw_ref[g] += _tn(pdv[:, gs], dypre[:, gs])

    tile = lambda w: pl.BlockSpec((TM, w), lambda i: (i, 0))
    ctile = lambda w: pl.BlockSpec((TM, w), lambda i: (i + 1, 0))
    return pl.pallas_call(
        body, name=name, grid=(n_tok // TM,),
        in_specs=[tile(D), tile(D), tile(D), tile(D), ctile(D), tile(NPOOL), ctile(2 * D),
                  pl.BlockSpec((1, 16, D), lambda i: (1, 0, 0)), _const_spec((1, HD)), _const_spec((4, 128, 128)),
                  _const_spec((1, NPOOL)), _const_spec((D, D)), _const_spec((NPOOL, D)), _const_spec((D, D))],
        out_specs=[tile(D), tile(D), tile(NPOOL), tile(2 * D), tile(D), tile(D), tile(D),
                   pl.BlockSpec((8, D), lambda i: (0, 0)), pl.BlockSpec((4, 128, 128), lambda i: (0, 0, 0))],
        out_shape=[jax.ShapeDtypeStruct((n_tok, D), F32), jax.ShapeDtypeStruct((n_tok, D), BF),
                   jax.ShapeDtypeStruct((n_tok, NPOOL), F32), jax.ShapeDtypeStruct((n_tok, 2 * D), BF),
                   jax.ShapeDtypeStruct((n_tok, D), BF), jax.ShapeDtypeStruct((n_tok, D), BF), jax.ShapeDtypeStruct((n_tok, D), BF),
                   jax.ShapeDtypeStruct((8, D), F32), jax.ShapeDtypeStruct((4, 128, 128), F32)],
        compiler_params=_cparams("arbitrary"),
    )(dx2, mix, of, ob, pgate, pd, br, modv, gnw, pool_w, pscale, w_gdn, w_pool, w_mo)


def _split(results, n):
    return (*results[:n], list(results[n:]))


def _local_step(ctx, x, target, modv, p, late=None):
    t_lat = x.shape[0]
    n_all = t_lat + TM
    nbc, nbx = TM // TS, t_lat // TS
    mod_lat = modv[1:2]
    gather = (lambda arrs: _ChipExchange(arrs, False)) if late else (lambda arrs: None)
    scatter = (lambda arrs: _ChipExchange(arrs, True)) if late else (lambda arrs: None)

    x1, h1, gu1, f1, *got = _ffn_fwd(x, modv, p["norm1"], p["w1_in"], p["w1_out"], mrow=0, name="ffn1_fwd", ctx=ctx,
                                     exchange=gather(late and late[0]))
    if late:
        p = {**p, "w_mix": _regroup_mix(_from_chip_major_cols(got[0])), "conv": jnp.pad(_from_chip_major_cols(got[1]), ((0, 3), (0, 0)))}
    u, p_qkv, p_gate, p_pool, p_br, p_ab, *got = _mix_in_fwd(x1, modv, p["norm2"], p["w_mix"], name="mix_in_fwd",
                                                              exchange=gather(late and late[2]))
    if late:
        p = {**p, "w_gdn": got[0].reshape(D, D), "w_pool": _from_chip_major_cols(got[1]), "w_mo": got[2].reshape(D, D)}
    qkv, pre_qkv = _prep_fwd(p_qkv, p["conv"], name="prep_fwd")
    s_zero = jnp.zeros((2, NH, HD, HD), F32)
    _, _, sall_cf, sall_cb, s_ctx = _scan_fwd(qkv, p_ab, p["cst"], s_zero, row_blk0=0, nb=nbc, name="scan_ctx")
    o_f, o_b, sall_f, sall_b, _, *got = _scan_fwd(qkv, p_ab, p["cst"], s_ctx, row_blk0=nbc, nb=nbx, name="scan_lat",
                                                  exchange=gather(late and late[1]))
    if late:
        p = {**p, "w2_in": got[0], "w2_out": got[1].reshape(FF, D)}
    pd = _pool(p_pool, row0=TM, transpose=False, name="pool_fwd")
    merge_w = (modv, p["gnw"], p["pool_w"], p["pscale"], p["w_gdn"], p["w_pool"], p["w_mo"])
    x2, og, yp, m, mix = _merge_fwd(x1, o_f, o_b, p_gate, pd, p_br, *merge_w, name="merge_fwd")
    dx3, h3, gu3, f3, acc_fin = _ffn_fwd(x2, mod_lat, p["norm3"], p["w2_in"], p["w2_out"], mrow=6, name="ffn2_fwd",
                                         head=(target, p["fnorm"]))

    dx2, a3, df3, dgu3, acc3 = _ffn_bwd(dx3, x2, gu3, f3, mod_lat, p["norm3"], p["w2_in"], p["w2_out"], mrow=6,
                                        name="ffn2_bwd")
    g = {}
    tkl = _k_tile(t_lat)
    g["w2_out"] = _matmul_tn(a3, df3, tmm=FF // 2, tn=D, tk=tkl, name="ffn2_wout_grad").reshape(NCHIP, FF // NCHIP, D)
    g["w2_in"] = _matmul_tn(h3, dgu3, tmm=D, tn=2 * FF // NCHIP, tk=tkl, nsplit=NCHIP, name="ffn2_win_grad")
    do, dgate, dpd, dbr, dmix, dyg, dyp, acc_m, dpw = _merge_bwd(dx2, mix, o_f, o_b, p_gate, pd, p_br, *merge_w, name="merge_bwd")
    g["w_mo"] = _matmul_tn(m, dmix, tmm=D, tn=D, tk=tkl, name="wmo_grad").reshape(NCHIP, D // NCHIP, D)
    g["w_gdn"] = _matmul_tn(og, dyg, tmm=D, tn=D, tk=tkl, name="wgdn_grad").reshape(NCHIP, D // NCHIP, D)
    g["w_pool"] = _matmul_tn(yp, dyp, tmm=NPOOL, tn=D // NCHIP, tk=tkl, nsplit=NCHIP, name="wpool_grad")
    dpool_in = _pool(dpd, row0=0, transpose=True, name="pool_bwd")
    acc = (lax.empty((n_all, NQKV), F32), lax.empty((n_all, NQKV), F32), lax.empty((n_all, 128), F32),
           lax.empty((n_all, 128), F32), jnp.zeros((8, 128), F32))
    behind_scan = ("w2_in", "w2_out", "w_gdn", "w_pool", "w_mo")
    *acc, ds_ctx, landed = _split(_scan_bwd(qkv, p_ab, p["cst"], sall_f, sall_b, do, s_zero, *acc, row_blk0=nbc, nb=nbx, has_do=True,
                                            name="scan_lat_bwd", exchange=scatter([g[k] for k in behind_scan])), 6)
    landed = dict(zip(behind_scan, landed))
    dqkv_f, dqkv_b, dab_f, dab_b, dcst, _ = _scan_bwd(qkv, p_ab, p["cst"], sall_cf, sall_cb, jnp.zeros((8, 128), F32), ds_ctx, *acc,
                                                      row_blk0=0, nb=nbc, has_do=False, name="scan_ctx_bwd")
    dpqkv, dconv = _prep_bwd(p_qkv, pre_qkv, dqkv_f, dqkv_b, p["conv"], name="prep_bwd")
    dx1, dp, acc_mix = _mix_in_bwd(dx2, x1, dpqkv, dgate, dpool_in, dbr, dab_f, dab_b, modv, p["norm2"], p["w_mix"],
                                   name="mix_in_bwd")
    tka = _k_tile(n_all)
    g["w_mix"] = _chip_major_cols(_ungroup_mix(_matmul_tn(u, dp, tmm=256, tn=NMIXP, tk=_k_tile(n_all, 1024), name="wmix_grad")))
    dx_lat, a1, df1, dgu1, acc1, got = _split(_ffn_bwd(dx1, x, gu1, f1, modv, p["norm1"], p["w1_in"], p["w1_out"], mrow=0, ctx=ctx,
                                                   name="ffn1_bwd", exchange=scatter([g["w_mix"]])), 5)
    landed.update(zip(("w_mix",), got))
    g["w1_out"] = _matmul_tn(a1, df1, tmm=FF // 2, tn=D, tk=tka, name="ffn1_wout_grad").reshape(NCHIP, FF // NCHIP, D)
    g["w1_in"] = _matmul_tn(h1, dgu1, tmm=D, tn=2 * FF // NCHIP, tk=tka, nsplit=NCHIP, name="ffn1_win_grad")

    small = dict(norm1=acc1[0, 3] + acc1[1, 3], norm2=acc_mix[0, 3] + acc_mix[1, 3], norm3=acc3[0, 3], fnorm=acc_fin[0],
                 gnw=acc_m[1, :HD], pscale=acc_m[2, :NPOOL], pool_w=dpw, conv=dconv[:5],
                 a_log=dcst[0, 2 * NH:4 * NH], dt_bias=dcst[1, 2 * NH:4 * NH])
    zero = jnp.zeros((D,), F32)
    dmod = jnp.stack([
        jnp.stack([acc1[0, 0], acc1[0, 1], acc1[0, 2], acc_mix[0, 0], acc_mix[0, 1], zero, zero, zero, zero]),
        jnp.stack([acc1[1, 0], acc1[1, 1], acc1[1, 2], acc_mix[1, 0], acc_mix[1, 1], acc_m[0], acc3[0, 0], acc3[0, 1], acc3[0, 2]]),
    ])
    return jnp.sum(acc_fin[1]), dx_lat, g, landed, small, dmod


_HI = lax.Precision.HIGHEST


def _ada_fwd(c_all, w_sh, b_sh, *, name):
    def body(c_ref, w_ref, b_ref, o_ref):
        o_ref[...] = jnp.dot(_silu(c_ref[...]), w_ref[...], precision=_HI, preferred_element_type=F32) + b_ref[...]

    return pl.pallas_call(body, name=name, out_shape=jax.ShapeDtypeStruct((16, w_sh.shape[1]), F32),
                          compiler_params=pltpu.CompilerParams(vmem_limit_bytes=VMEM_LIMIT))(c_all, w_sh, b_sh)


def _ada_bwd(c_all, dm, w_sh, *, name):
    def body(c_ref, dm_ref, w_ref, dw_ref, dc_ref):
        sc = _silu(c_ref[...])
        dw_ref[...] = lax.dot_general(sc, dm_ref[...], (((0,), (0,)), ((), ())), precision=_HI, preferred_element_type=F32)
        part = lax.dot_general(dm_ref[8:9, :], w_ref[...], (((1,), (1,)), ((), ())), precision=_HI, preferred_element_type=F32)
        dc_ref[...] = jnp.broadcast_to(part, dc_ref.shape)

    return pl.pallas_call(body, name=name,
                          out_shape=[jax.ShapeDtypeStruct(w_sh.shape, F32), jax.ShapeDtypeStruct((8, D), F32)],
                          compiler_params=pltpu.CompilerParams(vmem_limit_bytes=VMEM_LIMIT))(c_all, dm, w_sh)


def _cctx_grad(parts, c_ctx, *, name):
    def body(p_ref, c_ref, o_ref):
        tot = (p_ref[0, 0:1, :] + p_ref[2, 0:1, :]) + (p_ref[4, 0:1, :] + p_ref[6, 0:1, :])
        o_ref[...] = tot * _dsilu(c_ref[...])

    return pl.pallas_call(body, name=name, out_shape=jax.ShapeDtypeStruct((1, D), F32))(parts, c_ctx)


_MESH = pl.DeviceIdType.MESH
_ANY = pl.BlockSpec(memory_space=pl.ANY)


def _flip(v, bit):
    return (1 - v) if bit else v


def _all_gather8(x, *, name):
    def body(x_ref, out_ref, send_sems, recv_sems, local_sem):
        mx, my, mc = lax.axis_index("x"), lax.axis_index("y"), lax.axis_index("c")
        me = 4 * mx + 2 * my + mc
        mine = pltpu.make_async_copy(x_ref, out_ref.at[me], local_sem)
        mine.start()
        sends, recvs = [], []
        for k in range(1, 8):
            px, py, pc = _flip(mx, k & 4), _flip(my, k & 2), _flip(mc, k & 1)
            sends.append(pltpu.make_async_remote_copy(src_ref=x_ref, dst_ref=out_ref.at[me], send_sem=send_sems.at[k - 1],
                                                      recv_sem=recv_sems.at[k - 1], device_id=(px, py, pc), device_id_type=_MESH))
            recvs.append(pltpu.make_async_remote_copy(src_ref=x_ref, dst_ref=out_ref.at[4 * px + 2 * py + pc],
                                                      send_sem=send_sems.at[k - 1], recv_sem=recv_sems.at[k - 1],
                                                      device_id=(px, py, pc), device_id_type=_MESH))
        for cp in sends:
            cp.start()
        for cp in recvs:
            cp.wait_recv()
        for cp in sends:
            cp.wait_send()
        mine.wait()

    vm = pl.BlockSpec(memory_space=pltpu.VMEM)
    return pl.pallas_call(
        body, name=name, out_shape=jax.ShapeDtypeStruct((8,) + x.shape, x.dtype), in_specs=[vm], out_specs=vm,
        scratch_shapes=[pltpu.SemaphoreType.DMA((7,)), pltpu.SemaphoreType.DMA((7,)), pltpu.SemaphoreType.DMA],
        compiler_params=pltpu.CompilerParams(vmem_limit_bytes=VMEM_LIMIT),
    )(x)


class _ChipExchange:
    def __init__(self, arrs, scatter):
        self.arrs, self.scatter, self.n = list(arrs), scatter, len(arrs)
        self.out_shape = [jax.ShapeDtypeStruct(a.shape if scatter else (NCHIP,) + a.shape, a.dtype) for a in self.arrs]
        links = self.n * (NCHIP - 1)
        self.scratch = [pltpu.SemaphoreType.DMA((links,)), pltpu.SemaphoreType.DMA((links,)), pltpu.SemaphoreType.DMA((self.n,))]

    def copies(self, ins, outs, send_sems, recv_sems, local_sems):
        mx, my, mc = lax.axis_index("x"), lax.axis_index("y"), lax.axis_index("c")
        me = 2 * mx + my
        local, sends, recvs = [], [], []
        for j in range(self.n):
            src_own = ins[j].at[me] if self.scatter else ins[j]
            local.append(pltpu.make_async_copy(src_own, outs[j].at[me], local_sems.at[j]))
            for k in range(1, NCHIP):
                px, py = _flip(mx, k & 2), _flip(my, k & 1)
                peer = 2 * px + py
                sem = j * (NCHIP - 1) + k - 1
                src = ins[j].at[peer] if self.scatter else ins[j]
                sends.append(pltpu.make_async_remote_copy(src_ref=src, dst_ref=outs[j].at[me], send_sem=send_sems.at[sem],
                                                          recv_sem=recv_sems.at[sem], device_id=(px, py, mc), device_id_type=_MESH))
                recvs.append(pltpu.make_async_remote_copy(src_ref=src, dst_ref=outs[j].at[peer], send_sem=send_sems.at[sem],
                                                          recv_sem=recv_sems.at[sem], device_id=(px, py, mc), device_id_type=_MESH))
        return local, sends, recvs

    @staticmethod
    def start(local, sends, recvs):
        for cp in local + sends:
            cp.start()

    @staticmethod
    def finish(local, sends, recvs):
        for cp in recvs:
            cp.wait_recv()
        for cp in sends:
            cp.wait_send()
        for cp in local:
            cp.wait()


def _gather_split(arrs, *, name):
    n = len(arrs)
    links = n * (NCHIP - 1)

    def body(*refs):
        ins, outs = refs[:n], refs[n:2 * n]
        ici_send, ici_recv, d2d_send, d2d_recv, local_sems = refs[2 * n:]
        mx, my, mc = lax.axis_index("x"), lax.axis_index("y"), lax.axis_index("c")
        me = 2 * mx + my
        local, first, arrive, onward, handed = [], [], [], [], []
        for j in range(n):
            hr = ins[j].shape[0] // 2
            mine, other = pl.ds(mc * hr, hr), pl.ds((1 - mc) * hr, hr)
            local.append(pltpu.make_async_copy(ins[j], outs[j].at[me], local_sems.at[j]))
            for k in range(1, NCHIP):
                px, py = _flip(mx, k & 2), _flip(my, k & 1)
                peer = 2 * px + py
                sem = j * (NCHIP - 1) + k - 1
                ici = lambda slot: pltpu.make_async_remote_copy(
                    src_ref=ins[j].at[mine], dst_ref=outs[j].at[slot, mine], send_sem=ici_send.at[sem], recv_sem=ici_recv.at[sem],
                    device_id=(px, py, mc), device_id_type=_MESH)
                d2d = lambda rows: pltpu.make_async_remote_copy(
                    src_ref=outs[j].at[peer, rows], dst_ref=outs[j].at[peer, rows], send_sem=d2d_send.at[sem], recv_sem=d2d_recv.at[sem],
                    device_id=(mx, my, 1 - mc), device_id_type=_MESH)
                first.append(ici(me))
                arrive.append(ici(peer))
                onward.append(d2d(mine))
                handed.append(d2d(other))
        for cp in local + first:
            cp.start()
        for got, fwd in zip(arrive, onward):
            got.wait_recv()
            fwd.start()
        for cp in handed:
            cp.wait_recv()
        for cp in first + onward:
            cp.wait_send()
        for cp in local:
            cp.wait()

    sems = pltpu.SemaphoreType.DMA((links,))
    return pl.pallas_call(
        body, name=name, out_shape=[jax.ShapeDtypeStruct((NCHIP,) + a.shape, a.dtype) for a in arrs],
        in_specs=[_ANY] * n, out_specs=[_ANY] * n, scratch_shapes=[sems, sems, sems, sems, pltpu.SemaphoreType.DMA((n,))],
    )(*arrs)


_HBM = pl.BlockSpec(memory_space=pltpu.HBM)
_SEM = pl.BlockSpec(memory_space=pltpu.SEMAPHORE)
_DATAFLOW = pltpu.SideEffectType.DATAFLOW_SIDE_EFFECTING


def _scatter_copies(ins, lands, send_sems, recv_sems):
    mx, my, mc = lax.axis_index("x"), lax.axis_index("y"), lax.axis_index("c")
    me = 2 * mx + my
    sends, recvs = [], []
    for j in range(len(ins)):
        for k in range(1, NCHIP):
            px, py = _flip(mx, k & 2), _flip(my, k & 1)
            peer = 2 * px + py
            sem = j * (NCHIP - 1) + k - 1
            mk = lambda slot: pltpu.make_async_remote_copy(src_ref=ins[j].at[peer], dst_ref=lands[j].at[slot], send_sem=send_sems.at[sem],
                                                           recv_sem=recv_sems.at[sem], device_id=(px, py, mc), device_id_type=_MESH)
            sends.append(mk(me))
            recvs.append(mk(peer))
    return sends, recvs


def _scatter_start(arrs, after, *, name):
    n = len(arrs)
    links = n * (NCHIP - 1)
    n_in = 2 * n + len(after)

    def body(*refs):
        ins, lands = refs[:n], refs[n:2 * n]
        send_sems, recv_sems = refs[n_in], refs[n_in + 1]
        token = refs[-1]
        for cp in _scatter_copies(ins, lands, send_sems, recv_sems)[0]:
            cp.start()
        token[...] = jnp.zeros_like(token)

    hbm = lambda a: pltpu.HBM(a.shape, a.dtype)
    res = pl.pallas_call(
        body, name=name,
        out_shape=(pltpu.SemaphoreType.DMA((links,)), pltpu.SemaphoreType.DMA((links,)), *[hbm(a) for a in arrs], *[hbm(a) for a in arrs],
                   jax.ShapeDtypeStruct((8, 128), F32)),
        in_specs=[_HBM] * (2 * n) + [_ANY] * len(after), out_specs=(_SEM, _SEM, *[_HBM] * (2 * n), pl.BlockSpec(memory_space=pltpu.VMEM)),
        input_output_aliases={j: 2 + j for j in range(2 * n)},
        compiler_params=pltpu.CompilerParams(has_side_effects=_DATAFLOW),
    )(*[pltpu.with_memory_space_constraint(a, pltpu.HBM) for a in arrs],
      *[pltpu.with_memory_space_constraint(lax.empty(a.shape, a.dtype), pltpu.HBM) for a in arrs], *after)
    return res[0], res[1], list(res[2:2 + n]), list(res[2 + n:2 + 2 * n]), res[-1]


def _scatter_wait(send_sems, recv_sems, arrs, lands, after, *, name):
    n = len(arrs)

    def body(*refs):
        ins, lands_in = refs[:n], refs[n:2 * n]
        sends, recvs = _scatter_copies(ins, lands_in, refs[2 * n], refs[2 * n + 1])
        for cp in sends:
            cp.wait_send()
        for cp in recvs:
            cp.wait_recv()

    hbm = lambda a: pltpu.HBM(a.shape, a.dtype)
    res = pl.pallas_call(
        body, name=name, out_shape=(*[hbm(a) for a in arrs], *[hbm(a) for a in lands]),
        in_specs=[_HBM] * (2 * n) + [_SEM, _SEM] + [_ANY] * len(after), out_specs=[_HBM] * (2 * n),
        input_output_aliases={j: j for j in range(2 * n)},
        compiler_params=pltpu.CompilerParams(has_side_effects=_DATAFLOW),
    )(*arrs, *lands, send_sems, recv_sems, *after)
    return list(res[:n]), list(res[n:])


def _pallas(body, operands, *, name, grid, in_specs, out_specs, out_shape, sem, scratch_shapes=(), aliases=None, exchange=None):
    if exchange is None:
        return pl.pallas_call(body, name=name, grid=grid, in_specs=in_specs, out_specs=out_specs, out_shape=out_shape,
                              scratch_shapes=list(scratch_shapes), input_output_aliases=aliases or {},
                              compiler_params=_cparams(*sem))(*operands)
    ex = exchange
    (steps,) = grid
    n_in, n_out, n_scr, k = len(in_specs), len(out_specs), len(scratch_shapes), ex.n

    def hosted(*refs):
        ins, refs = refs[:n_in], refs[n_in:]
        ex_in, refs = refs[:k], refs[k:]
        outs, refs = refs[:n_out], refs[n_out:]
        ex_out, refs = refs[:k], refs[k:]
        scr, ex_sems = refs[:n_scr], refs[n_scr:]
        cps = ex.copies(ex_in, ex_out, *ex_sems)
        pl.when(pl.program_id(0) == 0)(lambda: ex.start(*cps))
        body(*ins, *outs, *scr)
        pl.when(pl.program_id(0) == steps - 1)(lambda: ex.finish(*cps))

    return pl.pallas_call(
        hosted, name=name, grid=grid, in_specs=list(in_specs) + [_ANY] * k, out_specs=list(out_specs) + [_ANY] * k,
        out_shape=list(out_shape) + ex.out_shape, scratch_shapes=list(scratch_shapes) + ex.scratch,
        input_output_aliases=aliases or {}, compiler_params=_cparams("arbitrary"),
    )(*operands, *ex.arrs)


def _core_swap(arrs, *, name):
    n = len(arrs)

    def body(*refs):
        ins, outs = refs[:n], refs[n:2 * n]
        send_sems, recv_sems = refs[2 * n:]
        sib = (lax.axis_index("x"), lax.axis_index("y"), 1 - lax.axis_index("c"))
        cps = [pltpu.make_async_remote_copy(src_ref=ins[j], dst_ref=outs[j], send_sem=send_sems.at[j], recv_sem=recv_sems.at[j],
                                            device_id=sib, device_id_type=_MESH) for j in range(n)]
        for cp in cps:
            cp.start()
        for cp in cps:
            cp.wait_recv()
        for cp in cps:
            cp.wait_send()

    return pl.pallas_call(
        body, name=name, out_shape=[jax.ShapeDtypeStruct(a.shape, a.dtype) for a in arrs],
        in_specs=[_ANY] * n, out_specs=[_ANY] * n,
        scratch_shapes=[pltpu.SemaphoreType.DMA((n,)), pltpu.SemaphoreType.DMA((n,))],
    )(*arrs)


def _row_tile(rows, cols, budget=1 << 20):
    best = None
    for t in range(8, rows + 1, 8):
        if rows % t == 0 and t * cols <= budget:
            best = t
    return best or rows


def _sum_slots(x, *, name, after=()):
    ns, r, c = x.shape
    tr = _row_tile(r, c * ns)

    def body(x_ref, *refs):
        acc = x_ref[0].astype(F32)
        for s in range(1, ns):
            acc = acc + x_ref[s].astype(F32)
        refs[-1][...] = acc

    return pl.pallas_call(
        body, name=name, grid=(r // tr,), out_shape=jax.ShapeDtypeStruct((r, c), F32),
        in_specs=[pl.BlockSpec((ns, tr, c), lambda i: (0, i, 0))] + [_ANY] * len(after), out_specs=pl.BlockSpec((tr, c), lambda i: (i, 0)),
        compiler_params=_cparams("parallel"),
    )(x, *after)


def _adamw(w, ga, gb, m, v, *, name):
    r, c = w.shape
    tr = _row_tile(r, c, budget=1 << 18)
    two = gb is not None

    def body(*refs):
        w_ref, ga_ref = refs[0], refs[1]
        m_ref, v_ref = refs[2 + two], refs[3 + two]
        g_ref, d_ref, mo_ref, vo_ref = refs[4 + two:]
        g = ga_ref[...] + refs[2][...] if two else ga_ref[...]
        mn = ADAM_B1 * m_ref[...] + (1.0 - ADAM_B1) * g
        vn = ADAM_B2 * v_ref[...] + (1.0 - ADAM_B2) * (g * g)
        m_hat = mn / (1.0 - ADAM_B1 ** ADAM_STEP)
        v_hat = vn / (1.0 - ADAM_B2 ** ADAM_STEP)
        g_ref[...] = g
        d_ref[...] = -ADAM_LR * (m_hat / (jnp.sqrt(v_hat) + ADAM_EPS) + ADAM_WD * w_ref[...])
        mo_ref[...] = mn
        vo_ref[...] = vn

    spec = pl.BlockSpec((tr, c), lambda i: (i, 0))
    ins = [w, ga] + ([gb] if two else []) + [m, v]
    return pl.pallas_call(
        body, name=name, grid=(r // tr,), out_shape=[jax.ShapeDtypeStruct((r, c), F32)] * 4,
        in_specs=[spec] * len(ins), out_specs=[spec] * 4, compiler_params=_cparams("parallel"),
    )(*ins)


_MIX_AB0, _MIX_AB1 = NQKV, NQKV + 4 * NH


def _regroup_mix(w):
    pad = jnp.zeros((w.shape[0], NMIXP - NMIX), w.dtype)
    return jnp.concatenate([w[:, :_MIX_AB0], w[:, _MIX_AB1:], w[:, _MIX_AB0:_MIX_AB1], pad], axis=1)


def _ungroup_mix(w):
    n_ab = _MIX_AB1 - _MIX_AB0
    return jnp.concatenate([w[:, :_MIX_AB0], w[:, NMIX - n_ab:NMIX], w[:, _MIX_AB0:NMIX - n_ab]], axis=1)


def _chip_major_cols(w):
    r, c = w.shape
    return w.reshape(r, NCHIP, c // NCHIP).transpose(1, 0, 2)


def _from_chip_major_cols(w):
    return w.transpose(1, 0, 2).reshape(w.shape[1], -1)


_SMALL = (("c_ctx", D), ("b_ada", 9 * D), ("norm1_w", D), ("norm2_w", D), ("norm3_w", D), ("final_norm_w", D),
          ("a_log", 2 * NH), ("dt_bias", 2 * NH), ("gdn_norm_w", HD), ("pool_w", 4 * 128 * 128), ("pool_scale", NPOOL),
          ("conv_w", 5 * NQKV // NCHIP))


def _pack(vals, lanes=128, row_mult=8):
    flat = jnp.concatenate([jnp.ravel(v) for v in vals])
    n = flat.shape[0]
    rows = -(-n // (lanes * row_mult)) * row_mult
    return jnp.pad(flat, (0, rows * lanes - n)).reshape(rows, lanes)


def _unpack(packed, sizes):
    flat = packed.reshape(-1)
    out, o = [], 0
    for n in sizes:
        out.append(flat[o:o + n])
        o += n
    return out


def kernel(x, c, ctx, c_ctx, w_ada, b_ada, norm1_w, ffn1_w_in, ffn1_w_out, norm2_w, w_mix_in, conv_w, a_log, dt_bias, gdn_norm_w, w_gdn_proj, pool_w, pool_scale, w_pool_proj, w_mix_out, norm3_w, ffn2_w_in, ffn2_w_out, final_norm_w, loss_target, m_c_ctx, m_w_ada, m_b_ada, m_norm1_w, m_ffn1_w_in, m_ffn1_w_out, m_norm2_w, m_w_mix_in, m_conv_w, m_a_log, m_dt_bias, m_gdn_norm_w, m_w_gdn_proj, m_pool_w, m_pool_scale, m_w_pool_proj, m_w_mix_out, m_norm3_w, m_ffn2_w_in, m_ffn2_w_out, m_final_norm_w, v_c_ctx, v_w_ada, v_b_ada, v_norm1_w, v_ffn1_w_in, v_ffn1_w_out, v_norm2_w, v_w_mix_in, v_conv_w, v_a_log, v_dt_bias, v_gdn_norm_w, v_w_gdn_proj, v_pool_w, v_pool_scale, v_w_pool_proj, v_w_mix_out, v_norm3_w, v_ffn2_w_in, v_ffn2_w_out, v_final_norm_w):
    names = ("c_ctx", "w_ada", "b_ada", "norm1_w", "ffn1_w_in", "ffn1_w_out", "norm2_w", "w_mix_in", "conv_w", "a_log", "dt_bias",
             "gdn_norm_w", "w_gdn_proj", "pool_w", "pool_scale", "w_pool_proj", "w_mix_out", "norm3_w", "ffn2_w_in", "ffn2_w_out",
             "final_norm_w")
    w = dict(zip(names, (c_ctx, w_ada, b_ada, norm1_w, ffn1_w_in, ffn1_w_out, norm2_w, w_mix_in, conv_w, a_log, dt_bias, gdn_norm_w,
                         w_gdn_proj, pool_w, pool_scale, w_pool_proj, w_mix_out, norm3_w, ffn2_w_in, ffn2_w_out, final_norm_w)))
    mom = dict(zip(names, (m_c_ctx, m_w_ada, m_b_ada, m_norm1_w, m_ffn1_w_in, m_ffn1_w_out, m_norm2_w, m_w_mix_in, m_conv_w, m_a_log,
                           m_dt_bias, m_gdn_norm_w, m_w_gdn_proj, m_pool_w, m_pool_scale, m_w_pool_proj, m_w_mix_out, m_norm3_w,
                           m_ffn2_w_in, m_ffn2_w_out, m_final_norm_w)))
    var = dict(zip(names, (v_c_ctx, v_w_ada, v_b_ada, v_norm1_w, v_ffn1_w_in, v_ffn1_w_out, v_norm2_w, v_w_mix_in, v_conv_w, v_a_log,
                           v_dt_bias, v_gdn_norm_w, v_w_gdn_proj, v_pool_w, v_pool_scale, v_w_pool_proj, v_w_mix_out, v_norm3_w,
                           v_ffn2_w_in, v_ffn2_w_out, v_final_norm_w)))
    mx, my, mc = lax.axis_index("x"), lax.axis_index("y"), lax.axis_index("c")
    chip = 2 * mx + my
    dev = 2 * chip + mc
    ada_cols = w_ada.shape[2]

    c_rows = _all_gather8(jnp.pad(c, ((0, 7), (0, 0))), name="gather_c")[:, 0, :]
    c_all = jnp.concatenate([c_rows, c_ctx[None], jnp.zeros((7, D), F32)], axis=0)
    b_sh = lax.dynamic_slice(b_ada, (0, chip * ada_cols), (1, ada_cols))
    mod_sh = _ada_fwd(c_all, w_ada[0], b_sh, name="ada_fwd")
    mod_parts = _all_gather8(mod_sh, name="gather_mod")
    mod_all = jnp.concatenate([mod_parts[2 * s] for s in range(NCHIP)], axis=1)
    mod_lat = lax.dynamic_index_in_dim(mod_all, dev, axis=0, keepdims=False).reshape(9, D)
    modv = jnp.zeros((2, 16, D), F32).at[0, :9].set(mod_all[8].reshape(9, D)).at[1, :9].set(mod_lat)

    big = ("ffn1_w_in", "ffn1_w_out", "w_mix_in", "w_gdn_proj", "w_pool_proj", "w_mix_out", "ffn2_w_in", "ffn2_w_out")
    shard = {k: w[k][0].astype(BF) for k in big}
    w1_in, w1_out = _gather_split([shard["ffn1_w_in"], shard["ffn1_w_out"]], name="gather_ffn1")
    p = dict(
        norm1=norm1_w, norm2=norm2_w, norm3=norm3_w, fnorm=final_norm_w[None], w1_in=w1_in, w1_out=w1_out.reshape(FF, D),
        cst=jnp.zeros((8, 128), F32).at[0, 2 * NH:4 * NH].set(jnp.exp(a_log).reshape(-1)).at[1, 2 * NH:4 * NH].set(dt_bias.reshape(-1)),
        gnw=gdn_norm_w, pool_w=pool_w[0], pscale=pool_scale)
    late = ([shard["w_mix_in"], conv_w[0]], [shard["ffn2_w_in"], shard["ffn2_w_out"]],
            [shard["w_gdn_proj"], shard["w_pool_proj"], shard["w_mix_out"]])

    loss_dev, dx_lat, g, landed, small, dmod = _local_step(ctx[0], x[0], loss_target[0], modv, p, late)
    loss = lax.psum(loss_dev, ("x", "y", "c"))
    grad_x = dx_lat[None]

    small_vals = [dmod[1], dmod[0], small["norm1"], small["norm2"], small["norm3"], small["fnorm"], small["a_log"], small["dt_bias"],
                  small["gnw"], small["pool_w"], small["pscale"], small["conv"]]
    small_sizes = [v.size for v in small_vals]
    packed = _all_gather8(_pack(small_vals), name="gather_small")
    tot = _unpack(_sum_slots(packed, name="sum_small"), small_sizes)
    dmod_lat_all = packed[:, :9 * D // 128, :].reshape(8, 9 * D)
    dm = jnp.concatenate([dmod_lat_all, tot[1][None], jnp.zeros((7, 9 * D), F32)], axis=0)
    dm_sh = lax.dynamic_slice(dm, (0, chip * ada_cols), (16, ada_cols))
    g_w_ada, cctx_part = _ada_bwd(c_all, dm_sh, w_ada[0], name="ada_bwd")
    g_c_ctx = _cctx_grad(_all_gather8(cctx_part, name="gather_cctx"), c_ctx[None], name="cctx_grad")[0]
    conv_tot = tot[11].reshape(5, NQKV)
    g_small = dict(c_ctx=g_c_ctx, b_ada=tot[0] + tot[1], norm1_w=tot[2], norm2_w=tot[3], norm3_w=tot[4], final_norm_w=tot[5],
                   a_log=tot[6], dt_bias=tot[7], gdn_norm_w=tot[8], pool_w=tot[9], pool_scale=tot[10],
                   conv_w=lax.dynamic_slice(conv_tot, (0, chip * (NQKV // NCHIP)), (5, NQKV // NCHIP)))

    first = ("ffn1_w_in", "ffn1_w_out")
    order = dict(zip(big, ("w1_in", "w1_out", "w_mix", "w_gdn", "w_pool", "w_mo", "w2_in", "w2_out")))
    rest = [k for k in big if k not in first]
    send_sems, recv_sems, sent, lands, token = _scatter_start([g["w1_in"], g["w1_out"]], [g_c_ctx, g_w_ada], name="scatter_ffn1_start")
    mine = {k: _sum_slots(landed[order[k]], name=f"sum_{k}", after=[token]) for k in rest}
    theirs = dict(zip(rest, _core_swap([mine[k] for k in rest], name="swap_grad_sums")))

    out = {}
    as2d = lambda a: a.reshape(-1, a.shape[-1])

    def update(k):
        res = _adamw(as2d(w[k]), as2d(mine[k]), as2d(theirs[k]), as2d(mom[k]), as2d(var[k]), name=f"adamw_{k}")
        out[k] = [r.reshape(w[k].shape) for r in res]

    for k in rest:
        update(k)
    out["w_ada"] = [r.reshape(w_ada.shape) for r in _adamw(w_ada[0], g_w_ada, None, m_w_ada[0], v_w_ada[0], name="adamw_w_ada")]
    sm_names = [n for n, _ in _SMALL]
    sm_sizes = [n for _, n in _SMALL]
    res = _adamw(_pack([w[k] for k in sm_names]), _pack([g_small[k] for k in sm_names]), None,
                 _pack([mom[k] for k in sm_names]), _pack([var[k] for k in sm_names]), name="adamw_small")
    done = [out[k][1] for k in rest] + [out["w_ada"][1], res[1]]
    res = [_unpack(r, sm_sizes) for r in res]
    for i, k in enumerate(sm_names):
        out[k] = [r[i].reshape(w[k].shape) for r in res]
    sent, lands = _scatter_wait(send_sems, recv_sems, sent, lands, done, name="scatter_ffn1_wait")
    for k, part, land in zip(first, sent, lands):
        own = lax.dynamic_slice_in_dim(part, chip, 1, axis=0)
        mine[k] = _sum_slots(lax.dynamic_update_slice_in_dim(land, own, chip, axis=0), name=f"sum_{k}")
    theirs.update(zip(first, _core_swap([mine[k] for k in first], name="swap_ffn1_sums")))
    for k in first:
        update(k)
    return (loss, grad_x, *[out[k][0] for k in names], *[out[k][1] for k in names], *[out[k][2] for k in names],
            *[out[k][3] for k in names])
```

```python
import functools

import jax
import jax.numpy as jnp
from jax import lax
from jax.experimental import pallas as pl
from jax.experimental.pallas import tpu as pltpu

F32 = jnp.float32
BF = jnp.bfloat16

D = 1024
FF = 2816
NH = 8
HD = 128
CH = 64
GW = 64
TM = 256
NQKV = 3 * NH * HD
NPOOL = 512
POOL_WINDOWS = (2, 4, 8, 16)
NMIX = 6688
NMIXP = 6784
EPS = 1e-6
NCHIP = 4
VMEM_LIMIT = 56 * 1024 * 1024

ADAM_LR, ADAM_B1, ADAM_B2, ADAM_EPS, ADAM_WD, ADAM_STEP = 0.001, 0.9, 0.999, 1e-08, 0.01, 10


def _cparams(*sem):
    return pltpu.CompilerParams(dimension_semantics=sem, vmem_limit_bytes=VMEM_LIMIT)


def _const_spec(shape):
    nd = len(shape)
    return pl.BlockSpec(shape, lambda *_: (0,) * nd, pipeline_mode=pl.Buffered(1))


def _dot(a, b, dims):
    return lax.dot_general(a.astype(BF), b.astype(BF), (dims, ((), ())), preferred_element_type=F32)


def _nn(a, b):
    return _dot(a, b, ((1,), (0,)))


def _nt(a, b):
    return _dot(a, b, ((1,), (1,)))


def _tn(a, b):
    return _dot(a, b, ((0,), (0,)))


def _silu(x):
    return x * jax.nn.sigmoid(x)


def _dsilu(x):
    s = jax.nn.sigmoid(x)
    return s * (1.0 + x * (1.0 - s))


def _norm_mod(x, nw, shift, scale):
    r = lax.rsqrt(jnp.mean(x * x, axis=-1, keepdims=True) + EPS)
    xh = x * r
    n = xh * nw
    return n * (1.0 + scale) + shift, n, xh, r


def _norm_mod_bwd(dh, n, xh, r, nw, scale):
    dn = dh * (1.0 + scale)
    dxh = dn * nw
    dx = r * (dxh - xh * jnp.mean(dxh * xh, axis=-1, keepdims=True))
    rs = lambda t: jnp.sum(t, axis=0, keepdims=True)
    return dx, rs(dh), rs(dh * n), rs(dn * xh)


def _stream_specs(ctx):
    if ctx is None:
        return [pl.BlockSpec((TM, D), lambda i: (i, 0))]
    return [pl.BlockSpec((TM, D), lambda i: (0, 0)), pl.BlockSpec((TM, D), lambda i: (jnp.maximum(i - 1, 0), 0))]


def _stream_tile(refs, lead):
    if not lead:
        return refs[0][...], refs[1:]
    return jnp.where(pl.program_id(0) == 0, refs[0][...], refs[1][...]), refs[2:]


def _loss_head(xv, target, w):
    r = lax.rsqrt(jnp.mean(xv * xv, axis=-1, keepdims=True) + EPS)
    xh = xv * r
    err = xh * w - target
    dy = err * (1.0 / D)
    dxh = dy * w
    dx = r * (dxh - xh * jnp.mean(dxh * xh, axis=-1, keepdims=True))
    return dx, jnp.sum(dy * xh, axis=0, keepdims=True), jnp.sum(err * err, axis=0, keepdims=True) * (0.5 / D)


def _ffn_fwd(x, modv, nw, w_in4, w_out, *, mrow, name, ctx=None, exchange=None, head=None):
    lead = ctx is not None
    n_tok = x.shape[0] + (TM if lead else 0)
    nt = n_tok // TM
    nset = modv.shape[0]
    ws = w_in4.shape[2]

    def body(*refs):
        xv, refs = _stream_tile(refs, lead)
        if head:
            (t_ref, fw_ref), refs, acc_ref = refs[:2], refs[2:-1], refs[-1]
        mod_ref, nw_ref, win_ref, wout_ref, x1_ref, h_ref, gu_ref, f_ref = refs
        shift, scale, gate = mod_ref[0, mrow:mrow + 1, :], mod_ref[0, mrow + 1:mrow + 2, :], mod_ref[0, mrow + 2:mrow + 3, :]
        h, _, _, _ = _norm_mod(xv, nw_ref[...], shift, scale)
        hb = h.astype(BF)
        h_ref[...] = hb
        gus = [_nn(hb, win_ref[s]) for s in range(NCHIP)]
        for s in range(NCHIP):
            gu_ref[:, s * ws:(s + 1) * ws] = gus[s].astype(BF)
        g = jnp.concatenate(gus[:2], axis=1)
        u = jnp.concatenate(gus[2:], axis=1)
        f = _nn(_silu(g) * u, wout_ref[...])
        f_ref[...] = f.astype(BF)
        out = xv + 0.5 * gate * f
        if not head:
            x1_ref[...] = out
            return
        x1_ref[...], dfw, loss = _loss_head(out, t_ref[...], fw_ref[...])

        @pl.when(pl.program_id(0) == 0)
        def _():
            acc_ref[...] = jnp.zeros_like(acc_ref)

        acc_ref[0:1, :] += dfw
        acc_ref[1:2, :] += loss

    tile = lambda w: pl.BlockSpec((TM, w), lambda i: (i, 0))
    return _pallas(
        body, (*([ctx] if lead else []), x, *(head or ()), modv, nw, w_in4, w_out), name=name, grid=(nt,),
        sem=("arbitrary",) if head else ("parallel",), exchange=exchange,
        in_specs=_stream_specs(ctx) + ([tile(D), _const_spec((1, D))] if head else [])
        + [pl.BlockSpec((1, 16, D), lambda i: (jnp.minimum(i, nset - 1), 0, 0)), _const_spec((1, D)),
           _const_spec(w_in4.shape), _const_spec(w_out.shape)],
        out_specs=[tile(D), tile(D), tile(2 * FF), tile(D)] + ([pl.BlockSpec((8, D), lambda i: (0, 0))] if head else []),
        out_shape=[jax.ShapeDtypeStruct((n_tok, D), F32), jax.ShapeDtypeStruct((n_tok, D), BF),
                   jax.ShapeDtypeStruct((n_tok, 2 * FF), BF), jax.ShapeDtypeStruct((n_tok, D), BF)]
        + ([jax.ShapeDtypeStruct((8, D), F32)] if head else []))


def _ffn_bwd(dxo, x, gu, fo, modv, nw, w_in4, w_out, *, mrow, name, ctx=None, exchange=None):
    lead = ctx is not None
    dx_skip = int(lead)
    n_tok = x.shape[0] + dx_skip * TM
    nt = n_tok // TM
    nset = modv.shape[0]
    ws = w_in4.shape[2]

    def body(*refs):
        xv, (dxo_ref, gu_ref, f_ref, mod_ref, nw_ref, win_ref, wout_ref, dx_ref, a_ref, df_ref, dgu_ref, acc_ref) = _stream_tile(refs, lead)
        i = pl.program_id(0)
        dxo_v = dxo_ref[...]
        shift, scale, gate = mod_ref[0, mrow:mrow + 1, :], mod_ref[0, mrow + 1:mrow + 2, :], mod_ref[0, mrow + 2:mrow + 3, :]
        _, n, xh, r = _norm_mod(xv, nw_ref[...], shift, scale)
        df = 0.5 * gate * dxo_v
        dfb = df.astype(BF)
        df_ref[...] = dfb
        dgate = jnp.sum(0.5 * dxo_v * f_ref[...].astype(F32), axis=0, keepdims=True)
        da = _nt(dfb, wout_ref[...])
        g = gu_ref[:, :FF].astype(F32)
        u = gu_ref[:, FF:].astype(F32)
        sig = jax.nn.sigmoid(g)
        sg = g * sig
        a_ref[...] = (sg * u).astype(BF)
        dgu_ref[:, :FF] = (da * u * (sig * (1.0 + g - sg))).astype(BF)
        dgu_ref[:, FF:] = (da * sg).astype(BF)
        dh = _nt(dgu_ref[:, 0:ws], win_ref[0])
        for s in range(1, NCHIP):
            dh = dh + _nt(dgu_ref[:, s * ws:(s + 1) * ws], win_ref[s])
        dx, dshift, dscale, dnw = _norm_mod_bwd(dh, n, xh, r, nw_ref[...], scale)
        dx_ref[...] = dxo_v + dx

        @pl.when((i == 0) | (i == nset - 1))
        def _():
            acc_ref[...] = jnp.zeros_like(acc_ref)

        acc_ref[0, 0:1, :] += dshift
        acc_ref[0, 1:2, :] += dscale
        acc_ref[0, 2:3, :] += dgate
        acc_ref[0, 3:4, :] += dnw

    tile = lambda w: pl.BlockSpec((TM, w), lambda i: (i, 0))
    return _pallas(
        body, (*([ctx] if lead else []), x, dxo, gu, fo, modv, nw, w_in4, w_out), name=name, grid=(nt,), sem=("arbitrary",),
        exchange=exchange,
        in_specs=_stream_specs(ctx) + [tile(D), tile(2 * FF), tile(D),
                                       pl.BlockSpec((1, 16, D), lambda i: (jnp.minimum(i, nset - 1), 0, 0)), _const_spec((1, D)),
                                       _const_spec(w_in4.shape), _const_spec(w_out.shape)],
        out_specs=[pl.BlockSpec((TM, D), lambda i: (jnp.maximum(i - dx_skip, 0), 0)), tile(FF), tile(D), tile(2 * FF),
                   pl.BlockSpec((1, 8, D), lambda i: (jnp.minimum(i, nset - 1), 0, 0))],
        out_shape=[jax.ShapeDtypeStruct((n_tok - dx_skip * TM, D), F32), jax.ShapeDtypeStruct((n_tok, FF), BF),
                   jax.ShapeDtypeStruct((n_tok, D), BF), jax.ShapeDtypeStruct((n_tok, 2 * FF), BF),
                   jax.ShapeDtypeStruct((nset, 8, D), F32)])


def _k_tile(n, target=3072):
    return max(t for t in range(TM, min(n, target) + 1, TM) if n % t == 0)


def _matmul_tn(a, b, *, tmm, tn, tk, nsplit=1, name):
    n_tok, m = a.shape
    kk = b.shape[1]
    nk = n_tok // tk

    def body(a_ref, b_ref, o_ref, acc):
        k = pl.program_id(2)

        @pl.when(k == 0)
        def _():
            acc[...] = jnp.zeros_like(acc)

        acc[...] += _tn(a_ref[...], b_ref[...])

        @pl.when(k == nk - 1)
        def _():
            o_ref[...] = acc[...].astype(BF).reshape(o_ref.shape)

    if nsplit == 1:
        out_shape = jax.ShapeDtypeStruct((m, kk), BF)
        out_spec = pl.BlockSpec((tmm, tn), lambda i, j, k: (i, j))
    else:
        assert tn == kk // nsplit
        out_shape = jax.ShapeDtypeStruct((nsplit, m, tn), BF)
        out_spec = pl.BlockSpec((1, tmm, tn), lambda i, j, k: (j, i, 0))
    return pl.pallas_call(
        body, name=name, grid=(m // tmm, kk // tn, nk),
        in_specs=[pl.BlockSpec((tk, tmm), lambda i, j, k: (k, i)), pl.BlockSpec((tk, tn), lambda i, j, k: (k, j))],
        out_specs=out_spec, out_shape=out_shape,
        scratch_shapes=[pltpu.VMEM((tmm, tn), F32)],
        compiler_params=_cparams("parallel", "parallel", "arbitrary"),
    )(a, b)


_MIX_PARTS = (("qkv", 0, NQKV), ("gate", NQKV, 1024), ("pool", NQKV + 1024, NPOOL), ("br", NQKV + 1024 + NPOOL, 2048),
              ("ab", NMIXP - 128, 128))


def _mix_in_fwd(x1, modv, nw, w_mix, *, name, exchange=None):
    n_tok = x1.shape[0]

    def body(x_ref, mod_ref, nw_ref, w_ref, u_ref, *p_refs):
        u, _, _, _ = _norm_mod(x_ref[...], nw_ref[...], mod_ref[0, 3:4, :], mod_ref[0, 4:5, :])
        ub = u.astype(BF)
        u_ref[...] = ub
        for (_, c0, w), p_ref in zip(_MIX_PARTS, p_refs):
            p_ref[...] = _nn(ub, w_ref[:, c0:c0 + w])

    tile = lambda w: pl.BlockSpec((TM, w), lambda i: (i, 0))
    return _pallas(
        body, (x1, modv, nw, w_mix), name=name, grid=(n_tok // TM,), sem=("parallel",), exchange=exchange,
        in_specs=[tile(D), pl.BlockSpec((1, 16, D), lambda i: (jnp.minimum(i, 1), 0, 0)), _const_spec((1, D)),
                  _const_spec(w_mix.shape)],
        out_specs=[tile(D)] + [tile(w) for _, _, w in _MIX_PARTS],
        out_shape=[jax.ShapeDtypeStruct((n_tok, D), BF)] + [jax.ShapeDtypeStruct((n_tok, w), F32) for _, _, w in _MIX_PARTS])


def _mix_in_bwd(dxo, x1, dqkv, dgate, dpool, dbr, dab_f, dab_b, modv, nw, w_mix, *, name):
    n_tok = x1.shape[0]

    def body(dxo_ref, x_ref, dqkv_ref, dgate_ref, dpool_ref, dbr_ref, dabf_ref, dabb_ref, mod_ref, nw_ref, w_ref,
             dx_ref, dp_ref, acc_ref):
        i = pl.program_id(0)
        lat = i >= 1
        scale = mod_ref[0, 4:5, :]
        _, n, xh, r = _norm_mod(x_ref[...], nw_ref[...], mod_ref[0, 3:4, :], scale)
        dp_ref[:, 0:NQKV] = dqkv_ref[...].astype(BF)
        dp_ref[:, NQKV:NQKV + 1024] = jnp.where(lat, dgate_ref[...], 0.0).astype(BF)
        dp_ref[:, NQKV + 1024:NQKV + 1536] = jnp.where(lat, dpool_ref[...], 0.0).astype(BF)
        dp_ref[:, NQKV + 1536:NMIXP - 128] = jnp.where(lat, dbr_ref[...], 0.0).astype(BF)
        dp_ref[:, NMIXP - 128:] = (dabf_ref[...] + dabb_ref[...]).astype(BF)
        du = _nt(dp_ref[...], w_ref[...])
        dx, dshift, dscale, dnw = _norm_mod_bwd(du, n, xh, r, nw_ref[...], scale)
        dx_ref[...] = jnp.where(lat, dxo_ref[...], 0.0) + dx

        @pl.when(i <= 1)
        def _():
            acc_ref[...] = jnp.zeros_like(acc_ref)

        acc_ref[0, 0:1, :] += dshift
        acc_ref[0, 1:2, :] += dscale
        acc_ref[0, 3:4, :] += dnw

    tile = lambda w: pl.BlockSpec((TM, w), lambda i: (i, 0))
    ltile = lambda w: pl.BlockSpec((TM, w), lambda i: (jnp.maximum(i - 1, 0), 0))
    return pl.pallas_call(
        body, name=name, grid=(n_tok // TM,),
        in_specs=[ltile(D), tile(D), tile(NQKV), ltile(1024), ltile(NPOOL), ltile(2048), tile(128), tile(128),
                  pl.BlockSpec((1, 16, D), lambda i: (jnp.minimum(i, 1), 0, 0)), _const_spec((1, D)), _const_spec(w_mix.shape)],
        out_specs=[tile(D), tile(NMIXP), pl.BlockSpec((1, 8, D), lambda i: (jnp.minimum(i, 1), 0, 0))],
        out_shape=[jax.ShapeDtypeStruct((n_tok, D), F32), jax.ShapeDtypeStruct((n_tok, NMIXP), BF),
                   jax.ShapeDtypeStruct((2, 8, D), F32)],
        compiler_params=_cparams("arbitrary"),
    )(dxo, x1, dqkv, dgate, dpool, dbr, dab_f, dab_b, modv, nw, w_mix)


def _qkv_act(pre, part):
    s = _silu(pre)
    if part == 2:
        return s
    nrm = s * lax.rsqrt(jnp.sum(s * s, axis=-1, keepdims=True) + EPS)
    return nrm * HD ** -0.5 if part == 0 else nrm


def _halo_specs(nt):
    r = TM // 8
    main = pl.BlockSpec((TM, NQKV), lambda i: (i, 0))
    prev = pl.BlockSpec((8, NQKV), lambda i: (jnp.maximum(i * r - 1, 0), 0))
    nxt = pl.BlockSpec((8, NQKV), lambda i: (jnp.minimum((i + 1) * r, nt * r - 1), 0))
    return main, prev, nxt


def _prep_fwd(p_qkv, conv_w8, *, name):
    n_tok = p_qkv.shape[0]
    nt = n_tok // TM

    def body(x_ref, xp_ref, xn_ref, w_ref, o_ref, pre_ref, win):
        i = pl.program_id(0)
        has_prev = (i != 0) & (i != 1)
        has_next = (i != 0) & (i != nt - 1)
        win[0:8, :] = jnp.where(has_prev, xp_ref[...], 0.0)
        win[8:8 + TM, :] = x_ref[...]
        win[8 + TM:, :] = jnp.where(has_next, xn_ref[...], 0.0)
        for hb in range(3 * NH):
            hs = slice(hb * HD, (hb + 1) * HD)
            pre = win[6:6 + TM, hs] * w_ref[0:1, hs]
            for k in range(1, 5):
                pre = pre + win[6 + k:6 + k + TM, hs] * w_ref[k:k + 1, hs]
            pre_ref[:, hs] = pre
            o_ref[:, hs] = _qkv_act(pre, hb // NH)

    main, prev, nxt = _halo_specs(nt)
    return pl.pallas_call(
        body, name=name, grid=(nt,),
        in_specs=[main, prev, nxt, pl.BlockSpec((8, NQKV), lambda i: (0, 0))],
        out_specs=[main, main], out_shape=[jax.ShapeDtypeStruct((n_tok, NQKV), F32)] * 2,
        scratch_shapes=[pltpu.VMEM((TM + 16, NQKV), F32)],
        compiler_params=_cparams("parallel"),
    )(p_qkv, p_qkv, p_qkv, conv_w8)


def _prep_bwd(p_qkv, pre, dqkv_f, dqkv_b, conv_w8, *, name):
    n_tok = p_qkv.shape[0]
    nt = n_tok // TM

    def body(x_ref, p_ref, pp_ref, pn_ref, g_ref, gp_ref, gn_ref, g2_ref, g2p_ref, g2n_ref, w_ref, dx_ref, dw_ref, pwin, gwin, dwin):
        i = pl.program_id(0)
        has_prev = (i != 0) & (i != 1)
        has_next = (i != 0) & (i != nt - 1)
        pwin[0:8, :] = jnp.where(has_prev, pp_ref[...], 0.0)
        pwin[8:8 + TM, :] = p_ref[...]
        pwin[8 + TM:, :] = jnp.where(has_next, pn_ref[...], 0.0)
        gwin[0:8, :] = jnp.where(has_prev, gp_ref[...] + g2p_ref[...], 0.0)
        gwin[8:8 + TM, :] = g_ref[...] + g2_ref[...]
        gwin[8 + TM:, :] = jnp.where(has_next, gn_ref[...] + g2n_ref[...], 0.0)

        @pl.when(i == 0)
        def _():
            dw_ref[...] = jnp.zeros_like(dw_ref)

        for hb in range(3 * NH):
            hs = slice(hb * HD, (hb + 1) * HD)
            _, vjp = jax.vjp(functools.partial(_qkv_act, part=hb // NH), pwin[:, hs])
            dwin[:, hs] = vjp(gwin[:, hs])[0]
            xv = x_ref[:, hs]
            dx = None
            for k in range(5):
                sh = dwin[10 - k:10 - k + TM, hs]
                dx = sh * w_ref[k:k + 1, hs] if dx is None else dx + sh * w_ref[k:k + 1, hs]
                dw_ref[k:k + 1, hs] += jnp.sum(sh * xv, axis=0, keepdims=True)
            dx_ref[:, hs] = dx.astype(BF)

    main, prev, nxt = _halo_specs(nt)
    wspec = pl.BlockSpec((8, NQKV), lambda i: (0, 0))
    return pl.pallas_call(
        body, name=name, grid=(nt,),
        in_specs=[main, main, prev, nxt, main, prev, nxt, main, prev, nxt, wspec],
        out_specs=[main, wspec],
        out_shape=[jax.ShapeDtypeStruct((n_tok, NQKV), BF), jax.ShapeDtypeStruct((8, NQKV), F32)],
        scratch_shapes=[pltpu.VMEM((TM + 16, NQKV), F32)] * 3,
        compiler_params=_cparams("arbitrary"),
    )(p_qkv, pre, pre, pre, dqkv_f, dqkv_f, dqkv_f, dqkv_b, dqkv_b, dqkv_b, conv_w8)


@jax.custom_vjp
def _mm_nn(a, b):
    return _nn(a, b)


@jax.custom_vjp
def _mm_nt(a, b):
    return _nt(a, b)


@jax.custom_vjp
def _mm_tn(a, b):
    return _tn(a, b)


_mm_nn.defvjp(lambda a, b: (_nn(a, b), (a, b)), lambda r, g: (_mm_nt(g, r[1]), _mm_tn(r[0], g)))
_mm_nt.defvjp(lambda a, b: (_nt(a, b), (a, b)), lambda r, g: (_mm_nn(g, r[1]), _mm_tn(g, r[0])))
_mm_tn.defvjp(lambda a, b: (_tn(a, b), (a, b)), lambda r, g: (_mm_nt(r[1], g), _mm_nn(r[0], g)))


def _each(f, *lists):
    return tuple(f(*a) for a in zip(*lists))


def _unit_tri_inv(ls, revs):
    ii = lax.broadcasted_iota(jnp.int32, (CH, CH), 0)
    jj = lax.broadcasted_iota(jnp.int32, (CH, CH), 1)
    eye = (ii == jj).astype(F32)
    xs = None
    s = 1
    while s < CH:
        same = (ii & -(2 * s)) == (jj & -(2 * s))
        off = {False: same & ((ii & s) != 0) & ((jj & s) == 0), True: same & ((jj & s) != 0) & ((ii & s) == 0)}
        cs = _each(lambda l, r: jnp.where(off[r], l, 0.0), ls, revs)
        if xs is None:
            xs = _each(lambda c: eye - c, cs)
        else:
            xc = _each(_nn, xs, cs)
            xcx = _each(_nn, xc, xs)
            xs = _each(lambda x, t: x - t, xs, xcx)
        s *= 2
    return xs


@functools.lru_cache(maxsize=None)
def _tri_solve(revs):
    @jax.custom_vjp
    def solve(ls, rhss):
        return _each(_mm_nn, _unit_tri_inv(ls, revs), rhss)

    def fwd(ls, rhss):
        ainv = _unit_tri_inv(ls, revs)
        xs = _each(_mm_nn, ainv, rhss)
        return xs, (ainv, xs)

    def bwd(res, gs):
        ainv, xs = res
        drhs = _each(_mm_tn, ainv, gs)
        return _each(lambda d, x: -_mm_nt(d, x), drhs, xs), drhs

    solve.defvjp(fwd, bwd)
    return solve


def _chunk_prep(q, k, v, beta, g, *, revs):
    ii = lax.broadcasted_iota(jnp.int32, (CH, CH), 0)
    jj = lax.broadcasted_iota(jnp.int32, (CH, CH), 1)
    eye = ii == jj
    incl_of = {False: ii >= jj, True: ii <= jj}
    strict_of = {False: ii > jj, True: ii < jj}
    g_row = _each(lambda t: jnp.sum(jnp.where(eye, t, 0.0), axis=0, keepdims=True), g)
    cum = _each(lambda t, r: jnp.sum(jnp.where(incl_of[r], t, 0.0), axis=1, keepdims=True), g_row, revs)
    cum_row = _each(lambda t: jnp.sum(jnp.where(eye, t, 0.0), axis=0, keepdims=True), cum)
    total = _each(lambda t: jnp.sum(t, axis=0, keepdims=True), g)
    decay = _each(lambda c, cr, r: jnp.where(incl_of[r], jnp.exp(jnp.where(incl_of[r], c - cr, 0.0)), 0.0), cum, cum_row, revs)
    kb = _each(jnp.multiply, k, beta)
    vb = _each(jnp.multiply, v, beta)
    kk = _each(_mm_nt, kb, k)
    lmat = _each(lambda t, dc, r: jnp.where(strict_of[r], t * dc, 0.0), kk, decay, revs)
    ecum = _each(jnp.exp, cum)
    rhs = _each(lambda a, b, e: jnp.concatenate([a, b * e], axis=1), vb, kb, ecum)
    sol = _tri_solve(revs)(lmat, rhs)
    qk = _each(_mm_nt, q, k)
    aqk = _each(jnp.multiply, qk, decay)
    qd = _each(jnp.multiply, q, ecum)
    kd = _each(lambda a, t, c: a * jnp.exp(t - c), k, total, cum)
    return sol, aqk, qd, kd, _each(jnp.exp, total)


def _chunk_rec(sol, aqk, qd, kd, bl, s):
    ws = _each(lambda so, st: _mm_nn(so[:, HD:], st), sol, s)
    v_new = _each(lambda so, t: so[:, :HD] - t, sol, ws)
    qs = _each(_mm_nn, qd, s)
    av = _each(_mm_nn, aqk, v_new)
    o = _each(jnp.add, qs, av)
    kv = _each(_mm_tn, kd, v_new)
    s_new = _each(lambda st, b, u: st * b + u, s, bl, kv)
    return o, s_new


def _lane_col(x, c):
    lane = lax.broadcasted_iota(jnp.int32, x.shape, 1)
    return jnp.sum(jnp.where(lane == c, x, 0.0), axis=1, keepdims=True)


def _gates(ab, cst):
    z = ab + cst[1:2, :]
    softplus = jnp.maximum(z, 0.0) + jnp.log(1.0 + jnp.exp(-jnp.abs(z)))
    return jax.nn.sigmoid(ab), -cst[0:1, :] * softplus, -cst[0:1, :] * jax.nn.sigmoid(z)


def _beta_g(gates, d, h):
    sig, g, dg = gates
    return _lane_col(sig, NH * d + h), _lane_col(g, 2 * NH + NH * d + h), _lane_col(dg, 2 * NH + NH * d + h)


STEPS = 2
TS = STEPS * CH
_CHAINS = tuple((t, d, h) for t in range(STEPS) for d in (0, 1) for h in range(NH))
_REVS = tuple(bool(d) for _, d, _ in _CHAINS)
_PER_STEP = 2 * NH


def _chain_inputs(refs, r0s, ab_refs, cst):
    hs = lambda h: slice(h * HD, (h + 1) * HD)
    gates = [[_gates(ab_refs[d][pl.ds(r0s[t][d], CH), :], cst) for d in (0, 1)] for t in range(STEPS)]
    q = _each(lambda c: refs[c[1]][0][pl.ds(r0s[c[0]][c[1]], CH), hs(c[2])], _CHAINS)
    k = _each(lambda c: refs[c[1]][1][pl.ds(r0s[c[0]][c[1]], CH), hs(c[2])], _CHAINS)
    v = _each(lambda c: refs[c[1]][2][pl.ds(r0s[c[0]][c[1]], CH), hs(c[2])], _CHAINS)
    bg = _each(lambda c: _beta_g(gates[c[0]][c[1]], c[1], c[2]), _CHAINS)
    return q, k, v, bg


def _of_step(parts, t):
    return tuple(p[t * _PER_STEP:(t + 1) * _PER_STEP] for p in parts)


def _scan_fwd(qkv, ab, cst, s0, *, row_blk0, nb, name, exchange=None):
    cb = TS // CH
    w = NH * HD

    def body(qf, kf, vf, abf, qb, kb, vb, abb, cst_ref, s0_ref, of_ref, ob_ref, sallf_ref, sallb_ref, sfin_ref, s_scr):
        i = pl.program_id(0)

        @pl.when(i == 0)
        def _():
            s_scr[...] = s0_ref[...]

        o_refs, sall_refs = (of_ref, ob_ref), (sallf_ref, sallb_ref)

        def chunks(ci, carry):
            cs = [(ci * STEPS + t, cb - 1 - ci * STEPS - t) for t in range(STEPS)]
            r0s = [tuple(pl.multiple_of(c * CH, CH) for c in ct) for ct in cs]
            q, k, v, bg = _chain_inputs(((qf, kf, vf), (qb, kb, vb)), r0s, (abf, abb), cst_ref[...])
            parts = _chunk_prep(q, k, v, _each(lambda t: t[0], bg), _each(lambda t: t[1], bg), revs=_REVS)
            s = _each(lambda c: s_scr[c[1], c[2]], _CHAINS[:_PER_STEP])
            for t in range(STEPS):
                for (_, d, h), sv in zip(_CHAINS, s):
                    sall_refs[d][cs[t][d], h] = sv
                o, s = _chunk_rec(*_of_step(parts, t), s)
                for (_, d, h), ov in zip(_CHAINS, o):
                    o_refs[d][pl.ds(r0s[t][d], CH), h * HD:(h + 1) * HD] = ov
            for (_, d, h), sv in zip(_CHAINS, s):
                s_scr[d, h] = sv
            return carry

        lax.fori_loop(0, cb // STEPS, chunks, 0)

        @pl.when(i == nb - 1)
        def _():
            sfin_ref[...] = s_scr[...]

    pos = (lambda i: i, lambda i: nb - 1 - i)
    col = lambda d, c: pl.BlockSpec((TS, w), lambda i: (row_blk0 + pos[d](i), c))
    abs_ = lambda d: pl.BlockSpec((TS, 128), lambda i: (row_blk0 + pos[d](i), 0))
    full4 = pl.BlockSpec((2, NH, HD, HD), lambda i: (0, 0, 0, 0))
    o_spec = lambda d: pl.BlockSpec((TS, w), lambda i: (pos[d](i), 0))
    sall_spec = lambda d: pl.BlockSpec((cb, NH, HD, HD), lambda i: (pos[d](i), 0, 0, 0))
    return _pallas(
        body, (qkv, qkv, qkv, ab, qkv, qkv, qkv, ab, cst, s0), name=name, grid=(nb,), sem=("arbitrary",), exchange=exchange,
        in_specs=[col(0, 0), col(0, 1), col(0, 2), abs_(0), col(1, 0), col(1, 1), col(1, 2), abs_(1),
                  pl.BlockSpec((8, 128), lambda i: (0, 0)), full4],
        out_specs=[o_spec(0), o_spec(1), sall_spec(0), sall_spec(1), full4],
        out_shape=[jax.ShapeDtypeStruct((nb * TS, w), F32)] * 2 + [jax.ShapeDtypeStruct((nb * cb, NH, HD, HD), F32)] * 2
        + [jax.ShapeDtypeStruct((2, NH, HD, HD), F32)],
        scratch_shapes=[pltpu.VMEM((2, NH, HD, HD), F32)])


def _scan_bwd(qkv, ab, cst, sall_f, sall_b, do, dsfin, dqkv_f, dqkv_b, dab_f, dab_b, dcst, *, row_blk0, nb, has_do, name,
              exchange=None):
    cb = TS // CH
    w = NH * HD

    def body(qf, kf, vf, abf, qb, kb, vb, abb, cst_ref, sallf_ref, sallb_ref, dof_ref, dob_ref, dsfin_ref, _f, _b, _af, _ab, dcst_in,
             dqkvf_ref, dqkvb_ref, dabf_ref, dabb_ref, dcst_ref, ds0_ref, ds_scr):
        i = pl.program_id(0)

        @pl.when(i == 0)
        def _():
            ds_scr[...] = dsfin_ref[...]
            dcst_ref[...] = dcst_in[...]

        lane = lax.broadcasted_iota(jnp.int32, (CH, 128), 1)
        lane1 = lax.broadcasted_iota(jnp.int32, (1, 128), 1)
        sall_refs, do_refs = (sallf_ref, sallb_ref), (dof_ref, dob_ref)
        dqkv_refs, dab_refs = (dqkvf_ref, dqkvb_ref), (dabf_ref, dabb_ref)

        def chunks(ci, carry):
            cs = [(cb - 1 - ci * STEPS - t, ci * STEPS + t) for t in range(STEPS)]
            r0s = [tuple(pl.multiple_of(c * CH, CH) for c in ct) for ct in cs]
            q, k, v, bg = _chain_inputs(((qf, kf, vf), (qb, kb, vb)), r0s, (abf, abb), cst_ref[...])
            beta, g = _each(lambda t: t[0], bg), _each(lambda t: t[1], bg)
            parts, prep_vjp = jax.vjp(functools.partial(_chunk_prep, revs=_REVS), q, k, v, beta, g)
            ds = _each(lambda c: ds_scr[c[1], c[2]], _CHAINS[:_PER_STEP])
            dparts = []
            for t in range(STEPS):
                s = _each(lambda c: sall_refs[c[1]][cs[t][c[1]], c[2]], _CHAINS[:_PER_STEP])
                _, rec_vjp = jax.vjp(_chunk_rec, *_of_step(parts, t), s)
                do_t = _each(lambda c: do_refs[c[1]][pl.ds(r0s[t][c[1]], CH), c[2] * HD:(c[2] + 1) * HD] if has_do
                             else jnp.zeros((CH, HD), F32), _CHAINS[:_PER_STEP])
                *dpt, ds = rec_vjp((do_t, ds))
                dparts.append(dpt)
            for (_, d, h), dsv in zip(_CHAINS, ds):
                ds_scr[d, h] = dsv
            dq, dk, dv, dbeta, dg = prep_vjp(tuple(sum((dparts[t][j] for t in range(STEPS)), ()) for j in range(len(dparts[0]))))
            dab = [[jnp.zeros((CH, 128), F32), jnp.zeros((CH, 128), F32)] for _ in range(STEPS)]
            dal = jnp.zeros((1, 128), F32)
            for n, (t, d, h) in enumerate(_CHAINS):
                for part, val in enumerate((dq[n], dk[n], dv[n])):
                    dqkv_refs[d][pl.ds(r0s[t][d], CH), part * w + h * HD:part * w + (h + 1) * HD] = val
                dbraw = dbeta[n] * beta[n] * (1.0 - beta[n])
                daraw = dg[n] * bg[n][2]
                dab[t][d] = dab[t][d] + jnp.where(lane == NH * d + h, dbraw, 0.0) + jnp.where(lane == 2 * NH + NH * d + h, daraw, 0.0)
                dal = dal + jnp.where(lane1 == 2 * NH + NH * d + h, jnp.sum(dg[n] * g[n], axis=0, keepdims=True), 0.0)
            dsum = jnp.zeros((CH, 128), F32)
            for t in range(STEPS):
                for d in (0, 1):
                    dab_refs[d][pl.ds(r0s[t][d], CH), :] = dab[t][d]
                    dsum = dsum + dab[t][d]
            dcst_ref[0:1, :] += dal
            dcst_ref[1:2, :] += jnp.sum(jnp.where(lane >= 2 * NH, dsum, 0.0), axis=0, keepdims=True)
            return carry

        lax.fori_loop(0, cb // STEPS, chunks, 0)

        @pl.when(i == nb - 1)
        def _():
            ds0_ref[...] = ds_scr[...]

    pos = (lambda i: nb - 1 - i, lambda i: i)
    col = lambda d, c: pl.BlockSpec((TS, w), lambda i: (row_blk0 + pos[d](i), c))
    abs_ = lambda d: pl.BlockSpec((TS, 128), lambda i: (row_blk0 + pos[d](i), 0))
    full4 = pl.BlockSpec((2, NH, HD, HD), lambda i: (0, 0, 0, 0))
    small = pl.BlockSpec((8, 128), lambda i: (0, 0))
    hbm = pl.BlockSpec(memory_space=pl.ANY)
    sall_spec = lambda d: pl.BlockSpec((cb, NH, HD, HD), lambda i: (pos[d](i), 0, 0, 0))
    do_spec = (lambda d: pl.BlockSpec((TS, w), lambda i: (pos[d](i), 0))) if has_do else (lambda d: small)
    acc_specs = [pl.BlockSpec((TS, 3 * w), lambda i: (row_blk0 + pos[0](i), 0)),
                 pl.BlockSpec((TS, 3 * w), lambda i: (row_blk0 + pos[1](i), 0)), abs_(0), abs_(1), small]
    return _pallas(
        body, (qkv, qkv, qkv, ab, qkv, qkv, qkv, ab, cst, sall_f, sall_b, do, do, dsfin, dqkv_f, dqkv_b, dab_f, dab_b, dcst),
        name=name, grid=(nb,), sem=("arbitrary",), exchange=exchange,
        in_specs=[col(0, 0), col(0, 1), col(0, 2), abs_(0), col(1, 0), col(1, 1), col(1, 2), abs_(1), small,
                  sall_spec(0), sall_spec(1), do_spec(0), do_spec(1), full4, hbm, hbm, hbm, hbm, small],
        out_specs=acc_specs + [full4],
        out_shape=[jax.ShapeDtypeStruct(dqkv_f.shape, F32), jax.ShapeDtypeStruct(dqkv_b.shape, F32),
                   jax.ShapeDtypeStruct(dab_f.shape, F32), jax.ShapeDtypeStruct(dab_b.shape, F32),
                   jax.ShapeDtypeStruct((8, 128), F32), jax.ShapeDtypeStruct((2, NH, HD, HD), F32)],
        aliases={14: 0, 15: 1, 16: 2, 17: 3, 18: 4},
        scratch_shapes=[pltpu.VMEM((2, NH, HD, HD), F32)])


def _pool(xin, *, row0, transpose, name):
    n_tok = xin.shape[0] - row0
    rows = n_tok // GW
    pad = 8 * GW
    tt = 512
    gsh = GW.bit_length() - 1

    def body(x_hbm, o_hbm, x_ref, o_ref, ybuf, in_sems, out_sems):
        cols = [slice(gi * 128, (gi + 1) * 128) for gi in range(len(POOL_WINDOWS))]
        fetch = [pltpu.make_async_copy(x_hbm.at[:, cs], x_ref.at[:, cs], in_sems.at[gi]) for gi, cs in enumerate(cols)]
        store = [pltpu.make_async_copy(o_ref.at[:, cs], o_hbm.at[:, cs], out_sems.at[gi]) for gi, cs in enumerate(cols)]
        for cp in fetch:
            cp.start()
        ii = lax.broadcasted_iota(jnp.int32, (128, 128), 0)
        jj = lax.broadcasted_iota(jnp.int32, (128, 128), 1)
        same_row = (ii >> gsh) == (jj >> gsh)
        ci, cj = ii & (GW - 1), jj & (GW - 1)
        tok = lax.broadcasted_iota(jnp.int32, (tt, 1), 0)
        zpad = jnp.zeros((pad, 128), F32)
        for gi, wdw in enumerate(POOL_WINDOWS):
            lo, hi = wdw // 2, wdw - wdw // 2
            if transpose:
                band = same_row & (ci - cj >= -lo) & (ci - cj < hi)
                offs = range(-hi + 1, lo + 1)
            else:
                band = same_row & (cj - ci >= -lo) & (cj - ci < hi)
                offs = range(-lo, hi)
            bandm = band.astype(BF)
            cs = cols[gi]
            ybuf[0:pad, :] = zpad
            ybuf[pad + n_tok:, :] = zpad
            fetch[gi].wait()

            def inv_area(t0):
                t = t0 + tok
                r, c = t >> gsh, t & (GW - 1)
                nr = jnp.minimum(r + hi, rows) - jnp.maximum(r - lo, 0)
                nc = jnp.minimum(c + hi, GW) - jnp.maximum(c - lo, 0)
                return 1.0 / (nr * nc).astype(F32)

            def col_pass(b, carry):
                t0 = pl.multiple_of(b * tt, tt)
                xv = x_ref[pl.ds(row0 + t0, tt), cs]
                if transpose:
                    xv = xv * inv_area(t0)
                hi_part = xv.astype(BF)
                lo_part = (xv - hi_part.astype(F32)).astype(BF)
                for s in range(tt // 128):
                    sl = slice(s * 128, (s + 1) * 128)
                    y = (jnp.dot(bandm, hi_part[sl], preferred_element_type=F32)
                         + jnp.dot(bandm, lo_part[sl], preferred_element_type=F32))
                    ybuf[pl.ds(pad + t0 + s * 128, 128), :] = y
                return carry

            lax.fori_loop(0, n_tok // tt, col_pass, 0)

            def row_pass(b, carry):
                t0 = pl.multiple_of(b * tt, tt)
                acc = ybuf[pl.ds(pad + t0 + offs[0] * GW, tt), :]
                for dr in offs[1:]:
                    acc = acc + ybuf[pl.ds(pad + t0 + dr * GW, tt), :]
                xv = x_ref[pl.ds(row0 + t0, tt), cs]
                if not transpose:
                    acc = acc * inv_area(t0)
                o_ref[pl.ds(t0, tt), cs] = acc - xv
                return carry

            lax.fori_loop(0, n_tok // tt, row_pass, 0)
            store[gi].start()
        for cp in store:
            cp.wait()

    groups = len(POOL_WINDOWS)
    return pl.pallas_call(
        body, name=name, out_shape=jax.ShapeDtypeStruct((n_tok, NPOOL), F32),
        in_specs=[pl.BlockSpec(memory_space=pl.ANY)], out_specs=pl.BlockSpec(memory_space=pl.ANY),
        scratch_shapes=[pltpu.VMEM(xin.shape, F32), pltpu.VMEM((n_tok, NPOOL), F32), pltpu.VMEM((n_tok + 2 * pad, 128), F32),
                        pltpu.SemaphoreType.DMA((groups,)), pltpu.SemaphoreType.DMA((groups,))],
        compiler_params=pltpu.CompilerParams(vmem_limit_bytes=VMEM_LIMIT),
    )(xin)


def _merge_parts(of, ob, pgate, pd, br, gnw, pw_ref, pscale, wg_ref, wp_ref):
    o = of + ob
    ons, ohs, rs = [], [], []
    for h in range(NH):
        oh = o[:, h * HD:(h + 1) * HD]
        r = lax.rsqrt(jnp.mean(oh * oh, axis=-1, keepdims=True) + EPS)
        ohs.append(oh * r)
        rs.append(r)
        ons.append(oh * r * gnw)
    on = jnp.concatenate(ons, axis=1)
    sig_gate = jax.nn.sigmoid(pgate)
    silu_gate = pgate * sig_gate
    og = on * silu_gate
    y_gdn = _nn(og, wg_ref[...])
    ypre = jnp.concatenate([_nn(pd[:, g * 128:(g + 1) * 128], pw_ref[g]) for g in range(4)], axis=1)
    yp = ypre * pscale
    y_pool = _nn(yp, wp_ref[...])
    g_pool = jax.nn.sigmoid(br[:, :D])
    g_gdn = jax.nn.sigmoid(br[:, D:])
    return dict(on=on, ohs=ohs, rs=rs, og=og, y_gdn=y_gdn, ypre=ypre, yp=yp, y_pool=y_pool, g_pool=g_pool, g_gdn=g_gdn,
                sig_gate=sig_gate, silu_gate=silu_gate)


def _merge_fwd(x1, of, ob, pgate, pd, br, modv, gnw, pool_w, pscale, w_gdn, w_pool, w_mo, *, name):
    n_tok = of.shape[0]

    def body(x_ref, of_ref, ob_ref, pg_ref, pd_ref, br_ref, mod_ref, gnw_ref, pw_ref, ps_ref, wg_ref, wp_ref, wmo_ref,
             x2_ref, og_ref, yp_ref, m_ref, mix_ref):
        t = _merge_parts(of_ref[...], ob_ref[...], pg_ref[...], pd_ref[...], br_ref[...], gnw_ref[...], pw_ref, ps_ref[...],
                         wg_ref, wp_ref)
        m = t["g_pool"] * t["y_pool"] + t["g_gdn"] * t["y_gdn"]
        mix = _nn(m, wmo_ref[...])
        og_ref[...] = t["og"].astype(BF)
        yp_ref[...] = t["yp"].astype(BF)
        m_ref[...] = m.astype(BF)
        mix_ref[...] = mix.astype(BF)
        x2_ref[...] = x_ref[...] + mod_ref[0, 5:6, :] * mix

    tile = lambda w: pl.BlockSpec((TM, w), lambda i: (i, 0))
    ctile = lambda w: pl.BlockSpec((TM, w), lambda i: (i + 1, 0))
    return pl.pallas_call(
        body, name=name, grid=(n_tok // TM,),
        in_specs=[ctile(D), tile(D), tile(D), ctile(D), tile(NPOOL), ctile(2 * D),
                  pl.BlockSpec((1, 16, D), lambda i: (1, 0, 0)), _const_spec((1, HD)), _const_spec((4, 128, 128)),
                  _const_spec((1, NPOOL)), _const_spec((D, D)), _const_spec((NPOOL, D)), _const_spec((D, D))],
        out_specs=[tile(D), tile(D), tile(NPOOL), tile(D), tile(D)],
        out_shape=[jax.ShapeDtypeStruct((n_tok, D), F32), jax.ShapeDtypeStruct((n_tok, D), BF),
                   jax.ShapeDtypeStruct((n_tok, NPOOL), BF), jax.ShapeDtypeStruct((n_tok, D), BF),
                   jax.ShapeDtypeStruct((n_tok, D), BF)],
        compiler_params=_cparams("parallel"),
    )(x1, of, ob, pgate, pd, br, modv, gnw, pool_w, pscale, w_gdn, w_pool, w_mo)


def _merge_bwd(dx2, mix, of, ob, pgate, pd, br, modv, gnw, pool_w, pscale, w_gdn, w_pool, w_mo, *, name):
    n_tok = of.shape[0]

    def body(dx2_ref, mix_ref, of_ref, ob_ref, pg_ref, pd_ref, br_ref, mod_ref, gnw_ref, pw_ref, ps_ref, wg_ref, wp_ref, wmo_ref,
             do_ref, dgate_ref, dpd_ref, dbr_ref, dmix_ref, dyg_ref, dyp_ref, acc_ref, dpw_ref):
        i = pl.program_id(0)
        pgate, pdv, gnw = pg_ref[...], pd_ref[...], gnw_ref[...]
        t = _merge_parts(of_ref[...], ob_ref[...], pgate, pdv, br_ref[...], gnw, pw_ref, ps_ref[...], wg_ref, wp_ref)
        dx2v = dx2_ref[...]
        dmix = mod_ref[0, 5:6, :] * dx2v
        dmixb = dmix.astype(BF)
        dmix_ref[...] = dmixb
        dm = _nt(dmixb, wmo_ref[...])
        gp, gg = t["g_pool"], t["g_gdn"]
        dbr_ref[:, :D] = (dm * t["y_pool"] * gp * (1.0 - gp)).astype(BF)
        dbr_ref[:, D:] = (dm * t["y_gdn"] * gg * (1.0 - gg)).astype(BF)
        dyp = (dm * gp).astype(BF)
        dyg = (dm * gg).astype(BF)
        dyp_ref[...] = dyp
        dyg_ref[...] = dyg
        dyp_in = _nt(dyp, wp_ref[...])
        dypre = dyp_in * ps_ref[...]
        for g in range(4):
            gs = slice(g * 128, (g + 1) * 128)
            dpd_ref[:, gs] = _nt(dypre[:, gs], pw_ref[g])
        dog = _nt(dyg, wg_ref[...])
        dgate_ref[...] = (dog * t["on"] * (t["sig_gate"] * (1.0 + pgate - t["silu_gate"]))).astype(BF)
        don = dog * t["silu_gate"]
        dgnw = jnp.zeros((1, HD), F32)
        for h in range(NH):
            hs = slice(h * HD, (h + 1) * HD)
            donh, oh, r = don[:, hs], t["ohs"][h], t["rs"][h]
            dgnw = dgnw + jnp.sum(donh * oh, axis=0, keepdims=True)
            doh = donh * gnw
            do_ref[:, hs] = r * (doh - oh * jnp.mean(doh * oh, axis=-1, keepdims=True))

        @pl.when(i == 0)
        def _():
            acc_ref[...] = jnp.zeros_like(acc_ref)
            dpw_ref[...] = jnp.zeros_like(dpw_ref)

        acc_ref[0:1, :] += jnp.sum(dx2v * mix_ref[...].astype(F32), axis=0, keepdims=True)
        acc_ref[1:2, 0:HD] += dgnw
        acc_ref[2:3, 0:NPOOL] += jnp.sum(dyp_in * t["ypre"], axis=0, keepdims=True)
        for g in range(4):
            gs = slice(g * 128, (g + 1) * 128)
            dpw_ref[g] += _tn(pdv[:, gs], dypre[:, gs])

    tile = lambda w: pl.BlockSpec((TM, w), lambda i: (i, 0))
    ctile = lambda w: pl.BlockSpec((TM, w), lambda i: (i + 1, 0))
    return pl.pallas_call(
        body, name=name, grid=(n_tok // TM,),
        in_specs=[tile(D), tile(D), tile(D), tile(D), ctile(D), tile(NPOOL), ctile(2 * D),
                  pl.BlockSpec((1, 16, D), lambda i: (1, 0, 0)), _const_spec((1, HD)), _const_spec((4, 128, 128)),
                  _const_spec((1, NPOOL)), _const_spec((D, D)), _const_spec((NPOOL, D)), _const_spec((D, D))],
        out_specs=[tile(D), tile(D), tile(NPOOL), tile(2 * D), tile(D), tile(D), tile(D),
                   pl.BlockSpec((8, D), lambda i: (0, 0)), pl.BlockSpec((4, 128, 128), lambda i: (0, 0, 0))],
        out_shape=[jax.ShapeDtypeStruct((n_tok, D), F32), jax.ShapeDtypeStruct((n_tok, D), BF),
                   jax.ShapeDtypeStruct((n_tok, NPOOL), F32), jax.ShapeDtypeStruct((n_tok, 2 * D), BF),
                   jax.ShapeDtypeStruct((n_tok, D), BF), jax.ShapeDtypeStruct((n_tok, D), BF), jax.ShapeDtypeStruct((n_tok, D), BF),
                   jax.ShapeDtypeStruct((8, D), F32), jax.ShapeDtypeStruct((4, 128, 128), F32)],
        compiler_params=_cparams("arbitrary"),
    )(dx2, mix, of, ob, pgate, pd, br, modv, gnw, pool_w, pscale, w_gdn, w_pool, w_mo)


def _split(results, n):
    return (*results[:n], list(results[n:]))


def _local_step(ctx, x, target, modv, p, late=None):
    t_lat = x.shape[0]
    n_all = t_lat + TM
    nbc, nbx = TM // TS, t_lat // TS
    mod_lat = modv[1:2]
    gather = (lambda arrs: _ChipExchange(arrs, False)) if late else (lambda arrs: None)
    scatter = (lambda arrs: _ChipExchange(arrs, True)) if late else (lambda arrs: None)

    x1, h1, gu1, f1, *got = _ffn_fwd(x, modv, p["norm1"], p["w1_in"], p["w1_out"], mrow=0, name="ffn1_fwd", ctx=ctx,
                                     exchange=gather(late and late[0]))
    if late:
        p = {**p, "w_mix": _regroup_mix(_from_chip_major_cols(got[0])), "conv": jnp.pad(_from_chip_major_cols(got[1]), ((0, 3), (0, 0)))}
    u, p_qkv, p_gate, p_pool, p_br, p_ab, *got = _mix_in_fwd(x1, modv, p["norm2"], p["w_mix"], name="mix_in_fwd",
                                                              exchange=gather(late and late[2]))
    if late:
        p = {**p, "w_gdn": got[0].reshape(D, D), "w_pool": _from_chip_major_cols(got[1]), "w_mo": got[2].reshape(D, D)}
    qkv, pre_qkv = _prep_fwd(p_qkv, p["conv"], name="prep_fwd")
    s_zero = jnp.zeros((2, NH, HD, HD), F32)
    _, _, sall_cf, sall_cb, s_ctx = _scan_fwd(qkv, p_ab, p["cst"], s_zero, row_blk0=0, nb=nbc, name="scan_ctx")
    o_f, o_b, sall_f, sall_b, _, *got = _scan_fwd(qkv, p_ab, p["cst"], s_ctx, row_blk0=nbc, nb=nbx, name="scan_lat",
                                                  exchange=gather(late and late[1]))
    if late:
        p = {**p, "w2_in": got[0], "w2_out": got[1].reshape(FF, D)}
    pd = _pool(p_pool, row0=TM, transpose=False, name="pool_fwd")
    merge_w = (modv, p["gnw"], p["pool_w"], p["pscale"], p["w_gdn"], p["w_pool"], p["w_mo"])
    x2, og, yp, m, mix = _merge_fwd(x1, o_f, o_b, p_gate, pd, p_br, *merge_w, name="merge_fwd")
    dx3, h3, gu3, f3, acc_fin = _ffn_fwd(x2, mod_lat, p["norm3"], p["w2_in"], p["w2_out"], mrow=6, name="ffn2_fwd",
                                         head=(target, p["fnorm"]))

    dx2, a3, df3, dgu3, acc3 = _ffn_bwd(dx3, x2, gu3, f3, mod_lat, p["norm3"], p["w2_in"], p["w2_out"], mrow=6,
                                        name="ffn2_bwd")
    g = {}
    tkl = _k_tile(t_lat)
    g["w2_out"] = _matmul_tn(a3, df3, tmm=FF // 2, tn=D, tk=tkl, name="ffn2_wout_grad").reshape(NCHIP, FF // NCHIP, D)
    g["w2_in"] = _matmul_tn(h3, dgu3, tmm=D, tn=2 * FF // NCHIP, tk=tkl, nsplit=NCHIP, name="ffn2_win_grad")
    do, dgate, dpd, dbr, dmix, dyg, dyp, acc_m, dpw = _merge_bwd(dx2, mix, o_f, o_b, p_gate, pd, p_br, *merge_w, name="merge_bwd")
    g["w_mo"] = _matmul_tn(m, dmix, tmm=D, tn=D, tk=tkl, name="wmo_grad").reshape(NCHIP, D // NCHIP, D)
    g["w_gdn"] = _matmul_tn(og, dyg, tmm=D, tn=D, tk=tkl, name="wgdn_grad").reshape(NCHIP, D // NCHIP, D)
    g["w_pool"] = _matmul_tn(yp, dyp, tmm=NPOOL, tn=D // NCHIP, tk=tkl, nsplit=NCHIP, name="wpool_grad")
    dpool_in = _pool(dpd, row0=0, transpose=True, name="pool_bwd")
    acc = (lax.empty((n_all, NQKV), F32), lax.empty((n_all, NQKV), F32), lax.empty((n_all, 128), F32),
           lax.empty((n_all, 128), F32), jnp.zeros((8, 128), F32))
    behind_scan = ("w2_in", "w2_out", "w_gdn", "w_pool", "w_mo")
    *acc, ds_ctx, landed = _split(_scan_bwd(qkv, p_ab, p["cst"], sall_f, sall_b, do, s_zero, *acc, row_blk0=nbc, nb=nbx, has_do=True,
                                            name="scan_lat_bwd", exchange=scatter([g[k] for k in behind_scan])), 6)
    landed = dict(zip(behind_scan, landed))
    dqkv_f, dqkv_b, dab_f, dab_b, dcst, _ = _scan_bwd(qkv, p_ab, p["cst"], sall_cf, sall_cb, jnp.zeros((8, 128), F32), ds_ctx, *acc,
                                                      row_blk0=0, nb=nbc, has_do=False, name="scan_ctx_bwd")
    dpqkv, dconv = _prep_bwd(p_qkv, pre_qkv, dqkv_f, dqkv_b, p["conv"], name="prep_bwd")
    dx1, dp, acc_mix = _mix_in_bwd(dx2, x1, dpqkv, dgate, dpool_in, dbr, dab_f, dab_b, modv, p["norm2"], p["w_mix"],
                                   name="mix_in_bwd")
    tka = _k_tile(n_all)
    g["w_mix"] = _chip_major_cols(_ungroup_mix(_matmul_tn(u, dp, tmm=256, tn=NMIXP, tk=_k_tile(n_all, 1024), name="wmix_grad")))
    dx_lat, a1, df1, dgu1, acc1, got = _split(_ffn_bwd(dx1, x, gu1, f1, modv, p["norm1"], p["w1_in"], p["w1_out"], mrow=0, ctx=ctx,
                                                   name="ffn1_bwd", exchange=scatter([g["w_mix"]])), 5)
    landed.update(zip(("w_mix",), got))
    g["w1_out"] = _matmul_tn(a1, df1, tmm=FF // 2, tn=D, tk=tka, name="ffn1_wout_grad").reshape(NCHIP, FF // NCHIP, D)
    g["w1_in"] = _matmul_tn(h1, dgu1, tmm=D, tn=2 * FF // NCHIP, tk=tka, nsplit=NCHIP, name="ffn1_win_grad")

    small = dict(norm1=acc1[0, 3] + acc1[1, 3], norm2=acc_mix[0, 3] + acc_mix[1, 3], norm3=acc3[0, 3], fnorm=acc_fin[0],
                 gnw=acc_m[1, :HD], pscale=acc_m[2, :NPOOL], pool_w=dpw, conv=dconv[:5],
                 a_log=dcst[0, 2 * NH:4 * NH], dt_bias=dcst[1, 2 * NH:4 * NH])
    zero = jnp.zeros((D,), F32)
    dmod = jnp.stack([
        jnp.stack([acc1[0, 0], acc1[0, 1], acc1[0, 2], acc_mix[0, 0], acc_mix[0, 1], zero, zero, zero, zero]),
        jnp.stack([acc1[1, 0], acc1[1, 1], acc1[1, 2], acc_mix[1, 0], acc_mix[1, 1], acc_m[0], acc3[0, 0], acc3[0, 1], acc3[0, 2]]),
    ])
    return jnp.sum(acc_fin[1]), dx_lat, g, landed, small, dmod


_HI = lax.Precision.HIGHEST


def _ada_fwd(c_all, w_sh, b_sh, *, name):
    def body(c_ref, w_ref, b_ref, o_ref):
        o_ref[...] = jnp.dot(_silu(c_ref[...]), w_ref[...], precision=_HI, preferred_element_type=F32) + b_ref[...]

    return pl.pallas_call(body, name=name, out_shape=jax.ShapeDtypeStruct((16, w_sh.shape[1]), F32),
                          compiler_params=pltpu.CompilerParams(vmem_limit_bytes=VMEM_LIMIT))(c_all, w_sh, b_sh)


def _ada_bwd(c_all, dm, w_sh, *, name):
    def body(c_ref, dm_ref, w_ref, dw_ref, dc_ref):
        sc = _silu(c_ref[...])
        dw_ref[...] = lax.dot_general(sc, dm_ref[...], (((0,), (0,)), ((), ())), precision=_HI, preferred_element_type=F32)
        part = lax.dot_general(dm_ref[8:9, :], w_ref[...], (((1,), (1,)), ((), ())), precision=_HI, preferred_element_type=F32)
        dc_ref[...] = jnp.broadcast_to(part, dc_ref.shape)

    return pl.pallas_call(body, name=name,
                          out_shape=[jax.ShapeDtypeStruct(w_sh.shape, F32), jax.ShapeDtypeStruct((8, D), F32)],
                          compiler_params=pltpu.CompilerParams(vmem_limit_bytes=VMEM_LIMIT))(c_all, dm, w_sh)


def _cctx_grad(parts, c_ctx, *, name):
    def body(p_ref, c_ref, o_ref):
        tot = (p_ref[0, 0:1, :] + p_ref[2, 0:1, :]) + (p_ref[4, 0:1, :] + p_ref[6, 0:1, :])
        o_ref[...] = tot * _dsilu(c_ref[...])

    return pl.pallas_call(body, name=name, out_shape=jax.ShapeDtypeStruct((1, D), F32))(parts, c_ctx)


_MESH = pl.DeviceIdType.MESH
_ANY = pl.BlockSpec(memory_space=pl.ANY)


def _flip(v, bit):
    return (1 - v) if bit else v


def _all_gather8(x, *, name):
    def body(x_ref, out_ref, send_sems, recv_sems, local_sem):
        mx, my, mc = lax.axis_index("x"), lax.axis_index("y"), lax.axis_index("c")
        me = 4 * mx + 2 * my + mc
        mine = pltpu.make_async_copy(x_ref, out_ref.at[me], local_sem)
        mine.start()
        sends, recvs = [], []
        for k in range(1, 8):
            px, py, pc = _flip(mx, k & 4), _flip(my, k & 2), _flip(mc, k & 1)
            sends.append(pltpu.make_async_remote_copy(src_ref=x_ref, dst_ref=out_ref.at[me], send_sem=send_sems.at[k - 1],
                                                      recv_sem=recv_sems.at[k - 1], device_id=(px, py, pc), device_id_type=_MESH))
            recvs.append(pltpu.make_async_remote_copy(src_ref=x_ref, dst_ref=out_ref.at[4 * px + 2 * py + pc],
                                                      send_sem=send_sems.at[k - 1], recv_sem=recv_sems.at[k - 1],
                                                      device_id=(px, py, pc), device_id_type=_MESH))
        for cp in sends:
            cp.start()
        for cp in recvs:
            cp.wait_recv()
        for cp in sends:
            cp.wait_send()
        mine.wait()

    vm = pl.BlockSpec(memory_space=pltpu.VMEM)
    return pl.pallas_call(
        body, name=name, out_shape=jax.ShapeDtypeStruct((8,) + x.shape, x.dtype), in_specs=[vm], out_specs=vm,
        scratch_shapes=[pltpu.SemaphoreType.DMA((7,)), pltpu.SemaphoreType.DMA((7,)), pltpu.SemaphoreType.DMA],
        compiler_params=pltpu.CompilerParams(vmem_limit_bytes=VMEM_LIMIT),
    )(x)


class _ChipExchange:
    def __init__(self, arrs, scatter):
        self.arrs, self.scatter, self.n = list(arrs), scatter, len(arrs)
        self.out_shape = [jax.ShapeDtypeStruct(a.shape if scatter else (NCHIP,) + a.shape, a.dtype) for a in self.arrs]
        links = self.n * (NCHIP - 1)
        self.scratch = [pltpu.SemaphoreType.DMA((links,)), pltpu.SemaphoreType.DMA((links,)), pltpu.SemaphoreType.DMA((self.n,))]

    def copies(self, ins, outs, send_sems, recv_sems, local_sems):
        mx, my, mc = lax.axis_index("x"), lax.axis_index("y"), lax.axis_index("c")
        me = 2 * mx + my
        local, sends, recvs = [], [], []
        for j in range(self.n):
            src_own = ins[j].at[me] if self.scatter else ins[j]
            local.append(pltpu.make_async_copy(src_own, outs[j].at[me], local_sems.at[j]))
            for k in range(1, NCHIP):
                px, py = _flip(mx, k & 2), _flip(my, k & 1)
                peer = 2 * px + py
                sem = j * (NCHIP - 1) + k - 1
                src = ins[j].at[peer] if self.scatter else ins[j]
                sends.append(pltpu.make_async_remote_copy(src_ref=src, dst_ref=outs[j].at[me], send_sem=send_sems.at[sem],
                                                          recv_sem=recv_sems.at[sem], device_id=(px, py, mc), device_id_type=_MESH))
                recvs.append(pltpu.make_async_remote_copy(src_ref=src, dst_ref=outs[j].at[peer], send_sem=send_sems.at[sem],
                                                          recv_sem=recv_sems.at[sem], device_id=(px, py, mc), device_id_type=_MESH))
        return local, sends, recvs

    @staticmethod
    def start(local, sends, recvs):
        for cp in local + sends:
            cp.start()

    @staticmethod
    def finish(local, sends, recvs):
        for cp in recvs:
            cp.wait_recv()
        for cp in sends:
            cp.wait_send()
        for cp in local:
            cp.wait()


def _gather_split(arrs, *, name):
    n = len(arrs)
    links = n * (NCHIP - 1)

    def body(*refs):
        ins, outs = refs[:n], refs[n:2 * n]
        ici_send, ici_recv, d2d_send, d2d_recv, local_sems = refs[2 * n:]
        mx, my, mc = lax.axis_index("x"), lax.axis_index("y"), lax.axis_index("c")
        me = 2 * mx + my
        local, first, arrive, onward, handed = [], [], [], [], []
        for j in range(n):
            hr = ins[j].shape[0] // 2
            mine, other = pl.ds(mc * hr, hr), pl.ds((1 - mc) * hr, hr)
            local.append(pltpu.make_async_copy(ins[j], outs[j].at[me], local_sems.at[j]))
            for k in range(1, NCHIP):
                px, py = _flip(mx, k & 2), _flip(my, k & 1)
                peer = 2 * px + py
                sem = j * (NCHIP - 1) + k - 1
                ici = lambda slot: pltpu.make_async_remote_copy(
                    src_ref=ins[j].at[mine], dst_ref=outs[j].at[slot, mine], send_sem=ici_send.at[sem], recv_sem=ici_recv.at[sem],
                    device_id=(px, py, mc), device_id_type=_MESH)
                d2d = lambda rows: pltpu.make_async_remote_copy(
                    src_ref=outs[j].at[peer, rows], dst_ref=outs[j].at[peer, rows], send_sem=d2d_send.at[sem], recv_sem=d2d_recv.at[sem],
                    device_id=(mx, my, 1 - mc), device_id_type=_MESH)
                first.append(ici(me))
                arrive.append(ici(peer))
                onward.append(d2d(mine))
                handed.append(d2d(other))
        for cp in local + first:
            cp.start()
        for got, fwd in zip(arrive, onward):
            got.wait_recv()
            fwd.start()
        for cp in handed:
            cp.wait_recv()
        for cp in first + onward:
            cp.wait_send()
        for cp in local:
            cp.wait()

    sems = pltpu.SemaphoreType.DMA((links,))
    return pl.pallas_call(
        body, name=name, out_shape=[jax.ShapeDtypeStruct((NCHIP,) + a.shape, a.dtype) for a in arrs],
        in_specs=[_ANY] * n, out_specs=[_ANY] * n, scratch_shapes=[sems, sems, sems, sems, pltpu.SemaphoreType.DMA((n,))],
    )(*arrs)


_HBM = pl.BlockSpec(memory_space=pltpu.HBM)
_SEM = pl.BlockSpec(memory_space=pltpu.SEMAPHORE)
_DATAFLOW = pltpu.SideEffectType.DATAFLOW_SIDE_EFFECTING


def _scatter_copies(ins, lands, send_sems, recv_sems):
    mx, my, mc = lax.axis_index("x"), lax.axis_index("y"), lax.axis_index("c")
    me = 2 * mx + my
    sends, recvs = [], []
    for j in range(len(ins)):
        for k in range(1, NCHIP):
            px, py = _flip(mx, k & 2), _flip(my, k & 1)
            peer = 2 * px + py
            sem = j * (NCHIP - 1) + k - 1
            mk = lambda slot: pltpu.make_async_remote_copy(src_ref=ins[j].at[peer], dst_ref=lands[j].at[slot], send_sem=send_sems.at[sem],
                                                           recv_sem=recv_sems.at[sem], device_id=(px, py, mc), device_id_type=_MESH)
            sends.append(mk(me))
            recvs.append(mk(peer))
    return sends, recvs


def _scatter_start(arrs, after, *, name):
    n = len(arrs)
    links = n * (NCHIP - 1)
    n_in = 2 * n + len(after)

    def body(*refs):
        ins, lands = refs[:n], refs[n:2 * n]
        send_sems, recv_sems = refs[n_in], refs[n_in + 1]
        token = refs[-1]
        for cp in _scatter_copies(ins, lands, send_sems, recv_sems)[0]:
            cp.start()
        token[...] = jnp.zeros_like(token)

    hbm = lambda a: pltpu.HBM(a.shape, a.dtype)
    res = pl.pallas_call(
        body, name=name,
        out_shape=(pltpu.SemaphoreType.DMA((links,)), pltpu.SemaphoreType.DMA((links,)), *[hbm(a) for a in arrs], *[hbm(a) for a in arrs],
                   jax.ShapeDtypeStruct((8, 128), F32)),
        in_specs=[_HBM] * (2 * n) + [_ANY] * len(after), out_specs=(_SEM, _SEM, *[_HBM] * (2 * n), pl.BlockSpec(memory_space=pltpu.VMEM)),
        input_output_aliases={j: 2 + j for j in range(2 * n)},
        compiler_params=pltpu.CompilerParams(has_side_effects=_DATAFLOW),
    )(*[pltpu.with_memory_space_constraint(a, pltpu.HBM) for a in arrs],
      *[pltpu.with_memory_space_constraint(lax.empty(a.shape, a.dtype), pltpu.HBM) for a in arrs], *after)
    return res[0], res[1], list(res[2:2 + n]), list(res[2 + n:2 + 2 * n]), res[-1]


def _scatter_wait(send_sems, recv_sems, arrs, lands, after, *, name):
    n = len(arrs)

    def body(*refs):
        ins, lands_in = refs[:n], refs[n:2 * n]
        sends, recvs = _scatter_copies(ins, lands_in, refs[2 * n], refs[2 * n + 1])
        for cp in sends:
            cp.wait_send()
        for cp in recvs:
            cp.wait_recv()

    hbm = lambda a: pltpu.HBM(a.shape, a.dtype)
    res = pl.pallas_call(
        body, name=name, out_shape=(*[hbm(a) for a in arrs], *[hbm(a) for a in lands]),
        in_specs=[_HBM] * (2 * n) + [_SEM, _SEM] + [_ANY] * len(after), out_specs=[_HBM] * (2 * n),
        input_output_aliases={j: j for j in range(2 * n)},
        compiler_params=pltpu.CompilerParams(has_side_effects=_DATAFLOW),
    )(*arrs, *lands, send_sems, recv_sems, *after)
    return list(res[:n]), list(res[n:])


def _pallas(body, operands, *, name, grid, in_specs, out_specs, out_shape, sem, scratch_shapes=(), aliases=None, exchange=None):
    if exchange is None:
        return pl.pallas_call(body, name=name, grid=grid, in_specs=in_specs, out_specs=out_specs, out_shape=out_shape,
                              scratch_shapes=list(scratch_shapes), input_output_aliases=aliases or {},
                              compiler_params=_cparams(*sem))(*operands)
    ex = exchange
    (steps,) = grid
    n_in, n_out, n_scr, k = len(in_specs), len(out_specs), len(scratch_shapes), ex.n

    def hosted(*refs):
        ins, refs = refs[:n_in], refs[n_in:]
        ex_in, refs = refs[:k], refs[k:]
        outs, refs = refs[:n_out], refs[n_out:]
        ex_out, refs = refs[:k], refs[k:]
        scr, ex_sems = refs[:n_scr], refs[n_scr:]
        cps = ex.copies(ex_in, ex_out, *ex_sems)
        pl.when(pl.program_id(0) == 0)(lambda: ex.start(*cps))
        body(*ins, *outs, *scr)
        pl.when(pl.program_id(0) == steps - 1)(lambda: ex.finish(*cps))

    return pl.pallas_call(
        hosted, name=name, grid=grid, in_specs=list(in_specs) + [_ANY] * k, out_specs=list(out_specs) + [_ANY] * k,
        out_shape=list(out_shape) + ex.out_shape, scratch_shapes=list(scratch_shapes) + ex.scratch,
        input_output_aliases=aliases or {}, compiler_params=_cparams("arbitrary"),
    )(*operands, *ex.arrs)


def _core_swap(arrs, *, name):
    n = len(arrs)

    def body(*refs):
        ins, outs = refs[:n], refs[n:2 * n]
        send_sems, recv_sems = refs[2 * n:]
        sib = (lax.axis_index("x"), lax.axis_index("y"), 1 - lax.axis_index("c"))
        cps = [pltpu.make_async_remote_copy(src_ref=ins[j], dst_ref=outs[j], send_sem=send_sems.at[j], recv_sem=recv_sems.at[j],
                                            device_id=sib, device_id_type=_MESH) for j in range(n)]
        for cp in cps:
            cp.start()
        for cp in cps:
            cp.wait_recv()
        for cp in cps:
            cp.wait_send()

    return pl.pallas_call(
        body, name=name, out_shape=[jax.ShapeDtypeStruct(a.shape, a.dtype) for a in arrs],
        in_specs=[_ANY] * n, out_specs=[_ANY] * n,
        scratch_shapes=[pltpu.SemaphoreType.DMA((n,)), pltpu.SemaphoreType.DMA((n,))],
    )(*arrs)


def _row_tile(rows, cols, budget=1 << 20):
    best = None
    for t in range(8, rows + 1, 8):
        if rows % t == 0 and t * cols <= budget:
            best = t
    return best or rows


def _sum_slots(x, *, name, after=()):
    ns, r, c = x.shape
    tr = _row_tile(r, c * ns)

    def body(x_ref, *refs):
        acc = x_ref[0].astype(F32)
        for s in range(1, ns):
            acc = acc + x_ref[s].astype(F32)
        refs[-1][...] = acc

    return pl.pallas_call(
        body, name=name, grid=(r // tr,), out_shape=jax.ShapeDtypeStruct((r, c), F32),
        in_specs=[pl.BlockSpec((ns, tr, c), lambda i: (0, i, 0))] + [_ANY] * len(after), out_specs=pl.BlockSpec((tr, c), lambda i: (i, 0)),
        compiler_params=_cparams("parallel"),
    )(x, *after)


def _adamw(w, ga, gb, m, v, *, name):
    r, c = w.shape
    tr = _row_tile(r, c, budget=1 << 18)
    two = gb is not None

    def body(*refs):
        w_ref, ga_ref = refs[0], refs[1]
        m_ref, v_ref = refs[2 + two], refs[3 + two]
        g_ref, d_ref, mo_ref, vo_ref = refs[4 + two:]
        g = ga_ref[...] + refs[2][...] if two else ga_ref[...]
        mn = ADAM_B1 * m_ref[...] + (1.0 - ADAM_B1) * g
        vn = ADAM_B2 * v_ref[...] + (1.0 - ADAM_B2) * (g * g)
        m_hat = mn / (1.0 - ADAM_B1 ** ADAM_STEP)
        v_hat = vn / (1.0 - ADAM_B2 ** ADAM_STEP)
        g_ref[...] = g
        d_ref[...] = -ADAM_LR * (m_hat / (jnp.sqrt(v_hat) + ADAM_EPS) + ADAM_WD * w_ref[...])
        mo_ref[...] = mn
        vo_ref[...] = vn

    spec = pl.BlockSpec((tr, c), lambda i: (i, 0))
    ins = [w, ga] + ([gb] if two else []) + [m, v]
    return pl.pallas_call(
        body, name=name, grid=(r // tr,), out_shape=[jax.ShapeDtypeStruct((r, c), F32)] * 4,
        in_specs=[spec] * len(ins), out_specs=[spec] * 4, compiler_params=_cparams("parallel"),
    )(*ins)


_MIX_AB0, _MIX_AB1 = NQKV, NQKV + 4 * NH


def _regroup_mix(w):
    pad = jnp.zeros((w.shape[0], NMIXP - NMIX), w.dtype)
    return jnp.concatenate([w[:, :_MIX_AB0], w[:, _MIX_AB1:], w[:, _MIX_AB0:_MIX_AB1], pad], axis=1)


def _ungroup_mix(w):
    n_ab = _MIX_AB1 - _MIX_AB0
    return jnp.concatenate([w[:, :_MIX_AB0], w[:, NMIX - n_ab:NMIX], w[:, _MIX_AB0:NMIX - n_ab]], axis=1)


def _chip_major_cols(w):
    r, c = w.shape
    return w.reshape(r, NCHIP, c // NCHIP).transpose(1, 0, 2)


def _from_chip_major_cols(w):
    return w.transpose(1, 0, 2).reshape(w.shape[1], -1)


_SMALL = (("c_ctx", D), ("b_ada", 9 * D), ("norm1_w", D), ("norm2_w", D), ("norm3_w", D), ("final_norm_w", D),
          ("a_log", 2 * NH), ("dt_bias", 2 * NH), ("gdn_norm_w", HD), ("pool_w", 4 * 128 * 128), ("pool_scale", NPOOL),
          ("conv_w", 5 * NQKV // NCHIP))


def _pack(vals, lanes=128, row_mult=8):
    flat = jnp.concatenate([jnp.ravel(v) for v in vals])
    n = flat.shape[0]
    rows = -(-n // (lanes * row_mult)) * row_mult
    return jnp.pad(flat, (0, rows * lanes - n)).reshape(rows, lanes)


def _unpack(packed, sizes):
    flat = packed.reshape(-1)
    out, o = [], 0
    for n in sizes:
        out.append(flat[o:o + n])
        o += n
    return out


def kernel(x, c, ctx, c_ctx, w_ada, b_ada, norm1_w, ffn1_w_in, ffn1_w_out, norm2_w, w_mix_in, conv_w, a_log, dt_bias, gdn_norm_w, w_gdn_proj, pool_w, pool_scale, w_pool_proj, w_mix_out, norm3_w, ffn2_w_in, ffn2_w_out, final_norm_w, loss_target, m_c_ctx, m_w_ada, m_b_ada, m_norm1_w, m_ffn1_w_in, m_ffn1_w_out, m_norm2_w, m_w_mix_in, m_conv_w, m_a_log, m_dt_bias, m_gdn_norm_w, m_w_gdn_proj, m_pool_w, m_pool_scale, m_w_pool_proj, m_w_mix_out, m_norm3_w, m_ffn2_w_in, m_ffn2_w_out, m_final_norm_w, v_c_ctx, v_w_ada, v_b_ada, v_norm1_w, v_ffn1_w_in, v_ffn1_w_out, v_norm2_w, v_w_mix_in, v_conv_w, v_a_log, v_dt_bias, v_gdn_norm_w, v_w_gdn_proj, v_pool_w, v_pool_scale, v_w_pool_proj, v_w_mix_out, v_norm3_w, v_ffn2_w_in, v_ffn2_w_out, v_final_norm_w):
    names = ("c_ctx", "w_ada", "b_ada", "norm1_w", "ffn1_w_in", "ffn1_w_out", "norm2_w", "w_mix_in", "conv_w", "a_log", "dt_bias",
             "gdn_norm_w", "w_gdn_proj", "pool_w", "pool_scale", "w_pool_proj", "w_mix_out", "norm3_w", "ffn2_w_in", "ffn2_w_out",
             "final_norm_w")
    w = dict(zip(names, (c_ctx, w_ada, b_ada, norm1_w, ffn1_w_in, ffn1_w_out, norm2_w, w_mix_in, conv_w, a_log, dt_bias, gdn_norm_w,
                         w_gdn_proj, pool_w, pool_scale, w_pool_proj, w_mix_out, norm3_w, ffn2_w_in, ffn2_w_out, final_norm_w)))
    mom = dict(zip(names, (m_c_ctx, m_w_ada, m_b_ada, m_norm1_w, m_ffn1_w_in, m_ffn1_w_out, m_norm2_w, m_w_mix_in, m_conv_w, m_a_log,
                           m_dt_bias, m_gdn_norm_w, m_w_gdn_proj, m_pool_w, m_pool_scale, m_w_pool_proj, m_w_mix_out, m_norm3_w,
                           m_ffn2_w_in, m_ffn2_w_out, m_final_norm_w)))
    var = dict(zip(names, (v_c_ctx, v_w_ada, v_b_ada, v_norm1_w, v_ffn1_w_in, v_ffn1_w_out, v_norm2_w, v_w_mix_in, v_conv_w, v_a_log,
                           v_dt_bias, v_gdn_norm_w, v_w_gdn_proj, v_pool_w, v_pool_scale, v_w_pool_proj, v_w_mix_out, v_norm3_w,
                           v_ffn2_w_in, v_ffn2_w_out, v_final_norm_w)))
    mx, my, mc = lax.axis_index("x"), lax.axis_index("y"), lax.axis_index("c")
    chip = 2 * mx + my
    dev = 2 * chip + mc
    ada_cols = w_ada.shape[2]

    c_rows = _all_gather8(jnp.pad(c, ((0, 7), (0, 0))), name="gather_c")[:, 0, :]
    c_all = jnp.concatenate([c_rows, c_ctx[None], jnp.zeros((7, D), F32)], axis=0)
    b_sh = lax.dynamic_slice(b_ada, (0, chip * ada_cols), (1, ada_cols))
    mod_sh = _ada_fwd(c_all, w_ada[0], b_sh, name="ada_fwd")
    mod_parts = _all_gather8(mod_sh, name="gather_mod")
    mod_all = jnp.concatenate([mod_parts[2 * s] for s in range(NCHIP)], axis=1)
    mod_lat = lax.dynamic_index_in_dim(mod_all, dev, axis=0, keepdims=False).reshape(9, D)
    modv = jnp.zeros((2, 16, D), F32).at[0, :9].set(mod_all[8].reshape(9, D)).at[1, :9].set(mod_lat)

    big = ("ffn1_w_in", "ffn1_w_out", "w_mix_in", "w_gdn_proj", "w_pool_proj", "w_mix_out", "ffn2_w_in", "ffn2_w_out")
    shard = {k: w[k][0].astype(BF) for k in big}
    w1_in, w1_out = _gather_split([shard["ffn1_w_in"], shard["ffn1_w_out"]], name="gather_ffn1")
    p = dict(
        norm1=norm1_w, norm2=norm2_w, norm3=norm3_w, fnorm=final_norm_w[None], w1_in=w1_in, w1_out=w1_out.reshape(FF, D),
        cst=jnp.zeros((8, 128), F32).at[0, 2 * NH:4 * NH].set(jnp.exp(a_log).reshape(-1)).at[1, 2 * NH:4 * NH].set(dt_bias.reshape(-1)),
        gnw=gdn_norm_w, pool_w=pool_w[0], pscale=pool_scale)
    late = ([shard["w_mix_in"], conv_w[0]], [shard["ffn2_w_in"], shard["ffn2_w_out"]],
            [shard["w_gdn_proj"], shard["w_pool_proj"], shard["w_mix_out"]])

    loss_dev, dx_lat, g, landed, small, dmod = _local_step(ctx[0], x[0], loss_target[0], modv, p, late)
    loss = lax.psum(loss_dev, ("x", "y", "c"))
    grad_x = dx_lat[None]

    small_vals = [dmod[1], dmod[0], small["norm1"], small["norm2"], small["norm3"], small["fnorm"], small["a_log"], small["dt_bias"],
                  small["gnw"], small["pool_w"], small["pscale"], small["conv"]]
    small_sizes = [v.size for v in small_vals]
    packed = _all_gather8(_pack(small_vals), name="gather_small")
    tot = _unpack(_sum_slots(packed, name="sum_small"), small_sizes)
    dmod_lat_all = packed[:, :9 * D // 128, :].reshape(8, 9 * D)
    dm = jnp.concatenate([dmod_lat_all, tot[1][None], jnp.zeros((7, 9 * D), F32)], axis=0)
    dm_sh = lax.dynamic_slice(dm, (0, chip * ada_cols), (16, ada_cols))
    g_w_ada, cctx_part = _ada_bwd(c_all, dm_sh, w_ada[0], name="ada_bwd")
    g_c_ctx = _cctx_grad(_all_gather8(cctx_part, name="gather_cctx"), c_ctx[None], name="cctx_grad")[0]
    conv_tot = tot[11].reshape(5, NQKV)
    g_small = dict(c_ctx=g_c_ctx, b_ada=tot[0] + tot[1], norm1_w=tot[2], norm2_w=tot[3], norm3_w=tot[4], final_norm_w=tot[5],
                   a_log=tot[6], dt_bias=tot[7], gdn_norm_w=tot[8], pool_w=tot[9], pool_scale=tot[10],
                   conv_w=lax.dynamic_slice(conv_tot, (0, chip * (NQKV // NCHIP)), (5, NQKV // NCHIP)))

    first = ("ffn1_w_in", "ffn1_w_out")
    order = dict(zip(big, ("w1_in", "w1_out", "w_mix", "w_gdn", "w_pool", "w_mo", "w2_in", "w2_out")))
    rest = [k for k in big if k not in first]
    send_sems, recv_sems, sent, lands, token = _scatter_start([g["w1_in"], g["w1_out"]], [g_c_ctx, g_w_ada], name="scatter_ffn1_start")
    mine = {k: _sum_slots(landed[order[k]], name=f"sum_{k}", after=[token]) for k in rest}
    theirs = dict(zip(rest, _core_swap([mine[k] for k in rest], name="swap_grad_sums")))

    out = {}
    as2d = lambda a: a.reshape(-1, a.shape[-1])

    def update(k):
        res = _adamw(as2d(w[k]), as2d(mine[k]), as2d(theirs[k]), as2d(mom[k]), as2d(var[k]), name=f"adamw_{k}")
        out[k] = [r.reshape(w[k].shape) for r in res]

    for k in rest:
        update(k)
    out["w_ada"] = [r.reshape(w_ada.shape) for r in _adamw(w_ada[0], g_w_ada, None, m_w_ada[0], v_w_ada[0], name="adamw_w_ada")]
    sm_names = [n for n, _ in _SMALL]
    sm_sizes = [n for _, n in _SMALL]
    res = _adamw(_pack([w[k] for k in sm_names]), _pack([g_small[k] for k in sm_names]), None,
                 _pack([mom[k] for k in sm_names]), _pack([var[k] for k in sm_names]), name="adamw_small")
    done = [out[k][1] for k in rest] + [out["w_ada"][1], res[1]]
    res = [_unpack(r, sm_sizes) for r in res]
    for i, k in enumerate(sm_names):
        out[k] = [r[i].reshape(w[k].shape) for r in res]
    sent, lands = _scatter_wait(send_sems, recv_sems, sent, lands, done, name="scatter_ffn1_wait")
    for k, part, land in zip(first, sent, lands):
        own = lax.dynamic_slice_in_dim(part, chip, 1, axis=0)
        mine[k] = _sum_slots(lax.dynamic_update_slice_in_dim(land, own, chip, axis=0), name=f"sum_{k}")
    theirs.update(zip(first, _core_swap([mine[k] for k in first], name="swap_ffn1_sums")))
    for k in first:
        update(k)
    return (loss, grad_x, *[out[k][0] for k in names], *[out[k][1] for k in names], *[out[k][2] for k in names],
            *[out[k][3] for k in names])
```

```python
import functools

import jax
import jax.numpy as jnp
from jax import lax
from jax.experimental import pallas as pl
from jax.experimental.pallas import tpu as pltpu

F32 = jnp.float32
BF = jnp.bfloat16

D = 1024
FF = 2816
NH = 8
HD = 128
CH = 64
GW = 64
TM = 256
NQKV = 3 * NH * HD
NPOOL = 512
POOL_WINDOWS = (2, 4, 8, 16)
NMIX = 6688
NMIXP = 6784
EPS = 1e-6
NCHIP = 4
VMEM_LIMIT = 56 * 1024 * 1024

ADAM_LR, ADAM_B1, ADAM_B2, ADAM_EPS, ADAM_WD, ADAM_STEP = 0.001, 0.9, 0.999, 1e-08, 0.01, 10


def _cparams(*sem):
    return pltpu.CompilerParams(dimension_semantics=sem, vmem_limit_bytes=VMEM_LIMIT)


def _const_spec(shape):
    nd = len(shape)
    return pl.BlockSpec(shape, lambda *_: (0,) * nd, pipeline_mode=pl.Buffered(1))


def _dot(a, b, dims):
    return lax.dot_general(a.astype(BF), b.astype(BF), (dims, ((), ())), preferred_element_type=F32)


def _nn(a, b):
    return _dot(a, b, ((1,), (0,)))


def _nt(a, b):
    return _dot(a, b, ((1,), (1,)))


def _tn(a, b):
    return _dot(a, b, ((0,), (0,)))


def _silu(x):
    return x * jax.nn.sigmoid(x)


def _dsilu(x):
    s = jax.nn.sigmoid(x)
    return s * (1.0 + x * (1.0 - s))


def _norm_mod(x, nw, shift, scale):
    r = lax.rsqrt(jnp.mean(x * x, axis=-1, keepdims=True) + EPS)
    xh = x * r
    n = xh * nw
    return n * (1.0 + scale) + shift, n, xh, r


def _norm_mod_bwd(dh, n, xh, r, nw, scale):
    dn = dh * (1.0 + scale)
    dxh = dn * nw
    dx = r * (dxh - xh * jnp.mean(dxh * xh, axis=-1, keepdims=True))
    rs = lambda t: jnp.sum(t, axis=0, keepdims=True)
    return dx, rs(dh), rs(dh * n), rs(dn * xh)


def _stream_specs(ctx):
    if ctx is None:
        return [pl.BlockSpec((TM, D), lambda i: (i, 0))]
    return [pl.BlockSpec((TM, D), lambda i: (0, 0)), pl.BlockSpec((TM, D), lambda i: (jnp.maximum(i - 1, 0), 0))]


def _stream_tile(refs, lead):
    if not lead:
        return refs[0][...], refs[1:]
    return jnp.where(pl.program_id(0) == 0, refs[0][...], refs[1][...]), refs[2:]


def _loss_head(xv, target, w):
    r = lax.rsqrt(jnp.mean(xv * xv, axis=-1, keepdims=True) + EPS)
    xh = xv * r
    err = xh * w - target
    dy = err * (1.0 / D)
    dxh = dy * w
    dx = r * (dxh - xh * jnp.mean(dxh * xh, axis=-1, keepdims=True))
    return dx, jnp.sum(dy * xh, axis=0, keepdims=True), jnp.sum(err * err, axis=0, keepdims=True) * (0.5 / D)


def _ffn_fwd(x, modv, nw, w_in4, w_out, *, mrow, name, ctx=None, exchange=None, head=None):
    lead = ctx is not None
    n_tok = x.shape[0] + (TM if lead else 0)
    nt = n_tok // TM
    nset = modv.shape[0]
    ws = w_in4.shape[2]

    def body(*refs):
        xv, refs = _stream_tile(refs, lead)
        if head:
            (t_ref, fw_ref), refs, acc_ref = refs[:2], refs[2:-1], refs[-1]
        mod_ref, nw_ref, win_ref, wout_ref, x1_ref, h_ref, gu_ref, f_ref = refs
        shift, scale, gate = mod_ref[0, mrow:mrow + 1, :], mod_ref[0, mrow + 1:mrow + 2, :], mod_ref[0, mrow + 2:mrow + 3, :]
        h, _, _, _ = _norm_mod(xv, nw_ref[...], shift, scale)
        hb = h.astype(BF)
        h_ref[...] = hb
        gus = [_nn(hb, win_ref[s]) for s in range(NCHIP)]
        for s in range(NCHIP):
            gu_ref[:, s * ws:(s + 1) * ws] = gus[s].astype(BF)
        g = jnp.concatenate(gus[:2], axis=1)
        u = jnp.concatenate(gus[2:], axis=1)
        f = _nn(_silu(g) * u, wout_ref[...])
        f_ref[...] = f.astype(BF)
        out = xv + 0.5 * gate * f
        if not head:
            x1_ref[...] = out
            return
        x1_ref[...], dfw, loss = _loss_head(out, t_ref[...], fw_ref[...])

        @pl.when(pl.program_id(0) == 0)
        def _():
            acc_ref[...] = jnp.zeros_like(acc_ref)

        acc_ref[0:1, :] += dfw
        acc_ref[1:2, :] += loss

    tile = lambda w: pl.BlockSpec((TM, w), lambda i: (i, 0))
    return _pallas(
        body, (*([ctx] if lead else []), x, *(head or ()), modv, nw, w_in4, w_out), name=name, grid=(nt,),
        sem=("arbitrary",) if head else ("parallel",), exchange=exchange,
        in_specs=_stream_specs(ctx) + ([tile(D), _const_spec((1, D))] if head else [])
        + [pl.BlockSpec((1, 16, D), lambda i: (jnp.minimum(i, nset - 1), 0, 0)), _const_spec((1, D)),
           _const_spec(w_in4.shape), _const_spec(w_out.shape)],
        out_specs=[tile(D), tile(D), tile(2 * FF), tile(D)] + ([pl.BlockSpec((8, D), lambda i: (0, 0))] if head else []),
        out_shape=[jax.ShapeDtypeStruct((n_tok, D), F32), jax.ShapeDtypeStruct((n_tok, D), BF),
                   jax.ShapeDtypeStruct((n_tok, 2 * FF), BF), jax.ShapeDtypeStruct((n_tok, D), BF)]
        + ([jax.ShapeDtypeStruct((8, D), F32)] if head else []))


def _ffn_bwd(dxo, x, gu, fo, modv, nw, w_in4, w_out, *, mrow, name, ctx=None, exchange=None):
    lead = ctx is not None
    dx_skip = int(lead)
    n_tok = x.shape[0] + dx_skip * TM
    nt = n_tok // TM
    nset = modv.shape[0]
    ws = w_in4.shape[2]

    def body(*refs):
        xv, (dxo_ref, gu_ref, f_ref, mod_ref, nw_ref, win_ref, wout_ref, dx_ref, a_ref, df_ref, dgu_ref, acc_ref) = _stream_tile(refs, lead)
        i = pl.program_id(0)
        dxo_v = dxo_ref[...]
        shift, scale, gate = mod_ref[0, mrow:mrow + 1, :], mod_ref[0, mrow + 1:mrow + 2, :], mod_ref[0, mrow + 2:mrow + 3, :]
        _, n, xh, r = _norm_mod(xv, nw_ref[...], shift, scale)
        df = 0.5 * gate * dxo_v
        dfb = df.astype(BF)
        df_ref[...] = dfb
        dgate = jnp.sum(0.5 * dxo_v * f_ref[...].astype(F32), axis=0, keepdims=True)
        da = _nt(dfb, wout_ref[...])
        g = gu_ref[:, :FF].astype(F32)
        u = gu_ref[:, FF:].astype(F32)
        sig = jax.nn.sigmoid(g)
        sg = g * sig
        a_ref[...] = (sg * u).astype(BF)
        dgu_ref[:, :FF] = (da * u * (sig * (1.0 + g - sg))).astype(BF)
        dgu_ref[:, FF:] = (da * sg).astype(BF)
        dh = _nt(dgu_ref[:, 0:ws], win_ref[0])
        for s in range(1, NCHIP):
            dh = dh + _nt(dgu_ref[:, s * ws:(s + 1) * ws], win_ref[s])
        dx, dshift, dscale, dnw = _norm_mod_bwd(dh, n, xh, r, nw_ref[...], scale)
        dx_ref[...] = dxo_v + dx

        @pl.when((i == 0) | (i == nset - 1))
        def _():
            acc_ref[...] = jnp.zeros_like(acc_ref)

        acc_ref[0, 0:1, :] += dshift
        acc_ref[0, 1:2, :] += dscale
        acc_ref[0, 2:3, :] += dgate
        acc_ref[0, 3:4, :] += dnw

    tile = lambda w: pl.BlockSpec((TM, w), lambda i: (i, 0))
    return _pallas(
        body, (*([ctx] if lead else []), x, dxo, gu, fo, modv, nw, w_in4, w_out), name=name, grid=(nt,), sem=("arbitrary",),
        exchange=exchange,
        in_specs=_stream_specs(ctx) + [tile(D), tile(2 * FF), tile(D),
                                       pl.BlockSpec((1, 16, D), lambda i: (jnp.minimum(i, nset - 1), 0, 0)), _const_spec((1, D)),
                                       _const_spec(w_in4.shape), _const_spec(w_out.shape)],
        out_specs=[pl.BlockSpec((TM, D), lambda i: (jnp.maximum(i - dx_skip, 0), 0)), tile(FF), tile(D), tile(2 * FF),
                   pl.BlockSpec((1, 8, D), lambda i: (jnp.minimum(i, nset - 1), 0, 0))],
        out_shape=[jax.ShapeDtypeStruct((n_tok - dx_skip * TM, D), F32), jax.ShapeDtypeStruct((n_tok, FF), BF),
                   jax.ShapeDtypeStruct((n_tok, D), BF), jax.ShapeDtypeStruct((n_tok, 2 * FF), BF),
                   jax.ShapeDtypeStruct((nset, 8, D), F32)])


def _k_tile(n, target=3072):
    return max(t for t in range(TM, min(n, target) + 1, TM) if n % t == 0)


def _matmul_tn(a, b, *, tmm, tn, tk, nsplit=1, name):
    n_tok, m = a.shape
    kk = b.shape[1]
    nk = n_tok // tk

    def body(a_ref, b_ref, o_ref, acc):
        k = pl.program_id(2)

        @pl.when(k == 0)
        def _():
            acc[...] = jnp.zeros_like(acc)

        acc[...] += _tn(a_ref[...], b_ref[...])

        @pl.when(k == nk - 1)
        def _():
            o_ref[...] = acc[...].astype(BF).reshape(o_ref.shape)

    if nsplit == 1:
        out_shape = jax.ShapeDtypeStruct((m, kk), BF)
        out_spec = pl.BlockSpec((tmm, tn), lambda i, j, k: (i, j))
    else:
        assert tn == kk // nsplit
        out_shape = jax.ShapeDtypeStruct((nsplit, m, tn), BF)
        out_spec = pl.BlockSpec((1, tmm, tn), lambda i, j, k: (j, i, 0))
    return pl.pallas_call(
        body, name=name, grid=(m // tmm, kk // tn, nk),
        in_specs=[pl.BlockSpec((tk, tmm), lambda i, j, k: (k, i)), pl.BlockSpec((tk, tn), lambda i, j, k: (k, j))],
        out_specs=out_spec, out_shape=out_shape,
        scratch_shapes=[pltpu.VMEM((tmm, tn), F32)],
        compiler_params=_cparams("parallel", "parallel", "arbitrary"),
    )(a, b)


_MIX_PARTS = (("qkv", 0, NQKV), ("gate", NQKV, 1024), ("pool", NQKV + 1024, NPOOL), ("br", NQKV + 1024 + NPOOL, 2048),
              ("ab", NMIXP - 128, 128))


def _mix_in_fwd(x1, modv, nw, w_mix, *, name, exchange=None):
    n_tok = x1.shape[0]

    def body(x_ref, mod_ref, nw_ref, w_ref, u_ref, *p_refs):
        u, _, _, _ = _norm_mod(x_ref[...], nw_ref[...], mod_ref[0, 3:4, :], mod_ref[0, 4:5, :])
        ub = u.astype(BF)
        u_ref[...] = ub
        for (_, c0, w), p_ref in zip(_MIX_PARTS, p_refs):
            p_ref[...] = _nn(ub, w_ref[:, c0:c0 + w])

    tile = lambda w: pl.BlockSpec((TM, w), lambda i: (i, 0))
    return _pallas(
        body, (x1, modv, nw, w_mix), name=name, grid=(n_tok // TM,), sem=("parallel",), exchange=exchange,
        in_specs=[tile(D), pl.BlockSpec((1, 16, D), lambda i: (jnp.minimum(i, 1), 0, 0)), _const_spec((1, D)),
                  _const_spec(w_mix.shape)],
        out_specs=[tile(D)] + [tile(w) for _, _, w in _MIX_PARTS],
        out_shape=[jax.ShapeDtypeStruct((n_tok, D), BF)] + [jax.ShapeDtypeStruct((n_tok, w), F32) for _, _, w in _MIX_PARTS])


def _mix_in_bwd(dxo, x1, dqkv, dgate, dpool, dbr, dab_f, dab_b, modv, nw, w_mix, *, name):
    n_tok = x1.shape[0]

    def body(dxo_ref, x_ref, dqkv_ref, dgate_ref, dpool_ref, dbr_ref, dabf_ref, dabb_ref, mod_ref, nw_ref, w_ref,
             dx_ref, dp_ref, acc_ref):
        i = pl.program_id(0)
        lat = i >= 1
        scale = mod_ref[0, 4:5, :]
        _, n, xh, r = _norm_mod(x_ref[...], nw_ref[...], mod_ref[0, 3:4, :], scale)
        dp_ref[:, 0:NQKV] = dqkv_ref[...].astype(BF)
        dp_ref[:, NQKV:NQKV + 1024] = jnp.where(lat, dgate_ref[...], 0.0).astype(BF)
        dp_ref[:, NQKV + 1024:NQKV + 1536] = jnp.where(lat, dpool_ref[...], 0.0).astype(BF)
        dp_ref[:, NQKV + 1536:NMIXP - 128] = jnp.where(lat, dbr_ref[...], 0.0).astype(BF)
        dp_ref[:, NMIXP - 128:] = (dabf_ref[...] + dabb_ref[...]).astype(BF)
        du = _nt(dp_ref[...], w_ref[...])
        dx, dshift, dscale, dnw = _norm_mod_bwd(du, n, xh, r, nw_ref[...], scale)
        dx_ref[...] = jnp.where(lat, dxo_ref[...], 0.0) + dx

        @pl.when(i <= 1)
        def _():
            acc_ref[...] = jnp.zeros_like(acc_ref)

        acc_ref[0, 0:1, :] += dshift
        acc_ref[0, 1:2, :] += dscale
        acc_ref[0, 3:4, :] += dnw

    tile = lambda w: pl.BlockSpec((TM, w), lambda i: (i, 0))
    ltile = lambda w: pl.BlockSpec((TM, w), lambda i: (jnp.maximum(i - 1, 0), 0))
    return pl.pallas_call(
        body, name=name, grid=(n_tok // TM,),
        in_specs=[ltile(D), tile(D), tile(NQKV), ltile(1024), ltile(NPOOL), ltile(2048), tile(128), tile(128),
                  pl.BlockSpec((1, 16, D), lambda i: (jnp.minimum(i, 1), 0, 0)), _const_spec((1, D)), _const_spec(w_mix.shape)],
        out_specs=[tile(D), tile(NMIXP), pl.BlockSpec((1, 8, D), lambda i: (jnp.minimum(i, 1), 0, 0))],
        out_shape=[jax.ShapeDtypeStruct((n_tok, D), F32), jax.ShapeDtypeStruct((n_tok, NMIXP), BF),
                   jax.ShapeDtypeStruct((2, 8, D), F32)],
        compiler_params=_cparams("arbitrary"),
    )(dxo, x1, dqkv, dgate, dpool, dbr, dab_f, dab_b, modv, nw, w_mix)


def _qkv_act(pre, part):
    s = _silu(pre)
    if part == 2:
        return s
    nrm = s * lax.rsqrt(jnp.sum(s * s, axis=-1, keepdims=True) + EPS)
    return nrm * HD ** -0.5 if part == 0 else nrm


def _halo_specs(nt):
    r = TM // 8
    main = pl.BlockSpec((TM, NQKV), lambda i: (i, 0))
    prev = pl.BlockSpec((8, NQKV), lambda i: (jnp.maximum(i * r - 1, 0), 0))
    nxt = pl.BlockSpec((8, NQKV), lambda i: (jnp.minimum((i + 1) * r, nt * r - 1), 0))
    return main, prev, nxt


def _prep_fwd(p_qkv, conv_w8, *, name):
    n_tok = p_qkv.shape[0]
    nt = n_tok // TM

    def body(x_ref, xp_ref, xn_ref, w_ref, o_ref, pre_ref, win):
        i = pl.program_id(0)
        has_prev = (i != 0) & (i != 1)
        has_next = (i != 0) & (i != nt - 1)
        win[0:8, :] = jnp.where(has_prev, xp_ref[...], 0.0)
        win[8:8 + TM, :] = x_ref[...]
        win[8 + TM:, :] = jnp.where(has_next, xn_ref[...], 0.0)
        for hb in range(3 * NH):
            hs = slice(hb * HD, (hb + 1) * HD)
            pre = win[6:6 + TM, hs] * w_ref[0:1, hs]
            for k in range(1, 5):
                pre = pre + win[6 + k:6 + k + TM, hs] * w_ref[k:k + 1, hs]
            pre_ref[:, hs] = pre
            o_ref[:, hs] = _qkv_act(pre, hb // NH)

    main, prev, nxt = _halo_specs(nt)
    return pl.pallas_call(
        body, name=name, grid=(nt,),
        in_specs=[main, prev, nxt, pl.BlockSpec((8, NQKV), lambda i: (0, 0))],
        out_specs=[main, main], out_shape=[jax.ShapeDtypeStruct((n_tok, NQKV), F32)] * 2,
        scratch_shapes=[pltpu.VMEM((TM + 16, NQKV), F32)],
        compiler_params=_cparams("parallel"),
    )(p_qkv, p_qkv, p_qkv, conv_w8)


def _prep_bwd(p_qkv, pre, dqkv_f, dqkv_b, conv_w8, *, name):
    n_tok = p_qkv.shape[0]
    nt = n_tok // TM

    def body(x_ref, p_ref, pp_ref, pn_ref, g_ref, gp_ref, gn_ref, g2_ref, g2p_ref, g2n_ref, w_ref, dx_ref, dw_ref, pwin, gwin, dwin):
        i = pl.program_id(0)
        has_prev = (i != 0) & (i != 1)
        has_next = (i != 0) & (i != nt - 1)
        pwin[0:8, :] = jnp.where(has_prev, pp_ref[...], 0.0)
        pwin[8:8 + TM, :] = p_ref[...]
        pwin[8 + TM:, :] = jnp.where(has_next, pn_ref[...], 0.0)
        gwin[0:8, :] = jnp.where(has_prev, gp_ref[...] + g2p_ref[...], 0.0)
        gwin[8:8 + TM, :] = g_ref[...] + g2_ref[...]
        gwin[8 + TM:, :] = jnp.where(has_next, gn_ref[...] + g2n_ref[...], 0.0)

        @pl.when(i == 0)
        def _():
            dw_ref[...] = jnp.zeros_like(dw_ref)

        for hb in range(3 * NH):
            hs = slice(hb * HD, (hb + 1) * HD)
            _, vjp = jax.vjp(functools.partial(_qkv_act, part=hb // NH), pwin[:, hs])
            dwin[:, hs] = vjp(gwin[:, hs])[0]
            xv = x_ref[:, hs]
            dx = None
            for k in range(5):
                sh = dwin[10 - k:10 - k + TM, hs]
                dx = sh * w_ref[k:k + 1, hs] if dx is None else dx + sh * w_ref[k:k + 1, hs]
                dw_ref[k:k + 1, hs] += jnp.sum(sh * xv, axis=0, keepdims=True)
            dx_ref[:, hs] = dx.astype(BF)

    main, prev, nxt = _halo_specs(nt)
    wspec = pl.BlockSpec((8, NQKV), lambda i: (0, 0))
    return pl.pallas_call(
        body, name=name, grid=(nt,),
        in_specs=[main, main, prev, nxt, main, prev, nxt, main, prev, nxt, wspec],
        out_specs=[main, wspec],
        out_shape=[jax.ShapeDtypeStruct((n_tok, NQKV), BF), jax.ShapeDtypeStruct((8, NQKV), F32)],
        scratch_shapes=[pltpu.VMEM((TM + 16, NQKV), F32)] * 3,
        compiler_params=_cparams("arbitrary"),
    )(p_qkv, pre, pre, pre, dqkv_f, dqkv_f, dqkv_f, dqkv_b, dqkv_b, dqkv_b, conv_w8)


@jax.custom_vjp
def _mm_nn(a, b):
    return _nn(a, b)


@jax.custom_vjp
def _mm_nt(a, b):
    return _nt(a, b)


@jax.custom_vjp
def _mm_tn(a, b):
    return _tn(a, b)


_mm_nn.defvjp(lambda a, b: (_nn(a, b), (a, b)), lambda r, g: (_mm_nt(g, r[1]), _mm_tn(r[0], g)))
_mm_nt.defvjp(lambda a, b: (_nt(a, b), (a, b)), lambda r, g: (_mm_nn(g, r[1]), _mm_tn(g, r[0])))
_mm_tn.defvjp(lambda a, b: (_tn(a, b), (a, b)), lambda r, g: (_mm_nt(r[1], g), _mm_nn(r[0], g)))


def _each(f, *lists):
    return tuple(f(*a) for a in zip(*lists))


def _unit_tri_inv(ls, revs):
    ii = lax.broadcasted_iota(jnp.int32, (CH, CH), 0)
    jj = lax.broadcasted_iota(jnp.int32, (CH, CH), 1)
    eye = (ii == jj).astype(F32)
    xs = None
    s = 1
    while s < CH:
        same = (ii & -(2 * s)) == (jj & -(2 * s))
        off = {False: same & ((ii & s) != 0) & ((jj & s) == 0), True: same & ((jj & s) != 0) & ((ii & s) == 0)}
        cs = _each(lambda l, r: jnp.where(off[r], l, 0.0), ls, revs)
        if xs is None:
            xs = _each(lambda c: eye - c, cs)
        else:
            xc = _each(_nn, xs, cs)
            xcx = _each(_nn, xc, xs)
            xs = _each(lambda x, t: x - t, xs, xcx)
        s *= 2
    return xs


@functools.lru_cache(maxsize=None)
def _tri_solve(revs):
    @jax.custom_vjp
    def solve(ls, rhss):
        return _each(_mm_nn, _unit_tri_inv(ls, revs), rhss)

    def fwd(ls, rhss):
        ainv = _unit_tri_inv(ls, revs)
        xs = _each(_mm_nn, ainv, rhss)
        return xs, (ainv, xs)

    def bwd(res, gs):
        ainv, xs = res
        drhs = _each(_mm_tn, ainv, gs)
        return _each(lambda d, x: -_mm_nt(d, x), drhs, xs), drhs

    solve.defvjp(fwd, bwd)
    return solve


def _chunk_prep(q, k, v, beta, g, *, revs):
    ii = lax.broadcasted_iota(jnp.int32, (CH, CH), 0)
    jj = lax.broadcasted_iota(jnp.int32, (CH, CH), 1)
    eye = ii == jj
    incl_of = {False: ii >= jj, True: ii <= jj}
    strict_of = {False: ii > jj, True: ii < jj}
    g_row = _each(lambda t: jnp.sum(jnp.where(eye, t, 0.0), axis=0, keepdims=True), g)
    cum = _each(lambda t, r: jnp.sum(jnp.where(incl_of[r], t, 0.0), axis=1, keepdims=True), g_row, revs)
    cum_row = _each(lambda t: jnp.sum(jnp.where(eye, t, 0.0), axis=0, keepdims=True), cum)
    total = _each(lambda t: jnp.sum(t, axis=0, keepdims=True), g)
    decay = _each(lambda c, cr, r: jnp.where(incl_of[r], jnp.exp(jnp.where(incl_of[r], c - cr, 0.0)), 0.0), cum, cum_row, revs)
    kb = _each(jnp.multiply, k, beta)
    vb = _each(jnp.multiply, v, beta)
    kk = _each(_mm_nt, kb, k)
    lmat = _each(lambda t, dc, r: jnp.where(strict_of[r], t * dc, 0.0), kk, decay, revs)
    ecum = _each(jnp.exp, cum)
    rhs = _each(lambda a, b, e: jnp.concatenate([a, b * e], axis=1), vb, kb, ecum)
    sol = _tri_solve(revs)(lmat, rhs)
    qk = _each(_mm_nt, q, k)
    aqk = _each(jnp.multiply, qk, decay)
    qd = _each(jnp.multiply, q, ecum)
    kd = _each(lambda a, t, c: a * jnp.exp(t - c), k, total, cum)
    return sol, aqk, qd, kd, _each(jnp.exp, total)


def _chunk_rec(sol, aqk, qd, kd, bl, s):
    ws = _each(lambda so, st: _mm_nn(so[:, HD:], st), sol, s)
    v_new = _each(lambda so, t: so[:, :HD] - t, sol, ws)
    qs = _each(_mm_nn, qd, s)
    av = _each(_mm_nn, aqk, v_new)
    o = _each(jnp.add, qs, av)
    kv = _each(_mm_tn, kd, v_new)
    s_new = _each(lambda st, b, u: st * b + u, s, bl, kv)
    return o, s_new


def _lane_col(x, c):
    lane = lax.broadcasted_iota(jnp.int32, x.shape, 1)
    return jnp.sum(jnp.where(lane == c, x, 0.0), axis=1, keepdims=True)


def _gates(ab, cst):
    z = ab + cst[1:2, :]
    softplus = jnp.maximum(z, 0.0) + jnp.log(1.0 + jnp.exp(-jnp.abs(z)))
    return jax.nn.sigmoid(ab), -cst[0:1, :] * softplus, -cst[0:1, :] * jax.nn.sigmoid(z)


def _beta_g(gates, d, h):
    sig, g, dg = gates
    return _lane_col(sig, NH * d + h), _lane_col(g, 2 * NH + NH * d + h), _lane_col(dg, 2 * NH + NH * d + h)


STEPS = 2
TS = STEPS * CH
_CHAINS = tuple((t, d, h) for t in range(STEPS) for d in (0, 1) for h in range(NH))
_REVS = tuple(bool(d) for _, d, _ in _CHAINS)
_PER_STEP = 2 * NH


def _chain_inputs(refs, r0s, ab_refs, cst):
    hs = lambda h: slice(h * HD, (h + 1) * HD)
    gates = [[_gates(ab_refs[d][pl.ds(r0s[t][d], CH), :], cst) for d in (0, 1)] for t in range(STEPS)]
    q = _each(lambda c: refs[c[1]][0][pl.ds(r0s[c[0]][c[1]], CH), hs(c[2])], _CHAINS)
    k = _each(lambda c: refs[c[1]][1][pl.ds(r0s[c[0]][c[1]], CH), hs(c[2])], _CHAINS)
    v = _each(lambda c: refs[c[1]][2][pl.ds(r0s[c[0]][c[1]], CH), hs(c[2])], _CHAINS)
    bg = _each(lambda c: _beta_g(gates[c[0]][c[1]], c[1], c[2]), _CHAINS)
    return q, k, v, bg


def _of_step(parts, t):
    return tuple(p[t * _PER_STEP:(t + 1) * _PER_STEP] for p in parts)


def _scan_fwd(qkv, ab, cst, s0, *, row_blk0, nb, name, exchange=None):
    cb = TS // CH
    w = NH * HD

    def body(qf, kf, vf, abf, qb, kb, vb, abb, cst_ref, s0_ref, of_ref, ob_ref, sallf_ref, sallb_ref, sfin_ref, s_scr):
        i = pl.program_id(0)

        @pl.when(i == 0)
        def _():
            s_scr[...] = s0_ref[...]

        o_refs, sall_refs = (of_ref, ob_ref), (sallf_ref, sallb_ref)

        def chunks(ci, carry):
            cs = [(ci * STEPS + t, cb - 1 - ci * STEPS - t) for t in range(STEPS)]
            r0s = [tuple(pl.multiple_of(c * CH, CH) for c in ct) for ct in cs]
            q, k, v, bg = _chain_inputs(((qf, kf, vf), (qb, kb, vb)), r0s, (abf, abb), cst_ref[...])
            parts = _chunk_prep(q, k, v, _each(lambda t: t[0], bg), _each(lambda t: t[1], bg), revs=_REVS)
            s = _each(lambda c: s_scr[c[1], c[2]], _CHAINS[:_PER_STEP])
            for t in range(STEPS):
                for (_, d, h), sv in zip(_CHAINS, s):
                    sall_refs[d][cs[t][d], h] = sv
                o, s = _chunk_rec(*_of_step(parts, t), s)
                for (_, d, h), ov in zip(_CHAINS, o):
                    o_refs[d][pl.ds(r0s[t][d], CH), h * HD:(h + 1) * HD] = ov
            for (_, d, h), sv in zip(_CHAINS, s):
                s_scr[d, h] = sv
            return carry

        lax.fori_loop(0, cb // STEPS, chunks, 0)

        @pl.when(i == nb - 1)
        def _():
            sfin_ref[...] = s_scr[...]

    pos = (lambda i: i, lambda i: nb - 1 - i)
    col = lambda d, c: pl.BlockSpec((TS, w), lambda i: (row_blk0 + pos[d](i), c))
    abs_ = lambda d: pl.BlockSpec((TS, 128), lambda i: (row_blk0 + pos[d](i), 0))
    full4 = pl.BlockSpec((2, NH, HD, HD), lambda i: (0, 0, 0, 0))
    o_spec = lambda d: pl.BlockSpec((TS, w), lambda i: (pos[d](i), 0))
    sall_spec = lambda d: pl.BlockSpec((cb, NH, HD, HD), lambda i: (pos[d](i), 0, 0, 0))
    return _pallas(
        body, (qkv, qkv, qkv, ab, qkv, qkv, qkv, ab, cst, s0), name=name, grid=(nb,), sem=("arbitrary",), exchange=exchange,
        in_specs=[col(0, 0), col(0, 1), col(0, 2), abs_(0), col(1, 0), col(1, 1), col(1, 2), abs_(1),
                  pl.BlockSpec((8, 128), lambda i: (0, 0)), full4],
        out_specs=[o_spec(0), o_spec(1), sall_spec(0), sall_spec(1), full4],
        out_shape=[jax.ShapeDtypeStruct((nb * TS, w), F32)] * 2 + [jax.ShapeDtypeStruct((nb * cb, NH, HD, HD), F32)] * 2
        + [jax.ShapeDtypeStruct((2, NH, HD, HD), F32)],
        scratch_shapes=[pltpu.VMEM((2, NH, HD, HD), F32)])


def _scan_bwd(qkv, ab, cst, sall_f, sall_b, do, dsfin, dqkv_f, dqkv_b, dab_f, dab_b, dcst, *, row_blk0, nb, has_do, name,
              exchange=None):
    cb = TS // CH
    w = NH * HD

    def body(qf, kf, vf, abf, qb, kb, vb, abb, cst_ref, sallf_ref, sallb_ref, dof_ref, dob_ref, dsfin_ref, _f, _b, _af, _ab, dcst_in,
             dqkvf_ref, dqkvb_ref, dabf_ref, dabb_ref, dcst_ref, ds0_ref, ds_scr):
        i = pl.program_id(0)

        @pl.when(i == 0)
        def _():
            ds_scr[...] = dsfin_ref[...]
            dcst_ref[...] = dcst_in[...]

        lane = lax.broadcasted_iota(jnp.int32, (CH, 128), 1)
        lane1 = lax.broadcasted_iota(jnp.int32, (1, 128), 1)
        sall_refs, do_refs = (sallf_ref, sallb_ref), (dof_ref, dob_ref)
        dqkv_refs, dab_refs = (dqkvf_ref, dqkvb_ref), (dabf_ref, dabb_ref)

        def chunks(ci, carry):
            cs = [(cb - 1 - ci * STEPS - t, ci * STEPS + t) for t in range(STEPS)]
            r0s = [tuple(pl.multiple_of(c * CH, CH) for c in ct) for ct in cs]
            q, k, v, bg = _chain_inputs(((qf, kf, vf), (qb, kb, vb)), r0s, (abf, abb), cst_ref[...])
            beta, g = _each(lambda t: t[0], bg), _each(lambda t: t[1], bg)
            parts, prep_vjp = jax.vjp(functools.partial(_chunk_prep, revs=_REVS), q, k, v, beta, g)
            ds = _each(lambda c: ds_scr[c[1], c[2]], _CHAINS[:_PER_STEP])
            dparts = []
            for t in range(STEPS):
                s = _each(lambda c: sall_refs[c[1]][cs[t][c[1]], c[2]], _CHAINS[:_PER_STEP])
                _, rec_vjp = jax.vjp(_chunk_rec, *_of_step(parts, t), s)
                do_t = _each(lambda c: do_refs[c[1]][pl.ds(r0s[t][c[1]], CH), c[2] * HD:(c[2] + 1) * HD] if has_do
                             else jnp.zeros((CH, HD), F32), _CHAINS[:_PER_STEP])
                *dpt, ds = rec_vjp((do_t, ds))
                dparts.append(dpt)
            for (_, d, h), dsv in zip(_CHAINS, ds):
                ds_scr[d, h] = dsv
            dq, dk, dv, dbeta, dg = prep_vjp(tuple(sum((dparts[t][j] for t in range(STEPS)), ()) for j in range(len(dparts[0]))))
            dab = [[jnp.zeros((CH, 128), F32), jnp.zeros((CH, 128), F32)] for _ in range(STEPS)]
            dal = jnp.zeros((1, 128), F32)
            for n, (t, d, h) in enumerate(_CHAINS):
                for part, val in enumerate((dq[n], dk[n], dv[n])):
                    dqkv_refs[d][pl.ds(r0s[t][d], CH), part * w + h * HD:part * w + (h + 1) * HD] = val
                dbraw = dbeta[n] * beta[n] * (1.0 - beta[n])
                daraw = dg[n] * bg[n][2]
                dab[t][d] = dab[t][d] + jnp.where(lane == NH * d + h, dbraw, 0.0) + jnp.where(lane == 2 * NH + NH * d + h, daraw, 0.0)
                dal = dal + jnp.where(lane1 == 2 * NH + NH * d + h, jnp.sum(dg[n] * g[n], axis=0, keepdims=True), 0.0)
            dsum = jnp.zeros((CH, 128), F32)
            for t in range(STEPS):
                for d in (0, 1):
                    dab_refs[d][pl.ds(r0s[t][d], CH), :] = dab[t][d]
                    dsum = dsum + dab[t][d]
            dcst_ref[0:1, :] += dal
            dcst_ref[1:2, :] += jnp.sum(jnp.where(lane >= 2 * NH, dsum, 0.0), axis=0, keepdims=True)
            return carry

        lax.fori_loop(0, cb // STEPS, chunks, 0)

        @pl.when(i == nb - 1)
        def _():
            ds0_ref[...] = ds_scr[...]

    pos = (lambda i: nb - 1 - i, lambda i: i)
    col = lambda d, c: pl.BlockSpec((TS, w), lambda i: (row_blk0 + pos[d](i), c))
    abs_ = lambda d: pl.BlockSpec((TS, 128), lambda i: (row_blk0 + pos[d](i), 0))
    full4 = pl.BlockSpec((2, NH, HD, HD), lambda i: (0, 0, 0, 0))
    small = pl.BlockSpec((8, 128), lambda i: (0, 0))
    hbm = pl.BlockSpec(memory_space=pl.ANY)
    sall_spec = lambda d: pl.BlockSpec((cb, NH, HD, HD), lambda i: (pos[d](i), 0, 0, 0))
    do_spec = (lambda d: pl.BlockSpec((TS, w), lambda i: (pos[d](i), 0))) if has_do else (lambda d: small)
    acc_specs = [pl.BlockSpec((TS, 3 * w), lambda i: (row_blk0 + pos[0](i), 0)),
                 pl.BlockSpec((TS, 3 * w), lambda i: (row_blk0 + pos[1](i), 0)), abs_(0), abs_(1), small]
    return _pallas(
        body, (qkv, qkv, qkv, ab, qkv, qkv, qkv, ab, cst, sall_f, sall_b, do, do, dsfin, dqkv_f, dqkv_b, dab_f, dab_b, dcst),
        name=name, grid=(nb,), sem=("arbitrary",), exchange=exchange,
        in_specs=[col(0, 0), col(0, 1), col(0, 2), abs_(0), col(1, 0), col(1, 1), col(1, 2), abs_(1), small,
                  sall_spec(0), sall_spec(1), do_spec(0), do_spec(1), full4, hbm, hbm, hbm, hbm, small],
        out_specs=acc_specs + [full4],
        out_shape=[jax.ShapeDtypeStruct(dqkv_f.shape, F32), jax.ShapeDtypeStruct(dqkv_b.shape, F32),
                   jax.ShapeDtypeStruct(dab_f.shape, F32), jax.ShapeDtypeStruct(dab_b.shape, F32),
                   jax.ShapeDtypeStruct((8, 128), F32), jax.ShapeDtypeStruct((2, NH, HD, HD), F32)],
        aliases={14: 0, 15: 1, 16: 2, 17: 3, 18: 4},
        scratch_shapes=[pltpu.VMEM((2, NH, HD, HD), F32)])


def _pool(xin, *, row0, transpose, name):
    n_tok = xin.shape[0] - row0
    rows = n_tok // GW
    pad = 8 * GW
    tt = 512
    gsh = GW.bit_length() - 1

    def body(x_hbm, o_hbm, x_ref, o_ref, ybuf, in_sems, out_sems):
        cols = [slice(gi * 128, (gi + 1) * 128) for gi in range(len(POOL_WINDOWS))]
        fetch = [pltpu.make_async_copy(x_hbm.at[:, cs], x_ref.at[:, cs], in_sems.at[gi]) for gi, cs in enumerate(cols)]
        store = [pltpu.make_async_copy(o_ref.at[:, cs], o_hbm.at[:, cs], out_sems.at[gi]) for gi, cs in enumerate(cols)]
        for cp in fetch:
            cp.start()
        ii = lax.broadcasted_iota(jnp.int32, (128, 128), 0)
        jj = lax.broadcasted_iota(jnp.int32, (128, 128), 1)
        same_row = (ii >> gsh) == (jj >> gsh)
        ci, cj = ii & (GW - 1), jj & (GW - 1)
        tok = lax.broadcasted_iota(jnp.int32, (tt, 1), 0)
        zpad = jnp.zeros((pad, 128), F32)
        for gi, wdw in enumerate(POOL_WINDOWS):
            lo, hi = wdw // 2, wdw - wdw // 2
            if transpose:
                band = same_row & (ci - cj >= -lo) & (ci - cj < hi)
                offs = range(-hi + 1, lo + 1)
            else:
                band = same_row & (cj - ci >= -lo) & (cj - ci < hi)
                offs = range(-lo, hi)
            bandm = band.astype(BF)
            cs = cols[gi]
            ybuf[0:pad, :] = zpad
            ybuf[pad + n_tok:, :] = zpad
            fetch[gi].wait()

            def inv_area(t0):
                t = t0 + tok
                r, c = t >> gsh, t & (GW - 1)
                nr = jnp.minimum(r + hi, rows) - jnp.maximum(r - lo, 0)
                nc = jnp.minimum(c + hi, GW) - jnp.maximum(c - lo, 0)
                return 1.0 / (nr * nc).astype(F32)

            def col_pass(b, carry):
                t0 = pl.multiple_of(b * tt, tt)
                xv = x_ref[pl.ds(row0 + t0, tt), cs]
                if transpose:
                    xv = xv * inv_area(t0)
                hi_part = xv.astype(BF)
                lo_part = (xv - hi_part.astype(F32)).astype(BF)
                for s in range(tt // 128):
                    sl = slice(s * 128, (s + 1) * 128)
                    y = (jnp.dot(bandm, hi_part[sl], preferred_element_type=F32)
                         + jnp.dot(bandm, lo_part[sl], preferred_element_type=F32))
                    ybuf[pl.ds(pad + t0 + s * 128, 128), :] = y
                return carry

            lax.fori_loop(0, n_tok // tt, col_pass, 0)

            def row_pass(b, carry):
                t0 = pl.multiple_of(b * tt, tt)
                acc = ybuf[pl.ds(pad + t0 + offs[0] * GW, tt), :]
                for dr in offs[1:]:
                    acc = acc + ybuf[pl.ds(pad + t0 + dr * GW, tt), :]
                xv = x_ref[pl.ds(row0 + t0, tt), cs]
                if not transpose:
                    acc = acc * inv_area(t0)
                o_ref[pl.ds(t0, tt), cs] = acc - xv
                return carry

            lax.fori_loop(0, n_tok // tt, row_pass, 0)
            store[gi].start()
        for cp in store:
            cp.wait()

    groups = len(POOL_WINDOWS)
    return pl.pallas_call(
        body, name=name, out_shape=jax.ShapeDtypeStruct((n_tok, NPOOL), F32),
        in_specs=[pl.BlockSpec(memory_space=pl.ANY)], out_specs=pl.BlockSpec(memory_space=pl.ANY),
        scratch_shapes=[pltpu.VMEM(xin.shape, F32), pltpu.VMEM((n_tok, NPOOL), F32), pltpu.VMEM((n_tok + 2 * pad, 128), F32),
                        pltpu.SemaphoreType.DMA((groups,)), pltpu.SemaphoreType.DMA((groups,))],
        compiler_params=pltpu.CompilerParams(vmem_limit_bytes=VMEM_LIMIT),
    )(xin)


def _merge_parts(of, ob, pgate, pd, br, gnw, pw_ref, pscale, wg_ref, wp_ref):
    o = of + ob
    ons, ohs, rs = [], [], []
    for h in range(NH):
        oh = o[:, h * HD:(h + 1) * HD]
        r = lax.rsqrt(jnp.mean(oh * oh, axis=-1, keepdims=True) + EPS)
        ohs.append(oh * r)
        rs.append(r)
        ons.append(oh * r * gnw)
    on = jnp.concatenate(ons, axis=1)
    sig_gate = jax.nn.sigmoid(pgate)
    silu_gate = pgate * sig_gate
    og = on * silu_gate
    y_gdn = _nn(og, wg_ref[...])
    ypre = jnp.concatenate([_nn(pd[:, g * 128:(g + 1) * 128], pw_ref[g]) for g in range(4)], axis=1)
    yp = ypre * pscale
    y_pool = _nn(yp, wp_ref[...])
    g_pool = jax.nn.sigmoid(br[:, :D])
    g_gdn = jax.nn.sigmoid(br[:, D:])
    return dict(on=on, ohs=ohs, rs=rs, og=og, y_gdn=y_gdn, ypre=ypre, yp=yp, y_pool=y_pool, g_pool=g_pool, g_gdn=g_gdn,
                sig_gate=sig_gate, silu_gate=silu_gate)


def _merge_fwd(x1, of, ob, pgate, pd, br, modv, gnw, pool_w, pscale, w_gdn, w_pool, w_mo, *, name):
    n_tok = of.shape[0]

    def body(x_ref, of_ref, ob_ref, pg_ref, pd_ref, br_ref, mod_ref, gnw_ref, pw_ref, ps_ref, wg_ref, wp_ref, wmo_ref,
             x2_ref, og_ref, yp_ref, m_ref, mix_ref):
        t = _merge_parts(of_ref[...], ob_ref[...], pg_ref[...], pd_ref[...], br_ref[...], gnw_ref[...], pw_ref, ps_ref[...],
                         wg_ref, wp_ref)
        m = t["g_pool"] * t["y_pool"] + t["g_gdn"] * t["y_gdn"]
        mix = _nn(m, wmo_ref[...])
        og_ref[...] = t["og"].astype(BF)
        yp_ref[...] = t["yp"].astype(BF)
        m_ref[...] = m.astype(BF)
        mix_ref[...] = mix.astype(BF)
        x2_ref[...] = x_ref[...] + mod_ref[0, 5:6, :] * mix

    tile = lambda w: pl.BlockSpec((TM, w), lambda i: (i, 0))
    ctile = lambda w: pl.BlockSpec((TM, w), lambda i: (i + 1, 0))
    return pl.pallas_call(
        body, name=name, grid=(n_tok // TM,),
        in_specs=[ctile(D), tile(D), tile(D), ctile(D), tile(NPOOL), ctile(2 * D),
                  pl.BlockSpec((1, 16, D), lambda i: (1, 0, 0)), _const_spec((1, HD)), _const_spec((4, 128, 128)),
                  _const_spec((1, NPOOL)), _const_spec((D, D)), _const_spec((NPOOL, D)), _const_spec((D, D))],
        out_specs=[tile(D), tile(D), tile(NPOOL), tile(D), tile(D)],
        out_shape=[jax.ShapeDtypeStruct((n_tok, D), F32), jax.ShapeDtypeStruct((n_tok, D), BF),
                   jax.ShapeDtypeStruct((n_tok, NPOOL), BF), jax.ShapeDtypeStruct((n_tok, D), BF),
                   jax.ShapeDtypeStruct((n_tok, D), BF)],
        compiler_params=_cparams("parallel"),
    )(x1, of, ob, pgate, pd, br, modv, gnw, pool_w, pscale, w_gdn, w_pool, w_mo)


def _merge_bwd(dx2, mix, of, ob, pgate, pd, br, modv, gnw, pool_w, pscale, w_gdn, w_pool, w_mo, *, name):
    n_tok = of.shape[0]

    def body(dx2_ref, mix_ref, of_ref, ob_ref, pg_ref, pd_ref, br_ref, mod_ref, gnw_ref, pw_ref, ps_ref, wg_ref, wp_ref, wmo_ref,
             do_ref, dgate_ref, dpd_ref, dbr_ref, dmix_ref, dyg_ref, dyp_ref, acc_ref, dpw_ref):
        i = pl.program_id(0)
        pgate, pdv, gnw = pg_ref[...], pd_ref[...], gnw_ref[...]
        t = _merge_parts(of_ref[...], ob_ref[...], pgate, pdv, br_ref[...], gnw, pw_ref, ps_ref[...], wg_ref, wp_ref)
        dx2v = dx2_ref[...]
        dmix = mod_ref[0, 5:6, :] * dx2v
        dmixb = dmix.astype(BF)
        dmix_ref[...] = dmixb
        dm = _nt(dmixb, wmo_ref[...])
        gp, gg = t["g_pool"], t["g_gdn"]
        dbr_ref[:, :D] = (dm * t["y_pool"] * gp * (1.0 - gp)).astype(BF)
        dbr_ref[:, D:] = (dm * t["y_gdn"] * gg * (1.0 - gg)).astype(BF)
        dyp = (dm * gp).astype(BF)
        dyg = (dm * gg).astype(BF)
        dyp_ref[...] = dyp
        dyg_ref[...] = dyg
        dyp_in = _nt(dyp, wp_ref[...])
        dypre = dyp_in * ps_ref[...]
        for g in range(4):
            gs = slice(g * 128, (g + 1) * 128)
            dpd_ref[:, gs] = _nt(dypre[:, gs], pw_ref[g])
        dog = _nt(dyg, wg_ref[...])
        dgate_ref[...] = (dog * t["on"] * (t["sig_gate"] * (1.0 + pgate - t["silu_gate"]))).astype(BF)
        don = dog * t["silu_gate"]
        dgnw = jnp.zeros((1, HD), F32)
        for h in range(NH):
            hs = slice(h * HD, (h + 1) * HD)
            donh, oh, r = don[:, hs], t["ohs"][h], t["rs"][h]
            dgnw = dgnw + jnp.sum(donh * oh, axis=0, keepdims=True)
            doh = donh * gnw
            do_ref[:, hs] = r * (doh - oh * jnp.mean(doh * oh, axis=-1, keepdims=True))

        @pl.when(i == 0)
        def _():
            acc_ref[...] = jnp.zeros_like(acc_ref)
            dpw_ref[...] = jnp.zeros_like(dpw_ref)

        acc_ref[0:1, :] += jnp.sum(dx2v * mix_ref[...].astype(F32), axis=0, keepdims=True)
        acc_ref[1:2, 0:HD] += dgnw
        acc_ref[2:3, 0:NPOOL] += jnp.sum(dyp_in * t["ypre"], axis=0, keepdims=True)
        for g in range(4):
            gs = slice(g * 128, (g + 1) * 128)
            dpw_ref[g] += _tn(pdv[:, gs], dypre[:, gs])

    tile = lambda w: pl.BlockSpec((TM, w), lambda i: (i, 0))
    ctile = lambda w: pl.BlockSpec((TM, w), lambda i: (i + 1, 0))
    return pl.pallas_call(
        body, name=name, grid=(n_tok // TM,),
        in_specs=[tile(D), tile(D), tile(D), tile(D), ctile(D), tile(NPOOL), ctile(2 * D),
                  pl.BlockSpec((1, 16, D), lambda i: (1, 0, 0)), _const_spec((1, HD)), _const_spec((4, 128, 128)),
                  _const_spec((1, NPOOL)), _const_spec((D, D)), _const_spec((NPOOL, D)), _const_spec((D, D))],
        out_specs=[tile(D), tile(D), tile(NPOOL), tile(2 * D), tile(D), tile(D), tile(D),
                   pl.BlockSpec((8, D), lambda i: (0, 0)), pl.BlockSpec((4, 128, 128), lambda i: (0, 0, 0))],
        out_shape=[jax.ShapeDtypeStruct((n_tok, D), F32), jax.ShapeDtypeStruct((n_tok, D), BF),
                   jax.ShapeDtypeStruct((n_tok, NPOOL), F32), jax.ShapeDtypeStruct((n_tok, 2 * D), BF),
                   jax.ShapeDtypeStruct((n_tok, D), BF), jax.ShapeDtypeStruct((n_tok, D), BF), jax.ShapeDtypeStruct((n_tok, D), BF),
                   jax.ShapeDtypeStruct((8, D), F32), jax.ShapeDtypeStruct((4, 128, 128), F32)],
        compiler_params=_cparams("arbitrary"),
    )(dx2, mix, of, ob, pgate, pd, br, modv, gnw, pool_w, pscale, w_gdn, w_pool, w_mo)


def _split(results, n):
    return (*results[:n], list(results[n:]))


def _local_step(ctx, x, target, modv, p, late=None):
    t_lat = x.shape[0]
    n_all = t_lat + TM
    nbc, nbx = TM // TS, t_lat // TS
    mod_lat = modv[1:2]
    gather = (lambda arrs: _ChipExchange(arrs, False)) if late else (lambda arrs: None)
    scatter = (lambda arrs: _ChipExchange(arrs, True)) if late else (lambda arrs: None)

    x1, h1, gu1, f1, *got = _ffn_fwd(x, modv, p["norm1"], p["w1_in"], p["w1_out"], mrow=0, name="ffn1_fwd", ctx=ctx,
                                     exchange=gather(late and late[0]))
    if late:
        p = {**p, "w_mix": _regroup_mix(_from_chip_major_cols(got[0])), "conv": jnp.pad(_from_chip_major_cols(got[1]), ((0, 3), (0, 0)))}
    u, p_qkv, p_gate, p_pool, p_br, p_ab, *got = _mix_in_fwd(x1, modv, p["norm2"], p["w_mix"], name="mix_in_fwd",
                                                              exchange=gather(late and late[2]))
    if late:
        p = {**p, "w_gdn": got[0].reshape(D, D), "w_pool": _from_chip_major_cols(got[1]), "w_mo": got[2].reshape(D, D)}
    qkv, pre_qkv = _prep_fwd(p_qkv, p["conv"], name="prep_fwd")
    s_zero = jnp.zeros((2, NH, HD, HD), F32)
    _, _, sall_cf, sall_cb, s_ctx = _scan_fwd(qkv, p_ab, p["cst"], s_zero, row_blk0=0, nb=nbc, name="scan_ctx")
    o_f, o_b, sall_f, sall_b, _, *got = _scan_fwd(qkv, p_ab, p["cst"], s_ctx, row_blk0=nbc, nb=nbx, name="scan_lat",
                                                  exchange=gather(late and late[1]))
    if late:
        p = {**p, "w2_in": got[0], "w2_out": got[1].reshape(FF, D)}
    pd = _pool(p_pool, row0=TM, transpose=False, name="pool_fwd")
    merge_w = (modv, p["gnw"], p["pool_w"], p["pscale"], p["w_gdn"], p["w_pool"], p["w_mo"])
    x2, og, yp, m, mix = _merge_fwd(x1, o_f, o_b, p_gate, pd, p_br, *merge_w, name="merge_fwd")
    dx3, h3, gu3, f3, acc_fin = _ffn_fwd(x2, mod_lat, p["norm3"], p["w2_in"], p["w2_out"], mrow=6, name="ffn2_fwd",
                                         head=(target, p["fnorm"]))

    dx2, a3, df3, dgu3, acc3 = _ffn_bwd(dx3, x2, gu3, f3, mod_lat, p["norm3"], p["w2_in"], p["w2_out"], mrow=6,
                                        name="ffn2_bwd")
    g = {}
    tkl = _k_tile(t_lat)
    g["w2_out"] = _matmul_tn(a3, df3, tmm=FF // 2, tn=D, tk=tkl, name="ffn2_wout_grad").reshape(NCHIP, FF // NCHIP, D)
    g["w2_in"] = _matmul_tn(h3, dgu3, tmm=D, tn=2 * FF // NCHIP, tk=tkl, nsplit=NCHIP, name="ffn2_win_grad")
    do, dgate, dpd, dbr, dmix, dyg, dyp, acc_m, dpw = _merge_bwd(dx2, mix, o_f, o_b, p_gate, pd, p_br, *merge_w, name="merge_bwd")
    g["w_mo"] = _matmul_tn(m, dmix, tmm=D, tn=D, tk=tkl, name="wmo_grad").reshape(NCHIP, D // NCHIP, D)
    g["w_gdn"] = _matmul_tn(og, dyg, tmm=D, tn=D, tk=tkl, name="wgdn_grad").reshape(NCHIP, D // NCHIP, D)
    g["w_pool"] = _matmul_tn(yp, dyp, tmm=NPOOL, tn=D // NCHIP, tk=tkl, nsplit=NCHIP, name="wpool_grad")
    dpool_in = _pool(dpd, row0=0, transpose=True, name="pool_bwd")
    acc = (lax.empty((n_all, NQKV), F32), lax.empty((n_all, NQKV), F32), lax.empty((n_all, 128), F32),
           lax.empty((n_all, 128), F32), jnp.zeros((8, 128), F32))
    behind_scan = ("w2_in", "w2_out", "w_gdn", "w_pool", "w_mo")
    *acc, ds_ctx, landed = _split(_scan_bwd(qkv, p_ab, p["cst"], sall_f, sall_b, do, s_zero, *acc, row_blk0=nbc, nb=nbx, has_do=True,
                                            name="scan_lat_bwd", exchange=scatter([g[k] for k in behind_scan])), 6)
    landed = dict(zip(behind_scan, landed))
    dqkv_f, dqkv_b, dab_f, dab_b, dcst, _ = _scan_bwd(qkv, p_ab, p["cst"], sall_cf, sall_cb, jnp.zeros((8, 128), F32), ds_ctx, *acc,
                                                      row_blk0=0, nb=nbc, has_do=False, name="scan_ctx_bwd")
    dpqkv, dconv = _prep_bwd(p_qkv, pre_qkv, dqkv_f, dqkv_b, p["conv"], name="prep_bwd")
    dx1, dp, acc_mix = _mix_in_bwd(dx2, x1, dpqkv, dgate, dpool_in, dbr, dab_f, dab_b, modv, p["norm2"], p["w_mix"],
                                   name="mix_in_bwd")
    tka = _k_tile(n_all)
    g["w_mix"] = _chip_major_cols(_ungroup_mix(_matmul_tn(u, dp, tmm=256, tn=NMIXP, tk=_k_tile(n_all, 1024), name="wmix_grad")))
    dx_lat, a1, df1, dgu1, acc1, got = _split(_ffn_bwd(dx1, x, gu1, f1, modv, p["norm1"], p["w1_in"], p["w1_out"], mrow=0, ctx=ctx,
                                                   name="ffn1_bwd", exchange=scatter([g["w_mix"]])), 5)
    landed.update(zip(("w_mix",), got))
    g["w1_out"] = _matmul_tn(a1, df1, tmm=FF // 2, tn=D, tk=tka, name="ffn1_wout_grad").reshape(NCHIP, FF // NCHIP, D)
    g["w1_in"] = _matmul_tn(h1, dgu1, tmm=D, tn=2 * FF // NCHIP, tk=tka, nsplit=NCHIP, name="ffn1_win_grad")

    small = dict(norm1=acc1[0, 3] + acc1[1, 3], norm2=acc_mix[0, 3] + acc_mix[1, 3], norm3=acc3[0, 3], fnorm=acc_fin[0],
                 gnw=acc_m[1, :HD], pscale=acc_m[2, :NPOOL], pool_w=dpw, conv=dconv[:5],
                 a_log=dcst[0, 2 * NH:4 * NH], dt_bias=dcst[1, 2 * NH:4 * NH])
    zero = jnp.zeros((D,), F32)
    dmod = jnp.stack([
        jnp.stack([acc1[0, 0], acc1[0, 1], acc1[0, 2], acc_mix[0, 0], acc_mix[0, 1], zero, zero, zero, zero]),
        jnp.stack([acc1[1, 0], acc1[1, 1], acc1[1, 2], acc_mix[1, 0], acc_mix[1, 1], acc_m[0], acc3[0, 0], acc3[0, 1], acc3[0, 2]]),
    ])
    return jnp.sum(acc_fin[1]), dx_lat, g, landed, small, dmod


_HI = lax.Precision.HIGHEST
ADA_BLOCKS = 3


def _ada_fwd(c_all, w_sh, b_sh, *, name):
    def body(c_ref, w_ref, b_ref, o_ref):
        o_ref[...] = jnp.dot(_silu(c_ref[...]), w_ref[...], precision=_HI, preferred_element_type=F32) + b_ref[...]

    cols = w_sh.shape[1]
    cb = cols // ADA_BLOCKS
    return pl.pallas_call(
        body, name=name, grid=(ADA_BLOCKS,), out_shape=jax.ShapeDtypeStruct((16, cols), F32),
        in_specs=[pl.BlockSpec((16, D), lambda j: (0, 0)), pl.BlockSpec((D, cb), lambda j: (0, j)), pl.BlockSpec((1, cb), lambda j: (0, j))],
        out_specs=pl.BlockSpec((16, cb), lambda j: (0, j)), compiler_params=_cparams("parallel"))(c_all, w_sh, b_sh)


def _ada_bwd(c_all, dm, w_sh, *, name):
    def body(c_ref, dm_ref, w_ref, dw_ref, dc_ref):
        sc = _silu(c_ref[...])
        dw_ref[...] = lax.dot_general(sc, dm_ref[...], (((0,), (0,)), ((), ())), precision=_HI, preferred_element_type=F32)
        part = lax.dot_general(dm_ref[8:9, :], w_ref[...], (((1,), (1,)), ((), ())), precision=_HI, preferred_element_type=F32)

        @pl.when(pl.program_id(0) == 0)
        def _():
            dc_ref[...] = jnp.zeros_like(dc_ref)

        dc_ref[...] += jnp.broadcast_to(part, dc_ref.shape)

    cols = w_sh.shape[1]
    cb = cols // ADA_BLOCKS
    wspec = pl.BlockSpec((D, cb), lambda j: (0, j))
    return pl.pallas_call(
        body, name=name, grid=(ADA_BLOCKS,),
        out_shape=[jax.ShapeDtypeStruct(w_sh.shape, F32), jax.ShapeDtypeStruct((8, D), F32)],
        in_specs=[pl.BlockSpec((16, D), lambda j: (0, 0)), pl.BlockSpec((16, cb), lambda j: (0, j)), wspec],
        out_specs=[wspec, pl.BlockSpec((8, D), lambda j: (0, 0))], compiler_params=_cparams("arbitrary"))(c_all, dm, w_sh)


def _cctx_grad(parts, c_ctx, *, name):
    def body(p_ref, c_ref, o_ref):
        tot = (p_ref[0, 0:1, :] + p_ref[2, 0:1, :]) + (p_ref[4, 0:1, :] + p_ref[6, 0:1, :])
        o_ref[...] = tot * _dsilu(c_ref[...])

    return pl.pallas_call(body, name=name, out_shape=jax.ShapeDtypeStruct((1, D), F32))(parts, c_ctx)


_MESH = pl.DeviceIdType.MESH
_ANY = pl.BlockSpec(memory_space=pl.ANY)


def _flip(v, bit):
    return (1 - v) if bit else v


def _all_gather8(x, *, name):
    def body(x_ref, out_ref, send_sems, recv_sems, local_sem):
        mx, my, mc = lax.axis_index("x"), lax.axis_index("y"), lax.axis_index("c")
        me = 4 * mx + 2 * my + mc
        mine = pltpu.make_async_copy(x_ref, out_ref.at[me], local_sem)
        mine.start()
        sends, recvs = [], []
        for k in range(1, 8):
            px, py, pc = _flip(mx, k & 4), _flip(my, k & 2), _flip(mc, k & 1)
            sends.append(pltpu.make_async_remote_copy(src_ref=x_ref, dst_ref=out_ref.at[me], send_sem=send_sems.at[k - 1],
                                                      recv_sem=recv_sems.at[k - 1], device_id=(px, py, pc), device_id_type=_MESH))
            recvs.append(pltpu.make_async_remote_copy(src_ref=x_ref, dst_ref=out_ref.at[4 * px + 2 * py + pc],
                                                      send_sem=send_sems.at[k - 1], recv_sem=recv_sems.at[k - 1],
                                                      device_id=(px, py, pc), device_id_type=_MESH))
        for cp in sends:
            cp.start()
        for cp in recvs:
            cp.wait_recv()
        for cp in sends:
            cp.wait_send()
        mine.wait()

    vm = pl.BlockSpec(memory_space=pltpu.VMEM)
    return pl.pallas_call(
        body, name=name, out_shape=jax.ShapeDtypeStruct((8,) + x.shape, x.dtype), in_specs=[vm], out_specs=vm,
        scratch_shapes=[pltpu.SemaphoreType.DMA((7,)), pltpu.SemaphoreType.DMA((7,)), pltpu.SemaphoreType.DMA],
        compiler_params=pltpu.CompilerParams(vmem_limit_bytes=VMEM_LIMIT),
    )(x)


class _ChipExchange:
    def __init__(self, arrs, scatter):
        self.arrs, self.scatter, self.n = list(arrs), scatter, len(arrs)
        self.out_shape = [jax.ShapeDtypeStruct(a.shape if scatter else (NCHIP,) + a.shape, a.dtype) for a in self.arrs]
        links = self.n * (NCHIP - 1)
        self.scratch = [pltpu.SemaphoreType.DMA((links,)), pltpu.SemaphoreType.DMA((links,)), pltpu.SemaphoreType.DMA((self.n,))]

    def copies(self, ins, outs, send_sems, recv_sems, local_sems):
        mx, my, mc = lax.axis_index("x"), lax.axis_index("y"), lax.axis_index("c")
        me = 2 * mx + my
        local, sends, recvs = [], [], []
        for j in range(self.n):
            src_own = ins[j].at[me] if self.scatter else ins[j]
            local.append(pltpu.make_async_copy(src_own, outs[j].at[me], local_sems.at[j]))
            for k in range(1, NCHIP):
                px, py = _flip(mx, k & 2), _flip(my, k & 1)
                peer = 2 * px + py
                sem = j * (NCHIP - 1) + k - 1
                src = ins[j].at[peer] if self.scatter else ins[j]
                sends.append(pltpu.make_async_remote_copy(src_ref=src, dst_ref=outs[j].at[me], send_sem=send_sems.at[sem],
                                                          recv_sem=recv_sems.at[sem], device_id=(px, py, mc), device_id_type=_MESH))
                recvs.append(pltpu.make_async_remote_copy(src_ref=src, dst_ref=outs[j].at[peer], send_sem=send_sems.at[sem],
                                                          recv_sem=recv_sems.at[sem], device_id=(px, py, mc), device_id_type=_MESH))
        return local, sends, recvs

    @staticmethod
    def start(local, sends, recvs):
        for cp in local + sends:
            cp.start()

    @staticmethod
    def finish(local, sends, recvs):
        for cp in recvs:
            cp.wait_recv()
        for cp in sends:
            cp.wait_send()
        for cp in local:
            cp.wait()


def _gather_split(arrs, *, name):
    n = len(arrs)
    links = n * (NCHIP - 1)

    def body(*refs):
        ins, outs = refs[:n], refs[n:2 * n]
        ici_send, ici_recv, d2d_send, d2d_recv, local_sems = refs[2 * n:]
        mx, my, mc = lax.axis_index("x"), lax.axis_index("y"), lax.axis_index("c")
        me = 2 * mx + my
        local, first, arrive, onward, handed = [], [], [], [], []
        for j in range(n):
            hr = ins[j].shape[0] // 2
            mine, other = pl.ds(mc * hr, hr), pl.ds((1 - mc) * hr, hr)
            local.append(pltpu.make_async_copy(ins[j], outs[j].at[me], local_sems.at[j]))
            for k in range(1, NCHIP):
                px, py = _flip(mx, k & 2), _flip(my, k & 1)
                peer = 2 * px + py
                sem = j * (NCHIP - 1) + k - 1
                ici = lambda slot: pltpu.make_async_remote_copy(
                    src_ref=ins[j].at[mine], dst_ref=outs[j].at[slot, mine], send_sem=ici_send.at[sem], recv_sem=ici_recv.at[sem],
                    device_id=(px, py, mc), device_id_type=_MESH)
                d2d = lambda rows: pltpu.make_async_remote_copy(
                    src_ref=outs[j].at[peer, rows], dst_ref=outs[j].at[peer, rows], send_sem=d2d_send.at[sem], recv_sem=d2d_recv.at[sem],
                    device_id=(mx, my, 1 - mc), device_id_type=_MESH)
                first.append(ici(me))
                arrive.append(ici(peer))
                onward.append(d2d(mine))
                handed.append(d2d(other))
        for cp in local + first:
            cp.start()
        for got, fwd in zip(arrive, onward):
            got.wait_recv()
            fwd.start()
        for cp in handed:
            cp.wait_recv()
        for cp in first + onward:
            cp.wait_send()
        for cp in local:
            cp.wait()

    sems = pltpu.SemaphoreType.DMA((links,))
    return pl.pallas_call(
        body, name=name, out_shape=[jax.ShapeDtypeStruct((NCHIP,) + a.shape, a.dtype) for a in arrs],
        in_specs=[_ANY] * n, out_specs=[_ANY] * n, scratch_shapes=[sems, sems, sems, sems, pltpu.SemaphoreType.DMA((n,))],
    )(*arrs)


_HBM = pl.BlockSpec(memory_space=pltpu.HBM)
_SEM = pl.BlockSpec(memory_space=pltpu.SEMAPHORE)
_DATAFLOW = pltpu.SideEffectType.DATAFLOW_SIDE_EFFECTING


def _scatter_copies(ins, lands, send_sems, recv_sems):
    mx, my, mc = lax.axis_index("x"), lax.axis_index("y"), lax.axis_index("c")
    me = 2 * mx + my
    sends, recvs = [], []
    for j in range(len(ins)):
        for k in range(1, NCHIP):
            px, py = _flip(mx, k & 2), _flip(my, k & 1)
            peer = 2 * px + py
            sem = j * (NCHIP - 1) + k - 1
            mk = lambda slot: pltpu.make_async_remote_copy(src_ref=ins[j].at[peer], dst_ref=lands[j].at[slot], send_sem=send_sems.at[sem],
                                                           recv_sem=recv_sems.at[sem], device_id=(px, py, mc), device_id_type=_MESH)
            sends.append(mk(me))
            recvs.append(mk(peer))
    return sends, recvs


def _scatter_start(arrs, after, *, name):
    n = len(arrs)
    links = n * (NCHIP - 1)
    n_in = 2 * n + len(after)

    def body(*refs):
        ins, lands = refs[:n], refs[n:2 * n]
        send_sems, recv_sems = refs[n_in], refs[n_in + 1]
        token = refs[-1]
        for cp in _scatter_copies(ins, lands, send_sems, recv_sems)[0]:
            cp.start()
        token[...] = jnp.zeros_like(token)

    hbm = lambda a: pltpu.HBM(a.shape, a.dtype)
    res = pl.pallas_call(
        body, name=name,
        out_shape=(pltpu.SemaphoreType.DMA((links,)), pltpu.SemaphoreType.DMA((links,)), *[hbm(a) for a in arrs], *[hbm(a) for a in arrs],
                   jax.ShapeDtypeStruct((8, 128), F32)),
        in_specs=[_HBM] * (2 * n) + [_ANY] * len(after), out_specs=(_SEM, _SEM, *[_HBM] * (2 * n), pl.BlockSpec(memory_space=pltpu.VMEM)),
        input_output_aliases={j: 2 + j for j in range(2 * n)},
        compiler_params=pltpu.CompilerParams(has_side_effects=_DATAFLOW),
    )(*[pltpu.with_memory_space_constraint(a, pltpu.HBM) for a in arrs],
      *[pltpu.with_memory_space_constraint(lax.empty(a.shape, a.dtype), pltpu.HBM) for a in arrs], *after)
    return res[0], res[1], list(res[2:2 + n]), list(res[2 + n:2 + 2 * n]), res[-1]


def _scatter_wait(send_sems, recv_sems, arrs, lands, after, *, name):
    n = len(arrs)

    def body(*refs):
        ins, lands_in = refs[:n], refs[n:2 * n]
        sends, recvs = _scatter_copies(ins, lands_in, refs[2 * n], refs[2 * n + 1])
        for cp in sends:
            cp.wait_send()
        for cp in recvs:
            cp.wait_recv()

    hbm = lambda a: pltpu.HBM(a.shape, a.dtype)
    res = pl.pallas_call(
        body, name=name, out_shape=(*[hbm(a) for a in arrs], *[hbm(a) for a in lands]),
        in_specs=[_HBM] * (2 * n) + [_SEM, _SEM] + [_ANY] * len(after), out_specs=[_HBM] * (2 * n),
        input_output_aliases={j: j for j in range(2 * n)},
        compiler_params=pltpu.CompilerParams(has_side_effects=_DATAFLOW),
    )(*arrs, *lands, send_sems, recv_sems, *after)
    return list(res[:n]), list(res[n:])


def _pallas(body, operands, *, name, grid, in_specs, out_specs, out_shape, sem, scratch_shapes=(), aliases=None, exchange=None):
    if exchange is None:
        return pl.pallas_call(body, name=name, grid=grid, in_specs=in_specs, out_specs=out_specs, out_shape=out_shape,
                              scratch_shapes=list(scratch_shapes), input_output_aliases=aliases or {},
                              compiler_params=_cparams(*sem))(*operands)
    ex = exchange
    (steps,) = grid
    n_in, n_out, n_scr, k = len(in_specs), len(out_specs), len(scratch_shapes), ex.n

    def hosted(*refs):
        ins, refs = refs[:n_in], refs[n_in:]
        ex_in, refs = refs[:k], refs[k:]
        outs, refs = refs[:n_out], refs[n_out:]
        ex_out, refs = refs[:k], refs[k:]
        scr, ex_sems = refs[:n_scr], refs[n_scr:]
        cps = ex.copies(ex_in, ex_out, *ex_sems)
        pl.when(pl.program_id(0) == 0)(lambda: ex.start(*cps))
        body(*ins, *outs, *scr)
        pl.when(pl.program_id(0) == steps - 1)(lambda: ex.finish(*cps))

    return pl.pallas_call(
        hosted, name=name, grid=grid, in_specs=list(in_specs) + [_ANY] * k, out_specs=list(out_specs) + [_ANY] * k,
        out_shape=list(out_shape) + ex.out_shape, scratch_shapes=list(scratch_shapes) + ex.scratch,
        input_output_aliases=aliases or {}, compiler_params=_cparams("arbitrary"),
    )(*operands, *ex.arrs)


def _core_swap(arrs, *, name):
    n = len(arrs)

    def body(*refs):
        ins, outs = refs[:n], refs[n:2 * n]
        send_sems, recv_sems = refs[2 * n:]
        sib = (lax.axis_index("x"), lax.axis_index("y"), 1 - lax.axis_index("c"))
        cps = [pltpu.make_async_remote_copy(src_ref=ins[j], dst_ref=outs[j], send_sem=send_sems.at[j], recv_sem=recv_sems.at[j],
                                            device_id=sib, device_id_type=_MESH) for j in range(n)]
        for cp in cps:
            cp.start()
        for cp in cps:
            cp.wait_recv()
        for cp in cps:
            cp.wait_send()

    return pl.pallas_call(
        body, name=name, out_shape=[jax.ShapeDtypeStruct(a.shape, a.dtype) for a in arrs],
        in_specs=[_ANY] * n, out_specs=[_ANY] * n,
        scratch_shapes=[pltpu.SemaphoreType.DMA((n,)), pltpu.SemaphoreType.DMA((n,))],
    )(*arrs)


def _row_tile(rows, cols, budget=1 << 20):
    best = None
    for t in range(8, rows + 1, 8):
        if rows % t == 0 and t * cols <= budget:
            best = t
    return best or rows


def _sum_slots(x, *, name, after=()):
    ns, r, c = x.shape
    tr = _row_tile(r, c * ns)

    def body(x_ref, *refs):
        acc = x_ref[0].astype(F32)
        for s in range(1, ns):
            acc = acc + x_ref[s].astype(F32)
        refs[-1][...] = acc

    return pl.pallas_call(
        body, name=name, grid=(r // tr,), out_shape=jax.ShapeDtypeStruct((r, c), F32),
        in_specs=[pl.BlockSpec((ns, tr, c), lambda i: (0, i, 0))] + [_ANY] * len(after), out_specs=pl.BlockSpec((tr, c), lambda i: (i, 0)),
        compiler_params=_cparams("parallel"),
    )(x, *after)


def _adamw(w, ga, gb, m, v, *, name):
    r, c = w.shape
    tr = _row_tile(r, c, budget=1 << 18)
    two = gb is not None

    def body(*refs):
        w_ref, ga_ref = refs[0], refs[1]
        m_ref, v_ref = refs[2 + two], refs[3 + two]
        g_ref, d_ref, mo_ref, vo_ref = refs[4 + two:]
        g = ga_ref[...] + refs[2][...] if two else ga_ref[...]
        mn = ADAM_B1 * m_ref[...] + (1.0 - ADAM_B1) * g
        vn = ADAM_B2 * v_ref[...] + (1.0 - ADAM_B2) * (g * g)
        m_hat = mn / (1.0 - ADAM_B1 ** ADAM_STEP)
        v_hat = vn / (1.0 - ADAM_B2 ** ADAM_STEP)
        g_ref[...] = g
        d_ref[...] = -ADAM_LR * (m_hat / (jnp.sqrt(v_hat) + ADAM_EPS) + ADAM_WD * w_ref[...])
        mo_ref[...] = mn
        vo_ref[...] = vn

    spec = pl.BlockSpec((tr, c), lambda i: (i, 0))
    ins = [w, ga] + ([gb] if two else []) + [m, v]
    return pl.pallas_call(
        body, name=name, grid=(r // tr,), out_shape=[jax.ShapeDtypeStruct((r, c), F32)] * 4,
        in_specs=[spec] * len(ins), out_specs=[spec] * 4, compiler_params=_cparams("parallel"),
    )(*ins)


_MIX_AB0, _MIX_AB1 = NQKV, NQKV + 4 * NH


def _regroup_mix(w):
    pad = jnp.zeros((w.shape[0], NMIXP - NMIX), w.dtype)
    return jnp.concatenate([w[:, :_MIX_AB0], w[:, _MIX_AB1:], w[:, _MIX_AB0:_MIX_AB1], pad], axis=1)


def _ungroup_mix(w):
    n_ab = _MIX_AB1 - _MIX_AB0
    return jnp.concatenate([w[:, :_MIX_AB0], w[:, NMIX - n_ab:NMIX], w[:, _MIX_AB0:NMIX - n_ab]], axis=1)


def _chip_major_cols(w):
    r, c = w.shape
    return w.reshape(r, NCHIP, c // NCHIP).transpose(1, 0, 2)


def _from_chip_major_cols(w):
    return w.transpose(1, 0, 2).reshape(w.shape[1], -1)


_SMALL = (("c_ctx", D), ("b_ada", 9 * D), ("norm1_w", D), ("norm2_w", D), ("norm3_w", D), ("final_norm_w", D),
          ("a_log", 2 * NH), ("dt_bias", 2 * NH), ("gdn_norm_w", HD), ("pool_w", 4 * 128 * 128), ("pool_scale", NPOOL),
          ("conv_w", 5 * NQKV // NCHIP))


def _pack(vals, lanes=128, row_mult=8):
    flat = jnp.concatenate([jnp.ravel(v) for v in vals])
    n = flat.shape[0]
    rows = -(-n // (lanes * row_mult)) * row_mult
    return jnp.pad(flat, (0, rows * lanes - n)).reshape(rows, lanes)


def _unpack(packed, sizes):
    flat = packed.reshape(-1)
    out, o = [], 0
    for n in sizes:
        out.append(flat[o:o + n])
        o += n
    return out


def kernel(x, c, ctx, c_ctx, w_ada, b_ada, norm1_w, ffn1_w_in, ffn1_w_out, norm2_w, w_mix_in, conv_w, a_log, dt_bias, gdn_norm_w, w_gdn_proj, pool_w, pool_scale, w_pool_proj, w_mix_out, norm3_w, ffn2_w_in, ffn2_w_out, final_norm_w, loss_target, m_c_ctx, m_w_ada, m_b_ada, m_norm1_w, m_ffn1_w_in, m_ffn1_w_out, m_norm2_w, m_w_mix_in, m_conv_w, m_a_log, m_dt_bias, m_gdn_norm_w, m_w_gdn_proj, m_pool_w, m_pool_scale, m_w_pool_proj, m_w_mix_out, m_norm3_w, m_ffn2_w_in, m_ffn2_w_out, m_final_norm_w, v_c_ctx, v_w_ada, v_b_ada, v_norm1_w, v_ffn1_w_in, v_ffn1_w_out, v_norm2_w, v_w_mix_in, v_conv_w, v_a_log, v_dt_bias, v_gdn_norm_w, v_w_gdn_proj, v_pool_w, v_pool_scale, v_w_pool_proj, v_w_mix_out, v_norm3_w, v_ffn2_w_in, v_ffn2_w_out, v_final_norm_w):
    names = ("c_ctx", "w_ada", "b_ada", "norm1_w", "ffn1_w_in", "ffn1_w_out", "norm2_w", "w_mix_in", "conv_w", "a_log", "dt_bias",
             "gdn_norm_w", "w_gdn_proj", "pool_w", "pool_scale", "w_pool_proj", "w_mix_out", "norm3_w", "ffn2_w_in", "ffn2_w_out",
             "final_norm_w")
    w = dict(zip(names, (c_ctx, w_ada, b_ada, norm1_w, ffn1_w_in, ffn1_w_out, norm2_w, w_mix_in, conv_w, a_log, dt_bias, gdn_norm_w,
                         w_gdn_proj, pool_w, pool_scale, w_pool_proj, w_mix_out, norm3_w, ffn2_w_in, ffn2_w_out, final_norm_w)))
    mom = dict(zip(names, (m_c_ctx, m_w_ada, m_b_ada, m_norm1_w, m_ffn1_w_in, m_ffn1_w_out, m_norm2_w, m_w_mix_in, m_conv_w, m_a_log,
                           m_dt_bias, m_gdn_norm_w, m_w_gdn_proj, m_pool_w, m_pool_scale, m_w_pool_proj, m_w_mix_out, m_norm3_w,
                           m_ffn2_w_in, m_ffn2_w_out, m_final_norm_w)))
    var = dict(zip(names, (v_c_ctx, v_w_ada, v_b_ada, v_norm1_w, v_ffn1_w_in, v_ffn1_w_out, v_norm2_w, v_w_mix_in, v_conv_w, v_a_log,
                           v_dt_bias, v_gdn_norm_w, v_w_gdn_proj, v_pool_w, v_pool_scale, v_w_pool_proj, v_w_mix_out, v_norm3_w,
                           v_ffn2_w_in, v_ffn2_w_out, v_final_norm_w)))
    mx, my, mc = lax.axis_index("x"), lax.axis_index("y"), lax.axis_index("c")
    chip = 2 * mx + my
    dev = 2 * chip + mc
    ada_cols = w_ada.shape[2]

    c_rows = _all_gather8(jnp.pad(c, ((0, 7), (0, 0))), name="gather_c")[:, 0, :]
    c_all = jnp.concatenate([c_rows, c_ctx[None], jnp.zeros((7, D), F32)], axis=0)
    b_sh = lax.dynamic_slice(b_ada, (0, chip * ada_cols), (1, ada_cols))
    mod_sh = _ada_fwd(c_all, w_ada[0], b_sh, name="ada_fwd")
    mod_parts = _all_gather8(mod_sh, name="gather_mod")
    mod_all = jnp.concatenate([mod_parts[2 * s] for s in range(NCHIP)], axis=1)
    mod_lat = lax.dynamic_index_in_dim(mod_all, dev, axis=0, keepdims=False).reshape(9, D)
    modv = jnp.zeros((2, 16, D), F32).at[0, :9].set(mod_all[8].reshape(9, D)).at[1, :9].set(mod_lat)

    big = ("ffn1_w_in", "ffn1_w_out", "w_mix_in", "w_gdn_proj", "w_pool_proj", "w_mix_out", "ffn2_w_in", "ffn2_w_out")
    shard = {k: w[k][0].astype(BF) for k in big}
    w1_in, w1_out = _gather_split([shard["ffn1_w_in"], shard["ffn1_w_out"]], name="gather_ffn1")
    p = dict(
        norm1=norm1_w, norm2=norm2_w, norm3=norm3_w, fnorm=final_norm_w[None], w1_in=w1_in, w1_out=w1_out.reshape(FF, D),
        cst=jnp.zeros((8, 128), F32).at[0, 2 * NH:4 * NH].set(jnp.exp(a_log).reshape(-1)).at[1, 2 * NH:4 * NH].set(dt_bias.reshape(-1)),
        gnw=gdn_norm_w, pool_w=pool_w[0], pscale=pool_scale)
    late = ([shard["w_mix_in"], conv_w[0]], [shard["ffn2_w_in"], shard["ffn2_w_out"]],
            [shard["w_gdn_proj"], shard["w_pool_proj"], shard["w_mix_out"]])

    loss_dev, dx_lat, g, landed, small, dmod = _local_step(ctx[0], x[0], loss_target[0], modv, p, late)
    loss = lax.psum(loss_dev, ("x", "y", "c"))
    grad_x = dx_lat[None]

    small_vals = [dmod[1], dmod[0], small["norm1"], small["norm2"], small["norm3"], small["fnorm"], small["a_log"], small["dt_bias"],
                  small["gnw"], small["pool_w"], small["pscale"], small["conv"]]
    small_sizes = [v.size for v in small_vals]
    packed = _all_gather8(_pack(small_vals), name="gather_small")
    tot = _unpack(_sum_slots(packed, name="sum_small"), small_sizes)
    dmod_lat_all = packed[:, :9 * D // 128, :].reshape(8, 9 * D)
    dm = jnp.concatenate([dmod_lat_all, tot[1][None], jnp.zeros((7, 9 * D), F32)], axis=0)
    dm_sh = lax.dynamic_slice(dm, (0, chip * ada_cols), (16, ada_cols))
    g_w_ada, cctx_part = _ada_bwd(c_all, dm_sh, w_ada[0], name="ada_bwd")
    g_c_ctx = _cctx_grad(_all_gather8(cctx_part, name="gather_cctx"), c_ctx[None], name="cctx_grad")[0]
    conv_tot = tot[11].reshape(5, NQKV)
    g_small = dict(c_ctx=g_c_ctx, b_ada=tot[0] + tot[1], norm1_w=tot[2], norm2_w=tot[3], norm3_w=tot[4], final_norm_w=tot[5],
                   a_log=tot[6], dt_bias=tot[7], gdn_norm_w=tot[8], pool_w=tot[9], pool_scale=tot[10],
                   conv_w=lax.dynamic_slice(conv_tot, (0, chip * (NQKV // NCHIP)), (5, NQKV // NCHIP)))

    first = ("ffn1_w_in", "ffn1_w_out")
    order = dict(zip(big, ("w1_in", "w1_out", "w_mix", "w_gdn", "w_pool", "w_mo", "w2_in", "w2_out")))
    rest = [k for k in big if k not in first]
    send_sems, recv_sems, sent, lands, token = _scatter_start([g["w1_in"], g["w1_out"]], [g_c_ctx, g_w_ada], name="scatter_ffn1_start")
    mine = {k: _sum_slots(landed[order[k]], name=f"sum_{k}", after=[token]) for k in rest}
    theirs = dict(zip(rest, _core_swap([mine[k] for k in rest], name="swap_grad_sums")))

    out = {}
    as2d = lambda a: a.reshape(-1, a.shape[-1])

    def update(k):
        res = _adamw(as2d(w[k]), as2d(mine[k]), as2d(theirs[k]), as2d(mom[k]), as2d(var[k]), name=f"adamw_{k}")
        out[k] = [r.reshape(w[k].shape) for r in res]

    for k in rest:
        update(k)
    out["w_ada"] = [r.reshape(w_ada.shape) for r in _adamw(w_ada[0], g_w_ada, None, m_w_ada[0], v_w_ada[0], name="adamw_w_ada")]
    sm_names = [n for n, _ in _SMALL]
    sm_sizes = [n for _, n in _SMALL]
    res = _adamw(_pack([w[k] for k in sm_names]), _pack([g_small[k] for k in sm_names]), None,
                 _pack([mom[k] for k in sm_names]), _pack([var[k] for k in sm_names]), name="adamw_small")
    done = [out[k][1] for k in rest] + [out["w_ada"][1], res[1]]
    res = [_unpack(r, sm_sizes) for r in res]
    for i, k in enumerate(sm_names):
        out[k] = [r[i].reshape(w[k].shape) for r in res]
    sent, lands = _scatter_wait(send_sems, recv_sems, sent, lands, done, name="scatter_ffn1_wait")
    for k, part, land in zip(first, sent, lands):
        own = lax.dynamic_slice_in_dim(part, chip, 1, axis=0)
        mine[k] = _sum_slots(lax.dynamic_update_slice_in_dim(land, own, chip, axis=0), name=f"sum_{k}")
    theirs.update(zip(first, _core_swap([mine[k] for k in first], name="swap_ffn1_sums")))
    for k in first:
        update(k)
    return (loss, grad_x, *[out[k][0] for k in names], *[out[k][1] for k in names], *[out[k][2] for k in names],
            *[out[k][3] for k in names])
```
